```python
import jax, jax.numpy as jnp
from jax import lax
import numpy as np

D_MODEL = 2048
BATCH = 8
SEQ = 8192
DEPTH = 2

GRID_W = 64
Q_BLOCK = 128
NORM_EPS = 1e-6
ROPE_THETA = 500000.0
AXIAL_THETA = 10000.0
MLA_HEADS = 8
MLA_Q_LORA = 512
MLA_KV_LORA = 256
MLA_NOPE_DIM = 128
MLA_ROPE_DIM = 64
MLA_QK_DIM = MLA_NOPE_DIM + MLA_ROPE_DIM
MLA_V_DIM = 128
GQA_HEADS = 8
GQA_KV_HEADS = 2
GQA_HEAD_DIM = 128
SWA_HEADS = 32
SWA_KV_HEADS = 4
SWA_HEAD_DIM = 64
SWA_WINDOW = 128
SWA_ROT_DIM = SWA_HEAD_DIM // 4
D_FF = 4 * D_MODEL
EVEN_IN = MLA_Q_LORA + MLA_KV_LORA + MLA_ROPE_DIM + (GQA_HEADS + 2 * GQA_KV_HEADS) * GQA_HEAD_DIM
EVEN_OUT = MLA_HEADS * MLA_V_DIM + GQA_HEADS * GQA_HEAD_DIM
ODD_IN = (SWA_HEADS + 2 * SWA_KV_HEADS) * SWA_HEAD_DIM
ODD_OUT = SWA_HEADS * SWA_HEAD_DIM
N_EVEN = (DEPTH + 1) // 2
N_ODD = DEPTH // 2

kernel_name = 'hybrid_mla_gridgqa_swa_sqrelu_encoder'


def rms_norm(x, gain):
    xf = x.astype(jnp.float32)
    y = xf * lax.rsqrt(jnp.mean(xf * xf, axis=-1, keepdims=True) + NORM_EPS)
    return (y * gain.astype(jnp.float32)).astype(x.dtype)


def rope_table(pos, dim, theta):
    inv = jnp.float32(theta) ** (-jnp.arange(0, dim, 2, dtype=jnp.float32) / dim)
    ang = pos.astype(jnp.float32)[:, None] * inv[None, :]
    return jnp.cos(ang), jnp.sin(ang)


def apply_rope(x, cos, sin):
    half = x.shape[-1] // 2
    c = cos[None, :, None, :].astype(x.dtype)
    s = sin[None, :, None, :].astype(x.dtype)
    x1 = x[..., :half]
    x2 = x[..., half:]
    return jnp.concatenate([x1 * c - x2 * s, x2 * c + x1 * s], axis=-1)


def dense_block_attention(q, k, v, scale):
    B, S, Hq, dk = q.shape
    Hkv = k.shape[2]
    G = Hq // Hkv
    dv = v.shape[-1]
    nb = S // Q_BLOCK
    qb = jnp.swapaxes(q.reshape(B, nb, Q_BLOCK, Hkv, G, dk), 0, 1)

    def one_block(qblk):
        s = jnp.einsum('bqhgd,bkhd->bhgqk', qblk, k, preferred_element_type=jnp.float32) * scale
        p = jax.nn.softmax(s, axis=-1).astype(v.dtype)
        return jnp.einsum('bhgqk,bkhd->bqhgd', p, v)

    o = lax.map(one_block, qb)
    return jnp.swapaxes(o, 0, 1).reshape(B, S, Hq, dv)


def banded_window_attention(q, k, v, sink, scale):
    B, S, Hq, d = q.shape
    Hkv = k.shape[2]
    G = Hq // Hkv
    nb = S // Q_BLOCK
    span = Q_BLOCK + 2 * SWA_WINDOW
    pad = ((0, 0), (SWA_WINDOW, SWA_WINDOW), (0, 0), (0, 0))
    kp = jnp.pad(k, pad)
    vp = jnp.pad(v, pad)
    qb = jnp.swapaxes(q.reshape(B, nb, Q_BLOCK, Hkv, G, d), 0, 1)
    sink_b = sink.astype(jnp.float32).reshape(1, Hkv, G, 1, 1)
    offs_q = jnp.arange(Q_BLOCK)
    offs_k = jnp.arange(span) - SWA_WINDOW

    def one_block(args):
        i, qblk = args
        start = i * Q_BLOCK
        kblk = lax.dynamic_slice_in_dim(kp, start, span, axis=1)
        vblk = lax.dynamic_slice_in_dim(vp, start, span, axis=1)
        q_pos = start + offs_q
        k_pos = start + offs_k
        valid = (jnp.abs(q_pos[:, None] - k_pos[None, :]) <= SWA_WINDOW) & ((k_pos >= 0) & (k_pos < S))[None, :]
        s = jnp.einsum('bqhgd,bkhd->bhgqk', qblk, kblk, preferred_element_type=jnp.float32) * scale
        s = jnp.where(valid, s, -jnp.inf)
        m = jnp.maximum(jnp.max(s, axis=-1, keepdims=True), sink_b)
        p = jnp.exp(s - m)
        denom = jnp.sum(p, axis=-1, keepdims=True) + jnp.exp(sink_b - m)
        p = (p / denom).astype(v.dtype)
        return jnp.einsum('bhgqk,bkhd->bqhgd', p, vblk)

    o = lax.map(one_block, (jnp.arange(nb), qb))
    return jnp.swapaxes(o, 0, 1).reshape(B, S, Hq, d)


def even_mixer(h, w_in, q_lat_norm, kv_lat_norm, w_uq, w_ukv, q_norm, k_nope_norm, k_rope_norm,
               g_q_norm, g_k_norm, w_out, mla_cos, mla_sin, row_cos, row_sin, col_cos, col_sin):
    B, S, _ = h.shape
    proj = h @ w_in
    o1 = MLA_Q_LORA
    o2 = o1 + MLA_KV_LORA
    o3 = o2 + MLA_ROPE_DIM
    o4 = o3 + GQA_HEADS * GQA_HEAD_DIM
    o5 = o4 + GQA_KV_HEADS * GQA_HEAD_DIM
    c_q = proj[..., :o1]
    c_kv = proj[..., o1:o2]
    k_rope = proj[..., o2:o3]
    q_g = proj[..., o3:o4].reshape(B, S, GQA_HEADS, GQA_HEAD_DIM)
    k_g = proj[..., o4:o5].reshape(B, S, GQA_KV_HEADS, GQA_HEAD_DIM)
    v_g = proj[..., o5:].reshape(B, S, GQA_KV_HEADS, GQA_HEAD_DIM)

    q_a = (rms_norm(c_q, q_lat_norm) @ w_uq).reshape(B, S, MLA_HEADS, MLA_QK_DIM)
    q_a = rms_norm(q_a, q_norm)
    q_a = jnp.concatenate([q_a[..., :MLA_NOPE_DIM], apply_rope(q_a[..., MLA_NOPE_DIM:], mla_cos, mla_sin)], axis=-1)
    kv = (rms_norm(c_kv, kv_lat_norm) @ w_ukv).reshape(B, S, MLA_HEADS, MLA_NOPE_DIM + MLA_V_DIM)
    k_nope = rms_norm(kv[..., :MLA_NOPE_DIM], k_nope_norm)
    v_a = kv[..., MLA_NOPE_DIM:]
    k_r = apply_rope(rms_norm(k_rope, k_rope_norm)[:, :, None, :], mla_cos, mla_sin)
    k_a = jnp.concatenate([k_nope, jnp.broadcast_to(k_r, (B, S, MLA_HEADS, MLA_ROPE_DIM))], axis=-1)
    o_a = dense_block_attention(q_a, k_a, v_a, MLA_QK_DIM ** -0.5)

    half = GQA_HEAD_DIM // 2
    def axial(t):
        return jnp.concatenate([apply_rope(t[..., :half], row_cos, row_sin), apply_rope(t[..., half:], col_cos, col_sin)], axis=-1)
    q_g = axial(rms_norm(q_g, g_q_norm))
    k_g = axial(rms_norm(k_g, g_k_norm))
    o_g = dense_block_attention(q_g, k_g, v_g, GQA_HEAD_DIM ** -0.5)

    merged = jnp.concatenate([o_a.reshape(B, S, -1), o_g.reshape(B, S, -1)], axis=-1)
    return merged @ w_out


def odd_mixer(h, w_qkv, q_norm, k_norm, sink, w_out, swa_cos, swa_sin):
    B, S, _ = h.shape
    qkv = h @ w_qkv
    nq = SWA_HEADS * SWA_HEAD_DIM
    nk = SWA_KV_HEADS * SWA_HEAD_DIM
    q = rms_norm(qkv[..., :nq].reshape(B, S, SWA_HEADS, SWA_HEAD_DIM), q_norm)
    k = rms_norm(qkv[..., nq:nq + nk].reshape(B, S, SWA_KV_HEADS, SWA_HEAD_DIM), k_norm)
    v = qkv[..., nq + nk:].reshape(B, S, SWA_KV_HEADS, SWA_HEAD_DIM)
    q = jnp.concatenate([apply_rope(q[..., :SWA_ROT_DIM], swa_cos, swa_sin), q[..., SWA_ROT_DIM:]], axis=-1)
    k = jnp.concatenate([apply_rope(k[..., :SWA_ROT_DIM], swa_cos, swa_sin), k[..., SWA_ROT_DIM:]], axis=-1)
    o = banded_window_attention(q, k, v, sink, SWA_HEAD_DIM ** -0.5)
    return o.reshape(B, S, -1) @ w_out


def squared_relu_mlp(x, gain, w_up, w_down):
    h = rms_norm(x, gain) @ w_up
    return jnp.square(jax.nn.relu(h)) @ w_down


def _dense(k, shape):
    return jax.random.normal(k, shape, jnp.float32) * (shape[-2] ** -0.5)


def _gain(k, shape):
    return 1.0 + 0.02 * jax.random.normal(k, shape, jnp.float32)


def _fwd_setup_inputs(seed: int = 0) -> dict:
    key = jax.random.key(seed)
    ks = jax.random.split(key, 22)
    return {
        'x': jax.random.normal(ks[0], (BATCH, SEQ, D_MODEL), jnp.float32),
        'even_norm': _gain(ks[1], (N_EVEN, D_MODEL)),
        'even_w_in': _dense(ks[2], (N_EVEN, D_MODEL, EVEN_IN)),
        'mla_q_lat_norm': _gain(ks[3], (N_EVEN, MLA_Q_LORA)),
        'mla_kv_lat_norm': _gain(ks[4], (N_EVEN, MLA_KV_LORA)),
        'mla_w_uq': _dense(ks[5], (N_EVEN, MLA_Q_LORA, MLA_HEADS * MLA_QK_DIM)),
        'mla_w_ukv': _dense(ks[6], (N_EVEN, MLA_KV_LORA, MLA_HEADS * (MLA_NOPE_DIM + MLA_V_DIM))),
        'mla_q_norm': _gain(ks[7], (N_EVEN, MLA_QK_DIM)),
        'mla_k_nope_norm': _gain(ks[8], (N_EVEN, MLA_NOPE_DIM)),
        'mla_k_rope_norm': _gain(ks[9], (N_EVEN, MLA_ROPE_DIM)),
        'gqa_q_norm': _gain(ks[10], (N_EVEN, GQA_HEAD_DIM)),
        'gqa_k_norm': _gain(ks[11], (N_EVEN, GQA_HEAD_DIM)),
        'even_w_out': _dense(ks[12], (N_EVEN, EVEN_OUT, D_MODEL)),
        'odd_norm': _gain(ks[13], (N_ODD, D_MODEL)),
        'odd_w_qkv': _dense(ks[14], (N_ODD, D_MODEL, ODD_IN)),
        'swa_q_norm': _gain(ks[15], (N_ODD, SWA_HEAD_DIM)),
        'swa_k_norm': _gain(ks[16], (N_ODD, SWA_HEAD_DIM)),
        'swa_sink': jax.random.normal(ks[17], (N_ODD, SWA_HEADS), jnp.float32),
        'odd_w_out': _dense(ks[18], (N_ODD, ODD_OUT, D_MODEL)),
        'mlp_norm': _gain(ks[19], (DEPTH, D_MODEL)),
        'mlp_w_up': _dense(ks[20], (DEPTH, D_MODEL, D_FF)),
        'mlp_w_down': _dense(ks[21], (DEPTH, D_FF, D_MODEL)),
    }


def _fwd_reference(x, even_norm, even_w_in, mla_q_lat_norm, mla_kv_lat_norm, mla_w_uq, mla_w_ukv,
              mla_q_norm, mla_k_nope_norm, mla_k_rope_norm, gqa_q_norm, gqa_k_norm, even_w_out,
              odd_norm, odd_w_qkv, swa_q_norm, swa_k_norm, swa_sink, odd_w_out,
              mlp_norm, mlp_w_up, mlp_w_down):
    B, S, _ = x.shape
    rows = S // GRID_W
    pos = jnp.arange(S)
    row_pos = jnp.repeat(jnp.arange(rows), GRID_W)
    col_pos = jnp.tile(jnp.arange(GRID_W), rows)
    mla_cos, mla_sin = rope_table(pos, MLA_ROPE_DIM, ROPE_THETA)
    row_cos, row_sin = rope_table(row_pos, GQA_HEAD_DIM // 2, AXIAL_THETA)
    col_cos, col_sin = rope_table(col_pos, GQA_HEAD_DIM // 2, AXIAL_THETA)
    swa_cos, swa_sin = rope_table(pos, SWA_ROT_DIM, ROPE_THETA)
    for layer in range(DEPTH):
        i = layer // 2
        if layer % 2 == 0:
            x = x + even_mixer(rms_norm(x, even_norm[i]), even_w_in[i], mla_q_lat_norm[i], mla_kv_lat_norm[i],
                               mla_w_uq[i], mla_w_ukv[i], mla_q_norm[i], mla_k_nope_norm[i], mla_k_rope_norm[i],
                               gqa_q_norm[i], gqa_k_norm[i], even_w_out[i],
                               mla_cos, mla_sin, row_cos, row_sin, col_cos, col_sin)
        else:
            x = x + odd_mixer(rms_norm(x, odd_norm[i]), odd_w_qkv[i], swa_q_norm[i], swa_k_norm[i],
                              swa_sink[i], odd_w_out[i], swa_cos, swa_sin)
        x = x + squared_relu_mlp(x, mlp_norm[layer], mlp_w_up[layer], mlp_w_down[layer])
    return x


import jax as _jax
import jax.numpy as _jnp

TWIN_FORMAT = 'train_step'
FWD_PARAMS = ['x', 'even_norm', 'even_w_in', 'mla_q_lat_norm', 'mla_kv_lat_norm', 'mla_w_uq', 'mla_w_ukv', 'mla_q_norm', 'mla_k_nope_norm', 'mla_k_rope_norm', 'gqa_q_norm', 'gqa_k_norm', 'even_w_out', 'odd_norm', 'odd_w_qkv', 'swa_q_norm', 'swa_k_norm', 'swa_sink', 'odd_w_out', 'mlp_norm', 'mlp_w_up', 'mlp_w_down']
TWIN_WEIGHTS = ['even_norm', 'even_w_in', 'mla_q_lat_norm', 'mla_kv_lat_norm', 'mla_w_uq', 'mla_w_ukv', 'mla_q_norm', 'mla_k_nope_norm', 'mla_k_rope_norm', 'gqa_q_norm', 'gqa_k_norm', 'even_w_out', 'odd_norm', 'odd_w_qkv', 'swa_q_norm', 'swa_k_norm', 'swa_sink', 'odd_w_out', 'mlp_norm', 'mlp_w_up', 'mlp_w_down']
TWIN_DIFF_INPUT = 'x'
TWIN_INPUTS = ['x', 'even_norm', 'even_w_in', 'mla_q_lat_norm', 'mla_kv_lat_norm', 'mla_w_uq', 'mla_w_ukv', 'mla_q_norm', 'mla_k_nope_norm', 'mla_k_rope_norm', 'gqa_q_norm', 'gqa_k_norm', 'even_w_out', 'odd_norm', 'odd_w_qkv', 'swa_q_norm', 'swa_k_norm', 'swa_sink', 'odd_w_out', 'mlp_norm', 'mlp_w_up', 'mlp_w_down', 'loss_target', 'm_even_norm', 'm_even_w_in', 'm_mla_q_lat_norm', 'm_mla_kv_lat_norm', 'm_mla_w_uq', 'm_mla_w_ukv', 'm_mla_q_norm', 'm_mla_k_nope_norm', 'm_mla_k_rope_norm', 'm_gqa_q_norm', 'm_gqa_k_norm', 'm_even_w_out', 'm_odd_norm', 'm_odd_w_qkv', 'm_swa_q_norm', 'm_swa_k_norm', 'm_swa_sink', 'm_odd_w_out', 'm_mlp_norm', 'm_mlp_w_up', 'm_mlp_w_down', 'v_even_norm', 'v_even_w_in', 'v_mla_q_lat_norm', 'v_mla_kv_lat_norm', 'v_mla_w_uq', 'v_mla_w_ukv', 'v_mla_q_norm', 'v_mla_k_nope_norm', 'v_mla_k_rope_norm', 'v_gqa_q_norm', 'v_gqa_k_norm', 'v_even_w_out', 'v_odd_norm', 'v_odd_w_qkv', 'v_swa_q_norm', 'v_swa_k_norm', 'v_swa_sink', 'v_odd_w_out', 'v_mlp_norm', 'v_mlp_w_up', 'v_mlp_w_down']
TWIN_OUTPUTS = ['loss', 'grad_x', 'grad_even_norm', 'grad_even_w_in', 'grad_mla_q_lat_norm', 'grad_mla_kv_lat_norm', 'grad_mla_w_uq', 'grad_mla_w_ukv', 'grad_mla_q_norm', 'grad_mla_k_nope_norm', 'grad_mla_k_rope_norm', 'grad_gqa_q_norm', 'grad_gqa_k_norm', 'grad_even_w_out', 'grad_odd_norm', 'grad_odd_w_qkv', 'grad_swa_q_norm', 'grad_swa_k_norm', 'grad_swa_sink', 'grad_odd_w_out', 'grad_mlp_norm', 'grad_mlp_w_up', 'grad_mlp_w_down', 'delta_even_norm', 'delta_even_w_in', 'delta_mla_q_lat_norm', 'delta_mla_kv_lat_norm', 'delta_mla_w_uq', 'delta_mla_w_ukv', 'delta_mla_q_norm', 'delta_mla_k_nope_norm', 'delta_mla_k_rope_norm', 'delta_gqa_q_norm', 'delta_gqa_k_norm', 'delta_even_w_out', 'delta_odd_norm', 'delta_odd_w_qkv', 'delta_swa_q_norm', 'delta_swa_k_norm', 'delta_swa_sink', 'delta_odd_w_out', 'delta_mlp_norm', 'delta_mlp_w_up', 'delta_mlp_w_down', 'new_m_even_norm', 'new_m_even_w_in', 'new_m_mla_q_lat_norm', 'new_m_mla_kv_lat_norm', 'new_m_mla_w_uq', 'new_m_mla_w_ukv', 'new_m_mla_q_norm', 'new_m_mla_k_nope_norm', 'new_m_mla_k_rope_norm', 'new_m_gqa_q_norm', 'new_m_gqa_k_norm', 'new_m_even_w_out', 'new_m_odd_norm', 'new_m_odd_w_qkv', 'new_m_swa_q_norm', 'new_m_swa_k_norm', 'new_m_swa_sink', 'new_m_odd_w_out', 'new_m_mlp_norm', 'new_m_mlp_w_up', 'new_m_mlp_w_down', 'new_v_even_norm', 'new_v_even_w_in', 'new_v_mla_q_lat_norm', 'new_v_mla_kv_lat_norm', 'new_v_mla_w_uq', 'new_v_mla_w_ukv', 'new_v_mla_q_norm', 'new_v_mla_k_nope_norm', 'new_v_mla_k_rope_norm', 'new_v_gqa_q_norm', 'new_v_gqa_k_norm', 'new_v_even_w_out', 'new_v_odd_norm', 'new_v_odd_w_qkv', 'new_v_swa_q_norm', 'new_v_swa_k_norm', 'new_v_swa_sink', 'new_v_odd_w_out', 'new_v_mlp_norm', 'new_v_mlp_w_up', 'new_v_mlp_w_down']
TWIN_LEAF_KINDS = {'loss': 'loss', 'grad_x': 'grad_x', 'grad_even_norm': 'grad_w', 'grad_even_w_in': 'grad_w', 'grad_mla_q_lat_norm': 'grad_w', 'grad_mla_kv_lat_norm': 'grad_w', 'grad_mla_w_uq': 'grad_w', 'grad_mla_w_ukv': 'grad_w', 'grad_mla_q_norm': 'grad_w', 'grad_mla_k_nope_norm': 'grad_w', 'grad_mla_k_rope_norm': 'grad_w', 'grad_gqa_q_norm': 'grad_w', 'grad_gqa_k_norm': 'grad_w', 'grad_even_w_out': 'grad_w', 'grad_odd_norm': 'grad_w', 'grad_odd_w_qkv': 'grad_w', 'grad_swa_q_norm': 'grad_w', 'grad_swa_k_norm': 'grad_w', 'grad_swa_sink': 'grad_w', 'grad_odd_w_out': 'grad_w', 'grad_mlp_norm': 'grad_w', 'grad_mlp_w_up': 'grad_w', 'grad_mlp_w_down': 'grad_w', 'delta_even_norm': 'delta_w', 'delta_even_w_in': 'delta_w', 'delta_mla_q_lat_norm': 'delta_w', 'delta_mla_kv_lat_norm': 'delta_w', 'delta_mla_w_uq': 'delta_w', 'delta_mla_w_ukv': 'delta_w', 'delta_mla_q_norm': 'delta_w', 'delta_mla_k_nope_norm': 'delta_w', 'delta_mla_k_rope_norm': 'delta_w', 'delta_gqa_q_norm': 'delta_w', 'delta_gqa_k_norm': 'delta_w', 'delta_even_w_out': 'delta_w', 'delta_odd_norm': 'delta_w', 'delta_odd_w_qkv': 'delta_w', 'delta_swa_q_norm': 'delta_w', 'delta_swa_k_norm': 'delta_w', 'delta_swa_sink': 'delta_w', 'delta_odd_w_out': 'delta_w', 'delta_mlp_norm': 'delta_w', 'delta_mlp_w_up': 'delta_w', 'delta_mlp_w_down': 'delta_w', 'new_m_even_norm': 'new_m', 'new_m_even_w_in': 'new_m', 'new_m_mla_q_lat_norm': 'new_m', 'new_m_mla_kv_lat_norm': 'new_m', 'new_m_mla_w_uq': 'new_m', 'new_m_mla_w_ukv': 'new_m', 'new_m_mla_q_norm': 'new_m', 'new_m_mla_k_nope_norm': 'new_m', 'new_m_mla_k_rope_norm': 'new_m', 'new_m_gqa_q_norm': 'new_m', 'new_m_gqa_k_norm': 'new_m', 'new_m_even_w_out': 'new_m', 'new_m_odd_norm': 'new_m', 'new_m_odd_w_qkv': 'new_m', 'new_m_swa_q_norm': 'new_m', 'new_m_swa_k_norm': 'new_m', 'new_m_swa_sink': 'new_m', 'new_m_odd_w_out': 'new_m', 'new_m_mlp_norm': 'new_m', 'new_m_mlp_w_up': 'new_m', 'new_m_mlp_w_down': 'new_m', 'new_v_even_norm': 'new_v', 'new_v_even_w_in': 'new_v', 'new_v_mla_q_lat_norm': 'new_v', 'new_v_mla_kv_lat_norm': 'new_v', 'new_v_mla_w_uq': 'new_v', 'new_v_mla_w_ukv': 'new_v', 'new_v_mla_q_norm': 'new_v', 'new_v_mla_k_nope_norm': 'new_v', 'new_v_mla_k_rope_norm': 'new_v', 'new_v_gqa_q_norm': 'new_v', 'new_v_gqa_k_norm': 'new_v', 'new_v_even_w_out': 'new_v', 'new_v_odd_norm': 'new_v', 'new_v_odd_w_qkv': 'new_v', 'new_v_swa_q_norm': 'new_v', 'new_v_swa_k_norm': 'new_v', 'new_v_swa_sink': 'new_v', 'new_v_odd_w_out': 'new_v', 'new_v_mlp_norm': 'new_v', 'new_v_mlp_w_up': 'new_v', 'new_v_mlp_w_down': 'new_v'}


def _forward(args):
    return _fwd_reference(*[args[k] for k in FWD_PARAMS])


def _output_shape():
    def fwd():
        inp = _fwd_setup_inputs(0)
        return _fwd_reference(*[inp[k] for k in FWD_PARAMS])
    out = _jax.eval_shape(fwd)
    return out.shape, out.dtype

N_MICROBATCH = 1
ADAM_LR = 0.001
ADAM_B1 = 0.9
ADAM_B2 = 0.999
ADAM_EPS = 1e-08
ADAM_WD = 0.01
ADAM_STEP = 10
PER_EXAMPLE_BATCH_AXIS = {'x': 0, 'loss_target': 0}
SHARED_INPUTS = []
_WEIGHT_DTYPES = {'even_norm': _jnp.float32, 'even_w_in': _jnp.float32, 'mla_q_lat_norm': _jnp.float32, 'mla_kv_lat_norm': _jnp.float32, 'mla_w_uq': _jnp.float32, 'mla_w_ukv': _jnp.float32, 'mla_q_norm': _jnp.float32, 'mla_k_nope_norm': _jnp.float32, 'mla_k_rope_norm': _jnp.float32, 'gqa_q_norm': _jnp.float32, 'gqa_k_norm': _jnp.float32, 'even_w_out': _jnp.float32, 'odd_norm': _jnp.float32, 'odd_w_qkv': _jnp.float32, 'swa_q_norm': _jnp.float32, 'swa_k_norm': _jnp.float32, 'swa_sink': _jnp.float32, 'odd_w_out': _jnp.float32, 'mlp_norm': _jnp.float32, 'mlp_w_up': _jnp.float32, 'mlp_w_down': _jnp.float32}
MOMENT_SCALE = {'even_norm': 3.305550e-01, 'even_w_in': 3.023924e-01, 'mla_q_lat_norm': 1.383440e-01, 'mla_kv_lat_norm': 7.698040e-01, 'mla_w_uq': 8.069713e-02, 'mla_w_ukv': 2.216988e-01, 'mla_q_norm': 4.146983e-01, 'mla_k_nope_norm': 4.661663e-01, 'mla_k_rope_norm': 3.330916e-01, 'gqa_q_norm': 4.817140e-01, 'gqa_k_norm': 4.847268e-01, 'even_w_out': 2.943859e-01, 'odd_norm': 1.461158e+01, 'odd_w_qkv': 1.254014e+01, 'swa_q_norm': 6.059946e+00, 'swa_k_norm': 6.049205e+00, 'swa_sink': 8.975850e-01, 'odd_w_out': 1.104806e+01, 'mlp_norm': 9.888615e+01, 'mlp_w_up': 4.856297e+00, 'mlp_w_down': 2.035336e+01}


def _to_microbatches(a, axis):
    t = _jnp.moveaxis(a, axis, 0)
    t = t.reshape((N_MICROBATCH, t.shape[0] // N_MICROBATCH) + t.shape[1:])
    return _jnp.moveaxis(t, 1, axis + 1)


def setup_inputs(seed: int = 0) -> dict:
    inp = _fwd_setup_inputs(seed)
    key = _jax.random.fold_in(_jax.random.key(seed), 7919)
    shape, _ = _output_shape()
    out = dict(inp)
    out["loss_target"] = _jax.random.normal(_jax.random.fold_in(key, 0), shape, _jnp.float32)
    for i, name in enumerate(TWIN_WEIGHTS):
        w = inp[name].astype(_jnp.float32)
        if MOMENT_SCALE is None:
            s = _jnp.sqrt(_jnp.mean(_jnp.square(w)) + 1e-30)
        else:
            s = MOMENT_SCALE[name]
        km, kv = _jax.random.split(_jax.random.fold_in(key, i + 1))
        out[name] = w
        out["m_" + name] = s * _jax.random.normal(km, w.shape, _jnp.float32)
        out["v_" + name] = (s * s) * _jax.random.uniform(kv, w.shape, _jnp.float32, 0.5, 1.5)
    if N_MICROBATCH > 1:
        for name, axis in PER_EXAMPLE_BATCH_AXIS.items():
            out[name] = _to_microbatches(out[name], axis)
    return {'x': out['x'], 'even_norm': out['even_norm'], 'even_w_in': out['even_w_in'], 'mla_q_lat_norm': out['mla_q_lat_norm'], 'mla_kv_lat_norm': out['mla_kv_lat_norm'], 'mla_w_uq': out['mla_w_uq'], 'mla_w_ukv': out['mla_w_ukv'], 'mla_q_norm': out['mla_q_norm'], 'mla_k_nope_norm': out['mla_k_nope_norm'], 'mla_k_rope_norm': out['mla_k_rope_norm'], 'gqa_q_norm': out['gqa_q_norm'], 'gqa_k_norm': out['gqa_k_norm'], 'even_w_out': out['even_w_out'], 'odd_norm': out['odd_norm'], 'odd_w_qkv': out['odd_w_qkv'], 'swa_q_norm': out['swa_q_norm'], 'swa_k_norm': out['swa_k_norm'], 'swa_sink': out['swa_sink'], 'odd_w_out': out['odd_w_out'], 'mlp_norm': out['mlp_norm'], 'mlp_w_up': out['mlp_w_up'], 'mlp_w_down': out['mlp_w_down'], 'loss_target': out['loss_target'], 'm_even_norm': out['m_even_norm'], 'm_even_w_in': out['m_even_w_in'], 'm_mla_q_lat_norm': out['m_mla_q_lat_norm'], 'm_mla_kv_lat_norm': out['m_mla_kv_lat_norm'], 'm_mla_w_uq': out['m_mla_w_uq'], 'm_mla_w_ukv': out['m_mla_w_ukv'], 'm_mla_q_norm': out['m_mla_q_norm'], 'm_mla_k_nope_norm': out['m_mla_k_nope_norm'], 'm_mla_k_rope_norm': out['m_mla_k_rope_norm'], 'm_gqa_q_norm': out['m_gqa_q_norm'], 'm_gqa_k_norm': out['m_gqa_k_norm'], 'm_even_w_out': out['m_even_w_out'], 'm_odd_norm': out['m_odd_norm'], 'm_odd_w_qkv': out['m_odd_w_qkv'], 'm_swa_q_norm': out['m_swa_q_norm'], 'm_swa_k_norm': out['m_swa_k_norm'], 'm_swa_sink': out['m_swa_sink'], 'm_odd_w_out': out['m_odd_w_out'], 'm_mlp_norm': out['m_mlp_norm'], 'm_mlp_w_up': out['m_mlp_w_up'], 'm_mlp_w_down': out['m_mlp_w_down'], 'v_even_norm': out['v_even_norm'], 'v_even_w_in': out['v_even_w_in'], 'v_mla_q_lat_norm': out['v_mla_q_lat_norm'], 'v_mla_kv_lat_norm': out['v_mla_kv_lat_norm'], 'v_mla_w_uq': out['v_mla_w_uq'], 'v_mla_w_ukv': out['v_mla_w_ukv'], 'v_mla_q_norm': out['v_mla_q_norm'], 'v_mla_k_nope_norm': out['v_mla_k_nope_norm'], 'v_mla_k_rope_norm': out['v_mla_k_rope_norm'], 'v_gqa_q_norm': out['v_gqa_q_norm'], 'v_gqa_k_norm': out['v_gqa_k_norm'], 'v_even_w_out': out['v_even_w_out'], 'v_odd_norm': out['v_odd_norm'], 'v_odd_w_qkv': out['v_odd_w_qkv'], 'v_swa_q_norm': out['v_swa_q_norm'], 'v_swa_k_norm': out['v_swa_k_norm'], 'v_swa_sink': out['v_swa_sink'], 'v_odd_w_out': out['v_odd_w_out'], 'v_mlp_norm': out['v_mlp_norm'], 'v_mlp_w_up': out['v_mlp_w_up'], 'v_mlp_w_down': out['v_mlp_w_down']}


def _loss(weights, diff, rest, loss_target):
    with _jax.named_scope("forward"):
        args = {**rest, TWIN_DIFF_INPUT: diff, **{k: w.astype(_WEIGHT_DTYPES[k]) for k, w in weights.items()}}
        y = _forward(args)
    with _jax.named_scope("loss_head"):
        err = _jnp.square(y.astype(_jnp.float32) - loss_target)
        return 0.5 * _jnp.sum(_jnp.mean(err, axis=-1)) if err.ndim else 0.5 * err


def _adamw(w, g, m, v):
    m = ADAM_B1 * m + (1.0 - ADAM_B1) * g
    v = ADAM_B2 * v + (1.0 - ADAM_B2) * _jnp.square(g)
    m_hat = m / (1.0 - ADAM_B1 ** ADAM_STEP)
    v_hat = v / (1.0 - ADAM_B2 ** ADAM_STEP)
    delta = -ADAM_LR * (m_hat / (_jnp.sqrt(v_hat) + ADAM_EPS) + ADAM_WD * w)
    return delta, m, v


def reference(x, even_norm, even_w_in, mla_q_lat_norm, mla_kv_lat_norm, mla_w_uq, mla_w_ukv, mla_q_norm, mla_k_nope_norm, mla_k_rope_norm, gqa_q_norm, gqa_k_norm, even_w_out, odd_norm, odd_w_qkv, swa_q_norm, swa_k_norm, swa_sink, odd_w_out, mlp_norm, mlp_w_up, mlp_w_down, loss_target, m_even_norm, m_even_w_in, m_mla_q_lat_norm, m_mla_kv_lat_norm, m_mla_w_uq, m_mla_w_ukv, m_mla_q_norm, m_mla_k_nope_norm, m_mla_k_rope_norm, m_gqa_q_norm, m_gqa_k_norm, m_even_w_out, m_odd_norm, m_odd_w_qkv, m_swa_q_norm, m_swa_k_norm, m_swa_sink, m_odd_w_out, m_mlp_norm, m_mlp_w_up, m_mlp_w_down, v_even_norm, v_even_w_in, v_mla_q_lat_norm, v_mla_kv_lat_norm, v_mla_w_uq, v_mla_w_ukv, v_mla_q_norm, v_mla_k_nope_norm, v_mla_k_rope_norm, v_gqa_q_norm, v_gqa_k_norm, v_even_w_out, v_odd_norm, v_odd_w_qkv, v_swa_q_norm, v_swa_k_norm, v_swa_sink, v_odd_w_out, v_mlp_norm, v_mlp_w_up, v_mlp_w_down):
    given = dict(x=x, even_norm=even_norm, even_w_in=even_w_in, mla_q_lat_norm=mla_q_lat_norm, mla_kv_lat_norm=mla_kv_lat_norm, mla_w_uq=mla_w_uq, mla_w_ukv=mla_w_ukv, mla_q_norm=mla_q_norm, mla_k_nope_norm=mla_k_nope_norm, mla_k_rope_norm=mla_k_rope_norm, gqa_q_norm=gqa_q_norm, gqa_k_norm=gqa_k_norm, even_w_out=even_w_out, odd_norm=odd_norm, odd_w_qkv=odd_w_qkv, swa_q_norm=swa_q_norm, swa_k_norm=swa_k_norm, swa_sink=swa_sink, odd_w_out=odd_w_out, mlp_norm=mlp_norm, mlp_w_up=mlp_w_up, mlp_w_down=mlp_w_down, loss_target=loss_target, m_even_norm=m_even_norm, m_even_w_in=m_even_w_in, m_mla_q_lat_norm=m_mla_q_lat_norm, m_mla_kv_lat_norm=m_mla_kv_lat_norm, m_mla_w_uq=m_mla_w_uq, m_mla_w_ukv=m_mla_w_ukv, m_mla_q_norm=m_mla_q_norm, m_mla_k_nope_norm=m_mla_k_nope_norm, m_mla_k_rope_norm=m_mla_k_rope_norm, m_gqa_q_norm=m_gqa_q_norm, m_gqa_k_norm=m_gqa_k_norm, m_even_w_out=m_even_w_out, m_odd_norm=m_odd_norm, m_odd_w_qkv=m_odd_w_qkv, m_swa_q_norm=m_swa_q_norm, m_swa_k_norm=m_swa_k_norm, m_swa_sink=m_swa_sink, m_odd_w_out=m_odd_w_out, m_mlp_norm=m_mlp_norm, m_mlp_w_up=m_mlp_w_up, m_mlp_w_down=m_mlp_w_down, v_even_norm=v_even_norm, v_even_w_in=v_even_w_in, v_mla_q_lat_norm=v_mla_q_lat_norm, v_mla_kv_lat_norm=v_mla_kv_lat_norm, v_mla_w_uq=v_mla_w_uq, v_mla_w_ukv=v_mla_w_ukv, v_mla_q_norm=v_mla_q_norm, v_mla_k_nope_norm=v_mla_k_nope_norm, v_mla_k_rope_norm=v_mla_k_rope_norm, v_gqa_q_norm=v_gqa_q_norm, v_gqa_k_norm=v_gqa_k_norm, v_even_w_out=v_even_w_out, v_odd_norm=v_odd_norm, v_odd_w_qkv=v_odd_w_qkv, v_swa_q_norm=v_swa_q_norm, v_swa_k_norm=v_swa_k_norm, v_swa_sink=v_swa_sink, v_odd_w_out=v_odd_w_out, v_mlp_norm=v_mlp_norm, v_mlp_w_up=v_mlp_w_up, v_mlp_w_down=v_mlp_w_down)
    weights = {n: given[n] for n in TWIN_WEIGHTS}
    shared = {n: given[n] for n in SHARED_INPUTS}
    per_example = {n: given[n] for n in ['x']}
    grad_fn = _jax.value_and_grad(_loss, argnums=(0, 1))

    def one_microbatch(ex, loss_target):
        ex = dict(ex)
        diff = ex.pop(TWIN_DIFF_INPUT)
        return grad_fn(weights, diff, {**shared, **ex}, loss_target)

    if N_MICROBATCH == 1:
        loss, (grad_w, grad_x) = one_microbatch(per_example, given["loss_target"])
    else:
        def body(carry, xs):
            loss_sum, grad_sum = carry
            l_k, (gw_k, gx_k) = one_microbatch(xs[0], xs[1])
            with _jax.named_scope("update"):
                return (loss_sum + l_k, _jax.tree.map(_jnp.add, grad_sum, gw_k)), gx_k

        init = (_jnp.zeros((), _jnp.float32), _jax.tree.map(_jnp.zeros_like, weights))
        (loss, grad_w), grad_x = _jax.lax.scan(body, init, (per_example, given["loss_target"]))
    with _jax.named_scope("update"):
        delta_w, new_m, new_v = {}, {}, {}
        for n in TWIN_WEIGHTS:
            delta_w[n], new_m[n], new_v[n] = _adamw(weights[n], grad_w[n], given["m_" + n], given["v_" + n])
    return (loss, grad_x, *[grad_w[n] for n in TWIN_WEIGHTS], *[delta_w[n] for n in TWIN_WEIGHTS],
            *[new_m[n] for n in TWIN_WEIGHTS], *[new_v[n] for n in TWIN_WEIGHTS])
```

```python
import functools
import math

import numpy as np
import jax
import jax.numpy as jnp
from jax import lax
from jax.experimental import pallas as pl
from jax.experimental.pallas import tpu as pltpu

F32 = jnp.float32
BF16 = jnp.bfloat16
MESH = pl.DeviceIdType.MESH

VMEM_BYTES_V7X = 64 * 1024 * 1024
LANES = 128
SUBLANES_BF16 = 16

GRID_W = 64
NORM_EPS = 1e-6
ROPE_THETA = 500000.0
AXIAL_THETA = 10000.0
MLA_HEADS = 8
MLA_Q_LORA = 512
MLA_KV_LORA = 256
MLA_NOPE = 128
MLA_ROPE = 64
MLA_QK = MLA_NOPE + MLA_ROPE
MLA_V = 128
GQA_HEADS = 8
GQA_KV = 2
GQA_DIM = 128
SWA_HEADS = 32
SWA_KV = 4
SWA_DIM = 64
SWA_WINDOW = 128
SWA_ROT = SWA_DIM // 4
SWA_BLOCK = 128
ADAM_LR = 0.001
ADAM_B1 = 0.9
ADAM_B2 = 0.999
ADAM_EPS = 1e-08
ADAM_WD = 0.01
ADAM_STEP = 10
N_CHIPS = 4
COMM_LANES = 1024


def _tile(dim, cap, mult=LANES):
    if dim <= cap:
        return dim
    t = (cap // mult) * mult
    while t >= mult:
        if dim % t == 0:
            return t
        t -= mult
    return dim


def _params(dims, vmem_estimate):
    limit = int(min(max(vmem_estimate * 1.25 + (4 << 20), 32 << 20), VMEM_BYTES_V7X - (6 << 20)))
    return pltpu.CompilerParams(dimension_semantics=dims, vmem_limit_bytes=limit)


def _nbytes(shape, dtype):
    return int(np.prod(shape)) * jnp.dtype(dtype).itemsize


def _mm(a, b, *, mode, name, out_dtype=F32, epi=None, extra=None, split=1, caps=(1024, 1024, 1024)):
    if mode == "nn":
        (M, K), (K2, N) = a.shape, b.shape
    elif mode == "nt":
        (M, K), (N, K2) = a.shape, b.shape
    else:
        (K, M), (K2, N) = a.shape, b.shape
    assert K == K2, (a.shape, b.shape, mode)
    assert N % split == 0
    ns = N // split
    tn, tk = _tile(ns, caps[1]), _tile(K, caps[2])
    tm = _tile(M, min(caps[0], max(LANES, caps[0] * caps[1] // tn)))
    nj_per = ns // tn
    grid = (M // tm, N // tn, K // tk)
    nk = grid[2]
    if mode == "nn":
        a_spec = pl.BlockSpec((tm, tk), lambda i, j, k: (i, k))
        b_spec = pl.BlockSpec((tk, tn), lambda i, j, k: (k, j))
        dn = (((1,), (0,)), ((), ()))
    elif mode == "nt":
        a_spec = pl.BlockSpec((tm, tk), lambda i, j, k: (i, k))
        b_spec = pl.BlockSpec((tn, tk), lambda i, j, k: (j, k))
        dn = (((1,), (1,)), ((), ()))
    else:
        a_spec = pl.BlockSpec((tk, tm), lambda i, j, k: (k, i))
        b_spec = pl.BlockSpec((tk, tn), lambda i, j, k: (k, j))
        dn = (((0,), (0,)), ((), ()))
    if split == 1:
        o_spec = pl.BlockSpec((tm, tn), lambda i, j, k: (i, j))
        o_shape = (M, N)
    else:
        o_spec = pl.BlockSpec((None, tm, tn), lambda i, j, k: (j // nj_per, i, j % nj_per))
        o_shape = (split, M, ns)
    mn_spec = pl.BlockSpec((tm, tn), lambda i, j, k: (i, j))
    in_specs, args = [a_spec, b_spec], [a, b]
    if epi in ("add", "dsqrelu"):
        in_specs.append(mn_spec)
        args.append(extra)
    if epi == "sqrelu":
        out_shape = (jax.ShapeDtypeStruct(o_shape, BF16), jax.ShapeDtypeStruct(o_shape, BF16))
        out_specs = (o_spec, o_spec)
        n_out = 2
    else:
        out_shape = jax.ShapeDtypeStruct(o_shape, out_dtype)
        out_specs = o_spec
        n_out = 1

    def body(*refs):
        a_ref, b_ref = refs[0], refs[1]
        e_ref = refs[2] if len(args) == 3 else None
        outs = refs[len(args):len(args) + n_out]
        acc_ref = refs[-1]
        k = pl.program_id(2)

        @pl.when(k == 0)
        def _():
            acc_ref[...] = jnp.zeros_like(acc_ref)

        acc_ref[...] += lax.dot_general(a_ref[...].astype(BF16), b_ref[...].astype(BF16), dn,
                                        preferred_element_type=F32)

        @pl.when(k == nk - 1)
        def _():
            acc = acc_ref[...]
            if epi is None:
                outs[0][...] = acc.astype(outs[0].dtype)
            elif epi == "add":
                outs[0][...] = (e_ref[...] + acc).astype(outs[0].dtype)
            elif epi == "sqrelu":
                r = jnp.maximum(acc, 0.0)
                outs[0][...] = acc.astype(BF16)
                outs[1][...] = (r * r).astype(BF16)
            else:
                u = e_ref[...].astype(F32)
                outs[0][...] = (acc * (2.0 * jnp.maximum(u, 0.0))).astype(outs[0].dtype)

    est = 2 * (_nbytes((tm, tk), a.dtype) + _nbytes((tk, tn), b.dtype)) + _nbytes((tm, tn), F32)
    est += 2 * n_out * _nbytes((tm, tn), out_dtype if n_out == 1 else BF16)
    if len(args) == 3:
        est += 2 * _nbytes((tm, tn), extra.dtype)
    est += 3 * _nbytes((tm, tn), F32)
    return pl.pallas_call(
        body, name=name, grid=grid, in_specs=in_specs, out_specs=out_specs, out_shape=out_shape,
        scratch_shapes=[pltpu.VMEM((tm, tn), F32)],
        compiler_params=_params(("parallel", "parallel", "arbitrary"), est),
    )(*args)


def _perm(y, p):
    hi = y.astype(BF16)
    r1 = y - hi.astype(F32)
    mid = r1.astype(BF16)
    lo = (r1 - mid.astype(F32)).astype(BF16)
    d = lambda t: jnp.dot(t, p, preferred_element_type=F32)
    return d(hi) + d(mid) + d(lo)


def _rows_tile(T, d):
    return _tile(T, 2048 if d <= 256 else 512, 128)


def _norm_fwd(x, gain, *, name, rope=None, out_dtype=BF16):
    H, T, d = x.shape
    tm = _rows_tile(T, d)
    g2 = gain.reshape(1, d).astype(F32)
    in_specs = [pl.BlockSpec((None, tm, d), lambda h, i: (h, i, 0)), pl.BlockSpec((1, d), lambda h, i: (0, 0))]
    args = [x, g2]
    if rope is not None:
        in_specs += [pl.BlockSpec((tm, d), lambda h, i: (i, 0)), pl.BlockSpec((tm, d), lambda h, i: (i, 0)),
                     pl.BlockSpec((d, d), lambda h, i: (0, 0))]
        args += list(rope)

    def body(*refs):
        x_ref, g_ref = refs[0], refs[1]
        o_ref = refs[-1]
        xv = x_ref[...]
        y = xv * lax.rsqrt(jnp.mean(xv * xv, axis=-1, keepdims=True) + NORM_EPS)
        y = y * g_ref[...]
        if rope is not None:
            c_ref, s_ref, p_ref = refs[2], refs[3], refs[4]
            y = y * c_ref[...] + _perm(y, p_ref[...]) * s_ref[...]
        o_ref[...] = y.astype(o_ref.dtype)

    est = 2 * (_nbytes((tm, max(d, LANES)), F32) * (3 if rope is not None else 1) + _nbytes((tm, max(d, LANES)), out_dtype))
    est += 6 * _nbytes((tm, max(d, LANES)), F32)
    return pl.pallas_call(
        body, name=name, grid=(H, T // tm), in_specs=in_specs,
        out_specs=pl.BlockSpec((None, tm, d), lambda h, i: (h, i, 0)),
        out_shape=jax.ShapeDtypeStruct((H, T, d), out_dtype),
        compiler_params=_params(("parallel", "parallel"), est),
    )(*args)


def _norm_bwd(x, gain, dy, *, name, rope=None, group=1, res=None, out_dtype=F32):
    H, T, d = x.shape
    assert dy.shape == (H * group, T, d), (dy.shape, x.shape, group)
    tm = _rows_tile(T, d)
    g2 = gain.reshape(1, d).astype(F32)
    in_specs = [pl.BlockSpec((None, tm, d), lambda h, i: (h, i, 0)), pl.BlockSpec((1, d), lambda h, i: (0, 0)),
                pl.BlockSpec((group, tm, d), lambda h, i: (h, i, 0))]
    args = [x, g2, dy]
    if rope is not None:
        in_specs += [pl.BlockSpec((tm, d), lambda h, i: (i, 0)), pl.BlockSpec((tm, d), lambda h, i: (i, 0)),
                     pl.BlockSpec((d, d), lambda h, i: (0, 0))]
        args += list(rope)
    if res is not None:
        assert H == 1
        in_specs.append(pl.BlockSpec((tm, d), lambda h, i: (i, 0)))
        args.append(res)
    n_in = len(args)

    def body(*refs):
        x_ref, g_ref, dy_ref = refs[0], refs[1], refs[2]
        dx_ref, dg_ref = refs[n_in], refs[n_in + 1]
        first = (pl.program_id(0) == 0) & (pl.program_id(1) == 0)

        @pl.when(first)
        def _():
            dg_ref[...] = jnp.zeros_like(dg_ref)

        dyv = dy_ref[0].astype(F32)
        for g in range(1, group):
            dyv = dyv + dy_ref[g].astype(F32)
        pos = 3
        if rope is not None:
            c_ref, s_ref, p_ref = refs[3], refs[4], refs[5]
            pos = 6
            dyv = dyv * c_ref[...] + _perm(dyv * s_ref[...], p_ref[...])
        xv = x_ref[...]
        r = lax.rsqrt(jnp.mean(xv * xv, axis=-1, keepdims=True) + NORM_EPS)
        xhat = xv * r
        dg_ref[...] += jnp.sum(dyv * xhat, axis=0, keepdims=True)
        dxh = dyv * g_ref[...]
        dx = r * (dxh - xhat * jnp.mean(dxh * xhat, axis=-1, keepdims=True))
        if res is not None:
            dx = dx + refs[pos][...]
        dx_ref[...] = dx.astype(dx_ref.dtype)

    wide = max(d, LANES)
    est = 2 * _nbytes((tm, wide), F32) * (2 + group + (2 if rope is not None else 0) + (1 if res is not None else 0))
    est += 8 * _nbytes((tm, wide), F32)
    return pl.pallas_call(
        body, name=name, grid=(H, T // tm), in_specs=in_specs,
        out_specs=(pl.BlockSpec((None, tm, d), lambda h, i: (h, i, 0)), pl.BlockSpec((1, d), lambda h, i: (0, 0))),
        out_shape=(jax.ShapeDtypeStruct((H, T, d), out_dtype), jax.ShapeDtypeStruct((1, d), F32)),
        compiler_params=_params(("arbitrary", "arbitrary"), est),
    )(*args)


def _group_sum(x, group, *, name):
    HG, T, d = x.shape
    H = HG // group
    tm = _rows_tile(T, d)

    def body(x_ref, o_ref):
        acc = x_ref[0]
        for g in range(1, group):
            acc = acc + x_ref[g]
        o_ref[...] = acc

    est = 2 * (group + 1) * _nbytes((tm, max(d, LANES)), F32)
    return pl.pallas_call(
        body, name=name, grid=(H, T // tm),
        in_specs=[pl.BlockSpec((group, tm, d), lambda h, i: (h, i, 0))],
        out_specs=pl.BlockSpec((None, tm, d), lambda h, i: (h, i, 0)),
        out_shape=jax.ShapeDtypeStruct((H, T, d), F32),
        compiler_params=_params(("parallel", "parallel"), est),
    )(x)


def _delta(o, do, *, name):
    H, T, d = o.shape
    tm = _rows_tile(T, d)

    def body(o_ref, do_ref, dl_ref, dob_ref):
        dov = do_ref[...]
        dl = jnp.sum(o_ref[...] * dov, axis=-1, keepdims=True)
        dl_ref[...] = jnp.broadcast_to(dl, (tm, LANES))
        dob_ref[...] = dov.astype(BF16)

    spec = pl.BlockSpec((None, tm, d), lambda h, i: (h, i, 0))
    est = 2 * (3 * _nbytes((tm, max(d, LANES)), F32) + _nbytes((tm, LANES), F32))
    return pl.pallas_call(
        body, name=name, grid=(H, T // tm), in_specs=[spec, spec],
        out_specs=(pl.BlockSpec((None, tm, LANES), lambda h, i: (h, i, 0)), spec),
        out_shape=(jax.ShapeDtypeStruct((H, T, LANES), F32), jax.ShapeDtypeStruct((H, T, d), BF16)),
        compiler_params=_params(("parallel", "parallel"), est),
    )(o, do)


NT_DIMS = (((1,), (1,)), ((), ()))
TN_DIMS = (((0,), (0,)), ((), ()))


def _flash_fwd(q, k, v, scale, *, name):
    H, T, dk = q.shape
    Hkv, _, dv = v.shape
    G = H // Hkv
    tq, tk = _tile(T, 1024), _tile(T, 1024)
    nk = T // tk

    def body(q_ref, k_ref, v_ref, o_ref, lse_ref, m_ref, l_ref, acc_ref):
        ki = pl.program_id(2)

        @pl.when(ki == 0)
        def _():
            m_ref[...] = jnp.full_like(m_ref, -jnp.inf)
            l_ref[...] = jnp.zeros_like(l_ref)
            acc_ref[...] = jnp.zeros_like(acc_ref)

        s = lax.dot_general(q_ref[...], k_ref[...], NT_DIMS, preferred_element_type=F32) * scale
        m_prev = m_ref[...]
        m_new = jnp.maximum(m_prev, jnp.max(s, axis=-1, keepdims=True))
        alpha = jnp.exp(m_prev - m_new)
        p = jnp.exp(s - m_new)
        l_ref[...] = alpha * l_ref[...] + jnp.sum(p, axis=-1, keepdims=True)
        acc_ref[...] = alpha * acc_ref[...] + jnp.dot(p.astype(BF16), v_ref[...], preferred_element_type=F32)
        m_ref[...] = m_new

        @pl.when(ki == nk - 1)
        def _():
            l = l_ref[...]
            o_ref[...] = acc_ref[...] / l
            lse_ref[...] = jnp.broadcast_to(m_ref[...] + jnp.log(l), (tq, LANES))

    est = 2 * (_nbytes((tq, dk), BF16) + _nbytes((tk, dk + dv), BF16) + _nbytes((tq, dv + LANES), F32))
    est += 4 * _nbytes((tq, tk), F32) + _nbytes((tq, dv + 2 * LANES), F32)
    return pl.pallas_call(
        body, name=name, grid=(H, T // tq, nk),
        in_specs=[pl.BlockSpec((None, tq, dk), lambda h, i, j: (h, i, 0)),
                  pl.BlockSpec((None, tk, dk), lambda h, i, j: (h // G, j, 0)),
                  pl.BlockSpec((None, tk, dv), lambda h, i, j: (h // G, j, 0))],
        out_specs=(pl.BlockSpec((None, tq, dv), lambda h, i, j: (h, i, 0)),
                   pl.BlockSpec((None, tq, LANES), lambda h, i, j: (h, i, 0))),
        out_shape=(jax.ShapeDtypeStruct((H, T, dv), F32), jax.ShapeDtypeStruct((H, T, LANES), F32)),
        scratch_shapes=[pltpu.VMEM((tq, 1), F32), pltpu.VMEM((tq, 1), F32), pltpu.VMEM((tq, dv), F32)],
        compiler_params=_params(("parallel", "parallel", "arbitrary"), est),
    )(q, k, v)


def _flash_bwd(q, k, v, do, lse, delta, scale, *, name):
    H, T, dk = q.shape
    Hkv, _, dv = v.shape
    G = H // Hkv
    tq, tk = _tile(T, 512), _tile(T, 1024)

    def body(q_ref, k_ref, v_ref, do_ref, lse_ref, dl_ref, dq_ref, dk_ref, dv_ref):
        ki, qi = pl.program_id(1), pl.program_id(2)
        qv, kv, vv, dov = q_ref[...], k_ref[...], v_ref[...], do_ref[...]
        s = lax.dot_general(qv, kv, NT_DIMS, preferred_element_type=F32) * scale
        p = jnp.exp(s - lse_ref[:, :1])
        dp = lax.dot_general(dov, vv, NT_DIMS, preferred_element_type=F32)
        ds = (p * (dp - dl_ref[:, :1]) * scale).astype(BF16)
        pb = p.astype(BF16)
        dv_c = lax.dot_general(pb, dov, TN_DIMS, preferred_element_type=F32)
        dk_c = lax.dot_general(ds, qv, TN_DIMS, preferred_element_type=F32)
        dq_c = jnp.dot(ds, kv, preferred_element_type=F32)
        rows = pl.ds(pl.multiple_of(qi * tq, tq), tq)

        @pl.when(qi == 0)
        def _():
            dk_ref[...] = dk_c
            dv_ref[...] = dv_c

        @pl.when(qi != 0)
        def _():
            dk_ref[...] += dk_c
            dv_ref[...] += dv_c

        @pl.when(ki == 0)
        def _():
            dq_ref[rows, :] = dq_c

        @pl.when(ki != 0)
        def _():
            dq_ref[rows, :] += dq_c

    est = 2 * (_nbytes((tq, dk + dv), BF16) + _nbytes((tk, dk + dv), BF16) + 2 * _nbytes((tq, LANES), F32))
    est += 2 * (_nbytes((T, dk), F32) + _nbytes((tk, dk + dv), F32)) + 6 * _nbytes((tq, tk), F32)
    return pl.pallas_call(
        body, name=name, grid=(H, T // tk, T // tq),
        in_specs=[pl.BlockSpec((None, tq, dk), lambda h, j, i: (h, i, 0)),
                  pl.BlockSpec((None, tk, dk), lambda h, j, i: (h // G, j, 0)),
                  pl.BlockSpec((None, tk, dv), lambda h, j, i: (h // G, j, 0)),
                  pl.BlockSpec((None, tq, dv), lambda h, j, i: (h, i, 0)),
                  pl.BlockSpec((None, tq, LANES), lambda h, j, i: (h, i, 0)),
                  pl.BlockSpec((None, tq, LANES), lambda h, j, i: (h, i, 0))],
        out_specs=(pl.BlockSpec((None, T, dk), lambda h, j, i: (h, 0, 0)),
                   pl.BlockSpec((None, tk, dk), lambda h, j, i: (h, j, 0)),
                   pl.BlockSpec((None, tk, dv), lambda h, j, i: (h, j, 0))),
        out_shape=(jax.ShapeDtypeStruct((H, T, dk), F32), jax.ShapeDtypeStruct((H, T, dk), F32),
                   jax.ShapeDtypeStruct((H, T, dv), F32)),
        compiler_params=_params(("arbitrary", "arbitrary", "arbitrary"), est),
    )(q, k, v, do, lse, delta)


def _swa_specs(G, d, n_blocks, lanes):
    B = SWA_BLOCK
    prev = lambda j, i: (j, jnp.maximum(i - 1, 0), 0)
    cur = lambda j, i: (j, i, 0)
    nxt = lambda j, i: (j, jnp.minimum(i + 1, n_blocks - 1), 0)
    q_specs = [pl.BlockSpec((G, B, lanes), m) for m in (prev, cur, nxt)]
    kv_specs = [pl.BlockSpec((None, B, d), m) for m in (prev, cur, nxt)]
    return q_specs, kv_specs, cur


def _swa_fwd(q, k, v, sink, scale, *, name):
    Hq, T, d = q.shape
    Hkv = k.shape[0]
    G = Hq // Hkv
    B = SWA_BLOCK
    nb = T // B
    _, kv_specs, cur = _swa_specs(G, d, nb, d)

    def body(q_ref, k0, k1, k2, v0, v1, v2, sink_ref, o_ref, lse_ref):
        i = pl.program_id(1)
        qv = q_ref[...].reshape(G * B, d)
        kv = jnp.concatenate([k0[...], k1[...], k2[...]], axis=0)
        vv = jnp.concatenate([v0[...], v1[...], v2[...]], axis=0)
        s = lax.dot_general(qv, kv, NT_DIMS, preferred_element_type=F32) * scale
        row = lax.broadcasted_iota(jnp.int32, (G * B, 3 * B), 0) & (B - 1)
        col = lax.broadcasted_iota(jnp.int32, (G * B, 3 * B), 1)
        kpos = (i - 1) * B + col
        valid = (col >= row) & (col <= row + 2 * SWA_WINDOW) & (kpos >= 0) & (kpos < T)
        s = jnp.where(valid, s, -jnp.inf).reshape(G, B, 3 * B)
        sk = sink_ref[...]
        m = jnp.maximum(jnp.max(s, axis=-1, keepdims=True), sk)
        p = jnp.exp(s - m)
        denom = jnp.sum(p, axis=-1, keepdims=True) + jnp.exp(sk - m)
        pn = (p / denom).reshape(G * B, 3 * B).astype(BF16)
        o_ref[...] = jnp.dot(pn, vv, preferred_element_type=F32).reshape(G, B, d)
        lse_ref[...] = jnp.broadcast_to(m + jnp.log(denom), (G, B, LANES))

    est = 2 * (_nbytes((G, B, LANES), BF16) + 6 * _nbytes((B, LANES), BF16) + 2 * _nbytes((G, B, LANES), F32))
    est += 8 * _nbytes((G * B, 3 * B), F32)
    return pl.pallas_call(
        body, name=name, grid=(Hkv, nb),
        in_specs=[pl.BlockSpec((G, B, d), cur)] + kv_specs + kv_specs + [pl.BlockSpec((G, 1, 1), lambda j, i: (j, 0, 0))],
        out_specs=(pl.BlockSpec((G, B, d), cur), pl.BlockSpec((G, B, LANES), cur)),
        out_shape=(jax.ShapeDtypeStruct((Hq, T, d), F32), jax.ShapeDtypeStruct((Hq, T, LANES), F32)),
        compiler_params=_params(("parallel", "parallel"), est),
    )(q, k, k, k, v, v, v, sink)


def _swa_dq(q, k, v, do, lse, delta, sink, scale, *, name):
    Hq, T, d = q.shape
    Hkv = k.shape[0]
    G = Hq // Hkv
    B = SWA_BLOCK
    nb = T // B
    _, kv_specs, cur = _swa_specs(G, d, nb, d)

    def body(q_ref, do_ref, lse_ref, dl_ref, k0, k1, k2, v0, v1, v2, sink_ref, dq_ref, dsink_ref):
        i = pl.program_id(1)
        qv = q_ref[...].reshape(G * B, d)
        dov = do_ref[...].reshape(G * B, d)
        kv = jnp.concatenate([k0[...], k1[...], k2[...]], axis=0)
        vv = jnp.concatenate([v0[...], v1[...], v2[...]], axis=0)
        s = lax.dot_general(qv, kv, NT_DIMS, preferred_element_type=F32) * scale
        row = lax.broadcasted_iota(jnp.int32, (G * B, 3 * B), 0) & (B - 1)
        col = lax.broadcasted_iota(jnp.int32, (G * B, 3 * B), 1)
        kpos = (i - 1) * B + col
        valid = (col >= row) & (col <= row + 2 * SWA_WINDOW) & (kpos >= 0) & (kpos < T)
        lse = lse_ref[:, :, :1]
        dl = dl_ref[:, :, :1]
        s = jnp.where(valid, s, -jnp.inf).reshape(G, B, 3 * B)
        p = jnp.exp(s - lse)
        dp = lax.dot_general(dov, vv, NT_DIMS, preferred_element_type=F32).reshape(G, B, 3 * B)
        ds = (p * (dp - dl) * scale).reshape(G * B, 3 * B).astype(BF16)
        dq_ref[...] = jnp.dot(ds, kv, preferred_element_type=F32).reshape(G, B, d)
        dsk = -jnp.sum(jnp.exp(sink_ref[...] - lse) * dl, axis=1, keepdims=True)

        @pl.when(i == 0)
        def _():
            dsink_ref[...] = jnp.zeros_like(dsink_ref)

        dsink_ref[...] += jnp.broadcast_to(dsk, (G, 1, LANES))

    est = 2 * (2 * _nbytes((G, B, LANES), BF16) + 6 * _nbytes((B, LANES), BF16) + 3 * _nbytes((G, B, LANES), F32))
    est += 8 * _nbytes((G * B, 3 * B), F32)
    return pl.pallas_call(
        body, name=name, grid=(Hkv, nb),
        in_specs=[pl.BlockSpec((G, B, d), cur), pl.BlockSpec((G, B, d), cur), pl.BlockSpec((G, B, LANES), cur),
                  pl.BlockSpec((G, B, LANES), cur)] + kv_specs + kv_specs
                 + [pl.BlockSpec((G, 1, 1), lambda j, i: (j, 0, 0))],
        out_specs=(pl.BlockSpec((G, B, d), cur), pl.BlockSpec((G, 1, LANES), lambda j, i: (j, 0, 0))),
        out_shape=(jax.ShapeDtypeStruct((Hq, T, d), F32), jax.ShapeDtypeStruct((Hq, 1, LANES), F32)),
        compiler_params=_params(("arbitrary", "arbitrary"), est),
    )(q, do, lse, delta, k, k, k, v, v, v, sink)


def _swa_dkv(q, k, v, do, lse, delta, scale, *, name):
    Hq, T, d = q.shape
    Hkv = k.shape[0]
    G = Hq // Hkv
    B = SWA_BLOCK
    nb = T // B
    q_specs, _, cur = _swa_specs(G, d, nb, d)
    l_specs, _, _ = _swa_specs(G, d, nb, LANES)

    def body(k_ref, v_ref, q0, q1, q2, d0, d1, d2, l0, l1, l2, e0, e1, e2, dk_ref, dv_ref):
        b = pl.program_id(1)
        kv, vv = k_ref[...], v_ref[...]
        dk_acc = jnp.zeros((B, d), F32)
        dv_acc = jnp.zeros((B, d), F32)
        for part, (q_ref, do_ref, lse_ref, dl_ref) in enumerate(((q0, d0, l0, e0), (q1, d1, l1, e1), (q2, d2, l2, e2))):
            qv = q_ref[...].reshape(G * B, d)
            dov = do_ref[...].reshape(G * B, d)
            s = lax.dot_general(qv, kv, NT_DIMS, preferred_element_type=F32) * scale
            row = lax.broadcasted_iota(jnp.int32, (G * B, B), 0) & (B - 1)
            col = lax.broadcasted_iota(jnp.int32, (G * B, B), 1)
            qpos = (b + part - 1) * B + row
            diff = qpos - (b * B + col)
            valid = (diff >= -SWA_WINDOW) & (diff <= SWA_WINDOW) & (qpos >= 0) & (qpos < T)
            lse = lse_ref[:, :, :1].reshape(G * B, 1)
            dl = dl_ref[:, :, :1].reshape(G * B, 1)
            p = jnp.exp(jnp.where(valid, s, -jnp.inf) - lse)
            dp = lax.dot_general(dov, vv, NT_DIMS, preferred_element_type=F32)
            ds = (p * (dp - dl) * scale).astype(BF16)
            dv_acc = dv_acc + lax.dot_general(p.astype(BF16), dov, TN_DIMS, preferred_element_type=F32)
            dk_acc = dk_acc + lax.dot_general(ds, qv, TN_DIMS, preferred_element_type=F32)
        dk_ref[...] = dk_acc
        dv_ref[...] = dv_acc

    est = 2 * (6 * _nbytes((G, B, LANES), BF16) + 6 * _nbytes((G, B, LANES), F32) + 4 * _nbytes((B, LANES), F32))
    est += 10 * _nbytes((G * B, B), F32)
    kspec = pl.BlockSpec((None, B, d), cur)
    return pl.pallas_call(
        body, name=name, grid=(Hkv, nb),
        in_specs=[kspec, kspec] + q_specs + q_specs + l_specs + l_specs,
        out_specs=(kspec, kspec),
        out_shape=(jax.ShapeDtypeStruct((Hkv, T, d), F32), jax.ShapeDtypeStruct((Hkv, T, d), F32)),
        compiler_params=_params(("parallel", "parallel"), est),
    )(k, v, q, q, q, do, do, do, lse, lse, lse, delta, delta, delta)


def _loss_head(y, target, *, name):
    T, D = y.shape
    tm = _tile(T, 512)

    def body(y_ref, t_ref, dy_ref, s_ref):
        @pl.when(pl.program_id(0) == 0)
        def _():
            s_ref[...] = jnp.zeros_like(s_ref)

        e = y_ref[...] - t_ref[...]
        dy_ref[...] = e / D
        s_ref[...] += jnp.sum(jnp.sum(e * e, axis=-1, keepdims=True), axis=0, keepdims=True)

    spec = pl.BlockSpec((tm, D), lambda i: (i, 0))
    return pl.pallas_call(
        body, name=name, grid=(T // tm,), in_specs=[spec, spec],
        out_specs=(spec, pl.BlockSpec((1, 1), lambda i: (0, 0))),
        out_shape=(jax.ShapeDtypeStruct((T, D), F32), jax.ShapeDtypeStruct((1, 1), F32)),
        compiler_params=_params(("arbitrary",), 8 * _nbytes((tm, D), F32)),
    )(y, target)


def _adamw(w, g, m, v, *, name):
    R, C = w.shape
    tr = _tile(R, max(8, (1 << 19) // max(C, LANES) // 8 * 8), 8)

    def body(w_ref, g_ref, m_ref, v_ref, d_ref, nm_ref, nv_ref):
        gv = g_ref[...]
        nm = ADAM_B1 * m_ref[...] + (1.0 - ADAM_B1) * gv
        nv = ADAM_B2 * v_ref[...] + (1.0 - ADAM_B2) * jnp.square(gv)
        m_hat = nm / (1.0 - ADAM_B1 ** ADAM_STEP)
        v_hat = nv / (1.0 - ADAM_B2 ** ADAM_STEP)
        d_ref[...] = -ADAM_LR * (m_hat / (jnp.sqrt(v_hat) + ADAM_EPS) + ADAM_WD * w_ref[...])
        nm_ref[...] = nm
        nv_ref[...] = nv

    spec = pl.BlockSpec((tr, C), lambda i: (i, 0))
    sds = jax.ShapeDtypeStruct((R, C), F32)
    return pl.pallas_call(
        body, name=name, grid=(R // tr,), in_specs=[spec] * 4, out_specs=(spec,) * 3, out_shape=(sds,) * 3,
        compiler_params=_params(("parallel",), 16 * _nbytes((tr, max(C, LANES)), F32)),
    )(w, g, m, v)


def _comm_rows_tile(R):
    return _tile(R, 512, SUBLANES_BF16)


def _pair_add(g, recv, c_idx, *, name):
    _, _, R, L = g.shape
    tr = _comm_rows_tile(R)

    def body(c_ref, g_ref, r_ref, o_ref):
        o_ref[...] = (g_ref[...] + r_ref[...]).astype(BF16)

    grid_spec = pltpu.PrefetchScalarGridSpec(
        num_scalar_prefetch=1, grid=(N_CHIPS, R // tr),
        in_specs=[pl.BlockSpec((None, None, tr, L), lambda j, i, c: (j, c[0], i, 0)),
                  pl.BlockSpec((None, tr, L), lambda j, i, c: (j, i, 0))],
        out_specs=pl.BlockSpec((None, tr, L), lambda j, i, c: (j, i, 0)))
    return pl.pallas_call(
        body, name=name, grid_spec=grid_spec, out_shape=jax.ShapeDtypeStruct((N_CHIPS, R, L), BF16),
        compiler_params=_params(("parallel", "parallel"), 8 * _nbytes((tr, L), F32)),
    )(c_idx, g, recv)


def _sum_chips(q, *, name):
    _, R, L = q.shape
    tr = _comm_rows_tile(R)

    def body(q_ref, o_ref):
        acc = q_ref[0].astype(F32)
        for j in range(1, N_CHIPS):
            acc = acc + q_ref[j].astype(F32)
        o_ref[...] = acc

    return pl.pallas_call(
        body, name=name, grid=(R // tr,),
        in_specs=[pl.BlockSpec((N_CHIPS, tr, L), lambda i: (0, i, 0))],
        out_specs=pl.BlockSpec((tr, L), lambda i: (i, 0)),
        out_shape=jax.ShapeDtypeStruct((R, L), F32),
        compiler_params=_params(("parallel",), 10 * _nbytes((tr, L), F32)),
    )(q)


HBM_SPEC = pl.BlockSpec(memory_space=pltpu.HBM)


def _position():
    return lax.axis_index("x"), lax.axis_index("y"), lax.axis_index("c")


def _other_chips(x, y):
    return [(1 - x, y), (x, 1 - y), (1 - x, 1 - y)]


def _all_gather_halves(w, *, name):
    _, R, L = w.shape

    def body(w_ref, out_ref, send_sems, recv_sems, local_sem):
        x, y, c = _position()
        me, sibling = (x, y, c), (x, y, 1 - c)
        chips = _other_chips(x, y)

        def slot(px, py, pc):
            return out_ref.at[4 * px + 2 * py + pc]

        def copy(k, block, to, src=None):
            return pltpu.make_async_remote_copy(
                src_ref=slot(*block) if src is None else src, dst_ref=slot(*block),
                send_sem=send_sems.at[k], recv_sem=recv_sems.at[k], device_id=to, device_id_type=MESH)

        own = w_ref.at[c]
        mine = pltpu.make_async_copy(own, slot(*me), local_sem)
        mine.start()
        first = [copy(0, me, sibling, src=own)]
        first += [copy(1 + j, me, (*chip, c), src=own) for j, chip in enumerate(chips)]
        for cp in first:
            cp.start()
        passed = [copy(4 + j, (*chip, c), sibling) for j, chip in enumerate(chips)]
        for j, chip in enumerate(chips):
            copy(1 + j, (*chip, c), me).wait_recv()
            passed[j].start()
        copy(0, sibling, me).wait_recv()
        for j, chip in enumerate(chips):
            copy(4 + j, (*chip, 1 - c), me).wait_recv()
        for cp in first + passed:
            cp.wait_send()
        mine.wait()

    return pl.pallas_call(
        body, name=name, in_specs=[HBM_SPEC], out_specs=HBM_SPEC,
        out_shape=jax.ShapeDtypeStruct((2 * N_CHIPS, R, L), w.dtype),
        scratch_shapes=[pltpu.SemaphoreType.DMA((7,)), pltpu.SemaphoreType.DMA((7,)), pltpu.SemaphoreType.DMA],
    )(w)


def _sibling_exchange(g, *, name):
    _, _, R, L = g.shape

    def body(g_ref, out_ref, send_sems, recv_sems):
        x, y, c = _position()
        copies = [pltpu.make_async_remote_copy(
            src_ref=g_ref.at[j, 1 - c], dst_ref=out_ref.at[j], send_sem=send_sems.at[j], recv_sem=recv_sems.at[j],
            device_id=(x, y, 1 - c), device_id_type=MESH) for j in range(N_CHIPS)]
        for cp in copies:
            cp.start()
        for cp in copies:
            cp.wait()

    return pl.pallas_call(
        body, name=name, in_specs=[HBM_SPEC], out_specs=HBM_SPEC,
        out_shape=jax.ShapeDtypeStruct((N_CHIPS, R, L), g.dtype),
        scratch_shapes=[pltpu.SemaphoreType.DMA((N_CHIPS,)), pltpu.SemaphoreType.DMA((N_CHIPS,))],
    )(g)


def _chip_scatter(p, *, name):
    _, R, L = p.shape

    def body(p_ref, q_ref, send_sems, recv_sems, local_sem):
        x, y, c = _position()
        me = 2 * x + y
        chips = _other_chips(x, y)
        local = pltpu.make_async_copy(p_ref.at[me], q_ref.at[me], local_sem)
        local.start()

        def copy(k, chip, src_slot, dst_slot):
            return pltpu.make_async_remote_copy(
                src_ref=p_ref.at[src_slot], dst_ref=q_ref.at[dst_slot], send_sem=send_sems.at[k],
                recv_sem=recv_sems.at[k], device_id=(*chip, c), device_id_type=MESH)

        sends = [copy(k, chip, 2 * chip[0] + chip[1], me) for k, chip in enumerate(chips)]
        for cp in sends:
            cp.start()
        for k, chip in enumerate(chips):
            copy(k, chip, me, 2 * chip[0] + chip[1]).wait_recv()
        for cp in sends:
            cp.wait_send()
        local.wait()

    return pl.pallas_call(
        body, name=name, in_specs=[HBM_SPEC], out_specs=HBM_SPEC,
        out_shape=jax.ShapeDtypeStruct((N_CHIPS, R, L), p.dtype),
        scratch_shapes=[pltpu.SemaphoreType.DMA((3,)), pltpu.SemaphoreType.DMA((3,)), pltpu.SemaphoreType.DMA],
    )(p)


def _sibling_share(r, *, name):
    R, L = r.shape

    def body(r_ref, f_ref, send_sem, recv_sem, local_sem):
        x, y, c = _position()
        local = pltpu.make_async_copy(r_ref, f_ref.at[c], local_sem)
        local.start()
        send = pltpu.make_async_remote_copy(src_ref=r_ref, dst_ref=f_ref.at[c], send_sem=send_sem, recv_sem=recv_sem,
                                            device_id=(x, y, 1 - c), device_id_type=MESH)
        send.start()
        pltpu.make_async_remote_copy(src_ref=r_ref, dst_ref=f_ref.at[1 - c], send_sem=send_sem, recv_sem=recv_sem,
                                     device_id=(x, y, 1 - c), device_id_type=MESH).wait_recv()
        send.wait_send()
        local.wait()

    return pl.pallas_call(
        body, name=name, in_specs=[HBM_SPEC], out_specs=HBM_SPEC,
        out_shape=jax.ShapeDtypeStruct((2, R, L), r.dtype),
        scratch_shapes=[pltpu.SemaphoreType.DMA, pltpu.SemaphoreType.DMA, pltpu.SemaphoreType.DMA],
    )(r)


def _all_reduce_small(s, *, name):
    R, L = s.shape
    n_dev = 2 * N_CHIPS

    def body(s_ref, out_ref, buf, send_sems, recv_sems, local_sem):
        x, y, c = _position()
        me, sibling = (x, y, c), (x, y, 1 - c)
        chips = _other_chips(x, y)

        def slot(px, py, pc):
            return buf.at[4 * px + 2 * py + pc]

        def copy(k, block, to, src=None):
            return pltpu.make_async_remote_copy(
                src_ref=slot(*block) if src is None else src, dst_ref=slot(*block),
                send_sem=send_sems.at[k], recv_sem=recv_sems.at[k], device_id=to, device_id_type=MESH)

        mine = pltpu.make_async_copy(s_ref, slot(*me), local_sem)
        mine.start()
        first = [copy(0, me, sibling, src=s_ref)]
        first += [copy(1 + j, me, (*chip, c), src=s_ref) for j, chip in enumerate(chips)]
        for cp in first:
            cp.start()
        passed = [copy(4 + j, (*chip, c), sibling) for j, chip in enumerate(chips)]
        for j, chip in enumerate(chips):
            copy(1 + j, (*chip, c), me).wait_recv()
            passed[j].start()
        copy(0, sibling, me).wait_recv()
        for j, chip in enumerate(chips):
            copy(4 + j, (*chip, 1 - c), me).wait_recv()
        for cp in first + passed:
            cp.wait_send()
        mine.wait()
        acc = buf[0]
        for j in range(1, n_dev):
            acc = acc + buf[j]
        out_ref[...] = acc

    vmem = pl.BlockSpec(memory_space=pltpu.VMEM)
    return pl.pallas_call(
        body, name=name, in_specs=[vmem], out_specs=vmem, out_shape=jax.ShapeDtypeStruct((R, L), F32),
        scratch_shapes=[pltpu.VMEM((n_dev, R, L), F32), pltpu.SemaphoreType.DMA((7,)), pltpu.SemaphoreType.DMA((7,)),
                        pltpu.SemaphoreType.DMA],
    )(s)


def _rope_cos_sin(pos, dim, theta):
    inv = jnp.float32(theta) ** (-jnp.arange(0, dim, 2, dtype=F32) / dim)
    ang = pos.astype(F32)[:, None] * inv[None, :]
    return jnp.cos(ang), jnp.sin(ang)


def _rope_tables(T, d, segments):
    C = jnp.ones((T, d), F32)
    S = jnp.zeros((T, d), F32)
    P = np.zeros((d, d), np.float32)
    for start, size, cos, sin in segments:
        half = size // 2
        C = C.at[:, start:start + half].set(cos).at[:, start + half:start + size].set(cos)
        S = S.at[:, start:start + half].set(-sin).at[:, start + half:start + size].set(sin)
        for p in range(half):
            P[start + half + p, start + p] = 1.0
            P[start + p, start + half + p] = 1.0
    return C, S, jnp.asarray(P, BF16)


def _heads(t, H, d):
    return t.reshape(t.shape[0], H, d).transpose(1, 0, 2)


def _unheads(t):
    H, T, d = t.shape
    return t.transpose(1, 0, 2).reshape(T, H * d)


def _mlp_fwd(x, gain, w_up, w_down, tag):
    hm = _norm_fwd(x[None], gain, name=f"mlp{tag}_norm")[0]
    u, act = _mm(hm, w_up, mode="nn", name=f"mlp{tag}_up", epi="sqrelu")
    x_out = _mm(act, w_down, mode="nn", name=f"mlp{tag}_down", epi="add", extra=x)
    return x_out, (hm, u, act)


def _mlp_bwd(x, gain, w_up, w_down, saved, dxo, tag):
    hm, u, act = saved
    du = _mm(dxo, w_down, mode="nt", name=f"mlp{tag}_dact", epi="dsqrelu", extra=u, out_dtype=BF16)
    dw_down = _mm(act, dxo, mode="tn", name=f"mlp{tag}_dwdown")
    dhm = _mm(du, w_up, mode="nt", name=f"mlp{tag}_dhm")
    dw_up = _mm(hm, du, mode="tn", name=f"mlp{tag}_dwup")
    dx, dgain = _norm_bwd(x[None], gain, dhm[None], name=f"mlp{tag}_dnorm", res=dxo)
    return dx[0], dgain[0], dw_up, dw_down


def _local_step(x, target, W, small):
    T, D = x.shape
    pos = jnp.arange(T)
    mla_cos, mla_sin = _rope_cos_sin(pos, MLA_ROPE, ROPE_THETA)
    row_cos, row_sin = _rope_cos_sin(pos // GRID_W, GQA_DIM // 2, AXIAL_THETA)
    col_cos, col_sin = _rope_cos_sin(pos % GRID_W, GQA_DIM // 2, AXIAL_THETA)
    swa_cos, swa_sin = _rope_cos_sin(pos, SWA_ROT, ROPE_THETA)
    rope_q = _rope_tables(T, MLA_QK, [(MLA_NOPE, MLA_ROPE, mla_cos, mla_sin)])
    rope_kr = _rope_tables(T, MLA_ROPE, [(0, MLA_ROPE, mla_cos, mla_sin)])
    half = GQA_DIM // 2
    rope_ax = _rope_tables(T, GQA_DIM, [(0, half, row_cos, row_sin), (half, half, col_cos, col_sin)])
    rope_sw = _rope_tables(T, SWA_DIM, [(0, SWA_ROT, swa_cos, swa_sin)])
    o1 = MLA_Q_LORA
    o2 = o1 + MLA_KV_LORA
    o3 = o2 + MLA_ROPE
    o4 = o3 + GQA_HEADS * GQA_DIM
    o5 = o4 + GQA_KV * GQA_DIM
    sc_a, sc_g, sc_s = MLA_QK ** -0.5, GQA_DIM ** -0.5, SWA_DIM ** -0.5
    kv_w = MLA_NOPE + MLA_V

    h0 = _norm_fwd(x[None], small["even_norm"], name="even_norm")[0]
    proj = _mm(h0, W["even_w_in"], mode="nn", name="even_in")
    c_q, c_kv, kr_raw = proj[:, :o1], proj[:, o1:o2], proj[:, o2:o3]
    qg_raw = _heads(proj[:, o3:o4], GQA_HEADS, GQA_DIM)
    kg_raw = _heads(proj[:, o4:o5], GQA_KV, GQA_DIM)
    vg = _heads(proj[:, o5:], GQA_KV, GQA_DIM).astype(BF16)
    cqn = _norm_fwd(c_q[None], small["mla_q_lat_norm"], name="q_lat_norm")[0]
    ckvn = _norm_fwd(c_kv[None], small["mla_kv_lat_norm"], name="kv_lat_norm")[0]
    qa_raw = _heads(_mm(cqn, W["mla_w_uq"], mode="nn", name="mla_uq"), MLA_HEADS, MLA_QK)
    kv = _mm(ckvn, W["mla_w_ukv"], mode="nn", name="mla_ukv").reshape(T, MLA_HEADS, kv_w)
    kn_raw = kv[:, :, :MLA_NOPE].transpose(1, 0, 2)
    va = kv[:, :, MLA_NOPE:].transpose(1, 0, 2).astype(BF16)
    q_a = _norm_fwd(qa_raw, small["mla_q_norm"], name="mla_q_prep", rope=rope_q)
    k_n = _norm_fwd(kn_raw, small["mla_k_nope_norm"], name="mla_kn_prep")
    k_r = _norm_fwd(kr_raw[None], small["mla_k_rope_norm"], name="mla_kr_prep", rope=rope_kr)
    k_a = jnp.concatenate([k_n, jnp.broadcast_to(k_r, (MLA_HEADS, T, MLA_ROPE))], axis=-1)
    o_a, lse_a = _flash_fwd(q_a, k_a, va, sc_a, name="mla_attn")
    q_g = _norm_fwd(qg_raw, small["gqa_q_norm"], name="gqa_q_prep", rope=rope_ax)
    k_g = _norm_fwd(kg_raw, small["gqa_k_norm"], name="gqa_k_prep", rope=rope_ax)
    o_g, lse_g = _flash_fwd(q_g, k_g, vg, sc_g, name="gqa_attn")
    merged = jnp.concatenate([_unheads(o_a), _unheads(o_g)], axis=-1).astype(BF16)
    x1 = _mm(merged, W["even_w_out"], mode="nn", name="even_out", epi="add", extra=x)
    x2, mlp0 = _mlp_fwd(x1, small["mlp_norm"][0], W["mlp_w_up0"], W["mlp_w_down0"], 0)

    h1 = _norm_fwd(x2[None], small["odd_norm"], name="odd_norm")[0]
    qkv = _mm(h1, W["odd_w_qkv"], mode="nn", name="odd_qkv")
    nq, nkk = SWA_HEADS * SWA_DIM, SWA_KV * SWA_DIM
    qs_raw = _heads(qkv[:, :nq], SWA_HEADS, SWA_DIM)
    ks_raw = _heads(qkv[:, nq:nq + nkk], SWA_KV, SWA_DIM)
    vs = _heads(qkv[:, nq + nkk:], SWA_KV, SWA_DIM).astype(BF16)
    q_s = _norm_fwd(qs_raw, small["swa_q_norm"], name="swa_q_prep", rope=rope_sw)
    k_s = _norm_fwd(ks_raw, small["swa_k_norm"], name="swa_k_prep", rope=rope_sw)
    sink = small["swa_sink"].reshape(SWA_HEADS, 1, 1)
    o_s, lse_s = _swa_fwd(q_s, k_s, vs, sink, sc_s, name="swa_attn")
    o_flat = _unheads(o_s).astype(BF16)
    x3 = _mm(o_flat, W["odd_w_out"], mode="nn", name="odd_out", epi="add", extra=x2)
    x4, mlp1 = _mlp_fwd(x3, small["mlp_norm"][1], W["mlp_w_up1"], W["mlp_w_down1"], 1)

    dy, loss_sum = _loss_head(x4, target, name="loss_head")
    gW, gs = {}, {}

    dx3, dg_m1, gW["mlp_w_up1"], gW["mlp_w_down1"] = _mlp_bwd(
        x3, small["mlp_norm"][1], W["mlp_w_up1"], W["mlp_w_down1"], mlp1, dy, 1)
    d_oflat = _mm(dx3, W["odd_w_out"], mode="nt", name="odd_dout")
    gW["odd_w_out"] = _mm(o_flat, dx3, mode="tn", name="odd_dwout")
    do_s = _heads(d_oflat, SWA_HEADS, SWA_DIM)
    delta_s, dob_s = _delta(o_s, do_s, name="swa_delta")
    dq_s, dsink = _swa_dq(q_s, k_s, vs, dob_s, lse_s, delta_s, sink, sc_s, name="swa_dq")
    dk_s, dv_s = _swa_dkv(q_s, k_s, vs, dob_s, lse_s, delta_s, sc_s, name="swa_dkv")
    gs["swa_sink"] = dsink[:, 0, 0]
    dqs_raw, gs["swa_q_norm"] = _norm_bwd(qs_raw, small["swa_q_norm"], dq_s, name="swa_dq_prep", rope=rope_sw)
    dks_raw, gs["swa_k_norm"] = _norm_bwd(ks_raw, small["swa_k_norm"], dk_s, name="swa_dk_prep", rope=rope_sw)
    dqkv = jnp.concatenate([_unheads(dqs_raw), _unheads(dks_raw), _unheads(dv_s)], axis=-1).astype(BF16)
    dh1 = _mm(dqkv, W["odd_w_qkv"], mode="nt", name="odd_dh")
    gW["odd_w_qkv"] = _mm(h1, dqkv, mode="tn", name="odd_dwqkv")
    dx2, gs["odd_norm"] = _norm_bwd(x2[None], small["odd_norm"], dh1[None], name="odd_dnorm", res=dx3)
    dx2 = dx2[0]

    dx1, dg_m0, gW["mlp_w_up0"], gW["mlp_w_down0"] = _mlp_bwd(
        x1, small["mlp_norm"][0], W["mlp_w_up0"], W["mlp_w_down0"], mlp0, dx2, 0)
    gs["mlp_norm"] = jnp.stack([dg_m0, dg_m1])
    d_merged = _mm(dx1, W["even_w_out"], mode="nt", name="even_dout")
    gW["even_w_out"] = _mm(merged, dx1, mode="tn", name="even_dwout")
    na = MLA_HEADS * MLA_V
    do_a = _heads(d_merged[:, :na], MLA_HEADS, MLA_V)
    do_g = _heads(d_merged[:, na:], GQA_HEADS, GQA_DIM)
    delta_a, dob_a = _delta(o_a, do_a, name="mla_delta")
    dq_a, dk_a, dv_a = _flash_bwd(q_a, k_a, va, dob_a, lse_a, delta_a, sc_a, name="mla_attn_bwd")
    delta_g, dob_g = _delta(o_g, do_g, name="gqa_delta")
    dq_g, dk_gp, dv_gp = _flash_bwd(q_g, k_g, vg, dob_g, lse_g, delta_g, sc_g, name="gqa_attn_bwd")
    grp = GQA_HEADS // GQA_KV
    dqg_raw, gs["gqa_q_norm"] = _norm_bwd(qg_raw, small["gqa_q_norm"], dq_g, name="gqa_dq_prep", rope=rope_ax)
    dkg_raw, gs["gqa_k_norm"] = _norm_bwd(kg_raw, small["gqa_k_norm"], dk_gp, name="gqa_dk_prep", rope=rope_ax, group=grp)
    dvg = _group_sum(dv_gp, grp, name="gqa_dv_sum")
    dqa_raw, gs["mla_q_norm"] = _norm_bwd(qa_raw, small["mla_q_norm"], dq_a, name="mla_dq_prep", rope=rope_q)
    dkn_raw, gs["mla_k_nope_norm"] = _norm_bwd(kn_raw, small["mla_k_nope_norm"], dk_a[:, :, :MLA_NOPE], name="mla_dkn_prep")
    dkr_raw, gs["mla_k_rope_norm"] = _norm_bwd(kr_raw[None], small["mla_k_rope_norm"], dk_a[:, :, MLA_NOPE:],
                                               name="mla_dkr_prep", rope=rope_kr, group=MLA_HEADS)
    dkv = jnp.concatenate([dkn_raw.transpose(1, 0, 2), dv_a.transpose(1, 0, 2)], axis=-1)
    dkv = dkv.reshape(T, MLA_HEADS * kv_w).astype(BF16)
    dqa = _unheads(dqa_raw).astype(BF16)
    dckvn = _mm(dkv, W["mla_w_ukv"], mode="nt", name="mla_dckv")
    gW["mla_w_ukv"] = _mm(ckvn, dkv, mode="tn", name="mla_dwukv")
    dcqn = _mm(dqa, W["mla_w_uq"], mode="nt", name="mla_dcq")
    gW["mla_w_uq"] = _mm(cqn, dqa, mode="tn", name="mla_dwuq")
    dc_q, gs["mla_q_lat_norm"] = _norm_bwd(c_q[None], small["mla_q_lat_norm"], dcqn[None], name="q_lat_dnorm")
    dc_kv, gs["mla_kv_lat_norm"] = _norm_bwd(c_kv[None], small["mla_kv_lat_norm"], dckvn[None], name="kv_lat_dnorm")
    dproj = jnp.concatenate([dc_q[0], dc_kv[0], dkr_raw[0], _unheads(dqg_raw), _unheads(dkg_raw), _unheads(dvg)],
                            axis=-1).astype(BF16)
    dh0 = _mm(dproj, W["even_w_in"], mode="nt", name="even_dh")
    gW["even_w_in"] = _mm(h0, dproj, mode="tn", name="even_dwin")
    dx0, gs["even_norm"] = _norm_bwd(x[None], small["even_norm"], dh0[None], name="even_dnorm", res=dx1)
    gs = {k: v.reshape(-1) for k, v in gs.items()}
    return loss_sum, dx0[0], gW, gs


BIG = (("even_w_in", 2), ("mla_w_uq", 2), ("mla_w_ukv", 2), ("even_w_out", 1), ("odd_w_qkv", 2), ("odd_w_out", 1),
       ("mlp_w_up", 2), ("mlp_w_down", 1))
SMALL = ("even_norm", "mla_q_lat_norm", "mla_kv_lat_norm", "mla_q_norm", "mla_k_nope_norm", "mla_k_rope_norm",
         "gqa_q_norm", "gqa_k_norm", "odd_norm", "swa_q_norm", "swa_k_norm", "swa_sink", "mlp_norm")
PACK_QUANTUM = 2 * SUBLANES_BF16 * COMM_LANES


def _pad_to(v, n):
    return v if v.shape[-1] == n else jnp.pad(v, [(0, 0)] * (v.ndim - 1) + [(0, n - v.shape[-1])])


def _packed_len(shards):
    n = sum(int(np.prod(s.shape)) for s in shards.values())
    return n, -(-n // PACK_QUANTUM) * PACK_QUANTUM


def _full_from_chips(stacked, axis):
    out = []
    for layer in range(stacked.shape[1]):
        s = stacked[:, layer]
        if axis == 1:
            out.append(s.reshape(-1, s.shape[-1]))
        else:
            out.append(s.transpose(1, 0, 2).reshape(s.shape[1], -1))
    return out


def _chip_slices(g, axis):
    K, N = g.shape
    if axis == 1:
        return g.reshape(N_CHIPS, K // N_CHIPS, N)
    return g.reshape(K, N_CHIPS, N // N_CHIPS).transpose(1, 0, 2)


def _pack_rows(flat, rows=8):
    n = flat.shape[0]
    padded = -(-n // (rows * LANES)) * rows * LANES
    return _pad_to(flat, padded).reshape(-1, LANES)


def kernel(x, even_norm, even_w_in, mla_q_lat_norm, mla_kv_lat_norm, mla_w_uq, mla_w_ukv, mla_q_norm, mla_k_nope_norm, mla_k_rope_norm, gqa_q_norm, gqa_k_norm, even_w_out, odd_norm, odd_w_qkv, swa_q_norm, swa_k_norm, swa_sink, odd_w_out, mlp_norm, mlp_w_up, mlp_w_down, loss_target, m_even_norm, m_even_w_in, m_mla_q_lat_norm, m_mla_kv_lat_norm, m_mla_w_uq, m_mla_w_ukv, m_mla_q_norm, m_mla_k_nope_norm, m_mla_k_rope_norm, m_gqa_q_norm, m_gqa_k_norm, m_even_w_out, m_odd_norm, m_odd_w_qkv, m_swa_q_norm, m_swa_k_norm, m_swa_sink, m_odd_w_out, m_mlp_norm, m_mlp_w_up, m_mlp_w_down, v_even_norm, v_even_w_in, v_mla_q_lat_norm, v_mla_kv_lat_norm, v_mla_w_uq, v_mla_w_ukv, v_mla_q_norm, v_mla_k_nope_norm, v_mla_k_rope_norm, v_gqa_q_norm, v_gqa_k_norm, v_even_w_out, v_odd_norm, v_odd_w_qkv, v_swa_q_norm, v_swa_k_norm, v_swa_sink, v_odd_w_out, v_mlp_norm, v_mlp_w_up, v_mlp_w_down):
    w = dict(even_norm=even_norm, even_w_in=even_w_in, mla_q_lat_norm=mla_q_lat_norm, mla_kv_lat_norm=mla_kv_lat_norm,
             mla_w_uq=mla_w_uq, mla_w_ukv=mla_w_ukv, mla_q_norm=mla_q_norm, mla_k_nope_norm=mla_k_nope_norm,
             mla_k_rope_norm=mla_k_rope_norm, gqa_q_norm=gqa_q_norm, gqa_k_norm=gqa_k_norm, even_w_out=even_w_out,
             odd_norm=odd_norm, odd_w_qkv=odd_w_qkv, swa_q_norm=swa_q_norm, swa_k_norm=swa_k_norm, swa_sink=swa_sink,
             odd_w_out=odd_w_out, mlp_norm=mlp_norm, mlp_w_up=mlp_w_up, mlp_w_down=mlp_w_down)
    m = dict(even_norm=m_even_norm, even_w_in=m_even_w_in, mla_q_lat_norm=m_mla_q_lat_norm,
             mla_kv_lat_norm=m_mla_kv_lat_norm, mla_w_uq=m_mla_w_uq, mla_w_ukv=m_mla_w_ukv, mla_q_norm=m_mla_q_norm,
             mla_k_nope_norm=m_mla_k_nope_norm, mla_k_rope_norm=m_mla_k_rope_norm, gqa_q_norm=m_gqa_q_norm,
             gqa_k_norm=m_gqa_k_norm, even_w_out=m_even_w_out, odd_norm=m_odd_norm, odd_w_qkv=m_odd_w_qkv,
             swa_q_norm=m_swa_q_norm, swa_k_norm=m_swa_k_norm, swa_sink=m_swa_sink, odd_w_out=m_odd_w_out,
             mlp_norm=m_mlp_norm, mlp_w_up=m_mlp_w_up, mlp_w_down=m_mlp_w_down)
    v = dict(even_norm=v_even_norm, even_w_in=v_even_w_in, mla_q_lat_norm=v_mla_q_lat_norm,
             mla_kv_lat_norm=v_mla_kv_lat_norm, mla_w_uq=v_mla_w_uq, mla_w_ukv=v_mla_w_ukv, mla_q_norm=v_mla_q_norm,
             mla_k_nope_norm=v_mla_k_nope_norm, mla_k_rope_norm=v_mla_k_rope_norm, gqa_q_norm=v_gqa_q_norm,
             gqa_k_norm=v_gqa_k_norm, even_w_out=v_even_w_out, odd_norm=v_odd_norm, odd_w_qkv=v_odd_w_qkv,
             swa_q_norm=v_swa_q_norm, swa_k_norm=v_swa_k_norm, swa_sink=v_swa_sink, odd_w_out=v_odd_w_out,
             mlp_norm=v_mlp_norm, mlp_w_up=v_mlp_w_up, mlp_w_down=v_mlp_w_down)
    xi, yi, ci = _position()
    chip = 2 * xi + yi
    T, D = x.shape[1], x.shape[2]

    shards = {name: w[name] for name, _ in BIG}
    n_used, n_pack = _packed_len(shards)
    rows_half = n_pack // (2 * COMM_LANES)
    flat = jnp.concatenate([shards[name].astype(BF16).reshape(-1) for name, _ in BIG])
    packed = _pad_to(flat, n_pack).reshape(2, rows_half, COMM_LANES)
    gathered = _all_gather_halves(packed, name="weights_all_gather").reshape(N_CHIPS, n_pack)
    W, off = {}, 0
    for name, axis in BIG:
        shape = shards[name].shape
        n = int(np.prod(shape))
        mats = _full_from_chips(gathered[:, off:off + n].reshape((N_CHIPS,) + shape), axis)
        off += n
        if len(mats) == 1:
            W[name] = mats[0]
        else:
            for layer, mat in enumerate(mats):
                W[f"{name}{layer}"] = mat

    odd_full = jnp.zeros((N_CHIPS, D // N_CHIPS), F32).at[chip].set(jnp.where(ci == 0, 1.0, 0.0) * w["odd_norm"][0])
    odd_full = _all_reduce_small(_pack_rows(odd_full.reshape(-1)), name="odd_norm_gather").reshape(-1)[:D]
    small = {name: w[name][0] for name in SMALL if name not in ("mlp_norm", "odd_norm")}
    small["mlp_norm"] = w["mlp_norm"]
    small["odd_norm"] = odd_full

    loss_sum, grad_x, gW, gs = _local_step(x[0], loss_target[0], W, small)

    loss_local = 0.5 * loss_sum.reshape(1) / D
    small_sizes = [(name, int(gs[name].shape[0])) for name in SMALL]
    ar_in = jnp.concatenate([_pad_to(loss_local, LANES)] + [gs[name] for name in SMALL])
    ar_out = _all_reduce_small(_pack_rows(ar_in), name="small_all_reduce").reshape(-1)
    loss = ar_out[0]
    g_small, off = {}, LANES
    for name, n in small_sizes:
        g_small[name] = ar_out[off:off + n]
        off += n
    shard_d = D // N_CHIPS
    g_small["odd_norm"] = lax.dynamic_slice(g_small["odd_norm"], (chip * shard_d,), (shard_d,))

    parts = []
    for name, axis in BIG:
        if w[name].shape[0] == 1:
            parts.append(_chip_slices(gW[name], axis).reshape(N_CHIPS, -1))
        else:
            layers = [_chip_slices(gW[f"{name}{l}"], axis).reshape(N_CHIPS, -1) for l in range(w[name].shape[0])]
            parts.append(jnp.concatenate(layers, axis=1))
    g_all = _pad_to(jnp.concatenate(parts, axis=1), n_pack).reshape(N_CHIPS, 2, rows_half, COMM_LANES)
    from_sibling = _sibling_exchange(g_all, name="grad_sibling_exchange")
    pair = _pair_add(g_all, from_sibling, ci.reshape(1).astype(jnp.int32), name="grad_pair_add")
    from_chips = _chip_scatter(pair, name="grad_chip_scatter")
    reduced_half = _sum_chips(from_chips, name="grad_chip_sum")
    g_shard = _sibling_share(reduced_half, name="grad_sibling_share").reshape(n_pack)

    grads, deltas, new_m, new_v = {}, {}, {}, {}
    off = 0
    for name, _ in BIG:
        shape = shards[name].shape
        n = int(np.prod(shape))
        g = g_shard[off:off + n].reshape(shape)
        off += n
        grads[name] = g
        two_d = (shape[0] * shape[1], shape[2])
        d_, m_, v_ = _adamw(w[name].reshape(two_d), g.reshape(two_d), m[name].reshape(two_d), v[name].reshape(two_d),
                            name=f"adamw_{name}")
        deltas[name], new_m[name], new_v[name] = d_.reshape(shape), m_.reshape(shape), v_.reshape(shape)
    pack_small = lambda d: _pack_rows(jnp.concatenate([d[name].reshape(-1) for name in SMALL]))
    for name in SMALL:
        grads[name] = g_small[name].reshape(w[name].shape)
    d_, m_, v_ = _adamw(pack_small(w), pack_small(grads), pack_small(m), pack_small(v), name="adamw_small")
    d_, m_, v_ = d_.reshape(-1), m_.reshape(-1), v_.reshape(-1)
    off = 0
    for name in SMALL:
        n = int(np.prod(w[name].shape))
        deltas[name] = d_[off:off + n].reshape(w[name].shape)
        new_m[name] = m_[off:off + n].reshape(w[name].shape)
        new_v[name] = v_[off:off + n].reshape(w[name].shape)
        off += n

    order = ("even_norm", "even_w_in", "mla_q_lat_norm", "mla_kv_lat_norm", "mla_w_uq", "mla_w_ukv", "mla_q_norm",
             "mla_k_nope_norm", "mla_k_rope_norm", "gqa_q_norm", "gqa_k_norm", "even_w_out", "odd_norm", "odd_w_qkv",
             "swa_q_norm", "swa_k_norm", "swa_sink", "odd_w_out", "mlp_norm", "mlp_w_up", "mlp_w_down")
    outs = [loss, grad_x[None]]
    for group in (grads, deltas, new_m, new_v):
        outs += [group[name] for name in order]
    return tuple(outs)
```

```python
import functools
import math

import numpy as np
import jax
import jax.numpy as jnp
from jax import lax
from jax.experimental import pallas as pl
from jax.experimental.pallas import tpu as pltpu

F32 = jnp.float32
BF16 = jnp.bfloat16
MESH = pl.DeviceIdType.MESH

VMEM_BYTES_V7X = 64 * 1024 * 1024
LANES = 128
SUBLANES_BF16 = 16

GRID_W = 64
NORM_EPS = 1e-6
ROPE_THETA = 500000.0
AXIAL_THETA = 10000.0
MLA_HEADS = 8
MLA_Q_LORA = 512
MLA_KV_LORA = 256
MLA_NOPE = 128
MLA_ROPE = 64
MLA_QK = MLA_NOPE + MLA_ROPE
MLA_V = 128
GQA_HEADS = 8
GQA_KV = 2
GQA_DIM = 128
SWA_HEADS = 32
SWA_KV = 4
SWA_DIM = 64
SWA_WINDOW = 128
SWA_ROT = SWA_DIM // 4
SWA_BLOCK = 128
ADAM_LR = 0.001
ADAM_B1 = 0.9
ADAM_B2 = 0.999
ADAM_EPS = 1e-08
ADAM_WD = 0.01
ADAM_STEP = 10
N_CHIPS = 4
COMM_LANES = 1024


def _tile(dim, cap, mult=LANES):
    if dim <= cap:
        return dim
    t = (cap // mult) * mult
    while t >= mult:
        if dim % t == 0:
            return t
        t -= mult
    return dim


def _params(dims, vmem_estimate):
    limit = int(min(max(vmem_estimate * 1.25 + (4 << 20), 32 << 20), VMEM_BYTES_V7X - (6 << 20)))
    return pltpu.CompilerParams(dimension_semantics=dims, vmem_limit_bytes=limit)


def _nbytes(shape, dtype):
    return int(np.prod(shape)) * jnp.dtype(dtype).itemsize


def _mm(a, b, *, mode, name, out_dtype=F32, epi=None, extra=None, split=1, caps=(1024, 1024, 1024)):
    if mode == "nn":
        (M, K), (K2, N) = a.shape, b.shape
    elif mode == "nt":
        (M, K), (N, K2) = a.shape, b.shape
    else:
        (K, M), (K2, N) = a.shape, b.shape
    assert K == K2, (a.shape, b.shape, mode)
    assert N % split == 0
    ns = N // split
    tn, tk = _tile(ns, caps[1]), _tile(K, caps[2])
    tm = _tile(M, min(caps[0], max(LANES, caps[0] * caps[1] // tn)))
    nj_per = ns // tn
    grid = (M // tm, N // tn, K // tk)
    nk = grid[2]
    if mode == "nn":
        a_spec = pl.BlockSpec((tm, tk), lambda i, j, k: (i, k))
        b_spec = pl.BlockSpec((tk, tn), lambda i, j, k: (k, j))
        dn = (((1,), (0,)), ((), ()))
    elif mode == "nt":
        a_spec = pl.BlockSpec((tm, tk), lambda i, j, k: (i, k))
        b_spec = pl.BlockSpec((tn, tk), lambda i, j, k: (j, k))
        dn = (((1,), (1,)), ((), ()))
    else:
        a_spec = pl.BlockSpec((tk, tm), lambda i, j, k: (k, i))
        b_spec = pl.BlockSpec((tk, tn), lambda i, j, k: (k, j))
        dn = (((0,), (0,)), ((), ()))
    if split == 1:
        o_spec = pl.BlockSpec((tm, tn), lambda i, j, k: (i, j))
        o_shape = (M, N)
    else:
        o_spec = pl.BlockSpec((None, tm, tn), lambda i, j, k: (j // nj_per, i, j % nj_per))
        o_shape = (split, M, ns)
    mn_spec = pl.BlockSpec((tm, tn), lambda i, j, k: (i, j))
    in_specs, args = [a_spec, b_spec], [a, b]
    if epi in ("add", "dsqrelu"):
        in_specs.append(mn_spec)
        args.append(extra)
    if epi == "sqrelu":
        out_shape = (jax.ShapeDtypeStruct(o_shape, BF16), jax.ShapeDtypeStruct(o_shape, BF16))
        out_specs = (o_spec, o_spec)
        n_out = 2
    else:
        out_shape = jax.ShapeDtypeStruct(o_shape, out_dtype)
        out_specs = o_spec
        n_out = 1

    def body(*refs):
        a_ref, b_ref = refs[0], refs[1]
        e_ref = refs[2] if len(args) == 3 else None
        outs = refs[len(args):len(args) + n_out]
        acc_ref = refs[-1]
        k = pl.program_id(2)

        @pl.when(k == 0)
        def _():
            acc_ref[...] = jnp.zeros_like(acc_ref)

        acc_ref[...] += lax.dot_general(a_ref[...].astype(BF16), b_ref[...].astype(BF16), dn,
                                        preferred_element_type=F32)

        @pl.when(k == nk - 1)
        def _():
            acc = acc_ref[...]
            if epi is None:
                outs[0][...] = acc.astype(outs[0].dtype)
            elif epi == "add":
                outs[0][...] = (e_ref[...] + acc).astype(outs[0].dtype)
            elif epi == "sqrelu":
                r = jnp.maximum(acc, 0.0)
                outs[0][...] = acc.astype(BF16)
                outs[1][...] = (r * r).astype(BF16)
            else:
                u = e_ref[...].astype(F32)
                outs[0][...] = (acc * (2.0 * jnp.maximum(u, 0.0))).astype(outs[0].dtype)

    est = 2 * (_nbytes((tm, tk), a.dtype) + _nbytes((tk, tn), b.dtype)) + _nbytes((tm, tn), F32)
    est += 2 * n_out * _nbytes((tm, tn), out_dtype if n_out == 1 else BF16)
    if len(args) == 3:
        est += 2 * _nbytes((tm, tn), extra.dtype)
    est += 3 * _nbytes((tm, tn), F32)
    return pl.pallas_call(
        body, name=name, grid=grid, in_specs=in_specs, out_specs=out_specs, out_shape=out_shape,
        scratch_shapes=[pltpu.VMEM((tm, tn), F32)],
        compiler_params=_params(("parallel", "parallel", "arbitrary"), est),
    )(*args)


def _perm(y, p):
    hi = y.astype(BF16)
    r1 = y - hi.astype(F32)
    mid = r1.astype(BF16)
    lo = (r1 - mid.astype(F32)).astype(BF16)
    d = lambda t: jnp.dot(t, p, preferred_element_type=F32)
    return d(hi) + d(mid) + d(lo)


def _rows_tile(T, d):
    return _tile(T, 2048 if d <= 256 else 512, 128)


def _norm_fwd(x, gain, *, name, rope=None, out_dtype=BF16):
    H, T, d = x.shape
    tm = _rows_tile(T, d)
    g2 = gain.reshape(1, d).astype(F32)
    in_specs = [pl.BlockSpec((None, tm, d), lambda h, i: (h, i, 0)), pl.BlockSpec((1, d), lambda h, i: (0, 0))]
    args = [x, g2]
    if rope is not None:
        in_specs += [pl.BlockSpec((tm, d), lambda h, i: (i, 0)), pl.BlockSpec((tm, d), lambda h, i: (i, 0)),
                     pl.BlockSpec((d, d), lambda h, i: (0, 0))]
        args += list(rope)

    def body(*refs):
        x_ref, g_ref = refs[0], refs[1]
        o_ref = refs[-1]
        xv = x_ref[...]
        y = xv * lax.rsqrt(jnp.mean(xv * xv, axis=-1, keepdims=True) + NORM_EPS)
        y = y * g_ref[...]
        if rope is not None:
            c_ref, s_ref, p_ref = refs[2], refs[3], refs[4]
            y = y * c_ref[...] + _perm(y, p_ref[...]) * s_ref[...]
        o_ref[...] = y.astype(o_ref.dtype)

    est = 2 * (_nbytes((tm, max(d, LANES)), F32) * (3 if rope is not None else 1) + _nbytes((tm, max(d, LANES)), out_dtype))
    est += 6 * _nbytes((tm, max(d, LANES)), F32)
    return pl.pallas_call(
        body, name=name, grid=(H, T // tm), in_specs=in_specs,
        out_specs=pl.BlockSpec((None, tm, d), lambda h, i: (h, i, 0)),
        out_shape=jax.ShapeDtypeStruct((H, T, d), out_dtype),
        compiler_params=_params(("parallel", "parallel"), est),
    )(*args)


def _norm_bwd(x, gain, dy, *, name, rope=None, group=1, res=None, out_dtype=F32, dy_scale=None):
    H, T, d = x.shape
    assert dy.shape == (H * group, T, d), (dy.shape, x.shape, group)
    tm = _rows_tile(T, d)
    g2 = gain.reshape(1, d).astype(F32)
    in_specs = [pl.BlockSpec((None, tm, d), lambda h, i: (h, i, 0)), pl.BlockSpec((1, d), lambda h, i: (0, 0)),
                pl.BlockSpec((group, tm, d), lambda h, i: (h, i, 0))]
    args = [x, g2, dy]
    if rope is not None:
        in_specs += [pl.BlockSpec((tm, d), lambda h, i: (i, 0)), pl.BlockSpec((tm, d), lambda h, i: (i, 0)),
                     pl.BlockSpec((d, d), lambda h, i: (0, 0))]
        args += list(rope)
    if res is not None:
        assert H == 1
        in_specs.append(pl.BlockSpec((tm, d), lambda h, i: (i, 0)))
        args.append(res)
    n_in = len(args)

    def body(*refs):
        x_ref, g_ref, dy_ref = refs[0], refs[1], refs[2]
        dx_ref, dg_ref = refs[n_in], refs[n_in + 1]
        first = (pl.program_id(0) == 0) & (pl.program_id(1) == 0)

        @pl.when(first)
        def _():
            dg_ref[...] = jnp.zeros_like(dg_ref)

        dyv = dy_ref[0].astype(F32)
        for g in range(1, group):
            dyv = dyv + dy_ref[g].astype(F32)
        if dy_scale is not None:
            dyv = dyv * dy_scale
        pos = 3
        if rope is not None:
            c_ref, s_ref, p_ref = refs[3], refs[4], refs[5]
            pos = 6
            dyv = dyv * c_ref[...] + _perm(dyv * s_ref[...], p_ref[...])
        xv = x_ref[...]
        r = lax.rsqrt(jnp.mean(xv * xv, axis=-1, keepdims=True) + NORM_EPS)
        xhat = xv * r
        dg_ref[...] += jnp.sum(dyv * xhat, axis=0, keepdims=True)
        dxh = dyv * g_ref[...]
        dx = r * (dxh - xhat * jnp.mean(dxh * xhat, axis=-1, keepdims=True))
        if res is not None:
            dx = dx + refs[pos][...]
        dx_ref[...] = dx.astype(dx_ref.dtype)

    wide = max(d, LANES)
    est = 2 * _nbytes((tm, wide), F32) * (2 + group + (2 if rope is not None else 0) + (1 if res is not None else 0))
    est += 8 * _nbytes((tm, wide), F32)
    return pl.pallas_call(
        body, name=name, grid=(H, T // tm), in_specs=in_specs,
        out_specs=(pl.BlockSpec((None, tm, d), lambda h, i: (h, i, 0)), pl.BlockSpec((1, d), lambda h, i: (0, 0))),
        out_shape=(jax.ShapeDtypeStruct((H, T, d), out_dtype), jax.ShapeDtypeStruct((1, d), F32)),
        compiler_params=_params(("arbitrary", "arbitrary"), est),
    )(*args)


def _group_sum(x, group, *, name):
    HG, T, d = x.shape
    H = HG // group
    tm = _rows_tile(T, d)

    def body(x_ref, o_ref):
        acc = x_ref[0]
        for g in range(1, group):
            acc = acc + x_ref[g]
        o_ref[...] = acc

    est = 2 * (group + 1) * _nbytes((tm, max(d, LANES)), F32)
    return pl.pallas_call(
        body, name=name, grid=(H, T // tm),
        in_specs=[pl.BlockSpec((group, tm, d), lambda h, i: (h, i, 0))],
        out_specs=pl.BlockSpec((None, tm, d), lambda h, i: (h, i, 0)),
        out_shape=jax.ShapeDtypeStruct((H, T, d), F32),
        compiler_params=_params(("parallel", "parallel"), est),
    )(x)


def _delta(o, do, *, name):
    H, T, d = o.shape
    tm = _rows_tile(T, d)

    def body(o_ref, do_ref, dl_ref, dob_ref):
        dov = do_ref[...]
        dl = jnp.sum(o_ref[...] * dov, axis=-1, keepdims=True)
        dl_ref[...] = jnp.broadcast_to(dl, (tm, LANES))
        dob_ref[...] = dov.astype(BF16)

    spec = pl.BlockSpec((None, tm, d), lambda h, i: (h, i, 0))
    est = 2 * (3 * _nbytes((tm, max(d, LANES)), F32) + _nbytes((tm, LANES), F32))
    return pl.pallas_call(
        body, name=name, grid=(H, T // tm), in_specs=[spec, spec],
        out_specs=(pl.BlockSpec((None, tm, LANES), lambda h, i: (h, i, 0)), spec),
        out_shape=(jax.ShapeDtypeStruct((H, T, LANES), F32), jax.ShapeDtypeStruct((H, T, d), BF16)),
        compiler_params=_params(("parallel", "parallel"), est),
    )(o, do)


NT_DIMS = (((1,), (1,)), ((), ()))
TN_DIMS = (((0,), (0,)), ((), ()))
LOG2E = math.log2(math.e)
FLASH_CHUNK = 512


def _flash_fwd(q, k, v, scale, *, name):
    H, T, dk = q.shape
    Hkv, _, dv = v.shape
    G = H // Hkv
    tq = tk = tc = _tile(T, 1024)
    nk = T // tk
    c2 = scale * LOG2E

    def body(q_ref, k_ref, v_ref, o_ref, lse_ref, m_ref, l_ref, acc_ref):
        ki = pl.program_id(2)

        @pl.when(ki == 0)
        def _():
            m_ref[...] = jnp.full_like(m_ref, -jnp.inf)
            l_ref[...] = jnp.zeros_like(l_ref)
            acc_ref[...] = jnp.zeros_like(acc_ref)

        qv = q_ref[...]
        m, l, acc = m_ref[...], l_ref[...], acc_ref[...]
        for c in range(tk // tc):
            kc, vc = k_ref[c * tc:(c + 1) * tc, :], v_ref[c * tc:(c + 1) * tc, :]
            s = lax.dot_general(qv, kc, NT_DIMS, preferred_element_type=F32)
            m_new = jnp.maximum(m, jnp.max(s, axis=-1, keepdims=True))
            alpha = jnp.exp2((m - m_new) * c2)
            p = jnp.exp2((s - m_new) * c2)
            l = alpha * l + jnp.sum(p, axis=-1, keepdims=True)
            acc = alpha * acc + jnp.dot(p.astype(BF16), vc, preferred_element_type=F32)
            m = m_new
        m_ref[...], l_ref[...], acc_ref[...] = m, l, acc

        @pl.when(ki == nk - 1)
        def _():
            o_ref[...] = acc / l
            lse_ref[...] = jnp.broadcast_to(m * scale + jnp.log(l), (tq, LANES))

    est = 2 * (_nbytes((tq, dk), BF16) + _nbytes((tk, dk + dv), BF16) + _nbytes((tq, dv + LANES), F32))
    est += 8 * _nbytes((tq, tc), F32) + 3 * _nbytes((tq, dv + 2 * LANES), F32)
    return pl.pallas_call(
        body, name=name, grid=(H, T // tq, nk),
        in_specs=[pl.BlockSpec((None, tq, dk), lambda h, i, j: (h, i, 0)),
                  pl.BlockSpec((None, tk, dk), lambda h, i, j: (h // G, j, 0)),
                  pl.BlockSpec((None, tk, dv), lambda h, i, j: (h // G, j, 0))],
        out_specs=(pl.BlockSpec((None, tq, dv), lambda h, i, j: (h, i, 0)),
                   pl.BlockSpec((None, tq, LANES), lambda h, i, j: (h, i, 0))),
        out_shape=(jax.ShapeDtypeStruct((H, T, dv), F32), jax.ShapeDtypeStruct((H, T, LANES), F32)),
        scratch_shapes=[pltpu.VMEM((tq, 1), F32), pltpu.VMEM((tq, 1), F32), pltpu.VMEM((tq, dv), F32)],
        compiler_params=_params(("parallel", "parallel", "arbitrary"), est),
    )(q, k, v)


def _flash_bwd(q, k, v, do, lse, delta, scale, *, name):
    H, T, dk = q.shape
    Hkv, _, dv = v.shape
    G = H // Hkv
    tq, tk = _tile(T, 512), _tile(T, 1024)
    tc = _tile(tk, FLASH_CHUNK)
    c2 = scale * LOG2E

    def body(q_ref, k_ref, v_ref, do_ref, lse_ref, dl_ref, dq_ref, dk_ref, dv_ref):
        ki, qi = pl.program_id(1), pl.program_id(2)
        rows = pl.ds(pl.multiple_of(qi * tq, tq), tq)

        @pl.when(qi == 0)
        def _():
            dk_ref[...] = jnp.zeros_like(dk_ref)
            dv_ref[...] = jnp.zeros_like(dv_ref)

        @pl.when(ki == 0)
        def _():
            dq_ref[rows, :] = jnp.zeros((tq, dk), F32)

        qv, dov = q_ref[...], do_ref[...]
        lse2 = lse_ref[:, :1] * LOG2E
        dl = dl_ref[:, :1]
        dq_c = jnp.zeros((tq, dk), F32)
        for c in range(tk // tc):
            ks = slice(c * tc, (c + 1) * tc)
            kc, vc = k_ref[ks, :], v_ref[ks, :]
            s = lax.dot_general(qv, kc, NT_DIMS, preferred_element_type=F32)
            p = jnp.exp2(s * c2 - lse2)
            dp = lax.dot_general(dov, vc, NT_DIMS, preferred_element_type=F32)
            ds = (p * (dp - dl)).astype(BF16)
            dv_ref[ks, :] += lax.dot_general(p.astype(BF16), dov, TN_DIMS, preferred_element_type=F32)
            dk_ref[ks, :] += lax.dot_general(ds, qv, TN_DIMS, preferred_element_type=F32)
            dq_c = dq_c + jnp.dot(ds, kc, preferred_element_type=F32)
        dq_ref[rows, :] += dq_c

    est = 2 * (_nbytes((tq, dk + dv), BF16) + _nbytes((tk, dk + dv), BF16) + 2 * _nbytes((tq, LANES), F32))
    est += 2 * (_nbytes((T, dk), F32) + _nbytes((tk, dk + dv), F32)) + 10 * _nbytes((tq, tc), F32)
    return pl.pallas_call(
        body, name=name, grid=(H, T // tk, T // tq),
        in_specs=[pl.BlockSpec((None, tq, dk), lambda h, j, i: (h, i, 0)),
                  pl.BlockSpec((None, tk, dk), lambda h, j, i: (h // G, j, 0)),
                  pl.BlockSpec((None, tk, dv), lambda h, j, i: (h // G, j, 0)),
                  pl.BlockSpec((None, tq, dv), lambda h, j, i: (h, i, 0)),
                  pl.BlockSpec((None, tq, LANES), lambda h, j, i: (h, i, 0)),
                  pl.BlockSpec((None, tq, LANES), lambda h, j, i: (h, i, 0))],
        out_specs=(pl.BlockSpec((None, T, dk), lambda h, j, i: (h, 0, 0)),
                   pl.BlockSpec((None, tk, dk), lambda h, j, i: (h, j, 0)),
                   pl.BlockSpec((None, tk, dv), lambda h, j, i: (h, j, 0))),
        out_shape=(jax.ShapeDtypeStruct((H, T, dk), F32), jax.ShapeDtypeStruct((H, T, dk), F32),
                   jax.ShapeDtypeStruct((H, T, dv), F32)),
        compiler_params=_params(("arbitrary", "arbitrary", "arbitrary"), est),
    )(q, k, v, do, lse, delta)


def _swa_specs(G, d, n_blocks, lanes):
    B = SWA_BLOCK
    prev = lambda j, i: (j, jnp.maximum(i - 1, 0), 0)
    cur = lambda j, i: (j, i, 0)
    nxt = lambda j, i: (j, jnp.minimum(i + 1, n_blocks - 1), 0)
    q_specs = [pl.BlockSpec((G, B, lanes), m) for m in (prev, cur, nxt)]
    kv_specs = [pl.BlockSpec((None, B, d), m) for m in (prev, cur, nxt)]
    return q_specs, kv_specs, cur


def _swa_fwd(q, k, v, sink, scale, *, name):
    Hq, T, d = q.shape
    Hkv = k.shape[0]
    G = Hq // Hkv
    B = SWA_BLOCK
    nb = T // B
    _, kv_specs, cur = _swa_specs(G, d, nb, d)

    def body(q_ref, k0, k1, k2, v0, v1, v2, sink_ref, o_ref, lse_ref):
        i = pl.program_id(1)
        qv = q_ref[...].reshape(G * B, d)
        kv = jnp.concatenate([k0[...], k1[...], k2[...]], axis=0)
        vv = jnp.concatenate([v0[...], v1[...], v2[...]], axis=0)
        s = lax.dot_general(qv, kv, NT_DIMS, preferred_element_type=F32) * scale
        row = lax.broadcasted_iota(jnp.int32, (G * B, 3 * B), 0) & (B - 1)
        col = lax.broadcasted_iota(jnp.int32, (G * B, 3 * B), 1)
        kpos = (i - 1) * B + col
        valid = (col >= row) & (col <= row + 2 * SWA_WINDOW) & (kpos >= 0) & (kpos < T)
        s = jnp.where(valid, s, -jnp.inf).reshape(G, B, 3 * B)
        sk = sink_ref[...]
        m = jnp.maximum(jnp.max(s, axis=-1, keepdims=True), sk)
        p = jnp.exp(s - m)
        denom = jnp.sum(p, axis=-1, keepdims=True) + jnp.exp(sk - m)
        pn = (p / denom).reshape(G * B, 3 * B).astype(BF16)
        o_ref[...] = jnp.dot(pn, vv, preferred_element_type=F32).reshape(G, B, d)
        lse_ref[...] = jnp.broadcast_to(m + jnp.log(denom), (G, B, LANES))

    est = 2 * (_nbytes((G, B, LANES), BF16) + 6 * _nbytes((B, LANES), BF16) + 2 * _nbytes((G, B, LANES), F32))
    est += 8 * _nbytes((G * B, 3 * B), F32)
    return pl.pallas_call(
        body, name=name, grid=(Hkv, nb),
        in_specs=[pl.BlockSpec((G, B, d), cur)] + kv_specs + kv_specs + [pl.BlockSpec((G, 1, 1), lambda j, i: (j, 0, 0))],
        out_specs=(pl.BlockSpec((G, B, d), cur), pl.BlockSpec((G, B, LANES), cur)),
        out_shape=(jax.ShapeDtypeStruct((Hq, T, d), F32), jax.ShapeDtypeStruct((Hq, T, LANES), F32)),
        compiler_params=_params(("parallel", "parallel"), est),
    )(q, k, k, k, v, v, v, sink)


def _swa_dq(q, k, v, do, lse, delta, sink, scale, *, name):
    Hq, T, d = q.shape
    Hkv = k.shape[0]
    G = Hq // Hkv
    B = SWA_BLOCK
    nb = T // B
    _, kv_specs, cur = _swa_specs(G, d, nb, d)

    def body(q_ref, do_ref, lse_ref, dl_ref, k0, k1, k2, v0, v1, v2, sink_ref, dq_ref, dsink_ref):
        i = pl.program_id(1)
        qv = q_ref[...].reshape(G * B, d)
        dov = do_ref[...].reshape(G * B, d)
        kv = jnp.concatenate([k0[...], k1[...], k2[...]], axis=0)
        vv = jnp.concatenate([v0[...], v1[...], v2[...]], axis=0)
        s = lax.dot_general(qv, kv, NT_DIMS, preferred_element_type=F32) * scale
        row = lax.broadcasted_iota(jnp.int32, (G * B, 3 * B), 0) & (B - 1)
        col = lax.broadcasted_iota(jnp.int32, (G * B, 3 * B), 1)
        kpos = (i - 1) * B + col
        valid = (col >= row) & (col <= row + 2 * SWA_WINDOW) & (kpos >= 0) & (kpos < T)
        lse = lse_ref[:, :, :1]
        dl = dl_ref[:, :, :1]
        s = jnp.where(valid, s, -jnp.inf).reshape(G, B, 3 * B)
        p = jnp.exp(s - lse)
        dp = lax.dot_general(dov, vv, NT_DIMS, preferred_element_type=F32).reshape(G, B, 3 * B)
        ds = (p * (dp - dl) * scale).reshape(G * B, 3 * B).astype(BF16)
        dq_ref[...] = jnp.dot(ds, kv, preferred_element_type=F32).reshape(G, B, d)
        dsk = -jnp.sum(jnp.exp(sink_ref[...] - lse) * dl, axis=1, keepdims=True)

        @pl.when(i == 0)
        def _():
            dsink_ref[...] = jnp.zeros_like(dsink_ref)

        dsink_ref[...] += jnp.broadcast_to(dsk, (G, 1, LANES))

    est = 2 * (2 * _nbytes((G, B, LANES), BF16) + 6 * _nbytes((B, LANES), BF16) + 3 * _nbytes((G, B, LANES), F32))
    est += 8 * _nbytes((G * B, 3 * B), F32)
    return pl.pallas_call(
        body, name=name, grid=(Hkv, nb),
        in_specs=[pl.BlockSpec((G, B, d), cur), pl.BlockSpec((G, B, d), cur), pl.BlockSpec((G, B, LANES), cur),
                  pl.BlockSpec((G, B, LANES), cur)] + kv_specs + kv_specs
                 + [pl.BlockSpec((G, 1, 1), lambda j, i: (j, 0, 0))],
        out_specs=(pl.BlockSpec((G, B, d), cur), pl.BlockSpec((G, 1, LANES), lambda j, i: (j, 0, 0))),
        out_shape=(jax.ShapeDtypeStruct((Hq, T, d), F32), jax.ShapeDtypeStruct((Hq, 1, LANES), F32)),
        compiler_params=_params(("arbitrary", "arbitrary"), est),
    )(q, do, lse, delta, k, k, k, v, v, v, sink)


def _swa_dkv(q, k, v, do, lse, delta, scale, *, name):
    Hq, T, d = q.shape
    Hkv = k.shape[0]
    G = Hq // Hkv
    B = SWA_BLOCK
    nb = T // B
    q_specs, _, cur = _swa_specs(G, d, nb, d)
    l_specs, _, _ = _swa_specs(G, d, nb, LANES)

    def body(k_ref, v_ref, q0, q1, q2, d0, d1, d2, l0, l1, l2, e0, e1, e2, dk_ref, dv_ref):
        b = pl.program_id(1)
        kv, vv = k_ref[...], v_ref[...]
        dk_acc = jnp.zeros((B, d), F32)
        dv_acc = jnp.zeros((B, d), F32)
        for part, (q_ref, do_ref, lse_ref, dl_ref) in enumerate(((q0, d0, l0, e0), (q1, d1, l1, e1), (q2, d2, l2, e2))):
            qv = q_ref[...].reshape(G * B, d)
            dov = do_ref[...].reshape(G * B, d)
            s = lax.dot_general(qv, kv, NT_DIMS, preferred_element_type=F32) * scale
            row = lax.broadcasted_iota(jnp.int32, (G * B, B), 0) & (B - 1)
            col = lax.broadcasted_iota(jnp.int32, (G * B, B), 1)
            qpos = (b + part - 1) * B + row
            diff = qpos - (b * B + col)
            valid = (diff >= -SWA_WINDOW) & (diff <= SWA_WINDOW) & (qpos >= 0) & (qpos < T)
            lse = lse_ref[:, :, :1].reshape(G * B, 1)
            dl = dl_ref[:, :, :1].reshape(G * B, 1)
            p = jnp.exp(jnp.where(valid, s, -jnp.inf) - lse)
            dp = lax.dot_general(dov, vv, NT_DIMS, preferred_element_type=F32)
            ds = (p * (dp - dl) * scale).astype(BF16)
            dv_acc = dv_acc + lax.dot_general(p.astype(BF16), dov, TN_DIMS, preferred_element_type=F32)
            dk_acc = dk_acc + lax.dot_general(ds, qv, TN_DIMS, preferred_element_type=F32)
        dk_ref[...] = dk_acc
        dv_ref[...] = dv_acc

    est = 2 * (6 * _nbytes((G, B, LANES), BF16) + 6 * _nbytes((G, B, LANES), F32) + 4 * _nbytes((B, LANES), F32))
    est += 10 * _nbytes((G * B, B), F32)
    kspec = pl.BlockSpec((None, B, d), cur)
    return pl.pallas_call(
        body, name=name, grid=(Hkv, nb),
        in_specs=[kspec, kspec] + q_specs + q_specs + l_specs + l_specs,
        out_specs=(kspec, kspec),
        out_shape=(jax.ShapeDtypeStruct((Hkv, T, d), F32), jax.ShapeDtypeStruct((Hkv, T, d), F32)),
        compiler_params=_params(("parallel", "parallel"), est),
    )(k, v, q, q, q, do, do, do, lse, lse, lse, delta, delta, delta)


def _loss_head(y, target, *, name):
    T, D = y.shape
    tm = _tile(T, 512)

    def body(y_ref, t_ref, dy_ref, s_ref):
        @pl.when(pl.program_id(0) == 0)
        def _():
            s_ref[...] = jnp.zeros_like(s_ref)

        e = y_ref[...] - t_ref[...]
        dy_ref[...] = e / D
        s_ref[...] += jnp.sum(jnp.sum(e * e, axis=-1, keepdims=True), axis=0, keepdims=True)

    spec = pl.BlockSpec((tm, D), lambda i: (i, 0))
    return pl.pallas_call(
        body, name=name, grid=(T // tm,), in_specs=[spec, spec],
        out_specs=(spec, pl.BlockSpec((1, 1), lambda i: (0, 0))),
        out_shape=(jax.ShapeDtypeStruct((T, D), F32), jax.ShapeDtypeStruct((1, 1), F32)),
        compiler_params=_params(("arbitrary",), 8 * _nbytes((tm, D), F32)),
    )(y, target)


def _adamw(w, g, m, v, *, name):
    R, C = w.shape
    tr = _tile(R, max(8, (1 << 19) // max(C, LANES) // 8 * 8), 8)

    def body(w_ref, g_ref, m_ref, v_ref, d_ref, nm_ref, nv_ref):
        gv = g_ref[...]
        nm = ADAM_B1 * m_ref[...] + (1.0 - ADAM_B1) * gv
        nv = ADAM_B2 * v_ref[...] + (1.0 - ADAM_B2) * jnp.square(gv)
        m_hat = nm / (1.0 - ADAM_B1 ** ADAM_STEP)
        v_hat = nv / (1.0 - ADAM_B2 ** ADAM_STEP)
        d_ref[...] = -ADAM_LR * (m_hat / (jnp.sqrt(v_hat) + ADAM_EPS) + ADAM_WD * w_ref[...])
        nm_ref[...] = nm
        nv_ref[...] = nv

    spec = pl.BlockSpec((tr, C), lambda i: (i, 0))
    sds = jax.ShapeDtypeStruct((R, C), F32)
    return pl.pallas_call(
        body, name=name, grid=(R // tr,), in_specs=[spec] * 4, out_specs=(spec,) * 3, out_shape=(sds,) * 3,
        compiler_params=_params(("parallel",), 16 * _nbytes((tr, max(C, LANES)), F32)),
    )(w, g, m, v)


def _comm_rows_tile(R, L):
    return _tile(R, max(SUBLANES_BF16, (1 << 19) // L // SUBLANES_BF16 * SUBLANES_BF16), SUBLANES_BF16)


def _pair_add(g, recv, c_idx, *, name):
    _, _, R, L = g.shape
    tr = _comm_rows_tile(R, L)

    def body(c_ref, g_ref, r_ref, o_ref):
        o_ref[...] = (g_ref[...] + r_ref[...]).astype(BF16)

    grid_spec = pltpu.PrefetchScalarGridSpec(
        num_scalar_prefetch=1, grid=(N_CHIPS, R // tr),
        in_specs=[pl.BlockSpec((None, None, tr, L), lambda j, i, c: (j, c[0], i, 0)),
                  pl.BlockSpec((None, tr, L), lambda j, i, c: (j, i, 0))],
        out_specs=pl.BlockSpec((None, tr, L), lambda j, i, c: (j, i, 0)))
    return pl.pallas_call(
        body, name=name, grid_spec=grid_spec, out_shape=jax.ShapeDtypeStruct((N_CHIPS, R, L), BF16),
        compiler_params=_params(("parallel", "parallel"), 8 * _nbytes((tr, L), F32)),
    )(c_idx, g, recv)


def _sum_chips(q, c_idx, *, name):
    _, R, L = q.shape
    tr = _comm_rows_tile(R, L)

    def body(c_ref, q_ref, o_ref):
        acc = q_ref[0].astype(F32)
        for j in range(1, N_CHIPS):
            acc = acc + q_ref[j].astype(F32)
        o_ref[...] = acc

    grid_spec = pltpu.PrefetchScalarGridSpec(
        num_scalar_prefetch=1, grid=(R // tr,),
        in_specs=[pl.BlockSpec((N_CHIPS, tr, L), lambda i, c: (0, i, 0))],
        out_specs=pl.BlockSpec((None, tr, L), lambda i, c: (c[0], i, 0)))
    return pl.pallas_call(
        body, name=name, grid_spec=grid_spec, out_shape=jax.ShapeDtypeStruct((2, R, L), F32),
        compiler_params=_params(("parallel",), 10 * _nbytes((tr, L), F32)),
    )(c_idx, q)


HBM_SPEC = pl.BlockSpec(memory_space=pltpu.HBM)


def _position():
    return lax.axis_index("x"), lax.axis_index("y"), lax.axis_index("c")


def _other_chips(x, y):
    return [(1 - x, y), (x, 1 - y), (1 - x, 1 - y)]


AG_COPIES = 7


def _all_gather_halves(ws, *, name):
    n = len(ws)

    def body(*refs):
        w_refs, out_refs = refs[:n], refs[n:2 * n]
        send_sems, recv_sems, local_sems = refs[2 * n:]
        x, y, c = _position()
        me, sibling = (x, y, c), (x, y, 1 - c)
        chips = _other_chips(x, y)

        def copy(i, k, block, to, src=None):
            px, py, pc = block
            slot = out_refs[i].at[4 * px + 2 * py + pc]
            return pltpu.make_async_remote_copy(
                src_ref=slot if src is None else src, dst_ref=slot, send_sem=send_sems.at[AG_COPIES * i + k],
                recv_sem=recv_sems.at[AG_COPIES * i + k], device_id=to, device_id_type=MESH)

        started, mine = [], []
        for i in range(n):
            own = w_refs[i].at[c]
            mine.append(pltpu.make_async_copy(own, out_refs[i].at[4 * x + 2 * y + c], local_sems.at[i]))
            mine[-1].start()
            first = [copy(i, 0, me, sibling, src=own)]
            first += [copy(i, 1 + j, me, (*chip, c), src=own) for j, chip in enumerate(chips)]
            for cp in first:
                cp.start()
            started += first
        for i in range(n):
            for j, chip in enumerate(chips):
                copy(i, 1 + j, (*chip, c), me).wait_recv()
                passed = copy(i, 4 + j, (*chip, c), sibling)
                passed.start()
                started.append(passed)
        for i in range(n):
            copy(i, 0, sibling, me).wait_recv()
            for j, chip in enumerate(chips):
                copy(i, 4 + j, (*chip, 1 - c), me).wait_recv()
        for cp in started:
            cp.wait_send()
        for cp in mine:
            cp.wait()

    return pl.pallas_call(
        body, name=name, in_specs=[HBM_SPEC] * n, out_specs=[HBM_SPEC] * n,
        out_shape=[jax.ShapeDtypeStruct((2 * N_CHIPS,) + w.shape[1:], w.dtype) for w in ws],
        scratch_shapes=[pltpu.SemaphoreType.DMA((AG_COPIES * n,)), pltpu.SemaphoreType.DMA((AG_COPIES * n,)),
                        pltpu.SemaphoreType.DMA((n,))],
    )(*ws)


def _sibling_exchange(gs, *, name):
    n = len(gs)

    def body(*refs):
        g_refs, out_refs = refs[:n], refs[n:2 * n]
        send_sems, recv_sems = refs[2 * n:]
        x, y, c = _position()
        copies = [pltpu.make_async_remote_copy(
            src_ref=g_refs[i].at[j, 1 - c], dst_ref=out_refs[i].at[j], send_sem=send_sems.at[N_CHIPS * i + j],
            recv_sem=recv_sems.at[N_CHIPS * i + j], device_id=(x, y, 1 - c), device_id_type=MESH)
            for i in range(n) for j in range(N_CHIPS)]
        for cp in copies:
            cp.start()
        for cp in copies:
            cp.wait()

    return pl.pallas_call(
        body, name=name, in_specs=[HBM_SPEC] * n, out_specs=[HBM_SPEC] * n,
        out_shape=[jax.ShapeDtypeStruct((N_CHIPS,) + g.shape[2:], g.dtype) for g in gs],
        scratch_shapes=[pltpu.SemaphoreType.DMA((N_CHIPS * n,)), pltpu.SemaphoreType.DMA((N_CHIPS * n,))],
    )(*gs)


def _chip_scatter(ps, *, name):
    n = len(ps)
    others = N_CHIPS - 1

    def body(*refs):
        p_refs, q_refs = refs[:n], refs[n:2 * n]
        send_sems, recv_sems, local_sems = refs[2 * n:]
        x, y, c = _position()
        me = 2 * x + y
        chips = _other_chips(x, y)

        def copy(i, k, chip, src_slot, dst_slot):
            return pltpu.make_async_remote_copy(
                src_ref=p_refs[i].at[src_slot], dst_ref=q_refs[i].at[dst_slot], send_sem=send_sems.at[others * i + k],
                recv_sem=recv_sems.at[others * i + k], device_id=(*chip, c), device_id_type=MESH)

        local = [pltpu.make_async_copy(p_refs[i].at[me], q_refs[i].at[me], local_sems.at[i]) for i in range(n)]
        sends = [copy(i, k, chip, 2 * chip[0] + chip[1], me) for i in range(n) for k, chip in enumerate(chips)]
        for cp in local + sends:
            cp.start()
        for i in range(n):
            for k, chip in enumerate(chips):
                copy(i, k, chip, me, 2 * chip[0] + chip[1]).wait_recv()
        for cp in sends:
            cp.wait_send()
        for cp in local:
            cp.wait()

    return pl.pallas_call(
        body, name=name, in_specs=[HBM_SPEC] * n, out_specs=[HBM_SPEC] * n,
        out_shape=[jax.ShapeDtypeStruct(p.shape, p.dtype) for p in ps],
        scratch_shapes=[pltpu.SemaphoreType.DMA((others * n,)), pltpu.SemaphoreType.DMA((others * n,)),
                        pltpu.SemaphoreType.DMA((n,))],
    )(*ps)


def _sibling_share(fs, *, name):
    n = len(fs)

    def body(*refs):
        in_refs, out_refs = refs[:n], refs[n:2 * n]
        send_sems, recv_sems = refs[2 * n:]
        x, y, c = _position()

        def copy(i, half):
            return pltpu.make_async_remote_copy(
                src_ref=in_refs[i].at[half], dst_ref=out_refs[i].at[half], send_sem=send_sems.at[i],
                recv_sem=recv_sems.at[i], device_id=(x, y, 1 - c), device_id_type=MESH)

        sends = [copy(i, c) for i in range(n)]
        for cp in sends:
            cp.start()
        for i in range(n):
            copy(i, 1 - c).wait_recv()
        for cp in sends:
            cp.wait_send()

    return pl.pallas_call(
        body, name=name, in_specs=[HBM_SPEC] * n, out_specs=[HBM_SPEC] * n,
        out_shape=[jax.ShapeDtypeStruct(f.shape, f.dtype) for f in fs],
        input_output_aliases={i: i for i in range(n)},
        scratch_shapes=[pltpu.SemaphoreType.DMA((n,)), pltpu.SemaphoreType.DMA((n,))],
    )(*fs)


def _all_reduce_small(s, *, name):
    R, L = s.shape
    n_dev = 2 * N_CHIPS

    def body(s_ref, out_ref, buf, send_sems, recv_sems, local_sem):
        x, y, c = _position()
        me, sibling = (x, y, c), (x, y, 1 - c)
        chips = _other_chips(x, y)

        def slot(px, py, pc):
            return buf.at[4 * px + 2 * py + pc]

        def copy(k, block, to, src=None):
            return pltpu.make_async_remote_copy(
                src_ref=slot(*block) if src is None else src, dst_ref=slot(*block),
                send_sem=send_sems.at[k], recv_sem=recv_sems.at[k], device_id=to, device_id_type=MESH)

        mine = pltpu.make_async_copy(s_ref, slot(*me), local_sem)
        mine.start()
        first = [copy(0, me, sibling, src=s_ref)]
        first += [copy(1 + j, me, (*chip, c), src=s_ref) for j, chip in enumerate(chips)]
        for cp in first:
            cp.start()
        passed = [copy(4 + j, (*chip, c), sibling) for j, chip in enumerate(chips)]
        for j, chip in enumerate(chips):
            copy(1 + j, (*chip, c), me).wait_recv()
            passed[j].start()
        copy(0, sibling, me).wait_recv()
        for j, chip in enumerate(chips):
            copy(4 + j, (*chip, 1 - c), me).wait_recv()
        for cp in first + passed:
            cp.wait_send()
        mine.wait()
        acc = buf[0]
        for j in range(1, n_dev):
            acc = acc + buf[j]
        out_ref[...] = acc

    vmem = pl.BlockSpec(memory_space=pltpu.VMEM)
    return pl.pallas_call(
        body, name=name, in_specs=[vmem], out_specs=vmem, out_shape=jax.ShapeDtypeStruct((R, L), F32),
        scratch_shapes=[pltpu.VMEM((n_dev, R, L), F32), pltpu.SemaphoreType.DMA((7,)), pltpu.SemaphoreType.DMA((7,)),
                        pltpu.SemaphoreType.DMA],
    )(s)


def _rope_cos_sin(pos, dim, theta):
    inv = jnp.float32(theta) ** (-jnp.arange(0, dim, 2, dtype=F32) / dim)
    ang = pos.astype(F32)[:, None] * inv[None, :]
    return jnp.cos(ang), jnp.sin(ang)


def _rope_tables(T, d, segments):
    C = jnp.ones((T, d), F32)
    S = jnp.zeros((T, d), F32)
    P = np.zeros((d, d), np.float32)
    for start, size, cos, sin in segments:
        half = size // 2
        C = C.at[:, start:start + half].set(cos).at[:, start + half:start + size].set(cos)
        S = S.at[:, start:start + half].set(-sin).at[:, start + half:start + size].set(sin)
        for p in range(half):
            P[start + half + p, start + p] = 1.0
            P[start + p, start + half + p] = 1.0
    return C, S, jnp.asarray(P, BF16)


def _heads(t, H, d):
    return t.reshape(t.shape[0], H, d).transpose(1, 0, 2)


def _unheads(t):
    H, T, d = t.shape
    return t.transpose(1, 0, 2).reshape(T, H * d)


def _dw(a, b, *, name, axis):
    K, N = a.shape[1], b.shape[1]
    if axis == 1:
        return _mm(a, b, mode="tn", name=name).reshape(N_CHIPS, K // N_CHIPS, N)
    if (N // N_CHIPS) % LANES == 0:
        return _mm(a, b, mode="tn", name=name, split=N_CHIPS)
    return _mm(a, b, mode="tn", name=name).reshape(K, N_CHIPS, N // N_CHIPS).transpose(1, 0, 2)


def _mlp_fwd(x, gain, w_up, w_down, tag):
    hm = _norm_fwd(x[None], gain, name=f"mlp{tag}_norm")[0]
    u, act = _mm(hm, w_up, mode="nn", name=f"mlp{tag}_up", epi="sqrelu")
    x_out = _mm(act, w_down, mode="nn", name=f"mlp{tag}_down", epi="add", extra=x)
    return x_out, (hm, u, act)


def _mlp_bwd(x, gain, w_up, w_down, saved, dxo, tag):
    hm, u, act = saved
    du = _mm(dxo, w_down, mode="nt", name=f"mlp{tag}_dact", epi="dsqrelu", extra=u, out_dtype=BF16)
    dw_down = _dw(act, dxo, name=f"mlp{tag}_dwdown", axis=1)
    dhm = _mm(du, w_up, mode="nt", name=f"mlp{tag}_dhm")
    dw_up = _dw(hm, du, name=f"mlp{tag}_dwup", axis=2)
    dx, dgain = _norm_bwd(x[None], gain, dhm[None], name=f"mlp{tag}_dnorm", res=dxo)
    return dx[0], dgain[0], dw_up, dw_down


def _local_step(x, target, W, small):
    T, D = x.shape
    pos = jnp.arange(T)
    mla_cos, mla_sin = _rope_cos_sin(pos, MLA_ROPE, ROPE_THETA)
    row_cos, row_sin = _rope_cos_sin(pos // GRID_W, GQA_DIM // 2, AXIAL_THETA)
    col_cos, col_sin = _rope_cos_sin(pos % GRID_W, GQA_DIM // 2, AXIAL_THETA)
    swa_cos, swa_sin = _rope_cos_sin(pos, SWA_ROT, ROPE_THETA)
    rope_q = _rope_tables(T, MLA_QK, [(MLA_NOPE, MLA_ROPE, mla_cos, mla_sin)])
    rope_kr = _rope_tables(T, MLA_ROPE, [(0, MLA_ROPE, mla_cos, mla_sin)])
    half = GQA_DIM // 2
    rope_ax = _rope_tables(T, GQA_DIM, [(0, half, row_cos, row_sin), (half, half, col_cos, col_sin)])
    rope_sw = _rope_tables(T, SWA_DIM, [(0, SWA_ROT, swa_cos, swa_sin)])
    o1 = MLA_Q_LORA
    o2 = o1 + MLA_KV_LORA
    o3 = o2 + MLA_ROPE
    o4 = o3 + GQA_HEADS * GQA_DIM
    o5 = o4 + GQA_KV * GQA_DIM
    sc_a, sc_g, sc_s = MLA_QK ** -0.5, GQA_DIM ** -0.5, SWA_DIM ** -0.5
    kv_w = MLA_NOPE + MLA_V

    h0 = _norm_fwd(x[None], small["even_norm"], name="even_norm")[0]
    proj = _mm(h0, W["even_w_in"], mode="nn", name="even_in")
    c_q, c_kv, kr_raw = proj[:, :o1], proj[:, o1:o2], proj[:, o2:o3]
    qg_raw = _heads(proj[:, o3:o4], GQA_HEADS, GQA_DIM)
    kg_raw = _heads(proj[:, o4:o5], GQA_KV, GQA_DIM)
    vg = _heads(proj[:, o5:], GQA_KV, GQA_DIM).astype(BF16)
    cqn = _norm_fwd(c_q[None], small["mla_q_lat_norm"], name="q_lat_norm")[0]
    ckvn = _norm_fwd(c_kv[None], small["mla_kv_lat_norm"], name="kv_lat_norm")[0]
    qa_raw = _heads(_mm(cqn, W["mla_w_uq"], mode="nn", name="mla_uq"), MLA_HEADS, MLA_QK)
    kv = _mm(ckvn, W["mla_w_ukv"], mode="nn", name="mla_ukv").reshape(T, MLA_HEADS, kv_w)
    kn_raw = kv[:, :, :MLA_NOPE].transpose(1, 0, 2)
    va = kv[:, :, MLA_NOPE:].transpose(1, 0, 2).astype(BF16)
    q_a = _norm_fwd(qa_raw, small["mla_q_norm"], name="mla_q_prep", rope=rope_q)
    k_n = _norm_fwd(kn_raw, small["mla_k_nope_norm"], name="mla_kn_prep")
    k_r = _norm_fwd(kr_raw[None], small["mla_k_rope_norm"], name="mla_kr_prep", rope=rope_kr)
    k_a = jnp.concatenate([k_n, jnp.broadcast_to(k_r, (MLA_HEADS, T, MLA_ROPE))], axis=-1)
    o_a, lse_a = _flash_fwd(q_a, k_a, va, sc_a, name="mla_attn")
    q_g = _norm_fwd(qg_raw, small["gqa_q_norm"], name="gqa_q_prep", rope=rope_ax)
    k_g = _norm_fwd(kg_raw, small["gqa_k_norm"], name="gqa_k_prep", rope=rope_ax)
    o_g, lse_g = _flash_fwd(q_g, k_g, vg, sc_g, name="gqa_attn")
    merged = jnp.concatenate([_unheads(o_a), _unheads(o_g)], axis=-1).astype(BF16)
    x1 = _mm(merged, W["even_w_out"], mode="nn", name="even_out", epi="add", extra=x)
    x2, mlp0 = _mlp_fwd(x1, small["mlp_norm"][0], W["mlp_w_up0"], W["mlp_w_down0"], 0)

    h1 = _norm_fwd(x2[None], small["odd_norm"], name="odd_norm")[0]
    qkv = _mm(h1, W["odd_w_qkv"], mode="nn", name="odd_qkv")
    nq, nkk = SWA_HEADS * SWA_DIM, SWA_KV * SWA_DIM
    qs_raw = _heads(qkv[:, :nq], SWA_HEADS, SWA_DIM)
    ks_raw = _heads(qkv[:, nq:nq + nkk], SWA_KV, SWA_DIM)
    vs = _heads(qkv[:, nq + nkk:], SWA_KV, SWA_DIM).astype(BF16)
    q_s = _norm_fwd(qs_raw, small["swa_q_norm"], name="swa_q_prep", rope=rope_sw)
    k_s = _norm_fwd(ks_raw, small["swa_k_norm"], name="swa_k_prep", rope=rope_sw)
    sink = small["swa_sink"].reshape(SWA_HEADS, 1, 1)
    o_s, lse_s = _swa_fwd(q_s, k_s, vs, sink, sc_s, name="swa_attn")
    o_flat = _unheads(o_s).astype(BF16)
    x3 = _mm(o_flat, W["odd_w_out"], mode="nn", name="odd_out", epi="add", extra=x2)
    x4, mlp1 = _mlp_fwd(x3, small["mlp_norm"][1], W["mlp_w_up1"], W["mlp_w_down1"], 1)

    dy, loss_sum = _loss_head(x4, target, name="loss_head")
    gW, gs = {}, {}

    dx3, dg_m1, gW["mlp_w_up1"], gW["mlp_w_down1"] = _mlp_bwd(
        x3, small["mlp_norm"][1], W["mlp_w_up1"], W["mlp_w_down1"], mlp1, dy, 1)
    d_oflat = _mm(dx3, W["odd_w_out"], mode="nt", name="odd_dout")
    gW["odd_w_out"] = _dw(o_flat, dx3, name="odd_dwout", axis=1)
    do_s = _heads(d_oflat, SWA_HEADS, SWA_DIM)
    delta_s, dob_s = _delta(o_s, do_s, name="swa_delta")
    dq_s, dsink = _swa_dq(q_s, k_s, vs, dob_s, lse_s, delta_s, sink, sc_s, name="swa_dq")
    dk_s, dv_s = _swa_dkv(q_s, k_s, vs, dob_s, lse_s, delta_s, sc_s, name="swa_dkv")
    gs["swa_sink"] = dsink[:, 0, 0]
    dqs_raw, gs["swa_q_norm"] = _norm_bwd(qs_raw, small["swa_q_norm"], dq_s, name="swa_dq_prep", rope=rope_sw)
    dks_raw, gs["swa_k_norm"] = _norm_bwd(ks_raw, small["swa_k_norm"], dk_s, name="swa_dk_prep", rope=rope_sw)
    dqkv = jnp.concatenate([_unheads(dqs_raw), _unheads(dks_raw), _unheads(dv_s)], axis=-1).astype(BF16)
    dh1 = _mm(dqkv, W["odd_w_qkv"], mode="nt", name="odd_dh")
    gW["odd_w_qkv"] = _dw(h1, dqkv, name="odd_dwqkv", axis=2)
    dx2, gs["odd_norm"] = _norm_bwd(x2[None], small["odd_norm"], dh1[None], name="odd_dnorm", res=dx3)
    dx2 = dx2[0]

    dx1, dg_m0, gW["mlp_w_up0"], gW["mlp_w_down0"] = _mlp_bwd(
        x1, small["mlp_norm"][0], W["mlp_w_up0"], W["mlp_w_down0"], mlp0, dx2, 0)
    gs["mlp_norm"] = jnp.stack([dg_m0, dg_m1])
    d_merged = _mm(dx1, W["even_w_out"], mode="nt", name="even_dout")
    gW["even_w_out"] = _dw(merged, dx1, name="even_dwout", axis=1)
    na = MLA_HEADS * MLA_V
    do_a = _heads(d_merged[:, :na], MLA_HEADS, MLA_V)
    do_g = _heads(d_merged[:, na:], GQA_HEADS, GQA_DIM)
    delta_a, dob_a = _delta(o_a, do_a, name="mla_delta")
    dq_a, dk_a, dv_a = _flash_bwd(q_a, k_a, va, dob_a, lse_a, delta_a, sc_a, name="mla_attn_bwd")
    delta_g, dob_g = _delta(o_g, do_g, name="gqa_delta")
    dq_g, dk_gp, dv_gp = _flash_bwd(q_g, k_g, vg, dob_g, lse_g, delta_g, sc_g, name="gqa_attn_bwd")
    grp = GQA_HEADS // GQA_KV
    dqg_raw, gs["gqa_q_norm"] = _norm_bwd(qg_raw, small["gqa_q_norm"], dq_g, name="gqa_dq_prep", rope=rope_ax,
                                          dy_scale=sc_g)
    dkg_raw, gs["gqa_k_norm"] = _norm_bwd(kg_raw, small["gqa_k_norm"], dk_gp, name="gqa_dk_prep", rope=rope_ax,
                                          group=grp, dy_scale=sc_g)
    dvg = _group_sum(dv_gp, grp, name="gqa_dv_sum")
    dqa_raw, gs["mla_q_norm"] = _norm_bwd(qa_raw, small["mla_q_norm"], dq_a, name="mla_dq_prep", rope=rope_q,
                                          dy_scale=sc_a)
    dkn_raw, gs["mla_k_nope_norm"] = _norm_bwd(kn_raw, small["mla_k_nope_norm"], dk_a[:, :, :MLA_NOPE],
                                               name="mla_dkn_prep", dy_scale=sc_a)
    dkr_raw, gs["mla_k_rope_norm"] = _norm_bwd(kr_raw[None], small["mla_k_rope_norm"], dk_a[:, :, MLA_NOPE:],
                                               name="mla_dkr_prep", rope=rope_kr, group=MLA_HEADS, dy_scale=sc_a)
    dkv = jnp.concatenate([dkn_raw.transpose(1, 0, 2), dv_a.transpose(1, 0, 2)], axis=-1)
    dkv = dkv.reshape(T, MLA_HEADS * kv_w).astype(BF16)
    dqa = _unheads(dqa_raw).astype(BF16)
    dckvn = _mm(dkv, W["mla_w_ukv"], mode="nt", name="mla_dckv")
    gW["mla_w_ukv"] = _dw(ckvn, dkv, name="mla_dwukv", axis=2)
    dcqn = _mm(dqa, W["mla_w_uq"], mode="nt", name="mla_dcq")
    gW["mla_w_uq"] = _dw(cqn, dqa, name="mla_dwuq", axis=2)
    dc_q, gs["mla_q_lat_norm"] = _norm_bwd(c_q[None], small["mla_q_lat_norm"], dcqn[None], name="q_lat_dnorm")
    dc_kv, gs["mla_kv_lat_norm"] = _norm_bwd(c_kv[None], small["mla_kv_lat_norm"], dckvn[None], name="kv_lat_dnorm")
    dproj = jnp.concatenate([dc_q[0], dc_kv[0], dkr_raw[0], _unheads(dqg_raw), _unheads(dkg_raw), _unheads(dvg)],
                            axis=-1).astype(BF16)
    dh0 = _mm(dproj, W["even_w_in"], mode="nt", name="even_dh")
    gW["even_w_in"] = _dw(h0, dproj, name="even_dwin", axis=2)
    dx0, gs["even_norm"] = _norm_bwd(x[None], small["even_norm"], dh0[None], name="even_dnorm", res=dx1)
    gs = {k: v.reshape(-1) for k, v in gs.items()}
    return loss_sum, dx0[0], gW, gs


BIG = (("even_w_in", 0, 2), ("mla_w_uq", 0, 2), ("mla_w_ukv", 0, 2), ("even_w_out", 0, 1), ("odd_w_qkv", 0, 2),
       ("odd_w_out", 0, 1), ("mlp_w_up", 0, 2), ("mlp_w_up", 1, 2), ("mlp_w_down", 0, 1), ("mlp_w_down", 1, 1))
SMALL = ("even_norm", "mla_q_lat_norm", "mla_kv_lat_norm", "mla_q_norm", "mla_k_nope_norm", "mla_k_rope_norm",
         "gqa_q_norm", "gqa_k_norm", "odd_norm", "swa_q_norm", "swa_k_norm", "swa_sink", "mlp_norm")
def _pad_to(v, n):
    return v if v.shape[-1] == n else jnp.pad(v, [(0, 0)] * (v.ndim - 1) + [(0, n - v.shape[-1])])


def _big_key(name, layer, w):
    return name if w[name].shape[0] == 1 else f"{name}{layer}"


def _pack_rows(flat, rows=8):
    n = flat.shape[0]
    padded = -(-n // (rows * LANES)) * rows * LANES
    return _pad_to(flat, padded).reshape(-1, LANES)


def kernel(x, even_norm, even_w_in, mla_q_lat_norm, mla_kv_lat_norm, mla_w_uq, mla_w_ukv, mla_q_norm, mla_k_nope_norm, mla_k_rope_norm, gqa_q_norm, gqa_k_norm, even_w_out, odd_norm, odd_w_qkv, swa_q_norm, swa_k_norm, swa_sink, odd_w_out, mlp_norm, mlp_w_up, mlp_w_down, loss_target, m_even_norm, m_even_w_in, m_mla_q_lat_norm, m_mla_kv_lat_norm, m_mla_w_uq, m_mla_w_ukv, m_mla_q_norm, m_mla_k_nope_norm, m_mla_k_rope_norm, m_gqa_q_norm, m_gqa_k_norm, m_even_w_out, m_odd_norm, m_odd_w_qkv, m_swa_q_norm, m_swa_k_norm, m_swa_sink, m_odd_w_out, m_mlp_norm, m_mlp_w_up, m_mlp_w_down, v_even_norm, v_even_w_in, v_mla_q_lat_norm, v_mla_kv_lat_norm, v_mla_w_uq, v_mla_w_ukv, v_mla_q_norm, v_mla_k_nope_norm, v_mla_k_rope_norm, v_gqa_q_norm, v_gqa_k_norm, v_even_w_out, v_odd_norm, v_odd_w_qkv, v_swa_q_norm, v_swa_k_norm, v_swa_sink, v_odd_w_out, v_mlp_norm, v_mlp_w_up, v_mlp_w_down):
    w = dict(even_norm=even_norm, even_w_in=even_w_in, mla_q_lat_norm=mla_q_lat_norm, mla_kv_lat_norm=mla_kv_lat_norm,
             mla_w_uq=mla_w_uq, mla_w_ukv=mla_w_ukv, mla_q_norm=mla_q_norm, mla_k_nope_norm=mla_k_nope_norm,
             mla_k_rope_norm=mla_k_rope_norm, gqa_q_norm=gqa_q_norm, gqa_k_norm=gqa_k_norm, even_w_out=even_w_out,
             odd_norm=odd_norm, odd_w_qkv=odd_w_qkv, swa_q_norm=swa_q_norm, swa_k_norm=swa_k_norm, swa_sink=swa_sink,
             odd_w_out=odd_w_out, mlp_norm=mlp_norm, mlp_w_up=mlp_w_up, mlp_w_down=mlp_w_down)
    m = dict(even_norm=m_even_norm, even_w_in=m_even_w_in, mla_q_lat_norm=m_mla_q_lat_norm,
             mla_kv_lat_norm=m_mla_kv_lat_norm, mla_w_uq=m_mla_w_uq, mla_w_ukv=m_mla_w_ukv, mla_q_norm=m_mla_q_norm,
             mla_k_nope_norm=m_mla_k_nope_norm, mla_k_rope_norm=m_mla_k_rope_norm, gqa_q_norm=m_gqa_q_norm,
             gqa_k_norm=m_gqa_k_norm, even_w_out=m_even_w_out, odd_norm=m_odd_norm, odd_w_qkv=m_odd_w_qkv,
             swa_q_norm=m_swa_q_norm, swa_k_norm=m_swa_k_norm, swa_sink=m_swa_sink, odd_w_out=m_odd_w_out,
             mlp_norm=m_mlp_norm, mlp_w_up=m_mlp_w_up, mlp_w_down=m_mlp_w_down)
    v = dict(even_norm=v_even_norm, even_w_in=v_even_w_in, mla_q_lat_norm=v_mla_q_lat_norm,
             mla_kv_lat_norm=v_mla_kv_lat_norm, mla_w_uq=v_mla_w_uq, mla_w_ukv=v_mla_w_ukv, mla_q_norm=v_mla_q_norm,
             mla_k_nope_norm=v_mla_k_nope_norm, mla_k_rope_norm=v_mla_k_rope_norm, gqa_q_norm=v_gqa_q_norm,
             gqa_k_norm=v_gqa_k_norm, even_w_out=v_even_w_out, odd_norm=v_odd_norm, odd_w_qkv=v_odd_w_qkv,
             swa_q_norm=v_swa_q_norm, swa_k_norm=v_swa_k_norm, swa_sink=v_swa_sink, odd_w_out=v_odd_w_out,
             mlp_norm=v_mlp_norm, mlp_w_up=v_mlp_w_up, mlp_w_down=v_mlp_w_down)
    xi, yi, ci = _position()
    chip = 2 * xi + yi
    T, D = x.shape[1], x.shape[2]

    halves = []
    for name, layer, _ in BIG:
        ks, ns = w[name].shape[1:]
        halves.append(w[name][layer].astype(BF16).reshape(2, ks // 2, ns))
    gathered = _all_gather_halves(halves, name="weights_all_gather")
    W = {}
    for (name, layer, axis), g in zip(BIG, gathered):
        ks, ns = w[name].shape[1:]
        stacked = g.reshape(N_CHIPS, ks, ns)
        if axis == 1:
            W[_big_key(name, layer, w)] = stacked.reshape(N_CHIPS * ks, ns)
        else:
            W[_big_key(name, layer, w)] = stacked.transpose(1, 0, 2).reshape(ks, N_CHIPS * ns)

    odd_full = jnp.zeros((N_CHIPS, D // N_CHIPS), F32).at[chip].set(jnp.where(ci == 0, 1.0, 0.0) * w["odd_norm"][0])
    odd_full = _all_reduce_small(_pack_rows(odd_full.reshape(-1)), name="odd_norm_gather").reshape(-1)[:D]
    small = {name: w[name][0] for name in SMALL if name not in ("mlp_norm", "odd_norm")}
    small["mlp_norm"] = w["mlp_norm"]
    small["odd_norm"] = odd_full

    loss_sum, grad_x, gW, gs = _local_step(x[0], loss_target[0], W, small)

    loss_local = 0.5 * loss_sum.reshape(1) / D
    small_sizes = [(name, int(gs[name].shape[0])) for name in SMALL]
    ar_in = jnp.concatenate([_pad_to(loss_local, LANES)] + [gs[name] for name in SMALL])
    ar_out = _all_reduce_small(_pack_rows(ar_in), name="small_all_reduce").reshape(-1)
    loss = ar_out[0]
    g_small, off = {}, LANES
    for name, n in small_sizes:
        g_small[name] = ar_out[off:off + n]
        off += n
    shard_d = D // N_CHIPS
    g_small["odd_norm"] = lax.dynamic_slice(g_small["odd_norm"], (chip * shard_d,), (shard_d,))

    keys = [_big_key(name, layer, w) for name, layer, _ in BIG]
    c_idx = ci.reshape(1).astype(jnp.int32)
    g_all = []
    for key in keys:
        _, ks, ns = gW[key].shape
        g_all.append(gW[key].reshape(N_CHIPS, 2, ks // 2, ns))
    from_sibling = _sibling_exchange(g_all, name="grad_sibling_exchange")
    pairs = [_pair_add(g, r, c_idx, name=f"grad_pair_add_{key}") for key, g, r in zip(keys, g_all, from_sibling)]
    from_chips = _chip_scatter(pairs, name="grad_chip_scatter")
    reduced = [_sum_chips(q, c_idx, name=f"grad_chip_sum_{key}") for key, q in zip(keys, from_chips)]
    shared = _sibling_share(reduced, name="grad_sibling_share")
    g_shards = {}
    for (name, layer, _), f in zip(BIG, shared):
        g_shards.setdefault(name, []).append(f.reshape(w[name].shape[1:]))

    grads, deltas, new_m, new_v = {}, {}, {}, {}
    for name in g_shards:
        shape = w[name].shape
        g = jnp.stack(g_shards[name])
        grads[name] = g
        two_d = (shape[0] * shape[1], shape[2])
        d_, m_, v_ = _adamw(w[name].reshape(two_d), g.reshape(two_d), m[name].reshape(two_d), v[name].reshape(two_d),
                            name=f"adamw_{name}")
        deltas[name], new_m[name], new_v[name] = d_.reshape(shape), m_.reshape(shape), v_.reshape(shape)
    pack_small = lambda d: _pack_rows(jnp.concatenate([d[name].reshape(-1) for name in SMALL]))
    for name in SMALL:
        grads[name] = g_small[name].reshape(w[name].shape)
    d_, m_, v_ = _adamw(pack_small(w), pack_small(grads), pack_small(m), pack_small(v), name="adamw_small")
    d_, m_, v_ = d_.reshape(-1), m_.reshape(-1), v_.reshape(-1)
    off = 0
    for name in SMALL:
        n = int(np.prod(w[name].shape))
        deltas[name] = d_[off:off + n].reshape(w[name].shape)
        new_m[name] = m_[off:off + n].reshape(w[name].shape)
        new_v[name] = v_[off:off + n].reshape(w[name].shape)
        off += n

    order = ("even_norm", "even_w_in", "mla_q_lat_norm", "mla_kv_lat_norm", "mla_w_uq", "mla_w_ukv", "mla_q_norm",
             "mla_k_nope_norm", "mla_k_rope_norm", "gqa_q_norm", "gqa_k_norm", "even_w_out", "odd_norm", "odd_w_qkv",
             "swa_q_norm", "swa_k_norm", "swa_sink", "odd_w_out", "mlp_norm", "mlp_w_up", "mlp_w_down")
    outs = [loss, grad_x[None]]
    for group in (grads, deltas, new_m, new_v):
        outs += [group[name] for name in order]
    return tuple(outs)
```

```python
import functools
import math

import numpy as np
import jax
import jax.numpy as jnp
from jax import lax
from jax.experimental import pallas as pl
from jax.experimental.pallas import tpu as pltpu

F32 = jnp.float32
BF16 = jnp.bfloat16
MESH = pl.DeviceIdType.MESH

VMEM_BYTES_V7X = 64 * 1024 * 1024
LANES = 128
SUBLANES_BF16 = 16

GRID_W = 64
NORM_EPS = 1e-6
ROPE_THETA = 500000.0
AXIAL_THETA = 10000.0
MLA_HEADS = 8
MLA_Q_LORA = 512
MLA_KV_LORA = 256
MLA_NOPE = 128
MLA_ROPE = 64
MLA_QK = MLA_NOPE + MLA_ROPE
MLA_V = 128
GQA_HEADS = 8
GQA_KV = 2
GQA_DIM = 128
SWA_HEADS = 32
SWA_KV = 4
SWA_DIM = 64
SWA_WINDOW = 128
SWA_ROT = SWA_DIM // 4
SWA_BLOCK = 128
SWA_HEAD_PARTS = 4
ADAM_LR = 0.001
ADAM_B1 = 0.9
ADAM_B2 = 0.999
ADAM_EPS = 1e-08
ADAM_WD = 0.01
ADAM_STEP = 10
N_CHIPS = 4
COMM_LANES = 1024


def _tile(dim, cap, mult=LANES):
    if dim <= cap:
        return dim
    t = (cap // mult) * mult
    while t >= mult:
        if dim % t == 0:
            return t
        t -= mult
    return dim


def _params(dims, vmem_estimate):
    limit = int(min(max(vmem_estimate * 1.25 + (4 << 20), 32 << 20), VMEM_BYTES_V7X - (6 << 20)))
    return pltpu.CompilerParams(dimension_semantics=dims, vmem_limit_bytes=limit)


def _nbytes(shape, dtype):
    return int(np.prod(shape)) * jnp.dtype(dtype).itemsize


def _mm(a, b, *, mode, name, out_dtype=F32, epi=None, extra=None, split=1, caps=(1024, 1024, 2048)):
    if mode == "nn":
        (M, K), (K2, N) = a.shape, b.shape
    elif mode == "nt":
        (M, K), (N, K2) = a.shape, b.shape
    else:
        (K, M), (K2, N) = a.shape, b.shape
    assert K == K2, (a.shape, b.shape, mode)
    assert N % split == 0
    ns = N // split
    tn, tk = _tile(ns, caps[1]), _tile(K, caps[2])
    tm = _tile(M, min(caps[0], max(LANES, caps[0] * caps[1] // tn)))
    nj_per = ns // tn
    grid = (M // tm, N // tn, K // tk)
    nk = grid[2]
    if mode == "nn":
        a_spec = pl.BlockSpec((tm, tk), lambda i, j, k: (i, k))
        b_spec = pl.BlockSpec((tk, tn), lambda i, j, k: (k, j))
        dn = (((1,), (0,)), ((), ()))
    elif mode == "nt":
        a_spec = pl.BlockSpec((tm, tk), lambda i, j, k: (i, k))
        b_spec = pl.BlockSpec((tn, tk), lambda i, j, k: (j, k))
        dn = (((1,), (1,)), ((), ()))
    else:
        a_spec = pl.BlockSpec((tk, tm), lambda i, j, k: (k, i))
        b_spec = pl.BlockSpec((tk, tn), lambda i, j, k: (k, j))
        dn = (((0,), (0,)), ((), ()))
    if split == 1:
        o_spec = pl.BlockSpec((tm, tn), lambda i, j, k: (i, j))
        o_shape = (M, N)
    else:
        o_spec = pl.BlockSpec((None, tm, tn), lambda i, j, k: (j // nj_per, i, j % nj_per))
        o_shape = (split, M, ns)
    mn_spec = pl.BlockSpec((tm, tn), lambda i, j, k: (i, j))
    in_specs, args = [a_spec, b_spec], [a, b]
    if epi in ("add", "dsqrelu"):
        in_specs.append(mn_spec)
        args.append(extra)
    if epi == "sqrelu":
        out_shape = (jax.ShapeDtypeStruct(o_shape, BF16), jax.ShapeDtypeStruct(o_shape, BF16))
        out_specs = (o_spec, o_spec)
        n_out = 2
    else:
        out_shape = jax.ShapeDtypeStruct(o_shape, out_dtype)
        out_specs = o_spec
        n_out = 1

    def body(*refs):
        a_ref, b_ref = refs[0], refs[1]
        e_ref = refs[2] if len(args) == 3 else None
        outs = refs[len(args):len(args) + n_out]

        def finish(acc):
            if epi is None:
                outs[0][...] = acc.astype(outs[0].dtype)
            elif epi == "add":
                outs[0][...] = (e_ref[...] + acc).astype(outs[0].dtype)
            elif epi == "sqrelu":
                r = jnp.maximum(acc, 0.0)
                outs[0][...] = acc.astype(BF16)
                outs[1][...] = (r * r).astype(BF16)
            else:
                u = e_ref[...].astype(F32)
                outs[0][...] = (acc * (2.0 * jnp.maximum(u, 0.0))).astype(outs[0].dtype)

        prod = lax.dot_general(a_ref[...].astype(BF16), b_ref[...].astype(BF16), dn, preferred_element_type=F32)
        if nk == 1:
            finish(prod)
            return
        acc_ref = refs[-1]
        k = pl.program_id(2)

        @pl.when(k == 0)
        def _():
            acc_ref[...] = prod

        @pl.when((k != 0) & (k != nk - 1))
        def _():
            acc_ref[...] += prod

        @pl.when(k == nk - 1)
        def _():
            finish(acc_ref[...] + prod)

    est = 2 * (_nbytes((tm, tk), a.dtype) + _nbytes((tk, tn), b.dtype)) + _nbytes((tm, tn), F32)
    est += 2 * n_out * _nbytes((tm, tn), out_dtype if n_out == 1 else BF16)
    if len(args) == 3:
        est += 2 * _nbytes((tm, tn), extra.dtype)
    est += 3 * _nbytes((tm, tn), F32)
    return pl.pallas_call(
        body, name=name, grid=grid, in_specs=in_specs, out_specs=out_specs, out_shape=out_shape,
        scratch_shapes=[] if nk == 1 else [pltpu.VMEM((tm, tn), F32)],
        compiler_params=_params(("parallel", "parallel", "arbitrary"), est),
    )(*args)


def _perm(y, p):
    hi = y.astype(BF16)
    r1 = y - hi.astype(F32)
    mid = r1.astype(BF16)
    lo = (r1 - mid.astype(F32)).astype(BF16)
    d = lambda t: jnp.dot(t, p, preferred_element_type=F32)
    return d(hi) + d(mid) + d(lo)


def _rows_tile(T, d):
    return _tile(T, 2048 if d <= 256 else 512, 128)


def _norm_fwd(x, gain, *, name, rope=None, out_dtype=BF16):
    H, T, d = x.shape
    tm = _rows_tile(T, d)
    g2 = gain.reshape(1, d).astype(F32)
    in_specs = [pl.BlockSpec((None, tm, d), lambda h, i: (h, i, 0)), pl.BlockSpec((1, d), lambda h, i: (0, 0))]
    args = [x, g2]
    if rope is not None:
        in_specs += [pl.BlockSpec((tm, d), lambda h, i: (i, 0)), pl.BlockSpec((tm, d), lambda h, i: (i, 0)),
                     pl.BlockSpec((d, d), lambda h, i: (0, 0))]
        args += list(rope)

    def body(*refs):
        x_ref, g_ref = refs[0], refs[1]
        o_ref = refs[-1]
        xv = x_ref[...]
        y = xv * lax.rsqrt(jnp.mean(xv * xv, axis=-1, keepdims=True) + NORM_EPS)
        y = y * g_ref[...]
        if rope is not None:
            c_ref, s_ref, p_ref = refs[2], refs[3], refs[4]
            y = y * c_ref[...] + _perm(y, p_ref[...]) * s_ref[...]
        o_ref[...] = y.astype(o_ref.dtype)

    est = 2 * (_nbytes((tm, max(d, LANES)), F32) * (3 if rope is not None else 1) + _nbytes((tm, max(d, LANES)), out_dtype))
    est += 6 * _nbytes((tm, max(d, LANES)), F32)
    return pl.pallas_call(
        body, name=name, grid=(H, T // tm), in_specs=in_specs,
        out_specs=pl.BlockSpec((None, tm, d), lambda h, i: (h, i, 0)),
        out_shape=jax.ShapeDtypeStruct((H, T, d), out_dtype),
        compiler_params=_params(("parallel", "parallel"), est),
    )(*args)


def _norm_bwd(x, gain, dy, *, name, rope=None, group=1, res=None, out_dtype=F32, dy_scale=None):
    H, T, d = x.shape
    assert dy.shape == (H * group, T, d), (dy.shape, x.shape, group)
    tm = _rows_tile(T, d)
    g2 = gain.reshape(1, d).astype(F32)
    in_specs = [pl.BlockSpec((None, tm, d), lambda h, i: (h, i, 0)), pl.BlockSpec((1, d), lambda h, i: (0, 0)),
                pl.BlockSpec((group, tm, d), lambda h, i: (h, i, 0))]
    args = [x, g2, dy]
    if rope is not None:
        in_specs += [pl.BlockSpec((tm, d), lambda h, i: (i, 0)), pl.BlockSpec((tm, d), lambda h, i: (i, 0)),
                     pl.BlockSpec((d, d), lambda h, i: (0, 0))]
        args += list(rope)
    if res is not None:
        assert H == 1
        in_specs.append(pl.BlockSpec((tm, d), lambda h, i: (i, 0)))
        args.append(res)
    n_in = len(args)

    def body(*refs):
        x_ref, g_ref, dy_ref = refs[0], refs[1], refs[2]
        dx_ref, dg_ref = refs[n_in], refs[n_in + 1]
        first = (pl.program_id(0) == 0) & (pl.program_id(1) == 0)

        @pl.when(first)
        def _():
            dg_ref[...] = jnp.zeros_like(dg_ref)

        dyv = dy_ref[0].astype(F32)
        for g in range(1, group):
            dyv = dyv + dy_ref[g].astype(F32)
        if dy_scale is not None:
            dyv = dyv * dy_scale
        pos = 3
        if rope is not None:
            c_ref, s_ref, p_ref = refs[3], refs[4], refs[5]
            pos = 6
            dyv = dyv * c_ref[...] + _perm(dyv * s_ref[...], p_ref[...])
        xv = x_ref[...]
        r = lax.rsqrt(jnp.mean(xv * xv, axis=-1, keepdims=True) + NORM_EPS)
        xhat = xv * r
        dg_ref[...] += jnp.sum(dyv * xhat, axis=0, keepdims=True)
        dxh = dyv * g_ref[...]
        dx = r * (dxh - xhat * jnp.mean(dxh * xhat, axis=-1, keepdims=True))
        if res is not None:
            dx = dx + refs[pos][...]
        dx_ref[...] = dx.astype(dx_ref.dtype)

    wide = max(d, LANES)
    est = 2 * _nbytes((tm, wide), F32) * (2 + group + (2 if rope is not None else 0) + (1 if res is not None else 0))
    est += 8 * _nbytes((tm, wide), F32)
    return pl.pallas_call(
        body, name=name, grid=(H, T // tm), in_specs=in_specs,
        out_specs=(pl.BlockSpec((None, tm, d), lambda h, i: (h, i, 0)), pl.BlockSpec((1, d), lambda h, i: (0, 0))),
        out_shape=(jax.ShapeDtypeStruct((H, T, d), out_dtype), jax.ShapeDtypeStruct((1, d), F32)),
        compiler_params=_params(("arbitrary", "arbitrary"), est),
    )(*args)


def _group_sum(x, group, *, name):
    HG, T, d = x.shape
    H = HG // group
    tm = _rows_tile(T, d)

    def body(x_ref, o_ref):
        acc = x_ref[0]
        for g in range(1, group):
            acc = acc + x_ref[g]
        o_ref[...] = acc

    est = 2 * (group + 1) * _nbytes((tm, max(d, LANES)), F32)
    return pl.pallas_call(
        body, name=name, grid=(H, T // tm),
        in_specs=[pl.BlockSpec((group, tm, d), lambda h, i: (h, i, 0))],
        out_specs=pl.BlockSpec((None, tm, d), lambda h, i: (h, i, 0)),
        out_shape=jax.ShapeDtypeStruct((H, T, d), F32),
        compiler_params=_params(("parallel", "parallel"), est),
    )(x)


def _delta(o, do, *, name):
    H, T, d = o.shape
    tm = _rows_tile(T, d)

    def body(o_ref, do_ref, dl_ref, dob_ref):
        dov = do_ref[...]
        dl = jnp.sum(o_ref[...] * dov, axis=-1, keepdims=True)
        dl_ref[...] = jnp.broadcast_to(dl, (tm, LANES))
        dob_ref[...] = dov.astype(BF16)

    spec = pl.BlockSpec((None, tm, d), lambda h, i: (h, i, 0))
    est = 2 * (3 * _nbytes((tm, max(d, LANES)), F32) + _nbytes((tm, LANES), F32))
    return pl.pallas_call(
        body, name=name, grid=(H, T // tm), in_specs=[spec, spec],
        out_specs=(pl.BlockSpec((None, tm, LANES), lambda h, i: (h, i, 0)), spec),
        out_shape=(jax.ShapeDtypeStruct((H, T, LANES), F32), jax.ShapeDtypeStruct((H, T, d), BF16)),
        compiler_params=_params(("parallel", "parallel"), est),
    )(o, do)


NT_DIMS = (((1,), (1,)), ((), ()))
TN_DIMS = (((0,), (0,)), ((), ()))
LOG2E = math.log2(math.e)
FLASH_CHUNK = 256
FLASH_ROW_PARTS = 4


def _flash_fwd(q, k, v, scale, *, name):
    H, T, dk = q.shape
    Hkv, _, dv = v.shape
    G = H // Hkv
    tq = tk = _tile(T, 1024)
    tp = _tile(tq, tq // FLASH_ROW_PARTS, SUBLANES_BF16)
    nk = T // tk
    c2 = scale * LOG2E

    def body(q_ref, k_ref, v_ref, o_ref, lse_ref, m_ref, l_ref, acc_ref):
        ki = pl.program_id(2)

        @pl.when(ki == 0)
        def _():
            m_ref[...] = jnp.full_like(m_ref, -jnp.inf)
            l_ref[...] = jnp.zeros_like(l_ref)
            acc_ref[...] = jnp.zeros_like(acc_ref)

        kv, vv = k_ref[...], v_ref[...]
        parts = [slice(part * tp, (part + 1) * tp) for part in range(tq // tp)]
        m_prev = [m_ref[rows, :] for rows in parts]
        l_prev = [l_ref[rows, :] for rows in parts]
        a_prev = [acc_ref[rows, :] for rows in parts]
        ss = [lax.dot_general(q_ref[rows, :], kv, NT_DIMS, preferred_element_type=F32) for rows in parts]
        m_new = [jnp.maximum(m, jnp.max(s, axis=-1, keepdims=True)) for m, s in zip(m_prev, ss)]
        alpha = [jnp.exp2((m - mn) * c2) for m, mn in zip(m_prev, m_new)]
        ps = [jnp.exp2((s - mn) * c2) for s, mn in zip(ss, m_new)]
        l_new = [a * l + jnp.sum(p, axis=-1, keepdims=True) for a, l, p in zip(alpha, l_prev, ps)]
        pv = [jnp.dot(p.astype(BF16), vv, preferred_element_type=F32) for p in ps]
        for rows, mn, ln, a, acc, o in zip(parts, m_new, l_new, alpha, a_prev, pv):
            m_ref[rows, :] = mn
            l_ref[rows, :] = ln
            acc_ref[rows, :] = a * acc + o

        @pl.when(ki == nk - 1)
        def _():
            l = l_ref[...]
            o_ref[...] = acc_ref[...] / l
            lse_ref[...] = jnp.broadcast_to(m_ref[...] * scale + jnp.log(l), (tq, LANES))

    est = 2 * (_nbytes((tq, dk), BF16) + _nbytes((tk, dk + dv), BF16) + _nbytes((tq, dv + LANES), F32))
    est += 4 * _nbytes((tq, tk), F32) + 3 * _nbytes((tq, dv + 3 * LANES), F32)
    return pl.pallas_call(
        body, name=name, grid=(H, T // tq, nk),
        in_specs=[pl.BlockSpec((None, tq, dk), lambda h, i, j: (h, i, 0)),
                  pl.BlockSpec((None, tk, dk), lambda h, i, j: (h // G, j, 0)),
                  pl.BlockSpec((None, tk, dv), lambda h, i, j: (h // G, j, 0))],
        out_specs=(pl.BlockSpec((None, tq, dv), lambda h, i, j: (h, i, 0)),
                   pl.BlockSpec((None, tq, LANES), lambda h, i, j: (h, i, 0))),
        out_shape=(jax.ShapeDtypeStruct((H, T, dv), F32), jax.ShapeDtypeStruct((H, T, LANES), F32)),
        scratch_shapes=[pltpu.VMEM((tq, 1), F32), pltpu.VMEM((tq, 1), F32), pltpu.VMEM((tq, dv), F32)],
        compiler_params=_params(("parallel", "parallel", "arbitrary"), est),
    )(q, k, v)


def _flash_bwd(q, k, v, do, lse, delta, scale, *, name):
    H, T, dk = q.shape
    Hkv, _, dv = v.shape
    G = H // Hkv
    tq, tk = _tile(T, 1024), _tile(T, 1024)
    tc = _tile(tk, FLASH_CHUNK)
    c2 = scale * LOG2E

    def body(q_ref, k_ref, v_ref, do_ref, lse_ref, dl_ref, dq_ref, dk_ref, dv_ref):
        ki, qi = pl.program_id(1), pl.program_id(2)
        rows = pl.ds(pl.multiple_of(qi * tq, tq), tq)

        @pl.when(qi == 0)
        def _():
            dk_ref[...] = jnp.zeros_like(dk_ref)
            dv_ref[...] = jnp.zeros_like(dv_ref)

        @pl.when(ki == 0)
        def _():
            dq_ref[rows, :] = jnp.zeros((tq, dk), F32)

        qv, dov = q_ref[...], do_ref[...]
        lse2 = lse_ref[:, :1] * LOG2E
        dl = dl_ref[:, :1]
        chunks = [slice(c * tc, (c + 1) * tc) for c in range(tk // tc)]
        kcs = [k_ref[ks, :] for ks in chunks]
        vcs = [v_ref[ks, :] for ks in chunks]
        dv_old = [dv_ref[ks, :] for ks in chunks]
        dk_old = [dk_ref[ks, :] for ks in chunks]
        dq_old = dq_ref[rows, :]
        ss = [lax.dot_general(qv, kc, NT_DIMS, preferred_element_type=F32) for kc in kcs]
        dps = [lax.dot_general(dov, vc, NT_DIMS, preferred_element_type=F32) for vc in vcs]
        ps = [jnp.exp2(s * c2 - lse2) for s in ss]
        dss = [(p * (dp - dl)).astype(BF16) for p, dp in zip(ps, dps)]
        pbs = [p.astype(BF16) for p in ps]
        dvs = [lax.dot_general(pb, dov, TN_DIMS, preferred_element_type=F32) for pb in pbs]
        dks = [lax.dot_general(ds, qv, TN_DIMS, preferred_element_type=F32) for ds in dss]
        dqs = [jnp.dot(ds, kc, preferred_element_type=F32) for ds, kc in zip(dss, kcs)]
        for ks, old, new in zip(chunks, dv_old, dvs):
            dv_ref[ks, :] = old + new
        for ks, old, new in zip(chunks, dk_old, dks):
            dk_ref[ks, :] = old + new
        dq_c = dqs[0]
        for extra in dqs[1:]:
            dq_c = dq_c + extra
        dq_ref[rows, :] = dq_old + dq_c

    est = 2 * (_nbytes((tq, dk + dv), BF16) + _nbytes((tk, dk + dv), BF16) + 2 * _nbytes((tq, LANES), F32))
    est += 2 * (_nbytes((T, dk), F32) + _nbytes((tk, dk + dv), F32)) + 10 * _nbytes((tq, tc), F32)
    return pl.pallas_call(
        body, name=name, grid=(H, T // tk, T // tq),
        in_specs=[pl.BlockSpec((None, tq, dk), lambda h, j, i: (h, i, 0)),
                  pl.BlockSpec((None, tk, dk), lambda h, j, i: (h // G, j, 0)),
                  pl.BlockSpec((None, tk, dv), lambda h, j, i: (h // G, j, 0)),
                  pl.BlockSpec((None, tq, dv), lambda h, j, i: (h, i, 0)),
                  pl.BlockSpec((None, tq, LANES), lambda h, j, i: (h, i, 0)),
                  pl.BlockSpec((None, tq, LANES), lambda h, j, i: (h, i, 0))],
        out_specs=(pl.BlockSpec((None, T, dk), lambda h, j, i: (h, 0, 0)),
                   pl.BlockSpec((None, tk, dk), lambda h, j, i: (h, j, 0)),
                   pl.BlockSpec((None, tk, dv), lambda h, j, i: (h, j, 0))),
        out_shape=(jax.ShapeDtypeStruct((H, T, dk), F32), jax.ShapeDtypeStruct((H, T, dk), F32),
                   jax.ShapeDtypeStruct((H, T, dv), F32)),
        compiler_params=_params(("arbitrary", "arbitrary", "arbitrary"), est),
    )(q, k, v, do, lse, delta)


def _swa_specs(G, d, n_blocks, lanes):
    B = SWA_BLOCK
    prev = lambda j, i: (j, jnp.maximum(i - 1, 0), 0)
    cur = lambda j, i: (j, i, 0)
    nxt = lambda j, i: (j, jnp.minimum(i + 1, n_blocks - 1), 0)
    q_specs = [pl.BlockSpec((G, B, lanes), m) for m in (prev, cur, nxt)]
    kv_specs = [pl.BlockSpec((None, B, d), m) for m in (prev, cur, nxt)]
    return q_specs, kv_specs, cur


def _swa_bias(i, T):
    B = SWA_BLOCK
    row = lax.broadcasted_iota(jnp.int32, (B, 3 * B), 0)
    col = lax.broadcasted_iota(jnp.int32, (B, 3 * B), 1)
    kpos = (i - 1) * B + col
    valid = (col >= row) & (col <= row + 2 * SWA_WINDOW) & (kpos >= 0) & (kpos < T)
    return jnp.where(valid, 0.0, -jnp.inf)


def _swa_fwd(q, k, v, sink, scale, *, name):
    Hq, T, d = q.shape
    Hkv = k.shape[0]
    G = Hq // Hkv
    B = SWA_BLOCK
    nb = T // B
    _, kv_specs, cur = _swa_specs(G, d, nb, d)

    def body(q_ref, k0, k1, k2, v0, v1, v2, sink_ref, o_ref, lse_ref):
        i = pl.program_id(1)
        kv = jnp.concatenate([k0[...], k1[...], k2[...]], axis=0)
        vv = jnp.concatenate([v0[...], v1[...], v2[...]], axis=0)
        bias = _swa_bias(i, T)[None]
        gp = G // SWA_HEAD_PARTS
        parts = [slice(part * gp, (part + 1) * gp) for part in range(SWA_HEAD_PARTS)]
        sks = [sink_ref[hs] for hs in parts]
        ss = [lax.dot_general(q_ref[hs].reshape(gp * B, d), kv, NT_DIMS, preferred_element_type=F32) for hs in parts]
        ss = [(s * scale).reshape(gp, B, 3 * B) + bias for s in ss]
        ms = [jnp.maximum(jnp.max(s, axis=-1, keepdims=True), sk) for s, sk in zip(ss, sks)]
        ps = [jnp.exp(s - m) for s, m in zip(ss, ms)]
        dens = [jnp.sum(p, axis=-1, keepdims=True) + jnp.exp(sk - m) for p, sk, m in zip(ps, sks, ms)]
        pns = [(p / den).reshape(gp * B, 3 * B).astype(BF16) for p, den in zip(ps, dens)]
        os_ = [jnp.dot(pn, vv, preferred_element_type=F32).reshape(gp, B, d) for pn in pns]
        for hs, o, m, den in zip(parts, os_, ms, dens):
            o_ref[hs] = o
            lse_ref[hs] = jnp.broadcast_to(m + jnp.log(den), (gp, B, LANES))

    est = 2 * (_nbytes((G, B, LANES), BF16) + 6 * _nbytes((B, LANES), BF16) + 2 * _nbytes((G, B, LANES), F32))
    est += 8 * _nbytes((G * B, 3 * B), F32)
    return pl.pallas_call(
        body, name=name, grid=(Hkv, nb),
        in_specs=[pl.BlockSpec((G, B, d), cur)] + kv_specs + kv_specs + [pl.BlockSpec((G, 1, 1), lambda j, i: (j, 0, 0))],
        out_specs=(pl.BlockSpec((G, B, d), cur), pl.BlockSpec((G, B, LANES), cur)),
        out_shape=(jax.ShapeDtypeStruct((Hq, T, d), F32), jax.ShapeDtypeStruct((Hq, T, LANES), F32)),
        compiler_params=_params(("parallel", "parallel"), est),
    )(q, k, k, k, v, v, v, sink)


def _swa_dq(q, k, v, do, lse, delta, sink, scale, *, name):
    Hq, T, d = q.shape
    Hkv = k.shape[0]
    G = Hq // Hkv
    B = SWA_BLOCK
    nb = T // B
    _, kv_specs, cur = _swa_specs(G, d, nb, d)

    def body(q_ref, do_ref, lse_ref, dl_ref, k0, k1, k2, v0, v1, v2, sink_ref, dq_ref, dsink_ref):
        i = pl.program_id(1)
        qv = q_ref[...].reshape(G * B, d)
        dov = do_ref[...].reshape(G * B, d)
        kv = jnp.concatenate([k0[...], k1[...], k2[...]], axis=0)
        vv = jnp.concatenate([v0[...], v1[...], v2[...]], axis=0)
        s = lax.dot_general(qv, kv, NT_DIMS, preferred_element_type=F32) * scale
        lse = lse_ref[:, :, :1]
        dl = dl_ref[:, :, :1]
        s = s.reshape(G, B, 3 * B) + _swa_bias(i, T)[None]
        p = jnp.exp(s - lse)
        dp = lax.dot_general(dov, vv, NT_DIMS, preferred_element_type=F32).reshape(G, B, 3 * B)
        ds = (p * (dp - dl) * scale).reshape(G * B, 3 * B).astype(BF16)
        dq_ref[...] = jnp.dot(ds, kv, preferred_element_type=F32).reshape(G, B, d)
        dsk = -jnp.sum(jnp.exp(sink_ref[...] - lse) * dl, axis=1, keepdims=True)

        @pl.when(i == 0)
        def _():
            dsink_ref[...] = jnp.zeros_like(dsink_ref)

        dsink_ref[...] += jnp.broadcast_to(dsk, (G, 1, LANES))

    est = 2 * (2 * _nbytes((G, B, LANES), BF16) + 6 * _nbytes((B, LANES), BF16) + 3 * _nbytes((G, B, LANES), F32))
    est += 8 * _nbytes((G * B, 3 * B), F32)
    return pl.pallas_call(
        body, name=name, grid=(Hkv, nb),
        in_specs=[pl.BlockSpec((G, B, d), cur), pl.BlockSpec((G, B, d), cur), pl.BlockSpec((G, B, LANES), cur),
                  pl.BlockSpec((G, B, LANES), cur)] + kv_specs + kv_specs
                 + [pl.BlockSpec((G, 1, 1), lambda j, i: (j, 0, 0))],
        out_specs=(pl.BlockSpec((G, B, d), cur), pl.BlockSpec((G, 1, LANES), lambda j, i: (j, 0, 0))),
        out_shape=(jax.ShapeDtypeStruct((Hq, T, d), F32), jax.ShapeDtypeStruct((Hq, 1, LANES), F32)),
        compiler_params=_params(("arbitrary", "arbitrary"), est),
    )(q, do, lse, delta, k, k, k, v, v, v, sink)


def _swa_dkv(q, k, v, do, lse, delta, scale, *, name):
    Hq, T, d = q.shape
    Hkv = k.shape[0]
    G = Hq // Hkv
    B = SWA_BLOCK
    nb = T // B
    q_specs, _, cur = _swa_specs(G, d, nb, d)
    l_specs, _, _ = _swa_specs(G, d, nb, LANES)

    def body(k_ref, v_ref, q0, q1, q2, d0, d1, d2, l0, l1, l2, e0, e1, e2, dk_ref, dv_ref):
        b = pl.program_id(1)
        kv, vv = k_ref[...], v_ref[...]
        dk_acc = jnp.zeros((B, d), F32)
        dv_acc = jnp.zeros((B, d), F32)
        for part, (q_ref, do_ref, lse_ref, dl_ref) in enumerate(((q0, d0, l0, e0), (q1, d1, l1, e1), (q2, d2, l2, e2))):
            qv = q_ref[...].reshape(G * B, d)
            dov = do_ref[...].reshape(G * B, d)
            s = lax.dot_general(qv, kv, NT_DIMS, preferred_element_type=F32) * scale
            row = lax.broadcasted_iota(jnp.int32, (B, B), 0)
            col = lax.broadcasted_iota(jnp.int32, (B, B), 1)
            qpos = (b + part - 1) * B + row
            diff = (part - 1) * B + row - col
            valid = (diff >= -SWA_WINDOW) & (diff <= SWA_WINDOW) & (qpos >= 0) & (qpos < T)
            bias = jnp.where(valid, 0.0, -jnp.inf)
            p = jnp.exp(s.reshape(G, B, B) + bias[None] - lse_ref[:, :, :1])
            dp = lax.dot_general(dov, vv, NT_DIMS, preferred_element_type=F32).reshape(G, B, B)
            ds = (p * (dp - dl_ref[:, :, :1]) * scale).reshape(G * B, B).astype(BF16)
            p = p.reshape(G * B, B)
            dv_acc = dv_acc + lax.dot_general(p.astype(BF16), dov, TN_DIMS, preferred_element_type=F32)
            dk_acc = dk_acc + lax.dot_general(ds, qv, TN_DIMS, preferred_element_type=F32)
        dk_ref[...] = dk_acc
        dv_ref[...] = dv_acc

    est = 2 * (6 * _nbytes((G, B, LANES), BF16) + 6 * _nbytes((G, B, LANES), F32) + 4 * _nbytes((B, LANES), F32))
    est += 10 * _nbytes((G * B, B), F32)
    kspec = pl.BlockSpec((None, B, d), cur)
    return pl.pallas_call(
        body, name=name, grid=(Hkv, nb),
        in_specs=[kspec, kspec] + q_specs + q_specs + l_specs + l_specs,
        out_specs=(kspec, kspec),
        out_shape=(jax.ShapeDtypeStruct((Hkv, T, d), F32), jax.ShapeDtypeStruct((Hkv, T, d), F32)),
        compiler_params=_params(("parallel", "parallel"), est),
    )(k, v, q, q, q, do, do, do, lse, lse, lse, delta, delta, delta)


def _loss_head(y, target, *, name):
    T, D = y.shape
    tm = _tile(T, 512)

    def body(y_ref, t_ref, dy_ref, s_ref):
        @pl.when(pl.program_id(0) == 0)
        def _():
            s_ref[...] = jnp.zeros_like(s_ref)

        e = y_ref[...] - t_ref[...]
        dy_ref[...] = e / D
        s_ref[...] += jnp.sum(jnp.sum(e * e, axis=-1, keepdims=True), axis=0, keepdims=True)

    spec = pl.BlockSpec((tm, D), lambda i: (i, 0))
    return pl.pallas_call(
        body, name=name, grid=(T // tm,), in_specs=[spec, spec],
        out_specs=(spec, pl.BlockSpec((1, 1), lambda i: (0, 0))),
        out_shape=(jax.ShapeDtypeStruct((T, D), F32), jax.ShapeDtypeStruct((1, 1), F32)),
        compiler_params=_params(("arbitrary",), 8 * _nbytes((tm, D), F32)),
    )(y, target)


def _adamw(w, g, m, v, *, name):
    R, C = w.shape
    tr = _tile(R, max(8, (1 << 19) // max(C, LANES) // 8 * 8), 8)

    def body(w_ref, g_ref, m_ref, v_ref, d_ref, nm_ref, nv_ref):
        gv = g_ref[...]
        nm = ADAM_B1 * m_ref[...] + (1.0 - ADAM_B1) * gv
        nv = ADAM_B2 * v_ref[...] + (1.0 - ADAM_B2) * jnp.square(gv)
        m_hat = nm / (1.0 - ADAM_B1 ** ADAM_STEP)
        v_hat = nv / (1.0 - ADAM_B2 ** ADAM_STEP)
        d_ref[...] = -ADAM_LR * (m_hat / (jnp.sqrt(v_hat) + ADAM_EPS) + ADAM_WD * w_ref[...])
        nm_ref[...] = nm
        nv_ref[...] = nv

    spec = pl.BlockSpec((tr, C), lambda i: (i, 0))
    sds = jax.ShapeDtypeStruct((R, C), F32)
    return pl.pallas_call(
        body, name=name, grid=(R // tr,), in_specs=[spec] * 4, out_specs=(spec,) * 3, out_shape=(sds,) * 3,
        compiler_params=_params(("parallel",), 16 * _nbytes((tr, max(C, LANES)), F32)),
    )(w, g, m, v)


def _comm_rows_tile(R, L):
    return _tile(R, max(SUBLANES_BF16, (1 << 19) // L // SUBLANES_BF16 * SUBLANES_BF16), SUBLANES_BF16)


def _pair_add(g, recv, c_idx, *, name):
    _, _, R, L = g.shape
    tr = _comm_rows_tile(R, L)

    def body(c_ref, g_ref, r_ref, o_ref):
        o_ref[...] = (g_ref[...] + r_ref[...]).astype(BF16)

    grid_spec = pltpu.PrefetchScalarGridSpec(
        num_scalar_prefetch=1, grid=(N_CHIPS, R // tr),
        in_specs=[pl.BlockSpec((None, None, tr, L), lambda j, i, c: (j, c[0], i, 0)),
                  pl.BlockSpec((None, tr, L), lambda j, i, c: (j, i, 0))],
        out_specs=pl.BlockSpec((None, tr, L), lambda j, i, c: (j, i, 0)))
    return pl.pallas_call(
        body, name=name, grid_spec=grid_spec, out_shape=jax.ShapeDtypeStruct((N_CHIPS, R, L), BF16),
        compiler_params=_params(("parallel", "parallel"), 8 * _nbytes((tr, L), F32)),
    )(c_idx, g, recv)


def _sum_chips(q, c_idx, *, name):
    _, R, L = q.shape
    tr = _comm_rows_tile(R, L)

    def body(c_ref, q_ref, o_ref):
        acc = q_ref[0].astype(F32)
        for j in range(1, N_CHIPS):
            acc = acc + q_ref[j].astype(F32)
        o_ref[...] = acc

    grid_spec = pltpu.PrefetchScalarGridSpec(
        num_scalar_prefetch=1, grid=(R // tr,),
        in_specs=[pl.BlockSpec((N_CHIPS, tr, L), lambda i, c: (0, i, 0))],
        out_specs=pl.BlockSpec((None, tr, L), lambda i, c: (c[0], i, 0)))
    return pl.pallas_call(
        body, name=name, grid_spec=grid_spec, out_shape=jax.ShapeDtypeStruct((2, R, L), F32),
        compiler_params=_params(("parallel",), 10 * _nbytes((tr, L), F32)),
    )(c_idx, q)


HBM_SPEC = pl.BlockSpec(memory_space=pltpu.HBM)


def _position():
    return lax.axis_index("x"), lax.axis_index("y"), lax.axis_index("c")


def _other_chips(x, y):
    return [(1 - x, y), (x, 1 - y), (1 - x, 1 - y)]


AG_COPIES = 7


def _all_gather_halves(ws, *, name):
    n = len(ws)

    def body(*refs):
        w_refs, out_refs = refs[:n], refs[n:2 * n]
        send_sems, recv_sems, local_sems = refs[2 * n:]
        x, y, c = _position()
        me, sibling = (x, y, c), (x, y, 1 - c)
        chips = _other_chips(x, y)

        def copy(i, k, block, to, src=None):
            px, py, pc = block
            slot = out_refs[i].at[4 * px + 2 * py + pc]
            return pltpu.make_async_remote_copy(
                src_ref=slot if src is None else src, dst_ref=slot, send_sem=send_sems.at[AG_COPIES * i + k],
                recv_sem=recv_sems.at[AG_COPIES * i + k], device_id=to, device_id_type=MESH)

        started, mine = [], []
        for i in range(n):
            own = w_refs[i].at[c]
            mine.append(pltpu.make_async_copy(own, out_refs[i].at[4 * x + 2 * y + c], local_sems.at[i]))
            mine[-1].start()
            first = [copy(i, 0, me, sibling, src=own)]
            first += [copy(i, 1 + j, me, (*chip, c), src=own) for j, chip in enumerate(chips)]
            for cp in first:
                cp.start()
            started += first
        for i in range(n):
            for j, chip in enumerate(chips):
                copy(i, 1 + j, (*chip, c), me).wait_recv()
                passed = copy(i, 4 + j, (*chip, c), sibling)
                passed.start()
                started.append(passed)
        for i in range(n):
            copy(i, 0, sibling, me).wait_recv()
            for j, chip in enumerate(chips):
                copy(i, 4 + j, (*chip, 1 - c), me).wait_recv()
        for cp in started:
            cp.wait_send()
        for cp in mine:
            cp.wait()

    return pl.pallas_call(
        body, name=name, in_specs=[HBM_SPEC] * n, out_specs=[HBM_SPEC] * n,
        out_shape=[jax.ShapeDtypeStruct((2 * N_CHIPS,) + w.shape[1:], w.dtype) for w in ws],
        scratch_shapes=[pltpu.SemaphoreType.DMA((AG_COPIES * n,)), pltpu.SemaphoreType.DMA((AG_COPIES * n,)),
                        pltpu.SemaphoreType.DMA((n,))],
    )(*ws)


def _sibling_exchange(gs, *, name):
    n = len(gs)

    def body(*refs):
        g_refs, out_refs = refs[:n], refs[n:2 * n]
        send_sems, recv_sems = refs[2 * n:]
        x, y, c = _position()
        copies = [pltpu.make_async_remote_copy(
            src_ref=g_refs[i].at[j, 1 - c], dst_ref=out_refs[i].at[j], send_sem=send_sems.at[N_CHIPS * i + j],
            recv_sem=recv_sems.at[N_CHIPS * i + j], device_id=(x, y, 1 - c), device_id_type=MESH)
            for i in range(n) for j in range(N_CHIPS)]
        for cp in copies:
            cp.start()
        for cp in copies:
            cp.wait()

    return pl.pallas_call(
        body, name=name, in_specs=[HBM_SPEC] * n, out_specs=[HBM_SPEC] * n,
        out_shape=[jax.ShapeDtypeStruct((N_CHIPS,) + g.shape[2:], g.dtype) for g in gs],
        scratch_shapes=[pltpu.SemaphoreType.DMA((N_CHIPS * n,)), pltpu.SemaphoreType.DMA((N_CHIPS * n,))],
    )(*gs)


def _chip_scatter(ps, *, name):
    n = len(ps)
    others = N_CHIPS - 1

    def body(*refs):
        p_refs, q_refs = refs[:n], refs[n:2 * n]
        send_sems, recv_sems, local_sems = refs[2 * n:]
        x, y, c = _position()
        me = 2 * x + y
        chips = _other_chips(x, y)

        def copy(i, k, chip, src_slot, dst_slot):
            return pltpu.make_async_remote_copy(
                src_ref=p_refs[i].at[src_slot], dst_ref=q_refs[i].at[dst_slot], send_sem=send_sems.at[others * i + k],
                recv_sem=recv_sems.at[others * i + k], device_id=(*chip, c), device_id_type=MESH)

        local = [pltpu.make_async_copy(p_refs[i].at[me], q_refs[i].at[me], local_sems.at[i]) for i in range(n)]
        sends = [copy(i, k, chip, 2 * chip[0] + chip[1], me) for i in range(n) for k, chip in enumerate(chips)]
        for cp in local + sends:
            cp.start()
        for i in range(n):
            for k, chip in enumerate(chips):
                copy(i, k, chip, me, 2 * chip[0] + chip[1]).wait_recv()
        for cp in sends:
            cp.wait_send()
        for cp in local:
            cp.wait()

    return pl.pallas_call(
        body, name=name, in_specs=[HBM_SPEC] * n, out_specs=[HBM_SPEC] * n,
        out_shape=[jax.ShapeDtypeStruct(p.shape, p.dtype) for p in ps],
        scratch_shapes=[pltpu.SemaphoreType.DMA((others * n,)), pltpu.SemaphoreType.DMA((others * n,)),
                        pltpu.SemaphoreType.DMA((n,))],
    )(*ps)


def _sibling_share(fs, *, name):
    n = len(fs)

    def body(*refs):
        in_refs, out_refs = refs[:n], refs[n:2 * n]
        send_sems, recv_sems = refs[2 * n:]
        x, y, c = _position()

        def copy(i, half):
            return pltpu.make_async_remote_copy(
                src_ref=in_refs[i].at[half], dst_ref=out_refs[i].at[half], send_sem=send_sems.at[i],
                recv_sem=recv_sems.at[i], device_id=(x, y, 1 - c), device_id_type=MESH)

        sends = [copy(i, c) for i in range(n)]
        for cp in sends:
            cp.start()
        for i in range(n):
            copy(i, 1 - c).wait_recv()
        for cp in sends:
            cp.wait_send()

    return pl.pallas_call(
        body, name=name, in_specs=[HBM_SPEC] * n, out_specs=[HBM_SPEC] * n,
        out_shape=[jax.ShapeDtypeStruct(f.shape, f.dtype) for f in fs],
        input_output_aliases={i: i for i in range(n)},
        scratch_shapes=[pltpu.SemaphoreType.DMA((n,)), pltpu.SemaphoreType.DMA((n,))],
    )(*fs)


def _all_reduce_small(s, *, name):
    R, L = s.shape
    n_dev = 2 * N_CHIPS

    def body(s_ref, out_ref, buf, send_sems, recv_sems, local_sem):
        x, y, c = _position()
        me, sibling = (x, y, c), (x, y, 1 - c)
        chips = _other_chips(x, y)

        def slot(px, py, pc):
            return buf.at[4 * px + 2 * py + pc]

        def copy(k, block, to, src=None):
            return pltpu.make_async_remote_copy(
                src_ref=slot(*block) if src is None else src, dst_ref=slot(*block),
                send_sem=send_sems.at[k], recv_sem=recv_sems.at[k], device_id=to, device_id_type=MESH)

        mine = pltpu.make_async_copy(s_ref, slot(*me), local_sem)
        mine.start()
        first = [copy(0, me, sibling, src=s_ref)]
        first += [copy(1 + j, me, (*chip, c), src=s_ref) for j, chip in enumerate(chips)]
        for cp in first:
            cp.start()
        passed = [copy(4 + j, (*chip, c), sibling) for j, chip in enumerate(chips)]
        for j, chip in enumerate(chips):
            copy(1 + j, (*chip, c), me).wait_recv()
            passed[j].start()
        copy(0, sibling, me).wait_recv()
        for j, chip in enumerate(chips):
            copy(4 + j, (*chip, 1 - c), me).wait_recv()
        for cp in first + passed:
            cp.wait_send()
        mine.wait()
        acc = buf[0]
        for j in range(1, n_dev):
            acc = acc + buf[j]
        out_ref[...] = acc

    vmem = pl.BlockSpec(memory_space=pltpu.VMEM)
    return pl.pallas_call(
        body, name=name, in_specs=[vmem], out_specs=vmem, out_shape=jax.ShapeDtypeStruct((R, L), F32),
        scratch_shapes=[pltpu.VMEM((n_dev, R, L), F32), pltpu.SemaphoreType.DMA((7,)), pltpu.SemaphoreType.DMA((7,)),
                        pltpu.SemaphoreType.DMA],
    )(s)


def _rope_cos_sin(pos, dim, theta):
    inv = jnp.float32(theta) ** (-jnp.arange(0, dim, 2, dtype=F32) / dim)
    ang = pos.astype(F32)[:, None] * inv[None, :]
    return jnp.cos(ang), jnp.sin(ang)


def _rope_tables(T, d, segments):
    C = jnp.ones((T, d), F32)
    S = jnp.zeros((T, d), F32)
    P = np.zeros((d, d), np.float32)
    for start, size, cos, sin in segments:
        half = size // 2
        C = C.at[:, start:start + half].set(cos).at[:, start + half:start + size].set(cos)
        S = S.at[:, start:start + half].set(-sin).at[:, start + half:start + size].set(sin)
        for p in range(half):
            P[start + half + p, start + p] = 1.0
            P[start + p, start + half + p] = 1.0
    return C, S, jnp.asarray(P, BF16)


def _heads(t, H, d):
    return t.reshape(t.shape[0], H, d).transpose(1, 0, 2)


def _unheads(t):
    H, T, d = t.shape
    return t.transpose(1, 0, 2).reshape(T, H * d)


def _dw(a, b, *, name, axis):
    K, N = a.shape[1], b.shape[1]
    if axis == 1:
        return _mm(a, b, mode="tn", name=name).reshape(N_CHIPS, K // N_CHIPS, N)
    if (N // N_CHIPS) % LANES == 0:
        return _mm(a, b, mode="tn", name=name, split=N_CHIPS)
    return _mm(a, b, mode="tn", name=name).reshape(K, N_CHIPS, N // N_CHIPS).transpose(1, 0, 2)


def _mlp_fwd(x, gain, w_up, w_down, tag):
    hm = _norm_fwd(x[None], gain, name=f"mlp{tag}_norm")[0]
    u, act = _mm(hm, w_up, mode="nn", name=f"mlp{tag}_up", epi="sqrelu")
    x_out = _mm(act, w_down, mode="nn", name=f"mlp{tag}_down", epi="add", extra=x)
    return x_out, (hm, u, act)


def _mlp_bwd(x, gain, w_up, w_down, saved, dxo, tag):
    hm, u, act = saved
    du = _mm(dxo, w_down, mode="nt", name=f"mlp{tag}_dact", epi="dsqrelu", extra=u, out_dtype=BF16)
    dw_down = _dw(act, dxo, name=f"mlp{tag}_dwdown", axis=1)
    dhm = _mm(du, w_up, mode="nt", name=f"mlp{tag}_dhm")
    dw_up = _dw(hm, du, name=f"mlp{tag}_dwup", axis=2)
    dx, dgain = _norm_bwd(x[None], gain, dhm[None], name=f"mlp{tag}_dnorm", res=dxo)
    return dx[0], dgain[0], dw_up, dw_down


def _local_step(x, target, W, small):
    T, D = x.shape
    pos = jnp.arange(T)
    mla_cos, mla_sin = _rope_cos_sin(pos, MLA_ROPE, ROPE_THETA)
    row_cos, row_sin = _rope_cos_sin(pos // GRID_W, GQA_DIM // 2, AXIAL_THETA)
    col_cos, col_sin = _rope_cos_sin(pos % GRID_W, GQA_DIM // 2, AXIAL_THETA)
    swa_cos, swa_sin = _rope_cos_sin(pos, SWA_ROT, ROPE_THETA)
    rope_q = _rope_tables(T, MLA_QK, [(MLA_NOPE, MLA_ROPE, mla_cos, mla_sin)])
    rope_kr = _rope_tables(T, MLA_ROPE, [(0, MLA_ROPE, mla_cos, mla_sin)])
    half = GQA_DIM // 2
    rope_ax = _rope_tables(T, GQA_DIM, [(0, half, row_cos, row_sin), (half, half, col_cos, col_sin)])
    rope_sw = _rope_tables(T, SWA_DIM, [(0, SWA_ROT, swa_cos, swa_sin)])
    o1 = MLA_Q_LORA
    o2 = o1 + MLA_KV_LORA
    o3 = o2 + MLA_ROPE
    o4 = o3 + GQA_HEADS * GQA_DIM
    o5 = o4 + GQA_KV * GQA_DIM
    sc_a, sc_g, sc_s = MLA_QK ** -0.5, GQA_DIM ** -0.5, SWA_DIM ** -0.5
    kv_w = MLA_NOPE + MLA_V

    h0 = _norm_fwd(x[None], small["even_norm"], name="even_norm")[0]
    proj = _mm(h0, W["even_w_in"], mode="nn", name="even_in")
    c_q, c_kv, kr_raw = proj[:, :o1], proj[:, o1:o2], proj[:, o2:o3]
    qg_raw = _heads(proj[:, o3:o4], GQA_HEADS, GQA_DIM)
    kg_raw = _heads(proj[:, o4:o5], GQA_KV, GQA_DIM)
    vg = _heads(proj[:, o5:], GQA_KV, GQA_DIM).astype(BF16)
    cqn = _norm_fwd(c_q[None], small["mla_q_lat_norm"], name="q_lat_norm")[0]
    ckvn = _norm_fwd(c_kv[None], small["mla_kv_lat_norm"], name="kv_lat_norm")[0]
    qa_raw = _heads(_mm(cqn, W["mla_w_uq"], mode="nn", name="mla_uq"), MLA_HEADS, MLA_QK)
    kv = _mm(ckvn, W["mla_w_ukv"], mode="nn", name="mla_ukv").reshape(T, MLA_HEADS, kv_w)
    kn_raw = kv[:, :, :MLA_NOPE].transpose(1, 0, 2)
    va = kv[:, :, MLA_NOPE:].transpose(1, 0, 2).astype(BF16)
    q_a = _norm_fwd(qa_raw, small["mla_q_norm"], name="mla_q_prep", rope=rope_q)
    k_n = _norm_fwd(kn_raw, small["mla_k_nope_norm"], name="mla_kn_prep")
    k_r = _norm_fwd(kr_raw[None], small["mla_k_rope_norm"], name="mla_kr_prep", rope=rope_kr)
    k_a = jnp.concatenate([k_n, jnp.broadcast_to(k_r, (MLA_HEADS, T, MLA_ROPE))], axis=-1)
    o_a, lse_a = _flash_fwd(q_a, k_a, va, sc_a, name="mla_attn")
    q_g = _norm_fwd(qg_raw, small["gqa_q_norm"], name="gqa_q_prep", rope=rope_ax)
    k_g = _norm_fwd(kg_raw, small["gqa_k_norm"], name="gqa_k_prep", rope=rope_ax)
    o_g, lse_g = _flash_fwd(q_g, k_g, vg, sc_g, name="gqa_attn")
    merged = jnp.concatenate([_unheads(o_a), _unheads(o_g)], axis=-1).astype(BF16)
    x1 = _mm(merged, W["even_w_out"], mode="nn", name="even_out", epi="add", extra=x)
    x2, mlp0 = _mlp_fwd(x1, small["mlp_norm"][0], W["mlp_w_up0"], W["mlp_w_down0"], 0)

    h1 = _norm_fwd(x2[None], small["odd_norm"], name="odd_norm")[0]
    qkv = _mm(h1, W["odd_w_qkv"], mode="nn", name="odd_qkv")
    nq, nkk = SWA_HEADS * SWA_DIM, SWA_KV * SWA_DIM
    qs_raw = _heads(qkv[:, :nq], SWA_HEADS, SWA_DIM)
    ks_raw = _heads(qkv[:, nq:nq + nkk], SWA_KV, SWA_DIM)
    vs = _heads(qkv[:, nq + nkk:], SWA_KV, SWA_DIM).astype(BF16)
    q_s = _norm_fwd(qs_raw, small["swa_q_norm"], name="swa_q_prep", rope=rope_sw)
    k_s = _norm_fwd(ks_raw, small["swa_k_norm"], name="swa_k_prep", rope=rope_sw)
    sink = small["swa_sink"].reshape(SWA_HEADS, 1, 1)
    o_s, lse_s = _swa_fwd(q_s, k_s, vs, sink, sc_s, name="swa_attn")
    o_flat = _unheads(o_s).astype(BF16)
    x3 = _mm(o_flat, W["odd_w_out"], mode="nn", name="odd_out", epi="add", extra=x2)
    x4, mlp1 = _mlp_fwd(x3, small["mlp_norm"][1], W["mlp_w_up1"], W["mlp_w_down1"], 1)

    dy, loss_sum = _loss_head(x4, target, name="loss_head")
    gW, gs = {}, {}

    dx3, dg_m1, gW["mlp_w_up1"], gW["mlp_w_down1"] = _mlp_bwd(
        x3, small["mlp_norm"][1], W["mlp_w_up1"], W["mlp_w_down1"], mlp1, dy, 1)
    d_oflat = _mm(dx3, W["odd_w_out"], mode="nt", name="odd_dout")
    gW["odd_w_out"] = _dw(o_flat, dx3, name="odd_dwout", axis=1)
    do_s = _heads(d_oflat, SWA_HEADS, SWA_DIM)
    delta_s, dob_s = _delta(o_s, do_s, name="swa_delta")
    dq_s, dsink = _swa_dq(q_s, k_s, vs, dob_s, lse_s, delta_s, sink, sc_s, name="swa_dq")
    dk_s, dv_s = _swa_dkv(q_s, k_s, vs, dob_s, lse_s, delta_s, sc_s, name="swa_dkv")
    gs["swa_sink"] = dsink[:, 0, 0]
    dqs_raw, gs["swa_q_norm"] = _norm_bwd(qs_raw, small["swa_q_norm"], dq_s, name="swa_dq_prep", rope=rope_sw)
    dks_raw, gs["swa_k_norm"] = _norm_bwd(ks_raw, small["swa_k_norm"], dk_s, name="swa_dk_prep", rope=rope_sw)
    dqkv = jnp.concatenate([_unheads(dqs_raw), _unheads(dks_raw), _unheads(dv_s)], axis=-1).astype(BF16)
    dh1 = _mm(dqkv, W["odd_w_qkv"], mode="nt", name="odd_dh")
    gW["odd_w_qkv"] = _dw(h1, dqkv, name="odd_dwqkv", axis=2)
    dx2, gs["odd_norm"] = _norm_bwd(x2[None], small["odd_norm"], dh1[None], name="odd_dnorm", res=dx3)
    dx2 = dx2[0]

    dx1, dg_m0, gW["mlp_w_up0"], gW["mlp_w_down0"] = _mlp_bwd(
        x1, small["mlp_norm"][0], W["mlp_w_up0"], W["mlp_w_down0"], mlp0, dx2, 0)
    gs["mlp_norm"] = jnp.stack([dg_m0, dg_m1])
    d_merged = _mm(dx1, W["even_w_out"], mode="nt", name="even_dout")
    gW["even_w_out"] = _dw(merged, dx1, name="even_dwout", axis=1)
    na = MLA_HEADS * MLA_V
    do_a = _heads(d_merged[:, :na], MLA_HEADS, MLA_V)
    do_g = _heads(d_merged[:, na:], GQA_HEADS, GQA_DIM)
    delta_a, dob_a = _delta(o_a, do_a, name="mla_delta")
    dq_a, dk_a, dv_a = _flash_bwd(q_a, k_a, va, dob_a, lse_a, delta_a, sc_a, name="mla_attn_bwd")
    delta_g, dob_g = _delta(o_g, do_g, name="gqa_delta")
    dq_g, dk_gp, dv_gp = _flash_bwd(q_g, k_g, vg, dob_g, lse_g, delta_g, sc_g, name="gqa_attn_bwd")
    grp = GQA_HEADS // GQA_KV
    dqg_raw, gs["gqa_q_norm"] = _norm_bwd(qg_raw, small["gqa_q_norm"], dq_g, name="gqa_dq_prep", rope=rope_ax,
                                          dy_scale=sc_g)
    dkg_raw, gs["gqa_k_norm"] = _norm_bwd(kg_raw, small["gqa_k_norm"], dk_gp, name="gqa_dk_prep", rope=rope_ax,
                                          group=grp, dy_scale=sc_g)
    dvg = _group_sum(dv_gp, grp, name="gqa_dv_sum")
    dqa_raw, gs["mla_q_norm"] = _norm_bwd(qa_raw, small["mla_q_norm"], dq_a, name="mla_dq_prep", rope=rope_q,
                                          dy_scale=sc_a)
    dkn_raw, gs["mla_k_nope_norm"] = _norm_bwd(kn_raw, small["mla_k_nope_norm"], dk_a[:, :, :MLA_NOPE],
                                               name="mla_dkn_prep", dy_scale=sc_a)
    dkr_raw, gs["mla_k_rope_norm"] = _norm_bwd(kr_raw[None], small["mla_k_rope_norm"], dk_a[:, :, MLA_NOPE:],
                                               name="mla_dkr_prep", rope=rope_kr, group=MLA_HEADS, dy_scale=sc_a)
    dkv = jnp.concatenate([dkn_raw.transpose(1, 0, 2), dv_a.transpose(1, 0, 2)], axis=-1)
    dkv = dkv.reshape(T, MLA_HEADS * kv_w).astype(BF16)
    dqa = _unheads(dqa_raw).astype(BF16)
    dckvn = _mm(dkv, W["mla_w_ukv"], mode="nt", name="mla_dckv")
    gW["mla_w_ukv"] = _dw(ckvn, dkv, name="mla_dwukv", axis=2)
    dcqn = _mm(dqa, W["mla_w_uq"], mode="nt", name="mla_dcq")
    gW["mla_w_uq"] = _dw(cqn, dqa, name="mla_dwuq", axis=2)
    dc_q, gs["mla_q_lat_norm"] = _norm_bwd(c_q[None], small["mla_q_lat_norm"], dcqn[None], name="q_lat_dnorm")
    dc_kv, gs["mla_kv_lat_norm"] = _norm_bwd(c_kv[None], small["mla_kv_lat_norm"], dckvn[None], name="kv_lat_dnorm")
    dproj = jnp.concatenate([dc_q[0], dc_kv[0], dkr_raw[0], _unheads(dqg_raw), _unheads(dkg_raw), _unheads(dvg)],
                            axis=-1).astype(BF16)
    dh0 = _mm(dproj, W["even_w_in"], mode="nt", name="even_dh")
    gW["even_w_in"] = _dw(h0, dproj, name="even_dwin", axis=2)
    dx0, gs["even_norm"] = _norm_bwd(x[None], small["even_norm"], dh0[None], name="even_dnorm", res=dx1)
    gs = {k: v.reshape(-1) for k, v in gs.items()}
    return loss_sum, dx0[0], gW, gs


BIG = (("even_w_in", 0, 2), ("mla_w_uq", 0, 2), ("mla_w_ukv", 0, 2), ("even_w_out", 0, 1), ("odd_w_qkv", 0, 2),
       ("odd_w_out", 0, 1), ("mlp_w_up", 0, 2), ("mlp_w_up", 1, 2), ("mlp_w_down", 0, 1), ("mlp_w_down", 1, 1))
SMALL = ("even_norm", "mla_q_lat_norm", "mla_kv_lat_norm", "mla_q_norm", "mla_k_nope_norm", "mla_k_rope_norm",
         "gqa_q_norm", "gqa_k_norm", "odd_norm", "swa_q_norm", "swa_k_norm", "swa_sink", "mlp_norm")
def _pad_to(v, n):
    return v if v.shape[-1] == n else jnp.pad(v, [(0, 0)] * (v.ndim - 1) + [(0, n - v.shape[-1])])


def _big_key(name, layer, w):
    return name if w[name].shape[0] == 1 else f"{name}{layer}"


def _pack_rows(flat, rows=8):
    n = flat.shape[0]
    padded = -(-n // (rows * LANES)) * rows * LANES
    return _pad_to(flat, padded).reshape(-1, LANES)


def kernel(x, even_norm, even_w_in, mla_q_lat_norm, mla_kv_lat_norm, mla_w_uq, mla_w_ukv, mla_q_norm, mla_k_nope_norm, mla_k_rope_norm, gqa_q_norm, gqa_k_norm, even_w_out, odd_norm, odd_w_qkv, swa_q_norm, swa_k_norm, swa_sink, odd_w_out, mlp_norm, mlp_w_up, mlp_w_down, loss_target, m_even_norm, m_even_w_in, m_mla_q_lat_norm, m_mla_kv_lat_norm, m_mla_w_uq, m_mla_w_ukv, m_mla_q_norm, m_mla_k_nope_norm, m_mla_k_rope_norm, m_gqa_q_norm, m_gqa_k_norm, m_even_w_out, m_odd_norm, m_odd_w_qkv, m_swa_q_norm, m_swa_k_norm, m_swa_sink, m_odd_w_out, m_mlp_norm, m_mlp_w_up, m_mlp_w_down, v_even_norm, v_even_w_in, v_mla_q_lat_norm, v_mla_kv_lat_norm, v_mla_w_uq, v_mla_w_ukv, v_mla_q_norm, v_mla_k_nope_norm, v_mla_k_rope_norm, v_gqa_q_norm, v_gqa_k_norm, v_even_w_out, v_odd_norm, v_odd_w_qkv, v_swa_q_norm, v_swa_k_norm, v_swa_sink, v_odd_w_out, v_mlp_norm, v_mlp_w_up, v_mlp_w_down):
    w = dict(even_norm=even_norm, even_w_in=even_w_in, mla_q_lat_norm=mla_q_lat_norm, mla_kv_lat_norm=mla_kv_lat_norm,
             mla_w_uq=mla_w_uq, mla_w_ukv=mla_w_ukv, mla_q_norm=mla_q_norm, mla_k_nope_norm=mla_k_nope_norm,
             mla_k_rope_norm=mla_k_rope_norm, gqa_q_norm=gqa_q_norm, gqa_k_norm=gqa_k_norm, even_w_out=even_w_out,
             odd_norm=odd_norm, odd_w_qkv=odd_w_qkv, swa_q_norm=swa_q_norm, swa_k_norm=swa_k_norm, swa_sink=swa_sink,
             odd_w_out=odd_w_out, mlp_norm=mlp_norm, mlp_w_up=mlp_w_up, mlp_w_down=mlp_w_down)
    m = dict(even_norm=m_even_norm, even_w_in=m_even_w_in, mla_q_lat_norm=m_mla_q_lat_norm,
             mla_kv_lat_norm=m_mla_kv_lat_norm, mla_w_uq=m_mla_w_uq, mla_w_ukv=m_mla_w_ukv, mla_q_norm=m_mla_q_norm,
             mla_k_nope_norm=m_mla_k_nope_norm, mla_k_rope_norm=m_mla_k_rope_norm, gqa_q_norm=m_gqa_q_norm,
             gqa_k_norm=m_gqa_k_norm, even_w_out=m_even_w_out, odd_norm=m_odd_norm, odd_w_qkv=m_odd_w_qkv,
             swa_q_norm=m_swa_q_norm, swa_k_norm=m_swa_k_norm, swa_sink=m_swa_sink, odd_w_out=m_odd_w_out,
             mlp_norm=m_mlp_norm, mlp_w_up=m_mlp_w_up, mlp_w_down=m_mlp_w_down)
    v = dict(even_norm=v_even_norm, even_w_in=v_even_w_in, mla_q_lat_norm=v_mla_q_lat_norm,
             mla_kv_lat_norm=v_mla_kv_lat_norm, mla_w_uq=v_mla_w_uq, mla_w_ukv=v_mla_w_ukv, mla_q_norm=v_mla_q_norm,
             mla_k_nope_norm=v_mla_k_nope_norm, mla_k_rope_norm=v_mla_k_rope_norm, gqa_q_norm=v_gqa_q_norm,
             gqa_k_norm=v_gqa_k_norm, even_w_out=v_even_w_out, odd_norm=v_odd_norm, odd_w_qkv=v_odd_w_qkv,
             swa_q_norm=v_swa_q_norm, swa_k_norm=v_swa_k_norm, swa_sink=v_swa_sink, odd_w_out=v_odd_w_out,
             mlp_norm=v_mlp_norm, mlp_w_up=v_mlp_w_up, mlp_w_down=v_mlp_w_down)
    xi, yi, ci = _position()
    chip = 2 * xi + yi
    T, D = x.shape[1], x.shape[2]

    halves = []
    for name, layer, _ in BIG:
        ks, ns = w[name].shape[1:]
        halves.append(w[name][layer].astype(BF16).reshape(2, ks // 2, ns))
    gathered = _all_gather_halves(halves, name="weights_all_gather")
    W = {}
    for (name, layer, axis), g in zip(BIG, gathered):
        ks, ns = w[name].shape[1:]
        stacked = g.reshape(N_CHIPS, ks, ns)
        if axis == 1:
            W[_big_key(name, layer, w)] = stacked.reshape(N_CHIPS * ks, ns)
        else:
            W[_big_key(name, layer, w)] = stacked.transpose(1, 0, 2).reshape(ks, N_CHIPS * ns)

    odd_full = jnp.zeros((N_CHIPS, D // N_CHIPS), F32).at[chip].set(jnp.where(ci == 0, 1.0, 0.0) * w["odd_norm"][0])
    odd_full = _all_reduce_small(_pack_rows(odd_full.reshape(-1)), name="odd_norm_gather").reshape(-1)[:D]
    small = {name: w[name][0] for name in SMALL if name not in ("mlp_norm", "odd_norm")}
    small["mlp_norm"] = w["mlp_norm"]
    small["odd_norm"] = odd_full

    loss_sum, grad_x, gW, gs = _local_step(x[0], loss_target[0], W, small)

    loss_local = 0.5 * loss_sum.reshape(1) / D
    small_sizes = [(name, int(gs[name].shape[0])) for name in SMALL]
    ar_in = jnp.concatenate([_pad_to(loss_local, LANES)] + [gs[name] for name in SMALL])
    ar_out = _all_reduce_small(_pack_rows(ar_in), name="small_all_reduce").reshape(-1)
    loss = ar_out[0]
    g_small, off = {}, LANES
    for name, n in small_sizes:
        g_small[name] = ar_out[off:off + n]
        off += n
    shard_d = D // N_CHIPS
    g_small["odd_norm"] = lax.dynamic_slice(g_small["odd_norm"], (chip * shard_d,), (shard_d,))

    keys = [_big_key(name, layer, w) for name, layer, _ in BIG]
    c_idx = ci.reshape(1).astype(jnp.int32)
    g_all = []
    for key in keys:
        _, ks, ns = gW[key].shape
        g_all.append(gW[key].reshape(N_CHIPS, 2, ks // 2, ns))
    from_sibling = _sibling_exchange(g_all, name="grad_sibling_exchange")
    pairs = [_pair_add(g, r, c_idx, name=f"grad_pair_add_{key}") for key, g, r in zip(keys, g_all, from_sibling)]
    from_chips = _chip_scatter(pairs, name="grad_chip_scatter")
    reduced = [_sum_chips(q, c_idx, name=f"grad_chip_sum_{key}") for key, q in zip(keys, from_chips)]
    shared = _sibling_share(reduced, name="grad_sibling_share")
    g_shards = {}
    for (name, layer, _), f in zip(BIG, shared):
        g_shards.setdefault(name, []).append(f.reshape(w[name].shape[1:]))

    grads, deltas, new_m, new_v = {}, {}, {}, {}
    for name in g_shards:
        shape = w[name].shape
        g = jnp.stack(g_shards[name])
        grads[name] = g
        two_d = (shape[0] * shape[1], shape[2])
        d_, m_, v_ = _adamw(w[name].reshape(two_d), g.reshape(two_d), m[name].reshape(two_d), v[name].reshape(two_d),
                            name=f"adamw_{name}")
        deltas[name], new_m[name], new_v[name] = d_.reshape(shape), m_.reshape(shape), v_.reshape(shape)
    pack_small = lambda d: _pack_rows(jnp.concatenate([d[name].reshape(-1) for name in SMALL]))
    for name in SMALL:
        grads[name] = g_small[name].reshape(w[name].shape)
    d_, m_, v_ = _adamw(pack_small(w), pack_small(grads), pack_small(m), pack_small(v), name="adamw_small")
    d_, m_, v_ = d_.reshape(-1), m_.reshape(-1), v_.reshape(-1)
    off = 0
    for name in SMALL:
        n = int(np.prod(w[name].shape))
        deltas[name] = d_[off:off + n].reshape(w[name].shape)
        new_m[name] = m_[off:off + n].reshape(w[name].shape)
        new_v[name] = v_[off:off + n].reshape(w[name].shape)
        off += n

    order = ("even_norm", "even_w_in", "mla_q_lat_norm", "mla_kv_lat_norm", "mla_w_uq", "mla_w_ukv", "mla_q_norm",
             "mla_k_nope_norm", "mla_k_rope_norm", "gqa_q_norm", "gqa_k_norm", "even_w_out", "odd_norm", "odd_w_qkv",
             "swa_q_norm", "swa_k_norm", "swa_sink", "odd_w_out", "mlp_norm", "mlp_w_up", "mlp_w_down")
    outs = [loss, grad_x[None]]
    for group in (grads, deltas, new_m, new_v):
        outs += [group[name] for name in order]
    return tuple(outs)
```

```python
import functools
import math

import numpy as np
import jax
import jax.numpy as jnp
from jax import lax
from jax.experimental import pallas as pl
from jax.experimental.pallas import tpu as pltpu

F32 = jnp.float32
BF16 = jnp.bfloat16
MESH = pl.DeviceIdType.MESH

VMEM_BYTES_V7X = 64 * 1024 * 1024
LANES = 128
SUBLANES_BF16 = 16

GRID_W = 64
NORM_EPS = 1e-6
ROPE_THETA = 500000.0
AXIAL_THETA = 10000.0
MLA_HEADS = 8
MLA_Q_LORA = 512
MLA_KV_LORA = 256
MLA_NOPE = 128
MLA_ROPE = 64
MLA_QK = MLA_NOPE + MLA_ROPE
MLA_V = 128
GQA_HEADS = 8
GQA_KV = 2
GQA_DIM = 128
SWA_HEADS = 32
SWA_KV = 4
SWA_DIM = 64
SWA_WINDOW = 128
SWA_ROT = SWA_DIM // 4
SWA_BLOCK = 128
SWA_HEAD_PARTS = 4
ADAM_LR = 0.001
ADAM_B1 = 0.9
ADAM_B2 = 0.999
ADAM_EPS = 1e-08
ADAM_WD = 0.01
ADAM_STEP = 10
N_CHIPS = 4
COMM_LANES = 1024


def _tile(dim, cap, mult=LANES):
    if dim <= cap:
        return dim
    t = (cap // mult) * mult
    while t >= mult:
        if dim % t == 0:
            return t
        t -= mult
    return dim


def _params(dims, vmem_estimate):
    limit = int(min(max(vmem_estimate * 1.25 + (4 << 20), 32 << 20), VMEM_BYTES_V7X - (6 << 20)))
    return pltpu.CompilerParams(dimension_semantics=dims, vmem_limit_bytes=limit)


def _nbytes(shape, dtype):
    return int(np.prod(shape)) * jnp.dtype(dtype).itemsize


def _mm(a, b, *, mode, name, out_dtype=F32, epi=None, extra=None, split=1, caps=(1024, 1024, 2048)):
    if mode == "nn":
        (M, K), (K2, N) = a.shape, b.shape
    elif mode == "nt":
        (M, K), (N, K2) = a.shape, b.shape
    else:
        (K, M), (K2, N) = a.shape, b.shape
    assert K == K2, (a.shape, b.shape, mode)
    assert N % split == 0
    ns = N // split
    tn, tk = _tile(ns, caps[1]), _tile(K, caps[2])
    tm = _tile(M, min(caps[0], max(LANES, caps[0] * caps[1] // tn)))
    nj_per = ns // tn
    grid = (M // tm, N // tn, K // tk)
    nk = grid[2]
    if mode == "nn":
        a_spec = pl.BlockSpec((tm, tk), lambda i, j, k: (i, k))
        b_spec = pl.BlockSpec((tk, tn), lambda i, j, k: (k, j))
        dn = (((1,), (0,)), ((), ()))
    elif mode == "nt":
        a_spec = pl.BlockSpec((tm, tk), lambda i, j, k: (i, k))
        b_spec = pl.BlockSpec((tn, tk), lambda i, j, k: (j, k))
        dn = (((1,), (1,)), ((), ()))
    else:
        a_spec = pl.BlockSpec((tk, tm), lambda i, j, k: (k, i))
        b_spec = pl.BlockSpec((tk, tn), lambda i, j, k: (k, j))
        dn = (((0,), (0,)), ((), ()))
    if split == 1:
        o_spec = pl.BlockSpec((tm, tn), lambda i, j, k: (i, j))
        o_shape = (M, N)
    else:
        o_spec = pl.BlockSpec((None, tm, tn), lambda i, j, k: (j // nj_per, i, j % nj_per))
        o_shape = (split, M, ns)
    mn_spec = pl.BlockSpec((tm, tn), lambda i, j, k: (i, j))
    in_specs, args = [a_spec, b_spec], [a, b]
    if epi in ("add", "dsqrelu"):
        in_specs.append(mn_spec)
        args.append(extra)
    if epi == "sqrelu":
        out_shape = (jax.ShapeDtypeStruct(o_shape, BF16), jax.ShapeDtypeStruct(o_shape, BF16))
        out_specs = (o_spec, o_spec)
        n_out = 2
    else:
        out_shape = jax.ShapeDtypeStruct(o_shape, out_dtype)
        out_specs = o_spec
        n_out = 1

    def body(*refs):
        a_ref, b_ref = refs[0], refs[1]
        e_ref = refs[2] if len(args) == 3 else None
        outs = refs[len(args):len(args) + n_out]

        def finish(acc):
            if epi is None:
                outs[0][...] = acc.astype(outs[0].dtype)
            elif epi == "add":
                outs[0][...] = (e_ref[...] + acc).astype(outs[0].dtype)
            elif epi == "sqrelu":
                r = jnp.maximum(acc, 0.0)
                outs[0][...] = acc.astype(BF16)
                outs[1][...] = (r * r).astype(BF16)
            else:
                u = e_ref[...].astype(F32)
                outs[0][...] = (acc * (2.0 * jnp.maximum(u, 0.0))).astype(outs[0].dtype)

        prod = lax.dot_general(a_ref[...].astype(BF16), b_ref[...].astype(BF16), dn, preferred_element_type=F32)
        if nk == 1:
            finish(prod)
            return
        acc_ref = refs[-1]
        k = pl.program_id(2)

        @pl.when(k == 0)
        def _():
            acc_ref[...] = prod

        @pl.when((k != 0) & (k != nk - 1))
        def _():
            acc_ref[...] += prod

        @pl.when(k == nk - 1)
        def _():
            finish(acc_ref[...] + prod)

    est = 2 * (_nbytes((tm, tk), a.dtype) + _nbytes((tk, tn), b.dtype)) + _nbytes((tm, tn), F32)
    est += 2 * n_out * _nbytes((tm, tn), out_dtype if n_out == 1 else BF16)
    if len(args) == 3:
        est += 2 * _nbytes((tm, tn), extra.dtype)
    est += 3 * _nbytes((tm, tn), F32)
    return pl.pallas_call(
        body, name=name, grid=grid, in_specs=in_specs, out_specs=out_specs, out_shape=out_shape,
        scratch_shapes=[] if nk == 1 else [pltpu.VMEM((tm, tn), F32)],
        compiler_params=_params(("parallel", "parallel", "arbitrary"), est),
    )(*args)


def _perm(y, p):
    hi = y.astype(BF16)
    r1 = y - hi.astype(F32)
    mid = r1.astype(BF16)
    lo = (r1 - mid.astype(F32)).astype(BF16)
    d = lambda t: jnp.dot(t, p, preferred_element_type=F32)
    return d(hi) + d(mid) + d(lo)


def _rows_tile(T, d):
    return _tile(T, 2048 if d <= 256 else 512, 128)


def _norm_fwd(x, gain, *, name, rope=None, out_dtype=BF16):
    H, T, d = x.shape
    tm = _rows_tile(T, d)
    g2 = gain.reshape(1, d).astype(F32)
    in_specs = [pl.BlockSpec((None, tm, d), lambda h, i: (h, i, 0)), pl.BlockSpec((1, d), lambda h, i: (0, 0))]
    args = [x, g2]
    if rope is not None:
        in_specs += [pl.BlockSpec((tm, d), lambda h, i: (i, 0)), pl.BlockSpec((tm, d), lambda h, i: (i, 0)),
                     pl.BlockSpec((d, d), lambda h, i: (0, 0))]
        args += list(rope)

    def body(*refs):
        x_ref, g_ref = refs[0], refs[1]
        o_ref = refs[-1]
        xv = x_ref[...]
        y = xv * lax.rsqrt(jnp.mean(xv * xv, axis=-1, keepdims=True) + NORM_EPS)
        y = y * g_ref[...]
        if rope is not None:
            c_ref, s_ref, p_ref = refs[2], refs[3], refs[4]
            y = y * c_ref[...] + _perm(y, p_ref[...]) * s_ref[...]
        o_ref[...] = y.astype(o_ref.dtype)

    est = 2 * (_nbytes((tm, max(d, LANES)), F32) * (3 if rope is not None else 1) + _nbytes((tm, max(d, LANES)), out_dtype))
    est += 6 * _nbytes((tm, max(d, LANES)), F32)
    return pl.pallas_call(
        body, name=name, grid=(H, T // tm), in_specs=in_specs,
        out_specs=pl.BlockSpec((None, tm, d), lambda h, i: (h, i, 0)),
        out_shape=jax.ShapeDtypeStruct((H, T, d), out_dtype),
        compiler_params=_params(("parallel", "parallel"), est),
    )(*args)


def _norm_bwd(x, gain, dy, *, name, rope=None, group=1, res=None, out_dtype=F32, dy_scale=None):
    H, T, d = x.shape
    assert dy.shape == (H * group, T, d), (dy.shape, x.shape, group)
    tm = _rows_tile(T, d)
    g2 = gain.reshape(1, d).astype(F32)
    in_specs = [pl.BlockSpec((None, tm, d), lambda h, i: (h, i, 0)), pl.BlockSpec((1, d), lambda h, i: (0, 0)),
                pl.BlockSpec((group, tm, d), lambda h, i: (h, i, 0))]
    args = [x, g2, dy]
    if rope is not None:
        in_specs += [pl.BlockSpec((tm, d), lambda h, i: (i, 0)), pl.BlockSpec((tm, d), lambda h, i: (i, 0)),
                     pl.BlockSpec((d, d), lambda h, i: (0, 0))]
        args += list(rope)
    if res is not None:
        assert H == 1
        in_specs.append(pl.BlockSpec((tm, d), lambda h, i: (i, 0)))
        args.append(res)
    n_in = len(args)

    def body(*refs):
        x_ref, g_ref, dy_ref = refs[0], refs[1], refs[2]
        dx_ref, dg_ref = refs[n_in], refs[n_in + 1]
        first = (pl.program_id(0) == 0) & (pl.program_id(1) == 0)

        @pl.when(first)
        def _():
            dg_ref[...] = jnp.zeros_like(dg_ref)

        dyv = dy_ref[0].astype(F32)
        for g in range(1, group):
            dyv = dyv + dy_ref[g].astype(F32)
        if dy_scale is not None:
            dyv = dyv * dy_scale
        pos = 3
        if rope is not None:
            c_ref, s_ref, p_ref = refs[3], refs[4], refs[5]
            pos = 6
            dyv = dyv * c_ref[...] + _perm(dyv * s_ref[...], p_ref[...])
        xv = x_ref[...]
        r = lax.rsqrt(jnp.mean(xv * xv, axis=-1, keepdims=True) + NORM_EPS)
        xhat = xv * r
        dg_ref[...] += jnp.sum(dyv * xhat, axis=0, keepdims=True)
        dxh = dyv * g_ref[...]
        dx = r * (dxh - xhat * jnp.mean(dxh * xhat, axis=-1, keepdims=True))
        if res is not None:
            dx = dx + refs[pos][...]
        dx_ref[...] = dx.astype(dx_ref.dtype)

    wide = max(d, LANES)
    est = 2 * _nbytes((tm, wide), F32) * (2 + group + (2 if rope is not None else 0) + (1 if res is not None else 0))
    est += 8 * _nbytes((tm, wide), F32)
    return pl.pallas_call(
        body, name=name, grid=(H, T // tm), in_specs=in_specs,
        out_specs=(pl.BlockSpec((None, tm, d), lambda h, i: (h, i, 0)), pl.BlockSpec((1, d), lambda h, i: (0, 0))),
        out_shape=(jax.ShapeDtypeStruct((H, T, d), out_dtype), jax.ShapeDtypeStruct((1, d), F32)),
        compiler_params=_params(("arbitrary", "arbitrary"), est),
    )(*args)


def _group_sum(x, group, *, name):
    HG, T, d = x.shape
    H = HG // group
    tm = _rows_tile(T, d)

    def body(x_ref, o_ref):
        acc = x_ref[0]
        for g in range(1, group):
            acc = acc + x_ref[g]
        o_ref[...] = acc

    est = 2 * (group + 1) * _nbytes((tm, max(d, LANES)), F32)
    return pl.pallas_call(
        body, name=name, grid=(H, T // tm),
        in_specs=[pl.BlockSpec((group, tm, d), lambda h, i: (h, i, 0))],
        out_specs=pl.BlockSpec((None, tm, d), lambda h, i: (h, i, 0)),
        out_shape=jax.ShapeDtypeStruct((H, T, d), F32),
        compiler_params=_params(("parallel", "parallel"), est),
    )(x)


def _delta(o, do, *, name):
    H, T, d = o.shape
    tm = _rows_tile(T, d)

    def body(o_ref, do_ref, dl_ref, dob_ref):
        dov = do_ref[...]
        dl = jnp.sum(o_ref[...] * dov, axis=-1, keepdims=True)
        dl_ref[...] = jnp.broadcast_to(dl, (tm, LANES))
        dob_ref[...] = dov.astype(BF16)

    spec = pl.BlockSpec((None, tm, d), lambda h, i: (h, i, 0))
    est = 2 * (3 * _nbytes((tm, max(d, LANES)), F32) + _nbytes((tm, LANES), F32))
    return pl.pallas_call(
        body, name=name, grid=(H, T // tm), in_specs=[spec, spec],
        out_specs=(pl.BlockSpec((None, tm, LANES), lambda h, i: (h, i, 0)), spec),
        out_shape=(jax.ShapeDtypeStruct((H, T, LANES), F32), jax.ShapeDtypeStruct((H, T, d), BF16)),
        compiler_params=_params(("parallel", "parallel"), est),
    )(o, do)


NT_DIMS = (((1,), (1,)), ((), ()))
TN_DIMS = (((0,), (0,)), ((), ()))
LOG2E = math.log2(math.e)
FLASH_CHUNK = 256
FLASH_ROW_PARTS = 4


def _flash_fwd(q, k, v, scale, *, name, gather=()):
    H, T, dk = q.shape
    Hkv, _, dv = v.shape
    G = H // Hkv
    tq = tk = _tile(T, 1024)
    tp = _tile(tq, tq // FLASH_ROW_PARTS, SUBLANES_BF16)
    nk = T // tk
    c2 = scale * LOG2E

    n_r = len(gather)
    grid = (H, T // tq, nk)
    assert n_r == 0 or H >= 2

    def body(*refs):
        q_ref, k_ref, v_ref = refs[:3]
        o_ref, lse_ref = refs[3 + n_r:5 + n_r]
        m_ref, l_ref, acc_ref = refs[5 + 2 * n_r:8 + 2 * n_r]
        hi, qi, ki = pl.program_id(0), pl.program_id(1), pl.program_id(2)
        if n_r:
            ag_start, ag_forward, ag_finish = _gather_plan(refs[3:3 + n_r], refs[5 + n_r:5 + 2 * n_r],
                                                           *refs[8 + 2 * n_r:])
            pl.when((hi == 0) & (qi == 0) & (ki == 0))(ag_start)
            pl.when((hi == grid[0] - 1) & (qi == 0) & (ki == 0))(ag_forward)

        @pl.when(ki == 0)
        def _():
            m_ref[...] = jnp.full_like(m_ref, -jnp.inf)
            l_ref[...] = jnp.zeros_like(l_ref)
            acc_ref[...] = jnp.zeros_like(acc_ref)

        kv, vv = k_ref[...], v_ref[...]
        parts = [slice(part * tp, (part + 1) * tp) for part in range(tq // tp)]
        m_prev = [m_ref[rows, :] for rows in parts]
        l_prev = [l_ref[rows, :] for rows in parts]
        a_prev = [acc_ref[rows, :] for rows in parts]
        ss = [lax.dot_general(q_ref[rows, :], kv, NT_DIMS, preferred_element_type=F32) for rows in parts]
        m_new = [jnp.maximum(m, jnp.max(s, axis=-1, keepdims=True)) for m, s in zip(m_prev, ss)]
        alpha = [jnp.exp2((m - mn) * c2) for m, mn in zip(m_prev, m_new)]
        ps = [jnp.exp2((s - mn) * c2) for s, mn in zip(ss, m_new)]
        l_new = [a * l + jnp.sum(p, axis=-1, keepdims=True) for a, l, p in zip(alpha, l_prev, ps)]
        pv = [jnp.dot(p.astype(BF16), vv, preferred_element_type=F32) for p in ps]
        for rows, mn, ln, a, acc, o in zip(parts, m_new, l_new, alpha, a_prev, pv):
            m_ref[rows, :] = mn
            l_ref[rows, :] = ln
            acc_ref[rows, :] = a * acc + o

        @pl.when(ki == nk - 1)
        def _():
            l = l_ref[...]
            o_ref[...] = acc_ref[...] / l
            lse_ref[...] = jnp.broadcast_to(m_ref[...] * scale + jnp.log(l), (tq, LANES))

        if n_r:
            pl.when((hi == grid[0] - 1) & (qi == grid[1] - 1) & (ki == grid[2] - 1))(ag_finish)

    est = 2 * (_nbytes((tq, dk), BF16) + _nbytes((tk, dk + dv), BF16) + _nbytes((tq, dv + LANES), F32))
    est += 4 * _nbytes((tq, tk), F32) + 3 * _nbytes((tq, dv + 3 * LANES), F32)
    outs = pl.pallas_call(
        body, name=name, grid=grid,
        in_specs=[pl.BlockSpec((None, tq, dk), lambda h, i, j: (h, i, 0)),
                  pl.BlockSpec((None, tk, dk), lambda h, i, j: (h // G, j, 0)),
                  pl.BlockSpec((None, tk, dv), lambda h, i, j: (h // G, j, 0))] + [HBM_SPEC] * n_r,
        out_specs=[pl.BlockSpec((None, tq, dv), lambda h, i, j: (h, i, 0)),
                   pl.BlockSpec((None, tq, LANES), lambda h, i, j: (h, i, 0))] + [HBM_SPEC] * n_r,
        out_shape=[jax.ShapeDtypeStruct((H, T, dv), F32), jax.ShapeDtypeStruct((H, T, LANES), F32)]
                  + _gathered_shapes(gather),
        scratch_shapes=[pltpu.VMEM((tq, 1), F32), pltpu.VMEM((tq, 1), F32), pltpu.VMEM((tq, dv), F32)]
                       + (_gather_scratch(n_r) if n_r else []),
        compiler_params=_params(("arbitrary",) * 3 if n_r else ("parallel", "parallel", "arbitrary"), est),
    )(q, k, v, *gather)
    return (outs[0], outs[1], outs[2:]) if n_r else (outs[0], outs[1])


def _flash_bwd(q, k, v, do, lse, delta, scale, *, name, scatter=()):
    H, T, dk = q.shape
    Hkv, _, dv = v.shape
    G = H // Hkv
    tq, tk = _tile(T, 1024), _tile(T, 1024)
    tc = _tile(tk, FLASH_CHUNK)
    c2 = scale * LOG2E

    n_r = len(scatter)
    grid = (H, T // tk, T // tq)

    def body(*refs):
        q_ref, k_ref, v_ref, do_ref, lse_ref, dl_ref = refs[:6]
        dq_ref, dk_ref, dv_ref = refs[6 + n_r:9 + n_r]
        hi, ki, qi = pl.program_id(0), pl.program_id(1), pl.program_id(2)
        if n_r:
            sc_start, sc_finish = _scatter_plan(refs[6:6 + n_r], refs[9 + n_r:9 + 2 * n_r], *refs[9 + 2 * n_r:])
            pl.when((hi == 0) & (ki == 0) & (qi == 0))(sc_start)
        rows = pl.ds(pl.multiple_of(qi * tq, tq), tq)

        @pl.when(qi == 0)
        def _():
            dk_ref[...] = jnp.zeros_like(dk_ref)
            dv_ref[...] = jnp.zeros_like(dv_ref)

        @pl.when(ki == 0)
        def _():
            dq_ref[rows, :] = jnp.zeros((tq, dk), F32)

        qv, dov = q_ref[...], do_ref[...]
        lse2 = lse_ref[:, :1] * LOG2E
        dl = dl_ref[:, :1]
        chunks = [slice(c * tc, (c + 1) * tc) for c in range(tk // tc)]
        kcs = [k_ref[ks, :] for ks in chunks]
        vcs = [v_ref[ks, :] for ks in chunks]
        dv_old = [dv_ref[ks, :] for ks in chunks]
        dk_old = [dk_ref[ks, :] for ks in chunks]
        dq_old = dq_ref[rows, :]
        ss = [lax.dot_general(qv, kc, NT_DIMS, preferred_element_type=F32) for kc in kcs]
        dps = [lax.dot_general(dov, vc, NT_DIMS, preferred_element_type=F32) for vc in vcs]
        ps = [jnp.exp2(s * c2 - lse2) for s in ss]
        dss = [(p * (dp - dl)).astype(BF16) for p, dp in zip(ps, dps)]
        pbs = [p.astype(BF16) for p in ps]
        dvs = [lax.dot_general(pb, dov, TN_DIMS, preferred_element_type=F32) for pb in pbs]
        dks = [lax.dot_general(ds, qv, TN_DIMS, preferred_element_type=F32) for ds in dss]
        dqs = [jnp.dot(ds, kc, preferred_element_type=F32) for ds, kc in zip(dss, kcs)]
        for ks, old, new in zip(chunks, dv_old, dvs):
            dv_ref[ks, :] = old + new
        for ks, old, new in zip(chunks, dk_old, dks):
            dk_ref[ks, :] = old + new
        dq_c = dqs[0]
        for extra in dqs[1:]:
            dq_c = dq_c + extra
        dq_ref[rows, :] = dq_old + dq_c

        if n_r:
            pl.when((hi == grid[0] - 1) & (ki == grid[1] - 1) & (qi == grid[2] - 1))(sc_finish)

    est = 2 * (_nbytes((tq, dk + dv), BF16) + _nbytes((tk, dk + dv), BF16) + 2 * _nbytes((tq, LANES), F32))
    est += 2 * (_nbytes((T, dk), F32) + _nbytes((tk, dk + dv), F32)) + 10 * _nbytes((tq, tc), F32)
    outs = pl.pallas_call(
        body, name=name, grid=grid,
        in_specs=[pl.BlockSpec((None, tq, dk), lambda h, j, i: (h, i, 0)),
                  pl.BlockSpec((None, tk, dk), lambda h, j, i: (h // G, j, 0)),
                  pl.BlockSpec((None, tk, dv), lambda h, j, i: (h // G, j, 0)),
                  pl.BlockSpec((None, tq, dv), lambda h, j, i: (h, i, 0)),
                  pl.BlockSpec((None, tq, LANES), lambda h, j, i: (h, i, 0)),
                  pl.BlockSpec((None, tq, LANES), lambda h, j, i: (h, i, 0))] + [HBM_SPEC] * n_r,
        out_specs=[pl.BlockSpec((None, T, dk), lambda h, j, i: (h, 0, 0)),
                   pl.BlockSpec((None, tk, dk), lambda h, j, i: (h, j, 0)),
                   pl.BlockSpec((None, tk, dv), lambda h, j, i: (h, j, 0))] + [HBM_SPEC] * n_r,
        out_shape=[jax.ShapeDtypeStruct((H, T, dk), F32), jax.ShapeDtypeStruct((H, T, dk), F32),
                   jax.ShapeDtypeStruct((H, T, dv), F32)] + [jax.ShapeDtypeStruct(p.shape, p.dtype) for p in scatter],
        scratch_shapes=_scatter_scratch(n_r) if n_r else [],
        compiler_params=_params(("arbitrary", "arbitrary", "arbitrary"), est),
    )(q, k, v, do, lse, delta, *scatter)
    return (outs[0], outs[1], outs[2], outs[3:]) if n_r else tuple(outs)


def _swa_specs(G, d, n_blocks, lanes):
    B = SWA_BLOCK
    prev = lambda j, i: (j, jnp.maximum(i - 1, 0), 0)
    cur = lambda j, i: (j, i, 0)
    nxt = lambda j, i: (j, jnp.minimum(i + 1, n_blocks - 1), 0)
    q_specs = [pl.BlockSpec((G, B, lanes), m) for m in (prev, cur, nxt)]
    kv_specs = [pl.BlockSpec((None, B, d), m) for m in (prev, cur, nxt)]
    return q_specs, kv_specs, cur


def _swa_bias(i, T):
    B = SWA_BLOCK
    row = lax.broadcasted_iota(jnp.int32, (B, 3 * B), 0)
    col = lax.broadcasted_iota(jnp.int32, (B, 3 * B), 1)
    kpos = (i - 1) * B + col
    valid = (col >= row) & (col <= row + 2 * SWA_WINDOW) & (kpos >= 0) & (kpos < T)
    return jnp.where(valid, 0.0, -jnp.inf)


def _swa_fwd(q, k, v, sink, scale, *, name):
    Hq, T, d = q.shape
    Hkv = k.shape[0]
    G = Hq // Hkv
    B = SWA_BLOCK
    nb = T // B
    _, kv_specs, cur = _swa_specs(G, d, nb, d)

    def body(q_ref, k0, k1, k2, v0, v1, v2, sink_ref, o_ref, lse_ref):
        i = pl.program_id(1)
        kv = jnp.concatenate([k0[...], k1[...], k2[...]], axis=0)
        vv = jnp.concatenate([v0[...], v1[...], v2[...]], axis=0)
        bias = _swa_bias(i, T)[None]
        gp = G // SWA_HEAD_PARTS
        parts = [slice(part * gp, (part + 1) * gp) for part in range(SWA_HEAD_PARTS)]
        sks = [sink_ref[hs] for hs in parts]
        ss = [lax.dot_general(q_ref[hs].reshape(gp * B, d), kv, NT_DIMS, preferred_element_type=F32) for hs in parts]
        ss = [(s * scale).reshape(gp, B, 3 * B) + bias for s in ss]
        ms = [jnp.maximum(jnp.max(s, axis=-1, keepdims=True), sk) for s, sk in zip(ss, sks)]
        ps = [jnp.exp(s - m) for s, m in zip(ss, ms)]
        dens = [jnp.sum(p, axis=-1, keepdims=True) + jnp.exp(sk - m) for p, sk, m in zip(ps, sks, ms)]
        pns = [(p / den).reshape(gp * B, 3 * B).astype(BF16) for p, den in zip(ps, dens)]
        os_ = [jnp.dot(pn, vv, preferred_element_type=F32).reshape(gp, B, d) for pn in pns]
        for hs, o, m, den in zip(parts, os_, ms, dens):
            o_ref[hs] = o
            lse_ref[hs] = jnp.broadcast_to(m + jnp.log(den), (gp, B, LANES))

    est = 2 * (_nbytes((G, B, LANES), BF16) + 6 * _nbytes((B, LANES), BF16) + 2 * _nbytes((G, B, LANES), F32))
    est += 8 * _nbytes((G * B, 3 * B), F32)
    return pl.pallas_call(
        body, name=name, grid=(Hkv, nb),
        in_specs=[pl.BlockSpec((G, B, d), cur)] + kv_specs + kv_specs + [pl.BlockSpec((G, 1, 1), lambda j, i: (j, 0, 0))],
        out_specs=(pl.BlockSpec((G, B, d), cur), pl.BlockSpec((G, B, LANES), cur)),
        out_shape=(jax.ShapeDtypeStruct((Hq, T, d), F32), jax.ShapeDtypeStruct((Hq, T, LANES), F32)),
        compiler_params=_params(("parallel", "parallel"), est),
    )(q, k, k, k, v, v, v, sink)


def _swa_dq(q, k, v, do, lse, delta, sink, scale, *, name):
    Hq, T, d = q.shape
    Hkv = k.shape[0]
    G = Hq // Hkv
    B = SWA_BLOCK
    nb = T // B
    _, kv_specs, cur = _swa_specs(G, d, nb, d)

    def body(q_ref, do_ref, lse_ref, dl_ref, k0, k1, k2, v0, v1, v2, sink_ref, dq_ref, dsink_ref):
        i = pl.program_id(1)
        qv = q_ref[...].reshape(G * B, d)
        dov = do_ref[...].reshape(G * B, d)
        kv = jnp.concatenate([k0[...], k1[...], k2[...]], axis=0)
        vv = jnp.concatenate([v0[...], v1[...], v2[...]], axis=0)
        s = lax.dot_general(qv, kv, NT_DIMS, preferred_element_type=F32) * scale
        lse = lse_ref[:, :, :1]
        dl = dl_ref[:, :, :1]
        s = s.reshape(G, B, 3 * B) + _swa_bias(i, T)[None]
        p = jnp.exp(s - lse)
        dp = lax.dot_general(dov, vv, NT_DIMS, preferred_element_type=F32).reshape(G, B, 3 * B)
        ds = (p * (dp - dl) * scale).reshape(G * B, 3 * B).astype(BF16)
        dq_ref[...] = jnp.dot(ds, kv, preferred_element_type=F32).reshape(G, B, d)
        dsk = -jnp.sum(jnp.exp(sink_ref[...] - lse) * dl, axis=1, keepdims=True)

        @pl.when(i == 0)
        def _():
            dsink_ref[...] = jnp.zeros_like(dsink_ref)

        dsink_ref[...] += jnp.broadcast_to(dsk, (G, 1, LANES))

    est = 2 * (2 * _nbytes((G, B, LANES), BF16) + 6 * _nbytes((B, LANES), BF16) + 3 * _nbytes((G, B, LANES), F32))
    est += 8 * _nbytes((G * B, 3 * B), F32)
    return pl.pallas_call(
        body, name=name, grid=(Hkv, nb),
        in_specs=[pl.BlockSpec((G, B, d), cur), pl.BlockSpec((G, B, d), cur), pl.BlockSpec((G, B, LANES), cur),
                  pl.BlockSpec((G, B, LANES), cur)] + kv_specs + kv_specs
                 + [pl.BlockSpec((G, 1, 1), lambda j, i: (j, 0, 0))],
        out_specs=(pl.BlockSpec((G, B, d), cur), pl.BlockSpec((G, 1, LANES), lambda j, i: (j, 0, 0))),
        out_shape=(jax.ShapeDtypeStruct((Hq, T, d), F32), jax.ShapeDtypeStruct((Hq, 1, LANES), F32)),
        compiler_params=_params(("arbitrary", "arbitrary"), est),
    )(q, do, lse, delta, k, k, k, v, v, v, sink)


def _swa_dkv(q, k, v, do, lse, delta, scale, *, name):
    Hq, T, d = q.shape
    Hkv = k.shape[0]
    G = Hq // Hkv
    B = SWA_BLOCK
    nb = T // B
    q_specs, _, cur = _swa_specs(G, d, nb, d)
    l_specs, _, _ = _swa_specs(G, d, nb, LANES)

    def body(k_ref, v_ref, q0, q1, q2, d0, d1, d2, l0, l1, l2, e0, e1, e2, dk_ref, dv_ref):
        b = pl.program_id(1)
        kv, vv = k_ref[...], v_ref[...]
        dk_acc = jnp.zeros((B, d), F32)
        dv_acc = jnp.zeros((B, d), F32)
        for part, (q_ref, do_ref, lse_ref, dl_ref) in enumerate(((q0, d0, l0, e0), (q1, d1, l1, e1), (q2, d2, l2, e2))):
            qv = q_ref[...].reshape(G * B, d)
            dov = do_ref[...].reshape(G * B, d)
            s = lax.dot_general(qv, kv, NT_DIMS, preferred_element_type=F32) * scale
            row = lax.broadcasted_iota(jnp.int32, (B, B), 0)
            col = lax.broadcasted_iota(jnp.int32, (B, B), 1)
            qpos = (b + part - 1) * B + row
            diff = (part - 1) * B + row - col
            valid = (diff >= -SWA_WINDOW) & (diff <= SWA_WINDOW) & (qpos >= 0) & (qpos < T)
            bias = jnp.where(valid, 0.0, -jnp.inf)
            p = jnp.exp(s.reshape(G, B, B) + bias[None] - lse_ref[:, :, :1])
            dp = lax.dot_general(dov, vv, NT_DIMS, preferred_element_type=F32).reshape(G, B, B)
            ds = (p * (dp - dl_ref[:, :, :1]) * scale).reshape(G * B, B).astype(BF16)
            p = p.reshape(G * B, B)
            dv_acc = dv_acc + lax.dot_general(p.astype(BF16), dov, TN_DIMS, preferred_element_type=F32)
            dk_acc = dk_acc + lax.dot_general(ds, qv, TN_DIMS, preferred_element_type=F32)
        dk_ref[...] = dk_acc
        dv_ref[...] = dv_acc

    est = 2 * (6 * _nbytes((G, B, LANES), BF16) + 6 * _nbytes((G, B, LANES), F32) + 4 * _nbytes((B, LANES), F32))
    est += 10 * _nbytes((G * B, B), F32)
    kspec = pl.BlockSpec((None, B, d), cur)
    return pl.pallas_call(
        body, name=name, grid=(Hkv, nb),
        in_specs=[kspec, kspec] + q_specs + q_specs + l_specs + l_specs,
        out_specs=(kspec, kspec),
        out_shape=(jax.ShapeDtypeStruct((Hkv, T, d), F32), jax.ShapeDtypeStruct((Hkv, T, d), F32)),
        compiler_params=_params(("parallel", "parallel"), est),
    )(k, v, q, q, q, do, do, do, lse, lse, lse, delta, delta, delta)


def _loss_head(y, target, *, name):
    T, D = y.shape
    tm = _tile(T, 512)

    def body(y_ref, t_ref, dy_ref, s_ref):
        @pl.when(pl.program_id(0) == 0)
        def _():
            s_ref[...] = jnp.zeros_like(s_ref)

        e = y_ref[...] - t_ref[...]
        dy_ref[...] = e / D
        s_ref[...] += jnp.sum(jnp.sum(e * e, axis=-1, keepdims=True), axis=0, keepdims=True)

    spec = pl.BlockSpec((tm, D), lambda i: (i, 0))
    return pl.pallas_call(
        body, name=name, grid=(T // tm,), in_specs=[spec, spec],
        out_specs=(spec, pl.BlockSpec((1, 1), lambda i: (0, 0))),
        out_shape=(jax.ShapeDtypeStruct((T, D), F32), jax.ShapeDtypeStruct((1, 1), F32)),
        compiler_params=_params(("arbitrary",), 8 * _nbytes((tm, D), F32)),
    )(y, target)


def _adamw(w, g, m, v, *, name):
    R, C = w.shape
    tr = _tile(R, max(8, (1 << 19) // max(C, LANES) // 8 * 8), 8)

    def body(w_ref, g_ref, m_ref, v_ref, d_ref, nm_ref, nv_ref):
        gv = g_ref[...]
        nm = ADAM_B1 * m_ref[...] + (1.0 - ADAM_B1) * gv
        nv = ADAM_B2 * v_ref[...] + (1.0 - ADAM_B2) * jnp.square(gv)
        m_hat = nm / (1.0 - ADAM_B1 ** ADAM_STEP)
        v_hat = nv / (1.0 - ADAM_B2 ** ADAM_STEP)
        d_ref[...] = -ADAM_LR * (m_hat / (jnp.sqrt(v_hat) + ADAM_EPS) + ADAM_WD * w_ref[...])
        nm_ref[...] = nm
        nv_ref[...] = nv

    spec = pl.BlockSpec((tr, C), lambda i: (i, 0))
    sds = jax.ShapeDtypeStruct((R, C), F32)
    return pl.pallas_call(
        body, name=name, grid=(R // tr,), in_specs=[spec] * 4, out_specs=(spec,) * 3, out_shape=(sds,) * 3,
        compiler_params=_params(("parallel",), 16 * _nbytes((tr, max(C, LANES)), F32)),
    )(w, g, m, v)


def _comm_rows_tile(R, L):
    return _tile(R, max(SUBLANES_BF16, (1 << 19) // L // SUBLANES_BF16 * SUBLANES_BF16), SUBLANES_BF16)


def _pair_add(g, recv, c_idx, *, name):
    _, _, R, L = g.shape
    tr = _comm_rows_tile(R, L)

    def body(c_ref, g_ref, r_ref, o_ref):
        o_ref[...] = (g_ref[...] + r_ref[...]).astype(BF16)

    grid_spec = pltpu.PrefetchScalarGridSpec(
        num_scalar_prefetch=1, grid=(N_CHIPS, R // tr),
        in_specs=[pl.BlockSpec((None, None, tr, L), lambda j, i, c: (j, c[0], i, 0)),
                  pl.BlockSpec((None, tr, L), lambda j, i, c: (j, i, 0))],
        out_specs=pl.BlockSpec((None, tr, L), lambda j, i, c: (j, i, 0)))
    return pl.pallas_call(
        body, name=name, grid_spec=grid_spec, out_shape=jax.ShapeDtypeStruct((N_CHIPS, R, L), BF16),
        compiler_params=_params(("parallel", "parallel"), 8 * _nbytes((tr, L), F32)),
    )(c_idx, g, recv)


def _sum_chips(q, c_idx, *, name):
    _, R, L = q.shape
    tr = _comm_rows_tile(R, L)

    def body(c_ref, q_ref, o_ref):
        acc = q_ref[0].astype(F32)
        for j in range(1, N_CHIPS):
            acc = acc + q_ref[j].astype(F32)
        o_ref[...] = acc

    grid_spec = pltpu.PrefetchScalarGridSpec(
        num_scalar_prefetch=1, grid=(R // tr,),
        in_specs=[pl.BlockSpec((N_CHIPS, tr, L), lambda i, c: (0, i, 0))],
        out_specs=pl.BlockSpec((None, tr, L), lambda i, c: (c[0], i, 0)))
    return pl.pallas_call(
        body, name=name, grid_spec=grid_spec, out_shape=jax.ShapeDtypeStruct((2, R, L), F32),
        compiler_params=_params(("parallel",), 10 * _nbytes((tr, L), F32)),
    )(c_idx, q)


HBM_SPEC = pl.BlockSpec(memory_space=pltpu.HBM)


def _position():
    return lax.axis_index("x"), lax.axis_index("y"), lax.axis_index("c")


def _other_chips(x, y):
    return [(1 - x, y), (x, 1 - y), (1 - x, 1 - y)]


AG_COPIES = 7


def _gather_plan(w_refs, out_refs, send_sems, recv_sems, local_sems):
    n = len(w_refs)
    x, y, c = _position()
    me, sibling = (x, y, c), (x, y, 1 - c)
    chips = _other_chips(x, y)

    def copy(i, k, block, to, src=None):
        px, py, pc = block
        slot = out_refs[i].at[4 * px + 2 * py + pc]
        return pltpu.make_async_remote_copy(
            src_ref=slot if src is None else src, dst_ref=slot, send_sem=send_sems.at[AG_COPIES * i + k],
            recv_sem=recv_sems.at[AG_COPIES * i + k], device_id=to, device_id_type=MESH)

    def local(i):
        return pltpu.make_async_copy(w_refs[i].at[c], out_refs[i].at[4 * x + 2 * y + c], local_sems.at[i])

    def first(i):
        own = w_refs[i].at[c]
        return [copy(i, 0, me, sibling, src=own)] + [copy(i, 1 + j, me, (*chip, c), src=own)
                                                     for j, chip in enumerate(chips)]

    def passed(i):
        return [copy(i, 4 + j, (*chip, c), sibling) for j, chip in enumerate(chips)]

    def start():
        for i in range(n):
            local(i).start()
            for cp in first(i):
                cp.start()

    def forward():
        for i in range(n):
            for j, chip in enumerate(chips):
                copy(i, 1 + j, (*chip, c), me).wait_recv()
                passed(i)[j].start()

    def finish():
        for i in range(n):
            copy(i, 0, sibling, me).wait_recv()
            for j, chip in enumerate(chips):
                copy(i, 4 + j, (*chip, 1 - c), me).wait_recv()
        for i in range(n):
            for cp in first(i) + passed(i):
                cp.wait_send()
            local(i).wait()

    return start, forward, finish


def _gather_scratch(n):
    return [pltpu.SemaphoreType.DMA((AG_COPIES * n,)), pltpu.SemaphoreType.DMA((AG_COPIES * n,)),
            pltpu.SemaphoreType.DMA((n,))]


def _gathered_shapes(ws):
    return [jax.ShapeDtypeStruct((2 * N_CHIPS,) + w.shape[1:], w.dtype) for w in ws]


def _all_gather_halves(ws, *, name):
    n = len(ws)

    def body(*refs):
        for step in _gather_plan(refs[:n], refs[n:2 * n], *refs[2 * n:]):
            step()

    return pl.pallas_call(
        body, name=name, in_specs=[HBM_SPEC] * n, out_specs=[HBM_SPEC] * n, out_shape=_gathered_shapes(ws),
        scratch_shapes=_gather_scratch(n),
    )(*ws)


def _sibling_exchange(gs, *, name):
    n = len(gs)

    def body(*refs):
        g_refs, out_refs = refs[:n], refs[n:2 * n]
        send_sems, recv_sems = refs[2 * n:]
        x, y, c = _position()
        copies = [pltpu.make_async_remote_copy(
            src_ref=g_refs[i].at[j, 1 - c], dst_ref=out_refs[i].at[j], send_sem=send_sems.at[N_CHIPS * i + j],
            recv_sem=recv_sems.at[N_CHIPS * i + j], device_id=(x, y, 1 - c), device_id_type=MESH)
            for i in range(n) for j in range(N_CHIPS)]
        for cp in copies:
            cp.start()
        for cp in copies:
            cp.wait()

    return pl.pallas_call(
        body, name=name, in_specs=[HBM_SPEC] * n, out_specs=[HBM_SPEC] * n,
        out_shape=[jax.ShapeDtypeStruct((N_CHIPS,) + g.shape[2:], g.dtype) for g in gs],
        scratch_shapes=[pltpu.SemaphoreType.DMA((N_CHIPS * n,)), pltpu.SemaphoreType.DMA((N_CHIPS * n,))],
    )(*gs)


def _scatter_plan(p_refs, q_refs, send_sems, recv_sems, local_sems):
    n = len(p_refs)
    others = N_CHIPS - 1
    x, y, c = _position()
    me = 2 * x + y
    chips = _other_chips(x, y)

    def copy(i, k, chip, src_slot, dst_slot):
        return pltpu.make_async_remote_copy(
            src_ref=p_refs[i].at[src_slot], dst_ref=q_refs[i].at[dst_slot], send_sem=send_sems.at[others * i + k],
            recv_sem=recv_sems.at[others * i + k], device_id=(*chip, c), device_id_type=MESH)

    def local(i):
        return pltpu.make_async_copy(p_refs[i].at[me], q_refs[i].at[me], local_sems.at[i])

    def sends(i):
        return [copy(i, k, chip, 2 * chip[0] + chip[1], me) for k, chip in enumerate(chips)]

    def start():
        for i in range(n):
            local(i).start()
            for cp in sends(i):
                cp.start()

    def finish():
        for i in range(n):
            for k, chip in enumerate(chips):
                copy(i, k, chip, me, 2 * chip[0] + chip[1]).wait_recv()
        for i in range(n):
            for cp in sends(i):
                cp.wait_send()
            local(i).wait()

    return start, finish


def _scatter_scratch(n):
    others = N_CHIPS - 1
    return [pltpu.SemaphoreType.DMA((others * n,)), pltpu.SemaphoreType.DMA((others * n,)),
            pltpu.SemaphoreType.DMA((n,))]


def _chip_scatter(ps, *, name):
    n = len(ps)

    def body(*refs):
        for step in _scatter_plan(refs[:n], refs[n:2 * n], *refs[2 * n:]):
            step()

    return pl.pallas_call(
        body, name=name, in_specs=[HBM_SPEC] * n, out_specs=[HBM_SPEC] * n,
        out_shape=[jax.ShapeDtypeStruct(p.shape, p.dtype) for p in ps], scratch_shapes=_scatter_scratch(n),
    )(*ps)


def _sibling_share(fs, *, name):
    n = len(fs)

    def body(*refs):
        in_refs, out_refs = refs[:n], refs[n:2 * n]
        send_sems, recv_sems = refs[2 * n:]
        x, y, c = _position()

        def copy(i, half):
            return pltpu.make_async_remote_copy(
                src_ref=in_refs[i].at[half], dst_ref=out_refs[i].at[half], send_sem=send_sems.at[i],
                recv_sem=recv_sems.at[i], device_id=(x, y, 1 - c), device_id_type=MESH)

        sends = [copy(i, c) for i in range(n)]
        for cp in sends:
            cp.start()
        for i in range(n):
            copy(i, 1 - c).wait_recv()
        for cp in sends:
            cp.wait_send()

    return pl.pallas_call(
        body, name=name, in_specs=[HBM_SPEC] * n, out_specs=[HBM_SPEC] * n,
        out_shape=[jax.ShapeDtypeStruct(f.shape, f.dtype) for f in fs],
        input_output_aliases={i: i for i in range(n)},
        scratch_shapes=[pltpu.SemaphoreType.DMA((n,)), pltpu.SemaphoreType.DMA((n,))],
    )(*fs)


def _all_reduce_small(s, *, name):
    R, L = s.shape
    n_dev = 2 * N_CHIPS

    def body(s_ref, out_ref, buf, send_sems, recv_sems, local_sem):
        x, y, c = _position()
        me, sibling = (x, y, c), (x, y, 1 - c)
        chips = _other_chips(x, y)

        def slot(px, py, pc):
            return buf.at[4 * px + 2 * py + pc]

        def copy(k, block, to, src=None):
            return pltpu.make_async_remote_copy(
                src_ref=slot(*block) if src is None else src, dst_ref=slot(*block),
                send_sem=send_sems.at[k], recv_sem=recv_sems.at[k], device_id=to, device_id_type=MESH)

        mine = pltpu.make_async_copy(s_ref, slot(*me), local_sem)
        mine.start()
        first = [copy(0, me, sibling, src=s_ref)]
        first += [copy(1 + j, me, (*chip, c), src=s_ref) for j, chip in enumerate(chips)]
        for cp in first:
            cp.start()
        passed = [copy(4 + j, (*chip, c), sibling) for j, chip in enumerate(chips)]
        for j, chip in enumerate(chips):
            copy(1 + j, (*chip, c), me).wait_recv()
            passed[j].start()
        copy(0, sibling, me).wait_recv()
        for j, chip in enumerate(chips):
            copy(4 + j, (*chip, 1 - c), me).wait_recv()
        for cp in first + passed:
            cp.wait_send()
        mine.wait()
        acc = buf[0]
        for j in range(1, n_dev):
            acc = acc + buf[j]
        out_ref[...] = acc

    vmem = pl.BlockSpec(memory_space=pltpu.VMEM)
    return pl.pallas_call(
        body, name=name, in_specs=[vmem], out_specs=vmem, out_shape=jax.ShapeDtypeStruct((R, L), F32),
        scratch_shapes=[pltpu.VMEM((n_dev, R, L), F32), pltpu.SemaphoreType.DMA((7,)), pltpu.SemaphoreType.DMA((7,)),
                        pltpu.SemaphoreType.DMA],
    )(s)


def _rope_cos_sin(pos, dim, theta):
    inv = jnp.float32(theta) ** (-jnp.arange(0, dim, 2, dtype=F32) / dim)
    ang = pos.astype(F32)[:, None] * inv[None, :]
    return jnp.cos(ang), jnp.sin(ang)


def _rope_tables(T, d, segments):
    P = np.zeros((d, d), np.float32)
    c_parts, s_parts, at = [], [], 0
    for start, size, cos, sin in segments:
        half = size // 2
        if start > at:
            c_parts.append(jnp.ones((T, start - at), F32))
            s_parts.append(jnp.zeros((T, start - at), F32))
        c_parts += [cos, cos]
        s_parts += [-sin, sin]
        at = start + size
        for p in range(half):
            P[start + half + p, start + p] = 1.0
            P[start + p, start + half + p] = 1.0
    if at < d:
        c_parts.append(jnp.ones((T, d - at), F32))
        s_parts.append(jnp.zeros((T, d - at), F32))
    return jnp.concatenate(c_parts, axis=1), jnp.concatenate(s_parts, axis=1), jnp.asarray(P, BF16)


def _heads(t, H, d):
    return t.reshape(t.shape[0], H, d).transpose(1, 0, 2)


def _unheads(t):
    H, T, d = t.shape
    return t.transpose(1, 0, 2).reshape(T, H * d)


def _dw(a, b, *, name, axis):
    K, N = a.shape[1], b.shape[1]
    if axis == 1:
        return _mm(a, b, mode="tn", name=name).reshape(N_CHIPS, K // N_CHIPS, N)
    if (N // N_CHIPS) % LANES == 0:
        return _mm(a, b, mode="tn", name=name, split=N_CHIPS)
    return _mm(a, b, mode="tn", name=name).reshape(K, N_CHIPS, N // N_CHIPS).transpose(1, 0, 2)


def _mlp_fwd(x, gain, w_up, w_down, tag):
    hm = _norm_fwd(x[None], gain, name=f"mlp{tag}_norm")[0]
    u, act = _mm(hm, w_up, mode="nn", name=f"mlp{tag}_up", epi="sqrelu")
    x_out = _mm(act, w_down, mode="nn", name=f"mlp{tag}_down", epi="add", extra=x)
    return x_out, (hm, u, act)


def _mlp_bwd(x, gain, w_up, w_down, saved, dxo, tag):
    hm, u, act = saved
    du = _mm(dxo, w_down, mode="nt", name=f"mlp{tag}_dact", epi="dsqrelu", extra=u, out_dtype=BF16)
    dw_down = _dw(act, dxo, name=f"mlp{tag}_dwdown", axis=1)
    dhm = _mm(du, w_up, mode="nt", name=f"mlp{tag}_dhm")
    dw_up = _dw(hm, du, name=f"mlp{tag}_dwup", axis=2)
    dx, dgain = _norm_bwd(x[None], gain, dhm[None], name=f"mlp{tag}_dnorm", res=dxo)
    return dx[0], dgain[0], dw_up, dw_down


def _local_step(x, target, W, small, late=None):
    T, D = x.shape
    W = dict(W)
    pos = jnp.arange(T)
    mla_cos, mla_sin = _rope_cos_sin(pos, MLA_ROPE, ROPE_THETA)
    row_cos, row_sin = _rope_cos_sin(pos // GRID_W, GQA_DIM // 2, AXIAL_THETA)
    col_cos, col_sin = _rope_cos_sin(pos % GRID_W, GQA_DIM // 2, AXIAL_THETA)
    swa_cos, swa_sin = _rope_cos_sin(pos, SWA_ROT, ROPE_THETA)
    rope_q = _rope_tables(T, MLA_QK, [(MLA_NOPE, MLA_ROPE, mla_cos, mla_sin)])
    rope_kr = _rope_tables(T, MLA_ROPE, [(0, MLA_ROPE, mla_cos, mla_sin)])
    half = GQA_DIM // 2
    rope_ax = _rope_tables(T, GQA_DIM, [(0, half, row_cos, row_sin), (half, half, col_cos, col_sin)])
    rope_sw = _rope_tables(T, SWA_DIM, [(0, SWA_ROT, swa_cos, swa_sin)])
    o1 = MLA_Q_LORA
    o2 = o1 + MLA_KV_LORA
    o3 = o2 + MLA_ROPE
    o4 = o3 + GQA_HEADS * GQA_DIM
    o5 = o4 + GQA_KV * GQA_DIM
    sc_a, sc_g, sc_s = MLA_QK ** -0.5, GQA_DIM ** -0.5, SWA_DIM ** -0.5
    kv_w = MLA_NOPE + MLA_V

    h0 = _norm_fwd(x[None], small["even_norm"], name="even_norm")[0]
    proj = _mm(h0, W["even_w_in"], mode="nn", name="even_in")
    c_q, c_kv, kr_raw = proj[:, :o1], proj[:, o1:o2], proj[:, o2:o3]
    qg_raw = _heads(proj[:, o3:o4], GQA_HEADS, GQA_DIM)
    kg_raw = _heads(proj[:, o4:o5], GQA_KV, GQA_DIM)
    vg = _heads(proj[:, o5:], GQA_KV, GQA_DIM).astype(BF16)
    cqn = _norm_fwd(c_q[None], small["mla_q_lat_norm"], name="q_lat_norm")[0]
    ckvn = _norm_fwd(c_kv[None], small["mla_kv_lat_norm"], name="kv_lat_norm")[0]
    qa_raw = _heads(_mm(cqn, W["mla_w_uq"], mode="nn", name="mla_uq"), MLA_HEADS, MLA_QK)
    kv = _mm(ckvn, W["mla_w_ukv"], mode="nn", name="mla_ukv").reshape(T, MLA_HEADS, kv_w)
    kn_raw = kv[:, :, :MLA_NOPE].transpose(1, 0, 2)
    va = kv[:, :, MLA_NOPE:].transpose(1, 0, 2).astype(BF16)
    q_a = _norm_fwd(qa_raw, small["mla_q_norm"], name="mla_q_prep", rope=rope_q)
    k_n = _norm_fwd(kn_raw, small["mla_k_nope_norm"], name="mla_kn_prep")
    k_r = _norm_fwd(kr_raw[None], small["mla_k_rope_norm"], name="mla_kr_prep", rope=rope_kr)
    k_a = jnp.concatenate([k_n, jnp.broadcast_to(k_r, (MLA_HEADS, T, MLA_ROPE))], axis=-1)
    if late is None:
        o_a, lse_a = _flash_fwd(q_a, k_a, va, sc_a, name="mla_attn")
    else:
        o_a, lse_a, gathered = _flash_fwd(q_a, k_a, va, sc_a, name="mla_attn", gather=late.halves)
        W.update(late.weights(gathered))
    q_g = _norm_fwd(qg_raw, small["gqa_q_norm"], name="gqa_q_prep", rope=rope_ax)
    k_g = _norm_fwd(kg_raw, small["gqa_k_norm"], name="gqa_k_prep", rope=rope_ax)
    o_g, lse_g = _flash_fwd(q_g, k_g, vg, sc_g, name="gqa_attn")
    merged = jnp.concatenate([_unheads(o_a), _unheads(o_g)], axis=-1).astype(BF16)
    x1 = _mm(merged, W["even_w_out"], mode="nn", name="even_out", epi="add", extra=x)
    x2, mlp0 = _mlp_fwd(x1, small["mlp_norm"][0], W["mlp_w_up0"], W["mlp_w_down0"], 0)

    h1 = _norm_fwd(x2[None], small["odd_norm"], name="odd_norm")[0]
    qkv = _mm(h1, W["odd_w_qkv"], mode="nn", name="odd_qkv")
    nq, nkk = SWA_HEADS * SWA_DIM, SWA_KV * SWA_DIM
    qs_raw = _heads(qkv[:, :nq], SWA_HEADS, SWA_DIM)
    ks_raw = _heads(qkv[:, nq:nq + nkk], SWA_KV, SWA_DIM)
    vs = _heads(qkv[:, nq + nkk:], SWA_KV, SWA_DIM).astype(BF16)
    q_s = _norm_fwd(qs_raw, small["swa_q_norm"], name="swa_q_prep", rope=rope_sw)
    k_s = _norm_fwd(ks_raw, small["swa_k_norm"], name="swa_k_prep", rope=rope_sw)
    sink = small["swa_sink"].reshape(SWA_HEADS, 1, 1)
    o_s, lse_s = _swa_fwd(q_s, k_s, vs, sink, sc_s, name="swa_attn")
    o_flat = _unheads(o_s).astype(BF16)
    x3 = _mm(o_flat, W["odd_w_out"], mode="nn", name="odd_out", epi="add", extra=x2)
    x4, mlp1 = _mlp_fwd(x3, small["mlp_norm"][1], W["mlp_w_up1"], W["mlp_w_down1"], 1)

    dy, loss_sum = _loss_head(x4, target, name="loss_head")
    gW, gs = {}, {}

    dx3, dg_m1, gW["mlp_w_up1"], gW["mlp_w_down1"] = _mlp_bwd(
        x3, small["mlp_norm"][1], W["mlp_w_up1"], W["mlp_w_down1"], mlp1, dy, 1)
    d_oflat = _mm(dx3, W["odd_w_out"], mode="nt", name="odd_dout")
    gW["odd_w_out"] = _dw(o_flat, dx3, name="odd_dwout", axis=1)
    do_s = _heads(d_oflat, SWA_HEADS, SWA_DIM)
    delta_s, dob_s = _delta(o_s, do_s, name="swa_delta")
    dq_s, dsink = _swa_dq(q_s, k_s, vs, dob_s, lse_s, delta_s, sink, sc_s, name="swa_dq")
    dk_s, dv_s = _swa_dkv(q_s, k_s, vs, dob_s, lse_s, delta_s, sc_s, name="swa_dkv")
    gs["swa_sink"] = dsink[:, 0, 0]
    dqs_raw, gs["swa_q_norm"] = _norm_bwd(qs_raw, small["swa_q_norm"], dq_s, name="swa_dq_prep", rope=rope_sw)
    dks_raw, gs["swa_k_norm"] = _norm_bwd(ks_raw, small["swa_k_norm"], dk_s, name="swa_dk_prep", rope=rope_sw)
    dqkv = jnp.concatenate([_unheads(dqs_raw), _unheads(dks_raw), _unheads(dv_s)], axis=-1).astype(BF16)
    dh1 = _mm(dqkv, W["odd_w_qkv"], mode="nt", name="odd_dh")
    gW["odd_w_qkv"] = _dw(h1, dqkv, name="odd_dwqkv", axis=2)
    dx2, gs["odd_norm"] = _norm_bwd(x2[None], small["odd_norm"], dh1[None], name="odd_dnorm", res=dx3)
    dx2 = dx2[0]

    dx1, dg_m0, gW["mlp_w_up0"], gW["mlp_w_down0"] = _mlp_bwd(
        x1, small["mlp_norm"][0], W["mlp_w_up0"], W["mlp_w_down0"], mlp0, dx2, 0)
    gs["mlp_norm"] = jnp.stack([dg_m0, dg_m1])
    d_merged = _mm(dx1, W["even_w_out"], mode="nt", name="even_dout")
    gW["even_w_out"] = _dw(merged, dx1, name="even_dwout", axis=1)
    na = MLA_HEADS * MLA_V
    do_a = _heads(d_merged[:, :na], MLA_HEADS, MLA_V)
    do_g = _heads(d_merged[:, na:], GQA_HEADS, GQA_DIM)
    delta_a, dob_a = _delta(o_a, do_a, name="mla_delta")
    dq_a, dk_a, dv_a = _flash_bwd(q_a, k_a, va, dob_a, lse_a, delta_a, sc_a, name="mla_attn_bwd")
    delta_g, dob_g = _delta(o_g, do_g, name="gqa_delta")
    if late is None:
        dq_g, dk_gp, dv_gp = _flash_bwd(q_g, k_g, vg, dob_g, lse_g, delta_g, sc_g, name="gqa_attn_bwd")
    else:
        dq_g, dk_gp, dv_gp, late.scattered = _flash_bwd(q_g, k_g, vg, dob_g, lse_g, delta_g, sc_g,
                                                        name="gqa_attn_bwd", scatter=late.pairs(gW))
    grp = GQA_HEADS // GQA_KV
    dqg_raw, gs["gqa_q_norm"] = _norm_bwd(qg_raw, small["gqa_q_norm"], dq_g, name="gqa_dq_prep", rope=rope_ax,
                                          dy_scale=sc_g)
    dkg_raw, gs["gqa_k_norm"] = _norm_bwd(kg_raw, small["gqa_k_norm"], dk_gp, name="gqa_dk_prep", rope=rope_ax,
                                          group=grp, dy_scale=sc_g)
    dvg = _group_sum(dv_gp, grp, name="gqa_dv_sum")
    dqa_raw, gs["mla_q_norm"] = _norm_bwd(qa_raw, small["mla_q_norm"], dq_a, name="mla_dq_prep", rope=rope_q,
                                          dy_scale=sc_a)
    dkn_raw, gs["mla_k_nope_norm"] = _norm_bwd(kn_raw, small["mla_k_nope_norm"], dk_a[:, :, :MLA_NOPE],
                                               name="mla_dkn_prep", dy_scale=sc_a)
    dkr_raw, gs["mla_k_rope_norm"] = _norm_bwd(kr_raw[None], small["mla_k_rope_norm"], dk_a[:, :, MLA_NOPE:],
                                               name="mla_dkr_prep", rope=rope_kr, group=MLA_HEADS, dy_scale=sc_a)
    dkv = jnp.concatenate([dkn_raw.transpose(1, 0, 2), dv_a.transpose(1, 0, 2)], axis=-1)
    dkv = dkv.reshape(T, MLA_HEADS * kv_w).astype(BF16)
    dqa = _unheads(dqa_raw).astype(BF16)
    dckvn = _mm(dkv, W["mla_w_ukv"], mode="nt", name="mla_dckv")
    gW["mla_w_ukv"] = _dw(ckvn, dkv, name="mla_dwukv", axis=2)
    dcqn = _mm(dqa, W["mla_w_uq"], mode="nt", name="mla_dcq")
    gW["mla_w_uq"] = _dw(cqn, dqa, name="mla_dwuq", axis=2)
    dc_q, gs["mla_q_lat_norm"] = _norm_bwd(c_q[None], small["mla_q_lat_norm"], dcqn[None], name="q_lat_dnorm")
    dc_kv, gs["mla_kv_lat_norm"] = _norm_bwd(c_kv[None], small["mla_kv_lat_norm"], dckvn[None], name="kv_lat_dnorm")
    dproj = jnp.concatenate([dc_q[0], dc_kv[0], dkr_raw[0], _unheads(dqg_raw), _unheads(dkg_raw), _unheads(dvg)],
                            axis=-1).astype(BF16)
    dh0 = _mm(dproj, W["even_w_in"], mode="nt", name="even_dh")
    gW["even_w_in"] = _dw(h0, dproj, name="even_dwin", axis=2)
    dx0, gs["even_norm"] = _norm_bwd(x[None], small["even_norm"], dh0[None], name="even_dnorm", res=dx1)
    gs = {k: v.reshape(-1) for k, v in gs.items()}
    return loss_sum, dx0[0], gW, gs


BIG = (("even_w_in", 0, 2), ("mla_w_uq", 0, 2), ("mla_w_ukv", 0, 2), ("even_w_out", 0, 1), ("odd_w_qkv", 0, 2),
       ("odd_w_out", 0, 1), ("mlp_w_up", 0, 2), ("mlp_w_up", 1, 2), ("mlp_w_down", 0, 1), ("mlp_w_down", 1, 1))
GATHER_FIRST = ("even_w_in", "mla_w_uq", "mla_w_ukv")
GRADS_LAST = ("even_w_in", "mla_w_uq", "mla_w_ukv")
SMALL = ("even_norm", "mla_q_lat_norm", "mla_kv_lat_norm", "mla_q_norm", "mla_k_nope_norm", "mla_k_rope_norm",
         "gqa_q_norm", "gqa_k_norm", "odd_norm", "swa_q_norm", "swa_k_norm", "swa_sink", "mlp_norm")
def _pad_to(v, n):
    return v if v.shape[-1] == n else jnp.pad(v, [(0, 0)] * (v.ndim - 1) + [(0, n - v.shape[-1])])


def _big_key(name, layer, w):
    return name if w[name].shape[0] == 1 else f"{name}{layer}"


def _pack_rows(flat, rows=8):
    n = flat.shape[0]
    padded = -(-n // (rows * LANES)) * rows * LANES
    return _pad_to(flat, padded).reshape(-1, LANES)


def kernel(x, even_norm, even_w_in, mla_q_lat_norm, mla_kv_lat_norm, mla_w_uq, mla_w_ukv, mla_q_norm, mla_k_nope_norm, mla_k_rope_norm, gqa_q_norm, gqa_k_norm, even_w_out, odd_norm, odd_w_qkv, swa_q_norm, swa_k_norm, swa_sink, odd_w_out, mlp_norm, mlp_w_up, mlp_w_down, loss_target, m_even_norm, m_even_w_in, m_mla_q_lat_norm, m_mla_kv_lat_norm, m_mla_w_uq, m_mla_w_ukv, m_mla_q_norm, m_mla_k_nope_norm, m_mla_k_rope_norm, m_gqa_q_norm, m_gqa_k_norm, m_even_w_out, m_odd_norm, m_odd_w_qkv, m_swa_q_norm, m_swa_k_norm, m_swa_sink, m_odd_w_out, m_mlp_norm, m_mlp_w_up, m_mlp_w_down, v_even_norm, v_even_w_in, v_mla_q_lat_norm, v_mla_kv_lat_norm, v_mla_w_uq, v_mla_w_ukv, v_mla_q_norm, v_mla_k_nope_norm, v_mla_k_rope_norm, v_gqa_q_norm, v_gqa_k_norm, v_even_w_out, v_odd_norm, v_odd_w_qkv, v_swa_q_norm, v_swa_k_norm, v_swa_sink, v_odd_w_out, v_mlp_norm, v_mlp_w_up, v_mlp_w_down):
    w = dict(even_norm=even_norm, even_w_in=even_w_in, mla_q_lat_norm=mla_q_lat_norm, mla_kv_lat_norm=mla_kv_lat_norm,
             mla_w_uq=mla_w_uq, mla_w_ukv=mla_w_ukv, mla_q_norm=mla_q_norm, mla_k_nope_norm=mla_k_nope_norm,
             mla_k_rope_norm=mla_k_rope_norm, gqa_q_norm=gqa_q_norm, gqa_k_norm=gqa_k_norm, even_w_out=even_w_out,
             odd_norm=odd_norm, odd_w_qkv=odd_w_qkv, swa_q_norm=swa_q_norm, swa_k_norm=swa_k_norm, swa_sink=swa_sink,
             odd_w_out=odd_w_out, mlp_norm=mlp_norm, mlp_w_up=mlp_w_up, mlp_w_down=mlp_w_down)
    m = dict(even_norm=m_even_norm, even_w_in=m_even_w_in, mla_q_lat_norm=m_mla_q_lat_norm,
             mla_kv_lat_norm=m_mla_kv_lat_norm, mla_w_uq=m_mla_w_uq, mla_w_ukv=m_mla_w_ukv, mla_q_norm=m_mla_q_norm,
             mla_k_nope_norm=m_mla_k_nope_norm, mla_k_rope_norm=m_mla_k_rope_norm, gqa_q_norm=m_gqa_q_norm,
             gqa_k_norm=m_gqa_k_norm, even_w_out=m_even_w_out, odd_norm=m_odd_norm, odd_w_qkv=m_odd_w_qkv,
             swa_q_norm=m_swa_q_norm, swa_k_norm=m_swa_k_norm, swa_sink=m_swa_sink, odd_w_out=m_odd_w_out,
             mlp_norm=m_mlp_norm, mlp_w_up=m_mlp_w_up, mlp_w_down=m_mlp_w_down)
    v = dict(even_norm=v_even_norm, even_w_in=v_even_w_in, mla_q_lat_norm=v_mla_q_lat_norm,
             mla_kv_lat_norm=v_mla_kv_lat_norm, mla_w_uq=v_mla_w_uq, mla_w_ukv=v_mla_w_ukv, mla_q_norm=v_mla_q_norm,
             mla_k_nope_norm=v_mla_k_nope_norm, mla_k_rope_norm=v_mla_k_rope_norm, gqa_q_norm=v_gqa_q_norm,
             gqa_k_norm=v_gqa_k_norm, even_w_out=v_even_w_out, odd_norm=v_odd_norm, odd_w_qkv=v_odd_w_qkv,
             swa_q_norm=v_swa_q_norm, swa_k_norm=v_swa_k_norm, swa_sink=v_swa_sink, odd_w_out=v_odd_w_out,
             mlp_norm=v_mlp_norm, mlp_w_up=v_mlp_w_up, mlp_w_down=v_mlp_w_down)
    xi, yi, ci = _position()
    chip = 2 * xi + yi
    T, D = x.shape[1], x.shape[2]

    c_idx = ci.reshape(1).astype(jnp.int32)
    key_of = lambda entry: _big_key(entry[0], entry[1], w)
    first_use = [e for e in BIG if e[0] in GATHER_FIRST]
    later_use = [e for e in BIG if e[0] not in GATHER_FIRST]
    early_grads = [e for e in BIG if e[0] not in GRADS_LAST]
    last_grads = [e for e in BIG if e[0] in GRADS_LAST]

    def halves_of(entries):
        out = []
        for name, layer, _ in entries:
            ks, ns = w[name].shape[1:]
            out.append(w[name][layer].astype(BF16).reshape(2, ks // 2, ns))
        return out

    def weights_of(entries, gathered):
        out = {}
        for (name, layer, axis), g in zip(entries, gathered):
            ks, ns = w[name].shape[1:]
            stacked = g.reshape(N_CHIPS, ks, ns)
            if axis == 1:
                out[_big_key(name, layer, w)] = stacked.reshape(N_CHIPS * ks, ns)
            else:
                out[_big_key(name, layer, w)] = stacked.transpose(1, 0, 2).reshape(ks, N_CHIPS * ns)
        return out

    def pair_sums(entries, gW, tag):
        g_all = []
        for entry in entries:
            _, ks, ns = gW[key_of(entry)].shape
            g_all.append(gW[key_of(entry)].reshape(N_CHIPS, 2, ks // 2, ns))
        from_sibling = _sibling_exchange(g_all, name=f"grad_sibling_exchange_{tag}")
        return [_pair_add(g, r, c_idx, name=f"grad_pair_add_{key_of(e)}") for e, g, r in zip(entries, g_all, from_sibling)]

    class _Late:
        halves = halves_of(later_use)
        scattered = None

        @staticmethod
        def weights(gathered):
            return weights_of(later_use, gathered)

        @staticmethod
        def pairs(gW):
            return pair_sums(early_grads, gW, "early")

    late = _Late()
    W = weights_of(first_use, _all_gather_halves(halves_of(first_use), name="weights_all_gather"))

    odd_full = jnp.zeros((N_CHIPS, D // N_CHIPS), F32).at[chip].set(jnp.where(ci == 0, 1.0, 0.0) * w["odd_norm"][0])
    odd_full = _all_reduce_small(_pack_rows(odd_full.reshape(-1)), name="odd_norm_gather").reshape(-1)[:D]
    small = {name: w[name][0] for name in SMALL if name not in ("mlp_norm", "odd_norm")}
    small["mlp_norm"] = w["mlp_norm"]
    small["odd_norm"] = odd_full

    loss_sum, grad_x, gW, gs = _local_step(x[0], loss_target[0], W, small, late)

    loss_local = 0.5 * loss_sum.reshape(1) / D
    small_sizes = [(name, int(gs[name].shape[0])) for name in SMALL]
    ar_in = jnp.concatenate([_pad_to(loss_local, LANES)] + [gs[name] for name in SMALL])
    ar_out = _all_reduce_small(_pack_rows(ar_in), name="small_all_reduce").reshape(-1)
    loss = ar_out[0]
    g_small, off = {}, LANES
    for name, n in small_sizes:
        g_small[name] = ar_out[off:off + n]
        off += n
    shard_d = D // N_CHIPS
    g_small["odd_norm"] = lax.dynamic_slice(g_small["odd_norm"], (chip * shard_d,), (shard_d,))

    from_chips = dict(zip(map(key_of, early_grads), late.scattered))
    last_scattered = _chip_scatter(pair_sums(last_grads, gW, "last"), name="grad_chip_scatter")
    from_chips.update(zip(map(key_of, last_grads), last_scattered))
    keys = [key_of(e) for e in BIG]
    reduced = [_sum_chips(from_chips[key], c_idx, name=f"grad_chip_sum_{key}") for key in keys]
    shared = _sibling_share(reduced, name="grad_sibling_share")
    g_shards = {}
    for (name, layer, _), f in zip(BIG, shared):
        g_shards.setdefault(name, []).append(f.reshape(w[name].shape[1:]))

    grads, deltas, new_m, new_v = {}, {}, {}, {}
    for name in g_shards:
        shape = w[name].shape
        g = jnp.stack(g_shards[name])
        grads[name] = g
        two_d = (shape[0] * shape[1], shape[2])
        d_, m_, v_ = _adamw(w[name].reshape(two_d), g.reshape(two_d), m[name].reshape(two_d), v[name].reshape(two_d),
                            name=f"adamw_{name}")
        deltas[name], new_m[name], new_v[name] = d_.reshape(shape), m_.reshape(shape), v_.reshape(shape)
    pack_small = lambda d: _pack_rows(jnp.concatenate([d[name].reshape(-1) for name in SMALL]))
    for name in SMALL:
        grads[name] = g_small[name].reshape(w[name].shape)
    d_, m_, v_ = _adamw(pack_small(w), pack_small(grads), pack_small(m), pack_small(v), name="adamw_small")
    d_, m_, v_ = d_.reshape(-1), m_.reshape(-1), v_.reshape(-1)
    off = 0
    for name in SMALL:
        n = int(np.prod(w[name].shape))
        deltas[name] = d_[off:off + n].reshape(w[name].shape)
        new_m[name] = m_[off:off + n].reshape(w[name].shape)
        new_v[name] = v_[off:off + n].reshape(w[name].shape)
        off += n

    order = ("even_norm", "even_w_in", "mla_q_lat_norm", "mla_kv_lat_norm", "mla_w_uq", "mla_w_ukv", "mla_q_norm",
             "mla_k_nope_norm", "mla_k_rope_norm", "gqa_q_norm", "gqa_k_norm", "even_w_out", "odd_norm", "odd_w_qkv",
             "swa_q_norm", "swa_k_norm", "swa_sink", "odd_w_out", "mlp_norm", "mlp_w_up", "mlp_w_down")
    outs = [loss, grad_x[None]]
    for group in (grads, deltas, new_m, new_v):
        outs += [group[name] for name in order]
    return tuple(outs)
```

```python
import functools
import math

import numpy as np
import jax
import jax.numpy as jnp
from jax import lax
from jax.experimental import pallas as pl
from jax.experimental.pallas import tpu as pltpu

F32 = jnp.float32
BF16 = jnp.bfloat16
MESH = pl.DeviceIdType.MESH

VMEM_BYTES_V7X = 64 * 1024 * 1024
LANES = 128
SUBLANES_BF16 = 16

GRID_W = 64
NORM_EPS = 1e-6
ROPE_THETA = 500000.0
AXIAL_THETA = 10000.0
MLA_HEADS = 8
MLA_Q_LORA = 512
MLA_KV_LORA = 256
MLA_NOPE = 128
MLA_ROPE = 64
MLA_QK = MLA_NOPE + MLA_ROPE
MLA_V = 128
GQA_HEADS = 8
GQA_KV = 2
GQA_DIM = 128
SWA_HEADS = 32
SWA_KV = 4
SWA_DIM = 64
SWA_WINDOW = 128
SWA_ROT = SWA_DIM // 4
SWA_BLOCK = 128
SWA_HEAD_PARTS = 2
ADAM_LR = 0.001
ADAM_B1 = 0.9
ADAM_B2 = 0.999
ADAM_EPS = 1e-08
ADAM_WD = 0.01
ADAM_STEP = 10
N_CHIPS = 4
COMM_LANES = 1024


def _tile(dim, cap, mult=LANES):
    if dim <= cap:
        return dim
    t = (cap // mult) * mult
    while t >= mult:
        if dim % t == 0:
            return t
        t -= mult
    return dim


def _params(dims, vmem_estimate):
    limit = int(min(max(vmem_estimate * 1.25 + (4 << 20), 32 << 20), VMEM_BYTES_V7X - (6 << 20)))
    return pltpu.CompilerParams(dimension_semantics=dims, vmem_limit_bytes=limit)


def _nbytes(shape, dtype):
    return int(np.prod(shape)) * jnp.dtype(dtype).itemsize


def _mm(a, b, *, mode, name, out_dtype=F32, epi=None, extra=None, split=1, caps=(1024, 1024, 2048)):
    if mode == "nn":
        (M, K), (K2, N) = a.shape, b.shape
    elif mode == "nt":
        (M, K), (N, K2) = a.shape, b.shape
    else:
        (K, M), (K2, N) = a.shape, b.shape
    assert K == K2, (a.shape, b.shape, mode)
    assert N % split == 0
    ns = N // split
    tn, tk = _tile(ns, caps[1]), _tile(K, caps[2])
    tm = _tile(M, min(caps[0], max(LANES, caps[0] * caps[1] // tn)))
    nj_per = ns // tn
    grid = (M // tm, N // tn, K // tk)
    nk = grid[2]
    if mode == "nn":
        a_spec = pl.BlockSpec((tm, tk), lambda i, j, k: (i, k))
        b_spec = pl.BlockSpec((tk, tn), lambda i, j, k: (k, j))
        dn = (((1,), (0,)), ((), ()))
    elif mode == "nt":
        a_spec = pl.BlockSpec((tm, tk), lambda i, j, k: (i, k))
        b_spec = pl.BlockSpec((tn, tk), lambda i, j, k: (j, k))
        dn = (((1,), (1,)), ((), ()))
    else:
        a_spec = pl.BlockSpec((tk, tm), lambda i, j, k: (k, i))
        b_spec = pl.BlockSpec((tk, tn), lambda i, j, k: (k, j))
        dn = (((0,), (0,)), ((), ()))
    if split == 1:
        o_spec = pl.BlockSpec((tm, tn), lambda i, j, k: (i, j))
        o_shape = (M, N)
    else:
        o_spec = pl.BlockSpec((None, tm, tn), lambda i, j, k: (j // nj_per, i, j % nj_per))
        o_shape = (split, M, ns)
    mn_spec = pl.BlockSpec((tm, tn), lambda i, j, k: (i, j))
    in_specs, args = [a_spec, b_spec], [a, b]
    if epi in ("add", "dsqrelu"):
        in_specs.append(mn_spec)
        args.append(extra)
    if epi == "sqrelu":
        out_shape = (jax.ShapeDtypeStruct(o_shape, BF16), jax.ShapeDtypeStruct(o_shape, BF16))
        out_specs = (o_spec, o_spec)
        n_out = 2
    else:
        out_shape = jax.ShapeDtypeStruct(o_shape, out_dtype)
        out_specs = o_spec
        n_out = 1

    def body(*refs):
        a_ref, b_ref = refs[0], refs[1]
        e_ref = refs[2] if len(args) == 3 else None
        outs = refs[len(args):len(args) + n_out]

        def finish(acc):
            if epi is None:
                outs[0][...] = acc.astype(outs[0].dtype)
            elif epi == "add":
                outs[0][...] = (e_ref[...] + acc).astype(outs[0].dtype)
            elif epi == "sqrelu":
                r = jnp.maximum(acc, 0.0)
                outs[0][...] = acc.astype(BF16)
                outs[1][...] = (r * r).astype(BF16)
            else:
                u = e_ref[...].astype(F32)
                outs[0][...] = (acc * (2.0 * jnp.maximum(u, 0.0))).astype(outs[0].dtype)

        prod = lax.dot_general(a_ref[...].astype(BF16), b_ref[...].astype(BF16), dn, preferred_element_type=F32)
        if nk == 1:
            finish(prod)
            return
        acc_ref = refs[-1]
        k = pl.program_id(2)

        @pl.when(k == 0)
        def _():
            acc_ref[...] = prod

        @pl.when((k != 0) & (k != nk - 1))
        def _():
            acc_ref[...] += prod

        @pl.when(k == nk - 1)
        def _():
            finish(acc_ref[...] + prod)

    est = 2 * (_nbytes((tm, tk), a.dtype) + _nbytes((tk, tn), b.dtype)) + _nbytes((tm, tn), F32)
    est += 2 * n_out * _nbytes((tm, tn), out_dtype if n_out == 1 else BF16)
    if len(args) == 3:
        est += 2 * _nbytes((tm, tn), extra.dtype)
    est += 3 * _nbytes((tm, tn), F32)
    return pl.pallas_call(
        body, name=name, grid=grid, in_specs=in_specs, out_specs=out_specs, out_shape=out_shape,
        scratch_shapes=[] if nk == 1 else [pltpu.VMEM((tm, tn), F32)],
        compiler_params=_params(("parallel", "parallel", "arbitrary"), est),
    )(*args)


def _perm(y, p):
    hi = y.astype(BF16)
    lo = (y - hi.astype(F32)).astype(BF16)
    d = lambda t: jnp.dot(t, p, preferred_element_type=F32)
    return d(hi) + d(lo)


def _rows_tile(T, d):
    return _tile(T, 2048 if d <= 256 else 512, 128)


def _norm_fwd(x, gain, *, name, rope=None, out_dtype=BF16, out_scale=None):
    H, T, d = x.shape
    tm = _rows_tile(T, d)
    g2 = gain.reshape(1, d).astype(F32)
    in_specs = [pl.BlockSpec((None, tm, d), lambda h, i: (h, i, 0)), pl.BlockSpec((1, d), lambda h, i: (0, 0))]
    args = [x, g2]
    if rope is not None:
        in_specs += [pl.BlockSpec((tm, d), lambda h, i: (i, 0)), pl.BlockSpec((tm, d), lambda h, i: (i, 0)),
                     pl.BlockSpec((d, d), lambda h, i: (0, 0))]
        args += list(rope)

    def body(*refs):
        x_ref, g_ref = refs[0], refs[1]
        o_ref = refs[-1]
        xv = x_ref[...]
        y = xv * lax.rsqrt(jnp.mean(xv * xv, axis=-1, keepdims=True) + NORM_EPS)
        y = y * g_ref[...]
        if rope is not None:
            c_ref, s_ref, p_ref = refs[2], refs[3], refs[4]
            y = y * c_ref[...] + _perm(y, p_ref[...]) * s_ref[...]
        if out_scale is not None:
            y = y * out_scale
        o_ref[...] = y.astype(o_ref.dtype)

    est = 2 * (_nbytes((tm, max(d, LANES)), F32) * (3 if rope is not None else 1) + _nbytes((tm, max(d, LANES)), out_dtype))
    est += 6 * _nbytes((tm, max(d, LANES)), F32)
    return pl.pallas_call(
        body, name=name, grid=(H, T // tm), in_specs=in_specs,
        out_specs=pl.BlockSpec((None, tm, d), lambda h, i: (h, i, 0)),
        out_shape=jax.ShapeDtypeStruct((H, T, d), out_dtype),
        compiler_params=_params(("parallel", "parallel"), est),
    )(*args)


def _norm_bwd(x, gain, dy, *, name, rope=None, group=1, res=None, out_dtype=F32, dy_scale=None):
    H, T, d = x.shape
    assert dy.shape == (H * group, T, d), (dy.shape, x.shape, group)
    tm = _rows_tile(T, d)
    g2 = gain.reshape(1, d).astype(F32)
    in_specs = [pl.BlockSpec((None, tm, d), lambda h, i: (h, i, 0)), pl.BlockSpec((1, d), lambda h, i: (0, 0)),
                pl.BlockSpec((group, tm, d), lambda h, i: (h, i, 0))]
    args = [x, g2, dy]
    if rope is not None:
        in_specs += [pl.BlockSpec((tm, d), lambda h, i: (i, 0)), pl.BlockSpec((tm, d), lambda h, i: (i, 0)),
                     pl.BlockSpec((d, d), lambda h, i: (0, 0))]
        args += list(rope)
    if res is not None:
        assert H == 1
        in_specs.append(pl.BlockSpec((tm, d), lambda h, i: (i, 0)))
        args.append(res)
    n_in = len(args)

    def body(*refs):
        x_ref, g_ref, dy_ref = refs[0], refs[1], refs[2]
        dx_ref, dg_ref = refs[n_in], refs[n_in + 1]
        first = (pl.program_id(0) == 0) & (pl.program_id(1) == 0)

        @pl.when(first)
        def _():
            dg_ref[...] = jnp.zeros_like(dg_ref)

        dyv = dy_ref[0].astype(F32)
        for g in range(1, group):
            dyv = dyv + dy_ref[g].astype(F32)
        if dy_scale is not None:
            dyv = dyv * dy_scale
        pos = 3
        if rope is not None:
            c_ref, s_ref, p_ref = refs[3], refs[4], refs[5]
            pos = 6
            dyv = dyv * c_ref[...] + _perm(dyv * s_ref[...], p_ref[...])
        xv = x_ref[...]
        r = lax.rsqrt(jnp.mean(xv * xv, axis=-1, keepdims=True) + NORM_EPS)
        xhat = xv * r
        dg_ref[...] += jnp.sum(dyv * xhat, axis=0, keepdims=True)
        dxh = dyv * g_ref[...]
        dx = r * (dxh - xhat * jnp.mean(dxh * xhat, axis=-1, keepdims=True))
        if res is not None:
            dx = dx + refs[pos][...]
        dx_ref[...] = dx.astype(dx_ref.dtype)

    wide = max(d, LANES)
    est = 2 * _nbytes((tm, wide), F32) * (2 + group + (2 if rope is not None else 0) + (1 if res is not None else 0))
    est += 8 * _nbytes((tm, wide), F32)
    return pl.pallas_call(
        body, name=name, grid=(H, T // tm), in_specs=in_specs,
        out_specs=(pl.BlockSpec((None, tm, d), lambda h, i: (h, i, 0)), pl.BlockSpec((1, d), lambda h, i: (0, 0))),
        out_shape=(jax.ShapeDtypeStruct((H, T, d), out_dtype), jax.ShapeDtypeStruct((1, d), F32)),
        compiler_params=_params(("arbitrary", "arbitrary"), est),
    )(*args)


def _group_sum(x, group, *, name):
    HG, T, d = x.shape
    H = HG // group
    tm = _rows_tile(T, d)

    def body(x_ref, o_ref):
        acc = x_ref[0]
        for g in range(1, group):
            acc = acc + x_ref[g]
        o_ref[...] = acc

    est = 2 * (group + 1) * _nbytes((tm, max(d, LANES)), F32)
    return pl.pallas_call(
        body, name=name, grid=(H, T // tm),
        in_specs=[pl.BlockSpec((group, tm, d), lambda h, i: (h, i, 0))],
        out_specs=pl.BlockSpec((None, tm, d), lambda h, i: (h, i, 0)),
        out_shape=jax.ShapeDtypeStruct((H, T, d), F32),
        compiler_params=_params(("parallel", "parallel"), est),
    )(x)


def _delta(o, do, *, name):
    H, T, d = o.shape
    tm = _rows_tile(T, d)

    def body(o_ref, do_ref, dl_ref, dob_ref):
        dov = do_ref[...]
        dl = jnp.sum(o_ref[...] * dov, axis=-1, keepdims=True)
        dl_ref[...] = jnp.broadcast_to(dl, (tm, LANES))
        dob_ref[...] = dov.astype(BF16)

    spec = pl.BlockSpec((None, tm, d), lambda h, i: (h, i, 0))
    est = 2 * (3 * _nbytes((tm, max(d, LANES)), F32) + _nbytes((tm, LANES), F32))
    return pl.pallas_call(
        body, name=name, grid=(H, T // tm), in_specs=[spec, spec],
        out_specs=(pl.BlockSpec((None, tm, LANES), lambda h, i: (h, i, 0)), spec),
        out_shape=(jax.ShapeDtypeStruct((H, T, LANES), F32), jax.ShapeDtypeStruct((H, T, d), BF16)),
        compiler_params=_params(("parallel", "parallel"), est),
    )(o, do)


NT_DIMS = (((1,), (1,)), ((), ()))
TN_DIMS = (((0,), (0,)), ((), ()))
LOG2E = math.log2(math.e)
FLASH_CHUNK = 256
FLASH_ROW_PARTS = 4


def _flash_fwd(q, k, v, *, name, gather=()):
    H, T, dk = q.shape
    Hkv, _, dv = v.shape
    G = H // Hkv
    tq = tk = _tile(T, 1024)
    tp = _tile(tq, tq // FLASH_ROW_PARTS, SUBLANES_BF16)
    nk = T // tk

    n_r = len(gather)
    grid = (H, T // tq, nk)
    assert n_r == 0 or H >= 2

    def body(*refs):
        q_ref, k_ref, v_ref = refs[:3]
        o_ref, lse_ref = refs[3 + n_r:5 + n_r]
        m_ref, l_ref, acc_ref = refs[5 + 2 * n_r:8 + 2 * n_r]
        hi, qi, ki = pl.program_id(0), pl.program_id(1), pl.program_id(2)
        if n_r:
            ag_start, ag_forward, ag_finish = _gather_plan(refs[3:3 + n_r], refs[5 + n_r:5 + 2 * n_r],
                                                           *refs[8 + 2 * n_r:])
            pl.when((hi == 0) & (qi == 0) & (ki == 0))(ag_start)
            pl.when((hi == grid[0] - 1) & (qi == 0) & (ki == 0))(ag_forward)

        @pl.when(ki == 0)
        def _():
            m_ref[...] = jnp.full_like(m_ref, -jnp.inf)
            l_ref[...] = jnp.zeros_like(l_ref)
            acc_ref[...] = jnp.zeros_like(acc_ref)

        kv, vv = k_ref[...], v_ref[...]
        parts = [slice(part * tp, (part + 1) * tp) for part in range(tq // tp)]
        m_prev = [m_ref[rows, :] for rows in parts]
        l_prev = [l_ref[rows, :] for rows in parts]
        a_prev = [acc_ref[rows, :] for rows in parts]
        ss = [lax.dot_general(q_ref[rows, :], kv, NT_DIMS, preferred_element_type=F32) for rows in parts]
        m_new = [jnp.maximum(m, jnp.max(s, axis=-1, keepdims=True)) for m, s in zip(m_prev, ss)]
        alpha = [jnp.exp2(m - mn) for m, mn in zip(m_prev, m_new)]
        ps = [jnp.exp2(s - mn) for s, mn in zip(ss, m_new)]
        l_new = [a * l + jnp.sum(p, axis=-1, keepdims=True) for a, l, p in zip(alpha, l_prev, ps)]
        pv = [jnp.dot(p.astype(BF16), vv, preferred_element_type=F32) for p in ps]
        for rows, mn, ln, a, acc, o in zip(parts, m_new, l_new, alpha, a_prev, pv):
            m_ref[rows, :] = mn
            l_ref[rows, :] = ln
            acc_ref[rows, :] = a * acc + o

        @pl.when(ki == nk - 1)
        def _():
            l = l_ref[...]
            o_ref[...] = acc_ref[...] / l
            lse_ref[...] = jnp.broadcast_to(m_ref[...] + jnp.log(l) * LOG2E, (tq, LANES))

        if n_r:
            pl.when((hi == grid[0] - 1) & (qi == grid[1] - 1) & (ki == grid[2] - 1))(ag_finish)

    est = 2 * (_nbytes((tq, dk), BF16) + _nbytes((tk, dk + dv), BF16) + _nbytes((tq, dv + LANES), F32))
    est += 4 * _nbytes((tq, tk), F32) + 3 * _nbytes((tq, dv + 3 * LANES), F32)
    outs = pl.pallas_call(
        body, name=name, grid=grid,
        in_specs=[pl.BlockSpec((None, tq, dk), lambda h, i, j: (h, i, 0)),
                  pl.BlockSpec((None, tk, dk), lambda h, i, j: (h // G, j, 0)),
                  pl.BlockSpec((None, tk, dv), lambda h, i, j: (h // G, j, 0))] + [HBM_SPEC] * n_r,
        out_specs=[pl.BlockSpec((None, tq, dv), lambda h, i, j: (h, i, 0)),
                   pl.BlockSpec((None, tq, LANES), lambda h, i, j: (h, i, 0))] + [HBM_SPEC] * n_r,
        out_shape=[jax.ShapeDtypeStruct((H, T, dv), F32), jax.ShapeDtypeStruct((H, T, LANES), F32)]
                  + _gathered_shapes(gather),
        scratch_shapes=[pltpu.VMEM((tq, 1), F32), pltpu.VMEM((tq, 1), F32), pltpu.VMEM((tq, dv), F32)]
                       + (_gather_scratch(n_r) if n_r else []),
        compiler_params=_params(("arbitrary",) * 3 if n_r else ("parallel", "parallel", "arbitrary"), est),
    )(q, k, v, *gather)
    return (outs[0], outs[1], outs[2:]) if n_r else (outs[0], outs[1])


def _flash_bwd(q, k, v, do, lse2, delta, *, name, rider=None):
    H, T, dk = q.shape
    Hkv, _, dv = v.shape
    G = H // Hkv
    tq, tk = _tile(T, 1024), _tile(T, 1024)
    tc = _tile(tk, FLASH_CHUNK)

    kind, carried = rider if rider is not None else (None, ())
    n_r = len(carried)
    grid = (H, T // tk, T // tq)
    if kind == "scatter":
        plan, rider_scratch = _scatter_plan, _scatter_scratch(n_r)
        rider_shapes = [jax.ShapeDtypeStruct(p.shape, p.dtype) for p in carried]
    elif kind == "exchange":
        plan, rider_scratch = _exchange_plan, _exchange_scratch(n_r)
        rider_shapes = [jax.ShapeDtypeStruct((N_CHIPS,) + g.shape[2:], g.dtype) for g in carried]
    else:
        assert kind is None
        rider_scratch, rider_shapes = [], []

    def body(*refs):
        q_ref, k_ref, v_ref, do_ref, lse_ref, dl_ref = refs[:6]
        dq_ref, dk_ref, dv_ref = refs[6 + n_r:9 + n_r]
        hi, ki, qi = pl.program_id(0), pl.program_id(1), pl.program_id(2)
        if n_r:
            rider_start, rider_finish = plan(refs[6:6 + n_r], refs[9 + n_r:9 + 2 * n_r], *refs[9 + 2 * n_r:])
            pl.when((hi == 0) & (ki == 0) & (qi == 0))(rider_start)
        rows = pl.ds(pl.multiple_of(qi * tq, tq), tq)

        @pl.when(qi == 0)
        def _():
            dk_ref[...] = jnp.zeros_like(dk_ref)
            dv_ref[...] = jnp.zeros_like(dv_ref)

        @pl.when(ki == 0)
        def _():
            dq_ref[rows, :] = jnp.zeros((tq, dk), F32)

        qv, dov = q_ref[...], do_ref[...]
        lse2 = lse_ref[:, :1]
        dl = dl_ref[:, :1]
        chunks = [slice(c * tc, (c + 1) * tc) for c in range(tk // tc)]
        kcs = [k_ref[ks, :] for ks in chunks]
        vcs = [v_ref[ks, :] for ks in chunks]
        dv_old = [dv_ref[ks, :] for ks in chunks]
        dk_old = [dk_ref[ks, :] for ks in chunks]
        dq_old = dq_ref[rows, :]
        ss = [lax.dot_general(qv, kc, NT_DIMS, preferred_element_type=F32) for kc in kcs]
        dps = [lax.dot_general(dov, vc, NT_DIMS, preferred_element_type=F32) for vc in vcs]
        ps = [jnp.exp2(s - lse2) for s in ss]
        dss = [(p * (dp - dl)).astype(BF16) for p, dp in zip(ps, dps)]
        pbs = [p.astype(BF16) for p in ps]
        dvs = [lax.dot_general(pb, dov, TN_DIMS, preferred_element_type=F32) for pb in pbs]
        dks = [lax.dot_general(ds, qv, TN_DIMS, preferred_element_type=F32) for ds in dss]
        dqs = [jnp.dot(ds, kc, preferred_element_type=F32) for ds, kc in zip(dss, kcs)]
        for ks, old, new in zip(chunks, dv_old, dvs):
            dv_ref[ks, :] = old + new
        for ks, old, new in zip(chunks, dk_old, dks):
            dk_ref[ks, :] = old + new
        dq_c = dqs[0]
        for extra in dqs[1:]:
            dq_c = dq_c + extra
        dq_ref[rows, :] = dq_old + dq_c

        if n_r:
            pl.when((hi == grid[0] - 1) & (ki == grid[1] - 1) & (qi == grid[2] - 1))(rider_finish)

    est = 2 * (_nbytes((tq, dk + dv), BF16) + _nbytes((tk, dk + dv), BF16) + 2 * _nbytes((tq, LANES), F32))
    est += 2 * (_nbytes((T, dk), F32) + _nbytes((tk, dk + dv), F32)) + 10 * _nbytes((tq, tc), F32)
    outs = pl.pallas_call(
        body, name=name, grid=grid,
        in_specs=[pl.BlockSpec((None, tq, dk), lambda h, j, i: (h, i, 0)),
                  pl.BlockSpec((None, tk, dk), lambda h, j, i: (h // G, j, 0)),
                  pl.BlockSpec((None, tk, dv), lambda h, j, i: (h // G, j, 0)),
                  pl.BlockSpec((None, tq, dv), lambda h, j, i: (h, i, 0)),
                  pl.BlockSpec((None, tq, LANES), lambda h, j, i: (h, i, 0)),
                  pl.BlockSpec((None, tq, LANES), lambda h, j, i: (h, i, 0))] + [HBM_SPEC] * n_r,
        out_specs=[pl.BlockSpec((None, T, dk), lambda h, j, i: (h, 0, 0)),
                   pl.BlockSpec((None, tk, dk), lambda h, j, i: (h, j, 0)),
                   pl.BlockSpec((None, tk, dv), lambda h, j, i: (h, j, 0))] + [HBM_SPEC] * n_r,
        out_shape=[jax.ShapeDtypeStruct((H, T, dk), F32), jax.ShapeDtypeStruct((H, T, dk), F32),
                   jax.ShapeDtypeStruct((H, T, dv), F32)] + rider_shapes,
        scratch_shapes=rider_scratch,
        compiler_params=_params(("arbitrary", "arbitrary", "arbitrary"), est),
    )(q, k, v, do, lse2, delta, *carried)
    return (outs[0], outs[1], outs[2], outs[3:]) if n_r else tuple(outs)


def _swa_specs(G, d, n_blocks, lanes):
    B = SWA_BLOCK
    prev = lambda j, i: (j, jnp.maximum(i - 1, 0), 0)
    cur = lambda j, i: (j, i, 0)
    nxt = lambda j, i: (j, jnp.minimum(i + 1, n_blocks - 1), 0)
    q_specs = [pl.BlockSpec((G, B, lanes), m) for m in (prev, cur, nxt)]
    kv_specs = [pl.BlockSpec((None, B, d), m) for m in (prev, cur, nxt)]
    return q_specs, kv_specs, cur


def _swa_bias(i, T):
    B = SWA_BLOCK
    row = lax.broadcasted_iota(jnp.int32, (B, 3 * B), 0)
    col = lax.broadcasted_iota(jnp.int32, (B, 3 * B), 1)
    kpos = (i - 1) * B + col
    valid = (col >= row) & (col <= row + 2 * SWA_WINDOW) & (kpos >= 0) & (kpos < T)
    return jnp.where(valid, 0.0, -jnp.inf)


def _swa_fwd(q, k, v, sink, scale, *, name):
    Hq, T, d = q.shape
    Hkv = k.shape[0]
    G = Hq // Hkv
    B = SWA_BLOCK
    nb = T // B
    _, kv_specs, cur = _swa_specs(G, d, nb, d)

    def body(q_ref, k0, k1, k2, v0, v1, v2, sink_ref, o_ref, lse_ref):
        i = pl.program_id(1)
        kv = jnp.concatenate([k0[...], k1[...], k2[...]], axis=0)
        vv = jnp.concatenate([v0[...], v1[...], v2[...]], axis=0)
        bias = _swa_bias(i, T)[None]
        gp = G // SWA_HEAD_PARTS
        parts = [slice(part * gp, (part + 1) * gp) for part in range(SWA_HEAD_PARTS)]
        sks = [sink_ref[hs] for hs in parts]
        ss = [lax.dot_general(q_ref[hs].reshape(gp * B, d), kv, NT_DIMS, preferred_element_type=F32) for hs in parts]
        ss = [(s * scale).reshape(gp, B, 3 * B) + bias for s in ss]
        ms = [jnp.maximum(jnp.max(s, axis=-1, keepdims=True), sk) for s, sk in zip(ss, sks)]
        ps = [jnp.exp(s - m) for s, m in zip(ss, ms)]
        dens = [jnp.sum(p, axis=-1, keepdims=True) + jnp.exp(sk - m) for p, sk, m in zip(ps, sks, ms)]
        pns = [(p / den).reshape(gp * B, 3 * B).astype(BF16) for p, den in zip(ps, dens)]
        os_ = [jnp.dot(pn, vv, preferred_element_type=F32).reshape(gp, B, d) for pn in pns]
        for hs, o, m, den in zip(parts, os_, ms, dens):
            o_ref[hs] = o
            lse_ref[hs] = jnp.broadcast_to(m + jnp.log(den), (gp, B, LANES))

    est = 2 * (_nbytes((G, B, LANES), BF16) + 6 * _nbytes((B, LANES), BF16) + 2 * _nbytes((G, B, LANES), F32))
    est += 8 * _nbytes((G * B, 3 * B), F32)
    return pl.pallas_call(
        body, name=name, grid=(Hkv, nb),
        in_specs=[pl.BlockSpec((G, B, d), cur)] + kv_specs + kv_specs + [pl.BlockSpec((G, 1, 1), lambda j, i: (j, 0, 0))],
        out_specs=(pl.BlockSpec((G, B, d), cur), pl.BlockSpec((G, B, LANES), cur)),
        out_shape=(jax.ShapeDtypeStruct((Hq, T, d), F32), jax.ShapeDtypeStruct((Hq, T, LANES), F32)),
        compiler_params=_params(("parallel", "parallel"), est),
    )(q, k, k, k, v, v, v, sink)


def _swa_dq(q, k, v, do, lse, delta, sink, scale, *, name):
    Hq, T, d = q.shape
    Hkv = k.shape[0]
    G = Hq // Hkv
    B = SWA_BLOCK
    nb = T // B
    _, kv_specs, cur = _swa_specs(G, d, nb, d)

    def body(q_ref, do_ref, lse_ref, dl_ref, k0, k1, k2, v0, v1, v2, sink_ref, dq_ref, dsink_ref):
        i = pl.program_id(1)
        kv = jnp.concatenate([k0[...], k1[...], k2[...]], axis=0)
        vv = jnp.concatenate([v0[...], v1[...], v2[...]], axis=0)
        bias = _swa_bias(i, T)[None]
        gp = G // SWA_HEAD_PARTS
        parts = [slice(part * gp, (part + 1) * gp) for part in range(SWA_HEAD_PARTS)]
        lses = [lse_ref[hs, :, :1] for hs in parts]
        dls = [dl_ref[hs, :, :1] for hs in parts]
        ss = [lax.dot_general(q_ref[hs].reshape(gp * B, d), kv, NT_DIMS, preferred_element_type=F32) for hs in parts]
        dps = [lax.dot_general(do_ref[hs].reshape(gp * B, d), vv, NT_DIMS, preferred_element_type=F32) for hs in parts]
        ps = [jnp.exp((s * scale).reshape(gp, B, 3 * B) + bias - lse) for s, lse in zip(ss, lses)]
        dss = [(p * (dp.reshape(gp, B, 3 * B) - dl) * scale).reshape(gp * B, 3 * B).astype(BF16)
               for p, dp, dl in zip(ps, dps, dls)]
        dqs = [jnp.dot(ds, kv, preferred_element_type=F32).reshape(gp, B, d) for ds in dss]
        dsks = [-jnp.sum(jnp.exp(sink_ref[hs] - lse) * dl, axis=1, keepdims=True) for hs, lse, dl in zip(parts, lses, dls)]

        @pl.when(i == 0)
        def _():
            dsink_ref[...] = jnp.zeros_like(dsink_ref)

        for hs, dq, dsk in zip(parts, dqs, dsks):
            dq_ref[hs] = dq
            dsink_ref[hs] += jnp.broadcast_to(dsk, (gp, 1, LANES))

    est = 2 * (2 * _nbytes((G, B, LANES), BF16) + 6 * _nbytes((B, LANES), BF16) + 3 * _nbytes((G, B, LANES), F32))
    est += 8 * _nbytes((G * B, 3 * B), F32)
    return pl.pallas_call(
        body, name=name, grid=(Hkv, nb),
        in_specs=[pl.BlockSpec((G, B, d), cur), pl.BlockSpec((G, B, d), cur), pl.BlockSpec((G, B, LANES), cur),
                  pl.BlockSpec((G, B, LANES), cur)] + kv_specs + kv_specs
                 + [pl.BlockSpec((G, 1, 1), lambda j, i: (j, 0, 0))],
        out_specs=(pl.BlockSpec((G, B, d), cur), pl.BlockSpec((G, 1, LANES), lambda j, i: (j, 0, 0))),
        out_shape=(jax.ShapeDtypeStruct((Hq, T, d), F32), jax.ShapeDtypeStruct((Hq, 1, LANES), F32)),
        compiler_params=_params(("arbitrary", "arbitrary"), est),
    )(q, do, lse, delta, k, k, k, v, v, v, sink)


def _swa_dkv(q, k, v, do, lse, delta, scale, *, name):
    Hq, T, d = q.shape
    Hkv = k.shape[0]
    G = Hq // Hkv
    B = SWA_BLOCK
    nb = T // B
    q_specs, _, cur = _swa_specs(G, d, nb, d)
    l_specs, _, _ = _swa_specs(G, d, nb, LANES)

    def body(k_ref, v_ref, q0, q1, q2, d0, d1, d2, l0, l1, l2, e0, e1, e2, dk_ref, dv_ref):
        b = pl.program_id(1)
        kv, vv = k_ref[...], v_ref[...]
        row = lax.broadcasted_iota(jnp.int32, (B, B), 0)
        col = lax.broadcasted_iota(jnp.int32, (B, B), 1)
        biases = []
        for part in range(3):
            qpos = (b + part - 1) * B + row
            diff = (part - 1) * B + row - col
            valid = (diff >= -SWA_WINDOW) & (diff <= SWA_WINDOW) & (qpos >= 0) & (qpos < T)
            biases.append(jnp.where(valid, 0.0, -jnp.inf)[None])
        qvs = [q_ref[...].reshape(G * B, d) for q_ref in (q0, q1, q2)]
        dovs = [do_ref[...].reshape(G * B, d) for do_ref in (d0, d1, d2)]
        lses = [lse_ref[:, :, :1] for lse_ref in (l0, l1, l2)]
        dls = [dl_ref[:, :, :1] for dl_ref in (e0, e1, e2)]
        ss = [lax.dot_general(qv, kv, NT_DIMS, preferred_element_type=F32) for qv in qvs]
        dps = [lax.dot_general(dov, vv, NT_DIMS, preferred_element_type=F32) for dov in dovs]
        ps = [jnp.exp((s * scale).reshape(G, B, B) + bias - lse) for s, bias, lse in zip(ss, biases, lses)]
        dss = [(p * (dp.reshape(G, B, B) - dl) * scale).reshape(G * B, B).astype(BF16) for p, dp, dl in zip(ps, dps, dls)]
        pbs = [p.reshape(G * B, B).astype(BF16) for p in ps]
        dvs = [lax.dot_general(pb, dov, TN_DIMS, preferred_element_type=F32) for pb, dov in zip(pbs, dovs)]
        dks = [lax.dot_general(ds, qv, TN_DIMS, preferred_element_type=F32) for ds, qv in zip(dss, qvs)]
        dk_ref[...] = dks[0] + dks[1] + dks[2]
        dv_ref[...] = dvs[0] + dvs[1] + dvs[2]

    est = 2 * (6 * _nbytes((G, B, LANES), BF16) + 6 * _nbytes((G, B, LANES), F32) + 4 * _nbytes((B, LANES), F32))
    est += 10 * _nbytes((G * B, B), F32)
    kspec = pl.BlockSpec((None, B, d), cur)
    return pl.pallas_call(
        body, name=name, grid=(Hkv, nb),
        in_specs=[kspec, kspec] + q_specs + q_specs + l_specs + l_specs,
        out_specs=(kspec, kspec),
        out_shape=(jax.ShapeDtypeStruct((Hkv, T, d), F32), jax.ShapeDtypeStruct((Hkv, T, d), F32)),
        compiler_params=_params(("parallel", "parallel"), est),
    )(k, v, q, q, q, do, do, do, lse, lse, lse, delta, delta, delta)


def _loss_head(y, target, *, name):
    T, D = y.shape
    tm = _tile(T, 512)

    def body(y_ref, t_ref, dy_ref, s_ref):
        @pl.when(pl.program_id(0) == 0)
        def _():
            s_ref[...] = jnp.zeros_like(s_ref)

        e = y_ref[...] - t_ref[...]
        dy_ref[...] = e / D
        s_ref[...] += jnp.sum(jnp.sum(e * e, axis=-1, keepdims=True), axis=0, keepdims=True)

    spec = pl.BlockSpec((tm, D), lambda i: (i, 0))
    return pl.pallas_call(
        body, name=name, grid=(T // tm,), in_specs=[spec, spec],
        out_specs=(spec, pl.BlockSpec((1, 1), lambda i: (0, 0))),
        out_shape=(jax.ShapeDtypeStruct((T, D), F32), jax.ShapeDtypeStruct((1, 1), F32)),
        compiler_params=_params(("arbitrary",), 8 * _nbytes((tm, D), F32)),
    )(y, target)


def _adamw(w, g, m, v, *, name):
    R, C = w.shape
    tr = _tile(R, max(8, (1 << 19) // max(C, LANES) // 8 * 8), 8)

    def body(w_ref, g_ref, m_ref, v_ref, d_ref, nm_ref, nv_ref):
        gv = g_ref[...]
        nm = ADAM_B1 * m_ref[...] + (1.0 - ADAM_B1) * gv
        nv = ADAM_B2 * v_ref[...] + (1.0 - ADAM_B2) * jnp.square(gv)
        m_hat = nm / (1.0 - ADAM_B1 ** ADAM_STEP)
        v_hat = nv / (1.0 - ADAM_B2 ** ADAM_STEP)
        d_ref[...] = -ADAM_LR * (m_hat / (jnp.sqrt(v_hat) + ADAM_EPS) + ADAM_WD * w_ref[...])
        nm_ref[...] = nm
        nv_ref[...] = nv

    spec = pl.BlockSpec((tr, C), lambda i: (i, 0))
    sds = jax.ShapeDtypeStruct((R, C), F32)
    return pl.pallas_call(
        body, name=name, grid=(R // tr,), in_specs=[spec] * 4, out_specs=(spec,) * 3, out_shape=(sds,) * 3,
        compiler_params=_params(("parallel",), 16 * _nbytes((tr, max(C, LANES)), F32)),
    )(w, g, m, v)


def _comm_rows_tile(R, L):
    return _tile(R, max(SUBLANES_BF16, (1 << 19) // L // SUBLANES_BF16 * SUBLANES_BF16), SUBLANES_BF16)


def _pair_add(g, recv, c_idx, *, name):
    _, _, R, L = g.shape
    tr = _comm_rows_tile(R, L)

    def body(c_ref, g_ref, r_ref, o_ref):
        o_ref[...] = (g_ref[...] + r_ref[...]).astype(BF16)

    grid_spec = pltpu.PrefetchScalarGridSpec(
        num_scalar_prefetch=1, grid=(N_CHIPS, R // tr),
        in_specs=[pl.BlockSpec((None, None, tr, L), lambda j, i, c: (j, c[0], i, 0)),
                  pl.BlockSpec((None, tr, L), lambda j, i, c: (j, i, 0))],
        out_specs=pl.BlockSpec((None, tr, L), lambda j, i, c: (j, i, 0)))
    return pl.pallas_call(
        body, name=name, grid_spec=grid_spec, out_shape=jax.ShapeDtypeStruct((N_CHIPS, R, L), BF16),
        compiler_params=_params(("parallel", "parallel"), 8 * _nbytes((tr, L), F32)),
    )(c_idx, g, recv)


def _sum_chips(q, c_idx, *, name):
    _, R, L = q.shape
    tr = _comm_rows_tile(R, L)

    def body(c_ref, q_ref, o_ref):
        acc = q_ref[0].astype(F32)
        for j in range(1, N_CHIPS):
            acc = acc + q_ref[j].astype(F32)
        o_ref[...] = acc

    grid_spec = pltpu.PrefetchScalarGridSpec(
        num_scalar_prefetch=1, grid=(R // tr,),
        in_specs=[pl.BlockSpec((N_CHIPS, tr, L), lambda i, c: (0, i, 0))],
        out_specs=pl.BlockSpec((None, tr, L), lambda i, c: (c[0], i, 0)))
    return pl.pallas_call(
        body, name=name, grid_spec=grid_spec, out_shape=jax.ShapeDtypeStruct((2, R, L), F32),
        compiler_params=_params(("parallel",), 10 * _nbytes((tr, L), F32)),
    )(c_idx, q)


HBM_SPEC = pl.BlockSpec(memory_space=pltpu.HBM)


def _position():
    return lax.axis_index("x"), lax.axis_index("y"), lax.axis_index("c")


def _other_chips(x, y):
    return [(1 - x, y), (x, 1 - y), (1 - x, 1 - y)]


AG_COPIES = 7


def _gather_plan(w_refs, out_refs, send_sems, recv_sems, local_sems):
    n = len(w_refs)
    x, y, c = _position()
    me, sibling = (x, y, c), (x, y, 1 - c)
    chips = _other_chips(x, y)

    def copy(i, k, block, to, src=None):
        px, py, pc = block
        slot = out_refs[i].at[4 * px + 2 * py + pc]
        return pltpu.make_async_remote_copy(
            src_ref=slot if src is None else src, dst_ref=slot, send_sem=send_sems.at[AG_COPIES * i + k],
            recv_sem=recv_sems.at[AG_COPIES * i + k], device_id=to, device_id_type=MESH)

    def local(i):
        return pltpu.make_async_copy(w_refs[i].at[c], out_refs[i].at[4 * x + 2 * y + c], local_sems.at[i])

    def first(i):
        own = w_refs[i].at[c]
        return [copy(i, 0, me, sibling, src=own)] + [copy(i, 1 + j, me, (*chip, c), src=own)
                                                     for j, chip in enumerate(chips)]

    def passed(i):
        return [copy(i, 4 + j, (*chip, c), sibling) for j, chip in enumerate(chips)]

    def start():
        for i in range(n):
            local(i).start()
            for cp in first(i):
                cp.start()

    def forward():
        for i in range(n):
            for j, chip in enumerate(chips):
                copy(i, 1 + j, (*chip, c), me).wait_recv()
                passed(i)[j].start()

    def finish():
        for i in range(n):
            copy(i, 0, sibling, me).wait_recv()
            for j, chip in enumerate(chips):
                copy(i, 4 + j, (*chip, 1 - c), me).wait_recv()
        for i in range(n):
            for cp in first(i) + passed(i):
                cp.wait_send()
            local(i).wait()

    return start, forward, finish


def _gather_scratch(n):
    return [pltpu.SemaphoreType.DMA((AG_COPIES * n,)), pltpu.SemaphoreType.DMA((AG_COPIES * n,)),
            pltpu.SemaphoreType.DMA((n,))]


def _gathered_shapes(ws):
    return [jax.ShapeDtypeStruct((2 * N_CHIPS,) + w.shape[1:], w.dtype) for w in ws]


def _all_gather_halves(ws, *, name):
    n = len(ws)

    def body(*refs):
        for step in _gather_plan(refs[:n], refs[n:2 * n], *refs[2 * n:]):
            step()

    return pl.pallas_call(
        body, name=name, in_specs=[HBM_SPEC] * n, out_specs=[HBM_SPEC] * n, out_shape=_gathered_shapes(ws),
        scratch_shapes=_gather_scratch(n),
    )(*ws)


def _exchange_plan(g_refs, out_refs, send_sems, recv_sems):
    n = len(g_refs)
    x, y, c = _position()

    def copies():
        return [pltpu.make_async_remote_copy(
            src_ref=g_refs[i].at[j, 1 - c], dst_ref=out_refs[i].at[j], send_sem=send_sems.at[N_CHIPS * i + j],
            recv_sem=recv_sems.at[N_CHIPS * i + j], device_id=(x, y, 1 - c), device_id_type=MESH)
            for i in range(n) for j in range(N_CHIPS)]

    def start():
        for cp in copies():
            cp.start()

    def finish():
        for cp in copies():
            cp.wait()

    return start, finish


def _exchange_scratch(n):
    return [pltpu.SemaphoreType.DMA((N_CHIPS * n,)), pltpu.SemaphoreType.DMA((N_CHIPS * n,))]


def _sibling_exchange(gs, *, name):
    n = len(gs)

    def body(*refs):
        for step in _exchange_plan(refs[:n], refs[n:2 * n], *refs[2 * n:]):
            step()

    return pl.pallas_call(
        body, name=name, in_specs=[HBM_SPEC] * n, out_specs=[HBM_SPEC] * n,
        out_shape=[jax.ShapeDtypeStruct((N_CHIPS,) + g.shape[2:], g.dtype) for g in gs],
        scratch_shapes=_exchange_scratch(n),
    )(*gs)


def _scatter_plan(p_refs, q_refs, send_sems, recv_sems, local_sems):
    n = len(p_refs)
    others = N_CHIPS - 1
    x, y, c = _position()
    me = 2 * x + y
    chips = _other_chips(x, y)

    def copy(i, k, chip, src_slot, dst_slot):
        return pltpu.make_async_remote_copy(
            src_ref=p_refs[i].at[src_slot], dst_ref=q_refs[i].at[dst_slot], send_sem=send_sems.at[others * i + k],
            recv_sem=recv_sems.at[others * i + k], device_id=(*chip, c), device_id_type=MESH)

    def local(i):
        return pltpu.make_async_copy(p_refs[i].at[me], q_refs[i].at[me], local_sems.at[i])

    def sends(i):
        return [copy(i, k, chip, 2 * chip[0] + chip[1], me) for k, chip in enumerate(chips)]

    def start():
        for i in range(n):
            local(i).start()
            for cp in sends(i):
                cp.start()

    def finish():
        for i in range(n):
            for k, chip in enumerate(chips):
                copy(i, k, chip, me, 2 * chip[0] + chip[1]).wait_recv()
        for i in range(n):
            for cp in sends(i):
                cp.wait_send()
            local(i).wait()

    return start, finish


def _scatter_scratch(n):
    others = N_CHIPS - 1
    return [pltpu.SemaphoreType.DMA((others * n,)), pltpu.SemaphoreType.DMA((others * n,)),
            pltpu.SemaphoreType.DMA((n,))]


def _chip_scatter(ps, *, name):
    n = len(ps)

    def body(*refs):
        for step in _scatter_plan(refs[:n], refs[n:2 * n], *refs[2 * n:]):
            step()

    return pl.pallas_call(
        body, name=name, in_specs=[HBM_SPEC] * n, out_specs=[HBM_SPEC] * n,
        out_shape=[jax.ShapeDtypeStruct(p.shape, p.dtype) for p in ps], scratch_shapes=_scatter_scratch(n),
    )(*ps)


def _sibling_share(fs, *, name):
    n = len(fs)

    def body(*refs):
        in_refs, out_refs = refs[:n], refs[n:2 * n]
        send_sems, recv_sems = refs[2 * n:]
        x, y, c = _position()

        def copy(i, half):
            return pltpu.make_async_remote_copy(
                src_ref=in_refs[i].at[half], dst_ref=out_refs[i].at[half], send_sem=send_sems.at[i],
                recv_sem=recv_sems.at[i], device_id=(x, y, 1 - c), device_id_type=MESH)

        sends = [copy(i, c) for i in range(n)]
        for cp in sends:
            cp.start()
        for i in range(n):
            copy(i, 1 - c).wait_recv()
        for cp in sends:
            cp.wait_send()

    return pl.pallas_call(
        body, name=name, in_specs=[HBM_SPEC] * n, out_specs=[HBM_SPEC] * n,
        out_shape=[jax.ShapeDtypeStruct(f.shape, f.dtype) for f in fs],
        input_output_aliases={i: i for i in range(n)},
        scratch_shapes=[pltpu.SemaphoreType.DMA((n,)), pltpu.SemaphoreType.DMA((n,))],
    )(*fs)


def _all_reduce_small(s, *, name):
    R, L = s.shape
    n_dev = 2 * N_CHIPS

    def body(s_ref, out_ref, buf, send_sems, recv_sems, local_sem):
        x, y, c = _position()
        me, sibling = (x, y, c), (x, y, 1 - c)
        chips = _other_chips(x, y)

        def slot(px, py, pc):
            return buf.at[4 * px + 2 * py + pc]

        def copy(k, block, to, src=None):
            return pltpu.make_async_remote_copy(
                src_ref=slot(*block) if src is None else src, dst_ref=slot(*block),
                send_sem=send_sems.at[k], recv_sem=recv_sems.at[k], device_id=to, device_id_type=MESH)

        mine = pltpu.make_async_copy(s_ref, slot(*me), local_sem)
        mine.start()
        first = [copy(0, me, sibling, src=s_ref)]
        first += [copy(1 + j, me, (*chip, c), src=s_ref) for j, chip in enumerate(chips)]
        for cp in first:
            cp.start()
        passed = [copy(4 + j, (*chip, c), sibling) for j, chip in enumerate(chips)]
        for j, chip in enumerate(chips):
            copy(1 + j, (*chip, c), me).wait_recv()
            passed[j].start()
        copy(0, sibling, me).wait_recv()
        for j, chip in enumerate(chips):
            copy(4 + j, (*chip, 1 - c), me).wait_recv()
        for cp in first + passed:
            cp.wait_send()
        mine.wait()
        acc = buf[0]
        for j in range(1, n_dev):
            acc = acc + buf[j]
        out_ref[...] = acc

    vmem = pl.BlockSpec(memory_space=pltpu.VMEM)
    return pl.pallas_call(
        body, name=name, in_specs=[vmem], out_specs=vmem, out_shape=jax.ShapeDtypeStruct((R, L), F32),
        scratch_shapes=[pltpu.VMEM((n_dev, R, L), F32), pltpu.SemaphoreType.DMA((7,)), pltpu.SemaphoreType.DMA((7,)),
                        pltpu.SemaphoreType.DMA],
    )(s)


def _rope_cos_sin(pos, dim, theta):
    inv = jnp.float32(theta) ** (-jnp.arange(0, dim, 2, dtype=F32) / dim)
    ang = pos.astype(F32)[:, None] * inv[None, :]
    return jnp.cos(ang), jnp.sin(ang)


def _rope_tables(T, d, segments):
    P = np.zeros((d, d), np.float32)
    c_parts, s_parts, at = [], [], 0
    for start, size, cos, sin in segments:
        half = size // 2
        if start > at:
            c_parts.append(jnp.ones((T, start - at), F32))
            s_parts.append(jnp.zeros((T, start - at), F32))
        c_parts += [cos, cos]
        s_parts += [-sin, sin]
        at = start + size
        for p in range(half):
            P[start + half + p, start + p] = 1.0
            P[start + p, start + half + p] = 1.0
    if at < d:
        c_parts.append(jnp.ones((T, d - at), F32))
        s_parts.append(jnp.zeros((T, d - at), F32))
    return jnp.concatenate(c_parts, axis=1), jnp.concatenate(s_parts, axis=1), jnp.asarray(P, BF16)


def _heads(t, H, d):
    return t.reshape(t.shape[0], H, d).transpose(1, 0, 2)


def _unheads(t):
    H, T, d = t.shape
    return t.transpose(1, 0, 2).reshape(T, H * d)


def _dw(a, b, *, name, axis):
    K, N = a.shape[1], b.shape[1]
    if axis == 1:
        return _mm(a, b, mode="tn", name=name).reshape(N_CHIPS, K // N_CHIPS, N)
    if (N // N_CHIPS) % LANES == 0:
        return _mm(a, b, mode="tn", name=name, split=N_CHIPS)
    return _mm(a, b, mode="tn", name=name).reshape(K, N_CHIPS, N // N_CHIPS).transpose(1, 0, 2)


def _mlp_fwd(x, gain, w_up, w_down, tag):
    hm = _norm_fwd(x[None], gain, name=f"mlp{tag}_norm")[0]
    u, act = _mm(hm, w_up, mode="nn", name=f"mlp{tag}_up", epi="sqrelu")
    x_out = _mm(act, w_down, mode="nn", name=f"mlp{tag}_down", epi="add", extra=x)
    return x_out, (hm, u, act)


def _mlp_bwd(x, gain, w_up, w_down, saved, dxo, tag):
    hm, u, act = saved
    du = _mm(dxo, w_down, mode="nt", name=f"mlp{tag}_dact", epi="dsqrelu", extra=u, out_dtype=BF16)
    dw_down = _dw(act, dxo, name=f"mlp{tag}_dwdown", axis=1)
    dhm = _mm(du, w_up, mode="nt", name=f"mlp{tag}_dhm")
    dw_up = _dw(hm, du, name=f"mlp{tag}_dwup", axis=2)
    dx, dgain = _norm_bwd(x[None], gain, dhm[None], name=f"mlp{tag}_dnorm", res=dxo)
    return dx[0], dgain[0], dw_up, dw_down


def _local_step(x, target, W, small, late=None):
    T, D = x.shape
    W = dict(W)
    pos = jnp.arange(T)
    mla_cos, mla_sin = _rope_cos_sin(pos, MLA_ROPE, ROPE_THETA)
    row_cos, row_sin = _rope_cos_sin(pos // GRID_W, GQA_DIM // 2, AXIAL_THETA)
    col_cos, col_sin = _rope_cos_sin(pos % GRID_W, GQA_DIM // 2, AXIAL_THETA)
    swa_cos, swa_sin = _rope_cos_sin(pos, SWA_ROT, ROPE_THETA)
    rope_q = _rope_tables(T, MLA_QK, [(MLA_NOPE, MLA_ROPE, mla_cos, mla_sin)])
    rope_kr = _rope_tables(T, MLA_ROPE, [(0, MLA_ROPE, mla_cos, mla_sin)])
    half = GQA_DIM // 2
    rope_ax = _rope_tables(T, GQA_DIM, [(0, half, row_cos, row_sin), (half, half, col_cos, col_sin)])
    rope_sw = _rope_tables(T, SWA_DIM, [(0, SWA_ROT, swa_cos, swa_sin)])
    o1 = MLA_Q_LORA
    o2 = o1 + MLA_KV_LORA
    o3 = o2 + MLA_ROPE
    o4 = o3 + GQA_HEADS * GQA_DIM
    o5 = o4 + GQA_KV * GQA_DIM
    sc_a, sc_g, sc_s = MLA_QK ** -0.5, GQA_DIM ** -0.5, SWA_DIM ** -0.5
    kv_w = MLA_NOPE + MLA_V

    h0 = _norm_fwd(x[None], small["even_norm"], name="even_norm")[0]
    proj = _mm(h0, W["even_w_in"], mode="nn", name="even_in")
    c_q, c_kv, kr_raw = proj[:, :o1], proj[:, o1:o2], proj[:, o2:o3]
    qg_raw = _heads(proj[:, o3:o4], GQA_HEADS, GQA_DIM)
    kg_raw = _heads(proj[:, o4:o5], GQA_KV, GQA_DIM)
    vg = _heads(proj[:, o5:], GQA_KV, GQA_DIM).astype(BF16)
    cqn = _norm_fwd(c_q[None], small["mla_q_lat_norm"], name="q_lat_norm")[0]
    ckvn = _norm_fwd(c_kv[None], small["mla_kv_lat_norm"], name="kv_lat_norm")[0]
    qa_raw = _heads(_mm(cqn, W["mla_w_uq"], mode="nn", name="mla_uq"), MLA_HEADS, MLA_QK)
    kv = _mm(ckvn, W["mla_w_ukv"], mode="nn", name="mla_ukv").reshape(T, MLA_HEADS, kv_w)
    kn_raw = kv[:, :, :MLA_NOPE].transpose(1, 0, 2)
    va = kv[:, :, MLA_NOPE:].transpose(1, 0, 2).astype(BF16)
    q_a = _norm_fwd(qa_raw, small["mla_q_norm"], name="mla_q_prep", rope=rope_q, out_scale=sc_a * LOG2E)
    k_n = _norm_fwd(kn_raw, small["mla_k_nope_norm"], name="mla_kn_prep")
    k_r = _norm_fwd(kr_raw[None], small["mla_k_rope_norm"], name="mla_kr_prep", rope=rope_kr)
    k_a = jnp.concatenate([k_n, jnp.broadcast_to(k_r, (MLA_HEADS, T, MLA_ROPE))], axis=-1)
    if late is None:
        o_a, lse_a = _flash_fwd(q_a, k_a, va, name="mla_attn")
    else:
        o_a, lse_a, gathered = _flash_fwd(q_a, k_a, va, name="mla_attn", gather=late.halves)
        W.update(late.weights(gathered))
    q_g = _norm_fwd(qg_raw, small["gqa_q_norm"], name="gqa_q_prep", rope=rope_ax, out_scale=sc_g * LOG2E)
    k_g = _norm_fwd(kg_raw, small["gqa_k_norm"], name="gqa_k_prep", rope=rope_ax)
    o_g, lse_g = _flash_fwd(q_g, k_g, vg, name="gqa_attn")
    merged = jnp.concatenate([_unheads(o_a), _unheads(o_g)], axis=-1).astype(BF16)
    x1 = _mm(merged, W["even_w_out"], mode="nn", name="even_out", epi="add", extra=x)
    x2, mlp0 = _mlp_fwd(x1, small["mlp_norm"][0], W["mlp_w_up0"], W["mlp_w_down0"], 0)

    h1 = _norm_fwd(x2[None], small["odd_norm"], name="odd_norm")[0]
    qkv = _mm(h1, W["odd_w_qkv"], mode="nn", name="odd_qkv")
    nq, nkk = SWA_HEADS * SWA_DIM, SWA_KV * SWA_DIM
    qs_raw = _heads(qkv[:, :nq], SWA_HEADS, SWA_DIM)
    ks_raw = _heads(qkv[:, nq:nq + nkk], SWA_KV, SWA_DIM)
    vs = _heads(qkv[:, nq + nkk:], SWA_KV, SWA_DIM).astype(BF16)
    q_s = _norm_fwd(qs_raw, small["swa_q_norm"], name="swa_q_prep", rope=rope_sw)
    k_s = _norm_fwd(ks_raw, small["swa_k_norm"], name="swa_k_prep", rope=rope_sw)
    sink = small["swa_sink"].reshape(SWA_HEADS, 1, 1)
    o_s, lse_s = _swa_fwd(q_s, k_s, vs, sink, sc_s, name="swa_attn")
    o_flat = _unheads(o_s).astype(BF16)
    x3 = _mm(o_flat, W["odd_w_out"], mode="nn", name="odd_out", epi="add", extra=x2)
    x4, mlp1 = _mlp_fwd(x3, small["mlp_norm"][1], W["mlp_w_up1"], W["mlp_w_down1"], 1)

    dy, loss_sum = _loss_head(x4, target, name="loss_head")
    gW, gs = {}, {}

    dx3, dg_m1, gW["mlp_w_up1"], gW["mlp_w_down1"] = _mlp_bwd(
        x3, small["mlp_norm"][1], W["mlp_w_up1"], W["mlp_w_down1"], mlp1, dy, 1)
    d_oflat = _mm(dx3, W["odd_w_out"], mode="nt", name="odd_dout")
    gW["odd_w_out"] = _dw(o_flat, dx3, name="odd_dwout", axis=1)
    do_s = _heads(d_oflat, SWA_HEADS, SWA_DIM)
    delta_s, dob_s = _delta(o_s, do_s, name="swa_delta")
    dq_s, dsink = _swa_dq(q_s, k_s, vs, dob_s, lse_s, delta_s, sink, sc_s, name="swa_dq")
    dk_s, dv_s = _swa_dkv(q_s, k_s, vs, dob_s, lse_s, delta_s, sc_s, name="swa_dkv")
    gs["swa_sink"] = dsink[:, 0, 0]
    dqs_raw, gs["swa_q_norm"] = _norm_bwd(qs_raw, small["swa_q_norm"], dq_s, name="swa_dq_prep", rope=rope_sw)
    dks_raw, gs["swa_k_norm"] = _norm_bwd(ks_raw, small["swa_k_norm"], dk_s, name="swa_dk_prep", rope=rope_sw)
    dqkv = jnp.concatenate([_unheads(dqs_raw), _unheads(dks_raw), _unheads(dv_s)], axis=-1).astype(BF16)
    dh1 = _mm(dqkv, W["odd_w_qkv"], mode="nt", name="odd_dh")
    gW["odd_w_qkv"] = _dw(h1, dqkv, name="odd_dwqkv", axis=2)
    dx2, gs["odd_norm"] = _norm_bwd(x2[None], small["odd_norm"], dh1[None], name="odd_dnorm", res=dx3)
    dx2 = dx2[0]

    dx1, dg_m0, gW["mlp_w_up0"], gW["mlp_w_down0"] = _mlp_bwd(
        x1, small["mlp_norm"][0], W["mlp_w_up0"], W["mlp_w_down0"], mlp0, dx2, 0)
    gs["mlp_norm"] = jnp.stack([dg_m0, dg_m1])
    d_merged = _mm(dx1, W["even_w_out"], mode="nt", name="even_dout")
    gW["even_w_out"] = _dw(merged, dx1, name="even_dwout", axis=1)
    na = MLA_HEADS * MLA_V
    do_a = _heads(d_merged[:, :na], MLA_HEADS, MLA_V)
    do_g = _heads(d_merged[:, na:], GQA_HEADS, GQA_DIM)
    delta_a, dob_a = _delta(o_a, do_a, name="mla_delta")
    delta_g, dob_g = _delta(o_g, do_g, name="gqa_delta")
    if late is None:
        dq_a, dk_a, dv_a = _flash_bwd(q_a, k_a, va, dob_a, lse_a, delta_a, name="mla_attn_bwd")
        dq_g, dk_gp, dv_gp = _flash_bwd(q_g, k_g, vg, dob_g, lse_g, delta_g, name="gqa_attn_bwd")
    else:
        halves = late.split(gW)
        dq_a, dk_a, dv_a, from_sibling = _flash_bwd(q_a, k_a, va, dob_a, lse_a, delta_a, name="mla_attn_bwd",
                                                    rider=("exchange", halves))
        dq_g, dk_gp, dv_gp, late.scattered = _flash_bwd(q_g, k_g, vg, dob_g, lse_g, delta_g, name="gqa_attn_bwd",
                                                        rider=("scatter", late.pairs(halves, from_sibling)))
    grp = GQA_HEADS // GQA_KV
    ln2 = 1.0 / LOG2E
    dqg_raw, gs["gqa_q_norm"] = _norm_bwd(qg_raw, small["gqa_q_norm"], dq_g, name="gqa_dq_prep", rope=rope_ax,
                                          dy_scale=sc_g)
    dkg_raw, gs["gqa_k_norm"] = _norm_bwd(kg_raw, small["gqa_k_norm"], dk_gp, name="gqa_dk_prep", rope=rope_ax,
                                          group=grp, dy_scale=ln2)
    dvg = _group_sum(dv_gp, grp, name="gqa_dv_sum")
    dqa_raw, gs["mla_q_norm"] = _norm_bwd(qa_raw, small["mla_q_norm"], dq_a, name="mla_dq_prep", rope=rope_q,
                                          dy_scale=sc_a)
    dkn_raw, gs["mla_k_nope_norm"] = _norm_bwd(kn_raw, small["mla_k_nope_norm"], dk_a[:, :, :MLA_NOPE],
                                               name="mla_dkn_prep", dy_scale=ln2)
    dkr_raw, gs["mla_k_rope_norm"] = _norm_bwd(kr_raw[None], small["mla_k_rope_norm"], dk_a[:, :, MLA_NOPE:],
                                               name="mla_dkr_prep", rope=rope_kr, group=MLA_HEADS, dy_scale=ln2)
    dkv = jnp.concatenate([dkn_raw.transpose(1, 0, 2), dv_a.transpose(1, 0, 2)], axis=-1)
    dkv = dkv.reshape(T, MLA_HEADS * kv_w).astype(BF16)
    dqa = _unheads(dqa_raw).astype(BF16)
    dckvn = _mm(dkv, W["mla_w_ukv"], mode="nt", name="mla_dckv")
    gW["mla_w_ukv"] = _dw(ckvn, dkv, name="mla_dwukv", axis=2)
    dcqn = _mm(dqa, W["mla_w_uq"], mode="nt", name="mla_dcq")
    gW["mla_w_uq"] = _dw(cqn, dqa, name="mla_dwuq", axis=2)
    dc_q, gs["mla_q_lat_norm"] = _norm_bwd(c_q[None], small["mla_q_lat_norm"], dcqn[None], name="q_lat_dnorm")
    dc_kv, gs["mla_kv_lat_norm"] = _norm_bwd(c_kv[None], small["mla_kv_lat_norm"], dckvn[None], name="kv_lat_dnorm")
    dproj = jnp.concatenate([dc_q[0], dc_kv[0], dkr_raw[0], _unheads(dqg_raw), _unheads(dkg_raw), _unheads(dvg)],
                            axis=-1).astype(BF16)
    dh0 = _mm(dproj, W["even_w_in"], mode="nt", name="even_dh")
    gW["even_w_in"] = _dw(h0, dproj, name="even_dwin", axis=2)
    dx0, gs["even_norm"] = _norm_bwd(x[None], small["even_norm"], dh0[None], name="even_dnorm", res=dx1)
    gs = {k: v.reshape(-1) for k, v in gs.items()}
    return loss_sum, dx0[0], gW, gs


BIG = (("even_w_in", 0, 2), ("mla_w_uq", 0, 2), ("mla_w_ukv", 0, 2), ("even_w_out", 0, 1), ("odd_w_qkv", 0, 2),
       ("odd_w_out", 0, 1), ("mlp_w_up", 0, 2), ("mlp_w_up", 1, 2), ("mlp_w_down", 0, 1), ("mlp_w_down", 1, 1))
GATHER_FIRST = ("even_w_in", "mla_w_uq", "mla_w_ukv")
GRADS_LAST = ("even_w_in", "mla_w_uq", "mla_w_ukv")
SMALL = ("even_norm", "mla_q_lat_norm", "mla_kv_lat_norm", "mla_q_norm", "mla_k_nope_norm", "mla_k_rope_norm",
         "gqa_q_norm", "gqa_k_norm", "odd_norm", "swa_q_norm", "swa_k_norm", "swa_sink", "mlp_norm")
def _pad_to(v, n):
    return v if v.shape[-1] == n else jnp.pad(v, [(0, 0)] * (v.ndim - 1) + [(0, n - v.shape[-1])])


def _big_key(name, layer, w):
    return name if w[name].shape[0] == 1 else f"{name}{layer}"


def _pack_rows(flat, rows=8):
    n = flat.shape[0]
    padded = -(-n // (rows * LANES)) * rows * LANES
    return _pad_to(flat, padded).reshape(-1, LANES)


def kernel(x, even_norm, even_w_in, mla_q_lat_norm, mla_kv_lat_norm, mla_w_uq, mla_w_ukv, mla_q_norm, mla_k_nope_norm, mla_k_rope_norm, gqa_q_norm, gqa_k_norm, even_w_out, odd_norm, odd_w_qkv, swa_q_norm, swa_k_norm, swa_sink, odd_w_out, mlp_norm, mlp_w_up, mlp_w_down, loss_target, m_even_norm, m_even_w_in, m_mla_q_lat_norm, m_mla_kv_lat_norm, m_mla_w_uq, m_mla_w_ukv, m_mla_q_norm, m_mla_k_nope_norm, m_mla_k_rope_norm, m_gqa_q_norm, m_gqa_k_norm, m_even_w_out, m_odd_norm, m_odd_w_qkv, m_swa_q_norm, m_swa_k_norm, m_swa_sink, m_odd_w_out, m_mlp_norm, m_mlp_w_up, m_mlp_w_down, v_even_norm, v_even_w_in, v_mla_q_lat_norm, v_mla_kv_lat_norm, v_mla_w_uq, v_mla_w_ukv, v_mla_q_norm, v_mla_k_nope_norm, v_mla_k_rope_norm, v_gqa_q_norm, v_gqa_k_norm, v_even_w_out, v_odd_norm, v_odd_w_qkv, v_swa_q_norm, v_swa_k_norm, v_swa_sink, v_odd_w_out, v_mlp_norm, v_mlp_w_up, v_mlp_w_down):
    w = dict(even_norm=even_norm, even_w_in=even_w_in, mla_q_lat_norm=mla_q_lat_norm, mla_kv_lat_norm=mla_kv_lat_norm,
             mla_w_uq=mla_w_uq, mla_w_ukv=mla_w_ukv, mla_q_norm=mla_q_norm, mla_k_nope_norm=mla_k_nope_norm,
             mla_k_rope_norm=mla_k_rope_norm, gqa_q_norm=gqa_q_norm, gqa_k_norm=gqa_k_norm, even_w_out=even_w_out,
             odd_norm=odd_norm, odd_w_qkv=odd_w_qkv, swa_q_norm=swa_q_norm, swa_k_norm=swa_k_norm, swa_sink=swa_sink,
             odd_w_out=odd_w_out, mlp_norm=mlp_norm, mlp_w_up=mlp_w_up, mlp_w_down=mlp_w_down)
    m = dict(even_norm=m_even_norm, even_w_in=m_even_w_in, mla_q_lat_norm=m_mla_q_lat_norm,
             mla_kv_lat_norm=m_mla_kv_lat_norm, mla_w_uq=m_mla_w_uq, mla_w_ukv=m_mla_w_ukv, mla_q_norm=m_mla_q_norm,
             mla_k_nope_norm=m_mla_k_nope_norm, mla_k_rope_norm=m_mla_k_rope_norm, gqa_q_norm=m_gqa_q_norm,
             gqa_k_norm=m_gqa_k_norm, even_w_out=m_even_w_out, odd_norm=m_odd_norm, odd_w_qkv=m_odd_w_qkv,
             swa_q_norm=m_swa_q_norm, swa_k_norm=m_swa_k_norm, swa_sink=m_swa_sink, odd_w_out=m_odd_w_out,
             mlp_norm=m_mlp_norm, mlp_w_up=m_mlp_w_up, mlp_w_down=m_mlp_w_down)
    v = dict(even_norm=v_even_norm, even_w_in=v_even_w_in, mla_q_lat_norm=v_mla_q_lat_norm,
             mla_kv_lat_norm=v_mla_kv_lat_norm, mla_w_uq=v_mla_w_uq, mla_w_ukv=v_mla_w_ukv, mla_q_norm=v_mla_q_norm,
             mla_k_nope_norm=v_mla_k_nope_norm, mla_k_rope_norm=v_mla_k_rope_norm, gqa_q_norm=v_gqa_q_norm,
             gqa_k_norm=v_gqa_k_norm, even_w_out=v_even_w_out, odd_norm=v_odd_norm, odd_w_qkv=v_odd_w_qkv,
             swa_q_norm=v_swa_q_norm, swa_k_norm=v_swa_k_norm, swa_sink=v_swa_sink, odd_w_out=v_odd_w_out,
             mlp_norm=v_mlp_norm, mlp_w_up=v_mlp_w_up, mlp_w_down=v_mlp_w_down)
    xi, yi, ci = _position()
    chip = 2 * xi + yi
    T, D = x.shape[1], x.shape[2]

    c_idx = ci.reshape(1).astype(jnp.int32)
    key_of = lambda entry: _big_key(entry[0], entry[1], w)
    first_use = [e for e in BIG if e[0] in GATHER_FIRST]
    later_use = [e for e in BIG if e[0] not in GATHER_FIRST]
    early_grads = [e for e in BIG if e[0] not in GRADS_LAST]
    last_grads = [e for e in BIG if e[0] in GRADS_LAST]

    def halves_of(entries):
        out = []
        for name, layer, _ in entries:
            ks, ns = w[name].shape[1:]
            out.append(w[name][layer].astype(BF16).reshape(2, ks // 2, ns))
        return out

    def weights_of(entries, gathered):
        out = {}
        for (name, layer, axis), g in zip(entries, gathered):
            ks, ns = w[name].shape[1:]
            stacked = g.reshape(N_CHIPS, ks, ns)
            if axis == 1:
                out[_big_key(name, layer, w)] = stacked.reshape(N_CHIPS * ks, ns)
            else:
                out[_big_key(name, layer, w)] = stacked.transpose(1, 0, 2).reshape(ks, N_CHIPS * ns)
        return out

    def split_halves(entries, gW):
        out = []
        for entry in entries:
            _, ks, ns = gW[key_of(entry)].shape
            out.append(gW[key_of(entry)].reshape(N_CHIPS, 2, ks // 2, ns))
        return out

    def pair_sums(entries, g_all, from_sibling):
        return [_pair_add(g, r, c_idx, name=f"grad_pair_add_{key_of(e)}") for e, g, r in zip(entries, g_all, from_sibling)]

    class _Late:
        halves = halves_of(later_use)
        scattered = None

        @staticmethod
        def weights(gathered):
            return weights_of(later_use, gathered)

        @staticmethod
        def split(gW):
            return split_halves(early_grads, gW)

        @staticmethod
        def pairs(g_all, from_sibling):
            return pair_sums(early_grads, g_all, from_sibling)

    late = _Late()
    W = weights_of(first_use, _all_gather_halves(halves_of(first_use), name="weights_all_gather"))

    odd_full = jnp.zeros((N_CHIPS, D // N_CHIPS), F32).at[chip].set(jnp.where(ci == 0, 1.0, 0.0) * w["odd_norm"][0])
    odd_full = _all_reduce_small(_pack_rows(odd_full.reshape(-1)), name="odd_norm_gather").reshape(-1)[:D]
    small = {name: w[name][0] for name in SMALL if name not in ("mlp_norm", "odd_norm")}
    small["mlp_norm"] = w["mlp_norm"]
    small["odd_norm"] = odd_full

    loss_sum, grad_x, gW, gs = _local_step(x[0], loss_target[0], W, small, late)

    loss_local = 0.5 * loss_sum.reshape(1) / D
    small_sizes = [(name, int(gs[name].shape[0])) for name in SMALL]
    ar_in = jnp.concatenate([_pad_to(loss_local, LANES)] + [gs[name] for name in SMALL])
    ar_out = _all_reduce_small(_pack_rows(ar_in), name="small_all_reduce").reshape(-1)
    loss = ar_out[0]
    g_small, off = {}, LANES
    for name, n in small_sizes:
        g_small[name] = ar_out[off:off + n]
        off += n
    shard_d = D // N_CHIPS
    g_small["odd_norm"] = lax.dynamic_slice(g_small["odd_norm"], (chip * shard_d,), (shard_d,))

    from_chips = dict(zip(map(key_of, early_grads), late.scattered))
    last_halves = split_halves(last_grads, gW)
    last_pairs = pair_sums(last_grads, last_halves, _sibling_exchange(last_halves, name="grad_sibling_exchange"))
    last_scattered = _chip_scatter(last_pairs, name="grad_chip_scatter")
    from_chips.update(zip(map(key_of, last_grads), last_scattered))
    keys = [key_of(e) for e in BIG]
    reduced = [_sum_chips(from_chips[key], c_idx, name=f"grad_chip_sum_{key}") for key in keys]
    shared = _sibling_share(reduced, name="grad_sibling_share")
    g_shards = {}
    for (name, layer, _), f in zip(BIG, shared):
        g_shards.setdefault(name, []).append(f.reshape(w[name].shape[1:]))

    grads, deltas, new_m, new_v = {}, {}, {}, {}
    for name in g_shards:
        shape = w[name].shape
        g = jnp.stack(g_shards[name])
        grads[name] = g
        two_d = (shape[0] * shape[1], shape[2])
        d_, m_, v_ = _adamw(w[name].reshape(two_d), g.reshape(two_d), m[name].reshape(two_d), v[name].reshape(two_d),
                            name=f"adamw_{name}")
        deltas[name], new_m[name], new_v[name] = d_.reshape(shape), m_.reshape(shape), v_.reshape(shape)
    pack_small = lambda d: _pack_rows(jnp.concatenate([d[name].reshape(-1) for name in SMALL]))
    for name in SMALL:
        grads[name] = g_small[name].reshape(w[name].shape)
    d_, m_, v_ = _adamw(pack_small(w), pack_small(grads), pack_small(m), pack_small(v), name="adamw_small")
    d_, m_, v_ = d_.reshape(-1), m_.reshape(-1), v_.reshape(-1)
    off = 0
    for name in SMALL:
        n = int(np.prod(w[name].shape))
        deltas[name] = d_[off:off + n].reshape(w[name].shape)
        new_m[name] = m_[off:off + n].reshape(w[name].shape)
        new_v[name] = v_[off:off + n].reshape(w[name].shape)
        off += n

    order = ("even_norm", "even_w_in", "mla_q_lat_norm", "mla_kv_lat_norm", "mla_w_uq", "mla_w_ukv", "mla_q_norm",
             "mla_k_nope_norm", "mla_k_rope_norm", "gqa_q_norm", "gqa_k_norm", "even_w_out", "odd_norm", "odd_w_qkv",
             "swa_q_norm", "swa_k_norm", "swa_sink", "odd_w_out", "mlp_norm", "mlp_w_up", "mlp_w_down")
    outs = [loss, grad_x[None]]
    for group in (grads, deltas, new_m, new_v):
        outs += [group[name] for name in order]
    return tuple(outs)
```

```python
import functools
import math

import numpy as np
import jax
import jax.numpy as jnp
from jax import lax
from jax.experimental import pallas as pl
from jax.experimental.pallas import tpu as pltpu

F32 = jnp.float32
BF16 = jnp.bfloat16
MESH = pl.DeviceIdType.MESH

VMEM_BYTES_V7X = 64 * 1024 * 1024
LANES = 128
SUBLANES_BF16 = 16

GRID_W = 64
NORM_EPS = 1e-6
ROPE_THETA = 500000.0
AXIAL_THETA = 10000.0
MLA_HEADS = 8
MLA_Q_LORA = 512
MLA_KV_LORA = 256
MLA_NOPE = 128
MLA_ROPE = 64
MLA_QK = MLA_NOPE + MLA_ROPE
MLA_V = 128
GQA_HEADS = 8
GQA_KV = 2
GQA_DIM = 128
SWA_HEADS = 32
SWA_KV = 4
SWA_DIM = 64
SWA_WINDOW = 128
SWA_ROT = SWA_DIM // 4
SWA_BLOCK = 128
SWA_HEAD_PARTS = 2
SWA_GROUPS = 4
ADAM_LR = 0.001
ADAM_B1 = 0.9
ADAM_B2 = 0.999
ADAM_EPS = 1e-08
ADAM_WD = 0.01
ADAM_STEP = 10
N_CHIPS = 4
COMM_LANES = 1024


def _tile(dim, cap, mult=LANES):
    if dim <= cap:
        return dim
    t = (cap // mult) * mult
    while t >= mult:
        if dim % t == 0:
            return t
        t -= mult
    return dim


def _params(dims, vmem_estimate):
    limit = int(min(max(vmem_estimate * 1.25 + (4 << 20), 32 << 20), VMEM_BYTES_V7X - (6 << 20)))
    return pltpu.CompilerParams(dimension_semantics=dims, vmem_limit_bytes=limit)


def _nbytes(shape, dtype):
    return int(np.prod(shape)) * jnp.dtype(dtype).itemsize


def _mm(a, b, *, mode, name, out_dtype=F32, epi=None, extra=None, split=1, caps=(1024, 1024, 2048)):
    if mode == "nn":
        (M, K), (K2, N) = a.shape, b.shape
    elif mode == "nt":
        (M, K), (N, K2) = a.shape, b.shape
    else:
        (K, M), (K2, N) = a.shape, b.shape
    assert K == K2, (a.shape, b.shape, mode)
    assert N % split == 0
    ns = N // split
    tn, tk = _tile(ns, caps[1]), _tile(K, caps[2])
    tm = _tile(M, min(caps[0], max(LANES, caps[0] * caps[1] // tn)))
    nj_per = ns // tn
    grid = (M // tm, N // tn, K // tk)
    nk = grid[2]
    if mode == "nn":
        a_spec = pl.BlockSpec((tm, tk), lambda i, j, k: (i, k))
        b_spec = pl.BlockSpec((tk, tn), lambda i, j, k: (k, j))
        dn = (((1,), (0,)), ((), ()))
    elif mode == "nt":
        a_spec = pl.BlockSpec((tm, tk), lambda i, j, k: (i, k))
        b_spec = pl.BlockSpec((tn, tk), lambda i, j, k: (j, k))
        dn = (((1,), (1,)), ((), ()))
    else:
        a_spec = pl.BlockSpec((tk, tm), lambda i, j, k: (k, i))
        b_spec = pl.BlockSpec((tk, tn), lambda i, j, k: (k, j))
        dn = (((0,), (0,)), ((), ()))
    if split == 1:
        o_spec = pl.BlockSpec((tm, tn), lambda i, j, k: (i, j))
        o_shape = (M, N)
    else:
        o_spec = pl.BlockSpec((None, tm, tn), lambda i, j, k: (j // nj_per, i, j % nj_per))
        o_shape = (split, M, ns)
    mn_spec = pl.BlockSpec((tm, tn), lambda i, j, k: (i, j))
    in_specs, args = [a_spec, b_spec], [a, b]
    if epi in ("add", "dsqrelu"):
        in_specs.append(mn_spec)
        args.append(extra)
    if epi == "sqrelu":
        out_shape = (jax.ShapeDtypeStruct(o_shape, BF16), jax.ShapeDtypeStruct(o_shape, BF16))
        out_specs = (o_spec, o_spec)
        n_out = 2
    else:
        out_shape = jax.ShapeDtypeStruct(o_shape, out_dtype)
        out_specs = o_spec
        n_out = 1

    def body(*refs):
        a_ref, b_ref = refs[0], refs[1]
        e_ref = refs[2] if len(args) == 3 else None
        outs = refs[len(args):len(args) + n_out]

        def finish(acc):
            if epi is None:
                outs[0][...] = acc.astype(outs[0].dtype)
            elif epi == "add":
                outs[0][...] = (e_ref[...] + acc).astype(outs[0].dtype)
            elif epi == "sqrelu":
                r = jnp.maximum(acc, 0.0)
                outs[0][...] = acc.astype(BF16)
                outs[1][...] = (r * r).astype(BF16)
            else:
                u = e_ref[...].astype(F32)
                outs[0][...] = (acc * (2.0 * jnp.maximum(u, 0.0))).astype(outs[0].dtype)

        prod = lax.dot_general(a_ref[...].astype(BF16), b_ref[...].astype(BF16), dn, preferred_element_type=F32)
        if nk == 1:
            finish(prod)
            return
        acc_ref = refs[-1]
        k = pl.program_id(2)

        @pl.when(k == 0)
        def _():
            acc_ref[...] = prod

        @pl.when((k != 0) & (k != nk - 1))
        def _():
            acc_ref[...] += prod

        @pl.when(k == nk - 1)
        def _():
            finish(acc_ref[...] + prod)

    est = 2 * (_nbytes((tm, tk), a.dtype) + _nbytes((tk, tn), b.dtype)) + _nbytes((tm, tn), F32)
    est += 2 * n_out * _nbytes((tm, tn), out_dtype if n_out == 1 else BF16)
    if len(args) == 3:
        est += 2 * _nbytes((tm, tn), extra.dtype)
    est += 3 * _nbytes((tm, tn), F32)
    return pl.pallas_call(
        body, name=name, grid=grid, in_specs=in_specs, out_specs=out_specs, out_shape=out_shape,
        scratch_shapes=[] if nk == 1 else [pltpu.VMEM((tm, tn), F32)],
        compiler_params=_params(("parallel", "parallel", "arbitrary"), est),
    )(*args)


def _perm(y, p):
    hi = y.astype(BF16)
    lo = (y - hi.astype(F32)).astype(BF16)
    d = lambda t: jnp.dot(t, p, preferred_element_type=F32)
    return d(hi) + d(lo)


def _rows_tile(T, d):
    return _tile(T, 2048 if d <= 256 else 512, 128)


def _norm_fwd(x, gain, *, name, rope=None, out_dtype=BF16, out_scale=None):
    H, T, d = x.shape
    tm = _rows_tile(T, d)
    g2 = gain.reshape(1, d).astype(F32)
    in_specs = [pl.BlockSpec((None, tm, d), lambda h, i: (h, i, 0)), pl.BlockSpec((1, d), lambda h, i: (0, 0))]
    args = [x, g2]
    if rope is not None:
        in_specs += [pl.BlockSpec((tm, d), lambda h, i: (i, 0)), pl.BlockSpec((tm, d), lambda h, i: (i, 0)),
                     pl.BlockSpec((d, d), lambda h, i: (0, 0))]
        args += list(rope)

    def body(*refs):
        x_ref, g_ref = refs[0], refs[1]
        o_ref = refs[-1]
        xv = x_ref[...]
        y = xv * lax.rsqrt(jnp.mean(xv * xv, axis=-1, keepdims=True) + NORM_EPS)
        y = y * g_ref[...]
        if rope is not None:
            c_ref, s_ref, p_ref = refs[2], refs[3], refs[4]
            y = y * c_ref[...] + _perm(y, p_ref[...]) * s_ref[...]
        if out_scale is not None:
            y = y * out_scale
        o_ref[...] = y.astype(o_ref.dtype)

    est = 2 * (_nbytes((tm, max(d, LANES)), F32) * (3 if rope is not None else 1) + _nbytes((tm, max(d, LANES)), out_dtype))
    est += 6 * _nbytes((tm, max(d, LANES)), F32)
    return pl.pallas_call(
        body, name=name, grid=(H, T // tm), in_specs=in_specs,
        out_specs=pl.BlockSpec((None, tm, d), lambda h, i: (h, i, 0)),
        out_shape=jax.ShapeDtypeStruct((H, T, d), out_dtype),
        compiler_params=_params(("parallel", "parallel"), est),
    )(*args)


def _norm_bwd(x, gain, dy, *, name, rope=None, group=1, res=None, out_dtype=F32, dy_scale=None):
    H, T, d = x.shape
    assert dy.shape == (H * group, T, d), (dy.shape, x.shape, group)
    tm = _rows_tile(T, d)
    g2 = gain.reshape(1, d).astype(F32)
    in_specs = [pl.BlockSpec((None, tm, d), lambda h, i: (h, i, 0)), pl.BlockSpec((1, d), lambda h, i: (0, 0)),
                pl.BlockSpec((group, tm, d), lambda h, i: (h, i, 0))]
    args = [x, g2, dy]
    if rope is not None:
        in_specs += [pl.BlockSpec((tm, d), lambda h, i: (i, 0)), pl.BlockSpec((tm, d), lambda h, i: (i, 0)),
                     pl.BlockSpec((d, d), lambda h, i: (0, 0))]
        args += list(rope)
    if res is not None:
        assert H == 1
        in_specs.append(pl.BlockSpec((tm, d), lambda h, i: (i, 0)))
        args.append(res)
    n_in = len(args)

    def body(*refs):
        x_ref, g_ref, dy_ref = refs[0], refs[1], refs[2]
        dx_ref, dg_ref = refs[n_in], refs[n_in + 1]
        first = (pl.program_id(0) == 0) & (pl.program_id(1) == 0)

        @pl.when(first)
        def _():
            dg_ref[...] = jnp.zeros_like(dg_ref)

        dyv = dy_ref[0].astype(F32)
        for g in range(1, group):
            dyv = dyv + dy_ref[g].astype(F32)
        if dy_scale is not None:
            dyv = dyv * dy_scale
        pos = 3
        if rope is not None:
            c_ref, s_ref, p_ref = refs[3], refs[4], refs[5]
            pos = 6
            dyv = dyv * c_ref[...] + _perm(dyv * s_ref[...], p_ref[...])
        xv = x_ref[...]
        r = lax.rsqrt(jnp.mean(xv * xv, axis=-1, keepdims=True) + NORM_EPS)
        xhat = xv * r
        dg_ref[...] += jnp.sum(dyv * xhat, axis=0, keepdims=True)
        dxh = dyv * g_ref[...]
        dx = r * (dxh - xhat * jnp.mean(dxh * xhat, axis=-1, keepdims=True))
        if res is not None:
            dx = dx + refs[pos][...]
        dx_ref[...] = dx.astype(dx_ref.dtype)

    wide = max(d, LANES)
    est = 2 * _nbytes((tm, wide), F32) * (2 + group + (2 if rope is not None else 0) + (1 if res is not None else 0))
    est += 8 * _nbytes((tm, wide), F32)
    return pl.pallas_call(
        body, name=name, grid=(H, T // tm), in_specs=in_specs,
        out_specs=(pl.BlockSpec((None, tm, d), lambda h, i: (h, i, 0)), pl.BlockSpec((1, d), lambda h, i: (0, 0))),
        out_shape=(jax.ShapeDtypeStruct((H, T, d), out_dtype), jax.ShapeDtypeStruct((1, d), F32)),
        compiler_params=_params(("arbitrary", "arbitrary"), est),
    )(*args)


def _group_sum(x, group, *, name):
    HG, T, d = x.shape
    H = HG // group
    tm = _rows_tile(T, d)

    def body(x_ref, o_ref):
        acc = x_ref[0]
        for g in range(1, group):
            acc = acc + x_ref[g]
        o_ref[...] = acc

    est = 2 * (group + 1) * _nbytes((tm, max(d, LANES)), F32)
    return pl.pallas_call(
        body, name=name, grid=(H, T // tm),
        in_specs=[pl.BlockSpec((group, tm, d), lambda h, i: (h, i, 0))],
        out_specs=pl.BlockSpec((None, tm, d), lambda h, i: (h, i, 0)),
        out_shape=jax.ShapeDtypeStruct((H, T, d), F32),
        compiler_params=_params(("parallel", "parallel"), est),
    )(x)


def _delta(o, do, *, name):
    H, T, d = o.shape
    tm = _rows_tile(T, d)

    def body(o_ref, do_ref, dl_ref, dob_ref):
        dov = do_ref[...]
        dl = jnp.sum(o_ref[...] * dov, axis=-1, keepdims=True)
        dl_ref[...] = jnp.broadcast_to(dl, (tm, LANES))
        dob_ref[...] = dov.astype(BF16)

    spec = pl.BlockSpec((None, tm, d), lambda h, i: (h, i, 0))
    est = 2 * (3 * _nbytes((tm, max(d, LANES)), F32) + _nbytes((tm, LANES), F32))
    return pl.pallas_call(
        body, name=name, grid=(H, T // tm), in_specs=[spec, spec],
        out_specs=(pl.BlockSpec((None, tm, LANES), lambda h, i: (h, i, 0)), spec),
        out_shape=(jax.ShapeDtypeStruct((H, T, LANES), F32), jax.ShapeDtypeStruct((H, T, d), BF16)),
        compiler_params=_params(("parallel", "parallel"), est),
    )(o, do)


NT_DIMS = (((1,), (1,)), ((), ()))
TN_DIMS = (((0,), (0,)), ((), ()))
LOG2E = math.log2(math.e)
FLASH_CHUNK = 256
FLASH_ROW_PARTS = 4


def _flash_fwd(q, k, v, *, name, gather=()):
    H, T, dk = q.shape
    Hkv, _, dv = v.shape
    G = H // Hkv
    tq, tk = _tile(T, 1024), _tile(T, 2048)
    tp = _tile(tq, tq // FLASH_ROW_PARTS, SUBLANES_BF16)
    nk = T // tk

    n_r = len(gather)
    grid = (H, T // tq, nk)
    assert n_r == 0 or H >= 2

    def body(*refs):
        q_ref, k_ref, v_ref = refs[:3]
        o_ref, lse_ref = refs[3 + n_r:5 + n_r]
        m_ref, l_ref, acc_ref = refs[5 + 2 * n_r:8 + 2 * n_r]
        hi, qi, ki = pl.program_id(0), pl.program_id(1), pl.program_id(2)
        if n_r:
            ag_start, ag_forward, ag_finish = _gather_plan(refs[3:3 + n_r], refs[5 + n_r:5 + 2 * n_r],
                                                           *refs[8 + 2 * n_r:])
            pl.when((hi == 0) & (qi == 0) & (ki == 0))(ag_start)
            pl.when((hi == grid[0] - 1) & (qi == 0) & (ki == 0))(ag_forward)

        @pl.when(ki == 0)
        def _():
            m_ref[...] = jnp.full_like(m_ref, -jnp.inf)
            l_ref[...] = jnp.zeros_like(l_ref)
            acc_ref[...] = jnp.zeros_like(acc_ref)

        kv, vv = k_ref[...], v_ref[...]
        parts = [slice(part * tp, (part + 1) * tp) for part in range(tq // tp)]
        m_prev = [m_ref[rows, :] for rows in parts]
        l_prev = [l_ref[rows, :] for rows in parts]
        a_prev = [acc_ref[rows, :] for rows in parts]
        ss = [lax.dot_general(q_ref[rows, :], kv, NT_DIMS, preferred_element_type=F32) for rows in parts]
        m_new = [jnp.maximum(m, jnp.max(s, axis=-1, keepdims=True)) for m, s in zip(m_prev, ss)]
        alpha = [jnp.exp2(m - mn) for m, mn in zip(m_prev, m_new)]
        ps = [jnp.exp2(s - mn) for s, mn in zip(ss, m_new)]
        l_new = [a * l + jnp.sum(p, axis=-1, keepdims=True) for a, l, p in zip(alpha, l_prev, ps)]
        pv = [jnp.dot(p.astype(BF16), vv, preferred_element_type=F32) for p in ps]
        for rows, mn, ln, a, acc, o in zip(parts, m_new, l_new, alpha, a_prev, pv):
            m_ref[rows, :] = mn
            l_ref[rows, :] = ln
            acc_ref[rows, :] = a * acc + o

        @pl.when(ki == nk - 1)
        def _():
            l = l_ref[...]
            o_ref[...] = acc_ref[...] / l
            lse_ref[...] = jnp.broadcast_to(m_ref[...] + jnp.log(l) * LOG2E, (tq, LANES))

        if n_r:
            pl.when((hi == grid[0] - 1) & (qi == grid[1] - 1) & (ki == grid[2] - 1))(ag_finish)

    est = 2 * (_nbytes((tq, dk), BF16) + _nbytes((tk, dk + dv), BF16) + _nbytes((tq, dv + LANES), F32))
    est += 4 * _nbytes((tq, tk), F32) + 3 * _nbytes((tq, dv + 3 * LANES), F32)
    outs = pl.pallas_call(
        body, name=name, grid=grid,
        in_specs=[pl.BlockSpec((None, tq, dk), lambda h, i, j: (h, i, 0)),
                  pl.BlockSpec((None, tk, dk), lambda h, i, j: (h // G, j, 0)),
                  pl.BlockSpec((None, tk, dv), lambda h, i, j: (h // G, j, 0))] + [HBM_SPEC] * n_r,
        out_specs=[pl.BlockSpec((None, tq, dv), lambda h, i, j: (h, i, 0)),
                   pl.BlockSpec((None, tq, LANES), lambda h, i, j: (h, i, 0))] + [HBM_SPEC] * n_r,
        out_shape=[jax.ShapeDtypeStruct((H, T, dv), F32), jax.ShapeDtypeStruct((H, T, LANES), F32)]
                  + _gathered_shapes(gather),
        scratch_shapes=[pltpu.VMEM((tq, 1), F32), pltpu.VMEM((tq, 1), F32), pltpu.VMEM((tq, dv), F32)]
                       + (_gather_scratch(n_r) if n_r else []),
        compiler_params=_params(("arbitrary",) * 3 if n_r else ("parallel", "parallel", "arbitrary"), est),
    )(q, k, v, *gather)
    return (outs[0], outs[1], outs[2:]) if n_r else (outs[0], outs[1])


def _flash_bwd(q, k, v, do, lse2, delta, *, name, rider=None):
    H, T, dk = q.shape
    Hkv, _, dv = v.shape
    G = H // Hkv
    tq, tk = _tile(T, 1024), _tile(T, 2048)
    tc = _tile(tk, FLASH_CHUNK)

    kind, carried = rider if rider is not None else (None, ())
    n_r = len(carried)
    grid = (H, T // tk, T // tq)
    if kind == "scatter":
        plan, rider_scratch = _scatter_plan, _scatter_scratch(n_r)
        rider_shapes = [jax.ShapeDtypeStruct(p.shape, p.dtype) for p in carried]
    elif kind == "exchange":
        plan, rider_scratch = _exchange_plan, _exchange_scratch(n_r)
        rider_shapes = [jax.ShapeDtypeStruct((N_CHIPS,) + g.shape[2:], g.dtype) for g in carried]
    else:
        assert kind is None
        rider_scratch, rider_shapes = [], []

    def body(*refs):
        q_ref, k_ref, v_ref, do_ref, lse_ref, dl_ref = refs[:6]
        dq_ref, dk_ref, dv_ref = refs[6 + n_r:9 + n_r]
        hi, ki, qi = pl.program_id(0), pl.program_id(1), pl.program_id(2)
        if n_r:
            rider_start, rider_finish = plan(refs[6:6 + n_r], refs[9 + n_r:9 + 2 * n_r], *refs[9 + 2 * n_r:])
            pl.when((hi == 0) & (ki == 0) & (qi == 0))(rider_start)
        rows = pl.ds(pl.multiple_of(qi * tq, tq), tq)

        @pl.when(qi == 0)
        def _():
            dk_ref[...] = jnp.zeros_like(dk_ref)
            dv_ref[...] = jnp.zeros_like(dv_ref)

        @pl.when(ki == 0)
        def _():
            dq_ref[rows, :] = jnp.zeros((tq, dk), F32)

        qv, dov = q_ref[...], do_ref[...]
        lse2 = lse_ref[:, :1]
        dl = dl_ref[:, :1]
        chunks = [slice(c * tc, (c + 1) * tc) for c in range(tk // tc)]
        kcs = [k_ref[ks, :] for ks in chunks]
        vcs = [v_ref[ks, :] for ks in chunks]
        dv_old = [dv_ref[ks, :] for ks in chunks]
        dk_old = [dk_ref[ks, :] for ks in chunks]
        dq_old = dq_ref[rows, :]
        ss = [lax.dot_general(qv, kc, NT_DIMS, preferred_element_type=F32) for kc in kcs]
        dps = [lax.dot_general(dov, vc, NT_DIMS, preferred_element_type=F32) for vc in vcs]
        ps = [jnp.exp2(s - lse2) for s in ss]
        dss = [(p * (dp - dl)).astype(BF16) for p, dp in zip(ps, dps)]
        pbs = [p.astype(BF16) for p in ps]
        dvs = [lax.dot_general(pb, dov, TN_DIMS, preferred_element_type=F32) for pb in pbs]
        dks = [lax.dot_general(ds, qv, TN_DIMS, preferred_element_type=F32) for ds in dss]
        dqs = [jnp.dot(ds, kc, preferred_element_type=F32) for ds, kc in zip(dss, kcs)]
        for ks, old, new in zip(chunks, dv_old, dvs):
            dv_ref[ks, :] = old + new
        for ks, old, new in zip(chunks, dk_old, dks):
            dk_ref[ks, :] = old + new
        dq_c = dqs[0]
        for extra in dqs[1:]:
            dq_c = dq_c + extra
        dq_ref[rows, :] = dq_old + dq_c

        if n_r:
            pl.when((hi == grid[0] - 1) & (ki == grid[1] - 1) & (qi == grid[2] - 1))(rider_finish)

    est = 2 * (_nbytes((tq, dk + dv), BF16) + _nbytes((tk, dk + dv), BF16) + 2 * _nbytes((tq, LANES), F32))
    est += 2 * (_nbytes((T, dk), F32) + _nbytes((tk, dk + dv), F32)) + 10 * _nbytes((tq, tc), F32)
    outs = pl.pallas_call(
        body, name=name, grid=grid,
        in_specs=[pl.BlockSpec((None, tq, dk), lambda h, j, i: (h, i, 0)),
                  pl.BlockSpec((None, tk, dk), lambda h, j, i: (h // G, j, 0)),
                  pl.BlockSpec((None, tk, dv), lambda h, j, i: (h // G, j, 0)),
                  pl.BlockSpec((None, tq, dv), lambda h, j, i: (h, i, 0)),
                  pl.BlockSpec((None, tq, LANES), lambda h, j, i: (h, i, 0)),
                  pl.BlockSpec((None, tq, LANES), lambda h, j, i: (h, i, 0))] + [HBM_SPEC] * n_r,
        out_specs=[pl.BlockSpec((None, T, dk), lambda h, j, i: (h, 0, 0)),
                   pl.BlockSpec((None, tk, dk), lambda h, j, i: (h, j, 0)),
                   pl.BlockSpec((None, tk, dv), lambda h, j, i: (h, j, 0))] + [HBM_SPEC] * n_r,
        out_shape=[jax.ShapeDtypeStruct((H, T, dk), F32), jax.ShapeDtypeStruct((H, T, dk), F32),
                   jax.ShapeDtypeStruct((H, T, dv), F32)] + rider_shapes,
        scratch_shapes=rider_scratch,
        compiler_params=_params(("arbitrary", "arbitrary", "arbitrary"), est),
    )(q, k, v, do, lse2, delta, *carried)
    return (outs[0], outs[1], outs[2], outs[3:]) if n_r else tuple(outs)


def _swa_specs(G, d, n_blocks, lanes, gpb):
    B = SWA_BLOCK
    prev = lambda j, i: (j, jnp.maximum(i - 1, 0), 0)
    cur = lambda j, i: (j, i, 0)
    nxt = lambda j, i: (j, jnp.minimum(i + 1, n_blocks - 1), 0)
    q_specs = [pl.BlockSpec((gpb * G, B, lanes), m) for m in (prev, cur, nxt)]
    kv_specs = [pl.BlockSpec((gpb, B, d), m) for m in (prev, cur, nxt)]
    return q_specs, kv_specs, cur


def _swa_parts(G, gpb):
    gp = G // SWA_HEAD_PARTS
    return gp, [(g, slice(g * G + part * gp, g * G + (part + 1) * gp)) for g in range(gpb) for part in range(SWA_HEAD_PARTS)]


def _swa_bias(i, T):
    B = SWA_BLOCK
    row = lax.broadcasted_iota(jnp.int32, (B, 3 * B), 0)
    col = lax.broadcasted_iota(jnp.int32, (B, 3 * B), 1)
    kpos = (i - 1) * B + col
    valid = (col >= row) & (col <= row + 2 * SWA_WINDOW) & (kpos >= 0) & (kpos < T)
    return jnp.where(valid, 0.0, -jnp.inf)


def _swa_fwd(q, k, v, sink, scale, *, name):
    Hq, T, d = q.shape
    Hkv = k.shape[0]
    G = Hq // Hkv
    B = SWA_BLOCK
    nb = T // B
    gpb = _tile(Hkv, SWA_GROUPS, 1)
    _, kv_specs, cur = _swa_specs(G, d, nb, d, gpb)

    def body(q_ref, k0, k1, k2, v0, v1, v2, sink_ref, o_ref, lse_ref):
        i = pl.program_id(1)
        kvs = [jnp.concatenate([k0[g], k1[g], k2[g]], axis=0) for g in range(gpb)]
        vvs = [jnp.concatenate([v0[g], v1[g], v2[g]], axis=0) for g in range(gpb)]
        bias = _swa_bias(i, T)[None]
        gp, parts = _swa_parts(G, gpb)
        sks = [sink_ref[hs] for _, hs in parts]
        ss = [lax.dot_general(q_ref[hs].reshape(gp * B, d), kvs[g], NT_DIMS, preferred_element_type=F32) for g, hs in parts]
        ss = [(s * scale).reshape(gp, B, 3 * B) + bias for s in ss]
        ms = [jnp.maximum(jnp.max(s, axis=-1, keepdims=True), sk) for s, sk in zip(ss, sks)]
        ps = [jnp.exp(s - m) for s, m in zip(ss, ms)]
        dens = [jnp.sum(p, axis=-1, keepdims=True) + jnp.exp(sk - m) for p, sk, m in zip(ps, sks, ms)]
        pns = [(p / den).reshape(gp * B, 3 * B).astype(BF16) for p, den in zip(ps, dens)]
        os_ = [jnp.dot(pn, vvs[g], preferred_element_type=F32).reshape(gp, B, d) for pn, (g, _) in zip(pns, parts)]
        for (_, hs), o, m, den in zip(parts, os_, ms, dens):
            o_ref[hs] = o
            lse_ref[hs] = jnp.broadcast_to(m + jnp.log(den), (gp, B, LANES))

    GG = gpb * G
    est = 2 * (_nbytes((GG, B, LANES), BF16) + 6 * gpb * _nbytes((B, LANES), BF16) + 2 * _nbytes((GG, B, LANES), F32))
    est += 8 * _nbytes((GG * B, 3 * B), F32)
    return pl.pallas_call(
        body, name=name, grid=(Hkv // gpb, nb),
        in_specs=[pl.BlockSpec((GG, B, d), cur)] + kv_specs + kv_specs + [pl.BlockSpec((GG, 1, 1), lambda j, i: (j, 0, 0))],
        out_specs=(pl.BlockSpec((GG, B, d), cur), pl.BlockSpec((GG, B, LANES), cur)),
        out_shape=(jax.ShapeDtypeStruct((Hq, T, d), F32), jax.ShapeDtypeStruct((Hq, T, LANES), F32)),
        compiler_params=_params(("parallel", "parallel"), est),
    )(q, k, k, k, v, v, v, sink)


def _swa_dq(q, k, v, do, lse, delta, sink, scale, *, name):
    Hq, T, d = q.shape
    Hkv = k.shape[0]
    G = Hq // Hkv
    B = SWA_BLOCK
    nb = T // B
    gpb = _tile(Hkv, SWA_GROUPS, 1)
    _, kv_specs, cur = _swa_specs(G, d, nb, d, gpb)

    def body(q_ref, do_ref, lse_ref, dl_ref, k0, k1, k2, v0, v1, v2, sink_ref, dq_ref, dsink_ref):
        i = pl.program_id(1)
        kvs = [jnp.concatenate([k0[g], k1[g], k2[g]], axis=0) for g in range(gpb)]
        vvs = [jnp.concatenate([v0[g], v1[g], v2[g]], axis=0) for g in range(gpb)]
        bias = _swa_bias(i, T)[None]
        gp, parts = _swa_parts(G, gpb)
        lses = [lse_ref[hs, :, :1] for _, hs in parts]
        dls = [dl_ref[hs, :, :1] for _, hs in parts]
        ss = [lax.dot_general(q_ref[hs].reshape(gp * B, d), kvs[g], NT_DIMS, preferred_element_type=F32) for g, hs in parts]
        dps = [lax.dot_general(do_ref[hs].reshape(gp * B, d), vvs[g], NT_DIMS, preferred_element_type=F32)
               for g, hs in parts]
        ps = [jnp.exp((s * scale).reshape(gp, B, 3 * B) + bias - lse) for s, lse in zip(ss, lses)]
        dss = [(p * (dp.reshape(gp, B, 3 * B) - dl) * scale).reshape(gp * B, 3 * B).astype(BF16)
               for p, dp, dl in zip(ps, dps, dls)]
        dqs = [jnp.dot(ds, kvs[g], preferred_element_type=F32).reshape(gp, B, d) for ds, (g, _) in zip(dss, parts)]
        dsks = [-jnp.sum(jnp.exp(sink_ref[hs] - lse) * dl, axis=1, keepdims=True)
                for (_, hs), lse, dl in zip(parts, lses, dls)]

        @pl.when(i == 0)
        def _():
            dsink_ref[...] = jnp.zeros_like(dsink_ref)

        for (_, hs), dq, dsk in zip(parts, dqs, dsks):
            dq_ref[hs] = dq
            dsink_ref[hs] += jnp.broadcast_to(dsk, (gp, 1, LANES))

    GG = gpb * G
    est = 2 * (2 * _nbytes((GG, B, LANES), BF16) + 6 * gpb * _nbytes((B, LANES), BF16) + 3 * _nbytes((GG, B, LANES), F32))
    est += 8 * _nbytes((GG * B, 3 * B), F32)
    return pl.pallas_call(
        body, name=name, grid=(Hkv // gpb, nb),
        in_specs=[pl.BlockSpec((GG, B, d), cur), pl.BlockSpec((GG, B, d), cur), pl.BlockSpec((GG, B, LANES), cur),
                  pl.BlockSpec((GG, B, LANES), cur)] + kv_specs + kv_specs
                 + [pl.BlockSpec((GG, 1, 1), lambda j, i: (j, 0, 0))],
        out_specs=(pl.BlockSpec((GG, B, d), cur), pl.BlockSpec((GG, 1, LANES), lambda j, i: (j, 0, 0))),
        out_shape=(jax.ShapeDtypeStruct((Hq, T, d), F32), jax.ShapeDtypeStruct((Hq, 1, LANES), F32)),
        compiler_params=_params(("arbitrary", "arbitrary"), est),
    )(q, do, lse, delta, k, k, k, v, v, v, sink)


def _swa_dkv(q, k, v, do, lse, delta, scale, *, name):
    Hq, T, d = q.shape
    Hkv = k.shape[0]
    G = Hq // Hkv
    B = SWA_BLOCK
    nb = T // B
    gpb = _tile(Hkv, SWA_GROUPS, 1)
    q_specs, _, cur = _swa_specs(G, d, nb, d, gpb)
    l_specs, _, _ = _swa_specs(G, d, nb, LANES, gpb)

    def body(k_ref, v_ref, q0, q1, q2, d0, d1, d2, l0, l1, l2, e0, e1, e2, dk_ref, dv_ref):
        b = pl.program_id(1)
        row = lax.broadcasted_iota(jnp.int32, (B, B), 0)
        col = lax.broadcasted_iota(jnp.int32, (B, B), 1)
        biases = []
        for part in range(3):
            qpos = (b + part - 1) * B + row
            diff = (part - 1) * B + row - col
            valid = (diff >= -SWA_WINDOW) & (diff <= SWA_WINDOW) & (qpos >= 0) & (qpos < T)
            biases.append(jnp.where(valid, 0.0, -jnp.inf)[None])
        chains = [(g, part) for g in range(gpb) for part in range(3)]
        heads = [slice(g * G, (g + 1) * G) for g, _ in chains]
        kvs = [k_ref[g] for g, _ in chains]
        vvs = [v_ref[g] for g, _ in chains]
        qvs = [(q0, q1, q2)[part][hs].reshape(G * B, d) for (_, part), hs in zip(chains, heads)]
        dovs = [(d0, d1, d2)[part][hs].reshape(G * B, d) for (_, part), hs in zip(chains, heads)]
        lses = [(l0, l1, l2)[part][hs, :, :1] for (_, part), hs in zip(chains, heads)]
        dls = [(e0, e1, e2)[part][hs, :, :1] for (_, part), hs in zip(chains, heads)]
        ss = [lax.dot_general(qv, kv, NT_DIMS, preferred_element_type=F32) for qv, kv in zip(qvs, kvs)]
        dps = [lax.dot_general(dov, vv, NT_DIMS, preferred_element_type=F32) for dov, vv in zip(dovs, vvs)]
        ps = [jnp.exp((s * scale).reshape(G, B, B) + biases[part] - lse) for s, (_, part), lse in zip(ss, chains, lses)]
        dss = [(p * (dp.reshape(G, B, B) - dl) * scale).reshape(G * B, B).astype(BF16) for p, dp, dl in zip(ps, dps, dls)]
        pbs = [p.reshape(G * B, B).astype(BF16) for p in ps]
        dvs = [lax.dot_general(pb, dov, TN_DIMS, preferred_element_type=F32) for pb, dov in zip(pbs, dovs)]
        dks = [lax.dot_general(ds, qv, TN_DIMS, preferred_element_type=F32) for ds, qv in zip(dss, qvs)]
        for g in range(gpb):
            dk_ref[g] = dks[3 * g] + dks[3 * g + 1] + dks[3 * g + 2]
            dv_ref[g] = dvs[3 * g] + dvs[3 * g + 1] + dvs[3 * g + 2]

    GG = gpb * G
    est = 2 * (6 * _nbytes((GG, B, LANES), BF16) + 6 * _nbytes((GG, B, LANES), F32) + 4 * gpb * _nbytes((B, LANES), F32))
    est += 10 * _nbytes((GG * B, B), F32)
    kspec = pl.BlockSpec((gpb, B, d), cur)
    return pl.pallas_call(
        body, name=name, grid=(Hkv // gpb, nb),
        in_specs=[kspec, kspec] + q_specs + q_specs + l_specs + l_specs,
        out_specs=(kspec, kspec),
        out_shape=(jax.ShapeDtypeStruct((Hkv, T, d), F32), jax.ShapeDtypeStruct((Hkv, T, d), F32)),
        compiler_params=_params(("parallel", "parallel"), est),
    )(k, v, q, q, q, do, do, do, lse, lse, lse, delta, delta, delta)


def _loss_head(y, target, *, name):
    T, D = y.shape
    tm = _tile(T, 512)

    def body(y_ref, t_ref, dy_ref, s_ref):
        @pl.when(pl.program_id(0) == 0)
        def _():
            s_ref[...] = jnp.zeros_like(s_ref)

        e = y_ref[...] - t_ref[...]
        dy_ref[...] = e / D
        s_ref[...] += jnp.sum(jnp.sum(e * e, axis=-1, keepdims=True), axis=0, keepdims=True)

    spec = pl.BlockSpec((tm, D), lambda i: (i, 0))
    return pl.pallas_call(
        body, name=name, grid=(T // tm,), in_specs=[spec, spec],
        out_specs=(spec, pl.BlockSpec((1, 1), lambda i: (0, 0))),
        out_shape=(jax.ShapeDtypeStruct((T, D), F32), jax.ShapeDtypeStruct((1, 1), F32)),
        compiler_params=_params(("arbitrary",), 8 * _nbytes((tm, D), F32)),
    )(y, target)


def _adamw(w, g, m, v, *, name):
    R, C = w.shape
    tr = _tile(R, max(8, (1 << 19) // max(C, LANES) // 8 * 8), 8)

    def body(w_ref, g_ref, m_ref, v_ref, d_ref, nm_ref, nv_ref):
        gv = g_ref[...]
        nm = ADAM_B1 * m_ref[...] + (1.0 - ADAM_B1) * gv
        nv = ADAM_B2 * v_ref[...] + (1.0 - ADAM_B2) * jnp.square(gv)
        m_hat = nm / (1.0 - ADAM_B1 ** ADAM_STEP)
        v_hat = nv / (1.0 - ADAM_B2 ** ADAM_STEP)
        d_ref[...] = -ADAM_LR * (m_hat / (jnp.sqrt(v_hat) + ADAM_EPS) + ADAM_WD * w_ref[...])
        nm_ref[...] = nm
        nv_ref[...] = nv

    spec = pl.BlockSpec((tr, C), lambda i: (i, 0))
    sds = jax.ShapeDtypeStruct((R, C), F32)
    return pl.pallas_call(
        body, name=name, grid=(R // tr,), in_specs=[spec] * 4, out_specs=(spec,) * 3, out_shape=(sds,) * 3,
        compiler_params=_params(("parallel",), 16 * _nbytes((tr, max(C, LANES)), F32)),
    )(w, g, m, v)


def _to_bf16(w, *, name):
    R, C = w.shape
    tr = _comm_rows_tile(R, C)

    def body(w_ref, o_ref):
        o_ref[...] = w_ref[...].astype(BF16)

    spec = pl.BlockSpec((tr, C), lambda i: (i, 0))
    return pl.pallas_call(
        body, name=name, grid=(R // tr,), in_specs=[spec], out_specs=spec, out_shape=jax.ShapeDtypeStruct((R, C), BF16),
        compiler_params=_params(("parallel",), 6 * _nbytes((tr, max(C, LANES)), F32)),
    )(w)


def _comm_rows_tile(R, L):
    return _tile(R, max(SUBLANES_BF16, (1 << 19) // L // SUBLANES_BF16 * SUBLANES_BF16), SUBLANES_BF16)


def _pair_add(g, recv, c_idx, *, name):
    _, _, R, L = g.shape
    tr = _comm_rows_tile(R, L)

    def body(c_ref, g_ref, r_ref, o_ref):
        o_ref[...] = (g_ref[...] + r_ref[...]).astype(BF16)

    grid_spec = pltpu.PrefetchScalarGridSpec(
        num_scalar_prefetch=1, grid=(N_CHIPS, R // tr),
        in_specs=[pl.BlockSpec((None, None, tr, L), lambda j, i, c: (j, c[0], i, 0)),
                  pl.BlockSpec((None, tr, L), lambda j, i, c: (j, i, 0))],
        out_specs=pl.BlockSpec((None, tr, L), lambda j, i, c: (j, i, 0)))
    return pl.pallas_call(
        body, name=name, grid_spec=grid_spec, out_shape=jax.ShapeDtypeStruct((N_CHIPS, R, L), BF16),
        compiler_params=_params(("parallel", "parallel"), 8 * _nbytes((tr, L), F32)),
    )(c_idx, g, recv)


def _sum_chips(q, c_idx, *, name):
    _, R, L = q.shape
    tr = _comm_rows_tile(R, L)

    def body(c_ref, q_ref, o_ref):
        acc = q_ref[0].astype(F32)
        for j in range(1, N_CHIPS):
            acc = acc + q_ref[j].astype(F32)
        o_ref[...] = acc

    grid_spec = pltpu.PrefetchScalarGridSpec(
        num_scalar_prefetch=1, grid=(R // tr,),
        in_specs=[pl.BlockSpec((N_CHIPS, tr, L), lambda i, c: (0, i, 0))],
        out_specs=pl.BlockSpec((None, tr, L), lambda i, c: (c[0], i, 0)))
    return pl.pallas_call(
        body, name=name, grid_spec=grid_spec, out_shape=jax.ShapeDtypeStruct((2, R, L), F32),
        compiler_params=_params(("parallel",), 10 * _nbytes((tr, L), F32)),
    )(c_idx, q)


HBM_SPEC = pl.BlockSpec(memory_space=pltpu.HBM)


def _position():
    return lax.axis_index("x"), lax.axis_index("y"), lax.axis_index("c")


def _other_chips(x, y):
    return [(1 - x, y), (x, 1 - y), (1 - x, 1 - y)]


AG_COPIES = 7


def _gather_plan(w_refs, out_refs, send_sems, recv_sems, local_sems):
    n = len(w_refs)
    x, y, c = _position()
    me, sibling = (x, y, c), (x, y, 1 - c)
    chips = _other_chips(x, y)

    def copy(i, k, block, to, src=None):
        px, py, pc = block
        slot = out_refs[i].at[4 * px + 2 * py + pc]
        return pltpu.make_async_remote_copy(
            src_ref=slot if src is None else src, dst_ref=slot, send_sem=send_sems.at[AG_COPIES * i + k],
            recv_sem=recv_sems.at[AG_COPIES * i + k], device_id=to, device_id_type=MESH)

    def local(i):
        return pltpu.make_async_copy(w_refs[i].at[c], out_refs[i].at[4 * x + 2 * y + c], local_sems.at[i])

    def first(i):
        own = w_refs[i].at[c]
        return [copy(i, 0, me, sibling, src=own)] + [copy(i, 1 + j, me, (*chip, c), src=own)
                                                     for j, chip in enumerate(chips)]

    def passed(i):
        return [copy(i, 4 + j, (*chip, c), sibling) for j, chip in enumerate(chips)]

    def start():
        for i in range(n):
            local(i).start()
            for cp in first(i):
                cp.start()

    def forward():
        for i in range(n):
            for j, chip in enumerate(chips):
                copy(i, 1 + j, (*chip, c), me).wait_recv()
                passed(i)[j].start()

    def finish():
        for i in range(n):
            copy(i, 0, sibling, me).wait_recv()
            for j, chip in enumerate(chips):
                copy(i, 4 + j, (*chip, 1 - c), me).wait_recv()
        for i in range(n):
            for cp in first(i) + passed(i):
                cp.wait_send()
            local(i).wait()

    return start, forward, finish


def _gather_scratch(n):
    return [pltpu.SemaphoreType.DMA((AG_COPIES * n,)), pltpu.SemaphoreType.DMA((AG_COPIES * n,)),
            pltpu.SemaphoreType.DMA((n,))]


def _gathered_shapes(ws):
    return [jax.ShapeDtypeStruct((2 * N_CHIPS,) + w.shape[1:], w.dtype) for w in ws]


def _all_gather_halves(ws, *, name):
    n = len(ws)

    def body(*refs):
        for step in _gather_plan(refs[:n], refs[n:2 * n], *refs[2 * n:]):
            step()

    return pl.pallas_call(
        body, name=name, in_specs=[HBM_SPEC] * n, out_specs=[HBM_SPEC] * n, out_shape=_gathered_shapes(ws),
        scratch_shapes=_gather_scratch(n),
    )(*ws)


def _exchange_plan(g_refs, out_refs, send_sems, recv_sems):
    n = len(g_refs)
    x, y, c = _position()

    def copies():
        return [pltpu.make_async_remote_copy(
            src_ref=g_refs[i].at[j, 1 - c], dst_ref=out_refs[i].at[j], send_sem=send_sems.at[N_CHIPS * i + j],
            recv_sem=recv_sems.at[N_CHIPS * i + j], device_id=(x, y, 1 - c), device_id_type=MESH)
            for i in range(n) for j in range(N_CHIPS)]

    def start():
        for cp in copies():
            cp.start()

    def finish():
        for cp in copies():
            cp.wait()

    return start, finish


def _exchange_scratch(n):
    return [pltpu.SemaphoreType.DMA((N_CHIPS * n,)), pltpu.SemaphoreType.DMA((N_CHIPS * n,))]


def _sibling_exchange(gs, *, name):
    n = len(gs)

    def body(*refs):
        for step in _exchange_plan(refs[:n], refs[n:2 * n], *refs[2 * n:]):
            step()

    return pl.pallas_call(
        body, name=name, in_specs=[HBM_SPEC] * n, out_specs=[HBM_SPEC] * n,
        out_shape=[jax.ShapeDtypeStruct((N_CHIPS,) + g.shape[2:], g.dtype) for g in gs],
        scratch_shapes=_exchange_scratch(n),
    )(*gs)


def _scatter_plan(p_refs, q_refs, send_sems, recv_sems, local_sems):
    n = len(p_refs)
    others = N_CHIPS - 1
    x, y, c = _position()
    me = 2 * x + y
    chips = _other_chips(x, y)

    def copy(i, k, chip, src_slot, dst_slot):
        return pltpu.make_async_remote_copy(
            src_ref=p_refs[i].at[src_slot], dst_ref=q_refs[i].at[dst_slot], send_sem=send_sems.at[others * i + k],
            recv_sem=recv_sems.at[others * i + k], device_id=(*chip, c), device_id_type=MESH)

    def local(i):
        return pltpu.make_async_copy(p_refs[i].at[me], q_refs[i].at[me], local_sems.at[i])

    def sends(i):
        return [copy(i, k, chip, 2 * chip[0] + chip[1], me) for k, chip in enumerate(chips)]

    def start():
        for i in range(n):
            local(i).start()
            for cp in sends(i):
                cp.start()

    def finish():
        for i in range(n):
            for k, chip in enumerate(chips):
                copy(i, k, chip, me, 2 * chip[0] + chip[1]).wait_recv()
        for i in range(n):
            for cp in sends(i):
                cp.wait_send()
            local(i).wait()

    return start, finish


def _scatter_scratch(n):
    others = N_CHIPS - 1
    return [pltpu.SemaphoreType.DMA((others * n,)), pltpu.SemaphoreType.DMA((others * n,)),
            pltpu.SemaphoreType.DMA((n,))]


def _chip_scatter(ps, *, name):
    n = len(ps)

    def body(*refs):
        for step in _scatter_plan(refs[:n], refs[n:2 * n], *refs[2 * n:]):
            step()

    return pl.pallas_call(
        body, name=name, in_specs=[HBM_SPEC] * n, out_specs=[HBM_SPEC] * n,
        out_shape=[jax.ShapeDtypeStruct(p.shape, p.dtype) for p in ps], scratch_shapes=_scatter_scratch(n),
    )(*ps)


def _sibling_share(fs, *, name):
    n = len(fs)

    def body(*refs):
        in_refs, out_refs = refs[:n], refs[n:2 * n]
        send_sems, recv_sems = refs[2 * n:]
        x, y, c = _position()

        def copy(i, half):
            return pltpu.make_async_remote_copy(
                src_ref=in_refs[i].at[half], dst_ref=out_refs[i].at[half], send_sem=send_sems.at[i],
                recv_sem=recv_sems.at[i], device_id=(x, y, 1 - c), device_id_type=MESH)

        sends = [copy(i, c) for i in range(n)]
        for cp in sends:
            cp.start()
        for i in range(n):
            copy(i, 1 - c).wait_recv()
        for cp in sends:
            cp.wait_send()

    return pl.pallas_call(
        body, name=name, in_specs=[HBM_SPEC] * n, out_specs=[HBM_SPEC] * n,
        out_shape=[jax.ShapeDtypeStruct(f.shape, f.dtype) for f in fs],
        input_output_aliases={i: i for i in range(n)},
        scratch_shapes=[pltpu.SemaphoreType.DMA((n,)), pltpu.SemaphoreType.DMA((n,))],
    )(*fs)


def _all_reduce_small(s, *, name):
    R, L = s.shape
    n_dev = 2 * N_CHIPS

    def body(s_ref, out_ref, buf, send_sems, recv_sems, local_sem):
        x, y, c = _position()
        me, sibling = (x, y, c), (x, y, 1 - c)
        chips = _other_chips(x, y)

        def slot(px, py, pc):
            return buf.at[4 * px + 2 * py + pc]

        def copy(k, block, to, src=None):
            return pltpu.make_async_remote_copy(
                src_ref=slot(*block) if src is None else src, dst_ref=slot(*block),
                send_sem=send_sems.at[k], recv_sem=recv_sems.at[k], device_id=to, device_id_type=MESH)

        mine = pltpu.make_async_copy(s_ref, slot(*me), local_sem)
        mine.start()
        first = [copy(0, me, sibling, src=s_ref)]
        first += [copy(1 + j, me, (*chip, c), src=s_ref) for j, chip in enumerate(chips)]
        for cp in first:
            cp.start()
        passed = [copy(4 + j, (*chip, c), sibling) for j, chip in enumerate(chips)]
        for j, chip in enumerate(chips):
            copy(1 + j, (*chip, c), me).wait_recv()
            passed[j].start()
        copy(0, sibling, me).wait_recv()
        for j, chip in enumerate(chips):
            copy(4 + j, (*chip, 1 - c), me).wait_recv()
        for cp in first + passed:
            cp.wait_send()
        mine.wait()
        acc = buf[0]
        for j in range(1, n_dev):
            acc = acc + buf[j]
        out_ref[...] = acc

    vmem = pl.BlockSpec(memory_space=pltpu.VMEM)
    return pl.pallas_call(
        body, name=name, in_specs=[vmem], out_specs=vmem, out_shape=jax.ShapeDtypeStruct((R, L), F32),
        scratch_shapes=[pltpu.VMEM((n_dev, R, L), F32), pltpu.SemaphoreType.DMA((7,)), pltpu.SemaphoreType.DMA((7,)),
                        pltpu.SemaphoreType.DMA],
    )(s)


def _rope_cos_sin(pos, dim, theta):
    inv = jnp.float32(theta) ** (-jnp.arange(0, dim, 2, dtype=F32) / dim)
    ang = pos.astype(F32)[:, None] * inv[None, :]
    return jnp.cos(ang), jnp.sin(ang)


def _rope_tables(T, d, segments):
    P = np.zeros((d, d), np.float32)
    c_parts, s_parts, at = [], [], 0
    for start, size, cos, sin in segments:
        half = size // 2
        if start > at:
            c_parts.append(jnp.ones((T, start - at), F32))
            s_parts.append(jnp.zeros((T, start - at), F32))
        c_parts += [cos, cos]
        s_parts += [-sin, sin]
        at = start + size
        for p in range(half):
            P[start + half + p, start + p] = 1.0
            P[start + p, start + half + p] = 1.0
    if at < d:
        c_parts.append(jnp.ones((T, d - at), F32))
        s_parts.append(jnp.zeros((T, d - at), F32))
    return jnp.concatenate(c_parts, axis=1), jnp.concatenate(s_parts, axis=1), jnp.asarray(P, BF16)


def _heads(t, H, d):
    return t.reshape(t.shape[0], H, d).transpose(1, 0, 2)


def _unheads(t):
    H, T, d = t.shape
    return t.transpose(1, 0, 2).reshape(T, H * d)


def _dw(a, b, *, name, axis):
    K, N = a.shape[1], b.shape[1]
    if axis == 1:
        return _mm(a, b, mode="tn", name=name).reshape(N_CHIPS, K // N_CHIPS, N)
    if (N // N_CHIPS) % LANES == 0:
        return _mm(a, b, mode="tn", name=name, split=N_CHIPS)
    return _mm(a, b, mode="tn", name=name).reshape(K, N_CHIPS, N // N_CHIPS).transpose(1, 0, 2)


def _mlp_fwd(x, gain, w_up, w_down, tag):
    hm = _norm_fwd(x[None], gain, name=f"mlp{tag}_norm")[0]
    u, act = _mm(hm, w_up, mode="nn", name=f"mlp{tag}_up", epi="sqrelu")
    x_out = _mm(act, w_down, mode="nn", name=f"mlp{tag}_down", epi="add", extra=x)
    return x_out, (hm, u, act)


def _mlp_bwd(x, gain, w_up, w_down, saved, dxo, tag):
    hm, u, act = saved
    du = _mm(dxo, w_down, mode="nt", name=f"mlp{tag}_dact", epi="dsqrelu", extra=u, out_dtype=BF16)
    dw_down = _dw(act, dxo, name=f"mlp{tag}_dwdown", axis=1)
    dhm = _mm(du, w_up, mode="nt", name=f"mlp{tag}_dhm")
    dw_up = _dw(hm, du, name=f"mlp{tag}_dwup", axis=2)
    dx, dgain = _norm_bwd(x[None], gain, dhm[None], name=f"mlp{tag}_dnorm", res=dxo)
    return dx[0], dgain[0], dw_up, dw_down


def _local_step(x, target, W, small, late=None):
    T, D = x.shape
    W = dict(W)
    pos = jnp.arange(T)
    mla_cos, mla_sin = _rope_cos_sin(pos, MLA_ROPE, ROPE_THETA)
    row_cos, row_sin = _rope_cos_sin(pos // GRID_W, GQA_DIM // 2, AXIAL_THETA)
    col_cos, col_sin = _rope_cos_sin(pos % GRID_W, GQA_DIM // 2, AXIAL_THETA)
    swa_cos, swa_sin = _rope_cos_sin(pos, SWA_ROT, ROPE_THETA)
    rope_q = _rope_tables(T, MLA_QK, [(MLA_NOPE, MLA_ROPE, mla_cos, mla_sin)])
    rope_kr = _rope_tables(T, MLA_ROPE, [(0, MLA_ROPE, mla_cos, mla_sin)])
    half = GQA_DIM // 2
    rope_ax = _rope_tables(T, GQA_DIM, [(0, half, row_cos, row_sin), (half, half, col_cos, col_sin)])
    rope_sw = _rope_tables(T, SWA_DIM, [(0, SWA_ROT, swa_cos, swa_sin)])
    o1 = MLA_Q_LORA
    o2 = o1 + MLA_KV_LORA
    o3 = o2 + MLA_ROPE
    o4 = o3 + GQA_HEADS * GQA_DIM
    o5 = o4 + GQA_KV * GQA_DIM
    sc_a, sc_g, sc_s = MLA_QK ** -0.5, GQA_DIM ** -0.5, SWA_DIM ** -0.5
    kv_w = MLA_NOPE + MLA_V

    h0 = _norm_fwd(x[None], small["even_norm"], name="even_norm")[0]
    proj = _mm(h0, W["even_w_in"], mode="nn", name="even_in")
    c_q, c_kv, kr_raw = proj[:, :o1], proj[:, o1:o2], proj[:, o2:o3]
    qg_raw = _heads(proj[:, o3:o4], GQA_HEADS, GQA_DIM)
    kg_raw = _heads(proj[:, o4:o5], GQA_KV, GQA_DIM)
    vg = _heads(proj[:, o5:], GQA_KV, GQA_DIM).astype(BF16)
    cqn = _norm_fwd(c_q[None], small["mla_q_lat_norm"], name="q_lat_norm")[0]
    ckvn = _norm_fwd(c_kv[None], small["mla_kv_lat_norm"], name="kv_lat_norm")[0]
    qa_raw = _heads(_mm(cqn, W["mla_w_uq"], mode="nn", name="mla_uq"), MLA_HEADS, MLA_QK)
    kv = _mm(ckvn, W["mla_w_ukv"], mode="nn", name="mla_ukv").reshape(T, MLA_HEADS, kv_w)
    kn_raw = kv[:, :, :MLA_NOPE].transpose(1, 0, 2)
    va = kv[:, :, MLA_NOPE:].transpose(1, 0, 2).astype(BF16)
    q_a = _norm_fwd(qa_raw, small["mla_q_norm"], name="mla_q_prep", rope=rope_q, out_scale=sc_a * LOG2E)
    k_n = _norm_fwd(kn_raw, small["mla_k_nope_norm"], name="mla_kn_prep")
    k_r = _norm_fwd(kr_raw[None], small["mla_k_rope_norm"], name="mla_kr_prep", rope=rope_kr)
    k_a = jnp.concatenate([k_n, jnp.broadcast_to(k_r, (MLA_HEADS, T, MLA_ROPE))], axis=-1)
    if late is None:
        o_a, lse_a = _flash_fwd(q_a, k_a, va, name="mla_attn")
    else:
        o_a, lse_a, gathered = _flash_fwd(q_a, k_a, va, name="mla_attn", gather=late.halves)
        W.update(late.weights(gathered))
    q_g = _norm_fwd(qg_raw, small["gqa_q_norm"], name="gqa_q_prep", rope=rope_ax, out_scale=sc_g * LOG2E)
    k_g = _norm_fwd(kg_raw, small["gqa_k_norm"], name="gqa_k_prep", rope=rope_ax)
    o_g, lse_g = _flash_fwd(q_g, k_g, vg, name="gqa_attn")
    merged = jnp.concatenate([_unheads(o_a), _unheads(o_g)], axis=-1).astype(BF16)
    x1 = _mm(merged, W["even_w_out"], mode="nn", name="even_out", epi="add", extra=x)
    x2, mlp0 = _mlp_fwd(x1, small["mlp_norm"][0], W["mlp_w_up0"], W["mlp_w_down0"], 0)

    h1 = _norm_fwd(x2[None], small["odd_norm"], name="odd_norm")[0]
    qkv = _mm(h1, W["odd_w_qkv"], mode="nn", name="odd_qkv")
    nq, nkk = SWA_HEADS * SWA_DIM, SWA_KV * SWA_DIM
    qs_raw = _heads(qkv[:, :nq], SWA_HEADS, SWA_DIM)
    ks_raw = _heads(qkv[:, nq:nq + nkk], SWA_KV, SWA_DIM)
    vs = _heads(qkv[:, nq + nkk:], SWA_KV, SWA_DIM).astype(BF16)
    q_s = _norm_fwd(qs_raw, small["swa_q_norm"], name="swa_q_prep", rope=rope_sw)
    k_s = _norm_fwd(ks_raw, small["swa_k_norm"], name="swa_k_prep", rope=rope_sw)
    sink = small["swa_sink"].reshape(SWA_HEADS, 1, 1)
    o_s, lse_s = _swa_fwd(q_s, k_s, vs, sink, sc_s, name="swa_attn")
    o_flat = _unheads(o_s).astype(BF16)
    x3 = _mm(o_flat, W["odd_w_out"], mode="nn", name="odd_out", epi="add", extra=x2)
    x4, mlp1 = _mlp_fwd(x3, small["mlp_norm"][1], W["mlp_w_up1"], W["mlp_w_down1"], 1)

    dy, loss_sum = _loss_head(x4, target, name="loss_head")
    gW, gs = {}, {}

    dx3, dg_m1, gW["mlp_w_up1"], gW["mlp_w_down1"] = _mlp_bwd(
        x3, small["mlp_norm"][1], W["mlp_w_up1"], W["mlp_w_down1"], mlp1, dy, 1)
    d_oflat = _mm(dx3, W["odd_w_out"], mode="nt", name="odd_dout")
    gW["odd_w_out"] = _dw(o_flat, dx3, name="odd_dwout", axis=1)
    do_s = _heads(d_oflat, SWA_HEADS, SWA_DIM)
    delta_s, dob_s = _delta(o_s, do_s, name="swa_delta")
    dq_s, dsink = _swa_dq(q_s, k_s, vs, dob_s, lse_s, delta_s, sink, sc_s, name="swa_dq")
    dk_s, dv_s = _swa_dkv(q_s, k_s, vs, dob_s, lse_s, delta_s, sc_s, name="swa_dkv")
    gs["swa_sink"] = dsink[:, 0, 0]
    dqs_raw, gs["swa_q_norm"] = _norm_bwd(qs_raw, small["swa_q_norm"], dq_s, name="swa_dq_prep", rope=rope_sw)
    dks_raw, gs["swa_k_norm"] = _norm_bwd(ks_raw, small["swa_k_norm"], dk_s, name="swa_dk_prep", rope=rope_sw)
    dqkv = jnp.concatenate([_unheads(dqs_raw), _unheads(dks_raw), _unheads(dv_s)], axis=-1).astype(BF16)
    dh1 = _mm(dqkv, W["odd_w_qkv"], mode="nt", name="odd_dh")
    gW["odd_w_qkv"] = _dw(h1, dqkv, name="odd_dwqkv", axis=2)
    dx2, gs["odd_norm"] = _norm_bwd(x2[None], small["odd_norm"], dh1[None], name="odd_dnorm", res=dx3)
    dx2 = dx2[0]

    dx1, dg_m0, gW["mlp_w_up0"], gW["mlp_w_down0"] = _mlp_bwd(
        x1, small["mlp_norm"][0], W["mlp_w_up0"], W["mlp_w_down0"], mlp0, dx2, 0)
    gs["mlp_norm"] = jnp.stack([dg_m0, dg_m1])
    d_merged = _mm(dx1, W["even_w_out"], mode="nt", name="even_dout")
    gW["even_w_out"] = _dw(merged, dx1, name="even_dwout", axis=1)
    na = MLA_HEADS * MLA_V
    do_a = _heads(d_merged[:, :na], MLA_HEADS, MLA_V)
    do_g = _heads(d_merged[:, na:], GQA_HEADS, GQA_DIM)
    delta_a, dob_a = _delta(o_a, do_a, name="mla_delta")
    delta_g, dob_g = _delta(o_g, do_g, name="gqa_delta")
    if late is None:
        dq_a, dk_a, dv_a = _flash_bwd(q_a, k_a, va, dob_a, lse_a, delta_a, name="mla_attn_bwd")
        dq_g, dk_gp, dv_gp = _flash_bwd(q_g, k_g, vg, dob_g, lse_g, delta_g, name="gqa_attn_bwd")
    else:
        halves = late.split(gW)
        dq_a, dk_a, dv_a, from_sibling = _flash_bwd(q_a, k_a, va, dob_a, lse_a, delta_a, name="mla_attn_bwd",
                                                    rider=("exchange", halves))
        dq_g, dk_gp, dv_gp, late.scattered = _flash_bwd(q_g, k_g, vg, dob_g, lse_g, delta_g, name="gqa_attn_bwd",
                                                        rider=("scatter", late.pairs(halves, from_sibling)))
    grp = GQA_HEADS // GQA_KV
    ln2 = 1.0 / LOG2E
    dqg_raw, gs["gqa_q_norm"] = _norm_bwd(qg_raw, small["gqa_q_norm"], dq_g, name="gqa_dq_prep", rope=rope_ax,
                                          dy_scale=sc_g)
    dkg_raw, gs["gqa_k_norm"] = _norm_bwd(kg_raw, small["gqa_k_norm"], dk_gp, name="gqa_dk_prep", rope=rope_ax,
                                          group=grp, dy_scale=ln2)
    dvg = _group_sum(dv_gp, grp, name="gqa_dv_sum")
    dqa_raw, gs["mla_q_norm"] = _norm_bwd(qa_raw, small["mla_q_norm"], dq_a, name="mla_dq_prep", rope=rope_q,
                                          dy_scale=sc_a)
    dkn_raw, gs["mla_k_nope_norm"] = _norm_bwd(kn_raw, small["mla_k_nope_norm"], dk_a[:, :, :MLA_NOPE],
                                               name="mla_dkn_prep", dy_scale=ln2)
    dkr_raw, gs["mla_k_rope_norm"] = _norm_bwd(kr_raw[None], small["mla_k_rope_norm"], dk_a[:, :, MLA_NOPE:],
                                               name="mla_dkr_prep", rope=rope_kr, group=MLA_HEADS, dy_scale=ln2)
    dkv = jnp.concatenate([dkn_raw.transpose(1, 0, 2), dv_a.transpose(1, 0, 2)], axis=-1)
    dkv = dkv.reshape(T, MLA_HEADS * kv_w).astype(BF16)
    dqa = _unheads(dqa_raw).astype(BF16)
    dckvn = _mm(dkv, W["mla_w_ukv"], mode="nt", name="mla_dckv")
    gW["mla_w_ukv"] = _dw(ckvn, dkv, name="mla_dwukv", axis=2)
    dcqn = _mm(dqa, W["mla_w_uq"], mode="nt", name="mla_dcq")
    gW["mla_w_uq"] = _dw(cqn, dqa, name="mla_dwuq", axis=2)
    dc_q, gs["mla_q_lat_norm"] = _norm_bwd(c_q[None], small["mla_q_lat_norm"], dcqn[None], name="q_lat_dnorm")
    dc_kv, gs["mla_kv_lat_norm"] = _norm_bwd(c_kv[None], small["mla_kv_lat_norm"], dckvn[None], name="kv_lat_dnorm")
    dproj = jnp.concatenate([dc_q[0], dc_kv[0], dkr_raw[0], _unheads(dqg_raw), _unheads(dkg_raw), _unheads(dvg)],
                            axis=-1).astype(BF16)
    dh0 = _mm(dproj, W["even_w_in"], mode="nt", name="even_dh")
    gW["even_w_in"] = _dw(h0, dproj, name="even_dwin", axis=2)
    dx0, gs["even_norm"] = _norm_bwd(x[None], small["even_norm"], dh0[None], name="even_dnorm", res=dx1)
    gs = {k: v.reshape(-1) for k, v in gs.items()}
    return loss_sum, dx0[0], gW, gs


BIG = (("even_w_in", 0, 2), ("mla_w_uq", 0, 2), ("mla_w_ukv", 0, 2), ("even_w_out", 0, 1), ("odd_w_qkv", 0, 2),
       ("odd_w_out", 0, 1), ("mlp_w_up", 0, 2), ("mlp_w_up", 1, 2), ("mlp_w_down", 0, 1), ("mlp_w_down", 1, 1))
GATHER_FIRST = ("even_w_in", "mla_w_uq", "mla_w_ukv")
GRADS_LAST = ("even_w_in", "mla_w_uq", "mla_w_ukv")
SMALL = ("even_norm", "mla_q_lat_norm", "mla_kv_lat_norm", "mla_q_norm", "mla_k_nope_norm", "mla_k_rope_norm",
         "gqa_q_norm", "gqa_k_norm", "odd_norm", "swa_q_norm", "swa_k_norm", "swa_sink", "mlp_norm")
def _pad_to(v, n):
    return v if v.shape[-1] == n else jnp.pad(v, [(0, 0)] * (v.ndim - 1) + [(0, n - v.shape[-1])])


def _big_key(name, layer, w):
    return name if w[name].shape[0] == 1 else f"{name}{layer}"


def _pack_rows(flat, rows=8):
    n = flat.shape[0]
    padded = -(-n // (rows * LANES)) * rows * LANES
    return _pad_to(flat, padded).reshape(-1, LANES)


def kernel(x, even_norm, even_w_in, mla_q_lat_norm, mla_kv_lat_norm, mla_w_uq, mla_w_ukv, mla_q_norm, mla_k_nope_norm, mla_k_rope_norm, gqa_q_norm, gqa_k_norm, even_w_out, odd_norm, odd_w_qkv, swa_q_norm, swa_k_norm, swa_sink, odd_w_out, mlp_norm, mlp_w_up, mlp_w_down, loss_target, m_even_norm, m_even_w_in, m_mla_q_lat_norm, m_mla_kv_lat_norm, m_mla_w_uq, m_mla_w_ukv, m_mla_q_norm, m_mla_k_nope_norm, m_mla_k_rope_norm, m_gqa_q_norm, m_gqa_k_norm, m_even_w_out, m_odd_norm, m_odd_w_qkv, m_swa_q_norm, m_swa_k_norm, m_swa_sink, m_odd_w_out, m_mlp_norm, m_mlp_w_up, m_mlp_w_down, v_even_norm, v_even_w_in, v_mla_q_lat_norm, v_mla_kv_lat_norm, v_mla_w_uq, v_mla_w_ukv, v_mla_q_norm, v_mla_k_nope_norm, v_mla_k_rope_norm, v_gqa_q_norm, v_gqa_k_norm, v_even_w_out, v_odd_norm, v_odd_w_qkv, v_swa_q_norm, v_swa_k_norm, v_swa_sink, v_odd_w_out, v_mlp_norm, v_mlp_w_up, v_mlp_w_down):
    w = dict(even_norm=even_norm, even_w_in=even_w_in, mla_q_lat_norm=mla_q_lat_norm, mla_kv_lat_norm=mla_kv_lat_norm,
             mla_w_uq=mla_w_uq, mla_w_ukv=mla_w_ukv, mla_q_norm=mla_q_norm, mla_k_nope_norm=mla_k_nope_norm,
             mla_k_rope_norm=mla_k_rope_norm, gqa_q_norm=gqa_q_norm, gqa_k_norm=gqa_k_norm, even_w_out=even_w_out,
             odd_norm=odd_norm, odd_w_qkv=odd_w_qkv, swa_q_norm=swa_q_norm, swa_k_norm=swa_k_norm, swa_sink=swa_sink,
             odd_w_out=odd_w_out, mlp_norm=mlp_norm, mlp_w_up=mlp_w_up, mlp_w_down=mlp_w_down)
    m = dict(even_norm=m_even_norm, even_w_in=m_even_w_in, mla_q_lat_norm=m_mla_q_lat_norm,
             mla_kv_lat_norm=m_mla_kv_lat_norm, mla_w_uq=m_mla_w_uq, mla_w_ukv=m_mla_w_ukv, mla_q_norm=m_mla_q_norm,
             mla_k_nope_norm=m_mla_k_nope_norm, mla_k_rope_norm=m_mla_k_rope_norm, gqa_q_norm=m_gqa_q_norm,
             gqa_k_norm=m_gqa_k_norm, even_w_out=m_even_w_out, odd_norm=m_odd_norm, odd_w_qkv=m_odd_w_qkv,
             swa_q_norm=m_swa_q_norm, swa_k_norm=m_swa_k_norm, swa_sink=m_swa_sink, odd_w_out=m_odd_w_out,
             mlp_norm=m_mlp_norm, mlp_w_up=m_mlp_w_up, mlp_w_down=m_mlp_w_down)
    v = dict(even_norm=v_even_norm, even_w_in=v_even_w_in, mla_q_lat_norm=v_mla_q_lat_norm,
             mla_kv_lat_norm=v_mla_kv_lat_norm, mla_w_uq=v_mla_w_uq, mla_w_ukv=v_mla_w_ukv, mla_q_norm=v_mla_q_norm,
             mla_k_nope_norm=v_mla_k_nope_norm, mla_k_rope_norm=v_mla_k_rope_norm, gqa_q_norm=v_gqa_q_norm,
             gqa_k_norm=v_gqa_k_norm, even_w_out=v_even_w_out, odd_norm=v_odd_norm, odd_w_qkv=v_odd_w_qkv,
             swa_q_norm=v_swa_q_norm, swa_k_norm=v_swa_k_norm, swa_sink=v_swa_sink, odd_w_out=v_odd_w_out,
             mlp_norm=v_mlp_norm, mlp_w_up=v_mlp_w_up, mlp_w_down=v_mlp_w_down)
    xi, yi, ci = _position()
    chip = 2 * xi + yi
    T, D = x.shape[1], x.shape[2]

    c_idx = ci.reshape(1).astype(jnp.int32)
    key_of = lambda entry: _big_key(entry[0], entry[1], w)
    first_use = [e for e in BIG if e[0] in GATHER_FIRST]
    later_use = [e for e in BIG if e[0] not in GATHER_FIRST]
    early_grads = [e for e in BIG if e[0] not in GRADS_LAST]
    last_grads = [e for e in BIG if e[0] in GRADS_LAST]

    as_bf16 = {}

    def halves_of(entries):
        out = []
        for name, layer, _ in entries:
            layers, ks, ns = w[name].shape
            if name not in as_bf16:
                as_bf16[name] = _to_bf16(w[name].reshape(layers * ks, ns), name=f"to_bf16_{name}")
            out.append(as_bf16[name].reshape(layers, 2, ks // 2, ns)[layer])
        return out

    def weights_of(entries, gathered):
        out = {}
        for (name, layer, axis), g in zip(entries, gathered):
            ks, ns = w[name].shape[1:]
            stacked = g.reshape(N_CHIPS, ks, ns)
            if axis == 1:
                out[_big_key(name, layer, w)] = stacked.reshape(N_CHIPS * ks, ns)
            else:
                out[_big_key(name, layer, w)] = stacked.transpose(1, 0, 2).reshape(ks, N_CHIPS * ns)
        return out

    def split_halves(entries, gW):
        out = []
        for entry in entries:
            _, ks, ns = gW[key_of(entry)].shape
            out.append(gW[key_of(entry)].reshape(N_CHIPS, 2, ks // 2, ns))
        return out

    def pair_sums(entries, g_all, from_sibling):
        return [_pair_add(g, r, c_idx, name=f"grad_pair_add_{key_of(e)}") for e, g, r in zip(entries, g_all, from_sibling)]

    class _Late:
        halves = halves_of(later_use)
        scattered = None

        @staticmethod
        def weights(gathered):
            return weights_of(later_use, gathered)

        @staticmethod
        def split(gW):
            return split_halves(early_grads, gW)

        @staticmethod
        def pairs(g_all, from_sibling):
            return pair_sums(early_grads, g_all, from_sibling)

    late = _Late()
    W = weights_of(first_use, _all_gather_halves(halves_of(first_use), name="weights_all_gather"))

    odd_full = jnp.zeros((N_CHIPS, D // N_CHIPS), F32).at[chip].set(jnp.where(ci == 0, 1.0, 0.0) * w["odd_norm"][0])
    odd_full = _all_reduce_small(_pack_rows(odd_full.reshape(-1)), name="odd_norm_gather").reshape(-1)[:D]
    small = {name: w[name][0] for name in SMALL if name not in ("mlp_norm", "odd_norm")}
    small["mlp_norm"] = w["mlp_norm"]
    small["odd_norm"] = odd_full

    loss_sum, grad_x, gW, gs = _local_step(x[0], loss_target[0], W, small, late)

    loss_local = 0.5 * loss_sum.reshape(1) / D
    small_sizes = [(name, int(gs[name].shape[0])) for name in SMALL]
    ar_in = jnp.concatenate([_pad_to(loss_local, LANES)] + [gs[name] for name in SMALL])
    ar_out = _all_reduce_small(_pack_rows(ar_in), name="small_all_reduce").reshape(-1)
    loss = ar_out[0]
    g_small, off = {}, LANES
    for name, n in small_sizes:
        g_small[name] = ar_out[off:off + n]
        off += n
    shard_d = D // N_CHIPS
    g_small["odd_norm"] = lax.dynamic_slice(g_small["odd_norm"], (chip * shard_d,), (shard_d,))

    from_chips = dict(zip(map(key_of, early_grads), late.scattered))
    last_halves = split_halves(last_grads, gW)
    last_pairs = pair_sums(last_grads, last_halves, _sibling_exchange(last_halves, name="grad_sibling_exchange"))
    last_scattered = _chip_scatter(last_pairs, name="grad_chip_scatter")
    from_chips.update(zip(map(key_of, last_grads), last_scattered))
    keys = [key_of(e) for e in BIG]
    reduced = [_sum_chips(from_chips[key], c_idx, name=f"grad_chip_sum_{key}") for key in keys]
    shared = _sibling_share(reduced, name="grad_sibling_share")
    g_shards = {}
    for (name, layer, _), f in zip(BIG, shared):
        g_shards.setdefault(name, []).append(f.reshape(w[name].shape[1:]))

    grads, deltas, new_m, new_v = {}, {}, {}, {}
    for name in g_shards:
        shape = w[name].shape
        g = jnp.stack(g_shards[name])
        grads[name] = g
        two_d = (shape[0] * shape[1], shape[2])
        d_, m_, v_ = _adamw(w[name].reshape(two_d), g.reshape(two_d), m[name].reshape(two_d), v[name].reshape(two_d),
                            name=f"adamw_{name}")
        deltas[name], new_m[name], new_v[name] = d_.reshape(shape), m_.reshape(shape), v_.reshape(shape)
    pack_small = lambda d: _pack_rows(jnp.concatenate([d[name].reshape(-1) for name in SMALL]))
    for name in SMALL:
        grads[name] = g_small[name].reshape(w[name].shape)
    d_, m_, v_ = _adamw(pack_small(w), pack_small(grads), pack_small(m), pack_small(v), name="adamw_small")
    d_, m_, v_ = d_.reshape(-1), m_.reshape(-1), v_.reshape(-1)
    off = 0
    for name in SMALL:
        n = int(np.prod(w[name].shape))
        deltas[name] = d_[off:off + n].reshape(w[name].shape)
        new_m[name] = m_[off:off + n].reshape(w[name].shape)
        new_v[name] = v_[off:off + n].reshape(w[name].shape)
        off += n

    order = ("even_norm", "even_w_in", "mla_q_lat_norm", "mla_kv_lat_norm", "mla_w_uq", "mla_w_ukv", "mla_q_norm",
             "mla_k_nope_norm", "mla_k_rope_norm", "gqa_q_norm", "gqa_k_norm", "even_w_out", "odd_norm", "odd_w_qkv",
             "swa_q_norm", "swa_k_norm", "swa_sink", "odd_w_out", "mlp_norm", "mlp_w_up", "mlp_w_down")
    outs = [loss, grad_x[None]]
    for group in (grads, deltas, new_m, new_v):
        outs += [group[name] for name in order]
    return tuple(outs)
```

```python
import functools
import math

import numpy as np
import jax
import jax.numpy as jnp
from jax import lax
from jax.experimental import pallas as pl
from jax.experimental.pallas import tpu as pltpu

F32 = jnp.float32
BF16 = jnp.bfloat16
MESH = pl.DeviceIdType.MESH

VMEM_BYTES_V7X = 64 * 1024 * 1024
LANES = 128
SUBLANES_BF16 = 16

GRID_W = 64
NORM_EPS = 1e-6
ROPE_THETA = 500000.0
AXIAL_THETA = 10000.0
MLA_HEADS = 8
MLA_Q_LORA = 512
MLA_KV_LORA = 256
MLA_NOPE = 128
MLA_ROPE = 64
MLA_QK = MLA_NOPE + MLA_ROPE
MLA_V = 128
GQA_HEADS = 8
GQA_KV = 2
GQA_DIM = 128
SWA_HEADS = 32
SWA_KV = 4
SWA_DIM = 64
SWA_WINDOW = 128
SWA_ROT = SWA_DIM // 4
SWA_BLOCK = 128
SWA_HEAD_PARTS = 2
SWA_GROUPS = 4
ADAM_LR = 0.001
ADAM_B1 = 0.9
ADAM_B2 = 0.999
ADAM_EPS = 1e-08
ADAM_WD = 0.01
ADAM_STEP = 10
N_CHIPS = 4
COMM_LANES = 1024


def _tile(dim, cap, mult=LANES):
    if dim <= cap:
        return dim
    t = (cap // mult) * mult
    while t >= mult:
        if dim % t == 0:
            return t
        t -= mult
    return dim


def _params(dims, vmem_estimate):
    limit = int(min(max(vmem_estimate * 1.25 + (4 << 20), 32 << 20), VMEM_BYTES_V7X - (6 << 20)))
    return pltpu.CompilerParams(dimension_semantics=dims, vmem_limit_bytes=limit)


def _nbytes(shape, dtype):
    return int(np.prod(shape)) * jnp.dtype(dtype).itemsize


def _mm(a, b, *, mode, name, out_dtype=F32, epi=None, extra=None, split=1, caps=(1024, 1024, 2048)):
    if mode == "nn":
        (M, K), (K2, N) = a.shape, b.shape
    elif mode == "nt":
        (M, K), (N, K2) = a.shape, b.shape
    else:
        (K, M), (K2, N) = a.shape, b.shape
    assert K == K2, (a.shape, b.shape, mode)
    assert N % split == 0
    ns = N // split
    tn, tk = _tile(ns, caps[1]), _tile(K, caps[2])
    tm = _tile(M, min(caps[0], max(LANES, caps[0] * caps[1] // tn)))
    nj_per = ns // tn
    grid = (M // tm, N // tn, K // tk)
    nk = grid[2]
    if mode == "nn":
        a_spec = pl.BlockSpec((tm, tk), lambda i, j, k: (i, k))
        b_spec = pl.BlockSpec((tk, tn), lambda i, j, k: (k, j))
        dn = (((1,), (0,)), ((), ()))
    elif mode == "nt":
        a_spec = pl.BlockSpec((tm, tk), lambda i, j, k: (i, k))
        b_spec = pl.BlockSpec((tn, tk), lambda i, j, k: (j, k))
        dn = (((1,), (1,)), ((), ()))
    else:
        a_spec = pl.BlockSpec((tk, tm), lambda i, j, k: (k, i))
        b_spec = pl.BlockSpec((tk, tn), lambda i, j, k: (k, j))
        dn = (((0,), (0,)), ((), ()))
    if split == 1:
        o_spec = pl.BlockSpec((tm, tn), lambda i, j, k: (i, j))
        o_shape = (M, N)
    else:
        o_spec = pl.BlockSpec((None, tm, tn), lambda i, j, k: (j // nj_per, i, j % nj_per))
        o_shape = (split, M, ns)
    mn_spec = pl.BlockSpec((tm, tn), lambda i, j, k: (i, j))
    in_specs, args = [a_spec, b_spec], [a, b]
    if epi in ("add", "dsqrelu"):
        in_specs.append(mn_spec)
        args.append(extra)
    if epi == "sqrelu":
        out_shape = (jax.ShapeDtypeStruct(o_shape, BF16), jax.ShapeDtypeStruct(o_shape, BF16))
        out_specs = (o_spec, o_spec)
        n_out = 2
    else:
        out_shape = jax.ShapeDtypeStruct(o_shape, out_dtype)
        out_specs = o_spec
        n_out = 1

    def body(*refs):
        a_ref, b_ref = refs[0], refs[1]
        e_ref = refs[2] if len(args) == 3 else None
        outs = refs[len(args):len(args) + n_out]

        def finish(acc):
            if epi is None:
                outs[0][...] = acc.astype(outs[0].dtype)
            elif epi == "add":
                outs[0][...] = (e_ref[...] + acc).astype(outs[0].dtype)
            elif epi == "sqrelu":
                r = jnp.maximum(acc, 0.0)
                outs[0][...] = acc.astype(BF16)
                outs[1][...] = (r * r).astype(BF16)
            else:
                u = e_ref[...].astype(F32)
                outs[0][...] = (acc * (2.0 * jnp.maximum(u, 0.0))).astype(outs[0].dtype)

        prod = lax.dot_general(a_ref[...].astype(BF16), b_ref[...].astype(BF16), dn, preferred_element_type=F32)
        if nk == 1:
            finish(prod)
            return
        acc_ref = refs[-1]
        k = pl.program_id(2)

        @pl.when(k == 0)
        def _():
            acc_ref[...] = prod

        @pl.when((k != 0) & (k != nk - 1))
        def _():
            acc_ref[...] += prod

        @pl.when(k == nk - 1)
        def _():
            finish(acc_ref[...] + prod)

    est = 2 * (_nbytes((tm, tk), a.dtype) + _nbytes((tk, tn), b.dtype)) + _nbytes((tm, tn), F32)
    est += 2 * n_out * _nbytes((tm, tn), out_dtype if n_out == 1 else BF16)
    if len(args) == 3:
        est += 2 * _nbytes((tm, tn), extra.dtype)
    est += 3 * _nbytes((tm, tn), F32)
    return pl.pallas_call(
        body, name=name, grid=grid, in_specs=in_specs, out_specs=out_specs, out_shape=out_shape,
        scratch_shapes=[] if nk == 1 else [pltpu.VMEM((tm, tn), F32)],
        compiler_params=_params(("parallel", "parallel", "arbitrary"), est),
    )(*args)


def _perm(y, p):
    hi = y.astype(BF16)
    lo = (y - hi.astype(F32)).astype(BF16)
    d = lambda t: jnp.dot(t, p, preferred_element_type=F32)
    return d(hi) + d(lo)


def _rows_tile(T, d):
    return _tile(T, 2048 if d <= 256 else 512, 128)


def _norm_fwd(x, gain, *, name, rope=None, out_dtype=BF16, out_scale=None):
    H, T, d = x.shape
    tm = _rows_tile(T, d)
    g2 = gain.reshape(1, d).astype(F32)
    in_specs = [pl.BlockSpec((None, tm, d), lambda h, i: (h, i, 0)), pl.BlockSpec((1, d), lambda h, i: (0, 0))]
    args = [x, g2]
    if rope is not None:
        in_specs += [pl.BlockSpec((tm, d), lambda h, i: (i, 0)), pl.BlockSpec((tm, d), lambda h, i: (i, 0)),
                     pl.BlockSpec((d, d), lambda h, i: (0, 0))]
        args += list(rope)

    def body(*refs):
        x_ref, g_ref = refs[0], refs[1]
        o_ref = refs[-1]
        xv = x_ref[...]
        y = xv * lax.rsqrt(jnp.mean(xv * xv, axis=-1, keepdims=True) + NORM_EPS)
        y = y * g_ref[...]
        if rope is not None:
            c_ref, s_ref, p_ref = refs[2], refs[3], refs[4]
            y = y * c_ref[...] + _perm(y, p_ref[...]) * s_ref[...]
        if out_scale is not None:
            y = y * out_scale
        o_ref[...] = y.astype(o_ref.dtype)

    est = 2 * (_nbytes((tm, max(d, LANES)), F32) * (3 if rope is not None else 1) + _nbytes((tm, max(d, LANES)), out_dtype))
    est += 6 * _nbytes((tm, max(d, LANES)), F32)
    return pl.pallas_call(
        body, name=name, grid=(H, T // tm), in_specs=in_specs,
        out_specs=pl.BlockSpec((None, tm, d), lambda h, i: (h, i, 0)),
        out_shape=jax.ShapeDtypeStruct((H, T, d), out_dtype),
        compiler_params=_params(("parallel", "parallel"), est),
    )(*args)


def _norm_bwd(x, gain, dy, *, name, rope=None, group=1, res=None, out_dtype=F32, dy_scale=None):
    H, T, d = x.shape
    assert dy.shape == (H * group, T, d), (dy.shape, x.shape, group)
    tm = _rows_tile(T, d)
    g2 = gain.reshape(1, d).astype(F32)
    in_specs = [pl.BlockSpec((None, tm, d), lambda h, i: (h, i, 0)), pl.BlockSpec((1, d), lambda h, i: (0, 0)),
                pl.BlockSpec((group, tm, d), lambda h, i: (h, i, 0))]
    args = [x, g2, dy]
    if rope is not None:
        in_specs += [pl.BlockSpec((tm, d), lambda h, i: (i, 0)), pl.BlockSpec((tm, d), lambda h, i: (i, 0)),
                     pl.BlockSpec((d, d), lambda h, i: (0, 0))]
        args += list(rope)
    if res is not None:
        assert H == 1
        in_specs.append(pl.BlockSpec((tm, d), lambda h, i: (i, 0)))
        args.append(res)
    n_in = len(args)

    def body(*refs):
        x_ref, g_ref, dy_ref = refs[0], refs[1], refs[2]
        dx_ref, dg_ref = refs[n_in], refs[n_in + 1]
        first = (pl.program_id(0) == 0) & (pl.program_id(1) == 0)

        @pl.when(first)
        def _():
            dg_ref[...] = jnp.zeros_like(dg_ref)

        dyv = dy_ref[0].astype(F32)
        for g in range(1, group):
            dyv = dyv + dy_ref[g].astype(F32)
        if dy_scale is not None:
            dyv = dyv * dy_scale
        pos = 3
        if rope is not None:
            c_ref, s_ref, p_ref = refs[3], refs[4], refs[5]
            pos = 6
            dyv = dyv * c_ref[...] + _perm(dyv * s_ref[...], p_ref[...])
        xv = x_ref[...]
        r = lax.rsqrt(jnp.mean(xv * xv, axis=-1, keepdims=True) + NORM_EPS)
        xhat = xv * r
        dg_ref[...] += jnp.sum(dyv * xhat, axis=0, keepdims=True)
        dxh = dyv * g_ref[...]
        dx = r * (dxh - xhat * jnp.mean(dxh * xhat, axis=-1, keepdims=True))
        if res is not None:
            dx = dx + refs[pos][...]
        dx_ref[...] = dx.astype(dx_ref.dtype)

    wide = max(d, LANES)
    est = 2 * _nbytes((tm, wide), F32) * (2 + group + (2 if rope is not None else 0) + (1 if res is not None else 0))
    est += 8 * _nbytes((tm, wide), F32)
    return pl.pallas_call(
        body, name=name, grid=(H, T // tm), in_specs=in_specs,
        out_specs=(pl.BlockSpec((None, tm, d), lambda h, i: (h, i, 0)), pl.BlockSpec((1, d), lambda h, i: (0, 0))),
        out_shape=(jax.ShapeDtypeStruct((H, T, d), out_dtype), jax.ShapeDtypeStruct((1, d), F32)),
        compiler_params=_params(("arbitrary", "arbitrary"), est),
    )(*args)


def _group_sum(x, group, *, name, out_dtype=F32):
    HG, T, d = x.shape
    H = HG // group
    tm = _rows_tile(T, d)

    def body(x_ref, o_ref):
        acc = x_ref[0]
        for g in range(1, group):
            acc = acc + x_ref[g]
        o_ref[...] = acc.astype(o_ref.dtype)

    est = 2 * (group + 1) * _nbytes((tm, max(d, LANES)), F32)
    return pl.pallas_call(
        body, name=name, grid=(H, T // tm),
        in_specs=[pl.BlockSpec((group, tm, d), lambda h, i: (h, i, 0))],
        out_specs=pl.BlockSpec((None, tm, d), lambda h, i: (h, i, 0)),
        out_shape=jax.ShapeDtypeStruct((H, T, d), out_dtype),
        compiler_params=_params(("parallel", "parallel"), est),
    )(x)


def _delta(o, do, *, name):
    H, T, d = o.shape
    tm = _rows_tile(T, d)

    def body(o_ref, do_ref, dl_ref, dob_ref):
        dov = do_ref[...]
        dl = jnp.sum(o_ref[...] * dov, axis=-1, keepdims=True)
        dl_ref[...] = jnp.broadcast_to(dl, (tm, LANES))
        dob_ref[...] = dov.astype(BF16)

    spec = pl.BlockSpec((None, tm, d), lambda h, i: (h, i, 0))
    est = 2 * (3 * _nbytes((tm, max(d, LANES)), F32) + _nbytes((tm, LANES), F32))
    return pl.pallas_call(
        body, name=name, grid=(H, T // tm), in_specs=[spec, spec],
        out_specs=(pl.BlockSpec((None, tm, LANES), lambda h, i: (h, i, 0)), spec),
        out_shape=(jax.ShapeDtypeStruct((H, T, LANES), F32), jax.ShapeDtypeStruct((H, T, d), BF16)),
        compiler_params=_params(("parallel", "parallel"), est),
    )(o, do)


NT_DIMS = (((1,), (1,)), ((), ()))
TN_DIMS = (((0,), (0,)), ((), ()))
LOG2E = math.log2(math.e)
FLASH_CHUNK = 256
FLASH_ROW_PARTS = 8


def _flash_fwd(q, k, v, *, name, gather=()):
    H, T, dk = q.shape
    Hkv, _, dv = v.shape
    G = H // Hkv
    tq, tk = _tile(T, 2048), _tile(T, 2048)
    tp = _tile(tq, tq // FLASH_ROW_PARTS, SUBLANES_BF16)
    nk = T // tk

    n_r = len(gather)
    grid = (H, T // tq, nk)
    assert n_r == 0 or H >= 2

    def body(*refs):
        q_ref, k_ref, v_ref = refs[:3]
        o_ref, lse_ref, ob_ref = refs[3 + n_r:6 + n_r]
        m_ref, l_ref, acc_ref = refs[6 + 2 * n_r:9 + 2 * n_r]
        hi, qi, ki = pl.program_id(0), pl.program_id(1), pl.program_id(2)
        if n_r:
            ag_start, ag_forward, ag_finish = _gather_plan(refs[3:3 + n_r], refs[6 + n_r:6 + 2 * n_r],
                                                           *refs[9 + 2 * n_r:])
            pl.when((hi == 0) & (qi == 0) & (ki == 0))(ag_start)
            pl.when((hi == grid[0] - 1) & (qi == 0) & (ki == 0))(ag_forward)

        @pl.when(ki == 0)
        def _():
            m_ref[...] = jnp.full_like(m_ref, -jnp.inf)
            l_ref[...] = jnp.zeros_like(l_ref)
            acc_ref[...] = jnp.zeros_like(acc_ref)

        kv, vv = k_ref[...], v_ref[...]
        parts = [slice(part * tp, (part + 1) * tp) for part in range(tq // tp)]
        m_prev = [m_ref[rows, :] for rows in parts]
        l_prev = [l_ref[rows, :] for rows in parts]
        a_prev = [acc_ref[rows, :] for rows in parts]
        ss = [lax.dot_general(q_ref[rows, :], kv, NT_DIMS, preferred_element_type=F32) for rows in parts]
        m_new = [jnp.maximum(m, jnp.max(s, axis=-1, keepdims=True)) for m, s in zip(m_prev, ss)]
        alpha = [jnp.exp2(m - mn) for m, mn in zip(m_prev, m_new)]
        ps = [jnp.exp2(s - mn) for s, mn in zip(ss, m_new)]
        l_new = [a * l + jnp.sum(p, axis=-1, keepdims=True) for a, l, p in zip(alpha, l_prev, ps)]
        pv = [jnp.dot(p.astype(BF16), vv, preferred_element_type=F32) for p in ps]
        for rows, mn, ln, a, acc, o in zip(parts, m_new, l_new, alpha, a_prev, pv):
            m_ref[rows, :] = mn
            l_ref[rows, :] = ln
            acc_ref[rows, :] = a * acc + o

        @pl.when(ki == nk - 1)
        def _():
            l = l_ref[...]
            o = acc_ref[...] / l
            o_ref[...] = o
            ob_ref[...] = o.astype(BF16)
            lse_ref[...] = jnp.broadcast_to(m_ref[...] + jnp.log(l) * LOG2E, (tq, LANES))

        if n_r:
            pl.when((hi == grid[0] - 1) & (qi == grid[1] - 1) & (ki == grid[2] - 1))(ag_finish)

    est = 2 * (_nbytes((tq, dk), BF16) + _nbytes((tk, dk + dv), BF16) + _nbytes((tq, dv + LANES), F32))
    est += 4 * _nbytes((tq, tk), F32) + 3 * _nbytes((tq, dv + 3 * LANES), F32)
    outs = pl.pallas_call(
        body, name=name, grid=grid,
        in_specs=[pl.BlockSpec((None, tq, dk), lambda h, i, j: (h, i, 0)),
                  pl.BlockSpec((None, tk, dk), lambda h, i, j: (h // G, j, 0)),
                  pl.BlockSpec((None, tk, dv), lambda h, i, j: (h // G, j, 0))] + [HBM_SPEC] * n_r,
        out_specs=[pl.BlockSpec((None, tq, dv), lambda h, i, j: (h, i, 0)),
                   pl.BlockSpec((None, tq, LANES), lambda h, i, j: (h, i, 0)),
                   pl.BlockSpec((None, tq, dv), lambda h, i, j: (h, i, 0))] + [HBM_SPEC] * n_r,
        out_shape=[jax.ShapeDtypeStruct((H, T, dv), F32), jax.ShapeDtypeStruct((H, T, LANES), F32),
                   jax.ShapeDtypeStruct((H, T, dv), BF16)] + _gathered_shapes(gather),
        scratch_shapes=[pltpu.VMEM((tq, 1), F32), pltpu.VMEM((tq, 1), F32), pltpu.VMEM((tq, dv), F32)]
                       + (_gather_scratch(n_r) if n_r else []),
        compiler_params=_params(("arbitrary",) * 3 if n_r else ("parallel", "parallel", "arbitrary"), est),
    )(q, k, v, *gather)
    return (outs[0], outs[1], outs[2], outs[3:]) if n_r else (outs[0], outs[1], outs[2])


def _flash_bwd(q, k, v, do, lse2, delta, *, name, rider=None):
    H, T, dk = q.shape
    Hkv, _, dv = v.shape
    G = H // Hkv
    tq, tk = _tile(T, 1024), _tile(T, 2048)
    tc = _tile(tk, FLASH_CHUNK)

    kind, carried = rider if rider is not None else (None, ())
    n_r = len(carried)
    grid = (H, T // tk, T // tq)
    if kind == "scatter":
        plan, rider_scratch = _scatter_plan, _scatter_scratch(n_r)
        rider_shapes = [jax.ShapeDtypeStruct(p.shape, p.dtype) for p in carried]
    elif kind == "exchange":
        plan, rider_scratch = _exchange_plan, _exchange_scratch(n_r)
        rider_shapes = [jax.ShapeDtypeStruct((N_CHIPS,) + g.shape[2:], g.dtype) for g in carried]
    else:
        assert kind is None
        rider_scratch, rider_shapes = [], []

    def body(*refs):
        q_ref, k_ref, v_ref, do_ref, lse_ref, dl_ref = refs[:6]
        dq_ref, dk_ref, dv_ref = refs[6 + n_r:9 + n_r]
        hi, ki, qi = pl.program_id(0), pl.program_id(1), pl.program_id(2)
        if n_r:
            rider_start, rider_finish = plan(refs[6:6 + n_r], refs[9 + n_r:9 + 2 * n_r], *refs[9 + 2 * n_r:])
            pl.when((hi == 0) & (ki == 0) & (qi == 0))(rider_start)
        rows = pl.ds(pl.multiple_of(qi * tq, tq), tq)

        @pl.when(qi == 0)
        def _():
            dk_ref[...] = jnp.zeros_like(dk_ref)
            dv_ref[...] = jnp.zeros_like(dv_ref)

        @pl.when(ki == 0)
        def _():
            dq_ref[rows, :] = jnp.zeros((tq, dk), F32)

        qv, dov = q_ref[...], do_ref[...]
        lse2 = lse_ref[:, :1]
        dl = dl_ref[:, :1]
        chunks = [slice(c * tc, (c + 1) * tc) for c in range(tk // tc)]
        kcs = [k_ref[ks, :] for ks in chunks]
        vcs = [v_ref[ks, :] for ks in chunks]
        dv_old = [dv_ref[ks, :] for ks in chunks]
        dk_old = [dk_ref[ks, :] for ks in chunks]
        dq_old = dq_ref[rows, :]
        ss = [lax.dot_general(qv, kc, NT_DIMS, preferred_element_type=F32) for kc in kcs]
        dps = [lax.dot_general(dov, vc, NT_DIMS, preferred_element_type=F32) for vc in vcs]
        ps = [jnp.exp2(s - lse2) for s in ss]
        dss = [(p * (dp - dl)).astype(BF16) for p, dp in zip(ps, dps)]
        pbs = [p.astype(BF16) for p in ps]
        dvs = [lax.dot_general(pb, dov, TN_DIMS, preferred_element_type=F32) for pb in pbs]
        dks = [lax.dot_general(ds, qv, TN_DIMS, preferred_element_type=F32) for ds in dss]
        dqs = [jnp.dot(ds, kc, preferred_element_type=F32) for ds, kc in zip(dss, kcs)]
        for ks, old, new in zip(chunks, dv_old, dvs):
            dv_ref[ks, :] = old + new
        for ks, old, new in zip(chunks, dk_old, dks):
            dk_ref[ks, :] = old + new
        dq_c = dqs[0]
        for extra in dqs[1:]:
            dq_c = dq_c + extra
        dq_ref[rows, :] = dq_old + dq_c

        if n_r:
            pl.when((hi == grid[0] - 1) & (ki == grid[1] - 1) & (qi == grid[2] - 1))(rider_finish)

    est = 2 * (_nbytes((tq, dk + dv), BF16) + _nbytes((tk, dk + dv), BF16) + 2 * _nbytes((tq, LANES), F32))
    est += 2 * (_nbytes((T, dk), F32) + _nbytes((tk, dk + dv), F32)) + 10 * _nbytes((tq, tc), F32)
    outs = pl.pallas_call(
        body, name=name, grid=grid,
        in_specs=[pl.BlockSpec((None, tq, dk), lambda h, j, i: (h, i, 0)),
                  pl.BlockSpec((None, tk, dk), lambda h, j, i: (h // G, j, 0)),
                  pl.BlockSpec((None, tk, dv), lambda h, j, i: (h // G, j, 0)),
                  pl.BlockSpec((None, tq, dv), lambda h, j, i: (h, i, 0)),
                  pl.BlockSpec((None, tq, LANES), lambda h, j, i: (h, i, 0)),
                  pl.BlockSpec((None, tq, LANES), lambda h, j, i: (h, i, 0))] + [HBM_SPEC] * n_r,
        out_specs=[pl.BlockSpec((None, T, dk), lambda h, j, i: (h, 0, 0)),
                   pl.BlockSpec((None, tk, dk), lambda h, j, i: (h, j, 0)),
                   pl.BlockSpec((None, tk, dv), lambda h, j, i: (h, j, 0))] + [HBM_SPEC] * n_r,
        out_shape=[jax.ShapeDtypeStruct((H, T, dk), F32), jax.ShapeDtypeStruct((H, T, dk), F32),
                   jax.ShapeDtypeStruct((H, T, dv), F32)] + rider_shapes,
        scratch_shapes=rider_scratch,
        compiler_params=_params(("arbitrary", "arbitrary", "arbitrary"), est),
    )(q, k, v, do, lse2, delta, *carried)
    return (outs[0], outs[1], outs[2], outs[3:]) if n_r else tuple(outs)


def _swa_specs(G, d, n_blocks, lanes, gpb):
    B = SWA_BLOCK
    prev = lambda j, i: (j, jnp.maximum(i - 1, 0), 0)
    cur = lambda j, i: (j, i, 0)
    nxt = lambda j, i: (j, jnp.minimum(i + 1, n_blocks - 1), 0)
    q_specs = [pl.BlockSpec((gpb * G, B, lanes), m) for m in (prev, cur, nxt)]
    kv_specs = [pl.BlockSpec((gpb, B, d), m) for m in (prev, cur, nxt)]
    return q_specs, kv_specs, cur


def _swa_parts(G, gpb):
    gp = G // SWA_HEAD_PARTS
    return gp, [(g, slice(g * G + part * gp, g * G + (part + 1) * gp)) for g in range(gpb) for part in range(SWA_HEAD_PARTS)]


def _swa_bias(i, T):
    B = SWA_BLOCK
    row = lax.broadcasted_iota(jnp.int32, (B, 3 * B), 0)
    col = lax.broadcasted_iota(jnp.int32, (B, 3 * B), 1)
    kpos = (i - 1) * B + col
    valid = (col >= row) & (col <= row + 2 * SWA_WINDOW) & (kpos >= 0) & (kpos < T)
    return jnp.where(valid, 0.0, -jnp.inf)


def _swa_fwd(q, k, v, sink, scale, *, name):
    Hq, T, d = q.shape
    Hkv = k.shape[0]
    G = Hq // Hkv
    B = SWA_BLOCK
    nb = T // B
    gpb = _tile(Hkv, SWA_GROUPS, 1)
    _, kv_specs, cur = _swa_specs(G, d, nb, d, gpb)

    def body(q_ref, k0, k1, k2, v0, v1, v2, sink_ref, o_ref, lse_ref, ob_ref):
        i = pl.program_id(1)
        kvs = [jnp.concatenate([k0[g], k1[g], k2[g]], axis=0) for g in range(gpb)]
        vvs = [jnp.concatenate([v0[g], v1[g], v2[g]], axis=0) for g in range(gpb)]
        bias = _swa_bias(i, T)[None]
        gp, parts = _swa_parts(G, gpb)
        sks = [sink_ref[hs] for _, hs in parts]
        ss = [lax.dot_general(q_ref[hs].reshape(gp * B, d), kvs[g], NT_DIMS, preferred_element_type=F32) for g, hs in parts]
        ss = [(s * scale).reshape(gp, B, 3 * B) + bias for s in ss]
        ms = [jnp.maximum(jnp.max(s, axis=-1, keepdims=True), sk) for s, sk in zip(ss, sks)]
        ps = [jnp.exp(s - m) for s, m in zip(ss, ms)]
        dens = [jnp.sum(p, axis=-1, keepdims=True) + jnp.exp(sk - m) for p, sk, m in zip(ps, sks, ms)]
        pns = [(p / den).reshape(gp * B, 3 * B).astype(BF16) for p, den in zip(ps, dens)]
        os_ = [jnp.dot(pn, vvs[g], preferred_element_type=F32).reshape(gp, B, d) for pn, (g, _) in zip(pns, parts)]
        for (_, hs), o, m, den in zip(parts, os_, ms, dens):
            o_ref[hs] = o
            ob_ref[hs] = o.astype(BF16)
            lse_ref[hs] = jnp.broadcast_to(m + jnp.log(den), (gp, B, LANES))

    GG = gpb * G
    est = 2 * (_nbytes((GG, B, LANES), BF16) + 6 * gpb * _nbytes((B, LANES), BF16) + 2 * _nbytes((GG, B, LANES), F32))
    est += 8 * _nbytes((GG * B, 3 * B), F32)
    return pl.pallas_call(
        body, name=name, grid=(Hkv // gpb, nb),
        in_specs=[pl.BlockSpec((GG, B, d), cur)] + kv_specs + kv_specs + [pl.BlockSpec((GG, 1, 1), lambda j, i: (j, 0, 0))],
        out_specs=(pl.BlockSpec((GG, B, d), cur), pl.BlockSpec((GG, B, LANES), cur), pl.BlockSpec((GG, B, d), cur)),
        out_shape=(jax.ShapeDtypeStruct((Hq, T, d), F32), jax.ShapeDtypeStruct((Hq, T, LANES), F32),
                   jax.ShapeDtypeStruct((Hq, T, d), BF16)),
        compiler_params=_params(("parallel", "parallel"), est),
    )(q, k, k, k, v, v, v, sink)


def _swa_dq(q, k, v, do, lse, delta, sink, scale, *, name):
    Hq, T, d = q.shape
    Hkv = k.shape[0]
    G = Hq // Hkv
    B = SWA_BLOCK
    nb = T // B
    gpb = _tile(Hkv, SWA_GROUPS, 1)
    _, kv_specs, cur = _swa_specs(G, d, nb, d, gpb)

    def body(q_ref, do_ref, lse_ref, dl_ref, k0, k1, k2, v0, v1, v2, sink_ref, dq_ref, dsink_ref):
        i = pl.program_id(1)
        kvs = [jnp.concatenate([k0[g], k1[g], k2[g]], axis=0) for g in range(gpb)]
        vvs = [jnp.concatenate([v0[g], v1[g], v2[g]], axis=0) for g in range(gpb)]
        bias = _swa_bias(i, T)[None]
        gp, parts = _swa_parts(G, gpb)
        lses = [lse_ref[hs, :, :1] for _, hs in parts]
        dls = [dl_ref[hs, :, :1] for _, hs in parts]
        ss = [lax.dot_general(q_ref[hs].reshape(gp * B, d), kvs[g], NT_DIMS, preferred_element_type=F32) for g, hs in parts]
        dps = [lax.dot_general(do_ref[hs].reshape(gp * B, d), vvs[g], NT_DIMS, preferred_element_type=F32)
               for g, hs in parts]
        ps = [jnp.exp((s * scale).reshape(gp, B, 3 * B) + bias - lse) for s, lse in zip(ss, lses)]
        dss = [(p * (dp.reshape(gp, B, 3 * B) - dl) * scale).reshape(gp * B, 3 * B).astype(BF16)
               for p, dp, dl in zip(ps, dps, dls)]
        dqs = [jnp.dot(ds, kvs[g], preferred_element_type=F32).reshape(gp, B, d) for ds, (g, _) in zip(dss, parts)]
        dsks = [-jnp.sum(jnp.exp(sink_ref[hs] - lse) * dl, axis=1, keepdims=True)
                for (_, hs), lse, dl in zip(parts, lses, dls)]

        @pl.when(i == 0)
        def _():
            dsink_ref[...] = jnp.zeros_like(dsink_ref)

        for (_, hs), dq, dsk in zip(parts, dqs, dsks):
            dq_ref[hs] = dq
            dsink_ref[hs] += jnp.broadcast_to(dsk, (gp, 1, LANES))

    GG = gpb * G
    est = 2 * (2 * _nbytes((GG, B, LANES), BF16) + 6 * gpb * _nbytes((B, LANES), BF16) + 3 * _nbytes((GG, B, LANES), F32))
    est += 8 * _nbytes((GG * B, 3 * B), F32)
    return pl.pallas_call(
        body, name=name, grid=(Hkv // gpb, nb),
        in_specs=[pl.BlockSpec((GG, B, d), cur), pl.BlockSpec((GG, B, d), cur), pl.BlockSpec((GG, B, LANES), cur),
                  pl.BlockSpec((GG, B, LANES), cur)] + kv_specs + kv_specs
                 + [pl.BlockSpec((GG, 1, 1), lambda j, i: (j, 0, 0))],
        out_specs=(pl.BlockSpec((GG, B, d), cur), pl.BlockSpec((GG, 1, LANES), lambda j, i: (j, 0, 0))),
        out_shape=(jax.ShapeDtypeStruct((Hq, T, d), F32), jax.ShapeDtypeStruct((Hq, 1, LANES), F32)),
        compiler_params=_params(("arbitrary", "arbitrary"), est),
    )(q, do, lse, delta, k, k, k, v, v, v, sink)


def _swa_dkv(q, k, v, do, lse, delta, scale, *, name):
    Hq, T, d = q.shape
    Hkv = k.shape[0]
    G = Hq // Hkv
    B = SWA_BLOCK
    nb = T // B
    gpb = _tile(Hkv, SWA_GROUPS, 1)
    q_specs, _, cur = _swa_specs(G, d, nb, d, gpb)
    l_specs, _, _ = _swa_specs(G, d, nb, LANES, gpb)

    def body(k_ref, v_ref, q0, q1, q2, d0, d1, d2, l0, l1, l2, e0, e1, e2, dk_ref, dv_ref):
        b = pl.program_id(1)
        row = lax.broadcasted_iota(jnp.int32, (B, B), 0)
        col = lax.broadcasted_iota(jnp.int32, (B, B), 1)
        biases = []
        for part in range(3):
            qpos = (b + part - 1) * B + row
            diff = (part - 1) * B + row - col
            valid = (diff >= -SWA_WINDOW) & (diff <= SWA_WINDOW) & (qpos >= 0) & (qpos < T)
            biases.append(jnp.where(valid, 0.0, -jnp.inf)[None])
        chains = [(g, part) for g in range(gpb) for part in range(3)]
        heads = [slice(g * G, (g + 1) * G) for g, _ in chains]
        kvs = [k_ref[g] for g, _ in chains]
        vvs = [v_ref[g] for g, _ in chains]
        qvs = [(q0, q1, q2)[part][hs].reshape(G * B, d) for (_, part), hs in zip(chains, heads)]
        dovs = [(d0, d1, d2)[part][hs].reshape(G * B, d) for (_, part), hs in zip(chains, heads)]
        lses = [(l0, l1, l2)[part][hs, :, :1] for (_, part), hs in zip(chains, heads)]
        dls = [(e0, e1, e2)[part][hs, :, :1] for (_, part), hs in zip(chains, heads)]
        ss = [lax.dot_general(qv, kv, NT_DIMS, preferred_element_type=F32) for qv, kv in zip(qvs, kvs)]
        dps = [lax.dot_general(dov, vv, NT_DIMS, preferred_element_type=F32) for dov, vv in zip(dovs, vvs)]
        ps = [jnp.exp((s * scale).reshape(G, B, B) + biases[part] - lse) for s, (_, part), lse in zip(ss, chains, lses)]
        dss = [(p * (dp.reshape(G, B, B) - dl) * scale).reshape(G * B, B).astype(BF16) for p, dp, dl in zip(ps, dps, dls)]
        pbs = [p.reshape(G * B, B).astype(BF16) for p in ps]
        dvs = [lax.dot_general(pb, dov, TN_DIMS, preferred_element_type=F32) for pb, dov in zip(pbs, dovs)]
        dks = [lax.dot_general(ds, qv, TN_DIMS, preferred_element_type=F32) for ds, qv in zip(dss, qvs)]
        for g in range(gpb):
            dk_ref[g] = dks[3 * g] + dks[3 * g + 1] + dks[3 * g + 2]
            dv_ref[g] = (dvs[3 * g] + dvs[3 * g + 1] + dvs[3 * g + 2]).astype(BF16)

    GG = gpb * G
    est = 2 * (6 * _nbytes((GG, B, LANES), BF16) + 6 * _nbytes((GG, B, LANES), F32) + 4 * gpb * _nbytes((B, LANES), F32))
    est += 10 * _nbytes((GG * B, B), F32)
    kspec = pl.BlockSpec((gpb, B, d), cur)
    return pl.pallas_call(
        body, name=name, grid=(Hkv // gpb, nb),
        in_specs=[kspec, kspec] + q_specs + q_specs + l_specs + l_specs,
        out_specs=(kspec, kspec),
        out_shape=(jax.ShapeDtypeStruct((Hkv, T, d), F32), jax.ShapeDtypeStruct((Hkv, T, d), BF16)),
        compiler_params=_params(("parallel", "parallel"), est),
    )(k, v, q, q, q, do, do, do, lse, lse, lse, delta, delta, delta)


def _loss_head(y, target, *, name):
    T, D = y.shape
    tm = _tile(T, 512)

    def body(y_ref, t_ref, dy_ref, s_ref):
        @pl.when(pl.program_id(0) == 0)
        def _():
            s_ref[...] = jnp.zeros_like(s_ref)

        e = y_ref[...] - t_ref[...]
        dy_ref[...] = e / D
        s_ref[...] += jnp.sum(jnp.sum(e * e, axis=-1, keepdims=True), axis=0, keepdims=True)

    spec = pl.BlockSpec((tm, D), lambda i: (i, 0))
    return pl.pallas_call(
        body, name=name, grid=(T // tm,), in_specs=[spec, spec],
        out_specs=(spec, pl.BlockSpec((1, 1), lambda i: (0, 0))),
        out_shape=(jax.ShapeDtypeStruct((T, D), F32), jax.ShapeDtypeStruct((1, 1), F32)),
        compiler_params=_params(("arbitrary",), 8 * _nbytes((tm, D), F32)),
    )(y, target)


def _adamw(w, g, m, v, *, name):
    R, C = w.shape
    tr = _tile(R, max(8, (1 << 19) // max(C, LANES) // 8 * 8), 8)

    def body(w_ref, g_ref, m_ref, v_ref, d_ref, nm_ref, nv_ref):
        gv = g_ref[...]
        nm = ADAM_B1 * m_ref[...] + (1.0 - ADAM_B1) * gv
        nv = ADAM_B2 * v_ref[...] + (1.0 - ADAM_B2) * jnp.square(gv)
        m_hat = nm / (1.0 - ADAM_B1 ** ADAM_STEP)
        v_hat = nv / (1.0 - ADAM_B2 ** ADAM_STEP)
        d_ref[...] = -ADAM_LR * (m_hat / (jnp.sqrt(v_hat) + ADAM_EPS) + ADAM_WD * w_ref[...])
        nm_ref[...] = nm
        nv_ref[...] = nv

    spec = pl.BlockSpec((tr, C), lambda i: (i, 0))
    sds = jax.ShapeDtypeStruct((R, C), F32)
    return pl.pallas_call(
        body, name=name, grid=(R // tr,), in_specs=[spec] * 4, out_specs=(spec,) * 3, out_shape=(sds,) * 3,
        compiler_params=_params(("parallel",), 16 * _nbytes((tr, max(C, LANES)), F32)),
    )(w, g, m, v)


def _to_bf16(w, *, name):
    R, C = w.shape
    tr = _comm_rows_tile(R, C)

    def body(w_ref, o_ref):
        o_ref[...] = w_ref[...].astype(BF16)

    spec = pl.BlockSpec((tr, C), lambda i: (i, 0))
    return pl.pallas_call(
        body, name=name, grid=(R // tr,), in_specs=[spec], out_specs=spec, out_shape=jax.ShapeDtypeStruct((R, C), BF16),
        compiler_params=_params(("parallel",), 6 * _nbytes((tr, max(C, LANES)), F32)),
    )(w)


def _comm_rows_tile(R, L):
    return _tile(R, max(SUBLANES_BF16, (1 << 19) // L // SUBLANES_BF16 * SUBLANES_BF16), SUBLANES_BF16)


def _pair_add(g, recv, c_idx, *, name):
    _, _, R, L = g.shape
    tr = _comm_rows_tile(R, L)

    def body(c_ref, g_ref, r_ref, o_ref):
        o_ref[...] = (g_ref[...] + r_ref[...]).astype(BF16)

    grid_spec = pltpu.PrefetchScalarGridSpec(
        num_scalar_prefetch=1, grid=(N_CHIPS, R // tr),
        in_specs=[pl.BlockSpec((None, None, tr, L), lambda j, i, c: (j, c[0], i, 0)),
                  pl.BlockSpec((None, tr, L), lambda j, i, c: (j, i, 0))],
        out_specs=pl.BlockSpec((None, tr, L), lambda j, i, c: (j, i, 0)))
    return pl.pallas_call(
        body, name=name, grid_spec=grid_spec, out_shape=jax.ShapeDtypeStruct((N_CHIPS, R, L), BF16),
        compiler_params=_params(("parallel", "parallel"), 8 * _nbytes((tr, L), F32)),
    )(c_idx, g, recv)


def _sum_chips(q, c_idx, *, name):
    _, R, L = q.shape
    tr = _comm_rows_tile(R, L)

    def body(c_ref, q_ref, o_ref):
        acc = q_ref[0].astype(F32)
        for j in range(1, N_CHIPS):
            acc = acc + q_ref[j].astype(F32)
        o_ref[...] = acc

    grid_spec = pltpu.PrefetchScalarGridSpec(
        num_scalar_prefetch=1, grid=(R // tr,),
        in_specs=[pl.BlockSpec((N_CHIPS, tr, L), lambda i, c: (0, i, 0))],
        out_specs=pl.BlockSpec((None, tr, L), lambda i, c: (c[0], i, 0)))
    return pl.pallas_call(
        body, name=name, grid_spec=grid_spec, out_shape=jax.ShapeDtypeStruct((2, R, L), F32),
        compiler_params=_params(("parallel",), 10 * _nbytes((tr, L), F32)),
    )(c_idx, q)


HBM_SPEC = pl.BlockSpec(memory_space=pltpu.HBM)


def _position():
    return lax.axis_index("x"), lax.axis_index("y"), lax.axis_index("c")


def _other_chips(x, y):
    return [(1 - x, y), (x, 1 - y), (1 - x, 1 - y)]


AG_COPIES = 7


def _gather_plan(w_refs, out_refs, send_sems, recv_sems, local_sems):
    n = len(w_refs)
    x, y, c = _position()
    me, sibling = (x, y, c), (x, y, 1 - c)
    chips = _other_chips(x, y)

    def copy(i, k, block, to, src=None):
        px, py, pc = block
        slot = out_refs[i].at[4 * px + 2 * py + pc]
        return pltpu.make_async_remote_copy(
            src_ref=slot if src is None else src, dst_ref=slot, send_sem=send_sems.at[AG_COPIES * i + k],
            recv_sem=recv_sems.at[AG_COPIES * i + k], device_id=to, device_id_type=MESH)

    def local(i):
        return pltpu.make_async_copy(w_refs[i].at[c], out_refs[i].at[4 * x + 2 * y + c], local_sems.at[i])

    def first(i):
        own = w_refs[i].at[c]
        return [copy(i, 0, me, sibling, src=own)] + [copy(i, 1 + j, me, (*chip, c), src=own)
                                                     for j, chip in enumerate(chips)]

    def passed(i):
        return [copy(i, 4 + j, (*chip, c), sibling) for j, chip in enumerate(chips)]

    def start():
        for i in range(n):
            local(i).start()
            for cp in first(i):
                cp.start()

    def forward():
        for i in range(n):
            for j, chip in enumerate(chips):
                copy(i, 1 + j, (*chip, c), me).wait_recv()
                passed(i)[j].start()

    def finish():
        for i in range(n):
            copy(i, 0, sibling, me).wait_recv()
            for j, chip in enumerate(chips):
                copy(i, 4 + j, (*chip, 1 - c), me).wait_recv()
        for i in range(n):
            for cp in first(i) + passed(i):
                cp.wait_send()
            local(i).wait()

    return start, forward, finish


def _gather_scratch(n):
    return [pltpu.SemaphoreType.DMA((AG_COPIES * n,)), pltpu.SemaphoreType.DMA((AG_COPIES * n,)),
            pltpu.SemaphoreType.DMA((n,))]


def _gathered_shapes(ws):
    return [jax.ShapeDtypeStruct((2 * N_CHIPS,) + w.shape[1:], w.dtype) for w in ws]


def _all_gather_halves(ws, *, name):
    n = len(ws)

    def body(*refs):
        for step in _gather_plan(refs[:n], refs[n:2 * n], *refs[2 * n:]):
            step()

    return pl.pallas_call(
        body, name=name, in_specs=[HBM_SPEC] * n, out_specs=[HBM_SPEC] * n, out_shape=_gathered_shapes(ws),
        scratch_shapes=_gather_scratch(n),
    )(*ws)


def _exchange_plan(g_refs, out_refs, send_sems, recv_sems):
    n = len(g_refs)
    x, y, c = _position()

    def copies():
        return [pltpu.make_async_remote_copy(
            src_ref=g_refs[i].at[j, 1 - c], dst_ref=out_refs[i].at[j], send_sem=send_sems.at[N_CHIPS * i + j],
            recv_sem=recv_sems.at[N_CHIPS * i + j], device_id=(x, y, 1 - c), device_id_type=MESH)
            for i in range(n) for j in range(N_CHIPS)]

    def start():
        for cp in copies():
            cp.start()

    def finish():
        for cp in copies():
            cp.wait()

    return start, finish


def _exchange_scratch(n):
    return [pltpu.SemaphoreType.DMA((N_CHIPS * n,)), pltpu.SemaphoreType.DMA((N_CHIPS * n,))]


def _sibling_exchange(gs, *, name):
    n = len(gs)

    def body(*refs):
        for step in _exchange_plan(refs[:n], refs[n:2 * n], *refs[2 * n:]):
            step()

    return pl.pallas_call(
        body, name=name, in_specs=[HBM_SPEC] * n, out_specs=[HBM_SPEC] * n,
        out_shape=[jax.ShapeDtypeStruct((N_CHIPS,) + g.shape[2:], g.dtype) for g in gs],
        scratch_shapes=_exchange_scratch(n),
    )(*gs)


def _scatter_plan(p_refs, q_refs, send_sems, recv_sems, local_sems):
    n = len(p_refs)
    others = N_CHIPS - 1
    x, y, c = _position()
    me = 2 * x + y
    chips = _other_chips(x, y)

    def copy(i, k, chip, src_slot, dst_slot):
        return pltpu.make_async_remote_copy(
            src_ref=p_refs[i].at[src_slot], dst_ref=q_refs[i].at[dst_slot], send_sem=send_sems.at[others * i + k],
            recv_sem=recv_sems.at[others * i + k], device_id=(*chip, c), device_id_type=MESH)

    def local(i):
        return pltpu.make_async_copy(p_refs[i].at[me], q_refs[i].at[me], local_sems.at[i])

    def sends(i):
        return [copy(i, k, chip, 2 * chip[0] + chip[1], me) for k, chip in enumerate(chips)]

    def start():
        for i in range(n):
            local(i).start()
            for cp in sends(i):
                cp.start()

    def finish():
        for i in range(n):
            for k, chip in enumerate(chips):
                copy(i, k, chip, me, 2 * chip[0] + chip[1]).wait_recv()
        for i in range(n):
            for cp in sends(i):
                cp.wait_send()
            local(i).wait()

    return start, finish


def _scatter_scratch(n):
    others = N_CHIPS - 1
    return [pltpu.SemaphoreType.DMA((others * n,)), pltpu.SemaphoreType.DMA((others * n,)),
            pltpu.SemaphoreType.DMA((n,))]


def _chip_scatter(ps, *, name):
    n = len(ps)

    def body(*refs):
        for step in _scatter_plan(refs[:n], refs[n:2 * n], *refs[2 * n:]):
            step()

    return pl.pallas_call(
        body, name=name, in_specs=[HBM_SPEC] * n, out_specs=[HBM_SPEC] * n,
        out_shape=[jax.ShapeDtypeStruct(p.shape, p.dtype) for p in ps], scratch_shapes=_scatter_scratch(n),
    )(*ps)


def _sibling_share(fs, *, name):
    n = len(fs)

    def body(*refs):
        in_refs, out_refs = refs[:n], refs[n:2 * n]
        send_sems, recv_sems = refs[2 * n:]
        x, y, c = _position()

        def copy(i, half):
            return pltpu.make_async_remote_copy(
                src_ref=in_refs[i].at[half], dst_ref=out_refs[i].at[half], send_sem=send_sems.at[i],
                recv_sem=recv_sems.at[i], device_id=(x, y, 1 - c), device_id_type=MESH)

        sends = [copy(i, c) for i in range(n)]
        for cp in sends:
            cp.start()
        for i in range(n):
            copy(i, 1 - c).wait_recv()
        for cp in sends:
            cp.wait_send()

    return pl.pallas_call(
        body, name=name, in_specs=[HBM_SPEC] * n, out_specs=[HBM_SPEC] * n,
        out_shape=[jax.ShapeDtypeStruct(f.shape, f.dtype) for f in fs],
        input_output_aliases={i: i for i in range(n)},
        scratch_shapes=[pltpu.SemaphoreType.DMA((n,)), pltpu.SemaphoreType.DMA((n,))],
    )(*fs)


def _all_reduce_small(s, *, name):
    R, L = s.shape
    n_dev = 2 * N_CHIPS

    def body(s_ref, out_ref, buf, send_sems, recv_sems, local_sem):
        x, y, c = _position()
        me, sibling = (x, y, c), (x, y, 1 - c)
        chips = _other_chips(x, y)

        def slot(px, py, pc):
            return buf.at[4 * px + 2 * py + pc]

        def copy(k, block, to, src=None):
            return pltpu.make_async_remote_copy(
                src_ref=slot(*block) if src is None else src, dst_ref=slot(*block),
                send_sem=send_sems.at[k], recv_sem=recv_sems.at[k], device_id=to, device_id_type=MESH)

        mine = pltpu.make_async_copy(s_ref, slot(*me), local_sem)
        mine.start()
        first = [copy(0, me, sibling, src=s_ref)]
        first += [copy(1 + j, me, (*chip, c), src=s_ref) for j, chip in enumerate(chips)]
        for cp in first:
            cp.start()
        passed = [copy(4 + j, (*chip, c), sibling) for j, chip in enumerate(chips)]
        for j, chip in enumerate(chips):
            copy(1 + j, (*chip, c), me).wait_recv()
            passed[j].start()
        copy(0, sibling, me).wait_recv()
        for j, chip in enumerate(chips):
            copy(4 + j, (*chip, 1 - c), me).wait_recv()
        for cp in first + passed:
            cp.wait_send()
        mine.wait()
        acc = buf[0]
        for j in range(1, n_dev):
            acc = acc + buf[j]
        out_ref[...] = acc

    vmem = pl.BlockSpec(memory_space=pltpu.VMEM)
    return pl.pallas_call(
        body, name=name, in_specs=[vmem], out_specs=vmem, out_shape=jax.ShapeDtypeStruct((R, L), F32),
        scratch_shapes=[pltpu.VMEM((n_dev, R, L), F32), pltpu.SemaphoreType.DMA((7,)), pltpu.SemaphoreType.DMA((7,)),
                        pltpu.SemaphoreType.DMA],
    )(s)


def _rope_cos_sin(pos, dim, theta):
    inv = jnp.float32(theta) ** (-jnp.arange(0, dim, 2, dtype=F32) / dim)
    ang = pos.astype(F32)[:, None] * inv[None, :]
    return jnp.cos(ang), jnp.sin(ang)


def _rope_tables(T, d, segments):
    P = np.zeros((d, d), np.float32)
    c_parts, s_parts, at = [], [], 0
    for start, size, cos, sin in segments:
        half = size // 2
        if start > at:
            c_parts.append(jnp.ones((T, start - at), F32))
            s_parts.append(jnp.zeros((T, start - at), F32))
        c_parts += [cos, cos]
        s_parts += [-sin, sin]
        at = start + size
        for p in range(half):
            P[start + half + p, start + p] = 1.0
            P[start + p, start + half + p] = 1.0
    if at < d:
        c_parts.append(jnp.ones((T, d - at), F32))
        s_parts.append(jnp.zeros((T, d - at), F32))
    return jnp.concatenate(c_parts, axis=1), jnp.concatenate(s_parts, axis=1), jnp.asarray(P, BF16)


def _heads(t, H, d):
    return t.reshape(t.shape[0], H, d).transpose(1, 0, 2)


def _unheads(t):
    H, T, d = t.shape
    return t.transpose(1, 0, 2).reshape(T, H * d)


def _dw(a, b, *, name, axis):
    K, N = a.shape[1], b.shape[1]
    if axis == 1:
        return _mm(a, b, mode="tn", name=name).reshape(N_CHIPS, K // N_CHIPS, N)
    if (N // N_CHIPS) % LANES == 0:
        return _mm(a, b, mode="tn", name=name, split=N_CHIPS)
    return _mm(a, b, mode="tn", name=name).reshape(K, N_CHIPS, N // N_CHIPS).transpose(1, 0, 2)


def _mlp_fwd(x, gain, w_up, w_down, tag):
    hm = _norm_fwd(x[None], gain, name=f"mlp{tag}_norm")[0]
    u, act = _mm(hm, w_up, mode="nn", name=f"mlp{tag}_up", epi="sqrelu")
    x_out = _mm(act, w_down, mode="nn", name=f"mlp{tag}_down", epi="add", extra=x)
    return x_out, (hm, u, act)


def _mlp_bwd(x, gain, w_up, w_down, saved, dxo, tag):
    hm, u, act = saved
    du = _mm(dxo, w_down, mode="nt", name=f"mlp{tag}_dact", epi="dsqrelu", extra=u, out_dtype=BF16)
    dw_down = _dw(act, dxo, name=f"mlp{tag}_dwdown", axis=1)
    dhm = _mm(du, w_up, mode="nt", name=f"mlp{tag}_dhm")
    dw_up = _dw(hm, du, name=f"mlp{tag}_dwup", axis=2)
    dx, dgain = _norm_bwd(x[None], gain, dhm[None], name=f"mlp{tag}_dnorm", res=dxo)
    return dx[0], dgain[0], dw_up, dw_down


def _local_step(x, target, W, small, late=None):
    T, D = x.shape
    W = dict(W)
    pos = jnp.arange(T)
    mla_cos, mla_sin = _rope_cos_sin(pos, MLA_ROPE, ROPE_THETA)
    row_cos, row_sin = _rope_cos_sin(pos // GRID_W, GQA_DIM // 2, AXIAL_THETA)
    col_cos, col_sin = _rope_cos_sin(pos % GRID_W, GQA_DIM // 2, AXIAL_THETA)
    swa_cos, swa_sin = _rope_cos_sin(pos, SWA_ROT, ROPE_THETA)
    rope_q = _rope_tables(T, MLA_QK, [(MLA_NOPE, MLA_ROPE, mla_cos, mla_sin)])
    rope_kr = _rope_tables(T, MLA_ROPE, [(0, MLA_ROPE, mla_cos, mla_sin)])
    half = GQA_DIM // 2
    rope_ax = _rope_tables(T, GQA_DIM, [(0, half, row_cos, row_sin), (half, half, col_cos, col_sin)])
    rope_sw = _rope_tables(T, SWA_DIM, [(0, SWA_ROT, swa_cos, swa_sin)])
    o1 = MLA_Q_LORA
    o2 = o1 + MLA_KV_LORA
    o3 = o2 + MLA_ROPE
    o4 = o3 + GQA_HEADS * GQA_DIM
    o5 = o4 + GQA_KV * GQA_DIM
    sc_a, sc_g, sc_s = MLA_QK ** -0.5, GQA_DIM ** -0.5, SWA_DIM ** -0.5
    kv_w = MLA_NOPE + MLA_V

    h0 = _norm_fwd(x[None], small["even_norm"], name="even_norm")[0]
    proj = _mm(h0, W["even_w_in"], mode="nn", name="even_in")
    c_q, c_kv, kr_raw = proj[:, :o1], proj[:, o1:o2], proj[:, o2:o3]
    qg_raw = _heads(proj[:, o3:o4], GQA_HEADS, GQA_DIM)
    kg_raw = _heads(proj[:, o4:o5], GQA_KV, GQA_DIM)
    vg = _heads(proj[:, o5:], GQA_KV, GQA_DIM).astype(BF16)
    cqn = _norm_fwd(c_q[None], small["mla_q_lat_norm"], name="q_lat_norm")[0]
    ckvn = _norm_fwd(c_kv[None], small["mla_kv_lat_norm"], name="kv_lat_norm")[0]
    qa_raw = _heads(_mm(cqn, W["mla_w_uq"], mode="nn", name="mla_uq"), MLA_HEADS, MLA_QK)
    kv = _mm(ckvn, W["mla_w_ukv"], mode="nn", name="mla_ukv").reshape(T, MLA_HEADS, kv_w)
    kn_raw = kv[:, :, :MLA_NOPE].transpose(1, 0, 2)
    va = kv[:, :, MLA_NOPE:].transpose(1, 0, 2).astype(BF16)
    q_a = _norm_fwd(qa_raw, small["mla_q_norm"], name="mla_q_prep", rope=rope_q, out_scale=sc_a * LOG2E)
    k_n = _norm_fwd(kn_raw, small["mla_k_nope_norm"], name="mla_kn_prep")
    k_r = _norm_fwd(kr_raw[None], small["mla_k_rope_norm"], name="mla_kr_prep", rope=rope_kr)
    k_a = jnp.concatenate([k_n, jnp.broadcast_to(k_r, (MLA_HEADS, T, MLA_ROPE))], axis=-1)
    if late is None:
        o_a, lse_a, ob_a = _flash_fwd(q_a, k_a, va, name="mla_attn")
    else:
        o_a, lse_a, ob_a, gathered = _flash_fwd(q_a, k_a, va, name="mla_attn", gather=late.halves)
        W.update(late.weights(gathered))
    q_g = _norm_fwd(qg_raw, small["gqa_q_norm"], name="gqa_q_prep", rope=rope_ax, out_scale=sc_g * LOG2E)
    k_g = _norm_fwd(kg_raw, small["gqa_k_norm"], name="gqa_k_prep", rope=rope_ax)
    o_g, lse_g, ob_g = _flash_fwd(q_g, k_g, vg, name="gqa_attn")
    merged = jnp.concatenate([_unheads(ob_a), _unheads(ob_g)], axis=-1)
    x1 = _mm(merged, W["even_w_out"], mode="nn", name="even_out", epi="add", extra=x)
    x2, mlp0 = _mlp_fwd(x1, small["mlp_norm"][0], W["mlp_w_up0"], W["mlp_w_down0"], 0)

    h1 = _norm_fwd(x2[None], small["odd_norm"], name="odd_norm")[0]
    qkv = _mm(h1, W["odd_w_qkv"], mode="nn", name="odd_qkv")
    nq, nkk = SWA_HEADS * SWA_DIM, SWA_KV * SWA_DIM
    qs_raw = _heads(qkv[:, :nq], SWA_HEADS, SWA_DIM)
    ks_raw = _heads(qkv[:, nq:nq + nkk], SWA_KV, SWA_DIM)
    vs = _heads(qkv[:, nq + nkk:], SWA_KV, SWA_DIM).astype(BF16)
    q_s = _norm_fwd(qs_raw, small["swa_q_norm"], name="swa_q_prep", rope=rope_sw)
    k_s = _norm_fwd(ks_raw, small["swa_k_norm"], name="swa_k_prep", rope=rope_sw)
    sink = small["swa_sink"].reshape(SWA_HEADS, 1, 1)
    o_s, lse_s, ob_s = _swa_fwd(q_s, k_s, vs, sink, sc_s, name="swa_attn")
    o_flat = _unheads(ob_s)
    x3 = _mm(o_flat, W["odd_w_out"], mode="nn", name="odd_out", epi="add", extra=x2)
    x4, mlp1 = _mlp_fwd(x3, small["mlp_norm"][1], W["mlp_w_up1"], W["mlp_w_down1"], 1)

    dy, loss_sum = _loss_head(x4, target, name="loss_head")
    gW, gs = {}, {}

    dx3, dg_m1, gW["mlp_w_up1"], gW["mlp_w_down1"] = _mlp_bwd(
        x3, small["mlp_norm"][1], W["mlp_w_up1"], W["mlp_w_down1"], mlp1, dy, 1)
    d_oflat = _mm(dx3, W["odd_w_out"], mode="nt", name="odd_dout")
    gW["odd_w_out"] = _dw(o_flat, dx3, name="odd_dwout", axis=1)
    do_s = _heads(d_oflat, SWA_HEADS, SWA_DIM)
    delta_s, dob_s = _delta(o_s, do_s, name="swa_delta")
    dq_s, dsink = _swa_dq(q_s, k_s, vs, dob_s, lse_s, delta_s, sink, sc_s, name="swa_dq")
    dk_s, dv_s = _swa_dkv(q_s, k_s, vs, dob_s, lse_s, delta_s, sc_s, name="swa_dkv")
    gs["swa_sink"] = dsink[:, 0, 0]
    dqs_raw, gs["swa_q_norm"] = _norm_bwd(qs_raw, small["swa_q_norm"], dq_s, name="swa_dq_prep", rope=rope_sw,
                                          out_dtype=BF16)
    dks_raw, gs["swa_k_norm"] = _norm_bwd(ks_raw, small["swa_k_norm"], dk_s, name="swa_dk_prep", rope=rope_sw,
                                          out_dtype=BF16)
    dqkv = jnp.concatenate([_unheads(dqs_raw), _unheads(dks_raw), _unheads(dv_s)], axis=-1).astype(BF16)
    dh1 = _mm(dqkv, W["odd_w_qkv"], mode="nt", name="odd_dh")
    gW["odd_w_qkv"] = _dw(h1, dqkv, name="odd_dwqkv", axis=2)
    dx2, gs["odd_norm"] = _norm_bwd(x2[None], small["odd_norm"], dh1[None], name="odd_dnorm", res=dx3)
    dx2 = dx2[0]

    dx1, dg_m0, gW["mlp_w_up0"], gW["mlp_w_down0"] = _mlp_bwd(
        x1, small["mlp_norm"][0], W["mlp_w_up0"], W["mlp_w_down0"], mlp0, dx2, 0)
    gs["mlp_norm"] = jnp.stack([dg_m0, dg_m1])
    d_merged = _mm(dx1, W["even_w_out"], mode="nt", name="even_dout")
    gW["even_w_out"] = _dw(merged, dx1, name="even_dwout", axis=1)
    na = MLA_HEADS * MLA_V
    do_a = _heads(d_merged[:, :na], MLA_HEADS, MLA_V)
    do_g = _heads(d_merged[:, na:], GQA_HEADS, GQA_DIM)
    delta_a, dob_a = _delta(o_a, do_a, name="mla_delta")
    delta_g, dob_g = _delta(o_g, do_g, name="gqa_delta")
    if late is None:
        dq_a, dk_a, dv_a = _flash_bwd(q_a, k_a, va, dob_a, lse_a, delta_a, name="mla_attn_bwd")
        dq_g, dk_gp, dv_gp = _flash_bwd(q_g, k_g, vg, dob_g, lse_g, delta_g, name="gqa_attn_bwd")
    else:
        halves = late.split(gW)
        dq_a, dk_a, dv_a, from_sibling = _flash_bwd(q_a, k_a, va, dob_a, lse_a, delta_a, name="mla_attn_bwd",
                                                    rider=("exchange", halves))
        dq_g, dk_gp, dv_gp, late.scattered = _flash_bwd(q_g, k_g, vg, dob_g, lse_g, delta_g, name="gqa_attn_bwd",
                                                        rider=("scatter", late.pairs(halves, from_sibling)))
    grp = GQA_HEADS // GQA_KV
    ln2 = 1.0 / LOG2E
    dqg_raw, gs["gqa_q_norm"] = _norm_bwd(qg_raw, small["gqa_q_norm"], dq_g, name="gqa_dq_prep", rope=rope_ax,
                                          dy_scale=sc_g, out_dtype=BF16)
    dkg_raw, gs["gqa_k_norm"] = _norm_bwd(kg_raw, small["gqa_k_norm"], dk_gp, name="gqa_dk_prep", rope=rope_ax,
                                          group=grp, dy_scale=ln2, out_dtype=BF16)
    dvg = _group_sum(dv_gp, grp, name="gqa_dv_sum", out_dtype=BF16)
    dqa_raw, gs["mla_q_norm"] = _norm_bwd(qa_raw, small["mla_q_norm"], dq_a, name="mla_dq_prep", rope=rope_q,
                                          dy_scale=sc_a, out_dtype=BF16)
    dkn_raw, gs["mla_k_nope_norm"] = _norm_bwd(kn_raw, small["mla_k_nope_norm"], dk_a[:, :, :MLA_NOPE],
                                               name="mla_dkn_prep", dy_scale=ln2, out_dtype=BF16)
    dkr_raw, gs["mla_k_rope_norm"] = _norm_bwd(kr_raw[None], small["mla_k_rope_norm"], dk_a[:, :, MLA_NOPE:],
                                               name="mla_dkr_prep", rope=rope_kr, group=MLA_HEADS, dy_scale=ln2,
                                               out_dtype=BF16)
    dkv = jnp.concatenate([dkn_raw.transpose(1, 0, 2), dv_a.astype(BF16).transpose(1, 0, 2)], axis=-1)
    dkv = dkv.reshape(T, MLA_HEADS * kv_w).astype(BF16)
    dqa = _unheads(dqa_raw).astype(BF16)
    dckvn = _mm(dkv, W["mla_w_ukv"], mode="nt", name="mla_dckv")
    gW["mla_w_ukv"] = _dw(ckvn, dkv, name="mla_dwukv", axis=2)
    dcqn = _mm(dqa, W["mla_w_uq"], mode="nt", name="mla_dcq")
    gW["mla_w_uq"] = _dw(cqn, dqa, name="mla_dwuq", axis=2)
    dc_q, gs["mla_q_lat_norm"] = _norm_bwd(c_q[None], small["mla_q_lat_norm"], dcqn[None], name="q_lat_dnorm",
                                           out_dtype=BF16)
    dc_kv, gs["mla_kv_lat_norm"] = _norm_bwd(c_kv[None], small["mla_kv_lat_norm"], dckvn[None], name="kv_lat_dnorm",
                                             out_dtype=BF16)
    dproj = jnp.concatenate([dc_q[0], dc_kv[0], dkr_raw[0], _unheads(dqg_raw), _unheads(dkg_raw), _unheads(dvg)],
                            axis=-1).astype(BF16)
    dh0 = _mm(dproj, W["even_w_in"], mode="nt", name="even_dh")
    gW["even_w_in"] = _dw(h0, dproj, name="even_dwin", axis=2)
    dx0, gs["even_norm"] = _norm_bwd(x[None], small["even_norm"], dh0[None], name="even_dnorm", res=dx1)
    gs = {k: v.reshape(-1) for k, v in gs.items()}
    return loss_sum, dx0[0], gW, gs


BIG = (("even_w_in", 0, 2), ("mla_w_uq", 0, 2), ("mla_w_ukv", 0, 2), ("even_w_out", 0, 1), ("odd_w_qkv", 0, 2),
       ("odd_w_out", 0, 1), ("mlp_w_up", 0, 2), ("mlp_w_up", 1, 2), ("mlp_w_down", 0, 1), ("mlp_w_down", 1, 1))
GATHER_FIRST = ("even_w_in", "mla_w_uq", "mla_w_ukv")
GRADS_LAST = ("even_w_in", "mla_w_uq", "mla_w_ukv")
SMALL = ("even_norm", "mla_q_lat_norm", "mla_kv_lat_norm", "mla_q_norm", "mla_k_nope_norm", "mla_k_rope_norm",
         "gqa_q_norm", "gqa_k_norm", "odd_norm", "swa_q_norm", "swa_k_norm", "swa_sink", "mlp_norm")
def _pad_to(v, n):
    return v if v.shape[-1] == n else jnp.pad(v, [(0, 0)] * (v.ndim - 1) + [(0, n - v.shape[-1])])


def _big_key(name, layer, w):
    return name if w[name].shape[0] == 1 else f"{name}{layer}"


def _pack_rows(flat, rows=8):
    n = flat.shape[0]
    padded = -(-n // (rows * LANES)) * rows * LANES
    return _pad_to(flat, padded).reshape(-1, LANES)


def kernel(x, even_norm, even_w_in, mla_q_lat_norm, mla_kv_lat_norm, mla_w_uq, mla_w_ukv, mla_q_norm, mla_k_nope_norm, mla_k_rope_norm, gqa_q_norm, gqa_k_norm, even_w_out, odd_norm, odd_w_qkv, swa_q_norm, swa_k_norm, swa_sink, odd_w_out, mlp_norm, mlp_w_up, mlp_w_down, loss_target, m_even_norm, m_even_w_in, m_mla_q_lat_norm, m_mla_kv_lat_norm, m_mla_w_uq, m_mla_w_ukv, m_mla_q_norm, m_mla_k_nope_norm, m_mla_k_rope_norm, m_gqa_q_norm, m_gqa_k_norm, m_even_w_out, m_odd_norm, m_odd_w_qkv, m_swa_q_norm, m_swa_k_norm, m_swa_sink, m_odd_w_out, m_mlp_norm, m_mlp_w_up, m_mlp_w_down, v_even_norm, v_even_w_in, v_mla_q_lat_norm, v_mla_kv_lat_norm, v_mla_w_uq, v_mla_w_ukv, v_mla_q_norm, v_mla_k_nope_norm, v_mla_k_rope_norm, v_gqa_q_norm, v_gqa_k_norm, v_even_w_out, v_odd_norm, v_odd_w_qkv, v_swa_q_norm, v_swa_k_norm, v_swa_sink, v_odd_w_out, v_mlp_norm, v_mlp_w_up, v_mlp_w_down):
    w = dict(even_norm=even_norm, even_w_in=even_w_in, mla_q_lat_norm=mla_q_lat_norm, mla_kv_lat_norm=mla_kv_lat_norm,
             mla_w_uq=mla_w_uq, mla_w_ukv=mla_w_ukv, mla_q_norm=mla_q_norm, mla_k_nope_norm=mla_k_nope_norm,
             mla_k_rope_norm=mla_k_rope_norm, gqa_q_norm=gqa_q_norm, gqa_k_norm=gqa_k_norm, even_w_out=even_w_out,
             odd_norm=odd_norm, odd_w_qkv=odd_w_qkv, swa_q_norm=swa_q_norm, swa_k_norm=swa_k_norm, swa_sink=swa_sink,
             odd_w_out=odd_w_out, mlp_norm=mlp_norm, mlp_w_up=mlp_w_up, mlp_w_down=mlp_w_down)
    m = dict(even_norm=m_even_norm, even_w_in=m_even_w_in, mla_q_lat_norm=m_mla_q_lat_norm,
             mla_kv_lat_norm=m_mla_kv_lat_norm, mla_w_uq=m_mla_w_uq, mla_w_ukv=m_mla_w_ukv, mla_q_norm=m_mla_q_norm,
             mla_k_nope_norm=m_mla_k_nope_norm, mla_k_rope_norm=m_mla_k_rope_norm, gqa_q_norm=m_gqa_q_norm,
             gqa_k_norm=m_gqa_k_norm, even_w_out=m_even_w_out, odd_norm=m_odd_norm, odd_w_qkv=m_odd_w_qkv,
             swa_q_norm=m_swa_q_norm, swa_k_norm=m_swa_k_norm, swa_sink=m_swa_sink, odd_w_out=m_odd_w_out,
             mlp_norm=m_mlp_norm, mlp_w_up=m_mlp_w_up, mlp_w_down=m_mlp_w_down)
    v = dict(even_norm=v_even_norm, even_w_in=v_even_w_in, mla_q_lat_norm=v_mla_q_lat_norm,
             mla_kv_lat_norm=v_mla_kv_lat_norm, mla_w_uq=v_mla_w_uq, mla_w_ukv=v_mla_w_ukv, mla_q_norm=v_mla_q_norm,
             mla_k_nope_norm=v_mla_k_nope_norm, mla_k_rope_norm=v_mla_k_rope_norm, gqa_q_norm=v_gqa_q_norm,
             gqa_k_norm=v_gqa_k_norm, even_w_out=v_even_w_out, odd_norm=v_odd_norm, odd_w_qkv=v_odd_w_qkv,
             swa_q_norm=v_swa_q_norm, swa_k_norm=v_swa_k_norm, swa_sink=v_swa_sink, odd_w_out=v_odd_w_out,
             mlp_norm=v_mlp_norm, mlp_w_up=v_mlp_w_up, mlp_w_down=v_mlp_w_down)
    xi, yi, ci = _position()
    chip = 2 * xi + yi
    T, D = x.shape[1], x.shape[2]

    c_idx = ci.reshape(1).astype(jnp.int32)
    key_of = lambda entry: _big_key(entry[0], entry[1], w)
    first_use = [e for e in BIG if e[0] in GATHER_FIRST]
    later_use = [e for e in BIG if e[0] not in GATHER_FIRST]
    early_grads = [e for e in BIG if e[0] not in GRADS_LAST]
    last_grads = [e for e in BIG if e[0] in GRADS_LAST]

    as_bf16 = {}

    def halves_of(entries):
        out = []
        for name, layer, _ in entries:
            layers, ks, ns = w[name].shape
            if name not in as_bf16:
                as_bf16[name] = _to_bf16(w[name].reshape(layers * ks, ns), name=f"to_bf16_{name}")
            out.append(as_bf16[name].reshape(layers, 2, ks // 2, ns)[layer])
        return out

    def weights_of(entries, gathered):
        out = {}
        for (name, layer, axis), g in zip(entries, gathered):
            ks, ns = w[name].shape[1:]
            stacked = g.reshape(N_CHIPS, ks, ns)
            if axis == 1:
                out[_big_key(name, layer, w)] = stacked.reshape(N_CHIPS * ks, ns)
            else:
                out[_big_key(name, layer, w)] = stacked.transpose(1, 0, 2).reshape(ks, N_CHIPS * ns)
        return out

    def split_halves(entries, gW):
        out = []
        for entry in entries:
            _, ks, ns = gW[key_of(entry)].shape
            out.append(gW[key_of(entry)].reshape(N_CHIPS, 2, ks // 2, ns))
        return out

    def pair_sums(entries, g_all, from_sibling):
        return [_pair_add(g, r, c_idx, name=f"grad_pair_add_{key_of(e)}") for e, g, r in zip(entries, g_all, from_sibling)]

    class _Late:
        halves = halves_of(later_use)
        scattered = None

        @staticmethod
        def weights(gathered):
            return weights_of(later_use, gathered)

        @staticmethod
        def split(gW):
            return split_halves(early_grads, gW)

        @staticmethod
        def pairs(g_all, from_sibling):
            return pair_sums(early_grads, g_all, from_sibling)

    late = _Late()
    W = weights_of(first_use, _all_gather_halves(halves_of(first_use), name="weights_all_gather"))

    odd_full = jnp.zeros((N_CHIPS, D // N_CHIPS), F32).at[chip].set(jnp.where(ci == 0, 1.0, 0.0) * w["odd_norm"][0])
    odd_full = _all_reduce_small(_pack_rows(odd_full.reshape(-1)), name="odd_norm_gather").reshape(-1)[:D]
    small = {name: w[name][0] for name in SMALL if name not in ("mlp_norm", "odd_norm")}
    small["mlp_norm"] = w["mlp_norm"]
    small["odd_norm"] = odd_full

    loss_sum, grad_x, gW, gs = _local_step(x[0], loss_target[0], W, small, late)

    loss_local = 0.5 * loss_sum.reshape(1) / D
    small_sizes = [(name, int(gs[name].shape[0])) for name in SMALL]
    ar_in = jnp.concatenate([_pad_to(loss_local, LANES)] + [gs[name] for name in SMALL])
    ar_out = _all_reduce_small(_pack_rows(ar_in), name="small_all_reduce").reshape(-1)
    loss = ar_out[0]
    g_small, off = {}, LANES
    for name, n in small_sizes:
        g_small[name] = ar_out[off:off + n]
        off += n
    shard_d = D // N_CHIPS
    g_small["odd_norm"] = lax.dynamic_slice(g_small["odd_norm"], (chip * shard_d,), (shard_d,))

    from_chips = dict(zip(map(key_of, early_grads), late.scattered))
    last_halves = split_halves(last_grads, gW)
    last_pairs = pair_sums(last_grads, last_halves, _sibling_exchange(last_halves, name="grad_sibling_exchange"))
    last_scattered = _chip_scatter(last_pairs, name="grad_chip_scatter")
    from_chips.update(zip(map(key_of, last_grads), last_scattered))
    keys = [key_of(e) for e in BIG]
    reduced = [_sum_chips(from_chips[key], c_idx, name=f"grad_chip_sum_{key}") for key in keys]
    shared = _sibling_share(reduced, name="grad_sibling_share")
    g_shards = {}
    for (name, layer, _), f in zip(BIG, shared):
        g_shards.setdefault(name, []).append(f.reshape(w[name].shape[1:]))

    grads, deltas, new_m, new_v = {}, {}, {}, {}
    for name in g_shards:
        shape = w[name].shape
        g = jnp.stack(g_shards[name])
        grads[name] = g
        two_d = (shape[0] * shape[1], shape[2])
        d_, m_, v_ = _adamw(w[name].reshape(two_d), g.reshape(two_d), m[name].reshape(two_d), v[name].reshape(two_d),
                            name=f"adamw_{name}")
        deltas[name], new_m[name], new_v[name] = d_.reshape(shape), m_.reshape(shape), v_.reshape(shape)
    pack_small = lambda d: _pack_rows(jnp.concatenate([d[name].reshape(-1) for name in SMALL]))
    for name in SMALL:
        grads[name] = g_small[name].reshape(w[name].shape)
    d_, m_, v_ = _adamw(pack_small(w), pack_small(grads), pack_small(m), pack_small(v), name="adamw_small")
    d_, m_, v_ = d_.reshape(-1), m_.reshape(-1), v_.reshape(-1)
    off = 0
    for name in SMALL:
        n = int(np.prod(w[name].shape))
        deltas[name] = d_[off:off + n].reshape(w[name].shape)
        new_m[name] = m_[off:off + n].reshape(w[name].shape)
        new_v[name] = v_[off:off + n].reshape(w[name].shape)
        off += n

    order = ("even_norm", "even_w_in", "mla_q_lat_norm", "mla_kv_lat_norm", "mla_w_uq", "mla_w_ukv", "mla_q_norm",
             "mla_k_nope_norm", "mla_k_rope_norm", "gqa_q_norm", "gqa_k_norm", "even_w_out", "odd_norm", "odd_w_qkv",
             "swa_q_norm", "swa_k_norm", "swa_sink", "odd_w_out", "mlp_norm", "mlp_w_up", "mlp_w_down")
    outs = [loss, grad_x[None]]
    for group in (grads, deltas, new_m, new_v):
        outs += [group[name] for name in order]
    return tuple(outs)
```

```python
import functools
import math

import numpy as np
import jax
import jax.numpy as jnp
from jax import lax
from jax.experimental import pallas as pl
from jax.experimental.pallas import tpu as pltpu

F32 = jnp.float32
BF16 = jnp.bfloat16
MESH = pl.DeviceIdType.MESH

VMEM_BYTES_V7X = 64 * 1024 * 1024
LANES = 128
SUBLANES_BF16 = 16

GRID_W = 64
NORM_EPS = 1e-6
ROPE_THETA = 500000.0
AXIAL_THETA = 10000.0
MLA_HEADS = 8
MLA_Q_LORA = 512
MLA_KV_LORA = 256
MLA_NOPE = 128
MLA_ROPE = 64
MLA_QK = MLA_NOPE + MLA_ROPE
MLA_V = 128
GQA_HEADS = 8
GQA_KV = 2
GQA_DIM = 128
SWA_HEADS = 32
SWA_KV = 4
SWA_DIM = 64
SWA_WINDOW = 128
SWA_ROT = SWA_DIM // 4
SWA_BLOCK = 128
SWA_HEAD_PARTS = 2
SWA_GROUPS = 4
ADAM_LR = 0.001
ADAM_B1 = 0.9
ADAM_B2 = 0.999
ADAM_EPS = 1e-08
ADAM_WD = 0.01
ADAM_STEP = 10
N_CHIPS = 4
COMM_LANES = 1024


def _tile(dim, cap, mult=LANES):
    if dim <= cap:
        return dim
    t = (cap // mult) * mult
    while t >= mult:
        if dim % t == 0:
            return t
        t -= mult
    return dim


def _params(dims, vmem_estimate):
    limit = int(min(max(vmem_estimate * 1.25 + (4 << 20), 32 << 20), VMEM_BYTES_V7X - (6 << 20)))
    return pltpu.CompilerParams(dimension_semantics=dims, vmem_limit_bytes=limit)


def _nbytes(shape, dtype):
    return int(np.prod(shape)) * jnp.dtype(dtype).itemsize


def _mm(a, b, *, mode, name, out_dtype=F32, epi=None, extra=None, split=1, caps=(1024, 1024, 2048)):
    if mode == "nn":
        (M, K), (K2, N) = a.shape, b.shape
    elif mode == "nt":
        (M, K), (N, K2) = a.shape, b.shape
    else:
        (K, M), (K2, N) = a.shape, b.shape
    assert K == K2, (a.shape, b.shape, mode)
    assert N % split == 0
    ns = N // split
    tn, tk = _tile(ns, caps[1]), _tile(K, caps[2])
    tm = _tile(M, min(caps[0], max(LANES, caps[0] * caps[1] // tn)))
    nj_per = ns // tn
    grid = (M // tm, N // tn, K // tk)
    nk = grid[2]
    if mode == "nn":
        a_spec = pl.BlockSpec((tm, tk), lambda i, j, k: (i, k))
        b_spec = pl.BlockSpec((tk, tn), lambda i, j, k: (k, j))
        dn = (((1,), (0,)), ((), ()))
    elif mode == "nt":
        a_spec = pl.BlockSpec((tm, tk), lambda i, j, k: (i, k))
        b_spec = pl.BlockSpec((tn, tk), lambda i, j, k: (j, k))
        dn = (((1,), (1,)), ((), ()))
    else:
        a_spec = pl.BlockSpec((tk, tm), lambda i, j, k: (k, i))
        b_spec = pl.BlockSpec((tk, tn), lambda i, j, k: (k, j))
        dn = (((0,), (0,)), ((), ()))
    if split == 1:
        o_spec = pl.BlockSpec((tm, tn), lambda i, j, k: (i, j))
        o_shape = (M, N)
    else:
        o_spec = pl.BlockSpec((None, tm, tn), lambda i, j, k: (j // nj_per, i, j % nj_per))
        o_shape = (split, M, ns)
    mn_spec = pl.BlockSpec((tm, tn), lambda i, j, k: (i, j))
    in_specs, args = [a_spec, b_spec], [a, b]
    if epi in ("add", "dsqrelu"):
        in_specs.append(mn_spec)
        args.append(extra)
    if epi == "sqrelu":
        out_shape = (jax.ShapeDtypeStruct(o_shape, BF16), jax.ShapeDtypeStruct(o_shape, BF16))
        out_specs = (o_spec, o_spec)
        n_out = 2
    else:
        out_shape = jax.ShapeDtypeStruct(o_shape, out_dtype)
        out_specs = o_spec
        n_out = 1

    def body(*refs):
        a_ref, b_ref = refs[0], refs[1]
        e_ref = refs[2] if len(args) == 3 else None
        outs = refs[len(args):len(args) + n_out]

        def finish(acc):
            if epi is None:
                outs[0][...] = acc.astype(outs[0].dtype)
            elif epi == "add":
                outs[0][...] = (e_ref[...] + acc).astype(outs[0].dtype)
            elif epi == "sqrelu":
                r = jnp.maximum(acc, 0.0)
                outs[0][...] = acc.astype(BF16)
                outs[1][...] = (r * r).astype(BF16)
            else:
                u = e_ref[...].astype(F32)
                outs[0][...] = (acc * (2.0 * jnp.maximum(u, 0.0))).astype(outs[0].dtype)

        prod = lax.dot_general(a_ref[...].astype(BF16), b_ref[...].astype(BF16), dn, preferred_element_type=F32)
        if nk == 1:
            finish(prod)
            return
        acc_ref = refs[-1]
        k = pl.program_id(2)

        @pl.when(k == 0)
        def _():
            acc_ref[...] = prod

        @pl.when((k != 0) & (k != nk - 1))
        def _():
            acc_ref[...] += prod

        @pl.when(k == nk - 1)
        def _():
            finish(acc_ref[...] + prod)

    est = 2 * (_nbytes((tm, tk), a.dtype) + _nbytes((tk, tn), b.dtype)) + _nbytes((tm, tn), F32)
    est += 2 * n_out * _nbytes((tm, tn), out_dtype if n_out == 1 else BF16)
    if len(args) == 3:
        est += 2 * _nbytes((tm, tn), extra.dtype)
    est += 3 * _nbytes((tm, tn), F32)
    return pl.pallas_call(
        body, name=name, grid=grid, in_specs=in_specs, out_specs=out_specs, out_shape=out_shape,
        scratch_shapes=[] if nk == 1 else [pltpu.VMEM((tm, tn), F32)],
        compiler_params=_params(("parallel", "parallel", "arbitrary"), est),
    )(*args)


def _perm(y, p):
    hi = y.astype(BF16)
    lo = (y - hi.astype(F32)).astype(BF16)
    d = lambda t: jnp.dot(t, p, preferred_element_type=F32)
    return d(hi) + d(lo)


def _rows_tile(T, d):
    return _tile(T, 2048 if d <= 256 else 512, 128)


def _norm_fwd(x, gain, *, name, rope=None, out_dtype=BF16, out_scale=None):
    H, T, d = x.shape
    tm = _rows_tile(T, d)
    g2 = gain.reshape(1, d).astype(F32)
    in_specs = [pl.BlockSpec((None, tm, d), lambda h, i: (h, i, 0)), pl.BlockSpec((1, d), lambda h, i: (0, 0))]
    args = [x, g2]
    if rope is not None:
        in_specs += [pl.BlockSpec((tm, d), lambda h, i: (i, 0)), pl.BlockSpec((tm, d), lambda h, i: (i, 0)),
                     pl.BlockSpec((d, d), lambda h, i: (0, 0))]
        args += list(rope)

    def body(*refs):
        x_ref, g_ref = refs[0], refs[1]
        o_ref = refs[-1]
        xv = x_ref[...]
        y = xv * lax.rsqrt(jnp.mean(xv * xv, axis=-1, keepdims=True) + NORM_EPS)
        y = y * g_ref[...]
        if rope is not None:
            c_ref, s_ref, p_ref = refs[2], refs[3], refs[4]
            y = y * c_ref[...] + _perm(y, p_ref[...]) * s_ref[...]
        if out_scale is not None:
            y = y * out_scale
        o_ref[...] = y.astype(o_ref.dtype)

    est = 2 * (_nbytes((tm, max(d, LANES)), F32) * (3 if rope is not None else 1) + _nbytes((tm, max(d, LANES)), out_dtype))
    est += 6 * _nbytes((tm, max(d, LANES)), F32)
    return pl.pallas_call(
        body, name=name, grid=(H, T // tm), in_specs=in_specs,
        out_specs=pl.BlockSpec((None, tm, d), lambda h, i: (h, i, 0)),
        out_shape=jax.ShapeDtypeStruct((H, T, d), out_dtype),
        compiler_params=_params(("parallel", "parallel"), est),
    )(*args)


def _norm_bwd(x, gain, dy, *, name, rope=None, group=1, res=None, out_dtype=F32, dy_scale=None):
    H, T, d = x.shape
    assert dy.shape == (H * group, T, d), (dy.shape, x.shape, group)
    tm = _rows_tile(T, d)
    g2 = gain.reshape(1, d).astype(F32)
    in_specs = [pl.BlockSpec((None, tm, d), lambda h, i: (h, i, 0)), pl.BlockSpec((1, d), lambda h, i: (0, 0)),
                pl.BlockSpec((group, tm, d), lambda h, i: (h, i, 0))]
    args = [x, g2, dy]
    if rope is not None:
        in_specs += [pl.BlockSpec((tm, d), lambda h, i: (i, 0)), pl.BlockSpec((tm, d), lambda h, i: (i, 0)),
                     pl.BlockSpec((d, d), lambda h, i: (0, 0))]
        args += list(rope)
    if res is not None:
        assert H == 1
        in_specs.append(pl.BlockSpec((tm, d), lambda h, i: (i, 0)))
        args.append(res)
    n_in = len(args)

    def body(*refs):
        x_ref, g_ref, dy_ref = refs[0], refs[1], refs[2]
        dx_ref, dg_ref = refs[n_in], refs[n_in + 1]
        first = (pl.program_id(0) == 0) & (pl.program_id(1) == 0)

        @pl.when(first)
        def _():
            dg_ref[...] = jnp.zeros_like(dg_ref)

        dyv = dy_ref[0].astype(F32)
        for g in range(1, group):
            dyv = dyv + dy_ref[g].astype(F32)
        if dy_scale is not None:
            dyv = dyv * dy_scale
        pos = 3
        if rope is not None:
            c_ref, s_ref, p_ref = refs[3], refs[4], refs[5]
            pos = 6
            dyv = dyv * c_ref[...] + _perm(dyv * s_ref[...], p_ref[...])
        xv = x_ref[...]
        r = lax.rsqrt(jnp.mean(xv * xv, axis=-1, keepdims=True) + NORM_EPS)
        xhat = xv * r
        dg_ref[...] += jnp.sum(dyv * xhat, axis=0, keepdims=True)
        dxh = dyv * g_ref[...]
        dx = r * (dxh - xhat * jnp.mean(dxh * xhat, axis=-1, keepdims=True))
        if res is not None:
            dx = dx + refs[pos][...]
        dx_ref[...] = dx.astype(dx_ref.dtype)

    wide = max(d, LANES)
    est = 2 * _nbytes((tm, wide), F32) * (2 + group + (2 if rope is not None else 0) + (1 if res is not None else 0))
    est += 8 * _nbytes((tm, wide), F32)
    return pl.pallas_call(
        body, name=name, grid=(H, T // tm), in_specs=in_specs,
        out_specs=(pl.BlockSpec((None, tm, d), lambda h, i: (h, i, 0)), pl.BlockSpec((1, d), lambda h, i: (0, 0))),
        out_shape=(jax.ShapeDtypeStruct((H, T, d), out_dtype), jax.ShapeDtypeStruct((1, d), F32)),
        compiler_params=_params(("arbitrary", "arbitrary"), est),
    )(*args)


def _group_sum(x, group, *, name, out_dtype=F32):
    HG, T, d = x.shape
    H = HG // group
    tm = _rows_tile(T, d)

    def body(x_ref, o_ref):
        acc = x_ref[0]
        for g in range(1, group):
            acc = acc + x_ref[g]
        o_ref[...] = acc.astype(o_ref.dtype)

    est = 2 * (group + 1) * _nbytes((tm, max(d, LANES)), F32)
    return pl.pallas_call(
        body, name=name, grid=(H, T // tm),
        in_specs=[pl.BlockSpec((group, tm, d), lambda h, i: (h, i, 0))],
        out_specs=pl.BlockSpec((None, tm, d), lambda h, i: (h, i, 0)),
        out_shape=jax.ShapeDtypeStruct((H, T, d), out_dtype),
        compiler_params=_params(("parallel", "parallel"), est),
    )(x)


def _delta(o, do, *, name):
    H, T, d = o.shape
    tm = _rows_tile(T, d)

    def body(o_ref, do_ref, dl_ref, dob_ref):
        dov = do_ref[...]
        dl = jnp.sum(o_ref[...] * dov, axis=-1, keepdims=True)
        dl_ref[...] = jnp.broadcast_to(dl, (tm, LANES))
        dob_ref[...] = dov.astype(BF16)

    spec = pl.BlockSpec((None, tm, d), lambda h, i: (h, i, 0))
    est = 2 * (3 * _nbytes((tm, max(d, LANES)), F32) + _nbytes((tm, LANES), F32))
    return pl.pallas_call(
        body, name=name, grid=(H, T // tm), in_specs=[spec, spec],
        out_specs=(pl.BlockSpec((None, tm, LANES), lambda h, i: (h, i, 0)), spec),
        out_shape=(jax.ShapeDtypeStruct((H, T, LANES), F32), jax.ShapeDtypeStruct((H, T, d), BF16)),
        compiler_params=_params(("parallel", "parallel"), est),
    )(o, do)


NT_DIMS = (((1,), (1,)), ((), ()))
TN_DIMS = (((0,), (0,)), ((), ()))
LOG2E = math.log2(math.e)
FLASH_CHUNK = 256
FLASH_ROW_PARTS = 4


def _flash_fwd(q, k, v, *, name, gather=()):
    H, T, dk = q.shape
    Hkv, _, dv = v.shape
    G = H // Hkv
    tq, tk = _tile(T, 1024), _tile(T, 4096)
    tp = _tile(tq, tq // FLASH_ROW_PARTS, SUBLANES_BF16)
    nk = T // tk

    n_r = len(gather)
    grid = (H, T // tq, nk)
    assert n_r == 0 or H >= 2

    def body(*refs):
        q_ref, k_ref, v_ref = refs[:3]
        o_ref, lse_ref, ob_ref = refs[3 + n_r:6 + n_r]
        m_ref, l_ref, acc_ref = refs[6 + 2 * n_r:9 + 2 * n_r]
        hi, qi, ki = pl.program_id(0), pl.program_id(1), pl.program_id(2)
        if n_r:
            ag_start, ag_forward, ag_finish = _gather_plan(refs[3:3 + n_r], refs[6 + n_r:6 + 2 * n_r],
                                                           *refs[9 + 2 * n_r:])
            pl.when((hi == 0) & (qi == 0) & (ki == 0))(ag_start)
            pl.when((hi == grid[0] - 1) & (qi == 0) & (ki == 0))(ag_forward)

        @pl.when(ki == 0)
        def _():
            m_ref[...] = jnp.full_like(m_ref, -jnp.inf)
            l_ref[...] = jnp.zeros_like(l_ref)
            acc_ref[...] = jnp.zeros_like(acc_ref)

        kv, vv = k_ref[...], v_ref[...]
        parts = [slice(part * tp, (part + 1) * tp) for part in range(tq // tp)]
        m_prev = [m_ref[rows, :] for rows in parts]
        l_prev = [l_ref[rows, :] for rows in parts]
        a_prev = [acc_ref[rows, :] for rows in parts]
        ss = [lax.dot_general(q_ref[rows, :], kv, NT_DIMS, preferred_element_type=F32) for rows in parts]
        m_new = [jnp.maximum(m, jnp.max(s, axis=-1, keepdims=True)) for m, s in zip(m_prev, ss)]
        alpha = [jnp.exp2(m - mn) for m, mn in zip(m_prev, m_new)]
        ps = [jnp.exp2(s - mn) for s, mn in zip(ss, m_new)]
        l_new = [a * l + jnp.sum(p, axis=-1, keepdims=True) for a, l, p in zip(alpha, l_prev, ps)]
        pv = [jnp.dot(p.astype(BF16), vv, preferred_element_type=F32) for p in ps]
        for rows, mn, ln, a, acc, o in zip(parts, m_new, l_new, alpha, a_prev, pv):
            m_ref[rows, :] = mn
            l_ref[rows, :] = ln
            acc_ref[rows, :] = a * acc + o

        @pl.when(ki == nk - 1)
        def _():
            l = l_ref[...]
            o = acc_ref[...] / l
            o_ref[...] = o
            ob_ref[...] = o.astype(BF16)
            lse_ref[...] = jnp.broadcast_to(m_ref[...] + jnp.log(l) * LOG2E, (tq, LANES))

        if n_r:
            pl.when((hi == grid[0] - 1) & (qi == grid[1] - 1) & (ki == grid[2] - 1))(ag_finish)

    est = 2 * (_nbytes((tq, dk), BF16) + _nbytes((tk, dk + dv), BF16) + _nbytes((tq, dv + LANES), F32))
    est += 4 * _nbytes((tq, tk), F32) + 3 * _nbytes((tq, dv + 3 * LANES), F32)
    outs = pl.pallas_call(
        body, name=name, grid=grid,
        in_specs=[pl.BlockSpec((None, tq, dk), lambda h, i, j: (h, i, 0)),
                  pl.BlockSpec((None, tk, dk), lambda h, i, j: (h // G, j, 0)),
                  pl.BlockSpec((None, tk, dv), lambda h, i, j: (h // G, j, 0))] + [HBM_SPEC] * n_r,
        out_specs=[pl.BlockSpec((None, tq, dv), lambda h, i, j: (h, i, 0)),
                   pl.BlockSpec((None, tq, LANES), lambda h, i, j: (h, i, 0)),
                   pl.BlockSpec((None, tq, dv), lambda h, i, j: (h, i, 0))] + [HBM_SPEC] * n_r,
        out_shape=[jax.ShapeDtypeStruct((H, T, dv), F32), jax.ShapeDtypeStruct((H, T, LANES), F32),
                   jax.ShapeDtypeStruct((H, T, dv), BF16)] + _gathered_shapes(gather),
        scratch_shapes=[pltpu.VMEM((tq, 1), F32), pltpu.VMEM((tq, 1), F32), pltpu.VMEM((tq, dv), F32)]
                       + (_gather_scratch(n_r) if n_r else []),
        compiler_params=_params(("arbitrary",) * 3 if n_r else ("parallel", "parallel", "arbitrary"), est),
    )(q, k, v, *gather)
    return (outs[0], outs[1], outs[2], outs[3:]) if n_r else (outs[0], outs[1], outs[2])


def _flash_bwd(q, k, v, do, lse2, delta, *, name, rider=None):
    H, T, dk = q.shape
    Hkv, _, dv = v.shape
    G = H // Hkv
    tq, tk = _tile(T, 1024), _tile(T, 2048)
    tc = _tile(tk, FLASH_CHUNK)

    kind, carried = rider if rider is not None else (None, ())
    n_r = len(carried)
    grid = (H, T // tk, T // tq)
    if kind == "scatter":
        plan, rider_scratch = _scatter_plan, _scatter_scratch(n_r)
        rider_shapes = [jax.ShapeDtypeStruct(p.shape, p.dtype) for p in carried]
    elif kind == "exchange":
        plan, rider_scratch = _exchange_plan, _exchange_scratch(n_r)
        rider_shapes = [jax.ShapeDtypeStruct((N_CHIPS,) + g.shape[2:], g.dtype) for g in carried]
    else:
        assert kind is None
        rider_scratch, rider_shapes = [], []

    def body(*refs):
        q_ref, k_ref, v_ref, do_ref, lse_ref, dl_ref = refs[:6]
        dq_ref, dk_ref, dv_ref = refs[6 + n_r:9 + n_r]
        hi, ki, qi = pl.program_id(0), pl.program_id(1), pl.program_id(2)
        if n_r:
            rider_start, rider_finish = plan(refs[6:6 + n_r], refs[9 + n_r:9 + 2 * n_r], *refs[9 + 2 * n_r:])
            pl.when((hi == 0) & (ki == 0) & (qi == 0))(rider_start)
        rows = pl.ds(pl.multiple_of(qi * tq, tq), tq)

        @pl.when(qi == 0)
        def _():
            dk_ref[...] = jnp.zeros_like(dk_ref)
            dv_ref[...] = jnp.zeros_like(dv_ref)

        @pl.when(ki == 0)
        def _():
            dq_ref[rows, :] = jnp.zeros((tq, dk), F32)

        qv, dov = q_ref[...], do_ref[...]
        lse2 = lse_ref[:, :1]
        dl = dl_ref[:, :1]
        chunks = [slice(c * tc, (c + 1) * tc) for c in range(tk // tc)]
        kcs = [k_ref[ks, :] for ks in chunks]
        vcs = [v_ref[ks, :] for ks in chunks]
        dv_old = [dv_ref[ks, :] for ks in chunks]
        dk_old = [dk_ref[ks, :] for ks in chunks]
        dq_old = dq_ref[rows, :]
        ss = [lax.dot_general(qv, kc, NT_DIMS, preferred_element_type=F32) for kc in kcs]
        dps = [lax.dot_general(dov, vc, NT_DIMS, preferred_element_type=F32) for vc in vcs]
        ps = [jnp.exp2(s - lse2) for s in ss]
        dss = [(p * (dp - dl)).astype(BF16) for p, dp in zip(ps, dps)]
        pbs = [p.astype(BF16) for p in ps]
        dvs = [lax.dot_general(pb, dov, TN_DIMS, preferred_element_type=F32) for pb in pbs]
        dks = [lax.dot_general(ds, qv, TN_DIMS, preferred_element_type=F32) for ds in dss]
        dqs = [jnp.dot(ds, kc, preferred_element_type=F32) for ds, kc in zip(dss, kcs)]
        for ks, old, new in zip(chunks, dv_old, dvs):
            dv_ref[ks, :] = old + new
        for ks, old, new in zip(chunks, dk_old, dks):
            dk_ref[ks, :] = old + new
        dq_c = dqs[0]
        for extra in dqs[1:]:
            dq_c = dq_c + extra
        dq_ref[rows, :] = dq_old + dq_c

        if n_r:
            pl.when((hi == grid[0] - 1) & (ki == grid[1] - 1) & (qi == grid[2] - 1))(rider_finish)

    est = 2 * (_nbytes((tq, dk + dv), BF16) + _nbytes((tk, dk + dv), BF16) + 2 * _nbytes((tq, LANES), F32))
    est += 2 * (_nbytes((T, dk), F32) + _nbytes((tk, dk + dv), F32)) + 10 * _nbytes((tq, tc), F32)
    outs = pl.pallas_call(
        body, name=name, grid=grid,
        in_specs=[pl.BlockSpec((None, tq, dk), lambda h, j, i: (h, i, 0)),
                  pl.BlockSpec((None, tk, dk), lambda h, j, i: (h // G, j, 0)),
                  pl.BlockSpec((None, tk, dv), lambda h, j, i: (h // G, j, 0)),
                  pl.BlockSpec((None, tq, dv), lambda h, j, i: (h, i, 0)),
                  pl.BlockSpec((None, tq, LANES), lambda h, j, i: (h, i, 0)),
                  pl.BlockSpec((None, tq, LANES), lambda h, j, i: (h, i, 0))] + [HBM_SPEC] * n_r,
        out_specs=[pl.BlockSpec((None, T, dk), lambda h, j, i: (h, 0, 0)),
                   pl.BlockSpec((None, tk, dk), lambda h, j, i: (h, j, 0)),
                   pl.BlockSpec((None, tk, dv), lambda h, j, i: (h, j, 0))] + [HBM_SPEC] * n_r,
        out_shape=[jax.ShapeDtypeStruct((H, T, dk), F32), jax.ShapeDtypeStruct((H, T, dk), F32),
                   jax.ShapeDtypeStruct((H, T, dv), F32)] + rider_shapes,
        scratch_shapes=rider_scratch,
        compiler_params=_params(("arbitrary", "arbitrary", "arbitrary"), est),
    )(q, k, v, do, lse2, delta, *carried)
    return (outs[0], outs[1], outs[2], outs[3:]) if n_r else tuple(outs)


def _swa_specs(G, d, n_blocks, lanes, gpb):
    B = SWA_BLOCK
    prev = lambda j, i: (j, jnp.maximum(i - 1, 0), 0)
    cur = lambda j, i: (j, i, 0)
    nxt = lambda j, i: (j, jnp.minimum(i + 1, n_blocks - 1), 0)
    q_specs = [pl.BlockSpec((gpb * G, B, lanes), m) for m in (prev, cur, nxt)]
    kv_specs = [pl.BlockSpec((gpb, B, d), m) for m in (prev, cur, nxt)]
    return q_specs, kv_specs, cur


def _swa_parts(G, gpb):
    gp = G // SWA_HEAD_PARTS
    return gp, [(g, slice(g * G + part * gp, g * G + (part + 1) * gp)) for g in range(gpb) for part in range(SWA_HEAD_PARTS)]


def _swa_bias(i, T):
    B = SWA_BLOCK
    row = lax.broadcasted_iota(jnp.int32, (B, 3 * B), 0)
    col = lax.broadcasted_iota(jnp.int32, (B, 3 * B), 1)
    kpos = (i - 1) * B + col
    valid = (col >= row) & (col <= row + 2 * SWA_WINDOW) & (kpos >= 0) & (kpos < T)
    return jnp.where(valid, 0.0, -jnp.inf)


def _swa_fwd(q, k, v, sink, scale, *, name):
    Hq, T, d = q.shape
    Hkv = k.shape[0]
    G = Hq // Hkv
    B = SWA_BLOCK
    nb = T // B
    gpb = _tile(Hkv, SWA_GROUPS, 1)
    _, kv_specs, cur = _swa_specs(G, d, nb, d, gpb)

    def body(q_ref, k0, k1, k2, v0, v1, v2, sink_ref, o_ref, lse_ref, ob_ref):
        i = pl.program_id(1)
        kvs = [jnp.concatenate([k0[g], k1[g], k2[g]], axis=0) for g in range(gpb)]
        vvs = [jnp.concatenate([v0[g], v1[g], v2[g]], axis=0) for g in range(gpb)]
        bias = _swa_bias(i, T)[None]
        gp, parts = _swa_parts(G, gpb)
        sks = [sink_ref[hs] for _, hs in parts]
        ss = [lax.dot_general(q_ref[hs].reshape(gp * B, d), kvs[g], NT_DIMS, preferred_element_type=F32) for g, hs in parts]
        ss = [(s * scale).reshape(gp, B, 3 * B) + bias for s in ss]
        ms = [jnp.maximum(jnp.max(s, axis=-1, keepdims=True), sk) for s, sk in zip(ss, sks)]
        ps = [jnp.exp(s - m) for s, m in zip(ss, ms)]
        dens = [jnp.sum(p, axis=-1, keepdims=True) + jnp.exp(sk - m) for p, sk, m in zip(ps, sks, ms)]
        pns = [(p / den).reshape(gp * B, 3 * B).astype(BF16) for p, den in zip(ps, dens)]
        os_ = [jnp.dot(pn, vvs[g], preferred_element_type=F32).reshape(gp, B, d) for pn, (g, _) in zip(pns, parts)]
        for (_, hs), o, m, den in zip(parts, os_, ms, dens):
            o_ref[hs] = o
            ob_ref[hs] = o.astype(BF16)
            lse_ref[hs] = jnp.broadcast_to(m + jnp.log(den), (gp, B, LANES))

    GG = gpb * G
    est = 2 * (_nbytes((GG, B, LANES), BF16) + 6 * gpb * _nbytes((B, LANES), BF16) + 2 * _nbytes((GG, B, LANES), F32))
    est += 8 * _nbytes((GG * B, 3 * B), F32)
    return pl.pallas_call(
        body, name=name, grid=(Hkv // gpb, nb),
        in_specs=[pl.BlockSpec((GG, B, d), cur)] + kv_specs + kv_specs + [pl.BlockSpec((GG, 1, 1), lambda j, i: (j, 0, 0))],
        out_specs=(pl.BlockSpec((GG, B, d), cur), pl.BlockSpec((GG, B, LANES), cur), pl.BlockSpec((GG, B, d), cur)),
        out_shape=(jax.ShapeDtypeStruct((Hq, T, d), F32), jax.ShapeDtypeStruct((Hq, T, LANES), F32),
                   jax.ShapeDtypeStruct((Hq, T, d), BF16)),
        compiler_params=_params(("parallel", "parallel"), est),
    )(q, k, k, k, v, v, v, sink)


def _swa_dq(q, k, v, o, do, lse, sink, scale, *, name):
    Hq, T, d = q.shape
    Hkv = k.shape[0]
    G = Hq // Hkv
    B = SWA_BLOCK
    nb = T // B
    gpb = _tile(Hkv, SWA_GROUPS, 1)
    _, kv_specs, cur = _swa_specs(G, d, nb, d, gpb)

    def body(q_ref, do_ref, lse_ref, o_ref, k0, k1, k2, v0, v1, v2, sink_ref, dq_ref, dsink_ref, dl_ref, dob_ref):
        i = pl.program_id(1)
        kvs = [jnp.concatenate([k0[g], k1[g], k2[g]], axis=0) for g in range(gpb)]
        vvs = [jnp.concatenate([v0[g], v1[g], v2[g]], axis=0) for g in range(gpb)]
        bias = _swa_bias(i, T)[None]
        gp, parts = _swa_parts(G, gpb)
        lses = [lse_ref[hs, :, :1] for _, hs in parts]
        dovs = [do_ref[hs] for _, hs in parts]
        dls = [jnp.sum(o_ref[hs] * dov, axis=-1, keepdims=True) for (_, hs), dov in zip(parts, dovs)]
        dobs = [dov.astype(BF16) for dov in dovs]
        ss = [lax.dot_general(q_ref[hs].reshape(gp * B, d), kvs[g], NT_DIMS, preferred_element_type=F32) for g, hs in parts]
        dps = [lax.dot_general(dob.reshape(gp * B, d), vvs[g], NT_DIMS, preferred_element_type=F32)
               for dob, (g, _) in zip(dobs, parts)]
        ps = [jnp.exp((s * scale).reshape(gp, B, 3 * B) + bias - lse) for s, lse in zip(ss, lses)]
        dss = [(p * (dp.reshape(gp, B, 3 * B) - dl) * scale).reshape(gp * B, 3 * B).astype(BF16)
               for p, dp, dl in zip(ps, dps, dls)]
        dqs = [jnp.dot(ds, kvs[g], preferred_element_type=F32).reshape(gp, B, d) for ds, (g, _) in zip(dss, parts)]
        dsks = [-jnp.sum(jnp.exp(sink_ref[hs] - lse) * dl, axis=1, keepdims=True)
                for (_, hs), lse, dl in zip(parts, lses, dls)]

        @pl.when(i == 0)
        def _():
            dsink_ref[...] = jnp.zeros_like(dsink_ref)

        for (_, hs), dq, dsk, dl, dob in zip(parts, dqs, dsks, dls, dobs):
            dq_ref[hs] = dq
            dsink_ref[hs] += jnp.broadcast_to(dsk, (gp, 1, LANES))
            dl_ref[hs] = jnp.broadcast_to(dl, (gp, B, LANES))
            dob_ref[hs] = dob

    GG = gpb * G
    est = 2 * (2 * _nbytes((GG, B, LANES), BF16) + 6 * gpb * _nbytes((B, LANES), BF16) + 5 * _nbytes((GG, B, LANES), F32))
    est += 8 * _nbytes((GG * B, 3 * B), F32)
    q_spec, l_spec = pl.BlockSpec((GG, B, d), cur), pl.BlockSpec((GG, B, LANES), cur)
    return pl.pallas_call(
        body, name=name, grid=(Hkv // gpb, nb),
        in_specs=[q_spec, q_spec, l_spec, q_spec] + kv_specs + kv_specs + [pl.BlockSpec((GG, 1, 1), lambda j, i: (j, 0, 0))],
        out_specs=(q_spec, pl.BlockSpec((GG, 1, LANES), lambda j, i: (j, 0, 0)), l_spec, q_spec),
        out_shape=(jax.ShapeDtypeStruct((Hq, T, d), F32), jax.ShapeDtypeStruct((Hq, 1, LANES), F32),
                   jax.ShapeDtypeStruct((Hq, T, LANES), F32), jax.ShapeDtypeStruct((Hq, T, d), BF16)),
        compiler_params=_params(("arbitrary", "arbitrary"), est),
    )(q, do, lse, o, k, k, k, v, v, v, sink)


def _swa_dkv(q, k, v, do, lse, delta, scale, *, name):
    Hq, T, d = q.shape
    Hkv = k.shape[0]
    G = Hq // Hkv
    B = SWA_BLOCK
    nb = T // B
    gpb = _tile(Hkv, SWA_GROUPS, 1)
    q_specs, _, cur = _swa_specs(G, d, nb, d, gpb)
    l_specs, _, _ = _swa_specs(G, d, nb, LANES, gpb)

    def body(k_ref, v_ref, q0, q1, q2, d0, d1, d2, l0, l1, l2, e0, e1, e2, dk_ref, dv_ref):
        b = pl.program_id(1)
        row = lax.broadcasted_iota(jnp.int32, (B, B), 0)
        col = lax.broadcasted_iota(jnp.int32, (B, B), 1)
        biases = []
        for part in range(3):
            qpos = (b + part - 1) * B + row
            diff = (part - 1) * B + row - col
            valid = (diff >= -SWA_WINDOW) & (diff <= SWA_WINDOW) & (qpos >= 0) & (qpos < T)
            biases.append(jnp.where(valid, 0.0, -jnp.inf)[None])
        chains = [(g, part) for g in range(gpb) for part in range(3)]
        heads = [slice(g * G, (g + 1) * G) for g, _ in chains]
        kvs = [k_ref[g] for g, _ in chains]
        vvs = [v_ref[g] for g, _ in chains]
        qvs = [(q0, q1, q2)[part][hs].reshape(G * B, d) for (_, part), hs in zip(chains, heads)]
        dovs = [(d0, d1, d2)[part][hs].reshape(G * B, d) for (_, part), hs in zip(chains, heads)]
        lses = [(l0, l1, l2)[part][hs, :, :1] for (_, part), hs in zip(chains, heads)]
        dls = [(e0, e1, e2)[part][hs, :, :1] for (_, part), hs in zip(chains, heads)]
        ss = [lax.dot_general(qv, kv, NT_DIMS, preferred_element_type=F32) for qv, kv in zip(qvs, kvs)]
        dps = [lax.dot_general(dov, vv, NT_DIMS, preferred_element_type=F32) for dov, vv in zip(dovs, vvs)]
        ps = [jnp.exp((s * scale).reshape(G, B, B) + biases[part] - lse) for s, (_, part), lse in zip(ss, chains, lses)]
        dss = [(p * (dp.reshape(G, B, B) - dl) * scale).reshape(G * B, B).astype(BF16) for p, dp, dl in zip(ps, dps, dls)]
        pbs = [p.reshape(G * B, B).astype(BF16) for p in ps]
        dvs = [lax.dot_general(pb, dov, TN_DIMS, preferred_element_type=F32) for pb, dov in zip(pbs, dovs)]
        dks = [lax.dot_general(ds, qv, TN_DIMS, preferred_element_type=F32) for ds, qv in zip(dss, qvs)]
        for g in range(gpb):
            dk_ref[g] = dks[3 * g] + dks[3 * g + 1] + dks[3 * g + 2]
            dv_ref[g] = (dvs[3 * g] + dvs[3 * g + 1] + dvs[3 * g + 2]).astype(BF16)

    GG = gpb * G
    est = 2 * (6 * _nbytes((GG, B, LANES), BF16) + 6 * _nbytes((GG, B, LANES), F32) + 4 * gpb * _nbytes((B, LANES), F32))
    est += 10 * _nbytes((GG * B, B), F32)
    kspec = pl.BlockSpec((gpb, B, d), cur)
    return pl.pallas_call(
        body, name=name, grid=(Hkv // gpb, nb),
        in_specs=[kspec, kspec] + q_specs + q_specs + l_specs + l_specs,
        out_specs=(kspec, kspec),
        out_shape=(jax.ShapeDtypeStruct((Hkv, T, d), F32), jax.ShapeDtypeStruct((Hkv, T, d), BF16)),
        compiler_params=_params(("parallel", "parallel"), est),
    )(k, v, q, q, q, do, do, do, lse, lse, lse, delta, delta, delta)


def _loss_head(y, target, *, name):
    T, D = y.shape
    tm = _tile(T, 512)

    def body(y_ref, t_ref, dy_ref, s_ref):
        @pl.when(pl.program_id(0) == 0)
        def _():
            s_ref[...] = jnp.zeros_like(s_ref)

        e = y_ref[...] - t_ref[...]
        dy_ref[...] = e / D
        s_ref[...] += jnp.sum(jnp.sum(e * e, axis=-1, keepdims=True), axis=0, keepdims=True)

    spec = pl.BlockSpec((tm, D), lambda i: (i, 0))
    return pl.pallas_call(
        body, name=name, grid=(T // tm,), in_specs=[spec, spec],
        out_specs=(spec, pl.BlockSpec((1, 1), lambda i: (0, 0))),
        out_shape=(jax.ShapeDtypeStruct((T, D), F32), jax.ShapeDtypeStruct((1, 1), F32)),
        compiler_params=_params(("arbitrary",), 8 * _nbytes((tm, D), F32)),
    )(y, target)


def _adamw(w, g, m, v, *, name):
    R, C = w.shape
    tr = _tile(R, max(8, (1 << 19) // max(C, LANES) // 8 * 8), 8)

    def body(w_ref, g_ref, m_ref, v_ref, d_ref, nm_ref, nv_ref):
        gv = g_ref[...]
        nm = ADAM_B1 * m_ref[...] + (1.0 - ADAM_B1) * gv
        nv = ADAM_B2 * v_ref[...] + (1.0 - ADAM_B2) * jnp.square(gv)
        m_hat = nm / (1.0 - ADAM_B1 ** ADAM_STEP)
        v_hat = nv / (1.0 - ADAM_B2 ** ADAM_STEP)
        d_ref[...] = -ADAM_LR * (m_hat / (jnp.sqrt(v_hat) + ADAM_EPS) + ADAM_WD * w_ref[...])
        nm_ref[...] = nm
        nv_ref[...] = nv

    spec = pl.BlockSpec((tr, C), lambda i: (i, 0))
    sds = jax.ShapeDtypeStruct((R, C), F32)
    return pl.pallas_call(
        body, name=name, grid=(R // tr,), in_specs=[spec] * 4, out_specs=(spec,) * 3, out_shape=(sds,) * 3,
        compiler_params=_params(("parallel",), 16 * _nbytes((tr, max(C, LANES)), F32)),
    )(w, g, m, v)


def _to_bf16(w, *, name):
    R, C = w.shape
    tr = _comm_rows_tile(R, C)

    def body(w_ref, o_ref):
        o_ref[...] = w_ref[...].astype(BF16)

    spec = pl.BlockSpec((tr, C), lambda i: (i, 0))
    return pl.pallas_call(
        body, name=name, grid=(R // tr,), in_specs=[spec], out_specs=spec, out_shape=jax.ShapeDtypeStruct((R, C), BF16),
        compiler_params=_params(("parallel",), 6 * _nbytes((tr, max(C, LANES)), F32)),
    )(w)


def _comm_rows_tile(R, L):
    return _tile(R, max(SUBLANES_BF16, (1 << 19) // L // SUBLANES_BF16 * SUBLANES_BF16), SUBLANES_BF16)


def _pair_add(g, recv, c_idx, *, name):
    _, _, R, L = g.shape
    tr = _comm_rows_tile(R, L)

    def body(c_ref, g_ref, r_ref, o_ref):
        o_ref[...] = (g_ref[...] + r_ref[...]).astype(BF16)

    grid_spec = pltpu.PrefetchScalarGridSpec(
        num_scalar_prefetch=1, grid=(N_CHIPS, R // tr),
        in_specs=[pl.BlockSpec((None, None, tr, L), lambda j, i, c: (j, c[0], i, 0)),
                  pl.BlockSpec((None, tr, L), lambda j, i, c: (j, i, 0))],
        out_specs=pl.BlockSpec((None, tr, L), lambda j, i, c: (j, i, 0)))
    return pl.pallas_call(
        body, name=name, grid_spec=grid_spec, out_shape=jax.ShapeDtypeStruct((N_CHIPS, R, L), BF16),
        compiler_params=_params(("parallel", "parallel"), 8 * _nbytes((tr, L), F32)),
    )(c_idx, g, recv)


def _sum_chips(q, c_idx, *, name):
    _, R, L = q.shape
    tr = _comm_rows_tile(R, L)

    def body(c_ref, q_ref, o_ref):
        acc = q_ref[0].astype(F32)
        for j in range(1, N_CHIPS):
            acc = acc + q_ref[j].astype(F32)
        o_ref[...] = acc

    grid_spec = pltpu.PrefetchScalarGridSpec(
        num_scalar_prefetch=1, grid=(R // tr,),
        in_specs=[pl.BlockSpec((N_CHIPS, tr, L), lambda i, c: (0, i, 0))],
        out_specs=pl.BlockSpec((None, tr, L), lambda i, c: (c[0], i, 0)))
    return pl.pallas_call(
        body, name=name, grid_spec=grid_spec, out_shape=jax.ShapeDtypeStruct((2, R, L), F32),
        compiler_params=_params(("parallel",), 10 * _nbytes((tr, L), F32)),
    )(c_idx, q)


HBM_SPEC = pl.BlockSpec(memory_space=pltpu.HBM)


def _position():
    return lax.axis_index("x"), lax.axis_index("y"), lax.axis_index("c")


def _other_chips(x, y):
    return [(1 - x, y), (x, 1 - y), (1 - x, 1 - y)]


AG_COPIES = 7


def _gather_plan(w_refs, out_refs, send_sems, recv_sems, local_sems):
    n = len(w_refs)
    x, y, c = _position()
    me, sibling = (x, y, c), (x, y, 1 - c)
    chips = _other_chips(x, y)

    def copy(i, k, block, to, src=None):
        px, py, pc = block
        slot = out_refs[i].at[4 * px + 2 * py + pc]
        return pltpu.make_async_remote_copy(
            src_ref=slot if src is None else src, dst_ref=slot, send_sem=send_sems.at[AG_COPIES * i + k],
            recv_sem=recv_sems.at[AG_COPIES * i + k], device_id=to, device_id_type=MESH)

    def local(i):
        return pltpu.make_async_copy(w_refs[i].at[c], out_refs[i].at[4 * x + 2 * y + c], local_sems.at[i])

    def first(i):
        own = w_refs[i].at[c]
        return [copy(i, 0, me, sibling, src=own)] + [copy(i, 1 + j, me, (*chip, c), src=own)
                                                     for j, chip in enumerate(chips)]

    def passed(i):
        return [copy(i, 4 + j, (*chip, c), sibling) for j, chip in enumerate(chips)]

    def start():
        for i in range(n):
            local(i).start()
            for cp in first(i):
                cp.start()

    def forward():
        for i in range(n):
            for j, chip in enumerate(chips):
                copy(i, 1 + j, (*chip, c), me).wait_recv()
                passed(i)[j].start()

    def finish():
        for i in range(n):
            copy(i, 0, sibling, me).wait_recv()
            for j, chip in enumerate(chips):
                copy(i, 4 + j, (*chip, 1 - c), me).wait_recv()
        for i in range(n):
            for cp in first(i) + passed(i):
                cp.wait_send()
            local(i).wait()

    return start, forward, finish


def _gather_scratch(n):
    return [pltpu.SemaphoreType.DMA((AG_COPIES * n,)), pltpu.SemaphoreType.DMA((AG_COPIES * n,)),
            pltpu.SemaphoreType.DMA((n,))]


def _gathered_shapes(ws):
    return [jax.ShapeDtypeStruct((2 * N_CHIPS,) + w.shape[1:], w.dtype) for w in ws]


def _all_gather_halves(ws, *, name):
    n = len(ws)

    def body(*refs):
        for step in _gather_plan(refs[:n], refs[n:2 * n], *refs[2 * n:]):
            step()

    return pl.pallas_call(
        body, name=name, in_specs=[HBM_SPEC] * n, out_specs=[HBM_SPEC] * n, out_shape=_gathered_shapes(ws),
        scratch_shapes=_gather_scratch(n),
    )(*ws)


def _exchange_plan(g_refs, out_refs, send_sems, recv_sems):
    n = len(g_refs)
    x, y, c = _position()

    def copies():
        return [pltpu.make_async_remote_copy(
            src_ref=g_refs[i].at[j, 1 - c], dst_ref=out_refs[i].at[j], send_sem=send_sems.at[N_CHIPS * i + j],
            recv_sem=recv_sems.at[N_CHIPS * i + j], device_id=(x, y, 1 - c), device_id_type=MESH)
            for i in range(n) for j in range(N_CHIPS)]

    def start():
        for cp in copies():
            cp.start()

    def finish():
        for cp in copies():
            cp.wait()

    return start, finish


def _exchange_scratch(n):
    return [pltpu.SemaphoreType.DMA((N_CHIPS * n,)), pltpu.SemaphoreType.DMA((N_CHIPS * n,))]


def _sibling_exchange(gs, *, name):
    n = len(gs)

    def body(*refs):
        for step in _exchange_plan(refs[:n], refs[n:2 * n], *refs[2 * n:]):
            step()

    return pl.pallas_call(
        body, name=name, in_specs=[HBM_SPEC] * n, out_specs=[HBM_SPEC] * n,
        out_shape=[jax.ShapeDtypeStruct((N_CHIPS,) + g.shape[2:], g.dtype) for g in gs],
        scratch_shapes=_exchange_scratch(n),
    )(*gs)


def _scatter_plan(p_refs, q_refs, send_sems, recv_sems, local_sems):
    n = len(p_refs)
    others = N_CHIPS - 1
    x, y, c = _position()
    me = 2 * x + y
    chips = _other_chips(x, y)

    def copy(i, k, chip, src_slot, dst_slot):
        return pltpu.make_async_remote_copy(
            src_ref=p_refs[i].at[src_slot], dst_ref=q_refs[i].at[dst_slot], send_sem=send_sems.at[others * i + k],
            recv_sem=recv_sems.at[others * i + k], device_id=(*chip, c), device_id_type=MESH)

    def local(i):
        return pltpu.make_async_copy(p_refs[i].at[me], q_refs[i].at[me], local_sems.at[i])

    def sends(i):
        return [copy(i, k, chip, 2 * chip[0] + chip[1], me) for k, chip in enumerate(chips)]

    def start():
        for i in range(n):
            local(i).start()
            for cp in sends(i):
                cp.start()

    def finish():
        for i in range(n):
            for k, chip in enumerate(chips):
                copy(i, k, chip, me, 2 * chip[0] + chip[1]).wait_recv()
        for i in range(n):
            for cp in sends(i):
                cp.wait_send()
            local(i).wait()

    return start, finish


def _scatter_scratch(n):
    others = N_CHIPS - 1
    return [pltpu.SemaphoreType.DMA((others * n,)), pltpu.SemaphoreType.DMA((others * n,)),
            pltpu.SemaphoreType.DMA((n,))]


def _chip_scatter(ps, *, name):
    n = len(ps)

    def body(*refs):
        for step in _scatter_plan(refs[:n], refs[n:2 * n], *refs[2 * n:]):
            step()

    return pl.pallas_call(
        body, name=name, in_specs=[HBM_SPEC] * n, out_specs=[HBM_SPEC] * n,
        out_shape=[jax.ShapeDtypeStruct(p.shape, p.dtype) for p in ps], scratch_shapes=_scatter_scratch(n),
    )(*ps)


def _sibling_share(fs, *, name):
    n = len(fs)

    def body(*refs):
        in_refs, out_refs = refs[:n], refs[n:2 * n]
        send_sems, recv_sems = refs[2 * n:]
        x, y, c = _position()

        def copy(i, half):
            return pltpu.make_async_remote_copy(
                src_ref=in_refs[i].at[half], dst_ref=out_refs[i].at[half], send_sem=send_sems.at[i],
                recv_sem=recv_sems.at[i], device_id=(x, y, 1 - c), device_id_type=MESH)

        sends = [copy(i, c) for i in range(n)]
        for cp in sends:
            cp.start()
        for i in range(n):
            copy(i, 1 - c).wait_recv()
        for cp in sends:
            cp.wait_send()

    return pl.pallas_call(
        body, name=name, in_specs=[HBM_SPEC] * n, out_specs=[HBM_SPEC] * n,
        out_shape=[jax.ShapeDtypeStruct(f.shape, f.dtype) for f in fs],
        input_output_aliases={i: i for i in range(n)},
        scratch_shapes=[pltpu.SemaphoreType.DMA((n,)), pltpu.SemaphoreType.DMA((n,))],
    )(*fs)


def _all_reduce_small(s, *, name):
    R, L = s.shape
    n_dev = 2 * N_CHIPS

    def body(s_ref, out_ref, buf, send_sems, recv_sems, local_sem):
        x, y, c = _position()
        me, sibling = (x, y, c), (x, y, 1 - c)
        chips = _other_chips(x, y)

        def slot(px, py, pc):
            return buf.at[4 * px + 2 * py + pc]

        def copy(k, block, to, src=None):
            return pltpu.make_async_remote_copy(
                src_ref=slot(*block) if src is None else src, dst_ref=slot(*block),
                send_sem=send_sems.at[k], recv_sem=recv_sems.at[k], device_id=to, device_id_type=MESH)

        mine = pltpu.make_async_copy(s_ref, slot(*me), local_sem)
        mine.start()
        first = [copy(0, me, sibling, src=s_ref)]
        first += [copy(1 + j, me, (*chip, c), src=s_ref) for j, chip in enumerate(chips)]
        for cp in first:
            cp.start()
        passed = [copy(4 + j, (*chip, c), sibling) for j, chip in enumerate(chips)]
        for j, chip in enumerate(chips):
            copy(1 + j, (*chip, c), me).wait_recv()
            passed[j].start()
        copy(0, sibling, me).wait_recv()
        for j, chip in enumerate(chips):
            copy(4 + j, (*chip, 1 - c), me).wait_recv()
        for cp in first + passed:
            cp.wait_send()
        mine.wait()
        acc = buf[0]
        for j in range(1, n_dev):
            acc = acc + buf[j]
        out_ref[...] = acc

    vmem = pl.BlockSpec(memory_space=pltpu.VMEM)
    return pl.pallas_call(
        body, name=name, in_specs=[vmem], out_specs=vmem, out_shape=jax.ShapeDtypeStruct((R, L), F32),
        scratch_shapes=[pltpu.VMEM((n_dev, R, L), F32), pltpu.SemaphoreType.DMA((7,)), pltpu.SemaphoreType.DMA((7,)),
                        pltpu.SemaphoreType.DMA],
    )(s)


def _rope_cos_sin(pos, dim, theta):
    inv = jnp.float32(theta) ** (-jnp.arange(0, dim, 2, dtype=F32) / dim)
    ang = pos.astype(F32)[:, None] * inv[None, :]
    return jnp.cos(ang), jnp.sin(ang)


def _rope_tables(T, d, segments):
    P = np.zeros((d, d), np.float32)
    c_parts, s_parts, at = [], [], 0
    for start, size, cos, sin in segments:
        half = size // 2
        if start > at:
            c_parts.append(jnp.ones((T, start - at), F32))
            s_parts.append(jnp.zeros((T, start - at), F32))
        c_parts += [cos, cos]
        s_parts += [-sin, sin]
        at = start + size
        for p in range(half):
            P[start + half + p, start + p] = 1.0
            P[start + p, start + half + p] = 1.0
    if at < d:
        c_parts.append(jnp.ones((T, d - at), F32))
        s_parts.append(jnp.zeros((T, d - at), F32))
    return jnp.concatenate(c_parts, axis=1), jnp.concatenate(s_parts, axis=1), jnp.asarray(P, BF16)


def _heads(t, H, d):
    return t.reshape(t.shape[0], H, d).transpose(1, 0, 2)


def _unheads(t):
    H, T, d = t.shape
    return t.transpose(1, 0, 2).reshape(T, H * d)


def _dw(a, b, *, name, axis):
    K, N = a.shape[1], b.shape[1]
    if axis == 1:
        return _mm(a, b, mode="tn", name=name).reshape(N_CHIPS, K // N_CHIPS, N)
    if (N // N_CHIPS) % LANES == 0:
        return _mm(a, b, mode="tn", name=name, split=N_CHIPS)
    return _mm(a, b, mode="tn", name=name).reshape(K, N_CHIPS, N // N_CHIPS).transpose(1, 0, 2)


def _mlp_fwd(x, gain, w_up, w_down, tag):
    hm = _norm_fwd(x[None], gain, name=f"mlp{tag}_norm")[0]
    u, act = _mm(hm, w_up, mode="nn", name=f"mlp{tag}_up", epi="sqrelu")
    x_out = _mm(act, w_down, mode="nn", name=f"mlp{tag}_down", epi="add", extra=x)
    return x_out, (hm, u, act)


def _mlp_bwd(x, gain, w_up, w_down, saved, dxo, tag):
    hm, u, act = saved
    du = _mm(dxo, w_down, mode="nt", name=f"mlp{tag}_dact", epi="dsqrelu", extra=u, out_dtype=BF16)
    dw_down = _dw(act, dxo, name=f"mlp{tag}_dwdown", axis=1)
    dhm = _mm(du, w_up, mode="nt", name=f"mlp{tag}_dhm")
    dw_up = _dw(hm, du, name=f"mlp{tag}_dwup", axis=2)
    dx, dgain = _norm_bwd(x[None], gain, dhm[None], name=f"mlp{tag}_dnorm", res=dxo)
    return dx[0], dgain[0], dw_up, dw_down


def _local_step(x, target, W, small, late=None):
    T, D = x.shape
    W = dict(W)
    pos = jnp.arange(T)
    mla_cos, mla_sin = _rope_cos_sin(pos, MLA_ROPE, ROPE_THETA)
    row_cos, row_sin = _rope_cos_sin(pos // GRID_W, GQA_DIM // 2, AXIAL_THETA)
    col_cos, col_sin = _rope_cos_sin(pos % GRID_W, GQA_DIM // 2, AXIAL_THETA)
    swa_cos, swa_sin = _rope_cos_sin(pos, SWA_ROT, ROPE_THETA)
    rope_q = _rope_tables(T, MLA_QK, [(MLA_NOPE, MLA_ROPE, mla_cos, mla_sin)])
    rope_kr = _rope_tables(T, MLA_ROPE, [(0, MLA_ROPE, mla_cos, mla_sin)])
    half = GQA_DIM // 2
    rope_ax = _rope_tables(T, GQA_DIM, [(0, half, row_cos, row_sin), (half, half, col_cos, col_sin)])
    rope_sw = _rope_tables(T, SWA_DIM, [(0, SWA_ROT, swa_cos, swa_sin)])
    o1 = MLA_Q_LORA
    o2 = o1 + MLA_KV_LORA
    o3 = o2 + MLA_ROPE
    o4 = o3 + GQA_HEADS * GQA_DIM
    o5 = o4 + GQA_KV * GQA_DIM
    sc_a, sc_g, sc_s = MLA_QK ** -0.5, GQA_DIM ** -0.5, SWA_DIM ** -0.5
    kv_w = MLA_NOPE + MLA_V

    h0 = _norm_fwd(x[None], small["even_norm"], name="even_norm")[0]
    proj = _mm(h0, W["even_w_in"], mode="nn", name="even_in")
    c_q, c_kv, kr_raw = proj[:, :o1], proj[:, o1:o2], proj[:, o2:o3]
    qg_raw = _heads(proj[:, o3:o4], GQA_HEADS, GQA_DIM)
    kg_raw = _heads(proj[:, o4:o5], GQA_KV, GQA_DIM)
    vg = _heads(proj[:, o5:], GQA_KV, GQA_DIM).astype(BF16)
    cqn = _norm_fwd(c_q[None], small["mla_q_lat_norm"], name="q_lat_norm")[0]
    ckvn = _norm_fwd(c_kv[None], small["mla_kv_lat_norm"], name="kv_lat_norm")[0]
    qa_raw = _heads(_mm(cqn, W["mla_w_uq"], mode="nn", name="mla_uq"), MLA_HEADS, MLA_QK)
    kv = _mm(ckvn, W["mla_w_ukv"], mode="nn", name="mla_ukv").reshape(T, MLA_HEADS, kv_w)
    kn_raw = kv[:, :, :MLA_NOPE].transpose(1, 0, 2)
    va = kv[:, :, MLA_NOPE:].transpose(1, 0, 2).astype(BF16)
    q_a = _norm_fwd(qa_raw, small["mla_q_norm"], name="mla_q_prep", rope=rope_q, out_scale=sc_a * LOG2E)
    k_n = _norm_fwd(kn_raw, small["mla_k_nope_norm"], name="mla_kn_prep")
    k_r = _norm_fwd(kr_raw[None], small["mla_k_rope_norm"], name="mla_kr_prep", rope=rope_kr)
    k_a = jnp.concatenate([k_n, jnp.broadcast_to(k_r, (MLA_HEADS, T, MLA_ROPE))], axis=-1)
    if late is None:
        o_a, lse_a, ob_a = _flash_fwd(q_a, k_a, va, name="mla_attn")
    else:
        o_a, lse_a, ob_a, gathered = _flash_fwd(q_a, k_a, va, name="mla_attn", gather=late.halves)
        W.update(late.weights(gathered))
    q_g = _norm_fwd(qg_raw, small["gqa_q_norm"], name="gqa_q_prep", rope=rope_ax, out_scale=sc_g * LOG2E)
    k_g = _norm_fwd(kg_raw, small["gqa_k_norm"], name="gqa_k_prep", rope=rope_ax)
    o_g, lse_g, ob_g = _flash_fwd(q_g, k_g, vg, name="gqa_attn")
    merged = jnp.concatenate([_unheads(ob_a), _unheads(ob_g)], axis=-1)
    x1 = _mm(merged, W["even_w_out"], mode="nn", name="even_out", epi="add", extra=x)
    x2, mlp0 = _mlp_fwd(x1, small["mlp_norm"][0], W["mlp_w_up0"], W["mlp_w_down0"], 0)

    h1 = _norm_fwd(x2[None], small["odd_norm"], name="odd_norm")[0]
    qkv = _mm(h1, W["odd_w_qkv"], mode="nn", name="odd_qkv")
    nq, nkk = SWA_HEADS * SWA_DIM, SWA_KV * SWA_DIM
    qs_raw = _heads(qkv[:, :nq], SWA_HEADS, SWA_DIM)
    ks_raw = _heads(qkv[:, nq:nq + nkk], SWA_KV, SWA_DIM)
    vs = _heads(qkv[:, nq + nkk:], SWA_KV, SWA_DIM).astype(BF16)
    q_s = _norm_fwd(qs_raw, small["swa_q_norm"], name="swa_q_prep", rope=rope_sw)
    k_s = _norm_fwd(ks_raw, small["swa_k_norm"], name="swa_k_prep", rope=rope_sw)
    sink = small["swa_sink"].reshape(SWA_HEADS, 1, 1)
    o_s, lse_s, ob_s = _swa_fwd(q_s, k_s, vs, sink, sc_s, name="swa_attn")
    o_flat = _unheads(ob_s)
    x3 = _mm(o_flat, W["odd_w_out"], mode="nn", name="odd_out", epi="add", extra=x2)
    x4, mlp1 = _mlp_fwd(x3, small["mlp_norm"][1], W["mlp_w_up1"], W["mlp_w_down1"], 1)

    dy, loss_sum = _loss_head(x4, target, name="loss_head")
    gW, gs = {}, {}

    dx3, dg_m1, gW["mlp_w_up1"], gW["mlp_w_down1"] = _mlp_bwd(
        x3, small["mlp_norm"][1], W["mlp_w_up1"], W["mlp_w_down1"], mlp1, dy, 1)
    d_oflat = _mm(dx3, W["odd_w_out"], mode="nt", name="odd_dout")
    gW["odd_w_out"] = _dw(o_flat, dx3, name="odd_dwout", axis=1)
    do_s = _heads(d_oflat, SWA_HEADS, SWA_DIM)
    dq_s, dsink, delta_s, dob_s = _swa_dq(q_s, k_s, vs, o_s, do_s, lse_s, sink, sc_s, name="swa_dq")
    dk_s, dv_s = _swa_dkv(q_s, k_s, vs, dob_s, lse_s, delta_s, sc_s, name="swa_dkv")
    gs["swa_sink"] = dsink[:, 0, 0]
    dqs_raw, gs["swa_q_norm"] = _norm_bwd(qs_raw, small["swa_q_norm"], dq_s, name="swa_dq_prep", rope=rope_sw,
                                          out_dtype=BF16)
    dks_raw, gs["swa_k_norm"] = _norm_bwd(ks_raw, small["swa_k_norm"], dk_s, name="swa_dk_prep", rope=rope_sw,
                                          out_dtype=BF16)
    dqkv = jnp.concatenate([_unheads(dqs_raw), _unheads(dks_raw), _unheads(dv_s)], axis=-1).astype(BF16)
    dh1 = _mm(dqkv, W["odd_w_qkv"], mode="nt", name="odd_dh")
    gW["odd_w_qkv"] = _dw(h1, dqkv, name="odd_dwqkv", axis=2)
    dx2, gs["odd_norm"] = _norm_bwd(x2[None], small["odd_norm"], dh1[None], name="odd_dnorm", res=dx3)
    dx2 = dx2[0]

    dx1, dg_m0, gW["mlp_w_up0"], gW["mlp_w_down0"] = _mlp_bwd(
        x1, small["mlp_norm"][0], W["mlp_w_up0"], W["mlp_w_down0"], mlp0, dx2, 0)
    gs["mlp_norm"] = jnp.stack([dg_m0, dg_m1])
    d_merged = _mm(dx1, W["even_w_out"], mode="nt", name="even_dout")
    gW["even_w_out"] = _dw(merged, dx1, name="even_dwout", axis=1)
    na = MLA_HEADS * MLA_V
    do_a = _heads(d_merged[:, :na], MLA_HEADS, MLA_V)
    do_g = _heads(d_merged[:, na:], GQA_HEADS, GQA_DIM)
    delta_a, dob_a = _delta(o_a, do_a, name="mla_delta")
    delta_g, dob_g = _delta(o_g, do_g, name="gqa_delta")
    if late is None:
        dq_a, dk_a, dv_a = _flash_bwd(q_a, k_a, va, dob_a, lse_a, delta_a, name="mla_attn_bwd")
        dq_g, dk_gp, dv_gp = _flash_bwd(q_g, k_g, vg, dob_g, lse_g, delta_g, name="gqa_attn_bwd")
    else:
        halves = late.split(gW)
        dq_a, dk_a, dv_a, from_sibling = _flash_bwd(q_a, k_a, va, dob_a, lse_a, delta_a, name="mla_attn_bwd",
                                                    rider=("exchange", halves))
        dq_g, dk_gp, dv_gp, late.scattered = _flash_bwd(q_g, k_g, vg, dob_g, lse_g, delta_g, name="gqa_attn_bwd",
                                                        rider=("scatter", late.pairs(halves, from_sibling)))
    grp = GQA_HEADS // GQA_KV
    ln2 = 1.0 / LOG2E
    dqg_raw, gs["gqa_q_norm"] = _norm_bwd(qg_raw, small["gqa_q_norm"], dq_g, name="gqa_dq_prep", rope=rope_ax,
                                          dy_scale=sc_g, out_dtype=BF16)
    dkg_raw, gs["gqa_k_norm"] = _norm_bwd(kg_raw, small["gqa_k_norm"], dk_gp, name="gqa_dk_prep", rope=rope_ax,
                                          group=grp, dy_scale=ln2, out_dtype=BF16)
    dvg = _group_sum(dv_gp, grp, name="gqa_dv_sum", out_dtype=BF16)
    dqa_raw, gs["mla_q_norm"] = _norm_bwd(qa_raw, small["mla_q_norm"], dq_a, name="mla_dq_prep", rope=rope_q,
                                          dy_scale=sc_a, out_dtype=BF16)
    dkn_raw, gs["mla_k_nope_norm"] = _norm_bwd(kn_raw, small["mla_k_nope_norm"], dk_a[:, :, :MLA_NOPE],
                                               name="mla_dkn_prep", dy_scale=ln2, out_dtype=BF16)
    dkr_raw, gs["mla_k_rope_norm"] = _norm_bwd(kr_raw[None], small["mla_k_rope_norm"], dk_a[:, :, MLA_NOPE:],
                                               name="mla_dkr_prep", rope=rope_kr, group=MLA_HEADS, dy_scale=ln2,
                                               out_dtype=BF16)
    dkv = jnp.concatenate([dkn_raw.transpose(1, 0, 2), dv_a.astype(BF16).transpose(1, 0, 2)], axis=-1)
    dkv = dkv.reshape(T, MLA_HEADS * kv_w).astype(BF16)
    dqa = _unheads(dqa_raw).astype(BF16)
    dckvn = _mm(dkv, W["mla_w_ukv"], mode="nt", name="mla_dckv")
    gW["mla_w_ukv"] = _dw(ckvn, dkv, name="mla_dwukv", axis=2)
    dcqn = _mm(dqa, W["mla_w_uq"], mode="nt", name="mla_dcq")
    gW["mla_w_uq"] = _dw(cqn, dqa, name="mla_dwuq", axis=2)
    dc_q, gs["mla_q_lat_norm"] = _norm_bwd(c_q[None], small["mla_q_lat_norm"], dcqn[None], name="q_lat_dnorm",
                                           out_dtype=BF16)
    dc_kv, gs["mla_kv_lat_norm"] = _norm_bwd(c_kv[None], small["mla_kv_lat_norm"], dckvn[None], name="kv_lat_dnorm",
                                             out_dtype=BF16)
    dproj = jnp.concatenate([dc_q[0], dc_kv[0], dkr_raw[0], _unheads(dqg_raw), _unheads(dkg_raw), _unheads(dvg)],
                            axis=-1).astype(BF16)
    dh0 = _mm(dproj, W["even_w_in"], mode="nt", name="even_dh")
    gW["even_w_in"] = _dw(h0, dproj, name="even_dwin", axis=2)
    dx0, gs["even_norm"] = _norm_bwd(x[None], small["even_norm"], dh0[None], name="even_dnorm", res=dx1)
    gs = {k: v.reshape(-1) for k, v in gs.items()}
    return loss_sum, dx0[0], gW, gs


BIG = (("even_w_in", 0, 2), ("mla_w_uq", 0, 2), ("mla_w_ukv", 0, 2), ("even_w_out", 0, 1), ("odd_w_qkv", 0, 2),
       ("odd_w_out", 0, 1), ("mlp_w_up", 0, 2), ("mlp_w_up", 1, 2), ("mlp_w_down", 0, 1), ("mlp_w_down", 1, 1))
GATHER_FIRST = ("even_w_in", "mla_w_uq", "mla_w_ukv")
GRADS_LAST = ("even_w_in", "mla_w_uq", "mla_w_ukv")
SMALL = ("even_norm", "mla_q_lat_norm", "mla_kv_lat_norm", "mla_q_norm", "mla_k_nope_norm", "mla_k_rope_norm",
         "gqa_q_norm", "gqa_k_norm", "odd_norm", "swa_q_norm", "swa_k_norm", "swa_sink", "mlp_norm")
def _pad_to(v, n):
    return v if v.shape[-1] == n else jnp.pad(v, [(0, 0)] * (v.ndim - 1) + [(0, n - v.shape[-1])])


def _big_key(name, layer, w):
    return name if w[name].shape[0] == 1 else f"{name}{layer}"


def _pack_rows(flat, rows=8):
    n = flat.shape[0]
    padded = -(-n // (rows * LANES)) * rows * LANES
    return _pad_to(flat, padded).reshape(-1, LANES)


def kernel(x, even_norm, even_w_in, mla_q_lat_norm, mla_kv_lat_norm, mla_w_uq, mla_w_ukv, mla_q_norm, mla_k_nope_norm, mla_k_rope_norm, gqa_q_norm, gqa_k_norm, even_w_out, odd_norm, odd_w_qkv, swa_q_norm, swa_k_norm, swa_sink, odd_w_out, mlp_norm, mlp_w_up, mlp_w_down, loss_target, m_even_norm, m_even_w_in, m_mla_q_lat_norm, m_mla_kv_lat_norm, m_mla_w_uq, m_mla_w_ukv, m_mla_q_norm, m_mla_k_nope_norm, m_mla_k_rope_norm, m_gqa_q_norm, m_gqa_k_norm, m_even_w_out, m_odd_norm, m_odd_w_qkv, m_swa_q_norm, m_swa_k_norm, m_swa_sink, m_odd_w_out, m_mlp_norm, m_mlp_w_up, m_mlp_w_down, v_even_norm, v_even_w_in, v_mla_q_lat_norm, v_mla_kv_lat_norm, v_mla_w_uq, v_mla_w_ukv, v_mla_q_norm, v_mla_k_nope_norm, v_mla_k_rope_norm, v_gqa_q_norm, v_gqa_k_norm, v_even_w_out, v_odd_norm, v_odd_w_qkv, v_swa_q_norm, v_swa_k_norm, v_swa_sink, v_odd_w_out, v_mlp_norm, v_mlp_w_up, v_mlp_w_down):
    w = dict(even_norm=even_norm, even_w_in=even_w_in, mla_q_lat_norm=mla_q_lat_norm, mla_kv_lat_norm=mla_kv_lat_norm,
             mla_w_uq=mla_w_uq, mla_w_ukv=mla_w_ukv, mla_q_norm=mla_q_norm, mla_k_nope_norm=mla_k_nope_norm,
             mla_k_rope_norm=mla_k_rope_norm, gqa_q_norm=gqa_q_norm, gqa_k_norm=gqa_k_norm, even_w_out=even_w_out,
             odd_norm=odd_norm, odd_w_qkv=odd_w_qkv, swa_q_norm=swa_q_norm, swa_k_norm=swa_k_norm, swa_sink=swa_sink,
             odd_w_out=odd_w_out, mlp_norm=mlp_norm, mlp_w_up=mlp_w_up, mlp_w_down=mlp_w_down)
    m = dict(even_norm=m_even_norm, even_w_in=m_even_w_in, mla_q_lat_norm=m_mla_q_lat_norm,
             mla_kv_lat_norm=m_mla_kv_lat_norm, mla_w_uq=m_mla_w_uq, mla_w_ukv=m_mla_w_ukv, mla_q_norm=m_mla_q_norm,
             mla_k_nope_norm=m_mla_k_nope_norm, mla_k_rope_norm=m_mla_k_rope_norm, gqa_q_norm=m_gqa_q_norm,
             gqa_k_norm=m_gqa_k_norm, even_w_out=m_even_w_out, odd_norm=m_odd_norm, odd_w_qkv=m_odd_w_qkv,
             swa_q_norm=m_swa_q_norm, swa_k_norm=m_swa_k_norm, swa_sink=m_swa_sink, odd_w_out=m_odd_w_out,
             mlp_norm=m_mlp_norm, mlp_w_up=m_mlp_w_up, mlp_w_down=m_mlp_w_down)
    v = dict(even_norm=v_even_norm, even_w_in=v_even_w_in, mla_q_lat_norm=v_mla_q_lat_norm,
             mla_kv_lat_norm=v_mla_kv_lat_norm, mla_w_uq=v_mla_w_uq, mla_w_ukv=v_mla_w_ukv, mla_q_norm=v_mla_q_norm,
             mla_k_nope_norm=v_mla_k_nope_norm, mla_k_rope_norm=v_mla_k_rope_norm, gqa_q_norm=v_gqa_q_norm,
             gqa_k_norm=v_gqa_k_norm, even_w_out=v_even_w_out, odd_norm=v_odd_norm, odd_w_qkv=v_odd_w_qkv,
             swa_q_norm=v_swa_q_norm, swa_k_norm=v_swa_k_norm, swa_sink=v_swa_sink, odd_w_out=v_odd_w_out,
             mlp_norm=v_mlp_norm, mlp_w_up=v_mlp_w_up, mlp_w_down=v_mlp_w_down)
    xi, yi, ci = _position()
    chip = 2 * xi + yi
    T, D = x.shape[1], x.shape[2]

    c_idx = ci.reshape(1).astype(jnp.int32)
    key_of = lambda entry: _big_key(entry[0], entry[1], w)
    first_use = [e for e in BIG if e[0] in GATHER_FIRST]
    later_use = [e for e in BIG if e[0] not in GATHER_FIRST]
    early_grads = [e for e in BIG if e[0] not in GRADS_LAST]
    last_grads = [e for e in BIG if e[0] in GRADS_LAST]

    as_bf16 = {}

    def halves_of(entries):
        out = []
        for name, layer, _ in entries:
            layers, ks, ns = w[name].shape
            if name not in as_bf16:
                as_bf16[name] = _to_bf16(w[name].reshape(layers * ks, ns), name=f"to_bf16_{name}")
            out.append(as_bf16[name].reshape(layers, 2, ks // 2, ns)[layer])
        return out

    def weights_of(entries, gathered):
        out = {}
        for (name, layer, axis), g in zip(entries, gathered):
            ks, ns = w[name].shape[1:]
            stacked = g.reshape(N_CHIPS, ks, ns)
            if axis == 1:
                out[_big_key(name, layer, w)] = stacked.reshape(N_CHIPS * ks, ns)
            else:
                out[_big_key(name, layer, w)] = stacked.transpose(1, 0, 2).reshape(ks, N_CHIPS * ns)
        return out

    def split_halves(entries, gW):
        out = []
        for entry in entries:
            _, ks, ns = gW[key_of(entry)].shape
            out.append(gW[key_of(entry)].reshape(N_CHIPS, 2, ks // 2, ns))
        return out

    def pair_sums(entries, g_all, from_sibling):
        return [_pair_add(g, r, c_idx, name=f"grad_pair_add_{key_of(e)}") for e, g, r in zip(entries, g_all, from_sibling)]

    class _Late:
        halves = halves_of(later_use)
        scattered = None

        @staticmethod
        def weights(gathered):
            return weights_of(later_use, gathered)

        @staticmethod
        def split(gW):
            return split_halves(early_grads, gW)

        @staticmethod
        def pairs(g_all, from_sibling):
            return pair_sums(early_grads, g_all, from_sibling)

    late = _Late()
    W = weights_of(first_use, _all_gather_halves(halves_of(first_use), name="weights_all_gather"))

    odd_full = jnp.zeros((N_CHIPS, D // N_CHIPS), F32).at[chip].set(jnp.where(ci == 0, 1.0, 0.0) * w["odd_norm"][0])
    odd_full = _all_reduce_small(_pack_rows(odd_full.reshape(-1)), name="odd_norm_gather").reshape(-1)[:D]
    small = {name: w[name][0] for name in SMALL if name not in ("mlp_norm", "odd_norm")}
    small["mlp_norm"] = w["mlp_norm"]
    small["odd_norm"] = odd_full

    loss_sum, grad_x, gW, gs = _local_step(x[0], loss_target[0], W, small, late)

    loss_local = 0.5 * loss_sum.reshape(1) / D
    small_sizes = [(name, int(gs[name].shape[0])) for name in SMALL]
    ar_in = jnp.concatenate([_pad_to(loss_local, LANES)] + [gs[name] for name in SMALL])
    ar_out = _all_reduce_small(_pack_rows(ar_in), name="small_all_reduce").reshape(-1)
    loss = ar_out[0]
    g_small, off = {}, LANES
    for name, n in small_sizes:
        g_small[name] = ar_out[off:off + n]
        off += n
    shard_d = D // N_CHIPS
    g_small["odd_norm"] = lax.dynamic_slice(g_small["odd_norm"], (chip * shard_d,), (shard_d,))

    from_chips = dict(zip(map(key_of, early_grads), late.scattered))
    last_halves = split_halves(last_grads, gW)
    last_pairs = pair_sums(last_grads, last_halves, _sibling_exchange(last_halves, name="grad_sibling_exchange"))
    last_scattered = _chip_scatter(last_pairs, name="grad_chip_scatter")
    from_chips.update(zip(map(key_of, last_grads), last_scattered))
    keys = [key_of(e) for e in BIG]
    reduced = [_sum_chips(from_chips[key], c_idx, name=f"grad_chip_sum_{key}") for key in keys]
    shared = _sibling_share(reduced, name="grad_sibling_share")
    g_shards = {}
    for (name, layer, _), f in zip(BIG, shared):
        g_shards.setdefault(name, []).append(f.reshape(w[name].shape[1:]))

    grads, deltas, new_m, new_v = {}, {}, {}, {}
    for name in g_shards:
        shape = w[name].shape
        g = jnp.stack(g_shards[name])
        grads[name] = g
        two_d = (shape[0] * shape[1], shape[2])
        d_, m_, v_ = _adamw(w[name].reshape(two_d), g.reshape(two_d), m[name].reshape(two_d), v[name].reshape(two_d),
                            name=f"adamw_{name}")
        deltas[name], new_m[name], new_v[name] = d_.reshape(shape), m_.reshape(shape), v_.reshape(shape)
    pack_small = lambda d: _pack_rows(jnp.concatenate([d[name].reshape(-1) for name in SMALL]))
    for name in SMALL:
        grads[name] = g_small[name].reshape(w[name].shape)
    d_, m_, v_ = _adamw(pack_small(w), pack_small(grads), pack_small(m), pack_small(v), name="adamw_small")
    d_, m_, v_ = d_.reshape(-1), m_.reshape(-1), v_.reshape(-1)
    off = 0
    for name in SMALL:
        n = int(np.prod(w[name].shape))
        deltas[name] = d_[off:off + n].reshape(w[name].shape)
        new_m[name] = m_[off:off + n].reshape(w[name].shape)
        new_v[name] = v_[off:off + n].reshape(w[name].shape)
        off += n

    order = ("even_norm", "even_w_in", "mla_q_lat_norm", "mla_kv_lat_norm", "mla_w_uq", "mla_w_ukv", "mla_q_norm",
             "mla_k_nope_norm", "mla_k_rope_norm", "gqa_q_norm", "gqa_k_norm", "even_w_out", "odd_norm", "odd_w_qkv",
             "swa_q_norm", "swa_k_norm", "swa_sink", "odd_w_out", "mlp_norm", "mlp_w_up", "mlp_w_down")
    outs = [loss, grad_x[None]]
    for group in (grads, deltas, new_m, new_v):
        outs += [group[name] for name in order]
    return tuple(outs)
```

```python
import math

import numpy as np
import jax
import jax.numpy as jnp
from jax import lax
from jax.experimental import pallas as pl
from jax.experimental.pallas import tpu as pltpu

F32 = jnp.float32
BF16 = jnp.bfloat16
MESH = pl.DeviceIdType.MESH

VMEM_BYTES_V7X = 64 * 1024 * 1024
LANES = 128
SUBLANES_BF16 = 16

GRID_W = 64
NORM_EPS = 1e-6
ROPE_THETA = 500000.0
AXIAL_THETA = 10000.0
MLA_HEADS = 8
MLA_Q_LORA = 512
MLA_KV_LORA = 256
MLA_NOPE = 128
MLA_ROPE = 64
MLA_QK = MLA_NOPE + MLA_ROPE
MLA_V = 128
GQA_HEADS = 8
GQA_KV = 2
GQA_DIM = 128
SWA_HEADS = 32
SWA_KV = 4
SWA_DIM = 64
SWA_WINDOW = 128
SWA_ROT = SWA_DIM // 4
SWA_BLOCK = 128
SWA_HEAD_PARTS = 2
SWA_GROUPS = 4
ADAM_LR = 0.001
ADAM_B1 = 0.9
ADAM_B2 = 0.999
ADAM_EPS = 1e-08
ADAM_WD = 0.01
ADAM_STEP = 10
N_CHIPS = 4


def _tile(dim, cap, mult=LANES):
    if dim <= cap:
        return dim
    t = (cap // mult) * mult
    while t >= mult:
        if dim % t == 0:
            return t
        t -= mult
    return dim


def _params(dims, vmem_estimate):
    limit = int(min(max(vmem_estimate * 1.25 + (4 << 20), 32 << 20), VMEM_BYTES_V7X - (6 << 20)))
    return pltpu.CompilerParams(dimension_semantics=dims, vmem_limit_bytes=limit)


def _nbytes(shape, dtype):
    return int(np.prod(shape)) * jnp.dtype(dtype).itemsize


def _mm(a, b, *, mode, name, out_dtype=F32, epi=None, extra=None, split=1, caps=(1024, 1024, 2048)):
    if mode == "nn":
        (M, K), (K2, N) = a.shape, b.shape
    elif mode == "nt":
        (M, K), (N, K2) = a.shape, b.shape
    else:
        (K, M), (K2, N) = a.shape, b.shape
    assert K == K2, (a.shape, b.shape, mode)
    assert N % split == 0
    ns = N // split
    tn, tk = _tile(ns, caps[1]), _tile(K, caps[2])
    tm = _tile(M, min(caps[0], max(LANES, caps[0] * caps[1] // tn)))
    nj_per = ns // tn
    grid = (M // tm, N // tn, K // tk)
    nk = grid[2]
    if mode == "nn":
        a_spec = pl.BlockSpec((tm, tk), lambda i, j, k: (i, k))
        b_spec = pl.BlockSpec((tk, tn), lambda i, j, k: (k, j))
        dn = (((1,), (0,)), ((), ()))
    elif mode == "nt":
        a_spec = pl.BlockSpec((tm, tk), lambda i, j, k: (i, k))
        b_spec = pl.BlockSpec((tn, tk), lambda i, j, k: (j, k))
        dn = (((1,), (1,)), ((), ()))
    else:
        a_spec = pl.BlockSpec((tk, tm), lambda i, j, k: (k, i))
        b_spec = pl.BlockSpec((tk, tn), lambda i, j, k: (k, j))
        dn = (((0,), (0,)), ((), ()))
    if split == 1:
        o_spec = pl.BlockSpec((tm, tn), lambda i, j, k: (i, j))
        o_shape = (M, N)
    else:
        o_spec = pl.BlockSpec((None, tm, tn), lambda i, j, k: (j // nj_per, i, j % nj_per))
        o_shape = (split, M, ns)
    mn_spec = pl.BlockSpec((tm, tn), lambda i, j, k: (i, j))
    in_specs, args = [a_spec, b_spec], [a, b]
    if epi in ("add", "dsqrelu"):
        in_specs.append(mn_spec)
        args.append(extra)
    if epi == "sqrelu":
        out_shape = (jax.ShapeDtypeStruct(o_shape, BF16), jax.ShapeDtypeStruct(o_shape, BF16))
        out_specs = (o_spec, o_spec)
        n_out = 2
    else:
        out_shape = jax.ShapeDtypeStruct(o_shape, out_dtype)
        out_specs = o_spec
        n_out = 1

    def body(*refs):
        a_ref, b_ref = refs[0], refs[1]
        e_ref = refs[2] if len(args) == 3 else None
        outs = refs[len(args):len(args) + n_out]

        def finish(acc):
            if epi is None:
                outs[0][...] = acc.astype(outs[0].dtype)
            elif epi == "add":
                outs[0][...] = (e_ref[...] + acc).astype(outs[0].dtype)
            elif epi == "sqrelu":
                r = jnp.maximum(acc, 0.0)
                outs[0][...] = acc.astype(BF16)
                outs[1][...] = (r * r).astype(BF16)
            else:
                u = e_ref[...].astype(F32)
                outs[0][...] = (acc * (2.0 * jnp.maximum(u, 0.0))).astype(outs[0].dtype)

        prod = lax.dot_general(a_ref[...].astype(BF16), b_ref[...].astype(BF16), dn, preferred_element_type=F32)
        if nk == 1:
            finish(prod)
            return
        acc_ref = refs[-1]
        k = pl.program_id(2)

        @pl.when(k == 0)
        def _():
            acc_ref[...] = prod

        @pl.when((k != 0) & (k != nk - 1))
        def _():
            acc_ref[...] += prod

        @pl.when(k == nk - 1)
        def _():
            finish(acc_ref[...] + prod)

    est = 2 * (_nbytes((tm, tk), a.dtype) + _nbytes((tk, tn), b.dtype)) + _nbytes((tm, tn), F32)
    est += 2 * n_out * _nbytes((tm, tn), out_dtype if n_out == 1 else BF16)
    if len(args) == 3:
        est += 2 * _nbytes((tm, tn), extra.dtype)
    est += 3 * _nbytes((tm, tn), F32)
    return pl.pallas_call(
        body, name=name, grid=grid, in_specs=in_specs, out_specs=out_specs, out_shape=out_shape,
        scratch_shapes=[] if nk == 1 else [pltpu.VMEM((tm, tn), F32)],
        compiler_params=_params(("parallel", "parallel", "arbitrary"), est),
    )(*args)


def _perm(y, p):
    hi = y.astype(BF16)
    lo = (y - hi.astype(F32)).astype(BF16)
    d = lambda t: jnp.dot(t, p, preferred_element_type=F32)
    return d(hi) + d(lo)


def _rows_tile(T, d):
    return _tile(T, 2048 if d <= 256 else 512, 128)


def _norm_fwd(x, gain, *, name, rope=None, out_dtype=BF16, out_scale=None):
    H, T, d = x.shape
    tm = _rows_tile(T, d)
    g2 = gain.reshape(1, d).astype(F32)
    in_specs = [pl.BlockSpec((None, tm, d), lambda h, i: (h, i, 0)), pl.BlockSpec((1, d), lambda h, i: (0, 0))]
    args = [x, g2]
    if rope is not None:
        in_specs += [pl.BlockSpec((tm, d), lambda h, i: (i, 0)), pl.BlockSpec((tm, d), lambda h, i: (i, 0)),
                     pl.BlockSpec((d, d), lambda h, i: (0, 0))]
        args += list(rope)

    def body(*refs):
        x_ref, g_ref = refs[0], refs[1]
        o_ref = refs[-1]
        xv = x_ref[...]
        y = xv * lax.rsqrt(jnp.mean(xv * xv, axis=-1, keepdims=True) + NORM_EPS)
        y = y * g_ref[...]
        if rope is not None:
            c_ref, s_ref, p_ref = refs[2], refs[3], refs[4]
            y = y * c_ref[...] + _perm(y, p_ref[...]) * s_ref[...]
        if out_scale is not None:
            y = y * out_scale
        o_ref[...] = y.astype(o_ref.dtype)

    est = 2 * (_nbytes((tm, max(d, LANES)), F32) * (3 if rope is not None else 1) + _nbytes((tm, max(d, LANES)), out_dtype))
    est += 6 * _nbytes((tm, max(d, LANES)), F32)
    return pl.pallas_call(
        body, name=name, grid=(H, T // tm), in_specs=in_specs,
        out_specs=pl.BlockSpec((None, tm, d), lambda h, i: (h, i, 0)),
        out_shape=jax.ShapeDtypeStruct((H, T, d), out_dtype),
        compiler_params=_params(("parallel", "parallel"), est),
    )(*args)


def _norm_bwd(x, gain, dy, *, name, rope=None, group=1, res=None, out_dtype=F32, dy_scale=None):
    H, T, d = x.shape
    assert dy.shape == (H * group, T, d), (dy.shape, x.shape, group)
    tm = _rows_tile(T, d)
    g2 = gain.reshape(1, d).astype(F32)
    in_specs = [pl.BlockSpec((None, tm, d), lambda h, i: (h, i, 0)), pl.BlockSpec((1, d), lambda h, i: (0, 0)),
                pl.BlockSpec((group, tm, d), lambda h, i: (h, i, 0))]
    args = [x, g2, dy]
    if rope is not None:
        in_specs += [pl.BlockSpec((tm, d), lambda h, i: (i, 0)), pl.BlockSpec((tm, d), lambda h, i: (i, 0)),
                     pl.BlockSpec((d, d), lambda h, i: (0, 0))]
        args += list(rope)
    if res is not None:
        assert H == 1
        in_specs.append(pl.BlockSpec((tm, d), lambda h, i: (i, 0)))
        args.append(res)
    n_in = len(args)

    def body(*refs):
        x_ref, g_ref, dy_ref = refs[0], refs[1], refs[2]
        dx_ref, dg_ref = refs[n_in], refs[n_in + 1]
        first = (pl.program_id(0) == 0) & (pl.program_id(1) == 0)

        @pl.when(first)
        def _():
            dg_ref[...] = jnp.zeros_like(dg_ref)

        dyv = dy_ref[0].astype(F32)
        for g in range(1, group):
            dyv = dyv + dy_ref[g].astype(F32)
        if dy_scale is not None:
            dyv = dyv * dy_scale
        pos = 3
        if rope is not None:
            c_ref, s_ref, p_ref = refs[3], refs[4], refs[5]
            pos = 6
            dyv = dyv * c_ref[...] + _perm(dyv * s_ref[...], p_ref[...])
        xv = x_ref[...]
        r = lax.rsqrt(jnp.mean(xv * xv, axis=-1, keepdims=True) + NORM_EPS)
        xhat = xv * r
        dg_ref[...] += jnp.sum(dyv * xhat, axis=0, keepdims=True)
        dxh = dyv * g_ref[...]
        dx = r * (dxh - xhat * jnp.mean(dxh * xhat, axis=-1, keepdims=True))
        if res is not None:
            dx = dx + refs[pos][...]
        dx_ref[...] = dx.astype(dx_ref.dtype)

    wide = max(d, LANES)
    est = 2 * _nbytes((tm, wide), F32) * (2 + group + (2 if rope is not None else 0) + (1 if res is not None else 0))
    est += 8 * _nbytes((tm, wide), F32)
    return pl.pallas_call(
        body, name=name, grid=(H, T // tm), in_specs=in_specs,
        out_specs=(pl.BlockSpec((None, tm, d), lambda h, i: (h, i, 0)), pl.BlockSpec((1, d), lambda h, i: (0, 0))),
        out_shape=(jax.ShapeDtypeStruct((H, T, d), out_dtype), jax.ShapeDtypeStruct((1, d), F32)),
        compiler_params=_params(("arbitrary", "arbitrary"), est),
    )(*args)


def _group_sum(x, group, *, name, out_dtype=F32):
    HG, T, d = x.shape
    H = HG // group
    tm = _rows_tile(T, d)

    def body(x_ref, o_ref):
        acc = x_ref[0]
        for g in range(1, group):
            acc = acc + x_ref[g]
        o_ref[...] = acc.astype(o_ref.dtype)

    est = 2 * (group + 1) * _nbytes((tm, max(d, LANES)), F32)
    return pl.pallas_call(
        body, name=name, grid=(H, T // tm),
        in_specs=[pl.BlockSpec((group, tm, d), lambda h, i: (h, i, 0))],
        out_specs=pl.BlockSpec((None, tm, d), lambda h, i: (h, i, 0)),
        out_shape=jax.ShapeDtypeStruct((H, T, d), out_dtype),
        compiler_params=_params(("parallel", "parallel"), est),
    )(x)


def _delta(o, do, *, name):
    H, T, d = o.shape
    tm = _rows_tile(T, d)

    def body(o_ref, do_ref, dl_ref, dob_ref):
        dov = do_ref[...]
        dl = jnp.sum(o_ref[...] * dov, axis=-1, keepdims=True)
        dl_ref[...] = jnp.broadcast_to(dl, (tm, LANES))
        dob_ref[...] = dov.astype(BF16)

    spec = pl.BlockSpec((None, tm, d), lambda h, i: (h, i, 0))
    est = 2 * (3 * _nbytes((tm, max(d, LANES)), F32) + _nbytes((tm, LANES), F32))
    return pl.pallas_call(
        body, name=name, grid=(H, T // tm), in_specs=[spec, spec],
        out_specs=(pl.BlockSpec((None, tm, LANES), lambda h, i: (h, i, 0)), spec),
        out_shape=(jax.ShapeDtypeStruct((H, T, LANES), F32), jax.ShapeDtypeStruct((H, T, d), BF16)),
        compiler_params=_params(("parallel", "parallel"), est),
    )(o, do)


NT_DIMS = (((1,), (1,)), ((), ()))
TN_DIMS = (((0,), (0,)), ((), ()))
LOG2E = math.log2(math.e)
FLASH_CHUNK = 256
FLASH_ROW_PARTS = 4


def _flash_fwd(q, k, v, *, name, gather=()):
    H, T, dk = q.shape
    Hkv, _, dv = v.shape
    G = H // Hkv
    tq, tk = _tile(T, 1024), _tile(T, 4096)
    tp = _tile(tq, tq // FLASH_ROW_PARTS, SUBLANES_BF16)
    nk = T // tk

    n_r = len(gather)
    grid = (H, T // tq, nk)
    assert n_r == 0 or H >= 2

    def body(*refs):
        q_ref, k_ref, v_ref = refs[:3]
        o_ref, lse_ref, ob_ref = refs[3 + n_r:6 + n_r]
        m_ref, l_ref, acc_ref = refs[6 + 2 * n_r:9 + 2 * n_r]
        hi, qi, ki = pl.program_id(0), pl.program_id(1), pl.program_id(2)
        if n_r:
            ag_start, ag_forward, ag_finish = _gather_plan(refs[3:3 + n_r], refs[6 + n_r:6 + 2 * n_r],
                                                           *refs[9 + 2 * n_r:])
            pl.when((hi == 0) & (qi == 0) & (ki == 0))(ag_start)
            pl.when((hi == grid[0] - 1) & (qi == 0) & (ki == 0))(ag_forward)

        @pl.when(ki == 0)
        def _():
            m_ref[...] = jnp.full_like(m_ref, -jnp.inf)
            l_ref[...] = jnp.zeros_like(l_ref)
            acc_ref[...] = jnp.zeros_like(acc_ref)

        kv, vv = k_ref[...], v_ref[...]
        parts = [slice(part * tp, (part + 1) * tp) for part in range(tq // tp)]
        m_prev = [m_ref[rows, :] for rows in parts]
        l_prev = [l_ref[rows, :] for rows in parts]
        a_prev = [acc_ref[rows, :] for rows in parts]
        ss = [lax.dot_general(q_ref[rows, :], kv, NT_DIMS, preferred_element_type=F32) for rows in parts]
        m_new = [jnp.maximum(m, jnp.max(s, axis=-1, keepdims=True)) for m, s in zip(m_prev, ss)]
        alpha = [jnp.exp2(m - mn) for m, mn in zip(m_prev, m_new)]
        ps = [jnp.exp2(s - mn) for s, mn in zip(ss, m_new)]
        l_new = [a * l + jnp.sum(p, axis=-1, keepdims=True) for a, l, p in zip(alpha, l_prev, ps)]
        pv = [jnp.dot(p.astype(BF16), vv, preferred_element_type=F32) for p in ps]
        for rows, mn, ln, a, acc, o in zip(parts, m_new, l_new, alpha, a_prev, pv):
            m_ref[rows, :] = mn
            l_ref[rows, :] = ln
            acc_ref[rows, :] = a * acc + o

        @pl.when(ki == nk - 1)
        def _():
            l = l_ref[...]
            o = acc_ref[...] / l
            o_ref[...] = o
            ob_ref[...] = o.astype(BF16)
            lse_ref[...] = jnp.broadcast_to(m_ref[...] + jnp.log(l) * LOG2E, (tq, LANES))

        if n_r:
            pl.when((hi == grid[0] - 1) & (qi == grid[1] - 1) & (ki == grid[2] - 1))(ag_finish)

    est = 2 * (_nbytes((tq, dk), BF16) + _nbytes((tk, dk + dv), BF16) + _nbytes((tq, dv + LANES), F32))
    est += 4 * _nbytes((tq, tk), F32) + 3 * _nbytes((tq, dv + 3 * LANES), F32)
    outs = pl.pallas_call(
        body, name=name, grid=grid,
        in_specs=[pl.BlockSpec((None, tq, dk), lambda h, i, j: (h, i, 0)),
                  pl.BlockSpec((None, tk, dk), lambda h, i, j: (h // G, j, 0)),
                  pl.BlockSpec((None, tk, dv), lambda h, i, j: (h // G, j, 0))] + [HBM_SPEC] * n_r,
        out_specs=[pl.BlockSpec((None, tq, dv), lambda h, i, j: (h, i, 0)),
                   pl.BlockSpec((None, tq, LANES), lambda h, i, j: (h, i, 0)),
                   pl.BlockSpec((None, tq, dv), lambda h, i, j: (h, i, 0))] + [HBM_SPEC] * n_r,
        out_shape=[jax.ShapeDtypeStruct((H, T, dv), F32), jax.ShapeDtypeStruct((H, T, LANES), F32),
                   jax.ShapeDtypeStruct((H, T, dv), BF16)] + _gathered_shapes(gather),
        scratch_shapes=[pltpu.VMEM((tq, 1), F32), pltpu.VMEM((tq, 1), F32), pltpu.VMEM((tq, dv), F32)]
                       + (_gather_scratch(n_r) if n_r else []),
        compiler_params=_params(("arbitrary",) * 3 if n_r else ("parallel", "parallel", "arbitrary"), est),
    )(q, k, v, *gather)
    return (outs[0], outs[1], outs[2], outs[3:]) if n_r else (outs[0], outs[1], outs[2])


def _flash_bwd(q, k, v, do, lse2, delta, *, name, rider=None):
    H, T, dk = q.shape
    Hkv, _, dv = v.shape
    G = H // Hkv
    tq, tk = _tile(T, 1024), _tile(T, 2048)
    tc = _tile(tk, FLASH_CHUNK)

    kind, carried = rider if rider is not None else (None, ())
    n_r = len(carried)
    grid = (H, T // tk, T // tq)
    if kind == "scatter":
        plan, rider_scratch = _scatter_plan, _scatter_scratch(n_r)
        rider_shapes = [jax.ShapeDtypeStruct(p.shape, p.dtype) for p in carried]
    elif kind == "exchange":
        plan, rider_scratch = _exchange_plan, _exchange_scratch(n_r)
        rider_shapes = [jax.ShapeDtypeStruct((N_CHIPS,) + g.shape[2:], g.dtype) for g in carried]
    else:
        assert kind is None
        rider_scratch, rider_shapes = [], []

    def body(*refs):
        q_ref, k_ref, v_ref, do_ref, lse_ref, dl_ref = refs[:6]
        dq_ref, dk_ref, dv_ref = refs[6 + n_r:9 + n_r]
        hi, ki, qi = pl.program_id(0), pl.program_id(1), pl.program_id(2)
        if n_r:
            rider_start, rider_finish = plan(refs[6:6 + n_r], refs[9 + n_r:9 + 2 * n_r], *refs[9 + 2 * n_r:])
            pl.when((hi == 0) & (ki == 0) & (qi == 0))(rider_start)
        rows = pl.ds(pl.multiple_of(qi * tq, tq), tq)

        @pl.when(qi == 0)
        def _():
            dk_ref[...] = jnp.zeros_like(dk_ref)
            dv_ref[...] = jnp.zeros_like(dv_ref)

        @pl.when(ki == 0)
        def _():
            dq_ref[rows, :] = jnp.zeros((tq, dk), F32)

        qv, dov = q_ref[...], do_ref[...]
        lse2 = lse_ref[:, :1]
        dl = dl_ref[:, :1]
        chunks = [slice(c * tc, (c + 1) * tc) for c in range(tk // tc)]
        kcs = [k_ref[ks, :] for ks in chunks]
        vcs = [v_ref[ks, :] for ks in chunks]
        dv_old = [dv_ref[ks, :] for ks in chunks]
        dk_old = [dk_ref[ks, :] for ks in chunks]
        dq_old = dq_ref[rows, :]
        ss = [lax.dot_general(qv, kc, NT_DIMS, preferred_element_type=F32) for kc in kcs]
        dps = [lax.dot_general(dov, vc, NT_DIMS, preferred_element_type=F32) for vc in vcs]
        ps = [jnp.exp2(s - lse2) for s in ss]
        dss = [(p * (dp - dl)).astype(BF16) for p, dp in zip(ps, dps)]
        pbs = [p.astype(BF16) for p in ps]
        dvs = [lax.dot_general(pb, dov, TN_DIMS, preferred_element_type=F32) for pb in pbs]
        dks = [lax.dot_general(ds, qv, TN_DIMS, preferred_element_type=F32) for ds in dss]
        dqs = [jnp.dot(ds, kc, preferred_element_type=F32) for ds, kc in zip(dss, kcs)]
        for ks, old, new in zip(chunks, dv_old, dvs):
            dv_ref[ks, :] = old + new
        for ks, old, new in zip(chunks, dk_old, dks):
            dk_ref[ks, :] = old + new
        dq_c = dqs[0]
        for extra in dqs[1:]:
            dq_c = dq_c + extra
        dq_ref[rows, :] = dq_old + dq_c

        if n_r:
            pl.when((hi == grid[0] - 1) & (ki == grid[1] - 1) & (qi == grid[2] - 1))(rider_finish)

    est = 2 * (_nbytes((tq, dk + dv), BF16) + _nbytes((tk, dk + dv), BF16) + 2 * _nbytes((tq, LANES), F32))
    est += 2 * (_nbytes((T, dk), F32) + _nbytes((tk, dk + dv), F32)) + 10 * _nbytes((tq, tc), F32)
    outs = pl.pallas_call(
        body, name=name, grid=grid,
        in_specs=[pl.BlockSpec((None, tq, dk), lambda h, j, i: (h, i, 0)),
                  pl.BlockSpec((None, tk, dk), lambda h, j, i: (h // G, j, 0)),
                  pl.BlockSpec((None, tk, dv), lambda h, j, i: (h // G, j, 0)),
                  pl.BlockSpec((None, tq, dv), lambda h, j, i: (h, i, 0)),
                  pl.BlockSpec((None, tq, LANES), lambda h, j, i: (h, i, 0)),
                  pl.BlockSpec((None, tq, LANES), lambda h, j, i: (h, i, 0))] + [HBM_SPEC] * n_r,
        out_specs=[pl.BlockSpec((None, T, dk), lambda h, j, i: (h, 0, 0)),
                   pl.BlockSpec((None, tk, dk), lambda h, j, i: (h, j, 0)),
                   pl.BlockSpec((None, tk, dv), lambda h, j, i: (h, j, 0))] + [HBM_SPEC] * n_r,
        out_shape=[jax.ShapeDtypeStruct((H, T, dk), F32), jax.ShapeDtypeStruct((H, T, dk), F32),
                   jax.ShapeDtypeStruct((H, T, dv), F32)] + rider_shapes,
        scratch_shapes=rider_scratch,
        compiler_params=_params(("arbitrary", "arbitrary", "arbitrary"), est),
    )(q, k, v, do, lse2, delta, *carried)
    return (outs[0], outs[1], outs[2], outs[3:]) if n_r else tuple(outs)


def _swa_specs(G, d, n_blocks, lanes, gpb):
    B = SWA_BLOCK
    prev = lambda j, i: (j, jnp.maximum(i - 1, 0), 0)
    cur = lambda j, i: (j, i, 0)
    nxt = lambda j, i: (j, jnp.minimum(i + 1, n_blocks - 1), 0)
    q_specs = [pl.BlockSpec((gpb * G, B, lanes), m) for m in (prev, cur, nxt)]
    kv_specs = [pl.BlockSpec((gpb, B, d), m) for m in (prev, cur, nxt)]
    return q_specs, kv_specs, cur


def _swa_parts(G, gpb):
    gp = G // SWA_HEAD_PARTS
    return gp, [(g, slice(g * G + part * gp, g * G + (part + 1) * gp)) for g in range(gpb) for part in range(SWA_HEAD_PARTS)]


def _swa_bias(i, T):
    B = SWA_BLOCK
    row = lax.broadcasted_iota(jnp.int32, (B, 3 * B), 0)
    col = lax.broadcasted_iota(jnp.int32, (B, 3 * B), 1)
    kpos = (i - 1) * B + col
    valid = (col >= row) & (col <= row + 2 * SWA_WINDOW) & (kpos >= 0) & (kpos < T)
    return jnp.where(valid, 0.0, -jnp.inf)


def _swa_fwd(q, k, v, sink, *, name):
    Hq, T, d = q.shape
    Hkv = k.shape[0]
    G = Hq // Hkv
    B = SWA_BLOCK
    nb = T // B
    gpb = _tile(Hkv, SWA_GROUPS, 1)
    _, kv_specs, cur = _swa_specs(G, d, nb, d, gpb)

    def body(q_ref, k0, k1, k2, v0, v1, v2, sink_ref, o_ref, lse_ref, ob_ref):
        i = pl.program_id(1)
        kvs = [jnp.concatenate([k0[g], k1[g], k2[g]], axis=0) for g in range(gpb)]
        vvs = [jnp.concatenate([v0[g], v1[g], v2[g]], axis=0) for g in range(gpb)]
        bias = _swa_bias(i, T)[None]
        gp, parts = _swa_parts(G, gpb)
        sks = [sink_ref[hs] for _, hs in parts]
        ss = [lax.dot_general(q_ref[hs].reshape(gp * B, d), kvs[g], NT_DIMS, preferred_element_type=F32) for g, hs in parts]
        ss = [s.reshape(gp, B, 3 * B) + bias for s in ss]
        ms = [jnp.maximum(jnp.max(s, axis=-1, keepdims=True), sk) for s, sk in zip(ss, sks)]
        ps = [jnp.exp(s - m) for s, m in zip(ss, ms)]
        dens = [jnp.sum(p, axis=-1, keepdims=True) + jnp.exp(sk - m) for p, sk, m in zip(ps, sks, ms)]
        pns = [(p * (1.0 / den)).reshape(gp * B, 3 * B).astype(BF16) for p, den in zip(ps, dens)]
        os_ = [jnp.dot(pn, vvs[g], preferred_element_type=F32).reshape(gp, B, d) for pn, (g, _) in zip(pns, parts)]
        for (_, hs), o, m, den in zip(parts, os_, ms, dens):
            o_ref[hs] = o
            ob_ref[hs] = o.astype(BF16)
            lse_ref[hs] = jnp.broadcast_to(m + jnp.log(den), (gp, B, LANES))

    GG = gpb * G
    est = 2 * (_nbytes((GG, B, LANES), BF16) + 6 * gpb * _nbytes((B, LANES), BF16) + 2 * _nbytes((GG, B, LANES), F32))
    est += 8 * _nbytes((GG * B, 3 * B), F32)
    return pl.pallas_call(
        body, name=name, grid=(Hkv // gpb, nb),
        in_specs=[pl.BlockSpec((GG, B, d), cur)] + kv_specs + kv_specs + [pl.BlockSpec((GG, 1, 1), lambda j, i: (j, 0, 0))],
        out_specs=(pl.BlockSpec((GG, B, d), cur), pl.BlockSpec((GG, B, LANES), cur), pl.BlockSpec((GG, B, d), cur)),
        out_shape=(jax.ShapeDtypeStruct((Hq, T, d), F32), jax.ShapeDtypeStruct((Hq, T, LANES), F32),
                   jax.ShapeDtypeStruct((Hq, T, d), BF16)),
        compiler_params=_params(("parallel", "parallel"), est),
    )(q, k, k, k, v, v, v, sink)


def _swa_dq(q, k, v, o, do, lse, sink, *, name):
    Hq, T, d = q.shape
    Hkv = k.shape[0]
    G = Hq // Hkv
    B = SWA_BLOCK
    nb = T // B
    gpb = _tile(Hkv, SWA_GROUPS, 1)
    _, kv_specs, cur = _swa_specs(G, d, nb, d, gpb)

    def body(q_ref, do_ref, lse_ref, o_ref, k0, k1, k2, v0, v1, v2, sink_ref, dq_ref, dsink_ref, dl_ref, dob_ref):
        i = pl.program_id(1)
        kvs = [jnp.concatenate([k0[g], k1[g], k2[g]], axis=0) for g in range(gpb)]
        vvs = [jnp.concatenate([v0[g], v1[g], v2[g]], axis=0) for g in range(gpb)]
        bias = _swa_bias(i, T)[None]
        gp, parts = _swa_parts(G, gpb)
        lses = [lse_ref[hs, :, :1] for _, hs in parts]
        dovs = [do_ref[hs] for _, hs in parts]
        dls = [jnp.sum(o_ref[hs] * dov, axis=-1, keepdims=True) for (_, hs), dov in zip(parts, dovs)]
        dobs = [dov.astype(BF16) for dov in dovs]
        ss = [lax.dot_general(q_ref[hs].reshape(gp * B, d), kvs[g], NT_DIMS, preferred_element_type=F32) for g, hs in parts]
        dps = [lax.dot_general(dob.reshape(gp * B, d), vvs[g], NT_DIMS, preferred_element_type=F32)
               for dob, (g, _) in zip(dobs, parts)]
        ps = [jnp.exp(s.reshape(gp, B, 3 * B) + bias - lse) for s, lse in zip(ss, lses)]
        dss = [(p * (dp.reshape(gp, B, 3 * B) - dl)).reshape(gp * B, 3 * B).astype(BF16)
               for p, dp, dl in zip(ps, dps, dls)]
        dqs = [jnp.dot(ds, kvs[g], preferred_element_type=F32).reshape(gp, B, d) for ds, (g, _) in zip(dss, parts)]
        dsks = [-jnp.sum(jnp.exp(sink_ref[hs] - lse) * dl, axis=1, keepdims=True)
                for (_, hs), lse, dl in zip(parts, lses, dls)]

        @pl.when(i == 0)
        def _():
            dsink_ref[...] = jnp.zeros_like(dsink_ref)

        for (_, hs), dq, dsk, dl, dob in zip(parts, dqs, dsks, dls, dobs):
            dq_ref[hs] = dq
            dsink_ref[hs] += jnp.broadcast_to(dsk, (gp, 1, LANES))
            dl_ref[hs] = jnp.broadcast_to(dl, (gp, B, LANES))
            dob_ref[hs] = dob

    GG = gpb * G
    est = 2 * (2 * _nbytes((GG, B, LANES), BF16) + 6 * gpb * _nbytes((B, LANES), BF16) + 5 * _nbytes((GG, B, LANES), F32))
    est += 8 * _nbytes((GG * B, 3 * B), F32)
    q_spec, l_spec = pl.BlockSpec((GG, B, d), cur), pl.BlockSpec((GG, B, LANES), cur)
    return pl.pallas_call(
        body, name=name, grid=(Hkv // gpb, nb),
        in_specs=[q_spec, q_spec, l_spec, q_spec] + kv_specs + kv_specs + [pl.BlockSpec((GG, 1, 1), lambda j, i: (j, 0, 0))],
        out_specs=(q_spec, pl.BlockSpec((GG, 1, LANES), lambda j, i: (j, 0, 0)), l_spec, q_spec),
        out_shape=(jax.ShapeDtypeStruct((Hq, T, d), F32), jax.ShapeDtypeStruct((Hq, 1, LANES), F32),
                   jax.ShapeDtypeStruct((Hq, T, LANES), F32), jax.ShapeDtypeStruct((Hq, T, d), BF16)),
        compiler_params=_params(("arbitrary", "arbitrary"), est),
    )(q, do, lse, o, k, k, k, v, v, v, sink)


def _swa_dkv(q, k, v, do, lse, delta, *, name):
    Hq, T, d = q.shape
    Hkv = k.shape[0]
    G = Hq // Hkv
    B = SWA_BLOCK
    nb = T // B
    gpb = _tile(Hkv, SWA_GROUPS, 1)
    q_specs, _, cur = _swa_specs(G, d, nb, d, gpb)
    l_specs, _, _ = _swa_specs(G, d, nb, LANES, gpb)

    def body(k_ref, v_ref, q0, q1, q2, d0, d1, d2, l0, l1, l2, e0, e1, e2, dk_ref, dv_ref):
        b = pl.program_id(1)
        row = lax.broadcasted_iota(jnp.int32, (B, B), 0)
        col = lax.broadcasted_iota(jnp.int32, (B, B), 1)
        biases = []
        for part in range(3):
            qpos = (b + part - 1) * B + row
            diff = (part - 1) * B + row - col
            valid = (diff >= -SWA_WINDOW) & (diff <= SWA_WINDOW) & (qpos >= 0) & (qpos < T)
            biases.append(jnp.where(valid, 0.0, -jnp.inf)[None])
        chains = [(g, part) for g in range(gpb) for part in range(3)]
        heads = [slice(g * G, (g + 1) * G) for g, _ in chains]
        kvs = [k_ref[g] for g, _ in chains]
        vvs = [v_ref[g] for g, _ in chains]
        qvs = [(q0, q1, q2)[part][hs].reshape(G * B, d) for (_, part), hs in zip(chains, heads)]
        dovs = [(d0, d1, d2)[part][hs].reshape(G * B, d) for (_, part), hs in zip(chains, heads)]
        lses = [(l0, l1, l2)[part][hs, :, :1] for (_, part), hs in zip(chains, heads)]
        dls = [(e0, e1, e2)[part][hs, :, :1] for (_, part), hs in zip(chains, heads)]
        ss = [lax.dot_general(qv, kv, NT_DIMS, preferred_element_type=F32) for qv, kv in zip(qvs, kvs)]
        dps = [lax.dot_general(dov, vv, NT_DIMS, preferred_element_type=F32) for dov, vv in zip(dovs, vvs)]
        ps = [jnp.exp(s.reshape(G, B, B) + biases[part] - lse) for s, (_, part), lse in zip(ss, chains, lses)]
        dss = [(p * (dp.reshape(G, B, B) - dl)).reshape(G * B, B).astype(BF16) for p, dp, dl in zip(ps, dps, dls)]
        pbs = [p.reshape(G * B, B).astype(BF16) for p in ps]
        dvs = [lax.dot_general(pb, dov, TN_DIMS, preferred_element_type=F32) for pb, dov in zip(pbs, dovs)]
        dks = [lax.dot_general(ds, qv, TN_DIMS, preferred_element_type=F32) for ds, qv in zip(dss, qvs)]
        for g in range(gpb):
            dk_ref[g] = dks[3 * g] + dks[3 * g + 1] + dks[3 * g + 2]
            dv_ref[g] = (dvs[3 * g] + dvs[3 * g + 1] + dvs[3 * g + 2]).astype(BF16)

    GG = gpb * G
    est = 2 * (6 * _nbytes((GG, B, LANES), BF16) + 6 * _nbytes((GG, B, LANES), F32) + 4 * gpb * _nbytes((B, LANES), F32))
    est += 10 * _nbytes((GG * B, B), F32)
    kspec = pl.BlockSpec((gpb, B, d), cur)
    return pl.pallas_call(
        body, name=name, grid=(Hkv // gpb, nb),
        in_specs=[kspec, kspec] + q_specs + q_specs + l_specs + l_specs,
        out_specs=(kspec, kspec),
        out_shape=(jax.ShapeDtypeStruct((Hkv, T, d), F32), jax.ShapeDtypeStruct((Hkv, T, d), BF16)),
        compiler_params=_params(("parallel", "parallel"), est),
    )(k, v, q, q, q, do, do, do, lse, lse, lse, delta, delta, delta)


def _loss_head(y, target, *, name):
    T, D = y.shape
    tm = _tile(T, 512)

    def body(y_ref, t_ref, dy_ref, s_ref):
        @pl.when(pl.program_id(0) == 0)
        def _():
            s_ref[...] = jnp.zeros_like(s_ref)

        e = y_ref[...] - t_ref[...]
        dy_ref[...] = e / D
        s_ref[...] += jnp.sum(jnp.sum(e * e, axis=-1, keepdims=True), axis=0, keepdims=True)

    spec = pl.BlockSpec((tm, D), lambda i: (i, 0))
    return pl.pallas_call(
        body, name=name, grid=(T // tm,), in_specs=[spec, spec],
        out_specs=(spec, pl.BlockSpec((1, 1), lambda i: (0, 0))),
        out_shape=(jax.ShapeDtypeStruct((T, D), F32), jax.ShapeDtypeStruct((1, 1), F32)),
        compiler_params=_params(("arbitrary",), 8 * _nbytes((tm, D), F32)),
    )(y, target)


def _adamw(w, g, m, v, *, name):
    R, C = w.shape
    tr = _tile(R, max(8, (1 << 19) // max(C, LANES) // 8 * 8), 8)

    def body(w_ref, g_ref, m_ref, v_ref, d_ref, nm_ref, nv_ref):
        gv = g_ref[...]
        nm = ADAM_B1 * m_ref[...] + (1.0 - ADAM_B1) * gv
        nv = ADAM_B2 * v_ref[...] + (1.0 - ADAM_B2) * jnp.square(gv)
        m_hat = nm / (1.0 - ADAM_B1 ** ADAM_STEP)
        v_hat = nv / (1.0 - ADAM_B2 ** ADAM_STEP)
        d_ref[...] = -ADAM_LR * (m_hat / (jnp.sqrt(v_hat) + ADAM_EPS) + ADAM_WD * w_ref[...])
        nm_ref[...] = nm
        nv_ref[...] = nv

    spec = pl.BlockSpec((tr, C), lambda i: (i, 0))
    sds = jax.ShapeDtypeStruct((R, C), F32)
    return pl.pallas_call(
        body, name=name, grid=(R // tr,), in_specs=[spec] * 4, out_specs=(spec,) * 3, out_shape=(sds,) * 3,
        compiler_params=_params(("parallel",), 16 * _nbytes((tr, max(C, LANES)), F32)),
    )(w, g, m, v)


def _to_bf16(w, *, name):
    R, C = w.shape
    tr = _comm_rows_tile(R, C)

    def body(w_ref, o_ref):
        o_ref[...] = w_ref[...].astype(BF16)

    spec = pl.BlockSpec((tr, C), lambda i: (i, 0))
    return pl.pallas_call(
        body, name=name, grid=(R // tr,), in_specs=[spec], out_specs=spec, out_shape=jax.ShapeDtypeStruct((R, C), BF16),
        compiler_params=_params(("parallel",), 6 * _nbytes((tr, max(C, LANES)), F32)),
    )(w)


def _comm_rows_tile(R, L):
    return _tile(R, max(SUBLANES_BF16, (1 << 19) // L // SUBLANES_BF16 * SUBLANES_BF16), SUBLANES_BF16)


def _pair_add(g, recv, c_idx, *, name):
    _, _, R, L = g.shape
    tr = _comm_rows_tile(R, L)

    def body(c_ref, g_ref, r_ref, o_ref):
        o_ref[...] = (g_ref[...] + r_ref[...]).astype(BF16)

    grid_spec = pltpu.PrefetchScalarGridSpec(
        num_scalar_prefetch=1, grid=(N_CHIPS, R // tr),
        in_specs=[pl.BlockSpec((None, None, tr, L), lambda j, i, c: (j, c[0], i, 0)),
                  pl.BlockSpec((None, tr, L), lambda j, i, c: (j, i, 0))],
        out_specs=pl.BlockSpec((None, tr, L), lambda j, i, c: (j, i, 0)))
    return pl.pallas_call(
        body, name=name, grid_spec=grid_spec, out_shape=jax.ShapeDtypeStruct((N_CHIPS, R, L), BF16),
        compiler_params=_params(("parallel", "parallel"), 8 * _nbytes((tr, L), F32)),
    )(c_idx, g, recv)


def _sum_chips(q, c_idx, *, name):
    _, R, L = q.shape
    tr = _comm_rows_tile(R, L)

    def body(c_ref, q_ref, o_ref):
        acc = q_ref[0].astype(F32)
        for j in range(1, N_CHIPS):
            acc = acc + q_ref[j].astype(F32)
        o_ref[...] = acc

    grid_spec = pltpu.PrefetchScalarGridSpec(
        num_scalar_prefetch=1, grid=(R // tr,),
        in_specs=[pl.BlockSpec((N_CHIPS, tr, L), lambda i, c: (0, i, 0))],
        out_specs=pl.BlockSpec((None, tr, L), lambda i, c: (c[0], i, 0)))
    return pl.pallas_call(
        body, name=name, grid_spec=grid_spec, out_shape=jax.ShapeDtypeStruct((2, R, L), F32),
        compiler_params=_params(("parallel",), 10 * _nbytes((tr, L), F32)),
    )(c_idx, q)


HBM_SPEC = pl.BlockSpec(memory_space=pltpu.HBM)


def _position():
    return lax.axis_index("x"), lax.axis_index("y"), lax.axis_index("c")


def _other_chips(x, y):
    return [(1 - x, y), (x, 1 - y), (1 - x, 1 - y)]


AG_COPIES = 7


def _gather_plan(w_refs, out_refs, send_sems, recv_sems, local_sems):
    n = len(w_refs)
    x, y, c = _position()
    me, sibling = (x, y, c), (x, y, 1 - c)
    chips = _other_chips(x, y)

    def copy(i, k, block, to, src=None):
        px, py, pc = block
        slot = out_refs[i].at[4 * px + 2 * py + pc]
        return pltpu.make_async_remote_copy(
            src_ref=slot if src is None else src, dst_ref=slot, send_sem=send_sems.at[AG_COPIES * i + k],
            recv_sem=recv_sems.at[AG_COPIES * i + k], device_id=to, device_id_type=MESH)

    def local(i):
        return pltpu.make_async_copy(w_refs[i].at[c], out_refs[i].at[4 * x + 2 * y + c], local_sems.at[i])

    def first(i):
        own = w_refs[i].at[c]
        return [copy(i, 0, me, sibling, src=own)] + [copy(i, 1 + j, me, (*chip, c), src=own)
                                                     for j, chip in enumerate(chips)]

    def passed(i):
        return [copy(i, 4 + j, (*chip, c), sibling) for j, chip in enumerate(chips)]

    def start():
        for i in range(n):
            local(i).start()
            for cp in first(i):
                cp.start()

    def forward():
        for i in range(n):
            for j, chip in enumerate(chips):
                copy(i, 1 + j, (*chip, c), me).wait_recv()
                passed(i)[j].start()

    def finish():
        for i in range(n):
            copy(i, 0, sibling, me).wait_recv()
            for j, chip in enumerate(chips):
                copy(i, 4 + j, (*chip, 1 - c), me).wait_recv()
        for i in range(n):
            for cp in first(i) + passed(i):
                cp.wait_send()
            local(i).wait()

    return start, forward, finish


def _gather_scratch(n):
    return [pltpu.SemaphoreType.DMA((AG_COPIES * n,)), pltpu.SemaphoreType.DMA((AG_COPIES * n,)),
            pltpu.SemaphoreType.DMA((n,))]


def _gathered_shapes(ws):
    return [jax.ShapeDtypeStruct((2 * N_CHIPS,) + w.shape[1:], w.dtype) for w in ws]


def _all_gather_halves(ws, *, name):
    n = len(ws)

    def body(*refs):
        for step in _gather_plan(refs[:n], refs[n:2 * n], *refs[2 * n:]):
            step()

    return pl.pallas_call(
        body, name=name, in_specs=[HBM_SPEC] * n, out_specs=[HBM_SPEC] * n, out_shape=_gathered_shapes(ws),
        scratch_shapes=_gather_scratch(n),
    )(*ws)


def _exchange_plan(g_refs, out_refs, send_sems, recv_sems):
    n = len(g_refs)
    x, y, c = _position()

    def copies():
        return [pltpu.make_async_remote_copy(
            src_ref=g_refs[i].at[j, 1 - c], dst_ref=out_refs[i].at[j], send_sem=send_sems.at[N_CHIPS * i + j],
            recv_sem=recv_sems.at[N_CHIPS * i + j], device_id=(x, y, 1 - c), device_id_type=MESH)
            for i in range(n) for j in range(N_CHIPS)]

    def start():
        for cp in copies():
            cp.start()

    def finish():
        for cp in copies():
            cp.wait()

    return start, finish


def _exchange_scratch(n):
    return [pltpu.SemaphoreType.DMA((N_CHIPS * n,)), pltpu.SemaphoreType.DMA((N_CHIPS * n,))]


def _sibling_exchange(gs, *, name):
    n = len(gs)

    def body(*refs):
        for step in _exchange_plan(refs[:n], refs[n:2 * n], *refs[2 * n:]):
            step()

    return pl.pallas_call(
        body, name=name, in_specs=[HBM_SPEC] * n, out_specs=[HBM_SPEC] * n,
        out_shape=[jax.ShapeDtypeStruct((N_CHIPS,) + g.shape[2:], g.dtype) for g in gs],
        scratch_shapes=_exchange_scratch(n),
    )(*gs)


def _scatter_plan(p_refs, q_refs, send_sems, recv_sems, local_sems):
    n = len(p_refs)
    others = N_CHIPS - 1
    x, y, c = _position()
    me = 2 * x + y
    chips = _other_chips(x, y)

    def copy(i, k, chip, src_slot, dst_slot):
        return pltpu.make_async_remote_copy(
            src_ref=p_refs[i].at[src_slot], dst_ref=q_refs[i].at[dst_slot], send_sem=send_sems.at[others * i + k],
            recv_sem=recv_sems.at[others * i + k], device_id=(*chip, c), device_id_type=MESH)

    def local(i):
        return pltpu.make_async_copy(p_refs[i].at[me], q_refs[i].at[me], local_sems.at[i])

    def sends(i):
        return [copy(i, k, chip, 2 * chip[0] + chip[1], me) for k, chip in enumerate(chips)]

    def start():
        for i in range(n):
            local(i).start()
            for cp in sends(i):
                cp.start()

    def finish():
        for i in range(n):
            for k, chip in enumerate(chips):
                copy(i, k, chip, me, 2 * chip[0] + chip[1]).wait_recv()
        for i in range(n):
            for cp in sends(i):
                cp.wait_send()
            local(i).wait()

    return start, finish


def _scatter_scratch(n):
    others = N_CHIPS - 1
    return [pltpu.SemaphoreType.DMA((others * n,)), pltpu.SemaphoreType.DMA((others * n,)),
            pltpu.SemaphoreType.DMA((n,))]


def _chip_scatter(ps, *, name):
    n = len(ps)

    def body(*refs):
        for step in _scatter_plan(refs[:n], refs[n:2 * n], *refs[2 * n:]):
            step()

    return pl.pallas_call(
        body, name=name, in_specs=[HBM_SPEC] * n, out_specs=[HBM_SPEC] * n,
        out_shape=[jax.ShapeDtypeStruct(p.shape, p.dtype) for p in ps], scratch_shapes=_scatter_scratch(n),
    )(*ps)


def _sibling_share(fs, *, name):
    n = len(fs)

    def body(*refs):
        in_refs, out_refs = refs[:n], refs[n:2 * n]
        send_sems, recv_sems = refs[2 * n:]
        x, y, c = _position()

        def copy(i, half):
            return pltpu.make_async_remote_copy(
                src_ref=in_refs[i].at[half], dst_ref=out_refs[i].at[half], send_sem=send_sems.at[i],
                recv_sem=recv_sems.at[i], device_id=(x, y, 1 - c), device_id_type=MESH)

        sends = [copy(i, c) for i in range(n)]
        for cp in sends:
            cp.start()
        for i in range(n):
            copy(i, 1 - c).wait_recv()
        for cp in sends:
            cp.wait_send()

    return pl.pallas_call(
        body, name=name, in_specs=[HBM_SPEC] * n, out_specs=[HBM_SPEC] * n,
        out_shape=[jax.ShapeDtypeStruct(f.shape, f.dtype) for f in fs],
        input_output_aliases={i: i for i in range(n)},
        scratch_shapes=[pltpu.SemaphoreType.DMA((n,)), pltpu.SemaphoreType.DMA((n,))],
    )(*fs)


def _all_reduce_small(s, *, name):
    R, L = s.shape
    n_dev = 2 * N_CHIPS

    def body(s_ref, out_ref, buf, send_sems, recv_sems, local_sem):
        x, y, c = _position()
        me, sibling = (x, y, c), (x, y, 1 - c)
        chips = _other_chips(x, y)

        def slot(px, py, pc):
            return buf.at[4 * px + 2 * py + pc]

        def copy(k, block, to, src=None):
            return pltpu.make_async_remote_copy(
                src_ref=slot(*block) if src is None else src, dst_ref=slot(*block),
                send_sem=send_sems.at[k], recv_sem=recv_sems.at[k], device_id=to, device_id_type=MESH)

        mine = pltpu.make_async_copy(s_ref, slot(*me), local_sem)
        mine.start()
        first = [copy(0, me, sibling, src=s_ref)]
        first += [copy(1 + j, me, (*chip, c), src=s_ref) for j, chip in enumerate(chips)]
        for cp in first:
            cp.start()
        passed = [copy(4 + j, (*chip, c), sibling) for j, chip in enumerate(chips)]
        for j, chip in enumerate(chips):
            copy(1 + j, (*chip, c), me).wait_recv()
            passed[j].start()
        copy(0, sibling, me).wait_recv()
        for j, chip in enumerate(chips):
            copy(4 + j, (*chip, 1 - c), me).wait_recv()
        for cp in first + passed:
            cp.wait_send()
        mine.wait()
        acc = buf[0]
        for j in range(1, n_dev):
            acc = acc + buf[j]
        out_ref[...] = acc

    vmem = pl.BlockSpec(memory_space=pltpu.VMEM)
    return pl.pallas_call(
        body, name=name, in_specs=[vmem], out_specs=vmem, out_shape=jax.ShapeDtypeStruct((R, L), F32),
        scratch_shapes=[pltpu.VMEM((n_dev, R, L), F32), pltpu.SemaphoreType.DMA((7,)), pltpu.SemaphoreType.DMA((7,)),
                        pltpu.SemaphoreType.DMA],
    )(s)


def _rope_cos_sin(pos, dim, theta):
    inv = jnp.float32(theta) ** (-jnp.arange(0, dim, 2, dtype=F32) / dim)
    ang = pos.astype(F32)[:, None] * inv[None, :]
    return jnp.cos(ang), jnp.sin(ang)


def _rope_tables(T, d, segments):
    P = np.zeros((d, d), np.float32)
    c_parts, s_parts, at = [], [], 0
    for start, size, cos, sin in segments:
        half = size // 2
        if start > at:
            c_parts.append(jnp.ones((T, start - at), F32))
            s_parts.append(jnp.zeros((T, start - at), F32))
        c_parts += [cos, cos]
        s_parts += [-sin, sin]
        at = start + size
        for p in range(half):
            P[start + half + p, start + p] = 1.0
            P[start + p, start + half + p] = 1.0
    if at < d:
        c_parts.append(jnp.ones((T, d - at), F32))
        s_parts.append(jnp.zeros((T, d - at), F32))
    return jnp.concatenate(c_parts, axis=1), jnp.concatenate(s_parts, axis=1), jnp.asarray(P, BF16)


def _heads(t, H, d):
    return t.reshape(t.shape[0], H, d).transpose(1, 0, 2)


def _unheads(t):
    H, T, d = t.shape
    return t.transpose(1, 0, 2).reshape(T, H * d)


def _dw(a, b, *, name, axis):
    K, N = a.shape[1], b.shape[1]
    if axis == 1:
        return _mm(a, b, mode="tn", name=name).reshape(N_CHIPS, K // N_CHIPS, N)
    if (N // N_CHIPS) % LANES == 0:
        return _mm(a, b, mode="tn", name=name, split=N_CHIPS)
    return _mm(a, b, mode="tn", name=name).reshape(K, N_CHIPS, N // N_CHIPS).transpose(1, 0, 2)


def _mlp_fwd(x, gain, w_up, w_down, tag):
    hm = _norm_fwd(x[None], gain, name=f"mlp{tag}_norm")[0]
    u, act = _mm(hm, w_up, mode="nn", name=f"mlp{tag}_up", epi="sqrelu")
    x_out = _mm(act, w_down, mode="nn", name=f"mlp{tag}_down", epi="add", extra=x)
    return x_out, (hm, u, act)


def _mlp_bwd(x, gain, w_up, w_down, saved, dxo, tag):
    hm, u, act = saved
    du = _mm(dxo, w_down, mode="nt", name=f"mlp{tag}_dact", epi="dsqrelu", extra=u, out_dtype=BF16)
    dw_down = _dw(act, dxo, name=f"mlp{tag}_dwdown", axis=1)
    dhm = _mm(du, w_up, mode="nt", name=f"mlp{tag}_dhm")
    dw_up = _dw(hm, du, name=f"mlp{tag}_dwup", axis=2)
    dx, dgain = _norm_bwd(x[None], gain, dhm[None], name=f"mlp{tag}_dnorm", res=dxo)
    return dx[0], dgain[0], dw_up, dw_down


def _local_step(x, target, W, small, late=None):
    T, D = x.shape
    W = dict(W)
    pos = jnp.arange(T)
    mla_cos, mla_sin = _rope_cos_sin(pos, MLA_ROPE, ROPE_THETA)
    row_cos, row_sin = _rope_cos_sin(pos // GRID_W, GQA_DIM // 2, AXIAL_THETA)
    col_cos, col_sin = _rope_cos_sin(pos % GRID_W, GQA_DIM // 2, AXIAL_THETA)
    swa_cos, swa_sin = _rope_cos_sin(pos, SWA_ROT, ROPE_THETA)
    rope_q = _rope_tables(T, MLA_QK, [(MLA_NOPE, MLA_ROPE, mla_cos, mla_sin)])
    rope_kr = _rope_tables(T, MLA_ROPE, [(0, MLA_ROPE, mla_cos, mla_sin)])
    half = GQA_DIM // 2
    rope_ax = _rope_tables(T, GQA_DIM, [(0, half, row_cos, row_sin), (half, half, col_cos, col_sin)])
    rope_sw = _rope_tables(T, SWA_DIM, [(0, SWA_ROT, swa_cos, swa_sin)])
    o1 = MLA_Q_LORA
    o2 = o1 + MLA_KV_LORA
    o3 = o2 + MLA_ROPE
    o4 = o3 + GQA_HEADS * GQA_DIM
    o5 = o4 + GQA_KV * GQA_DIM
    sc_a, sc_g, sc_s = MLA_QK ** -0.5, GQA_DIM ** -0.5, SWA_DIM ** -0.5
    kv_w = MLA_NOPE + MLA_V

    h0 = _norm_fwd(x[None], small["even_norm"], name="even_norm")[0]
    proj = _mm(h0, W["even_w_in"], mode="nn", name="even_in")
    c_q, c_kv, kr_raw = proj[:, :o1], proj[:, o1:o2], proj[:, o2:o3]
    qg_raw = _heads(proj[:, o3:o4], GQA_HEADS, GQA_DIM)
    kg_raw = _heads(proj[:, o4:o5], GQA_KV, GQA_DIM)
    vg = _heads(proj[:, o5:], GQA_KV, GQA_DIM).astype(BF16)
    cqn = _norm_fwd(c_q[None], small["mla_q_lat_norm"], name="q_lat_norm")[0]
    ckvn = _norm_fwd(c_kv[None], small["mla_kv_lat_norm"], name="kv_lat_norm")[0]
    qa_raw = _heads(_mm(cqn, W["mla_w_uq"], mode="nn", name="mla_uq"), MLA_HEADS, MLA_QK)
    kv = _mm(ckvn, W["mla_w_ukv"], mode="nn", name="mla_ukv").reshape(T, MLA_HEADS, kv_w)
    kn_raw = kv[:, :, :MLA_NOPE].transpose(1, 0, 2)
    va = kv[:, :, MLA_NOPE:].transpose(1, 0, 2).astype(BF16)
    q_a = _norm_fwd(qa_raw, small["mla_q_norm"], name="mla_q_prep", rope=rope_q, out_scale=sc_a * LOG2E)
    k_n = _norm_fwd(kn_raw, small["mla_k_nope_norm"], name="mla_kn_prep")
    k_r = _norm_fwd(kr_raw[None], small["mla_k_rope_norm"], name="mla_kr_prep", rope=rope_kr)
    k_a = jnp.concatenate([k_n, jnp.broadcast_to(k_r, (MLA_HEADS, T, MLA_ROPE))], axis=-1)
    if late is None:
        o_a, lse_a, ob_a = _flash_fwd(q_a, k_a, va, name="mla_attn")
    else:
        o_a, lse_a, ob_a, gathered = _flash_fwd(q_a, k_a, va, name="mla_attn", gather=late.halves)
        W.update(late.weights(gathered))
    q_g = _norm_fwd(qg_raw, small["gqa_q_norm"], name="gqa_q_prep", rope=rope_ax, out_scale=sc_g * LOG2E)
    k_g = _norm_fwd(kg_raw, small["gqa_k_norm"], name="gqa_k_prep", rope=rope_ax)
    o_g, lse_g, ob_g = _flash_fwd(q_g, k_g, vg, name="gqa_attn")
    merged = jnp.concatenate([_unheads(ob_a), _unheads(ob_g)], axis=-1)
    x1 = _mm(merged, W["even_w_out"], mode="nn", name="even_out", epi="add", extra=x)
    x2, mlp0 = _mlp_fwd(x1, small["mlp_norm"][0], W["mlp_w_up0"], W["mlp_w_down0"], 0)

    h1 = _norm_fwd(x2[None], small["odd_norm"], name="odd_norm")[0]
    qkv = _mm(h1, W["odd_w_qkv"], mode="nn", name="odd_qkv")
    nq, nkk = SWA_HEADS * SWA_DIM, SWA_KV * SWA_DIM
    qs_raw = _heads(qkv[:, :nq], SWA_HEADS, SWA_DIM)
    ks_raw = _heads(qkv[:, nq:nq + nkk], SWA_KV, SWA_DIM)
    vs = _heads(qkv[:, nq + nkk:], SWA_KV, SWA_DIM).astype(BF16)
    q_s = _norm_fwd(qs_raw, small["swa_q_norm"], name="swa_q_prep", rope=rope_sw, out_scale=sc_s)
    k_s = _norm_fwd(ks_raw, small["swa_k_norm"], name="swa_k_prep", rope=rope_sw)
    sink = small["swa_sink"].reshape(SWA_HEADS, 1, 1)
    o_s, lse_s, ob_s = _swa_fwd(q_s, k_s, vs, sink, name="swa_attn")
    o_flat = _unheads(ob_s)
    x3 = _mm(o_flat, W["odd_w_out"], mode="nn", name="odd_out", epi="add", extra=x2)
    x4, mlp1 = _mlp_fwd(x3, small["mlp_norm"][1], W["mlp_w_up1"], W["mlp_w_down1"], 1)

    dy, loss_sum = _loss_head(x4, target, name="loss_head")
    gW, gs = {}, {}

    dx3, dg_m1, gW["mlp_w_up1"], gW["mlp_w_down1"] = _mlp_bwd(
        x3, small["mlp_norm"][1], W["mlp_w_up1"], W["mlp_w_down1"], mlp1, dy, 1)
    d_oflat = _mm(dx3, W["odd_w_out"], mode="nt", name="odd_dout")
    gW["odd_w_out"] = _dw(o_flat, dx3, name="odd_dwout", axis=1)
    do_s = _heads(d_oflat, SWA_HEADS, SWA_DIM)
    dq_s, dsink, delta_s, dob_s = _swa_dq(q_s, k_s, vs, o_s, do_s, lse_s, sink, name="swa_dq")
    dk_s, dv_s = _swa_dkv(q_s, k_s, vs, dob_s, lse_s, delta_s, name="swa_dkv")
    gs["swa_sink"] = dsink[:, 0, 0]
    dqs_raw, gs["swa_q_norm"] = _norm_bwd(qs_raw, small["swa_q_norm"], dq_s, name="swa_dq_prep", rope=rope_sw,
                                          out_dtype=BF16, dy_scale=sc_s)
    dks_raw, gs["swa_k_norm"] = _norm_bwd(ks_raw, small["swa_k_norm"], dk_s, name="swa_dk_prep", rope=rope_sw,
                                          out_dtype=BF16)
    dqkv = jnp.concatenate([_unheads(dqs_raw), _unheads(dks_raw), _unheads(dv_s)], axis=-1).astype(BF16)
    dh1 = _mm(dqkv, W["odd_w_qkv"], mode="nt", name="odd_dh")
    gW["odd_w_qkv"] = _dw(h1, dqkv, name="odd_dwqkv", axis=2)
    dx2, gs["odd_norm"] = _norm_bwd(x2[None], small["odd_norm"], dh1[None], name="odd_dnorm", res=dx3)
    dx2 = dx2[0]

    dx1, dg_m0, gW["mlp_w_up0"], gW["mlp_w_down0"] = _mlp_bwd(
        x1, small["mlp_norm"][0], W["mlp_w_up0"], W["mlp_w_down0"], mlp0, dx2, 0)
    gs["mlp_norm"] = jnp.stack([dg_m0, dg_m1])
    d_merged = _mm(dx1, W["even_w_out"], mode="nt", name="even_dout")
    gW["even_w_out"] = _dw(merged, dx1, name="even_dwout", axis=1)
    na = MLA_HEADS * MLA_V
    do_a = _heads(d_merged[:, :na], MLA_HEADS, MLA_V)
    do_g = _heads(d_merged[:, na:], GQA_HEADS, GQA_DIM)
    delta_a, dob_a = _delta(o_a, do_a, name="mla_delta")
    delta_g, dob_g = _delta(o_g, do_g, name="gqa_delta")
    if late is None:
        dq_a, dk_a, dv_a = _flash_bwd(q_a, k_a, va, dob_a, lse_a, delta_a, name="mla_attn_bwd")
        dq_g, dk_gp, dv_gp = _flash_bwd(q_g, k_g, vg, dob_g, lse_g, delta_g, name="gqa_attn_bwd")
    else:
        halves = late.split(gW)
        dq_a, dk_a, dv_a, from_sibling = _flash_bwd(q_a, k_a, va, dob_a, lse_a, delta_a, name="mla_attn_bwd",
                                                    rider=("exchange", halves))
        dq_g, dk_gp, dv_gp, late.scattered = _flash_bwd(q_g, k_g, vg, dob_g, lse_g, delta_g, name="gqa_attn_bwd",
                                                        rider=("scatter", late.pairs(halves, from_sibling)))
    grp = GQA_HEADS // GQA_KV
    ln2 = 1.0 / LOG2E
    dqg_raw, gs["gqa_q_norm"] = _norm_bwd(qg_raw, small["gqa_q_norm"], dq_g, name="gqa_dq_prep", rope=rope_ax,
                                          dy_scale=sc_g, out_dtype=BF16)
    dkg_raw, gs["gqa_k_norm"] = _norm_bwd(kg_raw, small["gqa_k_norm"], dk_gp, name="gqa_dk_prep", rope=rope_ax,
                                          group=grp, dy_scale=ln2, out_dtype=BF16)
    dvg = _group_sum(dv_gp, grp, name="gqa_dv_sum", out_dtype=BF16)
    dqa_raw, gs["mla_q_norm"] = _norm_bwd(qa_raw, small["mla_q_norm"], dq_a, name="mla_dq_prep", rope=rope_q,
                                          dy_scale=sc_a, out_dtype=BF16)
    dkn_raw, gs["mla_k_nope_norm"] = _norm_bwd(kn_raw, small["mla_k_nope_norm"], dk_a[:, :, :MLA_NOPE],
                                               name="mla_dkn_prep", dy_scale=ln2, out_dtype=BF16)
    dkr_raw, gs["mla_k_rope_norm"] = _norm_bwd(kr_raw[None], small["mla_k_rope_norm"], dk_a[:, :, MLA_NOPE:],
                                               name="mla_dkr_prep", rope=rope_kr, group=MLA_HEADS, dy_scale=ln2,
                                               out_dtype=BF16)
    dkv = jnp.concatenate([dkn_raw.transpose(1, 0, 2), dv_a.astype(BF16).transpose(1, 0, 2)], axis=-1)
    dkv = dkv.reshape(T, MLA_HEADS * kv_w).astype(BF16)
    dqa = _unheads(dqa_raw).astype(BF16)
    dckvn = _mm(dkv, W["mla_w_ukv"], mode="nt", name="mla_dckv")
    gW["mla_w_ukv"] = _dw(ckvn, dkv, name="mla_dwukv", axis=2)
    dcqn = _mm(dqa, W["mla_w_uq"], mode="nt", name="mla_dcq")
    gW["mla_w_uq"] = _dw(cqn, dqa, name="mla_dwuq", axis=2)
    dc_q, gs["mla_q_lat_norm"] = _norm_bwd(c_q[None], small["mla_q_lat_norm"], dcqn[None], name="q_lat_dnorm",
                                           out_dtype=BF16)
    dc_kv, gs["mla_kv_lat_norm"] = _norm_bwd(c_kv[None], small["mla_kv_lat_norm"], dckvn[None], name="kv_lat_dnorm",
                                             out_dtype=BF16)
    dproj = jnp.concatenate([dc_q[0], dc_kv[0], dkr_raw[0], _unheads(dqg_raw), _unheads(dkg_raw), _unheads(dvg)],
                            axis=-1).astype(BF16)
    dh0 = _mm(dproj, W["even_w_in"], mode="nt", name="even_dh")
    gW["even_w_in"] = _dw(h0, dproj, name="even_dwin", axis=2)
    dx0, gs["even_norm"] = _norm_bwd(x[None], small["even_norm"], dh0[None], name="even_dnorm", res=dx1)
    gs = {k: v.reshape(-1) for k, v in gs.items()}
    return loss_sum, dx0[0], gW, gs


BIG = (("even_w_in", 0, 2), ("mla_w_uq", 0, 2), ("mla_w_ukv", 0, 2), ("even_w_out", 0, 1), ("odd_w_qkv", 0, 2),
       ("odd_w_out", 0, 1), ("mlp_w_up", 0, 2), ("mlp_w_up", 1, 2), ("mlp_w_down", 0, 1), ("mlp_w_down", 1, 1))
GATHER_FIRST = ("even_w_in", "mla_w_uq", "mla_w_ukv")
GRADS_LAST = ("even_w_in", "mla_w_uq", "mla_w_ukv")
SMALL = ("even_norm", "mla_q_lat_norm", "mla_kv_lat_norm", "mla_q_norm", "mla_k_nope_norm", "mla_k_rope_norm",
         "gqa_q_norm", "gqa_k_norm", "odd_norm", "swa_q_norm", "swa_k_norm", "swa_sink", "mlp_norm")
def _pad_to(v, n):
    return v if v.shape[-1] == n else jnp.pad(v, [(0, 0)] * (v.ndim - 1) + [(0, n - v.shape[-1])])


def _big_key(name, layer, w):
    return name if w[name].shape[0] == 1 else f"{name}{layer}"


def _pack_rows(flat, rows=8):
    n = flat.shape[0]
    padded = -(-n // (rows * LANES)) * rows * LANES
    return _pad_to(flat, padded).reshape(-1, LANES)


def kernel(x, even_norm, even_w_in, mla_q_lat_norm, mla_kv_lat_norm, mla_w_uq, mla_w_ukv, mla_q_norm, mla_k_nope_norm, mla_k_rope_norm, gqa_q_norm, gqa_k_norm, even_w_out, odd_norm, odd_w_qkv, swa_q_norm, swa_k_norm, swa_sink, odd_w_out, mlp_norm, mlp_w_up, mlp_w_down, loss_target, m_even_norm, m_even_w_in, m_mla_q_lat_norm, m_mla_kv_lat_norm, m_mla_w_uq, m_mla_w_ukv, m_mla_q_norm, m_mla_k_nope_norm, m_mla_k_rope_norm, m_gqa_q_norm, m_gqa_k_norm, m_even_w_out, m_odd_norm, m_odd_w_qkv, m_swa_q_norm, m_swa_k_norm, m_swa_sink, m_odd_w_out, m_mlp_norm, m_mlp_w_up, m_mlp_w_down, v_even_norm, v_even_w_in, v_mla_q_lat_norm, v_mla_kv_lat_norm, v_mla_w_uq, v_mla_w_ukv, v_mla_q_norm, v_mla_k_nope_norm, v_mla_k_rope_norm, v_gqa_q_norm, v_gqa_k_norm, v_even_w_out, v_odd_norm, v_odd_w_qkv, v_swa_q_norm, v_swa_k_norm, v_swa_sink, v_odd_w_out, v_mlp_norm, v_mlp_w_up, v_mlp_w_down):
    w = dict(even_norm=even_norm, even_w_in=even_w_in, mla_q_lat_norm=mla_q_lat_norm, mla_kv_lat_norm=mla_kv_lat_norm,
             mla_w_uq=mla_w_uq, mla_w_ukv=mla_w_ukv, mla_q_norm=mla_q_norm, mla_k_nope_norm=mla_k_nope_norm,
             mla_k_rope_norm=mla_k_rope_norm, gqa_q_norm=gqa_q_norm, gqa_k_norm=gqa_k_norm, even_w_out=even_w_out,
             odd_norm=odd_norm, odd_w_qkv=odd_w_qkv, swa_q_norm=swa_q_norm, swa_k_norm=swa_k_norm, swa_sink=swa_sink,
             odd_w_out=odd_w_out, mlp_norm=mlp_norm, mlp_w_up=mlp_w_up, mlp_w_down=mlp_w_down)
    m = dict(even_norm=m_even_norm, even_w_in=m_even_w_in, mla_q_lat_norm=m_mla_q_lat_norm,
             mla_kv_lat_norm=m_mla_kv_lat_norm, mla_w_uq=m_mla_w_uq, mla_w_ukv=m_mla_w_ukv, mla_q_norm=m_mla_q_norm,
             mla_k_nope_norm=m_mla_k_nope_norm, mla_k_rope_norm=m_mla_k_rope_norm, gqa_q_norm=m_gqa_q_norm,
             gqa_k_norm=m_gqa_k_norm, even_w_out=m_even_w_out, odd_norm=m_odd_norm, odd_w_qkv=m_odd_w_qkv,
             swa_q_norm=m_swa_q_norm, swa_k_norm=m_swa_k_norm, swa_sink=m_swa_sink, odd_w_out=m_odd_w_out,
             mlp_norm=m_mlp_norm, mlp_w_up=m_mlp_w_up, mlp_w_down=m_mlp_w_down)
    v = dict(even_norm=v_even_norm, even_w_in=v_even_w_in, mla_q_lat_norm=v_mla_q_lat_norm,
             mla_kv_lat_norm=v_mla_kv_lat_norm, mla_w_uq=v_mla_w_uq, mla_w_ukv=v_mla_w_ukv, mla_q_norm=v_mla_q_norm,
             mla_k_nope_norm=v_mla_k_nope_norm, mla_k_rope_norm=v_mla_k_rope_norm, gqa_q_norm=v_gqa_q_norm,
             gqa_k_norm=v_gqa_k_norm, even_w_out=v_even_w_out, odd_norm=v_odd_norm, odd_w_qkv=v_odd_w_qkv,
             swa_q_norm=v_swa_q_norm, swa_k_norm=v_swa_k_norm, swa_sink=v_swa_sink, odd_w_out=v_odd_w_out,
             mlp_norm=v_mlp_norm, mlp_w_up=v_mlp_w_up, mlp_w_down=v_mlp_w_down)
    xi, yi, ci = _position()
    chip = 2 * xi + yi
    T, D = x.shape[1], x.shape[2]

    c_idx = ci.reshape(1).astype(jnp.int32)
    key_of = lambda entry: _big_key(entry[0], entry[1], w)
    first_use = [e for e in BIG if e[0] in GATHER_FIRST]
    later_use = [e for e in BIG if e[0] not in GATHER_FIRST]
    early_grads = [e for e in BIG if e[0] not in GRADS_LAST]
    last_grads = [e for e in BIG if e[0] in GRADS_LAST]

    as_bf16 = {}

    def halves_of(entries):
        out = []
        for name, layer, _ in entries:
            layers, ks, ns = w[name].shape
            if name not in as_bf16:
                as_bf16[name] = _to_bf16(w[name].reshape(layers * ks, ns), name=f"to_bf16_{name}")
            out.append(as_bf16[name].reshape(layers, 2, ks // 2, ns)[layer])
        return out

    def weights_of(entries, gathered):
        out = {}
        for (name, layer, axis), g in zip(entries, gathered):
            ks, ns = w[name].shape[1:]
            stacked = g.reshape(N_CHIPS, ks, ns)
            if axis == 1:
                out[_big_key(name, layer, w)] = stacked.reshape(N_CHIPS * ks, ns)
            else:
                out[_big_key(name, layer, w)] = stacked.transpose(1, 0, 2).reshape(ks, N_CHIPS * ns)
        return out

    def split_halves(entries, gW):
        out = []
        for entry in entries:
            _, ks, ns = gW[key_of(entry)].shape
            out.append(gW[key_of(entry)].reshape(N_CHIPS, 2, ks // 2, ns))
        return out

    def pair_sums(entries, g_all, from_sibling):
        return [_pair_add(g, r, c_idx, name=f"grad_pair_add_{key_of(e)}") for e, g, r in zip(entries, g_all, from_sibling)]

    class _Late:
        halves = halves_of(later_use)
        scattered = None

        @staticmethod
        def weights(gathered):
            return weights_of(later_use, gathered)

        @staticmethod
        def split(gW):
            return split_halves(early_grads, gW)

        @staticmethod
        def pairs(g_all, from_sibling):
            return pair_sums(early_grads, g_all, from_sibling)

    late = _Late()
    W = weights_of(first_use, _all_gather_halves(halves_of(first_use), name="weights_all_gather"))

    odd_full = jnp.zeros((N_CHIPS, D // N_CHIPS), F32).at[chip].set(jnp.where(ci == 0, 1.0, 0.0) * w["odd_norm"][0])
    odd_full = _all_reduce_small(_pack_rows(odd_full.reshape(-1)), name="odd_norm_gather").reshape(-1)[:D]
    small = {name: w[name][0] for name in SMALL if name not in ("mlp_norm", "odd_norm")}
    small["mlp_norm"] = w["mlp_norm"]
    small["odd_norm"] = odd_full

    loss_sum, grad_x, gW, gs = _local_step(x[0], loss_target[0], W, small, late)

    loss_local = 0.5 * loss_sum.reshape(1) / D
    small_sizes = [(name, int(gs[name].shape[0])) for name in SMALL]
    ar_in = jnp.concatenate([_pad_to(loss_local, LANES)] + [gs[name] for name in SMALL])
    ar_out = _all_reduce_small(_pack_rows(ar_in), name="small_all_reduce").reshape(-1)
    loss = ar_out[0]
    g_small, off = {}, LANES
    for name, n in small_sizes:
        g_small[name] = ar_out[off:off + n]
        off += n
    shard_d = D // N_CHIPS
    g_small["odd_norm"] = lax.dynamic_slice(g_small["odd_norm"], (chip * shard_d,), (shard_d,))

    from_chips = dict(zip(map(key_of, early_grads), late.scattered))
    last_halves = split_halves(last_grads, gW)
    last_pairs = pair_sums(last_grads, last_halves, _sibling_exchange(last_halves, name="grad_sibling_exchange"))
    last_scattered = _chip_scatter(last_pairs, name="grad_chip_scatter")
    from_chips.update(zip(map(key_of, last_grads), last_scattered))
    keys = [key_of(e) for e in BIG]
    reduced = [_sum_chips(from_chips[key], c_idx, name=f"grad_chip_sum_{key}") for key in keys]
    shared = _sibling_share(reduced, name="grad_sibling_share")
    g_shards = {}
    for (name, layer, _), f in zip(BIG, shared):
        g_shards.setdefault(name, []).append(f.reshape(w[name].shape[1:]))

    grads, deltas, new_m, new_v = {}, {}, {}, {}
    for name in g_shards:
        shape = w[name].shape
        g = jnp.stack(g_shards[name])
        grads[name] = g
        two_d = (shape[0] * shape[1], shape[2])
        d_, m_, v_ = _adamw(w[name].reshape(two_d), g.reshape(two_d), m[name].reshape(two_d), v[name].reshape(two_d),
                            name=f"adamw_{name}")
        deltas[name], new_m[name], new_v[name] = d_.reshape(shape), m_.reshape(shape), v_.reshape(shape)
    pack_small = lambda d: _pack_rows(jnp.concatenate([d[name].reshape(-1) for name in SMALL]))
    for name in SMALL:
        grads[name] = g_small[name].reshape(w[name].shape)
    d_, m_, v_ = _adamw(pack_small(w), pack_small(grads), pack_small(m), pack_small(v), name="adamw_small")
    d_, m_, v_ = d_.reshape(-1), m_.reshape(-1), v_.reshape(-1)
    off = 0
    for name in SMALL:
        n = int(np.prod(w[name].shape))
        deltas[name] = d_[off:off + n].reshape(w[name].shape)
        new_m[name] = m_[off:off + n].reshape(w[name].shape)
        new_v[name] = v_[off:off + n].reshape(w[name].shape)
        off += n

    order = ("even_norm", "even_w_in", "mla_q_lat_norm", "mla_kv_lat_norm", "mla_w_uq", "mla_w_ukv", "mla_q_norm",
             "mla_k_nope_norm", "mla_k_rope_norm", "gqa_q_norm", "gqa_k_norm", "even_w_out", "odd_norm", "odd_w_qkv",
             "swa_q_norm", "swa_k_norm", "swa_sink", "odd_w_out", "mlp_norm", "mlp_w_up", "mlp_w_down")
    outs = [loss, grad_x[None]]
    for group in (grads, deltas, new_m, new_v):
        outs += [group[name] for name in order]
    return tuple(outs)
```

```python
import math

import numpy as np
import jax
import jax.numpy as jnp
from jax import lax
from jax.experimental import pallas as pl
from jax.experimental.pallas import tpu as pltpu

F32 = jnp.float32
BF16 = jnp.bfloat16
MESH = pl.DeviceIdType.MESH

VMEM_BYTES_V7X = 64 * 1024 * 1024
LANES = 128
SUBLANES_BF16 = 16

GRID_W = 64
NORM_EPS = 1e-6
ROPE_THETA = 500000.0
AXIAL_THETA = 10000.0
MLA_HEADS = 8
MLA_Q_LORA = 512
MLA_KV_LORA = 256
MLA_NOPE = 128
MLA_ROPE = 64
MLA_QK = MLA_NOPE + MLA_ROPE
MLA_V = 128
GQA_HEADS = 8
GQA_KV = 2
GQA_DIM = 128
SWA_HEADS = 32
SWA_KV = 4
SWA_DIM = 64
SWA_WINDOW = 128
SWA_ROT = SWA_DIM // 4
SWA_BLOCK = 128
SWA_HEAD_PARTS = 2
SWA_GROUPS = 4
ADAM_LR = 0.001
ADAM_B1 = 0.9
ADAM_B2 = 0.999
ADAM_EPS = 1e-08
ADAM_WD = 0.01
ADAM_STEP = 10
N_CHIPS = 4


def _tile(dim, cap, mult=LANES):
    if dim <= cap:
        return dim
    t = (cap // mult) * mult
    while t >= mult:
        if dim % t == 0:
            return t
        t -= mult
    return dim


def _params(dims, vmem_estimate):
    limit = int(min(max(vmem_estimate * 1.25 + (4 << 20), 32 << 20), VMEM_BYTES_V7X - (6 << 20)))
    return pltpu.CompilerParams(dimension_semantics=dims, vmem_limit_bytes=limit)


def _nbytes(shape, dtype):
    return int(np.prod(shape)) * jnp.dtype(dtype).itemsize


def _mm(a, b, *, mode, name, out_dtype=F32, epi=None, extra=None, split=1, caps=(1024, 1024, 2048)):
    if mode == "nn":
        (M, K), (K2, N) = a.shape, b.shape
    elif mode == "nt":
        (M, K), (N, K2) = a.shape, b.shape
    else:
        (K, M), (K2, N) = a.shape, b.shape
    assert K == K2, (a.shape, b.shape, mode)
    assert N % split == 0
    ns = N // split
    tn, tk = _tile(ns, caps[1]), _tile(K, caps[2])
    tm = _tile(M, min(caps[0], max(LANES, caps[0] * caps[1] // tn)))
    nj_per = ns // tn
    grid = (M // tm, N // tn, K // tk)
    nk = grid[2]
    if mode == "nn":
        a_spec = pl.BlockSpec((tm, tk), lambda i, j, k: (i, k))
        b_spec = pl.BlockSpec((tk, tn), lambda i, j, k: (k, j))
        dn = (((1,), (0,)), ((), ()))
    elif mode == "nt":
        a_spec = pl.BlockSpec((tm, tk), lambda i, j, k: (i, k))
        b_spec = pl.BlockSpec((tn, tk), lambda i, j, k: (j, k))
        dn = (((1,), (1,)), ((), ()))
    else:
        a_spec = pl.BlockSpec((tk, tm), lambda i, j, k: (k, i))
        b_spec = pl.BlockSpec((tk, tn), lambda i, j, k: (k, j))
        dn = (((0,), (0,)), ((), ()))
    if split == 1:
        o_spec = pl.BlockSpec((tm, tn), lambda i, j, k: (i, j))
        o_shape = (M, N)
    else:
        o_spec = pl.BlockSpec((None, tm, tn), lambda i, j, k: (j // nj_per, i, j % nj_per))
        o_shape = (split, M, ns)
    mn_spec = pl.BlockSpec((tm, tn), lambda i, j, k: (i, j))
    in_specs, args = [a_spec, b_spec], [a, b]
    if epi in ("add", "dsqrelu"):
        in_specs.append(mn_spec)
        args.append(extra)
    if epi == "sqrelu":
        out_shape = (jax.ShapeDtypeStruct(o_shape, BF16), jax.ShapeDtypeStruct(o_shape, BF16))
        out_specs = (o_spec, o_spec)
        n_out = 2
    else:
        out_shape = jax.ShapeDtypeStruct(o_shape, out_dtype)
        out_specs = o_spec
        n_out = 1

    def body(*refs):
        a_ref, b_ref = refs[0], refs[1]
        e_ref = refs[2] if len(args) == 3 else None
        outs = refs[len(args):len(args) + n_out]

        def finish(acc):
            if epi is None:
                outs[0][...] = acc.astype(outs[0].dtype)
            elif epi == "add":
                outs[0][...] = (e_ref[...] + acc).astype(outs[0].dtype)
            elif epi == "sqrelu":
                r = jnp.maximum(acc, 0.0)
                outs[0][...] = acc.astype(BF16)
                outs[1][...] = (r * r).astype(BF16)
            else:
                u = e_ref[...].astype(F32)
                outs[0][...] = (acc * (2.0 * jnp.maximum(u, 0.0))).astype(outs[0].dtype)

        prod = lax.dot_general(a_ref[...].astype(BF16), b_ref[...].astype(BF16), dn, preferred_element_type=F32)
        if nk == 1:
            finish(prod)
            return
        acc_ref = refs[-1]
        k = pl.program_id(2)

        @pl.when(k == 0)
        def _():
            acc_ref[...] = prod

        @pl.when((k != 0) & (k != nk - 1))
        def _():
            acc_ref[...] += prod

        @pl.when(k == nk - 1)
        def _():
            finish(acc_ref[...] + prod)

    est = 2 * (_nbytes((tm, tk), a.dtype) + _nbytes((tk, tn), b.dtype)) + _nbytes((tm, tn), F32)
    est += 2 * n_out * _nbytes((tm, tn), out_dtype if n_out == 1 else BF16)
    if len(args) == 3:
        est += 2 * _nbytes((tm, tn), extra.dtype)
    est += 3 * _nbytes((tm, tn), F32)
    return pl.pallas_call(
        body, name=name, grid=grid, in_specs=in_specs, out_specs=out_specs, out_shape=out_shape,
        scratch_shapes=[] if nk == 1 else [pltpu.VMEM((tm, tn), F32)],
        compiler_params=_params(("parallel", "parallel", "arbitrary"), est),
    )(*args)


def _perm(y, p):
    hi = y.astype(BF16)
    lo = (y - hi.astype(F32)).astype(BF16)
    d = lambda t: jnp.dot(t, p, preferred_element_type=F32)
    return d(hi) + d(lo)


def _rows_tile(T, d):
    return _tile(T, 2048 if d <= 256 else 512, 128)


def _norm_fwd(x, gain, *, name, rope=None, out_dtype=BF16, out_scale=None):
    H, T, d = x.shape
    tm = _rows_tile(T, d)
    g2 = gain.reshape(1, d).astype(F32)
    in_specs = [pl.BlockSpec((None, tm, d), lambda h, i: (h, i, 0)), pl.BlockSpec((1, d), lambda h, i: (0, 0))]
    args = [x, g2]
    if rope is not None:
        in_specs += [pl.BlockSpec((tm, d), lambda h, i: (i, 0)), pl.BlockSpec((tm, d), lambda h, i: (i, 0)),
                     pl.BlockSpec((d, d), lambda h, i: (0, 0))]
        args += list(rope)

    def body(*refs):
        x_ref, g_ref = refs[0], refs[1]
        o_ref = refs[-1]
        xv = x_ref[...]
        y = xv * lax.rsqrt(jnp.mean(xv * xv, axis=-1, keepdims=True) + NORM_EPS)
        y = y * g_ref[...]
        if rope is not None:
            c_ref, s_ref, p_ref = refs[2], refs[3], refs[4]
            y = y * c_ref[...] + _perm(y, p_ref[...]) * s_ref[...]
        if out_scale is not None:
            y = y * out_scale
        o_ref[...] = y.astype(o_ref.dtype)

    est = 2 * (_nbytes((tm, max(d, LANES)), F32) * (3 if rope is not None else 1) + _nbytes((tm, max(d, LANES)), out_dtype))
    est += 6 * _nbytes((tm, max(d, LANES)), F32)
    return pl.pallas_call(
        body, name=name, grid=(H, T // tm), in_specs=in_specs,
        out_specs=pl.BlockSpec((None, tm, d), lambda h, i: (h, i, 0)),
        out_shape=jax.ShapeDtypeStruct((H, T, d), out_dtype),
        compiler_params=_params(("parallel", "parallel"), est),
    )(*args)


def _norm_bwd(x, gain, dy, *, name, rope=None, group=1, res=None, out_dtype=F32, dy_scale=None):
    H, T, d = x.shape
    assert dy.shape == (H * group, T, d), (dy.shape, x.shape, group)
    tm = _rows_tile(T, d)
    g2 = gain.reshape(1, d).astype(F32)
    in_specs = [pl.BlockSpec((None, tm, d), lambda h, i: (h, i, 0)), pl.BlockSpec((1, d), lambda h, i: (0, 0)),
                pl.BlockSpec((group, tm, d), lambda h, i: (h, i, 0))]
    args = [x, g2, dy]
    if rope is not None:
        in_specs += [pl.BlockSpec((tm, d), lambda h, i: (i, 0)), pl.BlockSpec((tm, d), lambda h, i: (i, 0)),
                     pl.BlockSpec((d, d), lambda h, i: (0, 0))]
        args += list(rope)
    if res is not None:
        assert H == 1
        in_specs.append(pl.BlockSpec((tm, d), lambda h, i: (i, 0)))
        args.append(res)
    n_in = len(args)

    def body(*refs):
        x_ref, g_ref, dy_ref = refs[0], refs[1], refs[2]
        dx_ref, dg_ref = refs[n_in], refs[n_in + 1]
        first = (pl.program_id(0) == 0) & (pl.program_id(1) == 0)

        @pl.when(first)
        def _():
            dg_ref[...] = jnp.zeros_like(dg_ref)

        dyv = dy_ref[0].astype(F32)
        for g in range(1, group):
            dyv = dyv + dy_ref[g].astype(F32)
        if dy_scale is not None:
            dyv = dyv * dy_scale
        pos = 3
        if rope is not None:
            c_ref, s_ref, p_ref = refs[3], refs[4], refs[5]
            pos = 6
            dyv = dyv * c_ref[...] + _perm(dyv * s_ref[...], p_ref[...])
        xv = x_ref[...]
        r = lax.rsqrt(jnp.mean(xv * xv, axis=-1, keepdims=True) + NORM_EPS)
        xhat = xv * r
        dg_ref[...] += jnp.sum(dyv * xhat, axis=0, keepdims=True)
        dxh = dyv * g_ref[...]
        dx = r * (dxh - xhat * jnp.mean(dxh * xhat, axis=-1, keepdims=True))
        if res is not None:
            dx = dx + refs[pos][...]
        dx_ref[...] = dx.astype(dx_ref.dtype)

    wide = max(d, LANES)
    est = 2 * _nbytes((tm, wide), F32) * (2 + group + (2 if rope is not None else 0) + (1 if res is not None else 0))
    est += 8 * _nbytes((tm, wide), F32)
    return pl.pallas_call(
        body, name=name, grid=(H, T // tm), in_specs=in_specs,
        out_specs=(pl.BlockSpec((None, tm, d), lambda h, i: (h, i, 0)), pl.BlockSpec((1, d), lambda h, i: (0, 0))),
        out_shape=(jax.ShapeDtypeStruct((H, T, d), out_dtype), jax.ShapeDtypeStruct((1, d), F32)),
        compiler_params=_params(("arbitrary", "arbitrary"), est),
    )(*args)


def _group_sum(x, group, *, name, out_dtype=F32):
    HG, T, d = x.shape
    H = HG // group
    tm = _rows_tile(T, d)

    def body(x_ref, o_ref):
        acc = x_ref[0]
        for g in range(1, group):
            acc = acc + x_ref[g]
        o_ref[...] = acc.astype(o_ref.dtype)

    est = 2 * (group + 1) * _nbytes((tm, max(d, LANES)), F32)
    return pl.pallas_call(
        body, name=name, grid=(H, T // tm),
        in_specs=[pl.BlockSpec((group, tm, d), lambda h, i: (h, i, 0))],
        out_specs=pl.BlockSpec((None, tm, d), lambda h, i: (h, i, 0)),
        out_shape=jax.ShapeDtypeStruct((H, T, d), out_dtype),
        compiler_params=_params(("parallel", "parallel"), est),
    )(x)


def _delta(o, do, *, name):
    H, T, d = o.shape
    tm = _rows_tile(T, d)

    def body(o_ref, do_ref, dl_ref, dob_ref):
        dov = do_ref[...]
        dl = jnp.sum(o_ref[...] * dov, axis=-1, keepdims=True)
        dl_ref[...] = jnp.broadcast_to(dl, (tm, LANES))
        dob_ref[...] = dov.astype(BF16)

    spec = pl.BlockSpec((None, tm, d), lambda h, i: (h, i, 0))
    est = 2 * (3 * _nbytes((tm, max(d, LANES)), F32) + _nbytes((tm, LANES), F32))
    return pl.pallas_call(
        body, name=name, grid=(H, T // tm), in_specs=[spec, spec],
        out_specs=(pl.BlockSpec((None, tm, LANES), lambda h, i: (h, i, 0)), spec),
        out_shape=(jax.ShapeDtypeStruct((H, T, LANES), F32), jax.ShapeDtypeStruct((H, T, d), BF16)),
        compiler_params=_params(("parallel", "parallel"), est),
    )(o, do)


NT_DIMS = (((1,), (1,)), ((), ()))
TN_DIMS = (((0,), (0,)), ((), ()))
LOG2E = math.log2(math.e)
FLASH_CHUNK = 256
FLASH_ROW_PARTS = 8


def _flash_fwd(q, k, v, *, name, gather=()):
    H, T, dk = q.shape
    Hkv, _, dv = v.shape
    G = H // Hkv
    tq, tk = _tile(T, 1024), _tile(T, 4096)
    tp = _tile(tq, tq // FLASH_ROW_PARTS, SUBLANES_BF16)
    nk = T // tk

    n_r = len(gather)
    grid = (H, T // tq, nk)
    assert n_r == 0 or H >= 2

    def body(*refs):
        q_ref, k_ref, v_ref = refs[:3]
        o_ref, lse_ref, ob_ref = refs[3 + n_r:6 + n_r]
        m_ref, l_ref, acc_ref = refs[6 + 2 * n_r:9 + 2 * n_r]
        hi, qi, ki = pl.program_id(0), pl.program_id(1), pl.program_id(2)
        if n_r:
            ag_start, ag_forward, ag_finish = _gather_plan(refs[3:3 + n_r], refs[6 + n_r:6 + 2 * n_r],
                                                           *refs[9 + 2 * n_r:])
            pl.when((hi == 0) & (qi == 0) & (ki == 0))(ag_start)
            pl.when((hi == grid[0] - 1) & (qi == 0) & (ki == 0))(ag_forward)

        @pl.when(ki == 0)
        def _():
            m_ref[...] = jnp.full_like(m_ref, -jnp.inf)
            l_ref[...] = jnp.zeros_like(l_ref)
            acc_ref[...] = jnp.zeros_like(acc_ref)

        kv, vv = k_ref[...], v_ref[...]
        parts = [slice(part * tp, (part + 1) * tp) for part in range(tq // tp)]
        m_prev = [m_ref[rows, :] for rows in parts]
        l_prev = [l_ref[rows, :] for rows in parts]
        a_prev = [acc_ref[rows, :] for rows in parts]
        ss = [lax.dot_general(q_ref[rows, :], kv, NT_DIMS, preferred_element_type=F32) for rows in parts]
        m_new = [jnp.maximum(m, jnp.max(s, axis=-1, keepdims=True)) for m, s in zip(m_prev, ss)]
        alpha = [jnp.exp2(m - mn) for m, mn in zip(m_prev, m_new)]
        ps = [jnp.exp2(s - mn) for s, mn in zip(ss, m_new)]
        l_new = [a * l + jnp.sum(p, axis=-1, keepdims=True) for a, l, p in zip(alpha, l_prev, ps)]
        pv = [jnp.dot(p.astype(BF16), vv, preferred_element_type=F32) for p in ps]
        for rows, mn, ln, a, acc, o in zip(parts, m_new, l_new, alpha, a_prev, pv):
            m_ref[rows, :] = mn
            l_ref[rows, :] = ln
            acc_ref[rows, :] = a * acc + o

        @pl.when(ki == nk - 1)
        def _():
            l = l_ref[...]
            o = acc_ref[...] / l
            o_ref[...] = o
            ob_ref[...] = o.astype(BF16)
            lse_ref[...] = jnp.broadcast_to(m_ref[...] + jnp.log(l) * LOG2E, (tq, LANES))

        if n_r:
            pl.when((hi == grid[0] - 1) & (qi == grid[1] - 1) & (ki == grid[2] - 1))(ag_finish)

    est = 2 * (_nbytes((tq, dk), BF16) + _nbytes((tk, dk + dv), BF16) + _nbytes((tq, dv + LANES), F32))
    est += 4 * _nbytes((tq, tk), F32) + 3 * _nbytes((tq, dv + 3 * LANES), F32)
    outs = pl.pallas_call(
        body, name=name, grid=grid,
        in_specs=[pl.BlockSpec((None, tq, dk), lambda h, i, j: (h, i, 0)),
                  pl.BlockSpec((None, tk, dk), lambda h, i, j: (h // G, j, 0)),
                  pl.BlockSpec((None, tk, dv), lambda h, i, j: (h // G, j, 0))] + [HBM_SPEC] * n_r,
        out_specs=[pl.BlockSpec((None, tq, dv), lambda h, i, j: (h, i, 0)),
                   pl.BlockSpec((None, tq, LANES), lambda h, i, j: (h, i, 0)),
                   pl.BlockSpec((None, tq, dv), lambda h, i, j: (h, i, 0))] + [HBM_SPEC] * n_r,
        out_shape=[jax.ShapeDtypeStruct((H, T, dv), F32), jax.ShapeDtypeStruct((H, T, LANES), F32),
                   jax.ShapeDtypeStruct((H, T, dv), BF16)] + _gathered_shapes(gather),
        scratch_shapes=[pltpu.VMEM((tq, 1), F32), pltpu.VMEM((tq, 1), F32), pltpu.VMEM((tq, dv), F32)]
                       + (_gather_scratch(n_r) if n_r else []),
        compiler_params=_params(("arbitrary",) * 3 if n_r else ("parallel", "parallel", "arbitrary"), est),
    )(q, k, v, *gather)
    return (outs[0], outs[1], outs[2], outs[3:]) if n_r else (outs[0], outs[1], outs[2])


def _flash_bwd(q, k, v, do, lse2, delta, *, name, rider=None):
    H, T, dk = q.shape
    Hkv, _, dv = v.shape
    G = H // Hkv
    tq, tk = _tile(T, 1024), _tile(T, 2048)
    tc = _tile(tk, FLASH_CHUNK)

    kind, carried = rider if rider is not None else (None, ())
    n_r = len(carried)
    grid = (H, T // tk, T // tq)
    if kind == "scatter":
        plan, rider_scratch = _scatter_plan, _scatter_scratch(n_r)
        rider_shapes = [jax.ShapeDtypeStruct(p.shape, p.dtype) for p in carried]
    elif kind == "exchange":
        plan, rider_scratch = _exchange_plan, _exchange_scratch(n_r)
        rider_shapes = [jax.ShapeDtypeStruct((N_CHIPS,) + g.shape[2:], g.dtype) for g in carried]
    else:
        assert kind is None
        rider_scratch, rider_shapes = [], []

    def body(*refs):
        q_ref, k_ref, v_ref, do_ref, lse_ref, dl_ref = refs[:6]
        dq_ref, dk_ref, dv_ref = refs[6 + n_r:9 + n_r]
        hi, ki, qi = pl.program_id(0), pl.program_id(1), pl.program_id(2)
        if n_r:
            rider_start, rider_finish = plan(refs[6:6 + n_r], refs[9 + n_r:9 + 2 * n_r], *refs[9 + 2 * n_r:])
            pl.when((hi == 0) & (ki == 0) & (qi == 0))(rider_start)
        rows = pl.ds(pl.multiple_of(qi * tq, tq), tq)

        @pl.when(qi == 0)
        def _():
            dk_ref[...] = jnp.zeros_like(dk_ref)
            dv_ref[...] = jnp.zeros_like(dv_ref)

        @pl.when(ki == 0)
        def _():
            dq_ref[rows, :] = jnp.zeros((tq, dk), F32)

        qv, dov = q_ref[...], do_ref[...]
        lse2 = lse_ref[:, :1]
        dl = dl_ref[:, :1]
        chunks = [slice(c * tc, (c + 1) * tc) for c in range(tk // tc)]
        kcs = [k_ref[ks, :] for ks in chunks]
        vcs = [v_ref[ks, :] for ks in chunks]
        dv_old = [dv_ref[ks, :] for ks in chunks]
        dk_old = [dk_ref[ks, :] for ks in chunks]
        dq_old = dq_ref[rows, :]
        ss = [lax.dot_general(qv, kc, NT_DIMS, preferred_element_type=F32) for kc in kcs]
        dps = [lax.dot_general(dov, vc, NT_DIMS, preferred_element_type=F32) for vc in vcs]
        ps = [jnp.exp2(s - lse2) for s in ss]
        dss = [(p * (dp - dl)).astype(BF16) for p, dp in zip(ps, dps)]
        pbs = [p.astype(BF16) for p in ps]
        dvs = [lax.dot_general(pb, dov, TN_DIMS, preferred_element_type=F32) for pb in pbs]
        dks = [lax.dot_general(ds, qv, TN_DIMS, preferred_element_type=F32) for ds in dss]
        dqs = [jnp.dot(ds, kc, preferred_element_type=F32) for ds, kc in zip(dss, kcs)]
        for ks, old, new in zip(chunks, dv_old, dvs):
            dv_ref[ks, :] = old + new
        for ks, old, new in zip(chunks, dk_old, dks):
            dk_ref[ks, :] = old + new
        dq_c = dqs[0]
        for extra in dqs[1:]:
            dq_c = dq_c + extra
        dq_ref[rows, :] = dq_old + dq_c

        if n_r:
            pl.when((hi == grid[0] - 1) & (ki == grid[1] - 1) & (qi == grid[2] - 1))(rider_finish)

    est = 2 * (_nbytes((tq, dk + dv), BF16) + _nbytes((tk, dk + dv), BF16) + 2 * _nbytes((tq, LANES), F32))
    est += 2 * (_nbytes((T, dk), F32) + _nbytes((tk, dk + dv), F32)) + 10 * _nbytes((tq, tc), F32)
    outs = pl.pallas_call(
        body, name=name, grid=grid,
        in_specs=[pl.BlockSpec((None, tq, dk), lambda h, j, i: (h, i, 0)),
                  pl.BlockSpec((None, tk, dk), lambda h, j, i: (h // G, j, 0)),
                  pl.BlockSpec((None, tk, dv), lambda h, j, i: (h // G, j, 0)),
                  pl.BlockSpec((None, tq, dv), lambda h, j, i: (h, i, 0)),
                  pl.BlockSpec((None, tq, LANES), lambda h, j, i: (h, i, 0)),
                  pl.BlockSpec((None, tq, LANES), lambda h, j, i: (h, i, 0))] + [HBM_SPEC] * n_r,
        out_specs=[pl.BlockSpec((None, T, dk), lambda h, j, i: (h, 0, 0)),
                   pl.BlockSpec((None, tk, dk), lambda h, j, i: (h, j, 0)),
                   pl.BlockSpec((None, tk, dv), lambda h, j, i: (h, j, 0))] + [HBM_SPEC] * n_r,
        out_shape=[jax.ShapeDtypeStruct((H, T, dk), F32), jax.ShapeDtypeStruct((H, T, dk), F32),
                   jax.ShapeDtypeStruct((H, T, dv), F32)] + rider_shapes,
        scratch_shapes=rider_scratch,
        compiler_params=_params(("arbitrary", "arbitrary", "arbitrary"), est),
    )(q, k, v, do, lse2, delta, *carried)
    return (outs[0], outs[1], outs[2], outs[3:]) if n_r else tuple(outs)


def _swa_specs(G, d, n_blocks, lanes, gpb):
    B = SWA_BLOCK
    prev = lambda j, i: (j, jnp.maximum(i - 1, 0), 0)
    cur = lambda j, i: (j, i, 0)
    nxt = lambda j, i: (j, jnp.minimum(i + 1, n_blocks - 1), 0)
    q_specs = [pl.BlockSpec((gpb * G, B, lanes), m) for m in (prev, cur, nxt)]
    kv_specs = [pl.BlockSpec((gpb, B, d), m) for m in (prev, cur, nxt)]
    return q_specs, kv_specs, cur


def _swa_parts(G, gpb):
    gp = G // SWA_HEAD_PARTS
    return gp, [(g, slice(g * G + part * gp, g * G + (part + 1) * gp)) for g in range(gpb) for part in range(SWA_HEAD_PARTS)]


def _swa_bias(i, T):
    B = SWA_BLOCK
    row = lax.broadcasted_iota(jnp.int32, (B, 3 * B), 0)
    col = lax.broadcasted_iota(jnp.int32, (B, 3 * B), 1)
    kpos = (i - 1) * B + col
    valid = (col >= row) & (col <= row + 2 * SWA_WINDOW) & (kpos >= 0) & (kpos < T)
    return jnp.where(valid, 0.0, -jnp.inf)


def _swa_fwd(q, k, v, sink, *, name):
    Hq, T, d = q.shape
    Hkv = k.shape[0]
    G = Hq // Hkv
    B = SWA_BLOCK
    nb = T // B
    gpb = _tile(Hkv, SWA_GROUPS, 1)
    _, kv_specs, cur = _swa_specs(G, d, nb, d, gpb)

    def body(q_ref, k0, k1, k2, v0, v1, v2, sink_ref, o_ref, lse_ref, ob_ref):
        i = pl.program_id(1)
        kvs = [jnp.concatenate([k0[g], k1[g], k2[g]], axis=0) for g in range(gpb)]
        vvs = [jnp.concatenate([v0[g], v1[g], v2[g]], axis=0) for g in range(gpb)]
        bias = _swa_bias(i, T)[None]
        gp, parts = _swa_parts(G, gpb)
        sks = [sink_ref[hs] for _, hs in parts]
        ss = [lax.dot_general(q_ref[hs].reshape(gp * B, d), kvs[g], NT_DIMS, preferred_element_type=F32) for g, hs in parts]
        ss = [s.reshape(gp, B, 3 * B) + bias for s in ss]
        ms = [jnp.maximum(jnp.max(s, axis=-1, keepdims=True), sk) for s, sk in zip(ss, sks)]
        ps = [jnp.exp(s - m) for s, m in zip(ss, ms)]
        dens = [jnp.sum(p, axis=-1, keepdims=True) + jnp.exp(sk - m) for p, sk, m in zip(ps, sks, ms)]
        pns = [(p * (1.0 / den)).reshape(gp * B, 3 * B).astype(BF16) for p, den in zip(ps, dens)]
        os_ = [jnp.dot(pn, vvs[g], preferred_element_type=F32).reshape(gp, B, d) for pn, (g, _) in zip(pns, parts)]
        for (_, hs), o, m, den in zip(parts, os_, ms, dens):
            o_ref[hs] = o
            ob_ref[hs] = o.astype(BF16)
            lse_ref[hs] = jnp.broadcast_to(m + jnp.log(den), (gp, B, LANES))

    GG = gpb * G
    est = 2 * (_nbytes((GG, B, LANES), BF16) + 6 * gpb * _nbytes((B, LANES), BF16) + 2 * _nbytes((GG, B, LANES), F32))
    est += 8 * _nbytes((GG * B, 3 * B), F32)
    return pl.pallas_call(
        body, name=name, grid=(Hkv // gpb, nb),
        in_specs=[pl.BlockSpec((GG, B, d), cur)] + kv_specs + kv_specs + [pl.BlockSpec((GG, 1, 1), lambda j, i: (j, 0, 0))],
        out_specs=(pl.BlockSpec((GG, B, d), cur), pl.BlockSpec((GG, B, LANES), cur), pl.BlockSpec((GG, B, d), cur)),
        out_shape=(jax.ShapeDtypeStruct((Hq, T, d), F32), jax.ShapeDtypeStruct((Hq, T, LANES), F32),
                   jax.ShapeDtypeStruct((Hq, T, d), BF16)),
        compiler_params=_params(("parallel", "parallel"), est),
    )(q, k, k, k, v, v, v, sink)


def _swa_dq(q, k, v, o, do, lse, sink, *, name):
    Hq, T, d = q.shape
    Hkv = k.shape[0]
    G = Hq // Hkv
    B = SWA_BLOCK
    nb = T // B
    gpb = _tile(Hkv, SWA_GROUPS, 1)
    _, kv_specs, cur = _swa_specs(G, d, nb, d, gpb)

    def body(q_ref, do_ref, lse_ref, o_ref, k0, k1, k2, v0, v1, v2, sink_ref, dq_ref, dsink_ref, dl_ref, dob_ref):
        i = pl.program_id(1)
        kvs = [jnp.concatenate([k0[g], k1[g], k2[g]], axis=0) for g in range(gpb)]
        vvs = [jnp.concatenate([v0[g], v1[g], v2[g]], axis=0) for g in range(gpb)]
        bias = _swa_bias(i, T)[None]
        gp, parts = _swa_parts(G, gpb)
        lses = [lse_ref[hs, :, :1] for _, hs in parts]
        dovs = [do_ref[hs] for _, hs in parts]
        dls = [jnp.sum(o_ref[hs] * dov, axis=-1, keepdims=True) for (_, hs), dov in zip(parts, dovs)]
        dobs = [dov.astype(BF16) for dov in dovs]
        ss = [lax.dot_general(q_ref[hs].reshape(gp * B, d), kvs[g], NT_DIMS, preferred_element_type=F32) for g, hs in parts]
        dps = [lax.dot_general(dob.reshape(gp * B, d), vvs[g], NT_DIMS, preferred_element_type=F32)
               for dob, (g, _) in zip(dobs, parts)]
        ps = [jnp.exp(s.reshape(gp, B, 3 * B) + bias - lse) for s, lse in zip(ss, lses)]
        dss = [(p * (dp.reshape(gp, B, 3 * B) - dl)).reshape(gp * B, 3 * B).astype(BF16)
               for p, dp, dl in zip(ps, dps, dls)]
        dqs = [jnp.dot(ds, kvs[g], preferred_element_type=F32).reshape(gp, B, d) for ds, (g, _) in zip(dss, parts)]
        dsks = [-jnp.sum(jnp.exp(sink_ref[hs] - lse) * dl, axis=1, keepdims=True)
                for (_, hs), lse, dl in zip(parts, lses, dls)]

        @pl.when(i == 0)
        def _():
            dsink_ref[...] = jnp.zeros_like(dsink_ref)

        for (_, hs), dq, dsk, dl, dob in zip(parts, dqs, dsks, dls, dobs):
            dq_ref[hs] = dq
            dsink_ref[hs] += jnp.broadcast_to(dsk, (gp, 1, LANES))
            dl_ref[hs] = jnp.broadcast_to(dl, (gp, B, LANES))
            dob_ref[hs] = dob

    GG = gpb * G
    est = 2 * (2 * _nbytes((GG, B, LANES), BF16) + 6 * gpb * _nbytes((B, LANES), BF16) + 5 * _nbytes((GG, B, LANES), F32))
    est += 8 * _nbytes((GG * B, 3 * B), F32)
    q_spec, l_spec = pl.BlockSpec((GG, B, d), cur), pl.BlockSpec((GG, B, LANES), cur)
    return pl.pallas_call(
        body, name=name, grid=(Hkv // gpb, nb),
        in_specs=[q_spec, q_spec, l_spec, q_spec] + kv_specs + kv_specs + [pl.BlockSpec((GG, 1, 1), lambda j, i: (j, 0, 0))],
        out_specs=(q_spec, pl.BlockSpec((GG, 1, LANES), lambda j, i: (j, 0, 0)), l_spec, q_spec),
        out_shape=(jax.ShapeDtypeStruct((Hq, T, d), F32), jax.ShapeDtypeStruct((Hq, 1, LANES), F32),
                   jax.ShapeDtypeStruct((Hq, T, LANES), F32), jax.ShapeDtypeStruct((Hq, T, d), BF16)),
        compiler_params=_params(("arbitrary", "arbitrary"), est),
    )(q, do, lse, o, k, k, k, v, v, v, sink)


def _swa_dkv(q, k, v, do, lse, delta, *, name):
    Hq, T, d = q.shape
    Hkv = k.shape[0]
    G = Hq // Hkv
    B = SWA_BLOCK
    nb = T // B
    gpb = _tile(Hkv, SWA_GROUPS, 1)
    q_specs, _, cur = _swa_specs(G, d, nb, d, gpb)
    l_specs, _, _ = _swa_specs(G, d, nb, LANES, gpb)

    def body(k_ref, v_ref, q0, q1, q2, d0, d1, d2, l0, l1, l2, e0, e1, e2, dk_ref, dv_ref):
        b = pl.program_id(1)
        row = lax.broadcasted_iota(jnp.int32, (B, B), 0)
        col = lax.broadcasted_iota(jnp.int32, (B, B), 1)
        biases = []
        for part in range(3):
            qpos = (b + part - 1) * B + row
            diff = (part - 1) * B + row - col
            valid = (diff >= -SWA_WINDOW) & (diff <= SWA_WINDOW) & (qpos >= 0) & (qpos < T)
            biases.append(jnp.where(valid, 0.0, -jnp.inf)[None])
        chains = [(g, part) for g in range(gpb) for part in range(3)]
        heads = [slice(g * G, (g + 1) * G) for g, _ in chains]
        kvs = [k_ref[g] for g, _ in chains]
        vvs = [v_ref[g] for g, _ in chains]
        qvs = [(q0, q1, q2)[part][hs].reshape(G * B, d) for (_, part), hs in zip(chains, heads)]
        dovs = [(d0, d1, d2)[part][hs].reshape(G * B, d) for (_, part), hs in zip(chains, heads)]
        lses = [(l0, l1, l2)[part][hs, :, :1] for (_, part), hs in zip(chains, heads)]
        dls = [(e0, e1, e2)[part][hs, :, :1] for (_, part), hs in zip(chains, heads)]
        ss = [lax.dot_general(qv, kv, NT_DIMS, preferred_element_type=F32) for qv, kv in zip(qvs, kvs)]
        dps = [lax.dot_general(dov, vv, NT_DIMS, preferred_element_type=F32) for dov, vv in zip(dovs, vvs)]
        ps = [jnp.exp(s.reshape(G, B, B) + biases[part] - lse) for s, (_, part), lse in zip(ss, chains, lses)]
        dss = [(p * (dp.reshape(G, B, B) - dl)).reshape(G * B, B).astype(BF16) for p, dp, dl in zip(ps, dps, dls)]
        pbs = [p.reshape(G * B, B).astype(BF16) for p in ps]
        dvs = [lax.dot_general(pb, dov, TN_DIMS, preferred_element_type=F32) for pb, dov in zip(pbs, dovs)]
        dks = [lax.dot_general(ds, qv, TN_DIMS, preferred_element_type=F32) for ds, qv in zip(dss, qvs)]
        for g in range(gpb):
            dk_ref[g] = dks[3 * g] + dks[3 * g + 1] + dks[3 * g + 2]
            dv_ref[g] = (dvs[3 * g] + dvs[3 * g + 1] + dvs[3 * g + 2]).astype(BF16)

    GG = gpb * G
    est = 2 * (6 * _nbytes((GG, B, LANES), BF16) + 6 * _nbytes((GG, B, LANES), F32) + 4 * gpb * _nbytes((B, LANES), F32))
    est += 10 * _nbytes((GG * B, B), F32)
    kspec = pl.BlockSpec((gpb, B, d), cur)
    return pl.pallas_call(
        body, name=name, grid=(Hkv // gpb, nb),
        in_specs=[kspec, kspec] + q_specs + q_specs + l_specs + l_specs,
        out_specs=(kspec, kspec),
        out_shape=(jax.ShapeDtypeStruct((Hkv, T, d), F32), jax.ShapeDtypeStruct((Hkv, T, d), BF16)),
        compiler_params=_params(("parallel", "parallel"), est),
    )(k, v, q, q, q, do, do, do, lse, lse, lse, delta, delta, delta)


def _loss_head(y, target, *, name):
    T, D = y.shape
    tm = _tile(T, 512)

    def body(y_ref, t_ref, dy_ref, s_ref):
        @pl.when(pl.program_id(0) == 0)
        def _():
            s_ref[...] = jnp.zeros_like(s_ref)

        e = y_ref[...] - t_ref[...]
        dy_ref[...] = e / D
        s_ref[...] += jnp.sum(jnp.sum(e * e, axis=-1, keepdims=True), axis=0, keepdims=True)

    spec = pl.BlockSpec((tm, D), lambda i: (i, 0))
    return pl.pallas_call(
        body, name=name, grid=(T // tm,), in_specs=[spec, spec],
        out_specs=(spec, pl.BlockSpec((1, 1), lambda i: (0, 0))),
        out_shape=(jax.ShapeDtypeStruct((T, D), F32), jax.ShapeDtypeStruct((1, 1), F32)),
        compiler_params=_params(("arbitrary",), 8 * _nbytes((tm, D), F32)),
    )(y, target)


def _adamw(w, g, m, v, *, name):
    R, C = w.shape
    tr = _tile(R, max(8, (1 << 19) // max(C, LANES) // 8 * 8), 8)

    def body(w_ref, g_ref, m_ref, v_ref, d_ref, nm_ref, nv_ref):
        gv = g_ref[...]
        nm = ADAM_B1 * m_ref[...] + (1.0 - ADAM_B1) * gv
        nv = ADAM_B2 * v_ref[...] + (1.0 - ADAM_B2) * jnp.square(gv)
        m_hat = nm / (1.0 - ADAM_B1 ** ADAM_STEP)
        v_hat = nv / (1.0 - ADAM_B2 ** ADAM_STEP)
        d_ref[...] = -ADAM_LR * (m_hat / (jnp.sqrt(v_hat) + ADAM_EPS) + ADAM_WD * w_ref[...])
        nm_ref[...] = nm
        nv_ref[...] = nv

    spec = pl.BlockSpec((tr, C), lambda i: (i, 0))
    sds = jax.ShapeDtypeStruct((R, C), F32)
    return pl.pallas_call(
        body, name=name, grid=(R // tr,), in_specs=[spec] * 4, out_specs=(spec,) * 3, out_shape=(sds,) * 3,
        compiler_params=_params(("parallel",), 16 * _nbytes((tr, max(C, LANES)), F32)),
    )(w, g, m, v)


def _to_bf16(w, *, name):
    R, C = w.shape
    tr = _comm_rows_tile(R, C)

    def body(w_ref, o_ref):
        o_ref[...] = w_ref[...].astype(BF16)

    spec = pl.BlockSpec((tr, C), lambda i: (i, 0))
    return pl.pallas_call(
        body, name=name, grid=(R // tr,), in_specs=[spec], out_specs=spec, out_shape=jax.ShapeDtypeStruct((R, C), BF16),
        compiler_params=_params(("parallel",), 6 * _nbytes((tr, max(C, LANES)), F32)),
    )(w)


def _comm_rows_tile(R, L):
    return _tile(R, max(SUBLANES_BF16, (1 << 19) // L // SUBLANES_BF16 * SUBLANES_BF16), SUBLANES_BF16)


def _pair_add(g, recv, c_idx, *, name):
    _, _, R, L = g.shape
    tr = _comm_rows_tile(R, L)

    def body(c_ref, g_ref, r_ref, o_ref):
        o_ref[...] = (g_ref[...] + r_ref[...]).astype(BF16)

    grid_spec = pltpu.PrefetchScalarGridSpec(
        num_scalar_prefetch=1, grid=(N_CHIPS, R // tr),
        in_specs=[pl.BlockSpec((None, None, tr, L), lambda j, i, c: (j, c[0], i, 0)),
                  pl.BlockSpec((None, tr, L), lambda j, i, c: (j, i, 0))],
        out_specs=pl.BlockSpec((None, tr, L), lambda j, i, c: (j, i, 0)))
    return pl.pallas_call(
        body, name=name, grid_spec=grid_spec, out_shape=jax.ShapeDtypeStruct((N_CHIPS, R, L), BF16),
        compiler_params=_params(("parallel", "parallel"), 8 * _nbytes((tr, L), F32)),
    )(c_idx, g, recv)


def _sum_chips(q, c_idx, *, name):
    _, R, L = q.shape
    tr = _comm_rows_tile(R, L)

    def body(c_ref, q_ref, o_ref):
        acc = q_ref[0].astype(F32)
        for j in range(1, N_CHIPS):
            acc = acc + q_ref[j].astype(F32)
        o_ref[...] = acc

    grid_spec = pltpu.PrefetchScalarGridSpec(
        num_scalar_prefetch=1, grid=(R // tr,),
        in_specs=[pl.BlockSpec((N_CHIPS, tr, L), lambda i, c: (0, i, 0))],
        out_specs=pl.BlockSpec((None, tr, L), lambda i, c: (c[0], i, 0)))
    return pl.pallas_call(
        body, name=name, grid_spec=grid_spec, out_shape=jax.ShapeDtypeStruct((2, R, L), F32),
        compiler_params=_params(("parallel",), 10 * _nbytes((tr, L), F32)),
    )(c_idx, q)


HBM_SPEC = pl.BlockSpec(memory_space=pltpu.HBM)


def _position():
    return lax.axis_index("x"), lax.axis_index("y"), lax.axis_index("c")


def _other_chips(x, y):
    return [(1 - x, y), (x, 1 - y), (1 - x, 1 - y)]


AG_COPIES = 7


def _gather_plan(w_refs, out_refs, send_sems, recv_sems, local_sems):
    n = len(w_refs)
    x, y, c = _position()
    me, sibling = (x, y, c), (x, y, 1 - c)
    chips = _other_chips(x, y)

    def copy(i, k, block, to, src=None):
        px, py, pc = block
        slot = out_refs[i].at[4 * px + 2 * py + pc]
        return pltpu.make_async_remote_copy(
            src_ref=slot if src is None else src, dst_ref=slot, send_sem=send_sems.at[AG_COPIES * i + k],
            recv_sem=recv_sems.at[AG_COPIES * i + k], device_id=to, device_id_type=MESH)

    def local(i):
        return pltpu.make_async_copy(w_refs[i].at[c], out_refs[i].at[4 * x + 2 * y + c], local_sems.at[i])

    def first(i):
        own = w_refs[i].at[c]
        return [copy(i, 0, me, sibling, src=own)] + [copy(i, 1 + j, me, (*chip, c), src=own)
                                                     for j, chip in enumerate(chips)]

    def passed(i):
        return [copy(i, 4 + j, (*chip, c), sibling) for j, chip in enumerate(chips)]

    def start():
        for i in range(n):
            local(i).start()
            for cp in first(i):
                cp.start()

    def forward():
        for i in range(n):
            for j, chip in enumerate(chips):
                copy(i, 1 + j, (*chip, c), me).wait_recv()
                passed(i)[j].start()

    def finish():
        for i in range(n):
            copy(i, 0, sibling, me).wait_recv()
            for j, chip in enumerate(chips):
                copy(i, 4 + j, (*chip, 1 - c), me).wait_recv()
        for i in range(n):
            for cp in first(i) + passed(i):
                cp.wait_send()
            local(i).wait()

    return start, forward, finish


def _gather_scratch(n):
    return [pltpu.SemaphoreType.DMA((AG_COPIES * n,)), pltpu.SemaphoreType.DMA((AG_COPIES * n,)),
            pltpu.SemaphoreType.DMA((n,))]


def _gathered_shapes(ws):
    return [jax.ShapeDtypeStruct((2 * N_CHIPS,) + w.shape[1:], w.dtype) for w in ws]


def _all_gather_halves(ws, *, name):
    n = len(ws)

    def body(*refs):
        for step in _gather_plan(refs[:n], refs[n:2 * n], *refs[2 * n:]):
            step()

    return pl.pallas_call(
        body, name=name, in_specs=[HBM_SPEC] * n, out_specs=[HBM_SPEC] * n, out_shape=_gathered_shapes(ws),
        scratch_shapes=_gather_scratch(n),
    )(*ws)


def _exchange_plan(g_refs, out_refs, send_sems, recv_sems):
    n = len(g_refs)
    x, y, c = _position()

    def copies():
        return [pltpu.make_async_remote_copy(
            src_ref=g_refs[i].at[j, 1 - c], dst_ref=out_refs[i].at[j], send_sem=send_sems.at[N_CHIPS * i + j],
            recv_sem=recv_sems.at[N_CHIPS * i + j], device_id=(x, y, 1 - c), device_id_type=MESH)
            for i in range(n) for j in range(N_CHIPS)]

    def start():
        for cp in copies():
            cp.start()

    def finish():
        for cp in copies():
            cp.wait()

    return start, finish


def _exchange_scratch(n):
    return [pltpu.SemaphoreType.DMA((N_CHIPS * n,)), pltpu.SemaphoreType.DMA((N_CHIPS * n,))]


def _sibling_exchange(gs, *, name):
    n = len(gs)

    def body(*refs):
        for step in _exchange_plan(refs[:n], refs[n:2 * n], *refs[2 * n:]):
            step()

    return pl.pallas_call(
        body, name=name, in_specs=[HBM_SPEC] * n, out_specs=[HBM_SPEC] * n,
        out_shape=[jax.ShapeDtypeStruct((N_CHIPS,) + g.shape[2:], g.dtype) for g in gs],
        scratch_shapes=_exchange_scratch(n),
    )(*gs)


def _scatter_plan(p_refs, q_refs, send_sems, recv_sems, local_sems):
    n = len(p_refs)
    others = N_CHIPS - 1
    x, y, c = _position()
    me = 2 * x + y
    chips = _other_chips(x, y)

    def copy(i, k, chip, src_slot, dst_slot):
        return pltpu.make_async_remote_copy(
            src_ref=p_refs[i].at[src_slot], dst_ref=q_refs[i].at[dst_slot], send_sem=send_sems.at[others * i + k],
            recv_sem=recv_sems.at[others * i + k], device_id=(*chip, c), device_id_type=MESH)

    def local(i):
        return pltpu.make_async_copy(p_refs[i].at[me], q_refs[i].at[me], local_sems.at[i])

    def sends(i):
        return [copy(i, k, chip, 2 * chip[0] + chip[1], me) for k, chip in enumerate(chips)]

    def start():
        for i in range(n):
            local(i).start()
            for cp in sends(i):
                cp.start()

    def finish():
        for i in range(n):
            for k, chip in enumerate(chips):
                copy(i, k, chip, me, 2 * chip[0] + chip[1]).wait_recv()
        for i in range(n):
            for cp in sends(i):
                cp.wait_send()
            local(i).wait()

    return start, finish


def _scatter_scratch(n):
    others = N_CHIPS - 1
    return [pltpu.SemaphoreType.DMA((others * n,)), pltpu.SemaphoreType.DMA((others * n,)),
            pltpu.SemaphoreType.DMA((n,))]


def _chip_scatter(ps, *, name):
    n = len(ps)

    def body(*refs):
        for step in _scatter_plan(refs[:n], refs[n:2 * n], *refs[2 * n:]):
            step()

    return pl.pallas_call(
        body, name=name, in_specs=[HBM_SPEC] * n, out_specs=[HBM_SPEC] * n,
        out_shape=[jax.ShapeDtypeStruct(p.shape, p.dtype) for p in ps], scratch_shapes=_scatter_scratch(n),
    )(*ps)


def _sibling_share(fs, *, name):
    n = len(fs)

    def body(*refs):
        in_refs, out_refs = refs[:n], refs[n:2 * n]
        send_sems, recv_sems = refs[2 * n:]
        x, y, c = _position()

        def copy(i, half):
            return pltpu.make_async_remote_copy(
                src_ref=in_refs[i].at[half], dst_ref=out_refs[i].at[half], send_sem=send_sems.at[i],
                recv_sem=recv_sems.at[i], device_id=(x, y, 1 - c), device_id_type=MESH)

        sends = [copy(i, c) for i in range(n)]
        for cp in sends:
            cp.start()
        for i in range(n):
            copy(i, 1 - c).wait_recv()
        for cp in sends:
            cp.wait_send()

    return pl.pallas_call(
        body, name=name, in_specs=[HBM_SPEC] * n, out_specs=[HBM_SPEC] * n,
        out_shape=[jax.ShapeDtypeStruct(f.shape, f.dtype) for f in fs],
        input_output_aliases={i: i for i in range(n)},
        scratch_shapes=[pltpu.SemaphoreType.DMA((n,)), pltpu.SemaphoreType.DMA((n,))],
    )(*fs)


def _all_reduce_small(s, *, name):
    R, L = s.shape
    n_dev = 2 * N_CHIPS

    def body(s_ref, out_ref, buf, send_sems, recv_sems, local_sem):
        x, y, c = _position()
        me, sibling = (x, y, c), (x, y, 1 - c)
        chips = _other_chips(x, y)

        def slot(px, py, pc):
            return buf.at[4 * px + 2 * py + pc]

        def copy(k, block, to, src=None):
            return pltpu.make_async_remote_copy(
                src_ref=slot(*block) if src is None else src, dst_ref=slot(*block),
                send_sem=send_sems.at[k], recv_sem=recv_sems.at[k], device_id=to, device_id_type=MESH)

        mine = pltpu.make_async_copy(s_ref, slot(*me), local_sem)
        mine.start()
        first = [copy(0, me, sibling, src=s_ref)]
        first += [copy(1 + j, me, (*chip, c), src=s_ref) for j, chip in enumerate(chips)]
        for cp in first:
            cp.start()
        passed = [copy(4 + j, (*chip, c), sibling) for j, chip in enumerate(chips)]
        for j, chip in enumerate(chips):
            copy(1 + j, (*chip, c), me).wait_recv()
            passed[j].start()
        copy(0, sibling, me).wait_recv()
        for j, chip in enumerate(chips):
            copy(4 + j, (*chip, 1 - c), me).wait_recv()
        for cp in first + passed:
            cp.wait_send()
        mine.wait()
        acc = buf[0]
        for j in range(1, n_dev):
            acc = acc + buf[j]
        out_ref[...] = acc

    vmem = pl.BlockSpec(memory_space=pltpu.VMEM)
    return pl.pallas_call(
        body, name=name, in_specs=[vmem], out_specs=vmem, out_shape=jax.ShapeDtypeStruct((R, L), F32),
        scratch_shapes=[pltpu.VMEM((n_dev, R, L), F32), pltpu.SemaphoreType.DMA((7,)), pltpu.SemaphoreType.DMA((7,)),
                        pltpu.SemaphoreType.DMA],
    )(s)


def _rope_cos_sin(pos, dim, theta):
    inv = jnp.float32(theta) ** (-jnp.arange(0, dim, 2, dtype=F32) / dim)
    ang = pos.astype(F32)[:, None] * inv[None, :]
    return jnp.cos(ang), jnp.sin(ang)


def _rope_tables(T, d, segments):
    P = np.zeros((d, d), np.float32)
    c_parts, s_parts, at = [], [], 0
    for start, size, cos, sin in segments:
        half = size // 2
        if start > at:
            c_parts.append(jnp.ones((T, start - at), F32))
            s_parts.append(jnp.zeros((T, start - at), F32))
        c_parts += [cos, cos]
        s_parts += [-sin, sin]
        at = start + size
        for p in range(half):
            P[start + half + p, start + p] = 1.0
            P[start + p, start + half + p] = 1.0
    if at < d:
        c_parts.append(jnp.ones((T, d - at), F32))
        s_parts.append(jnp.zeros((T, d - at), F32))
    return jnp.concatenate(c_parts, axis=1), jnp.concatenate(s_parts, axis=1), jnp.asarray(P, BF16)


def _heads(t, H, d):
    return t.reshape(t.shape[0], H, d).transpose(1, 0, 2)


def _unheads(t):
    H, T, d = t.shape
    return t.transpose(1, 0, 2).reshape(T, H * d)


def _dw(a, b, *, name, axis):
    K, N = a.shape[1], b.shape[1]
    if axis == 1:
        return _mm(a, b, mode="tn", name=name).reshape(N_CHIPS, K // N_CHIPS, N)
    if (N // N_CHIPS) % LANES == 0:
        return _mm(a, b, mode="tn", name=name, split=N_CHIPS)
    return _mm(a, b, mode="tn", name=name).reshape(K, N_CHIPS, N // N_CHIPS).transpose(1, 0, 2)


def _mlp_fwd(x, gain, w_up, w_down, tag):
    hm = _norm_fwd(x[None], gain, name=f"mlp{tag}_norm")[0]
    u, act = _mm(hm, w_up, mode="nn", name=f"mlp{tag}_up", epi="sqrelu")
    x_out = _mm(act, w_down, mode="nn", name=f"mlp{tag}_down", epi="add", extra=x)
    return x_out, (hm, u, act)


def _mlp_bwd(x, gain, w_up, w_down, saved, dxo, tag):
    hm, u, act = saved
    du = _mm(dxo, w_down, mode="nt", name=f"mlp{tag}_dact", epi="dsqrelu", extra=u, out_dtype=BF16)
    dw_down = _dw(act, dxo, name=f"mlp{tag}_dwdown", axis=1)
    dhm = _mm(du, w_up, mode="nt", name=f"mlp{tag}_dhm")
    dw_up = _dw(hm, du, name=f"mlp{tag}_dwup", axis=2)
    dx, dgain = _norm_bwd(x[None], gain, dhm[None], name=f"mlp{tag}_dnorm", res=dxo)
    return dx[0], dgain[0], dw_up, dw_down


def _local_step(x, target, W, small, late=None):
    T, D = x.shape
    W = dict(W)
    pos = jnp.arange(T)
    mla_cos, mla_sin = _rope_cos_sin(pos, MLA_ROPE, ROPE_THETA)
    row_cos, row_sin = _rope_cos_sin(pos // GRID_W, GQA_DIM // 2, AXIAL_THETA)
    col_cos, col_sin = _rope_cos_sin(pos % GRID_W, GQA_DIM // 2, AXIAL_THETA)
    swa_cos, swa_sin = _rope_cos_sin(pos, SWA_ROT, ROPE_THETA)
    rope_q = _rope_tables(T, MLA_QK, [(MLA_NOPE, MLA_ROPE, mla_cos, mla_sin)])
    rope_kr = _rope_tables(T, MLA_ROPE, [(0, MLA_ROPE, mla_cos, mla_sin)])
    half = GQA_DIM // 2
    rope_ax = _rope_tables(T, GQA_DIM, [(0, half, row_cos, row_sin), (half, half, col_cos, col_sin)])
    rope_sw = _rope_tables(T, SWA_DIM, [(0, SWA_ROT, swa_cos, swa_sin)])
    o1 = MLA_Q_LORA
    o2 = o1 + MLA_KV_LORA
    o3 = o2 + MLA_ROPE
    o4 = o3 + GQA_HEADS * GQA_DIM
    o5 = o4 + GQA_KV * GQA_DIM
    sc_a, sc_g, sc_s = MLA_QK ** -0.5, GQA_DIM ** -0.5, SWA_DIM ** -0.5
    kv_w = MLA_NOPE + MLA_V

    h0 = _norm_fwd(x[None], small["even_norm"], name="even_norm")[0]
    proj = _mm(h0, W["even_w_in"], mode="nn", name="even_in")
    c_q, c_kv, kr_raw = proj[:, :o1], proj[:, o1:o2], proj[:, o2:o3]
    qg_raw = _heads(proj[:, o3:o4], GQA_HEADS, GQA_DIM)
    kg_raw = _heads(proj[:, o4:o5], GQA_KV, GQA_DIM)
    vg = _heads(proj[:, o5:], GQA_KV, GQA_DIM).astype(BF16)
    cqn = _norm_fwd(c_q[None], small["mla_q_lat_norm"], name="q_lat_norm")[0]
    ckvn = _norm_fwd(c_kv[None], small["mla_kv_lat_norm"], name="kv_lat_norm")[0]
    qa_raw = _heads(_mm(cqn, W["mla_w_uq"], mode="nn", name="mla_uq"), MLA_HEADS, MLA_QK)
    kv = _mm(ckvn, W["mla_w_ukv"], mode="nn", name="mla_ukv").reshape(T, MLA_HEADS, kv_w)
    kn_raw = kv[:, :, :MLA_NOPE].transpose(1, 0, 2)
    va = kv[:, :, MLA_NOPE:].transpose(1, 0, 2).astype(BF16)
    q_a = _norm_fwd(qa_raw, small["mla_q_norm"], name="mla_q_prep", rope=rope_q, out_scale=sc_a * LOG2E)
    k_n = _norm_fwd(kn_raw, small["mla_k_nope_norm"], name="mla_kn_prep")
    k_r = _norm_fwd(kr_raw[None], small["mla_k_rope_norm"], name="mla_kr_prep", rope=rope_kr)
    k_a = jnp.concatenate([k_n, jnp.broadcast_to(k_r, (MLA_HEADS, T, MLA_ROPE))], axis=-1)
    if late is None:
        o_a, lse_a, ob_a = _flash_fwd(q_a, k_a, va, name="mla_attn")
    else:
        o_a, lse_a, ob_a, gathered = _flash_fwd(q_a, k_a, va, name="mla_attn", gather=late.halves)
        W.update(late.weights(gathered))
    q_g = _norm_fwd(qg_raw, small["gqa_q_norm"], name="gqa_q_prep", rope=rope_ax, out_scale=sc_g * LOG2E)
    k_g = _norm_fwd(kg_raw, small["gqa_k_norm"], name="gqa_k_prep", rope=rope_ax)
    o_g, lse_g, ob_g = _flash_fwd(q_g, k_g, vg, name="gqa_attn")
    merged = jnp.concatenate([_unheads(ob_a), _unheads(ob_g)], axis=-1)
    x1 = _mm(merged, W["even_w_out"], mode="nn", name="even_out", epi="add", extra=x)
    x2, mlp0 = _mlp_fwd(x1, small["mlp_norm"][0], W["mlp_w_up0"], W["mlp_w_down0"], 0)

    h1 = _norm_fwd(x2[None], small["odd_norm"], name="odd_norm")[0]
    qkv = _mm(h1, W["odd_w_qkv"], mode="nn", name="odd_qkv")
    nq, nkk = SWA_HEADS * SWA_DIM, SWA_KV * SWA_DIM
    qs_raw = _heads(qkv[:, :nq], SWA_HEADS, SWA_DIM)
    ks_raw = _heads(qkv[:, nq:nq + nkk], SWA_KV, SWA_DIM)
    vs = _heads(qkv[:, nq + nkk:], SWA_KV, SWA_DIM).astype(BF16)
    q_s = _norm_fwd(qs_raw, small["swa_q_norm"], name="swa_q_prep", rope=rope_sw, out_scale=sc_s)
    k_s = _norm_fwd(ks_raw, small["swa_k_norm"], name="swa_k_prep", rope=rope_sw)
    sink = small["swa_sink"].reshape(SWA_HEADS, 1, 1)
    o_s, lse_s, ob_s = _swa_fwd(q_s, k_s, vs, sink, name="swa_attn")
    o_flat = _unheads(ob_s)
    x3 = _mm(o_flat, W["odd_w_out"], mode="nn", name="odd_out", epi="add", extra=x2)
    x4, mlp1 = _mlp_fwd(x3, small["mlp_norm"][1], W["mlp_w_up1"], W["mlp_w_down1"], 1)

    dy, loss_sum = _loss_head(x4, target, name="loss_head")
    gW, gs = {}, {}

    dx3, dg_m1, gW["mlp_w_up1"], gW["mlp_w_down1"] = _mlp_bwd(
        x3, small["mlp_norm"][1], W["mlp_w_up1"], W["mlp_w_down1"], mlp1, dy, 1)
    d_oflat = _mm(dx3, W["odd_w_out"], mode="nt", name="odd_dout")
    gW["odd_w_out"] = _dw(o_flat, dx3, name="odd_dwout", axis=1)
    do_s = _heads(d_oflat, SWA_HEADS, SWA_DIM)
    dq_s, dsink, delta_s, dob_s = _swa_dq(q_s, k_s, vs, o_s, do_s, lse_s, sink, name="swa_dq")
    dk_s, dv_s = _swa_dkv(q_s, k_s, vs, dob_s, lse_s, delta_s, name="swa_dkv")
    gs["swa_sink"] = dsink[:, 0, 0]
    dqs_raw, gs["swa_q_norm"] = _norm_bwd(qs_raw, small["swa_q_norm"], dq_s, name="swa_dq_prep", rope=rope_sw,
                                          out_dtype=BF16, dy_scale=sc_s)
    dks_raw, gs["swa_k_norm"] = _norm_bwd(ks_raw, small["swa_k_norm"], dk_s, name="swa_dk_prep", rope=rope_sw,
                                          out_dtype=BF16)
    dqkv = jnp.concatenate([_unheads(dqs_raw), _unheads(dks_raw), _unheads(dv_s)], axis=-1).astype(BF16)
    dh1 = _mm(dqkv, W["odd_w_qkv"], mode="nt", name="odd_dh")
    gW["odd_w_qkv"] = _dw(h1, dqkv, name="odd_dwqkv", axis=2)
    dx2, gs["odd_norm"] = _norm_bwd(x2[None], small["odd_norm"], dh1[None], name="odd_dnorm", res=dx3)
    dx2 = dx2[0]

    dx1, dg_m0, gW["mlp_w_up0"], gW["mlp_w_down0"] = _mlp_bwd(
        x1, small["mlp_norm"][0], W["mlp_w_up0"], W["mlp_w_down0"], mlp0, dx2, 0)
    gs["mlp_norm"] = jnp.stack([dg_m0, dg_m1])
    d_merged = _mm(dx1, W["even_w_out"], mode="nt", name="even_dout")
    gW["even_w_out"] = _dw(merged, dx1, name="even_dwout", axis=1)
    na = MLA_HEADS * MLA_V
    do_a = _heads(d_merged[:, :na], MLA_HEADS, MLA_V)
    do_g = _heads(d_merged[:, na:], GQA_HEADS, GQA_DIM)
    delta_a, dob_a = _delta(o_a, do_a, name="mla_delta")
    delta_g, dob_g = _delta(o_g, do_g, name="gqa_delta")
    if late is None:
        dq_a, dk_a, dv_a = _flash_bwd(q_a, k_a, va, dob_a, lse_a, delta_a, name="mla_attn_bwd")
        dq_g, dk_gp, dv_gp = _flash_bwd(q_g, k_g, vg, dob_g, lse_g, delta_g, name="gqa_attn_bwd")
    else:
        halves = late.split(gW)
        dq_a, dk_a, dv_a, from_sibling = _flash_bwd(q_a, k_a, va, dob_a, lse_a, delta_a, name="mla_attn_bwd",
                                                    rider=("exchange", halves))
        dq_g, dk_gp, dv_gp, late.scattered = _flash_bwd(q_g, k_g, vg, dob_g, lse_g, delta_g, name="gqa_attn_bwd",
                                                        rider=("scatter", late.pairs(halves, from_sibling)))
    grp = GQA_HEADS // GQA_KV
    ln2 = 1.0 / LOG2E
    dqg_raw, gs["gqa_q_norm"] = _norm_bwd(qg_raw, small["gqa_q_norm"], dq_g, name="gqa_dq_prep", rope=rope_ax,
                                          dy_scale=sc_g, out_dtype=BF16)
    dkg_raw, gs["gqa_k_norm"] = _norm_bwd(kg_raw, small["gqa_k_norm"], dk_gp, name="gqa_dk_prep", rope=rope_ax,
                                          group=grp, dy_scale=ln2, out_dtype=BF16)
    dvg = _group_sum(dv_gp, grp, name="gqa_dv_sum", out_dtype=BF16)
    dqa_raw, gs["mla_q_norm"] = _norm_bwd(qa_raw, small["mla_q_norm"], dq_a, name="mla_dq_prep", rope=rope_q,
                                          dy_scale=sc_a, out_dtype=BF16)
    dkn_raw, gs["mla_k_nope_norm"] = _norm_bwd(kn_raw, small["mla_k_nope_norm"], dk_a[:, :, :MLA_NOPE],
                                               name="mla_dkn_prep", dy_scale=ln2, out_dtype=BF16)
    dkr_raw, gs["mla_k_rope_norm"] = _norm_bwd(kr_raw[None], small["mla_k_rope_norm"], dk_a[:, :, MLA_NOPE:],
                                               name="mla_dkr_prep", rope=rope_kr, group=MLA_HEADS, dy_scale=ln2,
                                               out_dtype=BF16)
    dkv = jnp.concatenate([dkn_raw.transpose(1, 0, 2), dv_a.astype(BF16).transpose(1, 0, 2)], axis=-1)
    dkv = dkv.reshape(T, MLA_HEADS * kv_w).astype(BF16)
    dqa = _unheads(dqa_raw).astype(BF16)
    dckvn = _mm(dkv, W["mla_w_ukv"], mode="nt", name="mla_dckv")
    gW["mla_w_ukv"] = _dw(ckvn, dkv, name="mla_dwukv", axis=2)
    dcqn = _mm(dqa, W["mla_w_uq"], mode="nt", name="mla_dcq")
    gW["mla_w_uq"] = _dw(cqn, dqa, name="mla_dwuq", axis=2)
    dc_q, gs["mla_q_lat_norm"] = _norm_bwd(c_q[None], small["mla_q_lat_norm"], dcqn[None], name="q_lat_dnorm",
                                           out_dtype=BF16)
    dc_kv, gs["mla_kv_lat_norm"] = _norm_bwd(c_kv[None], small["mla_kv_lat_norm"], dckvn[None], name="kv_lat_dnorm",
                                             out_dtype=BF16)
    dproj = jnp.concatenate([dc_q[0], dc_kv[0], dkr_raw[0], _unheads(dqg_raw), _unheads(dkg_raw), _unheads(dvg)],
                            axis=-1).astype(BF16)
    dh0 = _mm(dproj, W["even_w_in"], mode="nt", name="even_dh")
    gW["even_w_in"] = _dw(h0, dproj, name="even_dwin", axis=2)
    dx0, gs["even_norm"] = _norm_bwd(x[None], small["even_norm"], dh0[None], name="even_dnorm", res=dx1)
    gs = {k: v.reshape(-1) for k, v in gs.items()}
    return loss_sum, dx0[0], gW, gs


BIG = (("even_w_in", 0, 2), ("mla_w_uq", 0, 2), ("mla_w_ukv", 0, 2), ("even_w_out", 0, 1), ("odd_w_qkv", 0, 2),
       ("odd_w_out", 0, 1), ("mlp_w_up", 0, 2), ("mlp_w_up", 1, 2), ("mlp_w_down", 0, 1), ("mlp_w_down", 1, 1))
GATHER_FIRST = ("even_w_in", "mla_w_uq", "mla_w_ukv")
GRADS_LAST = ("even_w_in", "mla_w_uq", "mla_w_ukv")
SMALL = ("even_norm", "mla_q_lat_norm", "mla_kv_lat_norm", "mla_q_norm", "mla_k_nope_norm", "mla_k_rope_norm",
         "gqa_q_norm", "gqa_k_norm", "odd_norm", "swa_q_norm", "swa_k_norm", "swa_sink", "mlp_norm")
def _pad_to(v, n):
    return v if v.shape[-1] == n else jnp.pad(v, [(0, 0)] * (v.ndim - 1) + [(0, n - v.shape[-1])])


def _big_key(name, layer, w):
    return name if w[name].shape[0] == 1 else f"{name}{layer}"


def _pack_rows(flat, rows=8):
    n = flat.shape[0]
    padded = -(-n // (rows * LANES)) * rows * LANES
    return _pad_to(flat, padded).reshape(-1, LANES)


def kernel(x, even_norm, even_w_in, mla_q_lat_norm, mla_kv_lat_norm, mla_w_uq, mla_w_ukv, mla_q_norm, mla_k_nope_norm, mla_k_rope_norm, gqa_q_norm, gqa_k_norm, even_w_out, odd_norm, odd_w_qkv, swa_q_norm, swa_k_norm, swa_sink, odd_w_out, mlp_norm, mlp_w_up, mlp_w_down, loss_target, m_even_norm, m_even_w_in, m_mla_q_lat_norm, m_mla_kv_lat_norm, m_mla_w_uq, m_mla_w_ukv, m_mla_q_norm, m_mla_k_nope_norm, m_mla_k_rope_norm, m_gqa_q_norm, m_gqa_k_norm, m_even_w_out, m_odd_norm, m_odd_w_qkv, m_swa_q_norm, m_swa_k_norm, m_swa_sink, m_odd_w_out, m_mlp_norm, m_mlp_w_up, m_mlp_w_down, v_even_norm, v_even_w_in, v_mla_q_lat_norm, v_mla_kv_lat_norm, v_mla_w_uq, v_mla_w_ukv, v_mla_q_norm, v_mla_k_nope_norm, v_mla_k_rope_norm, v_gqa_q_norm, v_gqa_k_norm, v_even_w_out, v_odd_norm, v_odd_w_qkv, v_swa_q_norm, v_swa_k_norm, v_swa_sink, v_odd_w_out, v_mlp_norm, v_mlp_w_up, v_mlp_w_down):
    w = dict(even_norm=even_norm, even_w_in=even_w_in, mla_q_lat_norm=mla_q_lat_norm, mla_kv_lat_norm=mla_kv_lat_norm,
             mla_w_uq=mla_w_uq, mla_w_ukv=mla_w_ukv, mla_q_norm=mla_q_norm, mla_k_nope_norm=mla_k_nope_norm,
             mla_k_rope_norm=mla_k_rope_norm, gqa_q_norm=gqa_q_norm, gqa_k_norm=gqa_k_norm, even_w_out=even_w_out,
             odd_norm=odd_norm, odd_w_qkv=odd_w_qkv, swa_q_norm=swa_q_norm, swa_k_norm=swa_k_norm, swa_sink=swa_sink,
             odd_w_out=odd_w_out, mlp_norm=mlp_norm, mlp_w_up=mlp_w_up, mlp_w_down=mlp_w_down)
    m = dict(even_norm=m_even_norm, even_w_in=m_even_w_in, mla_q_lat_norm=m_mla_q_lat_norm,
             mla_kv_lat_norm=m_mla_kv_lat_norm, mla_w_uq=m_mla_w_uq, mla_w_ukv=m_mla_w_ukv, mla_q_norm=m_mla_q_norm,
             mla_k_nope_norm=m_mla_k_nope_norm, mla_k_rope_norm=m_mla_k_rope_norm, gqa_q_norm=m_gqa_q_norm,
             gqa_k_norm=m_gqa_k_norm, even_w_out=m_even_w_out, odd_norm=m_odd_norm, odd_w_qkv=m_odd_w_qkv,
             swa_q_norm=m_swa_q_norm, swa_k_norm=m_swa_k_norm, swa_sink=m_swa_sink, odd_w_out=m_odd_w_out,
             mlp_norm=m_mlp_norm, mlp_w_up=m_mlp_w_up, mlp_w_down=m_mlp_w_down)
    v = dict(even_norm=v_even_norm, even_w_in=v_even_w_in, mla_q_lat_norm=v_mla_q_lat_norm,
             mla_kv_lat_norm=v_mla_kv_lat_norm, mla_w_uq=v_mla_w_uq, mla_w_ukv=v_mla_w_ukv, mla_q_norm=v_mla_q_norm,
             mla_k_nope_norm=v_mla_k_nope_norm, mla_k_rope_norm=v_mla_k_rope_norm, gqa_q_norm=v_gqa_q_norm,
             gqa_k_norm=v_gqa_k_norm, even_w_out=v_even_w_out, odd_norm=v_odd_norm, odd_w_qkv=v_odd_w_qkv,
             swa_q_norm=v_swa_q_norm, swa_k_norm=v_swa_k_norm, swa_sink=v_swa_sink, odd_w_out=v_odd_w_out,
             mlp_norm=v_mlp_norm, mlp_w_up=v_mlp_w_up, mlp_w_down=v_mlp_w_down)
    xi, yi, ci = _position()
    chip = 2 * xi + yi
    T, D = x.shape[1], x.shape[2]

    c_idx = ci.reshape(1).astype(jnp.int32)
    key_of = lambda entry: _big_key(entry[0], entry[1], w)
    first_use = [e for e in BIG if e[0] in GATHER_FIRST]
    later_use = [e for e in BIG if e[0] not in GATHER_FIRST]
    early_grads = [e for e in BIG if e[0] not in GRADS_LAST]
    last_grads = [e for e in BIG if e[0] in GRADS_LAST]

    as_bf16 = {}

    def halves_of(entries):
        out = []
        for name, layer, _ in entries:
            layers, ks, ns = w[name].shape
            if name not in as_bf16:
                as_bf16[name] = _to_bf16(w[name].reshape(layers * ks, ns), name=f"to_bf16_{name}")
            out.append(as_bf16[name].reshape(layers, 2, ks // 2, ns)[layer])
        return out

    def weights_of(entries, gathered):
        out = {}
        for (name, layer, axis), g in zip(entries, gathered):
            ks, ns = w[name].shape[1:]
            stacked = g.reshape(N_CHIPS, ks, ns)
            if axis == 1:
                out[_big_key(name, layer, w)] = stacked.reshape(N_CHIPS * ks, ns)
            else:
                out[_big_key(name, layer, w)] = stacked.transpose(1, 0, 2).reshape(ks, N_CHIPS * ns)
        return out

    def split_halves(entries, gW):
        out = []
        for entry in entries:
            _, ks, ns = gW[key_of(entry)].shape
            out.append(gW[key_of(entry)].reshape(N_CHIPS, 2, ks // 2, ns))
        return out

    def pair_sums(entries, g_all, from_sibling):
        return [_pair_add(g, r, c_idx, name=f"grad_pair_add_{key_of(e)}") for e, g, r in zip(entries, g_all, from_sibling)]

    class _Late:
        halves = halves_of(later_use)
        scattered = None

        @staticmethod
        def weights(gathered):
            return weights_of(later_use, gathered)

        @staticmethod
        def split(gW):
            return split_halves(early_grads, gW)

        @staticmethod
        def pairs(g_all, from_sibling):
            return pair_sums(early_grads, g_all, from_sibling)

    late = _Late()
    W = weights_of(first_use, _all_gather_halves(halves_of(first_use), name="weights_all_gather"))

    odd_full = jnp.zeros((N_CHIPS, D // N_CHIPS), F32).at[chip].set(jnp.where(ci == 0, 1.0, 0.0) * w["odd_norm"][0])
    odd_full = _all_reduce_small(_pack_rows(odd_full.reshape(-1)), name="odd_norm_gather").reshape(-1)[:D]
    small = {name: w[name][0] for name in SMALL if name not in ("mlp_norm", "odd_norm")}
    small["mlp_norm"] = w["mlp_norm"]
    small["odd_norm"] = odd_full

    loss_sum, grad_x, gW, gs = _local_step(x[0], loss_target[0], W, small, late)

    loss_local = 0.5 * loss_sum.reshape(1) / D
    small_sizes = [(name, int(gs[name].shape[0])) for name in SMALL]
    ar_in = jnp.concatenate([_pad_to(loss_local, LANES)] + [gs[name] for name in SMALL])
    ar_out = _all_reduce_small(_pack_rows(ar_in), name="small_all_reduce").reshape(-1)
    loss = ar_out[0]
    g_small, off = {}, LANES
    for name, n in small_sizes:
        g_small[name] = ar_out[off:off + n]
        off += n
    shard_d = D // N_CHIPS
    g_small["odd_norm"] = lax.dynamic_slice(g_small["odd_norm"], (chip * shard_d,), (shard_d,))

    from_chips = dict(zip(map(key_of, early_grads), late.scattered))
    last_halves = split_halves(last_grads, gW)
    last_pairs = pair_sums(last_grads, last_halves, _sibling_exchange(last_halves, name="grad_sibling_exchange"))
    last_scattered = _chip_scatter(last_pairs, name="grad_chip_scatter")
    from_chips.update(zip(map(key_of, last_grads), last_scattered))
    keys = [key_of(e) for e in BIG]
    reduced = [_sum_chips(from_chips[key], c_idx, name=f"grad_chip_sum_{key}") for key in keys]
    shared = _sibling_share(reduced, name="grad_sibling_share")
    g_shards = {}
    for (name, layer, _), f in zip(BIG, shared):
        g_shards.setdefault(name, []).append(f.reshape(w[name].shape[1:]))

    grads, deltas, new_m, new_v = {}, {}, {}, {}
    for name in g_shards:
        shape = w[name].shape
        g = jnp.stack(g_shards[name])
        grads[name] = g
        two_d = (shape[0] * shape[1], shape[2])
        d_, m_, v_ = _adamw(w[name].reshape(two_d), g.reshape(two_d), m[name].reshape(two_d), v[name].reshape(two_d),
                            name=f"adamw_{name}")
        deltas[name], new_m[name], new_v[name] = d_.reshape(shape), m_.reshape(shape), v_.reshape(shape)
    pack_small = lambda d: _pack_rows(jnp.concatenate([d[name].reshape(-1) for name in SMALL]))
    for name in SMALL:
        grads[name] = g_small[name].reshape(w[name].shape)
    d_, m_, v_ = _adamw(pack_small(w), pack_small(grads), pack_small(m), pack_small(v), name="adamw_small")
    d_, m_, v_ = d_.reshape(-1), m_.reshape(-1), v_.reshape(-1)
    off = 0
    for name in SMALL:
        n = int(np.prod(w[name].shape))
        deltas[name] = d_[off:off + n].reshape(w[name].shape)
        new_m[name] = m_[off:off + n].reshape(w[name].shape)
        new_v[name] = v_[off:off + n].reshape(w[name].shape)
        off += n

    order = ("even_norm", "even_w_in", "mla_q_lat_norm", "mla_kv_lat_norm", "mla_w_uq", "mla_w_ukv", "mla_q_norm",
             "mla_k_nope_norm", "mla_k_rope_norm", "gqa_q_norm", "gqa_k_norm", "even_w_out", "odd_norm", "odd_w_qkv",
             "swa_q_norm", "swa_k_norm", "swa_sink", "odd_w_out", "mlp_norm", "mlp_w_up", "mlp_w_down")
    outs = [loss, grad_x[None]]
    for group in (grads, deltas, new_m, new_v):
        outs += [group[name] for name in order]
    return tuple(outs)
```

```python
import math

import numpy as np
import jax
import jax.numpy as jnp
from jax import lax
from jax.experimental import pallas as pl
from jax.experimental.pallas import tpu as pltpu

F32 = jnp.float32
BF16 = jnp.bfloat16
MESH = pl.DeviceIdType.MESH

VMEM_BYTES_V7X = 64 * 1024 * 1024
LANES = 128
SUBLANES_BF16 = 16

GRID_W = 64
NORM_EPS = 1e-6
ROPE_THETA = 500000.0
AXIAL_THETA = 10000.0
MLA_HEADS = 8
MLA_Q_LORA = 512
MLA_KV_LORA = 256
MLA_NOPE = 128
MLA_ROPE = 64
MLA_QK = MLA_NOPE + MLA_ROPE
MLA_V = 128
GQA_HEADS = 8
GQA_KV = 2
GQA_DIM = 128
SWA_HEADS = 32
SWA_KV = 4
SWA_DIM = 64
SWA_WINDOW = 128
SWA_ROT = SWA_DIM // 4
SWA_BLOCK = 128
SWA_HEAD_PARTS = 2
SWA_GROUPS = 4
ADAM_LR = 0.001
ADAM_B1 = 0.9
ADAM_B2 = 0.999
ADAM_EPS = 1e-08
ADAM_WD = 0.01
ADAM_STEP = 10
N_CHIPS = 4


def _tile(dim, cap, mult=LANES):
    if dim <= cap:
        return dim
    t = (cap // mult) * mult
    while t >= mult:
        if dim % t == 0:
            return t
        t -= mult
    return dim


def _params(dims, vmem_estimate):
    limit = int(min(max(vmem_estimate * 1.25 + (4 << 20), 32 << 20), VMEM_BYTES_V7X - (6 << 20)))
    return pltpu.CompilerParams(dimension_semantics=dims, vmem_limit_bytes=limit)


def _nbytes(shape, dtype):
    return int(np.prod(shape)) * jnp.dtype(dtype).itemsize


def _mm(a, b, *, mode, name, out_dtype=F32, epi=None, extra=None, split=1, caps=(1024, 1024, 2048)):
    if mode == "nn":
        (M, K), (K2, N) = a.shape, b.shape
    elif mode == "nt":
        (M, K), (N, K2) = a.shape, b.shape
    else:
        (K, M), (K2, N) = a.shape, b.shape
    assert K == K2, (a.shape, b.shape, mode)
    assert N % split == 0
    ns = N // split
    tn, tk = _tile(ns, caps[1]), _tile(K, caps[2])
    tm = _tile(M, min(caps[0], max(LANES, caps[0] * caps[1] // tn)))
    nj_per = ns // tn
    grid = (M // tm, N // tn, K // tk)
    nk = grid[2]
    if mode == "nn":
        a_spec = pl.BlockSpec((tm, tk), lambda i, j, k: (i, k))
        b_spec = pl.BlockSpec((tk, tn), lambda i, j, k: (k, j))
        dn = (((1,), (0,)), ((), ()))
    elif mode == "nt":
        a_spec = pl.BlockSpec((tm, tk), lambda i, j, k: (i, k))
        b_spec = pl.BlockSpec((tn, tk), lambda i, j, k: (j, k))
        dn = (((1,), (1,)), ((), ()))
    else:
        a_spec = pl.BlockSpec((tk, tm), lambda i, j, k: (k, i))
        b_spec = pl.BlockSpec((tk, tn), lambda i, j, k: (k, j))
        dn = (((0,), (0,)), ((), ()))
    if split == 1:
        o_spec = pl.BlockSpec((tm, tn), lambda i, j, k: (i, j))
        o_shape = (M, N)
    else:
        o_spec = pl.BlockSpec((None, tm, tn), lambda i, j, k: (j // nj_per, i, j % nj_per))
        o_shape = (split, M, ns)
    mn_spec = pl.BlockSpec((tm, tn), lambda i, j, k: (i, j))
    in_specs, args = [a_spec, b_spec], [a, b]
    if epi in ("add", "dsqrelu"):
        in_specs.append(mn_spec)
        args.append(extra)
    if epi == "sqrelu":
        out_shape = (jax.ShapeDtypeStruct(o_shape, BF16), jax.ShapeDtypeStruct(o_shape, BF16))
        out_specs = (o_spec, o_spec)
        n_out = 2
    else:
        out_shape = jax.ShapeDtypeStruct(o_shape, out_dtype)
        out_specs = o_spec
        n_out = 1

    def body(*refs):
        a_ref, b_ref = refs[0], refs[1]
        e_ref = refs[2] if len(args) == 3 else None
        outs = refs[len(args):len(args) + n_out]

        def finish(acc):
            if epi is None:
                outs[0][...] = acc.astype(outs[0].dtype)
            elif epi == "add":
                outs[0][...] = (e_ref[...] + acc).astype(outs[0].dtype)
            elif epi == "sqrelu":
                r = jnp.maximum(acc, 0.0)
                outs[0][...] = acc.astype(BF16)
                outs[1][...] = (r * r).astype(BF16)
            else:
                u = e_ref[...].astype(F32)
                outs[0][...] = (acc * (2.0 * jnp.maximum(u, 0.0))).astype(outs[0].dtype)

        prod = lax.dot_general(a_ref[...].astype(BF16), b_ref[...].astype(BF16), dn, preferred_element_type=F32)
        if nk == 1:
            finish(prod)
            return
        acc_ref = refs[-1]
        k = pl.program_id(2)

        @pl.when(k == 0)
        def _():
            acc_ref[...] = prod

        @pl.when((k != 0) & (k != nk - 1))
        def _():
            acc_ref[...] += prod

        @pl.when(k == nk - 1)
        def _():
            finish(acc_ref[...] + prod)

    est = 2 * (_nbytes((tm, tk), a.dtype) + _nbytes((tk, tn), b.dtype)) + _nbytes((tm, tn), F32)
    est += 2 * n_out * _nbytes((tm, tn), out_dtype if n_out == 1 else BF16)
    if len(args) == 3:
        est += 2 * _nbytes((tm, tn), extra.dtype)
    est += 3 * _nbytes((tm, tn), F32)
    return pl.pallas_call(
        body, name=name, grid=grid, in_specs=in_specs, out_specs=out_specs, out_shape=out_shape,
        scratch_shapes=[] if nk == 1 else [pltpu.VMEM((tm, tn), F32)],
        compiler_params=_params(("parallel", "parallel", "arbitrary"), est),
    )(*args)


def _mm_dnorm(a, b, x, gain, res, *, name):
    (M, K), (N, K2) = a.shape, b.shape
    assert K == K2 and x.shape == (M, N) and res.shape == (M, N)
    tm, tk = _tile(M, 512), _tile(K, 1024)
    nk = K // tk
    g2 = gain.reshape(1, N).astype(F32)

    def body(a_ref, b_ref, x_ref, g_ref, r_ref, dx_ref, dg_ref, acc_ref):
        i, k = pl.program_id(0), pl.program_id(1)
        prod = lax.dot_general(a_ref[...].astype(BF16), b_ref[...].astype(BF16), NT_DIMS, preferred_element_type=F32)

        @pl.when(k == 0)
        def _():
            acc_ref[...] = prod

        @pl.when(k != 0)
        def _():
            acc_ref[...] += prod

        @pl.when(k == nk - 1)
        def _():
            dy = acc_ref[...]
            xv = x_ref[...]
            r = lax.rsqrt(jnp.mean(xv * xv, axis=-1, keepdims=True) + NORM_EPS)
            xhat = xv * r
            part = jnp.sum(dy * xhat, axis=0, keepdims=True)
            dxh = dy * g_ref[...]
            dx_ref[...] = r * (dxh - xhat * jnp.mean(dxh * xhat, axis=-1, keepdims=True)) + r_ref[...]

            @pl.when(i == 0)
            def _():
                dg_ref[...] = part

            @pl.when(i != 0)
            def _():
                dg_ref[...] += part

    row = pl.BlockSpec((tm, N), lambda i, k: (i, 0))
    est = 2 * (_nbytes((tm, tk), a.dtype) + _nbytes((N, tk), b.dtype)) + 10 * _nbytes((tm, N), F32)
    return pl.pallas_call(
        body, name=name, grid=(M // tm, nk),
        in_specs=[pl.BlockSpec((tm, tk), lambda i, k: (i, k)), pl.BlockSpec((N, tk), lambda i, k: (0, k)), row,
                  pl.BlockSpec((1, N), lambda i, k: (0, 0)), row],
        out_specs=(row, pl.BlockSpec((1, N), lambda i, k: (0, 0))),
        out_shape=(jax.ShapeDtypeStruct((M, N), F32), jax.ShapeDtypeStruct((1, N), F32)),
        scratch_shapes=[pltpu.VMEM((tm, N), F32)],
        compiler_params=_params(("arbitrary", "arbitrary"), est),
    )(a, b, x, g2, res)


def _perm(y, p):
    hi = y.astype(BF16)
    lo = (y - hi.astype(F32)).astype(BF16)
    d = lambda t: jnp.dot(t, p, preferred_element_type=F32)
    return d(hi) + d(lo)


def _rows_tile(T, d):
    return _tile(T, 2048 if d <= 256 else 512, 128)


def _norm_fwd(x, gain, *, name, rope=None, out_dtype=BF16, out_scale=None):
    H, T, d = x.shape
    tm = _rows_tile(T, d)
    g2 = gain.reshape(1, d).astype(F32)
    in_specs = [pl.BlockSpec((None, tm, d), lambda h, i: (h, i, 0)), pl.BlockSpec((1, d), lambda h, i: (0, 0))]
    args = [x, g2]
    if rope is not None:
        in_specs += [pl.BlockSpec((tm, d), lambda h, i: (i, 0)), pl.BlockSpec((tm, d), lambda h, i: (i, 0)),
                     pl.BlockSpec((d, d), lambda h, i: (0, 0))]
        args += list(rope)

    def body(*refs):
        x_ref, g_ref = refs[0], refs[1]
        o_ref = refs[-1]
        xv = x_ref[...]
        y = xv * lax.rsqrt(jnp.mean(xv * xv, axis=-1, keepdims=True) + NORM_EPS)
        y = y * g_ref[...]
        if rope is not None:
            c_ref, s_ref, p_ref = refs[2], refs[3], refs[4]
            y = y * c_ref[...] + _perm(y, p_ref[...]) * s_ref[...]
        if out_scale is not None:
            y = y * out_scale
        o_ref[...] = y.astype(o_ref.dtype)

    est = 2 * (_nbytes((tm, max(d, LANES)), F32) * (3 if rope is not None else 1) + _nbytes((tm, max(d, LANES)), out_dtype))
    est += 6 * _nbytes((tm, max(d, LANES)), F32)
    return pl.pallas_call(
        body, name=name, grid=(H, T // tm), in_specs=in_specs,
        out_specs=pl.BlockSpec((None, tm, d), lambda h, i: (h, i, 0)),
        out_shape=jax.ShapeDtypeStruct((H, T, d), out_dtype),
        compiler_params=_params(("parallel", "parallel"), est),
    )(*args)


def _norm_bwd(x, gain, dy, *, name, rope=None, group=1, res=None, out_dtype=F32, dy_scale=None):
    H, T, d = x.shape
    assert dy.shape == (H * group, T, d), (dy.shape, x.shape, group)
    tm = _rows_tile(T, d)
    g2 = gain.reshape(1, d).astype(F32)
    in_specs = [pl.BlockSpec((None, tm, d), lambda h, i: (h, i, 0)), pl.BlockSpec((1, d), lambda h, i: (0, 0)),
                pl.BlockSpec((group, tm, d), lambda h, i: (h, i, 0))]
    args = [x, g2, dy]
    if rope is not None:
        in_specs += [pl.BlockSpec((tm, d), lambda h, i: (i, 0)), pl.BlockSpec((tm, d), lambda h, i: (i, 0)),
                     pl.BlockSpec((d, d), lambda h, i: (0, 0))]
        args += list(rope)
    if res is not None:
        assert H == 1
        in_specs.append(pl.BlockSpec((tm, d), lambda h, i: (i, 0)))
        args.append(res)
    n_in = len(args)

    def body(*refs):
        x_ref, g_ref, dy_ref = refs[0], refs[1], refs[2]
        dx_ref, dg_ref = refs[n_in], refs[n_in + 1]
        first = (pl.program_id(0) == 0) & (pl.program_id(1) == 0)

        @pl.when(first)
        def _():
            dg_ref[...] = jnp.zeros_like(dg_ref)

        dyv = dy_ref[0].astype(F32)
        for g in range(1, group):
            dyv = dyv + dy_ref[g].astype(F32)
        if dy_scale is not None:
            dyv = dyv * dy_scale
        pos = 3
        if rope is not None:
            c_ref, s_ref, p_ref = refs[3], refs[4], refs[5]
            pos = 6
            dyv = dyv * c_ref[...] + _perm(dyv * s_ref[...], p_ref[...])
        xv = x_ref[...]
        r = lax.rsqrt(jnp.mean(xv * xv, axis=-1, keepdims=True) + NORM_EPS)
        xhat = xv * r
        dg_ref[...] += jnp.sum(dyv * xhat, axis=0, keepdims=True)
        dxh = dyv * g_ref[...]
        dx = r * (dxh - xhat * jnp.mean(dxh * xhat, axis=-1, keepdims=True))
        if res is not None:
            dx = dx + refs[pos][...]
        dx_ref[...] = dx.astype(dx_ref.dtype)

    wide = max(d, LANES)
    est = 2 * _nbytes((tm, wide), F32) * (2 + group + (2 if rope is not None else 0) + (1 if res is not None else 0))
    est += 8 * _nbytes((tm, wide), F32)
    return pl.pallas_call(
        body, name=name, grid=(H, T // tm), in_specs=in_specs,
        out_specs=(pl.BlockSpec((None, tm, d), lambda h, i: (h, i, 0)), pl.BlockSpec((1, d), lambda h, i: (0, 0))),
        out_shape=(jax.ShapeDtypeStruct((H, T, d), out_dtype), jax.ShapeDtypeStruct((1, d), F32)),
        compiler_params=_params(("arbitrary", "arbitrary"), est),
    )(*args)


def _group_sum(x, group, *, name, out_dtype=F32):
    HG, T, d = x.shape
    H = HG // group
    tm = _rows_tile(T, d)

    def body(x_ref, o_ref):
        acc = x_ref[0]
        for g in range(1, group):
            acc = acc + x_ref[g]
        o_ref[...] = acc.astype(o_ref.dtype)

    est = 2 * (group + 1) * _nbytes((tm, max(d, LANES)), F32)
    return pl.pallas_call(
        body, name=name, grid=(H, T // tm),
        in_specs=[pl.BlockSpec((group, tm, d), lambda h, i: (h, i, 0))],
        out_specs=pl.BlockSpec((None, tm, d), lambda h, i: (h, i, 0)),
        out_shape=jax.ShapeDtypeStruct((H, T, d), out_dtype),
        compiler_params=_params(("parallel", "parallel"), est),
    )(x)


def _delta(o, do, *, name):
    H, T, d = o.shape
    tm = _rows_tile(T, d)

    def body(o_ref, do_ref, dl_ref, dob_ref):
        dov = do_ref[...]
        dl = jnp.sum(o_ref[...] * dov, axis=-1, keepdims=True)
        dl_ref[...] = jnp.broadcast_to(dl, (tm, LANES))
        dob_ref[...] = dov.astype(BF16)

    spec = pl.BlockSpec((None, tm, d), lambda h, i: (h, i, 0))
    est = 2 * (3 * _nbytes((tm, max(d, LANES)), F32) + _nbytes((tm, LANES), F32))
    return pl.pallas_call(
        body, name=name, grid=(H, T // tm), in_specs=[spec, spec],
        out_specs=(pl.BlockSpec((None, tm, LANES), lambda h, i: (h, i, 0)), spec),
        out_shape=(jax.ShapeDtypeStruct((H, T, LANES), F32), jax.ShapeDtypeStruct((H, T, d), BF16)),
        compiler_params=_params(("parallel", "parallel"), est),
    )(o, do)


NT_DIMS = (((1,), (1,)), ((), ()))
TN_DIMS = (((0,), (0,)), ((), ()))
LOG2E = math.log2(math.e)
FLASH_CHUNK = 256
FLASH_ROW_PARTS = 4


def _flash_fwd(q, k, v, *, name, gather=()):
    H, T, dk = q.shape
    Hkv, _, dv = v.shape
    G = H // Hkv
    tq, tk = _tile(T, 1024), _tile(T, 4096)
    tp = _tile(tq, tq // FLASH_ROW_PARTS, SUBLANES_BF16)
    nk = T // tk

    n_r = len(gather)
    grid = (H, T // tq, nk)
    assert n_r == 0 or H >= 2

    def body(*refs):
        q_ref, k_ref, v_ref = refs[:3]
        o_ref, lse_ref, ob_ref = refs[3 + n_r:6 + n_r]
        m_ref, l_ref, acc_ref = refs[6 + 2 * n_r:9 + 2 * n_r]
        hi, qi, ki = pl.program_id(0), pl.program_id(1), pl.program_id(2)
        if n_r:
            ag_start, ag_forward, ag_finish = _gather_plan(refs[3:3 + n_r], refs[6 + n_r:6 + 2 * n_r],
                                                           *refs[9 + 2 * n_r:])
            pl.when((hi == 0) & (qi == 0) & (ki == 0))(ag_start)
            pl.when((hi == grid[0] - 1) & (qi == 0) & (ki == 0))(ag_forward)

        @pl.when(ki == 0)
        def _():
            m_ref[...] = jnp.full_like(m_ref, -jnp.inf)
            l_ref[...] = jnp.zeros_like(l_ref)
            acc_ref[...] = jnp.zeros_like(acc_ref)

        kv, vv = k_ref[...], v_ref[...]
        parts = [slice(part * tp, (part + 1) * tp) for part in range(tq // tp)]
        m_prev = [m_ref[rows, :] for rows in parts]
        l_prev = [l_ref[rows, :] for rows in parts]
        a_prev = [acc_ref[rows, :] for rows in parts]
        ss = [lax.dot_general(q_ref[rows, :], kv, NT_DIMS, preferred_element_type=F32) for rows in parts]
        m_new = [jnp.maximum(m, jnp.max(s, axis=-1, keepdims=True)) for m, s in zip(m_prev, ss)]
        alpha = [jnp.exp2(m - mn) for m, mn in zip(m_prev, m_new)]
        ps = [jnp.exp2(s - mn) for s, mn in zip(ss, m_new)]
        l_new = [a * l + jnp.sum(p, axis=-1, keepdims=True) for a, l, p in zip(alpha, l_prev, ps)]
        pv = [jnp.dot(p.astype(BF16), vv, preferred_element_type=F32) for p in ps]
        for rows, mn, ln, a, acc, o in zip(parts, m_new, l_new, alpha, a_prev, pv):
            m_ref[rows, :] = mn
            l_ref[rows, :] = ln
            acc_ref[rows, :] = a * acc + o

        @pl.when(ki == nk - 1)
        def _():
            l = l_ref[...]
            o = acc_ref[...] / l
            o_ref[...] = o
            ob_ref[...] = o.astype(BF16)
            lse_ref[...] = jnp.broadcast_to(m_ref[...] + jnp.log(l) * LOG2E, (tq, LANES))

        if n_r:
            pl.when((hi == grid[0] - 1) & (qi == grid[1] - 1) & (ki == grid[2] - 1))(ag_finish)

    est = 2 * (_nbytes((tq, dk), BF16) + _nbytes((tk, dk + dv), BF16) + _nbytes((tq, dv + LANES), F32))
    est += 4 * _nbytes((tq, tk), F32) + 3 * _nbytes((tq, dv + 3 * LANES), F32)
    outs = pl.pallas_call(
        body, name=name, grid=grid,
        in_specs=[pl.BlockSpec((None, tq, dk), lambda h, i, j: (h, i, 0)),
                  pl.BlockSpec((None, tk, dk), lambda h, i, j: (h // G, j, 0)),
                  pl.BlockSpec((None, tk, dv), lambda h, i, j: (h // G, j, 0))] + [HBM_SPEC] * n_r,
        out_specs=[pl.BlockSpec((None, tq, dv), lambda h, i, j: (h, i, 0)),
                   pl.BlockSpec((None, tq, LANES), lambda h, i, j: (h, i, 0)),
                   pl.BlockSpec((None, tq, dv), lambda h, i, j: (h, i, 0))] + [HBM_SPEC] * n_r,
        out_shape=[jax.ShapeDtypeStruct((H, T, dv), F32), jax.ShapeDtypeStruct((H, T, LANES), F32),
                   jax.ShapeDtypeStruct((H, T, dv), BF16)] + _gathered_shapes(gather),
        scratch_shapes=[pltpu.VMEM((tq, 1), F32), pltpu.VMEM((tq, 1), F32), pltpu.VMEM((tq, dv), F32)]
                       + (_gather_scratch(n_r) if n_r else []),
        compiler_params=_params(("arbitrary",) * 3 if n_r else ("parallel", "parallel", "arbitrary"), est),
    )(q, k, v, *gather)
    return (outs[0], outs[1], outs[2], outs[3:]) if n_r else (outs[0], outs[1], outs[2])


def _flash_bwd(q, k, v, do, lse2, delta, *, name, rider=None):
    H, T, dk = q.shape
    Hkv, _, dv = v.shape
    G = H // Hkv
    tq, tk = _tile(T, 1024), _tile(T, 2048)
    tc = _tile(tk, FLASH_CHUNK)

    kind, carried = rider if rider is not None else (None, ())
    n_r = len(carried)
    grid = (H, T // tk, T // tq)
    if kind == "scatter":
        plan, rider_scratch = _scatter_plan, _scatter_scratch(n_r)
        rider_shapes = [jax.ShapeDtypeStruct(p.shape, p.dtype) for p in carried]
    elif kind == "exchange":
        plan, rider_scratch = _exchange_plan, _exchange_scratch(n_r)
        rider_shapes = [jax.ShapeDtypeStruct((N_CHIPS,) + g.shape[2:], g.dtype) for g in carried]
    else:
        assert kind is None
        rider_scratch, rider_shapes = [], []

    def body(*refs):
        q_ref, k_ref, v_ref, do_ref, lse_ref, dl_ref = refs[:6]
        dq_ref, dk_ref, dv_ref = refs[6 + n_r:9 + n_r]
        hi, ki, qi = pl.program_id(0), pl.program_id(1), pl.program_id(2)
        if n_r:
            rider_start, rider_finish = plan(refs[6:6 + n_r], refs[9 + n_r:9 + 2 * n_r], *refs[9 + 2 * n_r:])
            pl.when((hi == 0) & (ki == 0) & (qi == 0))(rider_start)
        rows = pl.ds(pl.multiple_of(qi * tq, tq), tq)

        @pl.when(qi == 0)
        def _():
            dk_ref[...] = jnp.zeros_like(dk_ref)
            dv_ref[...] = jnp.zeros_like(dv_ref)

        @pl.when(ki == 0)
        def _():
            dq_ref[rows, :] = jnp.zeros((tq, dk), F32)

        qv, dov = q_ref[...], do_ref[...]
        lse2 = lse_ref[:, :1]
        dl = dl_ref[:, :1]
        chunks = [slice(c * tc, (c + 1) * tc) for c in range(tk // tc)]
        kcs = [k_ref[ks, :] for ks in chunks]
        vcs = [v_ref[ks, :] for ks in chunks]
        dv_old = [dv_ref[ks, :] for ks in chunks]
        dk_old = [dk_ref[ks, :] for ks in chunks]
        dq_old = dq_ref[rows, :]
        ss = [lax.dot_general(qv, kc, NT_DIMS, preferred_element_type=F32) for kc in kcs]
        dps = [lax.dot_general(dov, vc, NT_DIMS, preferred_element_type=F32) for vc in vcs]
        ps = [jnp.exp2(s - lse2) for s in ss]
        dss = [(p * (dp - dl)).astype(BF16) for p, dp in zip(ps, dps)]
        pbs = [p.astype(BF16) for p in ps]
        dvs = [lax.dot_general(pb, dov, TN_DIMS, preferred_element_type=F32) for pb in pbs]
        dks = [lax.dot_general(ds, qv, TN_DIMS, preferred_element_type=F32) for ds in dss]
        dqs = [jnp.dot(ds, kc, preferred_element_type=F32) for ds, kc in zip(dss, kcs)]
        for ks, old, new in zip(chunks, dv_old, dvs):
            dv_ref[ks, :] = old + new
        for ks, old, new in zip(chunks, dk_old, dks):
            dk_ref[ks, :] = old + new
        dq_c = dqs[0]
        for extra in dqs[1:]:
            dq_c = dq_c + extra
        dq_ref[rows, :] = dq_old + dq_c

        if n_r:
            pl.when((hi == grid[0] - 1) & (ki == grid[1] - 1) & (qi == grid[2] - 1))(rider_finish)

    est = 2 * (_nbytes((tq, dk + dv), BF16) + _nbytes((tk, dk + dv), BF16) + 2 * _nbytes((tq, LANES), F32))
    est += 2 * (_nbytes((T, dk), F32) + _nbytes((tk, dk + dv), F32)) + 10 * _nbytes((tq, tc), F32)
    outs = pl.pallas_call(
        body, name=name, grid=grid,
        in_specs=[pl.BlockSpec((None, tq, dk), lambda h, j, i: (h, i, 0)),
                  pl.BlockSpec((None, tk, dk), lambda h, j, i: (h // G, j, 0)),
                  pl.BlockSpec((None, tk, dv), lambda h, j, i: (h // G, j, 0)),
                  pl.BlockSpec((None, tq, dv), lambda h, j, i: (h, i, 0)),
                  pl.BlockSpec((None, tq, LANES), lambda h, j, i: (h, i, 0)),
                  pl.BlockSpec((None, tq, LANES), lambda h, j, i: (h, i, 0))] + [HBM_SPEC] * n_r,
        out_specs=[pl.BlockSpec((None, T, dk), lambda h, j, i: (h, 0, 0)),
                   pl.BlockSpec((None, tk, dk), lambda h, j, i: (h, j, 0)),
                   pl.BlockSpec((None, tk, dv), lambda h, j, i: (h, j, 0))] + [HBM_SPEC] * n_r,
        out_shape=[jax.ShapeDtypeStruct((H, T, dk), F32), jax.ShapeDtypeStruct((H, T, dk), F32),
                   jax.ShapeDtypeStruct((H, T, dv), F32)] + rider_shapes,
        scratch_shapes=rider_scratch,
        compiler_params=_params(("arbitrary", "arbitrary", "arbitrary"), est),
    )(q, k, v, do, lse2, delta, *carried)
    return (outs[0], outs[1], outs[2], outs[3:]) if n_r else tuple(outs)


def _swa_specs(G, d, n_blocks, lanes, gpb):
    B = SWA_BLOCK
    prev = lambda j, i: (j, jnp.maximum(i - 1, 0), 0)
    cur = lambda j, i: (j, i, 0)
    nxt = lambda j, i: (j, jnp.minimum(i + 1, n_blocks - 1), 0)
    q_specs = [pl.BlockSpec((gpb * G, B, lanes), m) for m in (prev, cur, nxt)]
    kv_specs = [pl.BlockSpec((gpb, B, d), m) for m in (prev, cur, nxt)]
    return q_specs, kv_specs, cur


def _swa_parts(G, gpb):
    gp = G // SWA_HEAD_PARTS
    return gp, [(g, slice(g * G + part * gp, g * G + (part + 1) * gp)) for g in range(gpb) for part in range(SWA_HEAD_PARTS)]


def _swa_bias(i, T):
    B = SWA_BLOCK
    row = lax.broadcasted_iota(jnp.int32, (B, 3 * B), 0)
    col = lax.broadcasted_iota(jnp.int32, (B, 3 * B), 1)
    kpos = (i - 1) * B + col
    valid = (col >= row) & (col <= row + 2 * SWA_WINDOW) & (kpos >= 0) & (kpos < T)
    return jnp.where(valid, 0.0, -jnp.inf)


def _swa_fwd(q, k, v, sink, *, name):
    Hq, T, d = q.shape
    Hkv = k.shape[0]
    G = Hq // Hkv
    B = SWA_BLOCK
    nb = T // B
    gpb = _tile(Hkv, SWA_GROUPS, 1)
    _, kv_specs, cur = _swa_specs(G, d, nb, d, gpb)

    def body(q_ref, k0, k1, k2, v0, v1, v2, sink_ref, o_ref, lse_ref, ob_ref):
        i = pl.program_id(1)
        kvs = [jnp.concatenate([k0[g], k1[g], k2[g]], axis=0) for g in range(gpb)]
        vvs = [jnp.concatenate([v0[g], v1[g], v2[g]], axis=0) for g in range(gpb)]
        bias = _swa_bias(i, T)[None]
        gp, parts = _swa_parts(G, gpb)
        sks = [sink_ref[hs] for _, hs in parts]
        ss = [lax.dot_general(q_ref[hs].reshape(gp * B, d), kvs[g], NT_DIMS, preferred_element_type=F32) for g, hs in parts]
        ss = [s.reshape(gp, B, 3 * B) + bias for s in ss]
        ms = [jnp.maximum(jnp.max(s, axis=-1, keepdims=True), sk) for s, sk in zip(ss, sks)]
        ps = [jnp.exp(s - m) for s, m in zip(ss, ms)]
        dens = [jnp.sum(p, axis=-1, keepdims=True) + jnp.exp(sk - m) for p, sk, m in zip(ps, sks, ms)]
        pns = [(p * (1.0 / den)).reshape(gp * B, 3 * B).astype(BF16) for p, den in zip(ps, dens)]
        os_ = [jnp.dot(pn, vvs[g], preferred_element_type=F32).reshape(gp, B, d) for pn, (g, _) in zip(pns, parts)]
        for (_, hs), o, m, den in zip(parts, os_, ms, dens):
            o_ref[hs] = o
            ob_ref[hs] = o.astype(BF16)
            lse_ref[hs] = jnp.broadcast_to(m + jnp.log(den), (gp, B, LANES))

    GG = gpb * G
    est = 2 * (_nbytes((GG, B, LANES), BF16) + 6 * gpb * _nbytes((B, LANES), BF16) + 2 * _nbytes((GG, B, LANES), F32))
    est += 8 * _nbytes((GG * B, 3 * B), F32)
    return pl.pallas_call(
        body, name=name, grid=(Hkv // gpb, nb),
        in_specs=[pl.BlockSpec((GG, B, d), cur)] + kv_specs + kv_specs + [pl.BlockSpec((GG, 1, 1), lambda j, i: (j, 0, 0))],
        out_specs=(pl.BlockSpec((GG, B, d), cur), pl.BlockSpec((GG, B, LANES), cur), pl.BlockSpec((GG, B, d), cur)),
        out_shape=(jax.ShapeDtypeStruct((Hq, T, d), F32), jax.ShapeDtypeStruct((Hq, T, LANES), F32),
                   jax.ShapeDtypeStruct((Hq, T, d), BF16)),
        compiler_params=_params(("parallel", "parallel"), est),
    )(q, k, k, k, v, v, v, sink)


def _swa_dq(q, k, v, o, do, lse, sink, *, name):
    Hq, T, d = q.shape
    Hkv = k.shape[0]
    G = Hq // Hkv
    B = SWA_BLOCK
    nb = T // B
    gpb = _tile(Hkv, SWA_GROUPS, 1)
    _, kv_specs, cur = _swa_specs(G, d, nb, d, gpb)

    def body(q_ref, do_ref, lse_ref, o_ref, k0, k1, k2, v0, v1, v2, sink_ref, dq_ref, dsink_ref, dl_ref, dob_ref):
        i = pl.program_id(1)
        kvs = [jnp.concatenate([k0[g], k1[g], k2[g]], axis=0) for g in range(gpb)]
        vvs = [jnp.concatenate([v0[g], v1[g], v2[g]], axis=0) for g in range(gpb)]
        bias = _swa_bias(i, T)[None]
        gp, parts = _swa_parts(G, gpb)
        lses = [lse_ref[hs, :, :1] for _, hs in parts]
        dovs = [do_ref[hs] for _, hs in parts]
        dls = [jnp.sum(o_ref[hs] * dov, axis=-1, keepdims=True) for (_, hs), dov in zip(parts, dovs)]
        dobs = [dov.astype(BF16) for dov in dovs]
        ss = [lax.dot_general(q_ref[hs].reshape(gp * B, d), kvs[g], NT_DIMS, preferred_element_type=F32) for g, hs in parts]
        dps = [lax.dot_general(dob.reshape(gp * B, d), vvs[g], NT_DIMS, preferred_element_type=F32)
               for dob, (g, _) in zip(dobs, parts)]
        ps = [jnp.exp(s.reshape(gp, B, 3 * B) + bias - lse) for s, lse in zip(ss, lses)]
        dss = [(p * (dp.reshape(gp, B, 3 * B) - dl)).reshape(gp * B, 3 * B).astype(BF16)
               for p, dp, dl in zip(ps, dps, dls)]
        dqs = [jnp.dot(ds, kvs[g], preferred_element_type=F32).reshape(gp, B, d) for ds, (g, _) in zip(dss, parts)]
        dsks = [-jnp.sum(jnp.exp(sink_ref[hs] - lse) * dl, axis=1, keepdims=True)
                for (_, hs), lse, dl in zip(parts, lses, dls)]

        @pl.when(i == 0)
        def _():
            dsink_ref[...] = jnp.zeros_like(dsink_ref)

        for (_, hs), dq, dsk, dl, dob in zip(parts, dqs, dsks, dls, dobs):
            dq_ref[hs] = dq
            dsink_ref[hs] += jnp.broadcast_to(dsk, (gp, 1, LANES))
            dl_ref[hs] = jnp.broadcast_to(dl, (gp, B, LANES))
            dob_ref[hs] = dob

    GG = gpb * G
    est = 2 * (2 * _nbytes((GG, B, LANES), BF16) + 6 * gpb * _nbytes((B, LANES), BF16) + 5 * _nbytes((GG, B, LANES), F32))
    est += 8 * _nbytes((GG * B, 3 * B), F32)
    q_spec, l_spec = pl.BlockSpec((GG, B, d), cur), pl.BlockSpec((GG, B, LANES), cur)
    return pl.pallas_call(
        body, name=name, grid=(Hkv // gpb, nb),
        in_specs=[q_spec, q_spec, l_spec, q_spec] + kv_specs + kv_specs + [pl.BlockSpec((GG, 1, 1), lambda j, i: (j, 0, 0))],
        out_specs=(q_spec, pl.BlockSpec((GG, 1, LANES), lambda j, i: (j, 0, 0)), l_spec, q_spec),
        out_shape=(jax.ShapeDtypeStruct((Hq, T, d), F32), jax.ShapeDtypeStruct((Hq, 1, LANES), F32),
                   jax.ShapeDtypeStruct((Hq, T, LANES), F32), jax.ShapeDtypeStruct((Hq, T, d), BF16)),
        compiler_params=_params(("arbitrary", "arbitrary"), est),
    )(q, do, lse, o, k, k, k, v, v, v, sink)


def _swa_dkv(q, k, v, do, lse, delta, *, name):
    Hq, T, d = q.shape
    Hkv = k.shape[0]
    G = Hq // Hkv
    B = SWA_BLOCK
    nb = T // B
    gpb = _tile(Hkv, SWA_GROUPS, 1)
    q_specs, _, cur = _swa_specs(G, d, nb, d, gpb)
    l_specs, _, _ = _swa_specs(G, d, nb, LANES, gpb)

    def body(k_ref, v_ref, q0, q1, q2, d0, d1, d2, l0, l1, l2, e0, e1, e2, dk_ref, dv_ref):
        b = pl.program_id(1)
        row = lax.broadcasted_iota(jnp.int32, (B, B), 0)
        col = lax.broadcasted_iota(jnp.int32, (B, B), 1)
        biases = []
        for part in range(3):
            qpos = (b + part - 1) * B + row
            diff = (part - 1) * B + row - col
            valid = (diff >= -SWA_WINDOW) & (diff <= SWA_WINDOW) & (qpos >= 0) & (qpos < T)
            biases.append(jnp.where(valid, 0.0, -jnp.inf)[None])
        chains = [(g, part) for g in range(gpb) for part in range(3)]
        heads = [slice(g * G, (g + 1) * G) for g, _ in chains]
        kvs = [k_ref[g] for g, _ in chains]
        vvs = [v_ref[g] for g, _ in chains]
        qvs = [(q0, q1, q2)[part][hs].reshape(G * B, d) for (_, part), hs in zip(chains, heads)]
        dovs = [(d0, d1, d2)[part][hs].reshape(G * B, d) for (_, part), hs in zip(chains, heads)]
        lses = [(l0, l1, l2)[part][hs, :, :1] for (_, part), hs in zip(chains, heads)]
        dls = [(e0, e1, e2)[part][hs, :, :1] for (_, part), hs in zip(chains, heads)]
        ss = [lax.dot_general(qv, kv, NT_DIMS, preferred_element_type=F32) for qv, kv in zip(qvs, kvs)]
        dps = [lax.dot_general(dov, vv, NT_DIMS, preferred_element_type=F32) for dov, vv in zip(dovs, vvs)]
        ps = [jnp.exp(s.reshape(G, B, B) + biases[part] - lse) for s, (_, part), lse in zip(ss, chains, lses)]
        dss = [(p * (dp.reshape(G, B, B) - dl)).reshape(G * B, B).astype(BF16) for p, dp, dl in zip(ps, dps, dls)]
        pbs = [p.reshape(G * B, B).astype(BF16) for p in ps]
        dvs = [lax.dot_general(pb, dov, TN_DIMS, preferred_element_type=F32) for pb, dov in zip(pbs, dovs)]
        dks = [lax.dot_general(ds, qv, TN_DIMS, preferred_element_type=F32) for ds, qv in zip(dss, qvs)]
        for g in range(gpb):
            dk_ref[g] = dks[3 * g] + dks[3 * g + 1] + dks[3 * g + 2]
            dv_ref[g] = (dvs[3 * g] + dvs[3 * g + 1] + dvs[3 * g + 2]).astype(BF16)

    GG = gpb * G
    est = 2 * (6 * _nbytes((GG, B, LANES), BF16) + 6 * _nbytes((GG, B, LANES), F32) + 4 * gpb * _nbytes((B, LANES), F32))
    est += 10 * _nbytes((GG * B, B), F32)
    kspec = pl.BlockSpec((gpb, B, d), cur)
    return pl.pallas_call(
        body, name=name, grid=(Hkv // gpb, nb),
        in_specs=[kspec, kspec] + q_specs + q_specs + l_specs + l_specs,
        out_specs=(kspec, kspec),
        out_shape=(jax.ShapeDtypeStruct((Hkv, T, d), F32), jax.ShapeDtypeStruct((Hkv, T, d), BF16)),
        compiler_params=_params(("parallel", "parallel"), est),
    )(k, v, q, q, q, do, do, do, lse, lse, lse, delta, delta, delta)


def _loss_head(y, target, *, name):
    T, D = y.shape
    tm = _tile(T, 512)

    def body(y_ref, t_ref, dy_ref, s_ref):
        @pl.when(pl.program_id(0) == 0)
        def _():
            s_ref[...] = jnp.zeros_like(s_ref)

        e = y_ref[...] - t_ref[...]
        dy_ref[...] = e / D
        s_ref[...] += jnp.sum(jnp.sum(e * e, axis=-1, keepdims=True), axis=0, keepdims=True)

    spec = pl.BlockSpec((tm, D), lambda i: (i, 0))
    return pl.pallas_call(
        body, name=name, grid=(T // tm,), in_specs=[spec, spec],
        out_specs=(spec, pl.BlockSpec((1, 1), lambda i: (0, 0))),
        out_shape=(jax.ShapeDtypeStruct((T, D), F32), jax.ShapeDtypeStruct((1, 1), F32)),
        compiler_params=_params(("arbitrary",), 8 * _nbytes((tm, D), F32)),
    )(y, target)


def _adamw(w, g, m, v, *, name):
    R, C = w.shape
    tr = _tile(R, max(8, (1 << 19) // max(C, LANES) // 8 * 8), 8)

    def body(w_ref, g_ref, m_ref, v_ref, d_ref, nm_ref, nv_ref):
        gv = g_ref[...]
        nm = ADAM_B1 * m_ref[...] + (1.0 - ADAM_B1) * gv
        nv = ADAM_B2 * v_ref[...] + (1.0 - ADAM_B2) * jnp.square(gv)
        m_hat = nm / (1.0 - ADAM_B1 ** ADAM_STEP)
        v_hat = nv / (1.0 - ADAM_B2 ** ADAM_STEP)
        d_ref[...] = -ADAM_LR * (m_hat / (jnp.sqrt(v_hat) + ADAM_EPS) + ADAM_WD * w_ref[...])
        nm_ref[...] = nm
        nv_ref[...] = nv

    spec = pl.BlockSpec((tr, C), lambda i: (i, 0))
    sds = jax.ShapeDtypeStruct((R, C), F32)
    return pl.pallas_call(
        body, name=name, grid=(R // tr,), in_specs=[spec] * 4, out_specs=(spec,) * 3, out_shape=(sds,) * 3,
        compiler_params=_params(("parallel",), 16 * _nbytes((tr, max(C, LANES)), F32)),
    )(w, g, m, v)


def _to_bf16(w, *, name):
    R, C = w.shape
    tr = _comm_rows_tile(R, C)

    def body(w_ref, o_ref):
        o_ref[...] = w_ref[...].astype(BF16)

    spec = pl.BlockSpec((tr, C), lambda i: (i, 0))
    return pl.pallas_call(
        body, name=name, grid=(R // tr,), in_specs=[spec], out_specs=spec, out_shape=jax.ShapeDtypeStruct((R, C), BF16),
        compiler_params=_params(("parallel",), 6 * _nbytes((tr, max(C, LANES)), F32)),
    )(w)


def _comm_rows_tile(R, L):
    return _tile(R, max(SUBLANES_BF16, (1 << 19) // L // SUBLANES_BF16 * SUBLANES_BF16), SUBLANES_BF16)


def _pair_add(g, recv, c_idx, *, name):
    _, _, R, L = g.shape
    tr = _comm_rows_tile(R, L)

    def body(c_ref, g_ref, r_ref, o_ref):
        o_ref[...] = (g_ref[...] + r_ref[...]).astype(BF16)

    grid_spec = pltpu.PrefetchScalarGridSpec(
        num_scalar_prefetch=1, grid=(N_CHIPS, R // tr),
        in_specs=[pl.BlockSpec((None, None, tr, L), lambda j, i, c: (j, c[0], i, 0)),
                  pl.BlockSpec((None, tr, L), lambda j, i, c: (j, i, 0))],
        out_specs=pl.BlockSpec((None, tr, L), lambda j, i, c: (j, i, 0)))
    return pl.pallas_call(
        body, name=name, grid_spec=grid_spec, out_shape=jax.ShapeDtypeStruct((N_CHIPS, R, L), BF16),
        compiler_params=_params(("parallel", "parallel"), 8 * _nbytes((tr, L), F32)),
    )(c_idx, g, recv)


def _sum_chips(q, c_idx, *, name):
    _, R, L = q.shape
    tr = _comm_rows_tile(R, L)

    def body(c_ref, q_ref, o_ref):
        acc = q_ref[0].astype(F32)
        for j in range(1, N_CHIPS):
            acc = acc + q_ref[j].astype(F32)
        o_ref[...] = acc

    grid_spec = pltpu.PrefetchScalarGridSpec(
        num_scalar_prefetch=1, grid=(R // tr,),
        in_specs=[pl.BlockSpec((N_CHIPS, tr, L), lambda i, c: (0, i, 0))],
        out_specs=pl.BlockSpec((None, tr, L), lambda i, c: (c[0], i, 0)))
    return pl.pallas_call(
        body, name=name, grid_spec=grid_spec, out_shape=jax.ShapeDtypeStruct((2, R, L), F32),
        compiler_params=_params(("parallel",), 10 * _nbytes((tr, L), F32)),
    )(c_idx, q)


HBM_SPEC = pl.BlockSpec(memory_space=pltpu.HBM)


def _position():
    return lax.axis_index("x"), lax.axis_index("y"), lax.axis_index("c")


def _other_chips(x, y):
    return [(1 - x, y), (x, 1 - y), (1 - x, 1 - y)]


AG_COPIES = 7


def _gather_plan(w_refs, out_refs, send_sems, recv_sems, local_sems):
    n = len(w_refs)
    x, y, c = _position()
    me, sibling = (x, y, c), (x, y, 1 - c)
    chips = _other_chips(x, y)

    def copy(i, k, block, to, src=None):
        px, py, pc = block
        slot = out_refs[i].at[4 * px + 2 * py + pc]
        return pltpu.make_async_remote_copy(
            src_ref=slot if src is None else src, dst_ref=slot, send_sem=send_sems.at[AG_COPIES * i + k],
            recv_sem=recv_sems.at[AG_COPIES * i + k], device_id=to, device_id_type=MESH)

    def local(i):
        return pltpu.make_async_copy(w_refs[i].at[c], out_refs[i].at[4 * x + 2 * y + c], local_sems.at[i])

    def first(i):
        own = w_refs[i].at[c]
        return [copy(i, 0, me, sibling, src=own)] + [copy(i, 1 + j, me, (*chip, c), src=own)
                                                     for j, chip in enumerate(chips)]

    def passed(i):
        return [copy(i, 4 + j, (*chip, c), sibling) for j, chip in enumerate(chips)]

    def start():
        for i in range(n):
            local(i).start()
            for cp in first(i):
                cp.start()

    def forward():
        for i in range(n):
            for j, chip in enumerate(chips):
                copy(i, 1 + j, (*chip, c), me).wait_recv()
                passed(i)[j].start()

    def finish():
        for i in range(n):
            copy(i, 0, sibling, me).wait_recv()
            for j, chip in enumerate(chips):
                copy(i, 4 + j, (*chip, 1 - c), me).wait_recv()
        for i in range(n):
            for cp in first(i) + passed(i):
                cp.wait_send()
            local(i).wait()

    return start, forward, finish


def _gather_scratch(n):
    return [pltpu.SemaphoreType.DMA((AG_COPIES * n,)), pltpu.SemaphoreType.DMA((AG_COPIES * n,)),
            pltpu.SemaphoreType.DMA((n,))]


def _gathered_shapes(ws):
    return [jax.ShapeDtypeStruct((2 * N_CHIPS,) + w.shape[1:], w.dtype) for w in ws]


def _all_gather_halves(ws, *, name):
    n = len(ws)

    def body(*refs):
        for step in _gather_plan(refs[:n], refs[n:2 * n], *refs[2 * n:]):
            step()

    return pl.pallas_call(
        body, name=name, in_specs=[HBM_SPEC] * n, out_specs=[HBM_SPEC] * n, out_shape=_gathered_shapes(ws),
        scratch_shapes=_gather_scratch(n),
    )(*ws)


def _exchange_plan(g_refs, out_refs, send_sems, recv_sems):
    n = len(g_refs)
    x, y, c = _position()

    def copies():
        return [pltpu.make_async_remote_copy(
            src_ref=g_refs[i].at[j, 1 - c], dst_ref=out_refs[i].at[j], send_sem=send_sems.at[N_CHIPS * i + j],
            recv_sem=recv_sems.at[N_CHIPS * i + j], device_id=(x, y, 1 - c), device_id_type=MESH)
            for i in range(n) for j in range(N_CHIPS)]

    def start():
        for cp in copies():
            cp.start()

    def finish():
        for cp in copies():
            cp.wait()

    return start, finish


def _exchange_scratch(n):
    return [pltpu.SemaphoreType.DMA((N_CHIPS * n,)), pltpu.SemaphoreType.DMA((N_CHIPS * n,))]


def _sibling_exchange(gs, *, name):
    n = len(gs)

    def body(*refs):
        for step in _exchange_plan(refs[:n], refs[n:2 * n], *refs[2 * n:]):
            step()

    return pl.pallas_call(
        body, name=name, in_specs=[HBM_SPEC] * n, out_specs=[HBM_SPEC] * n,
        out_shape=[jax.ShapeDtypeStruct((N_CHIPS,) + g.shape[2:], g.dtype) for g in gs],
        scratch_shapes=_exchange_scratch(n),
    )(*gs)


def _scatter_plan(p_refs, q_refs, send_sems, recv_sems, local_sems):
    n = len(p_refs)
    others = N_CHIPS - 1
    x, y, c = _position()
    me = 2 * x + y
    chips = _other_chips(x, y)

    def copy(i, k, chip, src_slot, dst_slot):
        return pltpu.make_async_remote_copy(
            src_ref=p_refs[i].at[src_slot], dst_ref=q_refs[i].at[dst_slot], send_sem=send_sems.at[others * i + k],
            recv_sem=recv_sems.at[others * i + k], device_id=(*chip, c), device_id_type=MESH)

    def local(i):
        return pltpu.make_async_copy(p_refs[i].at[me], q_refs[i].at[me], local_sems.at[i])

    def sends(i):
        return [copy(i, k, chip, 2 * chip[0] + chip[1], me) for k, chip in enumerate(chips)]

    def start():
        for i in range(n):
            local(i).start()
            for cp in sends(i):
                cp.start()

    def finish():
        for i in range(n):
            for k, chip in enumerate(chips):
                copy(i, k, chip, me, 2 * chip[0] + chip[1]).wait_recv()
        for i in range(n):
            for cp in sends(i):
                cp.wait_send()
            local(i).wait()

    return start, finish


def _scatter_scratch(n):
    others = N_CHIPS - 1
    return [pltpu.SemaphoreType.DMA((others * n,)), pltpu.SemaphoreType.DMA((others * n,)),
            pltpu.SemaphoreType.DMA((n,))]


def _chip_scatter(ps, *, name):
    n = len(ps)

    def body(*refs):
        for step in _scatter_plan(refs[:n], refs[n:2 * n], *refs[2 * n:]):
            step()

    return pl.pallas_call(
        body, name=name, in_specs=[HBM_SPEC] * n, out_specs=[HBM_SPEC] * n,
        out_shape=[jax.ShapeDtypeStruct(p.shape, p.dtype) for p in ps], scratch_shapes=_scatter_scratch(n),
    )(*ps)


def _sibling_share(fs, *, name):
    n = len(fs)

    def body(*refs):
        in_refs, out_refs = refs[:n], refs[n:2 * n]
        send_sems, recv_sems = refs[2 * n:]
        x, y, c = _position()

        def copy(i, half):
            return pltpu.make_async_remote_copy(
                src_ref=in_refs[i].at[half], dst_ref=out_refs[i].at[half], send_sem=send_sems.at[i],
                recv_sem=recv_sems.at[i], device_id=(x, y, 1 - c), device_id_type=MESH)

        sends = [copy(i, c) for i in range(n)]
        for cp in sends:
            cp.start()
        for i in range(n):
            copy(i, 1 - c).wait_recv()
        for cp in sends:
            cp.wait_send()

    return pl.pallas_call(
        body, name=name, in_specs=[HBM_SPEC] * n, out_specs=[HBM_SPEC] * n,
        out_shape=[jax.ShapeDtypeStruct(f.shape, f.dtype) for f in fs],
        input_output_aliases={i: i for i in range(n)},
        scratch_shapes=[pltpu.SemaphoreType.DMA((n,)), pltpu.SemaphoreType.DMA((n,))],
    )(*fs)


def _all_reduce_small(s, *, name):
    R, L = s.shape
    n_dev = 2 * N_CHIPS

    def body(s_ref, out_ref, buf, send_sems, recv_sems, local_sem):
        x, y, c = _position()
        me, sibling = (x, y, c), (x, y, 1 - c)
        chips = _other_chips(x, y)

        def slot(px, py, pc):
            return buf.at[4 * px + 2 * py + pc]

        def copy(k, block, to, src=None):
            return pltpu.make_async_remote_copy(
                src_ref=slot(*block) if src is None else src, dst_ref=slot(*block),
                send_sem=send_sems.at[k], recv_sem=recv_sems.at[k], device_id=to, device_id_type=MESH)

        mine = pltpu.make_async_copy(s_ref, slot(*me), local_sem)
        mine.start()
        first = [copy(0, me, sibling, src=s_ref)]
        first += [copy(1 + j, me, (*chip, c), src=s_ref) for j, chip in enumerate(chips)]
        for cp in first:
            cp.start()
        passed = [copy(4 + j, (*chip, c), sibling) for j, chip in enumerate(chips)]
        for j, chip in enumerate(chips):
            copy(1 + j, (*chip, c), me).wait_recv()
            passed[j].start()
        copy(0, sibling, me).wait_recv()
        for j, chip in enumerate(chips):
            copy(4 + j, (*chip, 1 - c), me).wait_recv()
        for cp in first + passed:
            cp.wait_send()
        mine.wait()
        acc = buf[0]
        for j in range(1, n_dev):
            acc = acc + buf[j]
        out_ref[...] = acc

    vmem = pl.BlockSpec(memory_space=pltpu.VMEM)
    return pl.pallas_call(
        body, name=name, in_specs=[vmem], out_specs=vmem, out_shape=jax.ShapeDtypeStruct((R, L), F32),
        scratch_shapes=[pltpu.VMEM((n_dev, R, L), F32), pltpu.SemaphoreType.DMA((7,)), pltpu.SemaphoreType.DMA((7,)),
                        pltpu.SemaphoreType.DMA],
    )(s)


def _rope_cos_sin(pos, dim, theta):
    inv = jnp.float32(theta) ** (-jnp.arange(0, dim, 2, dtype=F32) / dim)
    ang = pos.astype(F32)[:, None] * inv[None, :]
    return jnp.cos(ang), jnp.sin(ang)


def _rope_tables(T, d, segments):
    P = np.zeros((d, d), np.float32)
    c_parts, s_parts, at = [], [], 0
    for start, size, cos, sin in segments:
        half = size // 2
        if start > at:
            c_parts.append(jnp.ones((T, start - at), F32))
            s_parts.append(jnp.zeros((T, start - at), F32))
        c_parts += [cos, cos]
        s_parts += [-sin, sin]
        at = start + size
        for p in range(half):
            P[start + half + p, start + p] = 1.0
            P[start + p, start + half + p] = 1.0
    if at < d:
        c_parts.append(jnp.ones((T, d - at), F32))
        s_parts.append(jnp.zeros((T, d - at), F32))
    return jnp.concatenate(c_parts, axis=1), jnp.concatenate(s_parts, axis=1), jnp.asarray(P, BF16)


def _heads(t, H, d):
    return t.reshape(t.shape[0], H, d).transpose(1, 0, 2)


def _unheads(t):
    H, T, d = t.shape
    return t.transpose(1, 0, 2).reshape(T, H * d)


def _dw(a, b, *, name, axis):
    K, N = a.shape[1], b.shape[1]
    if axis == 1:
        return _mm(a, b, mode="tn", name=name).reshape(N_CHIPS, K // N_CHIPS, N)
    if (N // N_CHIPS) % LANES == 0:
        return _mm(a, b, mode="tn", name=name, split=N_CHIPS)
    return _mm(a, b, mode="tn", name=name).reshape(K, N_CHIPS, N // N_CHIPS).transpose(1, 0, 2)


def _mlp_fwd(x, gain, w_up, w_down, tag):
    hm = _norm_fwd(x[None], gain, name=f"mlp{tag}_norm")[0]
    u, act = _mm(hm, w_up, mode="nn", name=f"mlp{tag}_up", epi="sqrelu")
    x_out = _mm(act, w_down, mode="nn", name=f"mlp{tag}_down", epi="add", extra=x)
    return x_out, (hm, u, act)


def _mlp_bwd(x, gain, w_up, w_down, saved, dxo, tag):
    hm, u, act = saved
    du = _mm(dxo, w_down, mode="nt", name=f"mlp{tag}_dact", epi="dsqrelu", extra=u, out_dtype=BF16)
    dw_down = _dw(act, dxo, name=f"mlp{tag}_dwdown", axis=1)
    dx, dgain = _mm_dnorm(du, w_up, x, gain, dxo, name=f"mlp{tag}_dhm_dnorm")
    dw_up = _dw(hm, du, name=f"mlp{tag}_dwup", axis=2)
    return dx, dgain[0], dw_up, dw_down


def _local_step(x, target, W, small, late=None):
    T, D = x.shape
    W = dict(W)
    pos = jnp.arange(T)
    mla_cos, mla_sin = _rope_cos_sin(pos, MLA_ROPE, ROPE_THETA)
    row_cos, row_sin = _rope_cos_sin(pos // GRID_W, GQA_DIM // 2, AXIAL_THETA)
    col_cos, col_sin = _rope_cos_sin(pos % GRID_W, GQA_DIM // 2, AXIAL_THETA)
    swa_cos, swa_sin = _rope_cos_sin(pos, SWA_ROT, ROPE_THETA)
    rope_q = _rope_tables(T, MLA_QK, [(MLA_NOPE, MLA_ROPE, mla_cos, mla_sin)])
    rope_kr = _rope_tables(T, MLA_ROPE, [(0, MLA_ROPE, mla_cos, mla_sin)])
    half = GQA_DIM // 2
    rope_ax = _rope_tables(T, GQA_DIM, [(0, half, row_cos, row_sin), (half, half, col_cos, col_sin)])
    rope_sw = _rope_tables(T, SWA_DIM, [(0, SWA_ROT, swa_cos, swa_sin)])
    o1 = MLA_Q_LORA
    o2 = o1 + MLA_KV_LORA
    o3 = o2 + MLA_ROPE
    o4 = o3 + GQA_HEADS * GQA_DIM
    o5 = o4 + GQA_KV * GQA_DIM
    sc_a, sc_g, sc_s = MLA_QK ** -0.5, GQA_DIM ** -0.5, SWA_DIM ** -0.5
    kv_w = MLA_NOPE + MLA_V

    h0 = _norm_fwd(x[None], small["even_norm"], name="even_norm")[0]
    proj = _mm(h0, W["even_w_in"], mode="nn", name="even_in")
    c_q, c_kv, kr_raw = proj[:, :o1], proj[:, o1:o2], proj[:, o2:o3]
    qg_raw = _heads(proj[:, o3:o4], GQA_HEADS, GQA_DIM)
    kg_raw = _heads(proj[:, o4:o5], GQA_KV, GQA_DIM)
    vg = _heads(proj[:, o5:], GQA_KV, GQA_DIM).astype(BF16)
    cqn = _norm_fwd(c_q[None], small["mla_q_lat_norm"], name="q_lat_norm")[0]
    ckvn = _norm_fwd(c_kv[None], small["mla_kv_lat_norm"], name="kv_lat_norm")[0]
    qa_raw = _heads(_mm(cqn, W["mla_w_uq"], mode="nn", name="mla_uq"), MLA_HEADS, MLA_QK)
    kv = _mm(ckvn, W["mla_w_ukv"], mode="nn", name="mla_ukv").reshape(T, MLA_HEADS, kv_w)
    kn_raw = kv[:, :, :MLA_NOPE].transpose(1, 0, 2)
    va = kv[:, :, MLA_NOPE:].transpose(1, 0, 2).astype(BF16)
    q_a = _norm_fwd(qa_raw, small["mla_q_norm"], name="mla_q_prep", rope=rope_q, out_scale=sc_a * LOG2E)
    k_n = _norm_fwd(kn_raw, small["mla_k_nope_norm"], name="mla_kn_prep")
    k_r = _norm_fwd(kr_raw[None], small["mla_k_rope_norm"], name="mla_kr_prep", rope=rope_kr)
    k_a = jnp.concatenate([k_n, jnp.broadcast_to(k_r, (MLA_HEADS, T, MLA_ROPE))], axis=-1)
    if late is None:
        o_a, lse_a, ob_a = _flash_fwd(q_a, k_a, va, name="mla_attn")
    else:
        o_a, lse_a, ob_a, gathered = _flash_fwd(q_a, k_a, va, name="mla_attn", gather=late.halves)
        W.update(late.weights(gathered))
    q_g = _norm_fwd(qg_raw, small["gqa_q_norm"], name="gqa_q_prep", rope=rope_ax, out_scale=sc_g * LOG2E)
    k_g = _norm_fwd(kg_raw, small["gqa_k_norm"], name="gqa_k_prep", rope=rope_ax)
    o_g, lse_g, ob_g = _flash_fwd(q_g, k_g, vg, name="gqa_attn")
    merged = jnp.concatenate([_unheads(ob_a), _unheads(ob_g)], axis=-1)
    x1 = _mm(merged, W["even_w_out"], mode="nn", name="even_out", epi="add", extra=x)
    x2, mlp0 = _mlp_fwd(x1, small["mlp_norm"][0], W["mlp_w_up0"], W["mlp_w_down0"], 0)

    h1 = _norm_fwd(x2[None], small["odd_norm"], name="odd_norm")[0]
    qkv = _mm(h1, W["odd_w_qkv"], mode="nn", name="odd_qkv")
    nq, nkk = SWA_HEADS * SWA_DIM, SWA_KV * SWA_DIM
    qs_raw = _heads(qkv[:, :nq], SWA_HEADS, SWA_DIM)
    ks_raw = _heads(qkv[:, nq:nq + nkk], SWA_KV, SWA_DIM)
    vs = _heads(qkv[:, nq + nkk:], SWA_KV, SWA_DIM).astype(BF16)
    q_s = _norm_fwd(qs_raw, small["swa_q_norm"], name="swa_q_prep", rope=rope_sw, out_scale=sc_s)
    k_s = _norm_fwd(ks_raw, small["swa_k_norm"], name="swa_k_prep", rope=rope_sw)
    sink = small["swa_sink"].reshape(SWA_HEADS, 1, 1)
    o_s, lse_s, ob_s = _swa_fwd(q_s, k_s, vs, sink, name="swa_attn")
    o_flat = _unheads(ob_s)
    x3 = _mm(o_flat, W["odd_w_out"], mode="nn", name="odd_out", epi="add", extra=x2)
    x4, mlp1 = _mlp_fwd(x3, small["mlp_norm"][1], W["mlp_w_up1"], W["mlp_w_down1"], 1)

    dy, loss_sum = _loss_head(x4, target, name="loss_head")
    gW, gs = {}, {}

    dx3, dg_m1, gW["mlp_w_up1"], gW["mlp_w_down1"] = _mlp_bwd(
        x3, small["mlp_norm"][1], W["mlp_w_up1"], W["mlp_w_down1"], mlp1, dy, 1)
    d_oflat = _mm(dx3, W["odd_w_out"], mode="nt", name="odd_dout")
    gW["odd_w_out"] = _dw(o_flat, dx3, name="odd_dwout", axis=1)
    do_s = _heads(d_oflat, SWA_HEADS, SWA_DIM)
    dq_s, dsink, delta_s, dob_s = _swa_dq(q_s, k_s, vs, o_s, do_s, lse_s, sink, name="swa_dq")
    dk_s, dv_s = _swa_dkv(q_s, k_s, vs, dob_s, lse_s, delta_s, name="swa_dkv")
    gs["swa_sink"] = dsink[:, 0, 0]
    dqs_raw, gs["swa_q_norm"] = _norm_bwd(qs_raw, small["swa_q_norm"], dq_s, name="swa_dq_prep", rope=rope_sw,
                                          out_dtype=BF16, dy_scale=sc_s)
    dks_raw, gs["swa_k_norm"] = _norm_bwd(ks_raw, small["swa_k_norm"], dk_s, name="swa_dk_prep", rope=rope_sw,
                                          out_dtype=BF16)
    dqkv = jnp.concatenate([_unheads(dqs_raw), _unheads(dks_raw), _unheads(dv_s)], axis=-1).astype(BF16)
    gW["odd_w_qkv"] = _dw(h1, dqkv, name="odd_dwqkv", axis=2)
    dx2, gs["odd_norm"] = _mm_dnorm(dqkv, W["odd_w_qkv"], x2, small["odd_norm"], dx3, name="odd_dh_dnorm")

    dx1, dg_m0, gW["mlp_w_up0"], gW["mlp_w_down0"] = _mlp_bwd(
        x1, small["mlp_norm"][0], W["mlp_w_up0"], W["mlp_w_down0"], mlp0, dx2, 0)
    gs["mlp_norm"] = jnp.stack([dg_m0, dg_m1])
    d_merged = _mm(dx1, W["even_w_out"], mode="nt", name="even_dout")
    gW["even_w_out"] = _dw(merged, dx1, name="even_dwout", axis=1)
    na = MLA_HEADS * MLA_V
    do_a = _heads(d_merged[:, :na], MLA_HEADS, MLA_V)
    do_g = _heads(d_merged[:, na:], GQA_HEADS, GQA_DIM)
    delta_a, dob_a = _delta(o_a, do_a, name="mla_delta")
    delta_g, dob_g = _delta(o_g, do_g, name="gqa_delta")
    if late is None:
        dq_a, dk_a, dv_a = _flash_bwd(q_a, k_a, va, dob_a, lse_a, delta_a, name="mla_attn_bwd")
        dq_g, dk_gp, dv_gp = _flash_bwd(q_g, k_g, vg, dob_g, lse_g, delta_g, name="gqa_attn_bwd")
    else:
        halves = late.split(gW)
        dq_a, dk_a, dv_a, from_sibling = _flash_bwd(q_a, k_a, va, dob_a, lse_a, delta_a, name="mla_attn_bwd",
                                                    rider=("exchange", halves))
        dq_g, dk_gp, dv_gp, late.scattered = _flash_bwd(q_g, k_g, vg, dob_g, lse_g, delta_g, name="gqa_attn_bwd",
                                                        rider=("scatter", late.pairs(halves, from_sibling)))
    grp = GQA_HEADS // GQA_KV
    ln2 = 1.0 / LOG2E
    dqg_raw, gs["gqa_q_norm"] = _norm_bwd(qg_raw, small["gqa_q_norm"], dq_g, name="gqa_dq_prep", rope=rope_ax,
                                          dy_scale=sc_g, out_dtype=BF16)
    dkg_raw, gs["gqa_k_norm"] = _norm_bwd(kg_raw, small["gqa_k_norm"], dk_gp, name="gqa_dk_prep", rope=rope_ax,
                                          group=grp, dy_scale=ln2, out_dtype=BF16)
    dvg = _group_sum(dv_gp, grp, name="gqa_dv_sum", out_dtype=BF16)
    dqa_raw, gs["mla_q_norm"] = _norm_bwd(qa_raw, small["mla_q_norm"], dq_a, name="mla_dq_prep", rope=rope_q,
                                          dy_scale=sc_a, out_dtype=BF16)
    dkn_raw, gs["mla_k_nope_norm"] = _norm_bwd(kn_raw, small["mla_k_nope_norm"], dk_a[:, :, :MLA_NOPE],
                                               name="mla_dkn_prep", dy_scale=ln2, out_dtype=BF16)
    dkr_raw, gs["mla_k_rope_norm"] = _norm_bwd(kr_raw[None], small["mla_k_rope_norm"], dk_a[:, :, MLA_NOPE:],
                                               name="mla_dkr_prep", rope=rope_kr, group=MLA_HEADS, dy_scale=ln2,
                                               out_dtype=BF16)
    dkv = jnp.concatenate([dkn_raw.transpose(1, 0, 2), dv_a.astype(BF16).transpose(1, 0, 2)], axis=-1)
    dkv = dkv.reshape(T, MLA_HEADS * kv_w).astype(BF16)
    dqa = _unheads(dqa_raw).astype(BF16)
    dckvn = _mm(dkv, W["mla_w_ukv"], mode="nt", name="mla_dckv")
    gW["mla_w_ukv"] = _dw(ckvn, dkv, name="mla_dwukv", axis=2)
    dcqn = _mm(dqa, W["mla_w_uq"], mode="nt", name="mla_dcq")
    gW["mla_w_uq"] = _dw(cqn, dqa, name="mla_dwuq", axis=2)
    dc_q, gs["mla_q_lat_norm"] = _norm_bwd(c_q[None], small["mla_q_lat_norm"], dcqn[None], name="q_lat_dnorm",
                                           out_dtype=BF16)
    dc_kv, gs["mla_kv_lat_norm"] = _norm_bwd(c_kv[None], small["mla_kv_lat_norm"], dckvn[None], name="kv_lat_dnorm",
                                             out_dtype=BF16)
    dproj = jnp.concatenate([dc_q[0], dc_kv[0], dkr_raw[0], _unheads(dqg_raw), _unheads(dkg_raw), _unheads(dvg)],
                            axis=-1).astype(BF16)
    dh0 = _mm(dproj, W["even_w_in"], mode="nt", name="even_dh")
    gW["even_w_in"] = _dw(h0, dproj, name="even_dwin", axis=2)
    dx0, gs["even_norm"] = _norm_bwd(x[None], small["even_norm"], dh0[None], name="even_dnorm", res=dx1)
    gs = {k: v.reshape(-1) for k, v in gs.items()}
    return loss_sum, dx0[0], gW, gs


BIG = (("even_w_in", 0, 2), ("mla_w_uq", 0, 2), ("mla_w_ukv", 0, 2), ("even_w_out", 0, 1), ("odd_w_qkv", 0, 2),
       ("odd_w_out", 0, 1), ("mlp_w_up", 0, 2), ("mlp_w_up", 1, 2), ("mlp_w_down", 0, 1), ("mlp_w_down", 1, 1))
GATHER_FIRST = ("even_w_in", "mla_w_uq", "mla_w_ukv")
GRADS_LAST = ("even_w_in", "mla_w_uq", "mla_w_ukv")
SMALL = ("even_norm", "mla_q_lat_norm", "mla_kv_lat_norm", "mla_q_norm", "mla_k_nope_norm", "mla_k_rope_norm",
         "gqa_q_norm", "gqa_k_norm", "odd_norm", "swa_q_norm", "swa_k_norm", "swa_sink", "mlp_norm")
def _pad_to(v, n):
    return v if v.shape[-1] == n else jnp.pad(v, [(0, 0)] * (v.ndim - 1) + [(0, n - v.shape[-1])])


def _big_key(name, layer, w):
    return name if w[name].shape[0] == 1 else f"{name}{layer}"


def _pack_rows(flat, rows=8):
    n = flat.shape[0]
    padded = -(-n // (rows * LANES)) * rows * LANES
    return _pad_to(flat, padded).reshape(-1, LANES)


def kernel(x, even_norm, even_w_in, mla_q_lat_norm, mla_kv_lat_norm, mla_w_uq, mla_w_ukv, mla_q_norm, mla_k_nope_norm, mla_k_rope_norm, gqa_q_norm, gqa_k_norm, even_w_out, odd_norm, odd_w_qkv, swa_q_norm, swa_k_norm, swa_sink, odd_w_out, mlp_norm, mlp_w_up, mlp_w_down, loss_target, m_even_norm, m_even_w_in, m_mla_q_lat_norm, m_mla_kv_lat_norm, m_mla_w_uq, m_mla_w_ukv, m_mla_q_norm, m_mla_k_nope_norm, m_mla_k_rope_norm, m_gqa_q_norm, m_gqa_k_norm, m_even_w_out, m_odd_norm, m_odd_w_qkv, m_swa_q_norm, m_swa_k_norm, m_swa_sink, m_odd_w_out, m_mlp_norm, m_mlp_w_up, m_mlp_w_down, v_even_norm, v_even_w_in, v_mla_q_lat_norm, v_mla_kv_lat_norm, v_mla_w_uq, v_mla_w_ukv, v_mla_q_norm, v_mla_k_nope_norm, v_mla_k_rope_norm, v_gqa_q_norm, v_gqa_k_norm, v_even_w_out, v_odd_norm, v_odd_w_qkv, v_swa_q_norm, v_swa_k_norm, v_swa_sink, v_odd_w_out, v_mlp_norm, v_mlp_w_up, v_mlp_w_down):
    w = dict(even_norm=even_norm, even_w_in=even_w_in, mla_q_lat_norm=mla_q_lat_norm, mla_kv_lat_norm=mla_kv_lat_norm,
             mla_w_uq=mla_w_uq, mla_w_ukv=mla_w_ukv, mla_q_norm=mla_q_norm, mla_k_nope_norm=mla_k_nope_norm,
             mla_k_rope_norm=mla_k_rope_norm, gqa_q_norm=gqa_q_norm, gqa_k_norm=gqa_k_norm, even_w_out=even_w_out,
             odd_norm=odd_norm, odd_w_qkv=odd_w_qkv, swa_q_norm=swa_q_norm, swa_k_norm=swa_k_norm, swa_sink=swa_sink,
             odd_w_out=odd_w_out, mlp_norm=mlp_norm, mlp_w_up=mlp_w_up, mlp_w_down=mlp_w_down)
    m = dict(even_norm=m_even_norm, even_w_in=m_even_w_in, mla_q_lat_norm=m_mla_q_lat_norm,
             mla_kv_lat_norm=m_mla_kv_lat_norm, mla_w_uq=m_mla_w_uq, mla_w_ukv=m_mla_w_ukv, mla_q_norm=m_mla_q_norm,
             mla_k_nope_norm=m_mla_k_nope_norm, mla_k_rope_norm=m_mla_k_rope_norm, gqa_q_norm=m_gqa_q_norm,
             gqa_k_norm=m_gqa_k_norm, even_w_out=m_even_w_out, odd_norm=m_odd_norm, odd_w_qkv=m_odd_w_qkv,
             swa_q_norm=m_swa_q_norm, swa_k_norm=m_swa_k_norm, swa_sink=m_swa_sink, odd_w_out=m_odd_w_out,
             mlp_norm=m_mlp_norm, mlp_w_up=m_mlp_w_up, mlp_w_down=m_mlp_w_down)
    v = dict(even_norm=v_even_norm, even_w_in=v_even_w_in, mla_q_lat_norm=v_mla_q_lat_norm,
             mla_kv_lat_norm=v_mla_kv_lat_norm, mla_w_uq=v_mla_w_uq, mla_w_ukv=v_mla_w_ukv, mla_q_norm=v_mla_q_norm,
             mla_k_nope_norm=v_mla_k_nope_norm, mla_k_rope_norm=v_mla_k_rope_norm, gqa_q_norm=v_gqa_q_norm,
             gqa_k_norm=v_gqa_k_norm, even_w_out=v_even_w_out, odd_norm=v_odd_norm, odd_w_qkv=v_odd_w_qkv,
             swa_q_norm=v_swa_q_norm, swa_k_norm=v_swa_k_norm, swa_sink=v_swa_sink, odd_w_out=v_odd_w_out,
             mlp_norm=v_mlp_norm, mlp_w_up=v_mlp_w_up, mlp_w_down=v_mlp_w_down)
    xi, yi, ci = _position()
    chip = 2 * xi + yi
    T, D = x.shape[1], x.shape[2]

    c_idx = ci.reshape(1).astype(jnp.int32)
    key_of = lambda entry: _big_key(entry[0], entry[1], w)
    first_use = [e for e in BIG if e[0] in GATHER_FIRST]
    later_use = [e for e in BIG if e[0] not in GATHER_FIRST]
    early_grads = [e for e in BIG if e[0] not in GRADS_LAST]
    last_grads = [e for e in BIG if e[0] in GRADS_LAST]

    as_bf16 = {}

    def halves_of(entries):
        out = []
        for name, layer, _ in entries:
            layers, ks, ns = w[name].shape
            if name not in as_bf16:
                as_bf16[name] = _to_bf16(w[name].reshape(layers * ks, ns), name=f"to_bf16_{name}")
            out.append(as_bf16[name].reshape(layers, 2, ks // 2, ns)[layer])
        return out

    def weights_of(entries, gathered):
        out = {}
        for (name, layer, axis), g in zip(entries, gathered):
            ks, ns = w[name].shape[1:]
            stacked = g.reshape(N_CHIPS, ks, ns)
            if axis == 1:
                out[_big_key(name, layer, w)] = stacked.reshape(N_CHIPS * ks, ns)
            else:
                out[_big_key(name, layer, w)] = stacked.transpose(1, 0, 2).reshape(ks, N_CHIPS * ns)
        return out

    def split_halves(entries, gW):
        out = []
        for entry in entries:
            _, ks, ns = gW[key_of(entry)].shape
            out.append(gW[key_of(entry)].reshape(N_CHIPS, 2, ks // 2, ns))
        return out

    def pair_sums(entries, g_all, from_sibling):
        return [_pair_add(g, r, c_idx, name=f"grad_pair_add_{key_of(e)}") for e, g, r in zip(entries, g_all, from_sibling)]

    class _Late:
        halves = halves_of(later_use)
        scattered = None

        @staticmethod
        def weights(gathered):
            return weights_of(later_use, gathered)

        @staticmethod
        def split(gW):
            return split_halves(early_grads, gW)

        @staticmethod
        def pairs(g_all, from_sibling):
            return pair_sums(early_grads, g_all, from_sibling)

    late = _Late()
    W = weights_of(first_use, _all_gather_halves(halves_of(first_use), name="weights_all_gather"))

    odd_full = jnp.zeros((N_CHIPS, D // N_CHIPS), F32).at[chip].set(jnp.where(ci == 0, 1.0, 0.0) * w["odd_norm"][0])
    odd_full = _all_reduce_small(_pack_rows(odd_full.reshape(-1)), name="odd_norm_gather").reshape(-1)[:D]
    small = {name: w[name][0] for name in SMALL if name not in ("mlp_norm", "odd_norm")}
    small["mlp_norm"] = w["mlp_norm"]
    small["odd_norm"] = odd_full

    loss_sum, grad_x, gW, gs = _local_step(x[0], loss_target[0], W, small, late)

    loss_local = 0.5 * loss_sum.reshape(1) / D
    small_sizes = [(name, int(gs[name].shape[0])) for name in SMALL]
    ar_in = jnp.concatenate([_pad_to(loss_local, LANES)] + [gs[name] for name in SMALL])
    ar_out = _all_reduce_small(_pack_rows(ar_in), name="small_all_reduce").reshape(-1)
    loss = ar_out[0]
    g_small, off = {}, LANES
    for name, n in small_sizes:
        g_small[name] = ar_out[off:off + n]
        off += n
    shard_d = D // N_CHIPS
    g_small["odd_norm"] = lax.dynamic_slice(g_small["odd_norm"], (chip * shard_d,), (shard_d,))

    from_chips = dict(zip(map(key_of, early_grads), late.scattered))
    last_halves = split_halves(last_grads, gW)
    last_pairs = pair_sums(last_grads, last_halves, _sibling_exchange(last_halves, name="grad_sibling_exchange"))
    last_scattered = _chip_scatter(last_pairs, name="grad_chip_scatter")
    from_chips.update(zip(map(key_of, last_grads), last_scattered))
    keys = [key_of(e) for e in BIG]
    reduced = [_sum_chips(from_chips[key], c_idx, name=f"grad_chip_sum_{key}") for key in keys]
    shared = _sibling_share(reduced, name="grad_sibling_share")
    g_shards = {}
    for (name, layer, _), f in zip(BIG, shared):
        g_shards.setdefault(name, []).append(f.reshape(w[name].shape[1:]))

    grads, deltas, new_m, new_v = {}, {}, {}, {}
    for name in g_shards:
        shape = w[name].shape
        g = jnp.stack(g_shards[name])
        grads[name] = g
        two_d = (shape[0] * shape[1], shape[2])
        d_, m_, v_ = _adamw(w[name].reshape(two_d), g.reshape(two_d), m[name].reshape(two_d), v[name].reshape(two_d),
                            name=f"adamw_{name}")
        deltas[name], new_m[name], new_v[name] = d_.reshape(shape), m_.reshape(shape), v_.reshape(shape)
    pack_small = lambda d: _pack_rows(jnp.concatenate([d[name].reshape(-1) for name in SMALL]))
    for name in SMALL:
        grads[name] = g_small[name].reshape(w[name].shape)
    d_, m_, v_ = _adamw(pack_small(w), pack_small(grads), pack_small(m), pack_small(v), name="adamw_small")
    d_, m_, v_ = d_.reshape(-1), m_.reshape(-1), v_.reshape(-1)
    off = 0
    for name in SMALL:
        n = int(np.prod(w[name].shape))
        deltas[name] = d_[off:off + n].reshape(w[name].shape)
        new_m[name] = m_[off:off + n].reshape(w[name].shape)
        new_v[name] = v_[off:off + n].reshape(w[name].shape)
        off += n

    order = ("even_norm", "even_w_in", "mla_q_lat_norm", "mla_kv_lat_norm", "mla_w_uq", "mla_w_ukv", "mla_q_norm",
             "mla_k_nope_norm", "mla_k_rope_norm", "gqa_q_norm", "gqa_k_norm", "even_w_out", "odd_norm", "odd_w_qkv",
             "swa_q_norm", "swa_k_norm", "swa_sink", "odd_w_out", "mlp_norm", "mlp_w_up", "mlp_w_down")
    outs = [loss, grad_x[None]]
    for group in (grads, deltas, new_m, new_v):
        outs += [group[name] for name in order]
    return tuple(outs)
```

```python
import math

import numpy as np
import jax
import jax.numpy as jnp
from jax import lax
from jax.experimental import pallas as pl
from jax.experimental.pallas import tpu as pltpu

F32 = jnp.float32
BF16 = jnp.bfloat16
MESH = pl.DeviceIdType.MESH

VMEM_BYTES_V7X = 64 * 1024 * 1024
LANES = 128
SUBLANES_BF16 = 16

GRID_W = 64
NORM_EPS = 1e-6
ROPE_THETA = 500000.0
AXIAL_THETA = 10000.0
MLA_HEADS = 8
MLA_Q_LORA = 512
MLA_KV_LORA = 256
MLA_NOPE = 128
MLA_ROPE = 64
MLA_QK = MLA_NOPE + MLA_ROPE
MLA_V = 128
GQA_HEADS = 8
GQA_KV = 2
GQA_DIM = 128
SWA_HEADS = 32
SWA_KV = 4
SWA_DIM = 64
SWA_WINDOW = 128
SWA_ROT = SWA_DIM // 4
SWA_BLOCK = 128
SWA_HEAD_PARTS = 2
SWA_GROUPS = 4
ADAM_LR = 0.001
ADAM_B1 = 0.9
ADAM_B2 = 0.999
ADAM_EPS = 1e-08
ADAM_WD = 0.01
ADAM_STEP = 10
N_CHIPS = 4


def _tile(dim, cap, mult=LANES):
    if dim <= cap:
        return dim
    t = (cap // mult) * mult
    while t >= mult:
        if dim % t == 0:
            return t
        t -= mult
    return dim


def _params(dims, vmem_estimate):
    limit = int(min(max(vmem_estimate * 1.25 + (4 << 20), 32 << 20), VMEM_BYTES_V7X - (6 << 20)))
    return pltpu.CompilerParams(dimension_semantics=dims, vmem_limit_bytes=limit)


def _nbytes(shape, dtype):
    return int(np.prod(shape)) * jnp.dtype(dtype).itemsize


def _mm(a, b, *, mode, name, out_dtype=F32, epi=None, extra=None, split=1, caps=(1024, 1024, 2048)):
    if mode == "nn":
        (M, K), (K2, N) = a.shape, b.shape
    elif mode == "nt":
        (M, K), (N, K2) = a.shape, b.shape
    else:
        (K, M), (K2, N) = a.shape, b.shape
    assert K == K2, (a.shape, b.shape, mode)
    assert N % split == 0
    ns = N // split
    tn, tk = _tile(ns, caps[1]), _tile(K, caps[2])
    tm = _tile(M, min(caps[0], max(LANES, caps[0] * caps[1] // tn)))
    nj_per = ns // tn
    grid = (M // tm, N // tn, K // tk)
    nk = grid[2]
    if mode == "nn":
        a_spec = pl.BlockSpec((tm, tk), lambda i, j, k: (i, k))
        b_spec = pl.BlockSpec((tk, tn), lambda i, j, k: (k, j))
        dn = (((1,), (0,)), ((), ()))
    elif mode == "nt":
        a_spec = pl.BlockSpec((tm, tk), lambda i, j, k: (i, k))
        b_spec = pl.BlockSpec((tn, tk), lambda i, j, k: (j, k))
        dn = (((1,), (1,)), ((), ()))
    else:
        a_spec = pl.BlockSpec((tk, tm), lambda i, j, k: (k, i))
        b_spec = pl.BlockSpec((tk, tn), lambda i, j, k: (k, j))
        dn = (((0,), (0,)), ((), ()))
    if split == 1:
        o_spec = pl.BlockSpec((tm, tn), lambda i, j, k: (i, j))
        o_shape = (M, N)
    else:
        o_spec = pl.BlockSpec((None, tm, tn), lambda i, j, k: (j // nj_per, i, j % nj_per))
        o_shape = (split, M, ns)
    mn_spec = pl.BlockSpec((tm, tn), lambda i, j, k: (i, j))
    in_specs, args = [a_spec, b_spec], [a, b]
    if epi in ("add", "dsqrelu"):
        in_specs.append(mn_spec)
        args.append(extra)
    if epi == "sqrelu":
        out_shape = (jax.ShapeDtypeStruct(o_shape, BF16), jax.ShapeDtypeStruct(o_shape, BF16))
        out_specs = (o_spec, o_spec)
        n_out = 2
    else:
        out_shape = jax.ShapeDtypeStruct(o_shape, out_dtype)
        out_specs = o_spec
        n_out = 1

    def body(*refs):
        a_ref, b_ref = refs[0], refs[1]
        e_ref = refs[2] if len(args) == 3 else None
        outs = refs[len(args):len(args) + n_out]

        def finish(acc):
            if epi is None:
                outs[0][...] = acc.astype(outs[0].dtype)
            elif epi == "add":
                outs[0][...] = (e_ref[...] + acc).astype(outs[0].dtype)
            elif epi == "sqrelu":
                r = jnp.maximum(acc, 0.0)
                outs[0][...] = acc.astype(BF16)
                outs[1][...] = (r * r).astype(BF16)
            else:
                u = e_ref[...].astype(F32)
                outs[0][...] = (acc * (2.0 * jnp.maximum(u, 0.0))).astype(outs[0].dtype)

        prod = lax.dot_general(a_ref[...].astype(BF16), b_ref[...].astype(BF16), dn, preferred_element_type=F32)
        if nk == 1:
            finish(prod)
            return
        acc_ref = refs[-1]
        k = pl.program_id(2)

        @pl.when(k == 0)
        def _():
            acc_ref[...] = prod

        @pl.when((k != 0) & (k != nk - 1))
        def _():
            acc_ref[...] += prod

        @pl.when(k == nk - 1)
        def _():
            finish(acc_ref[...] + prod)

    est = 2 * (_nbytes((tm, tk), a.dtype) + _nbytes((tk, tn), b.dtype)) + _nbytes((tm, tn), F32)
    est += 2 * n_out * _nbytes((tm, tn), out_dtype if n_out == 1 else BF16)
    if len(args) == 3:
        est += 2 * _nbytes((tm, tn), extra.dtype)
    est += 3 * _nbytes((tm, tn), F32)
    return pl.pallas_call(
        body, name=name, grid=grid, in_specs=in_specs, out_specs=out_specs, out_shape=out_shape,
        scratch_shapes=[] if nk == 1 else [pltpu.VMEM((tm, tn), F32)],
        compiler_params=_params(("parallel", "parallel", "arbitrary"), est),
    )(*args)


def _perm(y, p):
    hi = y.astype(BF16)
    lo = (y - hi.astype(F32)).astype(BF16)
    d = lambda t: jnp.dot(t, p, preferred_element_type=F32)
    return d(hi) + d(lo)


def _rows_tile(T, d):
    return _tile(T, 2048 if d <= 256 else 512, 128)


def _norm_fwd(x, gain, *, name, rope=None, out_dtype=BF16, out_scale=None):
    H, T, d = x.shape
    tm = _rows_tile(T, d)
    g2 = gain.reshape(1, d).astype(F32)
    in_specs = [pl.BlockSpec((None, tm, d), lambda h, i: (h, i, 0)), pl.BlockSpec((1, d), lambda h, i: (0, 0))]
    args = [x, g2]
    if rope is not None:
        in_specs += [pl.BlockSpec((tm, d), lambda h, i: (i, 0)), pl.BlockSpec((tm, d), lambda h, i: (i, 0)),
                     pl.BlockSpec((d, d), lambda h, i: (0, 0))]
        args += list(rope)

    def body(*refs):
        x_ref, g_ref = refs[0], refs[1]
        o_ref = refs[-1]
        xv = x_ref[...]
        y = xv * lax.rsqrt(jnp.mean(xv * xv, axis=-1, keepdims=True) + NORM_EPS)
        y = y * g_ref[...]
        if rope is not None:
            c_ref, s_ref, p_ref = refs[2], refs[3], refs[4]
            y = y * c_ref[...] + _perm(y, p_ref[...]) * s_ref[...]
        if out_scale is not None:
            y = y * out_scale
        o_ref[...] = y.astype(o_ref.dtype)

    est = 2 * (_nbytes((tm, max(d, LANES)), F32) * (3 if rope is not None else 1) + _nbytes((tm, max(d, LANES)), out_dtype))
    est += 6 * _nbytes((tm, max(d, LANES)), F32)
    return pl.pallas_call(
        body, name=name, grid=(H, T // tm), in_specs=in_specs,
        out_specs=pl.BlockSpec((None, tm, d), lambda h, i: (h, i, 0)),
        out_shape=jax.ShapeDtypeStruct((H, T, d), out_dtype),
        compiler_params=_params(("parallel", "parallel"), est),
    )(*args)


def _norm_bwd(x, gain, dy, *, name, rope=None, group=1, res=None, out_dtype=F32, dy_scale=None):
    H, T, d = x.shape
    assert dy.shape == (H * group, T, d), (dy.shape, x.shape, group)
    tm = _rows_tile(T, d)
    g2 = gain.reshape(1, d).astype(F32)
    in_specs = [pl.BlockSpec((None, tm, d), lambda h, i: (h, i, 0)), pl.BlockSpec((1, d), lambda h, i: (0, 0)),
                pl.BlockSpec((group, tm, d), lambda h, i: (h, i, 0))]
    args = [x, g2, dy]
    if rope is not None:
        in_specs += [pl.BlockSpec((tm, d), lambda h, i: (i, 0)), pl.BlockSpec((tm, d), lambda h, i: (i, 0)),
                     pl.BlockSpec((d, d), lambda h, i: (0, 0))]
        args += list(rope)
    if res is not None:
        assert H == 1
        in_specs.append(pl.BlockSpec((tm, d), lambda h, i: (i, 0)))
        args.append(res)
    n_in = len(args)

    def body(*refs):
        x_ref, g_ref, dy_ref = refs[0], refs[1], refs[2]
        dx_ref, dg_ref = refs[n_in], refs[n_in + 1]
        first = (pl.program_id(0) == 0) & (pl.program_id(1) == 0)

        @pl.when(first)
        def _():
            dg_ref[...] = jnp.zeros_like(dg_ref)

        dyv = dy_ref[0].astype(F32)
        for g in range(1, group):
            dyv = dyv + dy_ref[g].astype(F32)
        if dy_scale is not None:
            dyv = dyv * dy_scale
        pos = 3
        if rope is not None:
            c_ref, s_ref, p_ref = refs[3], refs[4], refs[5]
            pos = 6
            dyv = dyv * c_ref[...] + _perm(dyv * s_ref[...], p_ref[...])
        xv = x_ref[...]
        r = lax.rsqrt(jnp.mean(xv * xv, axis=-1, keepdims=True) + NORM_EPS)
        xhat = xv * r
        dg_ref[...] += jnp.sum(dyv * xhat, axis=0, keepdims=True)
        dxh = dyv * g_ref[...]
        dx = r * (dxh - xhat * jnp.mean(dxh * xhat, axis=-1, keepdims=True))
        if res is not None:
            dx = dx + refs[pos][...]
        dx_ref[...] = dx.astype(dx_ref.dtype)

    wide = max(d, LANES)
    est = 2 * _nbytes((tm, wide), F32) * (2 + group + (2 if rope is not None else 0) + (1 if res is not None else 0))
    est += 8 * _nbytes((tm, wide), F32)
    return pl.pallas_call(
        body, name=name, grid=(H, T // tm), in_specs=in_specs,
        out_specs=(pl.BlockSpec((None, tm, d), lambda h, i: (h, i, 0)), pl.BlockSpec((1, d), lambda h, i: (0, 0))),
        out_shape=(jax.ShapeDtypeStruct((H, T, d), out_dtype), jax.ShapeDtypeStruct((1, d), F32)),
        compiler_params=_params(("arbitrary", "arbitrary"), est),
    )(*args)


def _group_sum(x, group, *, name, out_dtype=F32):
    HG, T, d = x.shape
    H = HG // group
    tm = _rows_tile(T, d)

    def body(x_ref, o_ref):
        acc = x_ref[0]
        for g in range(1, group):
            acc = acc + x_ref[g]
        o_ref[...] = acc.astype(o_ref.dtype)

    est = 2 * (group + 1) * _nbytes((tm, max(d, LANES)), F32)
    return pl.pallas_call(
        body, name=name, grid=(H, T // tm),
        in_specs=[pl.BlockSpec((group, tm, d), lambda h, i: (h, i, 0))],
        out_specs=pl.BlockSpec((None, tm, d), lambda h, i: (h, i, 0)),
        out_shape=jax.ShapeDtypeStruct((H, T, d), out_dtype),
        compiler_params=_params(("parallel", "parallel"), est),
    )(x)


def _delta(o, do, *, name):
    H, T, d = o.shape
    tm = _rows_tile(T, d)

    def body(o_ref, do_ref, dl_ref, dob_ref):
        dov = do_ref[...]
        dl = jnp.sum(o_ref[...] * dov, axis=-1, keepdims=True)
        dl_ref[...] = jnp.broadcast_to(dl, (tm, LANES))
        dob_ref[...] = dov.astype(BF16)

    spec = pl.BlockSpec((None, tm, d), lambda h, i: (h, i, 0))
    est = 2 * (3 * _nbytes((tm, max(d, LANES)), F32) + _nbytes((tm, LANES), F32))
    return pl.pallas_call(
        body, name=name, grid=(H, T // tm), in_specs=[spec, spec],
        out_specs=(pl.BlockSpec((None, tm, LANES), lambda h, i: (h, i, 0)), spec),
        out_shape=(jax.ShapeDtypeStruct((H, T, LANES), F32), jax.ShapeDtypeStruct((H, T, d), BF16)),
        compiler_params=_params(("parallel", "parallel"), est),
    )(o, do)


NT_DIMS = (((1,), (1,)), ((), ()))
TN_DIMS = (((0,), (0,)), ((), ()))
LOG2E = math.log2(math.e)
FLASH_CHUNK = 256
FLASH_ROW_PARTS = 4


def _flash_fwd(q, k, v, *, name, gather=()):
    H, T, dk = q.shape
    Hkv, _, dv = v.shape
    G = H // Hkv
    tq, tk = _tile(T, 1024), _tile(T, 4096)
    tp = _tile(tq, tq // FLASH_ROW_PARTS, SUBLANES_BF16)
    nk = T // tk

    n_r = len(gather)
    grid = (H, T // tq, nk)
    assert n_r == 0 or H >= 2

    def body(*refs):
        q_ref, k_ref, v_ref = refs[:3]
        o_ref, lse_ref, ob_ref = refs[3 + n_r:6 + n_r]
        m_ref, l_ref, acc_ref = refs[6 + 2 * n_r:9 + 2 * n_r]
        hi, qi, ki = pl.program_id(0), pl.program_id(1), pl.program_id(2)
        if n_r:
            ag_start, ag_forward, ag_finish = _gather_plan(refs[3:3 + n_r], refs[6 + n_r:6 + 2 * n_r],
                                                           *refs[9 + 2 * n_r:])
            pl.when((hi == 0) & (qi == 0) & (ki == 0))(ag_start)
            pl.when((hi == grid[0] - 1) & (qi == 0) & (ki == 0))(ag_forward)

        @pl.when(ki == 0)
        def _():
            m_ref[...] = jnp.full_like(m_ref, -jnp.inf)
            l_ref[...] = jnp.zeros_like(l_ref)
            acc_ref[...] = jnp.zeros_like(acc_ref)

        kv, vv = k_ref[...], v_ref[...]
        parts = [slice(part * tp, (part + 1) * tp) for part in range(tq // tp)]
        m_prev = [m_ref[rows, :] for rows in parts]
        l_prev = [l_ref[rows, :] for rows in parts]
        a_prev = [acc_ref[rows, :] for rows in parts]
        ss = [lax.dot_general(q_ref[rows, :], kv, NT_DIMS, preferred_element_type=F32) for rows in parts]
        m_new = [jnp.maximum(m, jnp.max(s, axis=-1, keepdims=True)) for m, s in zip(m_prev, ss)]
        alpha = [jnp.exp2(m - mn) for m, mn in zip(m_prev, m_new)]
        ps = [jnp.exp2(s - mn) for s, mn in zip(ss, m_new)]
        l_new = [a * l + jnp.sum(p, axis=-1, keepdims=True) for a, l, p in zip(alpha, l_prev, ps)]
        pv = [jnp.dot(p.astype(BF16), vv, preferred_element_type=F32) for p in ps]
        for rows, mn, ln, a, acc, o in zip(parts, m_new, l_new, alpha, a_prev, pv):
            m_ref[rows, :] = mn
            l_ref[rows, :] = ln
            acc_ref[rows, :] = a * acc + o

        @pl.when(ki == nk - 1)
        def _():
            l = l_ref[...]
            o = acc_ref[...] / l
            o_ref[...] = o
            ob_ref[...] = o.astype(BF16)
            lse_ref[...] = jnp.broadcast_to(m_ref[...] + jnp.log(l) * LOG2E, (tq, LANES))

        if n_r:
            pl.when((hi == grid[0] - 1) & (qi == grid[1] - 1) & (ki == grid[2] - 1))(ag_finish)

    est = 2 * (_nbytes((tq, dk), BF16) + _nbytes((tk, dk + dv), BF16) + _nbytes((tq, dv + LANES), F32))
    est += 4 * _nbytes((tq, tk), F32) + 3 * _nbytes((tq, dv + 3 * LANES), F32)
    outs = pl.pallas_call(
        body, name=name, grid=grid,
        in_specs=[pl.BlockSpec((None, tq, dk), lambda h, i, j: (h, i, 0)),
                  pl.BlockSpec((None, tk, dk), lambda h, i, j: (h // G, j, 0)),
                  pl.BlockSpec((None, tk, dv), lambda h, i, j: (h // G, j, 0))] + [HBM_SPEC] * n_r,
        out_specs=[pl.BlockSpec((None, tq, dv), lambda h, i, j: (h, i, 0)),
                   pl.BlockSpec((None, tq, LANES), lambda h, i, j: (h, i, 0)),
                   pl.BlockSpec((None, tq, dv), lambda h, i, j: (h, i, 0))] + [HBM_SPEC] * n_r,
        out_shape=[jax.ShapeDtypeStruct((H, T, dv), F32), jax.ShapeDtypeStruct((H, T, LANES), F32),
                   jax.ShapeDtypeStruct((H, T, dv), BF16)] + _gathered_shapes(gather),
        scratch_shapes=[pltpu.VMEM((tq, 1), F32), pltpu.VMEM((tq, 1), F32), pltpu.VMEM((tq, dv), F32)]
                       + (_gather_scratch(n_r) if n_r else []),
        compiler_params=_params(("arbitrary",) * 3 if n_r else ("parallel", "parallel", "arbitrary"), est),
    )(q, k, v, *gather)
    return (outs[0], outs[1], outs[2], outs[3:]) if n_r else (outs[0], outs[1], outs[2])


def _flash_bwd(q, k, v, do, lse2, delta, *, name, rider=None):
    H, T, dk = q.shape
    Hkv, _, dv = v.shape
    G = H // Hkv
    tq, tk = _tile(T, 1024), _tile(T, 2048)
    tc = _tile(tk, FLASH_CHUNK)

    kind, carried = rider if rider is not None else (None, ())
    n_r = len(carried)
    grid = (H, T // tk, T // tq)
    if kind == "scatter":
        plan, rider_scratch = _scatter_plan, _scatter_scratch(n_r)
        rider_shapes = [jax.ShapeDtypeStruct(p.shape, p.dtype) for p in carried]
    elif kind == "exchange":
        plan, rider_scratch = _exchange_plan, _exchange_scratch(n_r)
        rider_shapes = [jax.ShapeDtypeStruct((N_CHIPS,) + g.shape[2:], g.dtype) for g in carried]
    else:
        assert kind is None
        rider_scratch, rider_shapes = [], []

    def body(*refs):
        q_ref, k_ref, v_ref, do_ref, lse_ref, dl_ref = refs[:6]
        dq_ref, dk_ref, dv_ref = refs[6 + n_r:9 + n_r]
        hi, ki, qi = pl.program_id(0), pl.program_id(1), pl.program_id(2)
        if n_r:
            rider_start, rider_finish = plan(refs[6:6 + n_r], refs[9 + n_r:9 + 2 * n_r], *refs[9 + 2 * n_r:])
            pl.when((hi == 0) & (ki == 0) & (qi == 0))(rider_start)
        rows = pl.ds(pl.multiple_of(qi * tq, tq), tq)

        @pl.when(qi == 0)
        def _():
            dk_ref[...] = jnp.zeros_like(dk_ref)
            dv_ref[...] = jnp.zeros_like(dv_ref)

        @pl.when(ki == 0)
        def _():
            dq_ref[rows, :] = jnp.zeros((tq, dk), F32)

        qv, dov = q_ref[...], do_ref[...]
        lse2 = lse_ref[:, :1]
        dl = dl_ref[:, :1]
        chunks = [slice(c * tc, (c + 1) * tc) for c in range(tk // tc)]
        kcs = [k_ref[ks, :] for ks in chunks]
        vcs = [v_ref[ks, :] for ks in chunks]
        dv_old = [dv_ref[ks, :] for ks in chunks]
        dk_old = [dk_ref[ks, :] for ks in chunks]
        dq_old = dq_ref[rows, :]
        ss = [lax.dot_general(qv, kc, NT_DIMS, preferred_element_type=F32) for kc in kcs]
        dps = [lax.dot_general(dov, vc, NT_DIMS, preferred_element_type=F32) for vc in vcs]
        ps = [jnp.exp2(s - lse2) for s in ss]
        dss = [(p * (dp - dl)).astype(BF16) for p, dp in zip(ps, dps)]
        pbs = [p.astype(BF16) for p in ps]
        dvs = [lax.dot_general(pb, dov, TN_DIMS, preferred_element_type=F32) for pb in pbs]
        dks = [lax.dot_general(ds, qv, TN_DIMS, preferred_element_type=F32) for ds in dss]
        dqs = [jnp.dot(ds, kc, preferred_element_type=F32) for ds, kc in zip(dss, kcs)]
        for ks, old, new in zip(chunks, dv_old, dvs):
            dv_ref[ks, :] = old + new
        for ks, old, new in zip(chunks, dk_old, dks):
            dk_ref[ks, :] = old + new
        dq_c = dqs[0]
        for extra in dqs[1:]:
            dq_c = dq_c + extra
        dq_ref[rows, :] = dq_old + dq_c

        if n_r:
            pl.when((hi == grid[0] - 1) & (ki == grid[1] - 1) & (qi == grid[2] - 1))(rider_finish)

    est = 2 * (_nbytes((tq, dk + dv), BF16) + _nbytes((tk, dk + dv), BF16) + 2 * _nbytes((tq, LANES), F32))
    est += 2 * (_nbytes((T, dk), F32) + _nbytes((tk, dk + dv), F32)) + 10 * _nbytes((tq, tc), F32)
    outs = pl.pallas_call(
        body, name=name, grid=grid,
        in_specs=[pl.BlockSpec((None, tq, dk), lambda h, j, i: (h, i, 0)),
                  pl.BlockSpec((None, tk, dk), lambda h, j, i: (h // G, j, 0)),
                  pl.BlockSpec((None, tk, dv), lambda h, j, i: (h // G, j, 0)),
                  pl.BlockSpec((None, tq, dv), lambda h, j, i: (h, i, 0)),
                  pl.BlockSpec((None, tq, LANES), lambda h, j, i: (h, i, 0)),
                  pl.BlockSpec((None, tq, LANES), lambda h, j, i: (h, i, 0))] + [HBM_SPEC] * n_r,
        out_specs=[pl.BlockSpec((None, T, dk), lambda h, j, i: (h, 0, 0)),
                   pl.BlockSpec((None, tk, dk), lambda h, j, i: (h, j, 0)),
                   pl.BlockSpec((None, tk, dv), lambda h, j, i: (h, j, 0))] + [HBM_SPEC] * n_r,
        out_shape=[jax.ShapeDtypeStruct((H, T, dk), F32), jax.ShapeDtypeStruct((H, T, dk), F32),
                   jax.ShapeDtypeStruct((H, T, dv), F32)] + rider_shapes,
        scratch_shapes=rider_scratch,
        compiler_params=_params(("arbitrary", "arbitrary", "arbitrary"), est),
    )(q, k, v, do, lse2, delta, *carried)
    return (outs[0], outs[1], outs[2], outs[3:]) if n_r else tuple(outs)


def _swa_specs(G, d, n_blocks, lanes, gpb):
    B = SWA_BLOCK
    prev = lambda j, i: (j, jnp.maximum(i - 1, 0), 0)
    cur = lambda j, i: (j, i, 0)
    nxt = lambda j, i: (j, jnp.minimum(i + 1, n_blocks - 1), 0)
    q_specs = [pl.BlockSpec((gpb * G, B, lanes), m) for m in (prev, cur, nxt)]
    kv_specs = [pl.BlockSpec((gpb, B, d), m) for m in (prev, cur, nxt)]
    return q_specs, kv_specs, cur


def _swa_parts(G, gpb):
    gp = G // SWA_HEAD_PARTS
    return gp, [(g, slice(g * G + part * gp, g * G + (part + 1) * gp)) for g in range(gpb) for part in range(SWA_HEAD_PARTS)]


def _swa_bias(i, T):
    B = SWA_BLOCK
    row = lax.broadcasted_iota(jnp.int32, (B, 3 * B), 0)
    col = lax.broadcasted_iota(jnp.int32, (B, 3 * B), 1)
    kpos = (i - 1) * B + col
    valid = (col >= row) & (col <= row + 2 * SWA_WINDOW) & (kpos >= 0) & (kpos < T)
    return jnp.where(valid, 0.0, -jnp.inf)


def _swa_fwd(q, k, v, sink, *, name):
    Hq, T, d = q.shape
    Hkv = k.shape[0]
    G = Hq // Hkv
    B = SWA_BLOCK
    nb = T // B
    gpb = _tile(Hkv, SWA_GROUPS, 1)
    _, kv_specs, cur = _swa_specs(G, d, nb, d, gpb)

    def body(q_ref, k0, k1, k2, v0, v1, v2, sink_ref, o_ref, lse_ref, ob_ref):
        i = pl.program_id(1)
        kvs = [jnp.concatenate([k0[g], k1[g], k2[g]], axis=0) for g in range(gpb)]
        vvs = [jnp.concatenate([v0[g], v1[g], v2[g]], axis=0) for g in range(gpb)]
        bias = _swa_bias(i, T)[None]
        gp, parts = _swa_parts(G, gpb)
        sks = [sink_ref[hs] for _, hs in parts]
        ss = [lax.dot_general(q_ref[hs].reshape(gp * B, d), kvs[g], NT_DIMS, preferred_element_type=F32) for g, hs in parts]
        ss = [s.reshape(gp, B, 3 * B) + bias for s in ss]
        ms = [jnp.maximum(jnp.max(s, axis=-1, keepdims=True), sk) for s, sk in zip(ss, sks)]
        ps = [jnp.exp(s - m) for s, m in zip(ss, ms)]
        dens = [jnp.sum(p, axis=-1, keepdims=True) + jnp.exp(sk - m) for p, sk, m in zip(ps, sks, ms)]
        pns = [(p * (1.0 / den)).reshape(gp * B, 3 * B).astype(BF16) for p, den in zip(ps, dens)]
        os_ = [jnp.dot(pn, vvs[g], preferred_element_type=F32).reshape(gp, B, d) for pn, (g, _) in zip(pns, parts)]
        for (_, hs), o, m, den in zip(parts, os_, ms, dens):
            o_ref[hs] = o
            ob_ref[hs] = o.astype(BF16)
            lse_ref[hs] = jnp.broadcast_to(m + jnp.log(den), (gp, B, LANES))

    GG = gpb * G
    est = 2 * (_nbytes((GG, B, LANES), BF16) + 6 * gpb * _nbytes((B, LANES), BF16) + 2 * _nbytes((GG, B, LANES), F32))
    est += 8 * _nbytes((GG * B, 3 * B), F32)
    return pl.pallas_call(
        body, name=name, grid=(Hkv // gpb, nb),
        in_specs=[pl.BlockSpec((GG, B, d), cur)] + kv_specs + kv_specs + [pl.BlockSpec((GG, 1, 1), lambda j, i: (j, 0, 0))],
        out_specs=(pl.BlockSpec((GG, B, d), cur), pl.BlockSpec((GG, B, LANES), cur), pl.BlockSpec((GG, B, d), cur)),
        out_shape=(jax.ShapeDtypeStruct((Hq, T, d), F32), jax.ShapeDtypeStruct((Hq, T, LANES), F32),
                   jax.ShapeDtypeStruct((Hq, T, d), BF16)),
        compiler_params=_params(("parallel", "parallel"), est),
    )(q, k, k, k, v, v, v, sink)


def _swa_dq(q, k, v, o, do, lse, sink, *, name):
    Hq, T, d = q.shape
    Hkv = k.shape[0]
    G = Hq // Hkv
    B = SWA_BLOCK
    nb = T // B
    gpb = _tile(Hkv, SWA_GROUPS, 1)
    _, kv_specs, cur = _swa_specs(G, d, nb, d, gpb)

    def body(q_ref, do_ref, lse_ref, o_ref, k0, k1, k2, v0, v1, v2, sink_ref, dq_ref, dsink_ref, dl_ref, dob_ref):
        i = pl.program_id(1)
        kvs = [jnp.concatenate([k0[g], k1[g], k2[g]], axis=0) for g in range(gpb)]
        vvs = [jnp.concatenate([v0[g], v1[g], v2[g]], axis=0) for g in range(gpb)]
        bias = _swa_bias(i, T)[None]
        gp, parts = _swa_parts(G, gpb)
        lses = [lse_ref[hs, :, :1] for _, hs in parts]
        dovs = [do_ref[hs] for _, hs in parts]
        dls = [jnp.sum(o_ref[hs] * dov, axis=-1, keepdims=True) for (_, hs), dov in zip(parts, dovs)]
        dobs = [dov.astype(BF16) for dov in dovs]
        ss = [lax.dot_general(q_ref[hs].reshape(gp * B, d), kvs[g], NT_DIMS, preferred_element_type=F32) for g, hs in parts]
        dps = [lax.dot_general(dob.reshape(gp * B, d), vvs[g], NT_DIMS, preferred_element_type=F32)
               for dob, (g, _) in zip(dobs, parts)]
        ps = [jnp.exp(s.reshape(gp, B, 3 * B) + bias - lse) for s, lse in zip(ss, lses)]
        dss = [(p * (dp.reshape(gp, B, 3 * B) - dl)).reshape(gp * B, 3 * B).astype(BF16)
               for p, dp, dl in zip(ps, dps, dls)]
        dqs = [jnp.dot(ds, kvs[g], preferred_element_type=F32).reshape(gp, B, d) for ds, (g, _) in zip(dss, parts)]
        dsks = [-jnp.sum(jnp.exp(sink_ref[hs] - lse) * dl, axis=1, keepdims=True)
                for (_, hs), lse, dl in zip(parts, lses, dls)]

        @pl.when(i == 0)
        def _():
            dsink_ref[...] = jnp.zeros_like(dsink_ref)

        for (_, hs), dq, dsk, dl, dob in zip(parts, dqs, dsks, dls, dobs):
            dq_ref[hs] = dq
            dsink_ref[hs] += jnp.broadcast_to(dsk, (gp, 1, LANES))
            dl_ref[hs] = jnp.broadcast_to(dl, (gp, B, LANES))
            dob_ref[hs] = dob

    GG = gpb * G
    est = 2 * (2 * _nbytes((GG, B, LANES), BF16) + 6 * gpb * _nbytes((B, LANES), BF16) + 5 * _nbytes((GG, B, LANES), F32))
    est += 8 * _nbytes((GG * B, 3 * B), F32)
    q_spec, l_spec = pl.BlockSpec((GG, B, d), cur), pl.BlockSpec((GG, B, LANES), cur)
    return pl.pallas_call(
        body, name=name, grid=(Hkv // gpb, nb),
        in_specs=[q_spec, q_spec, l_spec, q_spec] + kv_specs + kv_specs + [pl.BlockSpec((GG, 1, 1), lambda j, i: (j, 0, 0))],
        out_specs=(q_spec, pl.BlockSpec((GG, 1, LANES), lambda j, i: (j, 0, 0)), l_spec, q_spec),
        out_shape=(jax.ShapeDtypeStruct((Hq, T, d), F32), jax.ShapeDtypeStruct((Hq, 1, LANES), F32),
                   jax.ShapeDtypeStruct((Hq, T, LANES), F32), jax.ShapeDtypeStruct((Hq, T, d), BF16)),
        compiler_params=_params(("arbitrary", "arbitrary"), est),
    )(q, do, lse, o, k, k, k, v, v, v, sink)


def _swa_dkv(q, k, v, do, lse, delta, *, name):
    Hq, T, d = q.shape
    Hkv = k.shape[0]
    G = Hq // Hkv
    B = SWA_BLOCK
    nb = T // B
    gpb = _tile(Hkv, SWA_GROUPS, 1)
    q_specs, _, cur = _swa_specs(G, d, nb, d, gpb)
    l_specs, _, _ = _swa_specs(G, d, nb, LANES, gpb)

    def body(k_ref, v_ref, q0, q1, q2, d0, d1, d2, l0, l1, l2, e0, e1, e2, dk_ref, dv_ref):
        b = pl.program_id(1)
        row = lax.broadcasted_iota(jnp.int32, (B, B), 0)
        col = lax.broadcasted_iota(jnp.int32, (B, B), 1)
        biases = []
        for part in range(3):
            qpos = (b + part - 1) * B + row
            diff = (part - 1) * B + row - col
            valid = (diff >= -SWA_WINDOW) & (diff <= SWA_WINDOW) & (qpos >= 0) & (qpos < T)
            biases.append(jnp.where(valid, 0.0, -jnp.inf)[None])
        chains = [(g, part) for g in range(gpb) for part in range(3)]
        heads = [slice(g * G, (g + 1) * G) for g, _ in chains]
        kvs = [k_ref[g] for g, _ in chains]
        vvs = [v_ref[g] for g, _ in chains]
        qvs = [(q0, q1, q2)[part][hs].reshape(G * B, d) for (_, part), hs in zip(chains, heads)]
        dovs = [(d0, d1, d2)[part][hs].reshape(G * B, d) for (_, part), hs in zip(chains, heads)]
        lses = [(l0, l1, l2)[part][hs, :, :1] for (_, part), hs in zip(chains, heads)]
        dls = [(e0, e1, e2)[part][hs, :, :1] for (_, part), hs in zip(chains, heads)]
        ss = [lax.dot_general(qv, kv, NT_DIMS, preferred_element_type=F32) for qv, kv in zip(qvs, kvs)]
        dps = [lax.dot_general(dov, vv, NT_DIMS, preferred_element_type=F32) for dov, vv in zip(dovs, vvs)]
        ps = [jnp.exp(s.reshape(G, B, B) + biases[part] - lse) for s, (_, part), lse in zip(ss, chains, lses)]
        dss = [(p * (dp.reshape(G, B, B) - dl)).reshape(G * B, B).astype(BF16) for p, dp, dl in zip(ps, dps, dls)]
        pbs = [p.reshape(G * B, B).astype(BF16) for p in ps]
        dvs = [lax.dot_general(pb, dov, TN_DIMS, preferred_element_type=F32) for pb, dov in zip(pbs, dovs)]
        dks = [lax.dot_general(ds, qv, TN_DIMS, preferred_element_type=F32) for ds, qv in zip(dss, qvs)]
        for g in range(gpb):
            dk_ref[g] = dks[3 * g] + dks[3 * g + 1] + dks[3 * g + 2]
            dv_ref[g] = (dvs[3 * g] + dvs[3 * g + 1] + dvs[3 * g + 2]).astype(BF16)

    GG = gpb * G
    est = 2 * (6 * _nbytes((GG, B, LANES), BF16) + 6 * _nbytes((GG, B, LANES), F32) + 4 * gpb * _nbytes((B, LANES), F32))
    est += 10 * _nbytes((GG * B, B), F32)
    kspec = pl.BlockSpec((gpb, B, d), cur)
    return pl.pallas_call(
        body, name=name, grid=(Hkv // gpb, nb),
        in_specs=[kspec, kspec] + q_specs + q_specs + l_specs + l_specs,
        out_specs=(kspec, kspec),
        out_shape=(jax.ShapeDtypeStruct((Hkv, T, d), F32), jax.ShapeDtypeStruct((Hkv, T, d), BF16)),
        compiler_params=_params(("parallel", "parallel"), est),
    )(k, v, q, q, q, do, do, do, lse, lse, lse, delta, delta, delta)


def _loss_head(y, target, *, name):
    T, D = y.shape
    tm = _tile(T, 512)

    def body(y_ref, t_ref, dy_ref, s_ref):
        @pl.when(pl.program_id(0) == 0)
        def _():
            s_ref[...] = jnp.zeros_like(s_ref)

        e = y_ref[...] - t_ref[...]
        dy_ref[...] = e / D
        s_ref[...] += jnp.sum(jnp.sum(e * e, axis=-1, keepdims=True), axis=0, keepdims=True)

    spec = pl.BlockSpec((tm, D), lambda i: (i, 0))
    return pl.pallas_call(
        body, name=name, grid=(T // tm,), in_specs=[spec, spec],
        out_specs=(spec, pl.BlockSpec((1, 1), lambda i: (0, 0))),
        out_shape=(jax.ShapeDtypeStruct((T, D), F32), jax.ShapeDtypeStruct((1, 1), F32)),
        compiler_params=_params(("arbitrary",), 8 * _nbytes((tm, D), F32)),
    )(y, target)


def _adamw(w, g, m, v, *, name):
    R, C = w.shape
    tr = _tile(R, max(8, (1 << 19) // max(C, LANES) // 8 * 8), 8)

    def body(w_ref, g_ref, m_ref, v_ref, d_ref, nm_ref, nv_ref):
        gv = g_ref[...]
        nm = ADAM_B1 * m_ref[...] + (1.0 - ADAM_B1) * gv
        nv = ADAM_B2 * v_ref[...] + (1.0 - ADAM_B2) * jnp.square(gv)
        m_hat = nm / (1.0 - ADAM_B1 ** ADAM_STEP)
        v_hat = nv / (1.0 - ADAM_B2 ** ADAM_STEP)
        d_ref[...] = -ADAM_LR * (m_hat / (jnp.sqrt(v_hat) + ADAM_EPS) + ADAM_WD * w_ref[...])
        nm_ref[...] = nm
        nv_ref[...] = nv

    spec = pl.BlockSpec((tr, C), lambda i: (i, 0))
    sds = jax.ShapeDtypeStruct((R, C), F32)
    return pl.pallas_call(
        body, name=name, grid=(R // tr,), in_specs=[spec] * 4, out_specs=(spec,) * 3, out_shape=(sds,) * 3,
        compiler_params=_params(("parallel",), 16 * _nbytes((tr, max(C, LANES)), F32)),
    )(w, g, m, v)


def _to_bf16(w, *, name):
    R, C = w.shape
    tr = _comm_rows_tile(R, C)

    def body(w_ref, o_ref):
        o_ref[...] = w_ref[...].astype(BF16)

    spec = pl.BlockSpec((tr, C), lambda i: (i, 0))
    return pl.pallas_call(
        body, name=name, grid=(R // tr,), in_specs=[spec], out_specs=spec, out_shape=jax.ShapeDtypeStruct((R, C), BF16),
        compiler_params=_params(("parallel",), 6 * _nbytes((tr, max(C, LANES)), F32)),
    )(w)


def _comm_rows_tile(R, L):
    return _tile(R, max(SUBLANES_BF16, (1 << 19) // L // SUBLANES_BF16 * SUBLANES_BF16), SUBLANES_BF16)


def _pair_add(g, recv, c_idx, *, name):
    _, _, R, L = g.shape
    tr = _comm_rows_tile(R, L)

    def body(c_ref, g_ref, r_ref, o_ref):
        o_ref[...] = (g_ref[...] + r_ref[...]).astype(BF16)

    grid_spec = pltpu.PrefetchScalarGridSpec(
        num_scalar_prefetch=1, grid=(N_CHIPS, R // tr),
        in_specs=[pl.BlockSpec((None, None, tr, L), lambda j, i, c: (j, c[0], i, 0)),
                  pl.BlockSpec((None, tr, L), lambda j, i, c: (j, i, 0))],
        out_specs=pl.BlockSpec((None, tr, L), lambda j, i, c: (j, i, 0)))
    return pl.pallas_call(
        body, name=name, grid_spec=grid_spec, out_shape=jax.ShapeDtypeStruct((N_CHIPS, R, L), BF16),
        compiler_params=_params(("parallel", "parallel"), 8 * _nbytes((tr, L), F32)),
    )(c_idx, g, recv)


def _sum_chips(q, c_idx, *, name):
    _, R, L = q.shape
    tr = _comm_rows_tile(R, L)

    def body(c_ref, q_ref, o_ref):
        acc = q_ref[0].astype(F32)
        for j in range(1, N_CHIPS):
            acc = acc + q_ref[j].astype(F32)
        o_ref[...] = acc

    grid_spec = pltpu.PrefetchScalarGridSpec(
        num_scalar_prefetch=1, grid=(R // tr,),
        in_specs=[pl.BlockSpec((N_CHIPS, tr, L), lambda i, c: (0, i, 0))],
        out_specs=pl.BlockSpec((None, tr, L), lambda i, c: (c[0], i, 0)))
    return pl.pallas_call(
        body, name=name, grid_spec=grid_spec, out_shape=jax.ShapeDtypeStruct((2, R, L), F32),
        compiler_params=_params(("parallel",), 10 * _nbytes((tr, L), F32)),
    )(c_idx, q)


HBM_SPEC = pl.BlockSpec(memory_space=pltpu.HBM)


def _position():
    return lax.axis_index("x"), lax.axis_index("y"), lax.axis_index("c")


def _other_chips(x, y):
    return [(1 - x, y), (x, 1 - y), (1 - x, 1 - y)]


AG_COPIES = 7


def _gather_plan(w_refs, out_refs, send_sems, recv_sems, local_sems):
    n = len(w_refs)
    x, y, c = _position()
    me, sibling = (x, y, c), (x, y, 1 - c)
    chips = _other_chips(x, y)

    def copy(i, k, block, to, src=None):
        px, py, pc = block
        slot = out_refs[i].at[4 * px + 2 * py + pc]
        return pltpu.make_async_remote_copy(
            src_ref=slot if src is None else src, dst_ref=slot, send_sem=send_sems.at[AG_COPIES * i + k],
            recv_sem=recv_sems.at[AG_COPIES * i + k], device_id=to, device_id_type=MESH)

    def local(i):
        return pltpu.make_async_copy(w_refs[i].at[c], out_refs[i].at[4 * x + 2 * y + c], local_sems.at[i])

    def first(i):
        own = w_refs[i].at[c]
        return [copy(i, 0, me, sibling, src=own)] + [copy(i, 1 + j, me, (*chip, c), src=own)
                                                     for j, chip in enumerate(chips)]

    def passed(i):
        return [copy(i, 4 + j, (*chip, c), sibling) for j, chip in enumerate(chips)]

    def start():
        for i in range(n):
            local(i).start()
            for cp in first(i):
                cp.start()

    def forward():
        for i in range(n):
            for j, chip in enumerate(chips):
                copy(i, 1 + j, (*chip, c), me).wait_recv()
                passed(i)[j].start()

    def finish():
        for i in range(n):
            copy(i, 0, sibling, me).wait_recv()
            for j, chip in enumerate(chips):
                copy(i, 4 + j, (*chip, 1 - c), me).wait_recv()
        for i in range(n):
            for cp in first(i) + passed(i):
                cp.wait_send()
            local(i).wait()

    return start, forward, finish


def _gather_scratch(n):
    return [pltpu.SemaphoreType.DMA((AG_COPIES * n,)), pltpu.SemaphoreType.DMA((AG_COPIES * n,)),
            pltpu.SemaphoreType.DMA((n,))]


def _gathered_shapes(ws):
    return [jax.ShapeDtypeStruct((2 * N_CHIPS,) + w.shape[1:], w.dtype) for w in ws]


def _all_gather_halves(ws, *, name):
    n = len(ws)

    def body(*refs):
        for step in _gather_plan(refs[:n], refs[n:2 * n], *refs[2 * n:]):
            step()

    return pl.pallas_call(
        body, name=name, in_specs=[HBM_SPEC] * n, out_specs=[HBM_SPEC] * n, out_shape=_gathered_shapes(ws),
        scratch_shapes=_gather_scratch(n),
    )(*ws)


def _exchange_plan(g_refs, out_refs, send_sems, recv_sems):
    n = len(g_refs)
    x, y, c = _position()

    def copies():
        return [pltpu.make_async_remote_copy(
            src_ref=g_refs[i].at[j, 1 - c], dst_ref=out_refs[i].at[j], send_sem=send_sems.at[N_CHIPS * i + j],
            recv_sem=recv_sems.at[N_CHIPS * i + j], device_id=(x, y, 1 - c), device_id_type=MESH)
            for i in range(n) for j in range(N_CHIPS)]

    def start():
        for cp in copies():
            cp.start()

    def finish():
        for cp in copies():
            cp.wait()

    return start, finish


def _exchange_scratch(n):
    return [pltpu.SemaphoreType.DMA((N_CHIPS * n,)), pltpu.SemaphoreType.DMA((N_CHIPS * n,))]


def _sibling_exchange(gs, *, name):
    n = len(gs)

    def body(*refs):
        for step in _exchange_plan(refs[:n], refs[n:2 * n], *refs[2 * n:]):
            step()

    return pl.pallas_call(
        body, name=name, in_specs=[HBM_SPEC] * n, out_specs=[HBM_SPEC] * n,
        out_shape=[jax.ShapeDtypeStruct((N_CHIPS,) + g.shape[2:], g.dtype) for g in gs],
        scratch_shapes=_exchange_scratch(n),
    )(*gs)


def _scatter_plan(p_refs, q_refs, send_sems, recv_sems, local_sems):
    n = len(p_refs)
    others = N_CHIPS - 1
    x, y, c = _position()
    me = 2 * x + y
    chips = _other_chips(x, y)

    def copy(i, k, chip, src_slot, dst_slot):
        return pltpu.make_async_remote_copy(
            src_ref=p_refs[i].at[src_slot], dst_ref=q_refs[i].at[dst_slot], send_sem=send_sems.at[others * i + k],
            recv_sem=recv_sems.at[others * i + k], device_id=(*chip, c), device_id_type=MESH)

    def local(i):
        return pltpu.make_async_copy(p_refs[i].at[me], q_refs[i].at[me], local_sems.at[i])

    def sends(i):
        return [copy(i, k, chip, 2 * chip[0] + chip[1], me) for k, chip in enumerate(chips)]

    def start():
        for i in range(n):
            local(i).start()
            for cp in sends(i):
                cp.start()

    def finish():
        for i in range(n):
            for k, chip in enumerate(chips):
                copy(i, k, chip, me, 2 * chip[0] + chip[1]).wait_recv()
        for i in range(n):
            for cp in sends(i):
                cp.wait_send()
            local(i).wait()

    return start, finish


def _scatter_scratch(n):
    others = N_CHIPS - 1
    return [pltpu.SemaphoreType.DMA((others * n,)), pltpu.SemaphoreType.DMA((others * n,)),
            pltpu.SemaphoreType.DMA((n,))]


def _chip_scatter(ps, *, name):
    n = len(ps)

    def body(*refs):
        for step in _scatter_plan(refs[:n], refs[n:2 * n], *refs[2 * n:]):
            step()

    return pl.pallas_call(
        body, name=name, in_specs=[HBM_SPEC] * n, out_specs=[HBM_SPEC] * n,
        out_shape=[jax.ShapeDtypeStruct(p.shape, p.dtype) for p in ps], scratch_shapes=_scatter_scratch(n),
    )(*ps)


def _sibling_share(fs, *, name):
    n = len(fs)

    def body(*refs):
        in_refs, out_refs = refs[:n], refs[n:2 * n]
        send_sems, recv_sems = refs[2 * n:]
        x, y, c = _position()

        def copy(i, half):
            return pltpu.make_async_remote_copy(
                src_ref=in_refs[i].at[half], dst_ref=out_refs[i].at[half], send_sem=send_sems.at[i],
                recv_sem=recv_sems.at[i], device_id=(x, y, 1 - c), device_id_type=MESH)

        sends = [copy(i, c) for i in range(n)]
        for cp in sends:
            cp.start()
        for i in range(n):
            copy(i, 1 - c).wait_recv()
        for cp in sends:
            cp.wait_send()

    return pl.pallas_call(
        body, name=name, in_specs=[HBM_SPEC] * n, out_specs=[HBM_SPEC] * n,
        out_shape=[jax.ShapeDtypeStruct(f.shape, f.dtype) for f in fs],
        input_output_aliases={i: i for i in range(n)},
        scratch_shapes=[pltpu.SemaphoreType.DMA((n,)), pltpu.SemaphoreType.DMA((n,))],
    )(*fs)


def _all_reduce_small(s, *, name):
    R, L = s.shape
    n_dev = 2 * N_CHIPS

    def body(s_ref, out_ref, buf, send_sems, recv_sems, local_sem):
        x, y, c = _position()
        me, sibling = (x, y, c), (x, y, 1 - c)
        chips = _other_chips(x, y)

        def slot(px, py, pc):
            return buf.at[4 * px + 2 * py + pc]

        def copy(k, block, to, src=None):
            return pltpu.make_async_remote_copy(
                src_ref=slot(*block) if src is None else src, dst_ref=slot(*block),
                send_sem=send_sems.at[k], recv_sem=recv_sems.at[k], device_id=to, device_id_type=MESH)

        mine = pltpu.make_async_copy(s_ref, slot(*me), local_sem)
        mine.start()
        first = [copy(0, me, sibling, src=s_ref)]
        first += [copy(1 + j, me, (*chip, c), src=s_ref) for j, chip in enumerate(chips)]
        for cp in first:
            cp.start()
        passed = [copy(4 + j, (*chip, c), sibling) for j, chip in enumerate(chips)]
        for j, chip in enumerate(chips):
            copy(1 + j, (*chip, c), me).wait_recv()
            passed[j].start()
        copy(0, sibling, me).wait_recv()
        for j, chip in enumerate(chips):
            copy(4 + j, (*chip, 1 - c), me).wait_recv()
        for cp in first + passed:
            cp.wait_send()
        mine.wait()
        acc = buf[0]
        for j in range(1, n_dev):
            acc = acc + buf[j]
        out_ref[...] = acc

    vmem = pl.BlockSpec(memory_space=pltpu.VMEM)
    return pl.pallas_call(
        body, name=name, in_specs=[vmem], out_specs=vmem, out_shape=jax.ShapeDtypeStruct((R, L), F32),
        scratch_shapes=[pltpu.VMEM((n_dev, R, L), F32), pltpu.SemaphoreType.DMA((7,)), pltpu.SemaphoreType.DMA((7,)),
                        pltpu.SemaphoreType.DMA],
    )(s)


def _rope_cos_sin(pos, dim, theta):
    inv = jnp.float32(theta) ** (-jnp.arange(0, dim, 2, dtype=F32) / dim)
    ang = pos.astype(F32)[:, None] * inv[None, :]
    return jnp.cos(ang), jnp.sin(ang)


def _rope_tables(T, d, segments):
    P = np.zeros((d, d), np.float32)
    c_parts, s_parts, at = [], [], 0
    for start, size, cos, sin in segments:
        half = size // 2
        if start > at:
            c_parts.append(jnp.ones((T, start - at), F32))
            s_parts.append(jnp.zeros((T, start - at), F32))
        c_parts += [cos, cos]
        s_parts += [-sin, sin]
        at = start + size
        for p in range(half):
            P[start + half + p, start + p] = 1.0
            P[start + p, start + half + p] = 1.0
    if at < d:
        c_parts.append(jnp.ones((T, d - at), F32))
        s_parts.append(jnp.zeros((T, d - at), F32))
    return jnp.concatenate(c_parts, axis=1), jnp.concatenate(s_parts, axis=1), jnp.asarray(P, BF16)


def _heads(t, H, d):
    return t.reshape(t.shape[0], H, d).transpose(1, 0, 2)


def _unheads(t):
    H, T, d = t.shape
    return t.transpose(1, 0, 2).reshape(T, H * d)


def _dw(a, b, *, name, axis):
    K, N = a.shape[1], b.shape[1]
    if axis == 1:
        return _mm(a, b, mode="tn", name=name).reshape(N_CHIPS, K // N_CHIPS, N)
    if (N // N_CHIPS) % LANES == 0:
        return _mm(a, b, mode="tn", name=name, split=N_CHIPS)
    return _mm(a, b, mode="tn", name=name).reshape(K, N_CHIPS, N // N_CHIPS).transpose(1, 0, 2)


def _mlp_fwd(x, gain, w_up, w_down, tag):
    hm = _norm_fwd(x[None], gain, name=f"mlp{tag}_norm")[0]
    u, act = _mm(hm, w_up, mode="nn", name=f"mlp{tag}_up", epi="sqrelu")
    x_out = _mm(act, w_down, mode="nn", name=f"mlp{tag}_down", epi="add", extra=x)
    return x_out, (hm, u, act)


def _mlp_bwd(x, gain, w_up, w_down, saved, dxo, tag):
    hm, u, act = saved
    du = _mm(dxo, w_down, mode="nt", name=f"mlp{tag}_dact", epi="dsqrelu", extra=u, out_dtype=BF16)
    dw_down = _dw(act, dxo, name=f"mlp{tag}_dwdown", axis=1)
    dhm = _mm(du, w_up, mode="nt", name=f"mlp{tag}_dhm")
    dw_up = _dw(hm, du, name=f"mlp{tag}_dwup", axis=2)
    dx, dgain = _norm_bwd(x[None], gain, dhm[None], name=f"mlp{tag}_dnorm", res=dxo)
    return dx[0], dgain[0], dw_up, dw_down


def _local_step(x, target, W, small, late=None):
    T, D = x.shape
    W = dict(W)
    pos = jnp.arange(T)
    mla_cos, mla_sin = _rope_cos_sin(pos, MLA_ROPE, ROPE_THETA)
    row_cos, row_sin = _rope_cos_sin(pos // GRID_W, GQA_DIM // 2, AXIAL_THETA)
    col_cos, col_sin = _rope_cos_sin(pos % GRID_W, GQA_DIM // 2, AXIAL_THETA)
    swa_cos, swa_sin = _rope_cos_sin(pos, SWA_ROT, ROPE_THETA)
    rope_q = _rope_tables(T, MLA_QK, [(MLA_NOPE, MLA_ROPE, mla_cos, mla_sin)])
    rope_kr = _rope_tables(T, MLA_ROPE, [(0, MLA_ROPE, mla_cos, mla_sin)])
    half = GQA_DIM // 2
    rope_ax = _rope_tables(T, GQA_DIM, [(0, half, row_cos, row_sin), (half, half, col_cos, col_sin)])
    rope_sw = _rope_tables(T, SWA_DIM, [(0, SWA_ROT, swa_cos, swa_sin)])
    o1 = MLA_Q_LORA
    o2 = o1 + MLA_KV_LORA
    o3 = o2 + MLA_ROPE
    o4 = o3 + GQA_HEADS * GQA_DIM
    o5 = o4 + GQA_KV * GQA_DIM
    sc_a, sc_g, sc_s = MLA_QK ** -0.5, GQA_DIM ** -0.5, SWA_DIM ** -0.5
    kv_w = MLA_NOPE + MLA_V

    h0 = _norm_fwd(x[None], small["even_norm"], name="even_norm")[0]
    proj = _mm(h0, W["even_w_in"], mode="nn", name="even_in")
    c_q, c_kv, kr_raw = proj[:, :o1], proj[:, o1:o2], proj[:, o2:o3]
    qg_raw = _heads(proj[:, o3:o4], GQA_HEADS, GQA_DIM)
    kg_raw = _heads(proj[:, o4:o5], GQA_KV, GQA_DIM)
    vg = _heads(proj[:, o5:], GQA_KV, GQA_DIM).astype(BF16)
    cqn = _norm_fwd(c_q[None], small["mla_q_lat_norm"], name="q_lat_norm")[0]
    ckvn = _norm_fwd(c_kv[None], small["mla_kv_lat_norm"], name="kv_lat_norm")[0]
    qa_raw = _heads(_mm(cqn, W["mla_w_uq"], mode="nn", name="mla_uq"), MLA_HEADS, MLA_QK)
    kv = _mm(ckvn, W["mla_w_ukv"], mode="nn", name="mla_ukv").reshape(T, MLA_HEADS, kv_w)
    kn_raw = kv[:, :, :MLA_NOPE].transpose(1, 0, 2)
    va = kv[:, :, MLA_NOPE:].transpose(1, 0, 2).astype(BF16)
    q_a = _norm_fwd(qa_raw, small["mla_q_norm"], name="mla_q_prep", rope=rope_q, out_scale=sc_a * LOG2E)
    k_n = _norm_fwd(kn_raw, small["mla_k_nope_norm"], name="mla_kn_prep")
    k_r = _norm_fwd(kr_raw[None], small["mla_k_rope_norm"], name="mla_kr_prep", rope=rope_kr)
    k_a = jnp.concatenate([k_n, jnp.broadcast_to(k_r, (MLA_HEADS, T, MLA_ROPE))], axis=-1)
    if late is None:
        o_a, lse_a, ob_a = _flash_fwd(q_a, k_a, va, name="mla_attn")
    else:
        o_a, lse_a, ob_a, gathered = _flash_fwd(q_a, k_a, va, name="mla_attn", gather=late.halves)
        W.update(late.weights(gathered))
    q_g = _norm_fwd(qg_raw, small["gqa_q_norm"], name="gqa_q_prep", rope=rope_ax, out_scale=sc_g * LOG2E)
    k_g = _norm_fwd(kg_raw, small["gqa_k_norm"], name="gqa_k_prep", rope=rope_ax)
    if late is None:
        o_g, lse_g, ob_g = _flash_fwd(q_g, k_g, vg, name="gqa_attn")
    else:
        o_g, lse_g, ob_g, gathered = _flash_fwd(q_g, k_g, vg, name="gqa_attn", gather=late.halves2)
        W.update(late.weights2(gathered))
    merged = jnp.concatenate([_unheads(ob_a), _unheads(ob_g)], axis=-1)
    x1 = _mm(merged, W["even_w_out"], mode="nn", name="even_out", epi="add", extra=x)
    x2, mlp0 = _mlp_fwd(x1, small["mlp_norm"][0], W["mlp_w_up0"], W["mlp_w_down0"], 0)

    h1 = _norm_fwd(x2[None], small["odd_norm"], name="odd_norm")[0]
    qkv = _mm(h1, W["odd_w_qkv"], mode="nn", name="odd_qkv")
    nq, nkk = SWA_HEADS * SWA_DIM, SWA_KV * SWA_DIM
    qs_raw = _heads(qkv[:, :nq], SWA_HEADS, SWA_DIM)
    ks_raw = _heads(qkv[:, nq:nq + nkk], SWA_KV, SWA_DIM)
    vs = _heads(qkv[:, nq + nkk:], SWA_KV, SWA_DIM).astype(BF16)
    q_s = _norm_fwd(qs_raw, small["swa_q_norm"], name="swa_q_prep", rope=rope_sw, out_scale=sc_s)
    k_s = _norm_fwd(ks_raw, small["swa_k_norm"], name="swa_k_prep", rope=rope_sw)
    sink = small["swa_sink"].reshape(SWA_HEADS, 1, 1)
    o_s, lse_s, ob_s = _swa_fwd(q_s, k_s, vs, sink, name="swa_attn")
    o_flat = _unheads(ob_s)
    x3 = _mm(o_flat, W["odd_w_out"], mode="nn", name="odd_out", epi="add", extra=x2)
    x4, mlp1 = _mlp_fwd(x3, small["mlp_norm"][1], W["mlp_w_up1"], W["mlp_w_down1"], 1)

    dy, loss_sum = _loss_head(x4, target, name="loss_head")
    gW, gs = {}, {}

    dx3, dg_m1, gW["mlp_w_up1"], gW["mlp_w_down1"] = _mlp_bwd(
        x3, small["mlp_norm"][1], W["mlp_w_up1"], W["mlp_w_down1"], mlp1, dy, 1)
    d_oflat = _mm(dx3, W["odd_w_out"], mode="nt", name="odd_dout")
    gW["odd_w_out"] = _dw(o_flat, dx3, name="odd_dwout", axis=1)
    do_s = _heads(d_oflat, SWA_HEADS, SWA_DIM)
    dq_s, dsink, delta_s, dob_s = _swa_dq(q_s, k_s, vs, o_s, do_s, lse_s, sink, name="swa_dq")
    dk_s, dv_s = _swa_dkv(q_s, k_s, vs, dob_s, lse_s, delta_s, name="swa_dkv")
    gs["swa_sink"] = dsink[:, 0, 0]
    dqs_raw, gs["swa_q_norm"] = _norm_bwd(qs_raw, small["swa_q_norm"], dq_s, name="swa_dq_prep", rope=rope_sw,
                                          out_dtype=BF16, dy_scale=sc_s)
    dks_raw, gs["swa_k_norm"] = _norm_bwd(ks_raw, small["swa_k_norm"], dk_s, name="swa_dk_prep", rope=rope_sw,
                                          out_dtype=BF16)
    dqkv = jnp.concatenate([_unheads(dqs_raw), _unheads(dks_raw), _unheads(dv_s)], axis=-1).astype(BF16)
    dh1 = _mm(dqkv, W["odd_w_qkv"], mode="nt", name="odd_dh")
    gW["odd_w_qkv"] = _dw(h1, dqkv, name="odd_dwqkv", axis=2)
    dx2, gs["odd_norm"] = _norm_bwd(x2[None], small["odd_norm"], dh1[None], name="odd_dnorm", res=dx3)
    dx2 = dx2[0]

    dx1, dg_m0, gW["mlp_w_up0"], gW["mlp_w_down0"] = _mlp_bwd(
        x1, small["mlp_norm"][0], W["mlp_w_up0"], W["mlp_w_down0"], mlp0, dx2, 0)
    gs["mlp_norm"] = jnp.stack([dg_m0, dg_m1])
    d_merged = _mm(dx1, W["even_w_out"], mode="nt", name="even_dout")
    gW["even_w_out"] = _dw(merged, dx1, name="even_dwout", axis=1)
    na = MLA_HEADS * MLA_V
    do_a = _heads(d_merged[:, :na], MLA_HEADS, MLA_V)
    do_g = _heads(d_merged[:, na:], GQA_HEADS, GQA_DIM)
    delta_a, dob_a = _delta(o_a, do_a, name="mla_delta")
    delta_g, dob_g = _delta(o_g, do_g, name="gqa_delta")
    if late is None:
        dq_a, dk_a, dv_a = _flash_bwd(q_a, k_a, va, dob_a, lse_a, delta_a, name="mla_attn_bwd")
        dq_g, dk_gp, dv_gp = _flash_bwd(q_g, k_g, vg, dob_g, lse_g, delta_g, name="gqa_attn_bwd")
    else:
        halves = late.split(gW)
        dq_a, dk_a, dv_a, from_sibling = _flash_bwd(q_a, k_a, va, dob_a, lse_a, delta_a, name="mla_attn_bwd",
                                                    rider=("exchange", halves))
        dq_g, dk_gp, dv_gp, late.scattered = _flash_bwd(q_g, k_g, vg, dob_g, lse_g, delta_g, name="gqa_attn_bwd",
                                                        rider=("scatter", late.pairs(halves, from_sibling)))
    grp = GQA_HEADS // GQA_KV
    ln2 = 1.0 / LOG2E
    dqg_raw, gs["gqa_q_norm"] = _norm_bwd(qg_raw, small["gqa_q_norm"], dq_g, name="gqa_dq_prep", rope=rope_ax,
                                          dy_scale=sc_g, out_dtype=BF16)
    dkg_raw, gs["gqa_k_norm"] = _norm_bwd(kg_raw, small["gqa_k_norm"], dk_gp, name="gqa_dk_prep", rope=rope_ax,
                                          group=grp, dy_scale=ln2, out_dtype=BF16)
    dvg = _group_sum(dv_gp, grp, name="gqa_dv_sum", out_dtype=BF16)
    dqa_raw, gs["mla_q_norm"] = _norm_bwd(qa_raw, small["mla_q_norm"], dq_a, name="mla_dq_prep", rope=rope_q,
                                          dy_scale=sc_a, out_dtype=BF16)
    dkn_raw, gs["mla_k_nope_norm"] = _norm_bwd(kn_raw, small["mla_k_nope_norm"], dk_a[:, :, :MLA_NOPE],
                                               name="mla_dkn_prep", dy_scale=ln2, out_dtype=BF16)
    dkr_raw, gs["mla_k_rope_norm"] = _norm_bwd(kr_raw[None], small["mla_k_rope_norm"], dk_a[:, :, MLA_NOPE:],
                                               name="mla_dkr_prep", rope=rope_kr, group=MLA_HEADS, dy_scale=ln2,
                                               out_dtype=BF16)
    dkv = jnp.concatenate([dkn_raw.transpose(1, 0, 2), dv_a.astype(BF16).transpose(1, 0, 2)], axis=-1)
    dkv = dkv.reshape(T, MLA_HEADS * kv_w).astype(BF16)
    dqa = _unheads(dqa_raw).astype(BF16)
    dckvn = _mm(dkv, W["mla_w_ukv"], mode="nt", name="mla_dckv")
    gW["mla_w_ukv"] = _dw(ckvn, dkv, name="mla_dwukv", axis=2)
    dcqn = _mm(dqa, W["mla_w_uq"], mode="nt", name="mla_dcq")
    gW["mla_w_uq"] = _dw(cqn, dqa, name="mla_dwuq", axis=2)
    dc_q, gs["mla_q_lat_norm"] = _norm_bwd(c_q[None], small["mla_q_lat_norm"], dcqn[None], name="q_lat_dnorm",
                                           out_dtype=BF16)
    dc_kv, gs["mla_kv_lat_norm"] = _norm_bwd(c_kv[None], small["mla_kv_lat_norm"], dckvn[None], name="kv_lat_dnorm",
                                             out_dtype=BF16)
    dproj = jnp.concatenate([dc_q[0], dc_kv[0], dkr_raw[0], _unheads(dqg_raw), _unheads(dkg_raw), _unheads(dvg)],
                            axis=-1).astype(BF16)
    dh0 = _mm(dproj, W["even_w_in"], mode="nt", name="even_dh")
    gW["even_w_in"] = _dw(h0, dproj, name="even_dwin", axis=2)
    dx0, gs["even_norm"] = _norm_bwd(x[None], small["even_norm"], dh0[None], name="even_dnorm", res=dx1)
    gs = {k: v.reshape(-1) for k, v in gs.items()}
    return loss_sum, dx0[0], gW, gs


BIG = (("even_w_in", 0, 2), ("mla_w_uq", 0, 2), ("mla_w_ukv", 0, 2), ("even_w_out", 0, 1), ("odd_w_qkv", 0, 2),
       ("odd_w_out", 0, 1), ("mlp_w_up", 0, 2), ("mlp_w_up", 1, 2), ("mlp_w_down", 0, 1), ("mlp_w_down", 1, 1))
GATHER_FIRST = ("even_w_in", "mla_w_uq", "mla_w_ukv")
GRADS_LAST = ("even_w_in", "mla_w_uq", "mla_w_ukv")
SMALL = ("even_norm", "mla_q_lat_norm", "mla_kv_lat_norm", "mla_q_norm", "mla_k_nope_norm", "mla_k_rope_norm",
         "gqa_q_norm", "gqa_k_norm", "odd_norm", "swa_q_norm", "swa_k_norm", "swa_sink", "mlp_norm")
def _pad_to(v, n):
    return v if v.shape[-1] == n else jnp.pad(v, [(0, 0)] * (v.ndim - 1) + [(0, n - v.shape[-1])])


def _big_key(name, layer, w):
    return name if w[name].shape[0] == 1 else f"{name}{layer}"


def _pack_rows(flat, rows=8):
    n = flat.shape[0]
    padded = -(-n // (rows * LANES)) * rows * LANES
    return _pad_to(flat, padded).reshape(-1, LANES)


def kernel(x, even_norm, even_w_in, mla_q_lat_norm, mla_kv_lat_norm, mla_w_uq, mla_w_ukv, mla_q_norm, mla_k_nope_norm, mla_k_rope_norm, gqa_q_norm, gqa_k_norm, even_w_out, odd_norm, odd_w_qkv, swa_q_norm, swa_k_norm, swa_sink, odd_w_out, mlp_norm, mlp_w_up, mlp_w_down, loss_target, m_even_norm, m_even_w_in, m_mla_q_lat_norm, m_mla_kv_lat_norm, m_mla_w_uq, m_mla_w_ukv, m_mla_q_norm, m_mla_k_nope_norm, m_mla_k_rope_norm, m_gqa_q_norm, m_gqa_k_norm, m_even_w_out, m_odd_norm, m_odd_w_qkv, m_swa_q_norm, m_swa_k_norm, m_swa_sink, m_odd_w_out, m_mlp_norm, m_mlp_w_up, m_mlp_w_down, v_even_norm, v_even_w_in, v_mla_q_lat_norm, v_mla_kv_lat_norm, v_mla_w_uq, v_mla_w_ukv, v_mla_q_norm, v_mla_k_nope_norm, v_mla_k_rope_norm, v_gqa_q_norm, v_gqa_k_norm, v_even_w_out, v_odd_norm, v_odd_w_qkv, v_swa_q_norm, v_swa_k_norm, v_swa_sink, v_odd_w_out, v_mlp_norm, v_mlp_w_up, v_mlp_w_down):
    w = dict(even_norm=even_norm, even_w_in=even_w_in, mla_q_lat_norm=mla_q_lat_norm, mla_kv_lat_norm=mla_kv_lat_norm,
             mla_w_uq=mla_w_uq, mla_w_ukv=mla_w_ukv, mla_q_norm=mla_q_norm, mla_k_nope_norm=mla_k_nope_norm,
             mla_k_rope_norm=mla_k_rope_norm, gqa_q_norm=gqa_q_norm, gqa_k_norm=gqa_k_norm, even_w_out=even_w_out,
             odd_norm=odd_norm, odd_w_qkv=odd_w_qkv, swa_q_norm=swa_q_norm, swa_k_norm=swa_k_norm, swa_sink=swa_sink,
             odd_w_out=odd_w_out, mlp_norm=mlp_norm, mlp_w_up=mlp_w_up, mlp_w_down=mlp_w_down)
    m = dict(even_norm=m_even_norm, even_w_in=m_even_w_in, mla_q_lat_norm=m_mla_q_lat_norm,
             mla_kv_lat_norm=m_mla_kv_lat_norm, mla_w_uq=m_mla_w_uq, mla_w_ukv=m_mla_w_ukv, mla_q_norm=m_mla_q_norm,
             mla_k_nope_norm=m_mla_k_nope_norm, mla_k_rope_norm=m_mla_k_rope_norm, gqa_q_norm=m_gqa_q_norm,
             gqa_k_norm=m_gqa_k_norm, even_w_out=m_even_w_out, odd_norm=m_odd_norm, odd_w_qkv=m_odd_w_qkv,
             swa_q_norm=m_swa_q_norm, swa_k_norm=m_swa_k_norm, swa_sink=m_swa_sink, odd_w_out=m_odd_w_out,
             mlp_norm=m_mlp_norm, mlp_w_up=m_mlp_w_up, mlp_w_down=m_mlp_w_down)
    v = dict(even_norm=v_even_norm, even_w_in=v_even_w_in, mla_q_lat_norm=v_mla_q_lat_norm,
             mla_kv_lat_norm=v_mla_kv_lat_norm, mla_w_uq=v_mla_w_uq, mla_w_ukv=v_mla_w_ukv, mla_q_norm=v_mla_q_norm,
             mla_k_nope_norm=v_mla_k_nope_norm, mla_k_rope_norm=v_mla_k_rope_norm, gqa_q_norm=v_gqa_q_norm,
             gqa_k_norm=v_gqa_k_norm, even_w_out=v_even_w_out, odd_norm=v_odd_norm, odd_w_qkv=v_odd_w_qkv,
             swa_q_norm=v_swa_q_norm, swa_k_norm=v_swa_k_norm, swa_sink=v_swa_sink, odd_w_out=v_odd_w_out,
             mlp_norm=v_mlp_norm, mlp_w_up=v_mlp_w_up, mlp_w_down=v_mlp_w_down)
    xi, yi, ci = _position()
    chip = 2 * xi + yi
    T, D = x.shape[1], x.shape[2]

    c_idx = ci.reshape(1).astype(jnp.int32)
    key_of = lambda entry: _big_key(entry[0], entry[1], w)
    first_use = [e for e in BIG if e[0] in GATHER_FIRST]
    later_use = [e for e in BIG if e[0] not in GATHER_FIRST]
    early_grads = [e for e in BIG if e[0] not in GRADS_LAST]
    last_grads = [e for e in BIG if e[0] in GRADS_LAST]

    as_bf16 = {}

    def halves_of(entries):
        out = []
        for name, layer, _ in entries:
            layers, ks, ns = w[name].shape
            if name not in as_bf16:
                as_bf16[name] = _to_bf16(w[name].reshape(layers * ks, ns), name=f"to_bf16_{name}")
            out.append(as_bf16[name].reshape(layers, 2, ks // 2, ns)[layer])
        return out

    def weights_of(entries, gathered):
        out = {}
        for (name, layer, axis), g in zip(entries, gathered):
            ks, ns = w[name].shape[1:]
            stacked = g.reshape(N_CHIPS, ks, ns)
            if axis == 1:
                out[_big_key(name, layer, w)] = stacked.reshape(N_CHIPS * ks, ns)
            else:
                out[_big_key(name, layer, w)] = stacked.transpose(1, 0, 2).reshape(ks, N_CHIPS * ns)
        return out

    def split_halves(entries, gW):
        out = []
        for entry in entries:
            _, ks, ns = gW[key_of(entry)].shape
            out.append(gW[key_of(entry)].reshape(N_CHIPS, 2, ks // 2, ns))
        return out

    def pair_sums(entries, g_all, from_sibling):
        return [_pair_add(g, r, c_idx, name=f"grad_pair_add_{key_of(e)}") for e, g, r in zip(entries, g_all, from_sibling)]

    layer0_use = [e for e in later_use if e[0] in ("even_w_out",) or (e[0].startswith("mlp_") and e[1] == 0)]
    layer1_use = [e for e in later_use if e not in layer0_use]

    class _Late:
        halves = halves_of(layer0_use)
        halves2 = halves_of(layer1_use)
        scattered = None

        @staticmethod
        def weights(gathered):
            return weights_of(layer0_use, gathered)

        @staticmethod
        def weights2(gathered):
            return weights_of(layer1_use, gathered)

        @staticmethod
        def split(gW):
            return split_halves(early_grads, gW)

        @staticmethod
        def pairs(g_all, from_sibling):
            return pair_sums(early_grads, g_all, from_sibling)

    late = _Late()
    W = weights_of(first_use, _all_gather_halves(halves_of(first_use), name="weights_all_gather"))

    odd_full = jnp.zeros((N_CHIPS, D // N_CHIPS), F32).at[chip].set(jnp.where(ci == 0, 1.0, 0.0) * w["odd_norm"][0])
    odd_full = _all_reduce_small(_pack_rows(odd_full.reshape(-1)), name="odd_norm_gather").reshape(-1)[:D]
    small = {name: w[name][0] for name in SMALL if name not in ("mlp_norm", "odd_norm")}
    small["mlp_norm"] = w["mlp_norm"]
    small["odd_norm"] = odd_full

    loss_sum, grad_x, gW, gs = _local_step(x[0], loss_target[0], W, small, late)

    loss_local = 0.5 * loss_sum.reshape(1) / D
    small_sizes = [(name, int(gs[name].shape[0])) for name in SMALL]
    ar_in = jnp.concatenate([_pad_to(loss_local, LANES)] + [gs[name] for name in SMALL])
    ar_out = _all_reduce_small(_pack_rows(ar_in), name="small_all_reduce").reshape(-1)
    loss = ar_out[0]
    g_small, off = {}, LANES
    for name, n in small_sizes:
        g_small[name] = ar_out[off:off + n]
        off += n
    shard_d = D // N_CHIPS
    g_small["odd_norm"] = lax.dynamic_slice(g_small["odd_norm"], (chip * shard_d,), (shard_d,))

    from_chips = dict(zip(map(key_of, early_grads), late.scattered))
    last_halves = split_halves(last_grads, gW)
    last_pairs = pair_sums(last_grads, last_halves, _sibling_exchange(last_halves, name="grad_sibling_exchange"))
    last_scattered = _chip_scatter(last_pairs, name="grad_chip_scatter")
    from_chips.update(zip(map(key_of, last_grads), last_scattered))
    keys = [key_of(e) for e in BIG]
    reduced = [_sum_chips(from_chips[key], c_idx, name=f"grad_chip_sum_{key}") for key in keys]
    shared = _sibling_share(reduced, name="grad_sibling_share")
    g_shards = {}
    for (name, layer, _), f in zip(BIG, shared):
        g_shards.setdefault(name, []).append(f.reshape(w[name].shape[1:]))

    grads, deltas, new_m, new_v = {}, {}, {}, {}
    for name in g_shards:
        shape = w[name].shape
        g = jnp.stack(g_shards[name])
        grads[name] = g
        two_d = (shape[0] * shape[1], shape[2])
        d_, m_, v_ = _adamw(w[name].reshape(two_d), g.reshape(two_d), m[name].reshape(two_d), v[name].reshape(two_d),
                            name=f"adamw_{name}")
        deltas[name], new_m[name], new_v[name] = d_.reshape(shape), m_.reshape(shape), v_.reshape(shape)
    pack_small = lambda d: _pack_rows(jnp.concatenate([d[name].reshape(-1) for name in SMALL]))
    for name in SMALL:
        grads[name] = g_small[name].reshape(w[name].shape)
    d_, m_, v_ = _adamw(pack_small(w), pack_small(grads), pack_small(m), pack_small(v), name="adamw_small")
    d_, m_, v_ = d_.reshape(-1), m_.reshape(-1), v_.reshape(-1)
    off = 0
    for name in SMALL:
        n = int(np.prod(w[name].shape))
        deltas[name] = d_[off:off + n].reshape(w[name].shape)
        new_m[name] = m_[off:off + n].reshape(w[name].shape)
        new_v[name] = v_[off:off + n].reshape(w[name].shape)
        off += n

    order = ("even_norm", "even_w_in", "mla_q_lat_norm", "mla_kv_lat_norm", "mla_w_uq", "mla_w_ukv", "mla_q_norm",
             "mla_k_nope_norm", "mla_k_rope_norm", "gqa_q_norm", "gqa_k_norm", "even_w_out", "odd_norm", "odd_w_qkv",
             "swa_q_norm", "swa_k_norm", "swa_sink", "odd_w_out", "mlp_norm", "mlp_w_up", "mlp_w_down")
    outs = [loss, grad_x[None]]
    for group in (grads, deltas, new_m, new_v):
        outs += [group[name] for name in order]
    return tuple(outs)
```
